```python
import math
import jax, jax.numpy as jnp
from jax import lax
import numpy as np

D_MODEL = 1024
BATCH = 4
SEQ = 4096
DEPTH = 1

D_MIX = D_MODEL
D_ATTN = D_MIX // 2
D_SSM = D_MIX - D_ATTN
HEAD_DIM = 64
N_Q_HEADS = D_ATTN // HEAD_DIM
N_KV_HEADS = 2
GQA_GROUP = N_Q_HEADS // N_KV_HEADS
WINDOW = 128
BLOCK = 128
HALO_BLOCKS = -(-WINDOW // BLOCK)
ROPE_THETA = 10000.0
SSM_CH = 16
N_SSM_GROUPS = D_SSM // SSM_CH
SSM_STATE = 64
N_DIR = 2
DT_MIN = 1e-3
DT_MAX = 1e-1
NORM_EPS = 1e-5
NEG_INF = -1e30
DEEPNORM_ALPHA = (2.0 * DEPTH) ** 0.25
DEEPNORM_BETA = (8.0 * DEPTH) ** -0.25
Q_COLS = N_Q_HEADS * HEAD_DIM
KV_COLS = N_KV_HEADS * HEAD_DIM
D_IN_PROJ = Q_COLS + 2 * KV_COLS + D_ATTN + D_SSM + D_SSM
SPLITS = [Q_COLS, Q_COLS + KV_COLS, Q_COLS + 2 * KV_COLS,
          Q_COLS + 2 * KV_COLS + D_ATTN, Q_COLS + 2 * KV_COLS + D_ATTN + D_SSM]

kernel_name = "hymba_deepnorm_swa_s5_encoder"


def layer_norm(x, g, b):
    xf = x.astype(jnp.float32)
    mu = jnp.mean(xf, axis=-1, keepdims=True)
    var = jnp.mean(jnp.square(xf - mu), axis=-1, keepdims=True)
    y = (xf - mu) * lax.rsqrt(var + NORM_EPS)
    return (y * g.astype(jnp.float32) + b.astype(jnp.float32)).astype(x.dtype)


def rms_norm(x, g):
    xf = x.astype(jnp.float32)
    y = xf * lax.rsqrt(jnp.mean(jnp.square(xf), axis=-1, keepdims=True) + NORM_EPS)
    return (y * g.astype(jnp.float32)).astype(x.dtype)


def rope(x, pos):
    half = HEAD_DIM // 2
    inv_freq = ROPE_THETA ** (-jnp.arange(half, dtype=jnp.float32) / half)
    ang = pos.astype(jnp.float32)[:, None] * inv_freq[None, :]
    cos = jnp.cos(ang)[None, :, None, :].astype(x.dtype)
    sin = jnp.sin(ang)[None, :, None, :].astype(x.dtype)
    x1, x2 = x[..., :half], x[..., half:]
    return jnp.concatenate([x1 * cos - x2 * sin, x2 * cos + x1 * sin], axis=-1)


def windowed_gqa_with_sink(q, k, v, sink):
    b, s = q.shape[0], q.shape[1]
    nb = s // BLOCK
    pad = HALO_BLOCKS * BLOCK
    span = 2 * HALO_BLOCKS + 1
    n_keys = span * BLOCK
    kp = jnp.pad(k, ((0, 0), (pad, pad), (0, 0), (0, 0))).reshape(b, nb + 2 * HALO_BLOCKS, BLOCK, N_KV_HEADS, HEAD_DIM)
    vp = jnp.pad(v, ((0, 0), (pad, pad), (0, 0), (0, 0))).reshape(b, nb + 2 * HALO_BLOCKS, BLOCK, N_KV_HEADS, HEAD_DIM)
    kb = jnp.concatenate([kp[:, j:j + nb] for j in range(span)], axis=2)
    vb = jnp.concatenate([vp[:, j:j + nb] for j in range(span)], axis=2)
    qb = q.reshape(b, nb, BLOCK, N_KV_HEADS, GQA_GROUP, HEAD_DIM)
    scale = HEAD_DIM ** -0.5
    scores = jnp.einsum('bnqgrd,bnkgd->bngrqk', qb, kb).astype(jnp.float32) * scale
    blk = jnp.arange(nb, dtype=jnp.int32)[:, None] * BLOCK
    qpos = blk + jnp.arange(BLOCK, dtype=jnp.int32)[None, :]
    kpos = blk - pad + jnp.arange(n_keys, dtype=jnp.int32)[None, :]
    valid = ((jnp.abs(qpos[:, :, None] - kpos[:, None, :]) <= WINDOW)
             & (kpos[:, None, :] >= 0) & (kpos[:, None, :] < s))
    scores = jnp.where(valid[None, :, None, None], scores, NEG_INF)
    sink_l = sink.astype(jnp.float32).reshape(N_KV_HEADS, GQA_GROUP)[None, None, :, :, None, None]
    m = jnp.maximum(jnp.max(scores, axis=-1, keepdims=True), sink_l)
    p = jnp.exp(scores - m)
    denom = jnp.sum(p, axis=-1, keepdims=True) + jnp.exp(sink_l - m)
    probs = (p / denom).astype(v.dtype)
    out = jnp.einsum('bngrqk,bnkgd->bnqgrd', probs, vb)
    return out.reshape(b, s, N_Q_HEADS * HEAD_DIM)


def _complex_linear_combine(e1, e2):
    a1r, a1i, b1r, b1i = e1
    a2r, a2i, b2r, b2i = e2
    return (a2r * a1r - a2i * a1i,
            a2r * a1i + a2i * a1r,
            a2r * b1r - a2i * b1i + b2r,
            a2r * b1i + a2i * b1r + b2i)


def s5_bidirectional(u, a_re, a_im, log_dt, b_re, b_im, c_re, c_im, d, w_glu, b_glu):
    bsz, s = u.shape[0], u.shape[1]
    uf = u.astype(jnp.float32)
    ug = uf.reshape(bsz, s, N_SSM_GROUPS, SSM_CH)
    y = d.astype(jnp.float32) * uf
    for direction in range(N_DIR):
        dt = jnp.exp(log_dt[direction].astype(jnp.float32))[:, None]
        ar = a_re[direction].astype(jnp.float32)
        ai = a_im[direction].astype(jnp.float32)
        mag = jnp.exp(dt * ar)
        lr = mag * jnp.cos(dt * ai)
        li = mag * jnp.sin(dt * ai)
        den = ar * ar + ai * ai
        fr = ((lr - 1.0) * ar + li * ai) / den
        fi = (li * ar - (lr - 1.0) * ai) / den
        br = b_re[direction].astype(jnp.float32)
        bi = b_im[direction].astype(jnp.float32)
        bbar_r = fr[:, :, None] * br - fi[:, :, None] * bi
        bbar_i = fr[:, :, None] * bi + fi[:, :, None] * br
        bu_r = jnp.einsum('bsgc,gpc->bsgp', ug, bbar_r)
        bu_i = jnp.einsum('bsgc,gpc->bsgp', ug, bbar_i)
        a_r = jnp.broadcast_to(lr[None, None], bu_r.shape)
        a_i = jnp.broadcast_to(li[None, None], bu_r.shape)
        _, _, xr, xi = lax.associative_scan(_complex_linear_combine, (a_r, a_i, bu_r, bu_i),
                                            reverse=(direction == 1), axis=1)
        cr = c_re[direction].astype(jnp.float32)
        ci = c_im[direction].astype(jnp.float32)
        yc = jnp.einsum('bsgp,gcp->bsgc', xr, cr) - jnp.einsum('bsgp,gcp->bsgc', xi, ci)
        y = y + yc.reshape(bsz, s, D_SSM)
    g = jax.nn.gelu(y)
    y = g * jax.nn.sigmoid(g @ w_glu.astype(jnp.float32) + b_glu.astype(jnp.float32))
    return y.astype(u.dtype)


def setup_inputs(seed: int = 0) -> dict:
    key = jax.random.key(seed)
    ks = jax.random.split(key, 20)
    n = jnp.arange(SSM_STATE, dtype=jnp.float32)
    shp_a = (DEPTH, N_DIR, N_SSM_GROUPS, SSM_STATE)
    shp_b = (DEPTH, N_DIR, N_SSM_GROUPS, SSM_STATE, SSM_CH)
    shp_c = (DEPTH, N_DIR, N_SSM_GROUPS, SSM_CH, SSM_STATE)
    return {
        "x": jax.random.normal(ks[0], (BATCH, SEQ, D_MODEL), jnp.float32),
        "w_in": jax.random.normal(ks[1], (DEPTH, D_MODEL, D_IN_PROJ), jnp.float32) * D_MODEL ** -0.5,
        "attn_sink": jax.random.normal(ks[2], (DEPTH, N_Q_HEADS), jnp.float32) * 0.5,
        "ssm_a_re": -0.5 + 0.01 * jax.random.normal(ks[3], shp_a, jnp.float32),
        "ssm_a_im": math.pi * n + 0.01 * jax.random.normal(ks[4], shp_a, jnp.float32),
        "ssm_log_dt": jax.random.uniform(ks[5], (DEPTH, N_DIR, N_SSM_GROUPS), jnp.float32,
                                         math.log(DT_MIN), math.log(DT_MAX)),
        "ssm_b_re": jax.random.normal(ks[6], shp_b, jnp.float32) * (2.0 * SSM_CH) ** -0.5,
        "ssm_b_im": jax.random.normal(ks[7], shp_b, jnp.float32) * (2.0 * SSM_CH) ** -0.5,
        "ssm_c_re": jax.random.normal(ks[8], shp_c, jnp.float32) * (2.0 * SSM_STATE) ** -0.5,
        "ssm_c_im": jax.random.normal(ks[9], shp_c, jnp.float32) * (2.0 * SSM_STATE) ** -0.5,
        "ssm_d": jax.random.normal(ks[10], (DEPTH, D_SSM), jnp.float32),
        "w_glu": jax.random.normal(ks[11], (DEPTH, D_SSM, D_SSM), jnp.float32) * D_SSM ** -0.5,
        "b_glu": 0.01 * jax.random.normal(ks[12], (DEPTH, D_SSM), jnp.float32),
        "norm_attn_g": 1.0 + 0.01 * jax.random.normal(ks[13], (DEPTH, D_ATTN), jnp.float32),
        "norm_ssm_g": 1.0 + 0.01 * jax.random.normal(ks[14], (DEPTH, D_SSM), jnp.float32),
        "w_out": jax.random.normal(ks[15], (DEPTH, D_MIX, D_MODEL), jnp.float32) * (D_MIX ** -0.5) * DEEPNORM_BETA,
        "ln_g": 1.0 + 0.01 * jax.random.normal(ks[16], (DEPTH, D_MODEL), jnp.float32),
        "ln_b": 0.01 * jax.random.normal(ks[17], (DEPTH, D_MODEL), jnp.float32),
    }


def reference(x, w_in, attn_sink, ssm_a_re, ssm_a_im, ssm_log_dt, ssm_b_re, ssm_b_im, ssm_c_re, ssm_c_im,
              ssm_d, w_glu, b_glu, norm_attn_g, norm_ssm_g, w_out, ln_g, ln_b):
    bsz, s = x.shape[0], x.shape[1]
    pos = jnp.arange(s, dtype=jnp.int32)
    h = x
    for l in range(DEPTH):
        proj = h @ w_in[l]
        q, k, v, z_attn, u, z_ssm = jnp.split(proj, SPLITS, axis=-1)
        q = rope(q.reshape(bsz, s, N_Q_HEADS, HEAD_DIM), pos)
        k = rope(k.reshape(bsz, s, N_KV_HEADS, HEAD_DIM), pos)
        v = v.reshape(bsz, s, N_KV_HEADS, HEAD_DIM)
        y_attn = windowed_gqa_with_sink(q, k, v, attn_sink[l]) * jax.nn.silu(z_attn)
        y_attn = rms_norm(y_attn, norm_attn_g[l])
        y_ssm = s5_bidirectional(u, ssm_a_re[l], ssm_a_im[l], ssm_log_dt[l], ssm_b_re[l], ssm_b_im[l],
                                 ssm_c_re[l], ssm_c_im[l], ssm_d[l], w_glu[l], b_glu[l]) * jax.nn.silu(z_ssm)
        y_ssm = rms_norm(y_ssm, norm_ssm_g[l])
        mixed = jnp.concatenate([y_attn, y_ssm], axis=-1)
        h = layer_norm(DEEPNORM_ALPHA * h + mixed @ w_out[l], ln_g[l], ln_b[l])
    return h
```

```python
import jax
import jax.numpy as jnp
from jax import lax
from jax.experimental import pallas as pl
from jax.experimental.pallas import tpu as pltpu

D_MODEL = 1024
BATCH = 4
SEQ = 4096
D_ATTN = 512
D_SSM = 512
HEAD_DIM = 64
N_Q_HEADS = 8
N_KV_HEADS = 2
WINDOW = 128
BLOCK = 128
ROPE_THETA = 10000.0
SSM_CH = 16
N_GROUPS = 32
SSM_STATE = 64
NORM_EPS = 1e-5
NEG_INF = -1e30
DEEPNORM_ALPHA = 2.0 ** 0.25

Q_COLS = N_Q_HEADS * HEAD_DIM
KV_COLS = N_KV_HEADS * HEAD_DIM
CHUNK = 16
N_CHUNKS = SEQ // CHUNK
CHUNK_VEC = CHUNK * SSM_CH
STATE_VEC = 2 * SSM_STATE
LANES = 128
ROWS = BATCH * N_CHUNKS
SLAB_PITCH = N_CHUNKS + 8
SLAB_ROWS = BATCH * SLAB_PITCH + 8
TOK_BLOCK = 1024
REGROUP_TOK = LANES * CHUNK
STEPS_PER_REGROUP = REGROUP_TOK // TOK_BLOCK
SUB_ROWS = 512
N_LANE_BLOCKS = D_SSM // LANES
GROUPS_PER_LANE_BLOCK = LANES // SSM_CH
VMEM_LIMIT = 56 * 1024 * 1024
HIGHEST = lax.Precision.HIGHEST


def _ssm_tables(a_re, a_im, log_dt, b_re, b_im, c_re, c_im, d):
    f32 = jnp.float32
    dt = jnp.exp(log_dt.astype(f32))[..., None]
    ar = a_re.astype(f32)
    ai = a_im.astype(f32)
    zr = dt * ar
    zi = dt * ai
    mag = jnp.exp(zr)
    lr = mag * jnp.cos(zi)
    li = mag * jnp.sin(zi)
    den = ar * ar + ai * ai
    fr = ((lr - 1.0) * ar + li * ai) / den
    fi = (li * ar - (lr - 1.0) * ai) / den
    br = b_re.astype(f32)
    bi = b_im.astype(f32)
    bbr = fr[..., None] * br - fi[..., None] * bi
    bbi = fr[..., None] * bi + fi[..., None] * br
    cr = c_re.astype(f32)
    ci = c_im.astype(f32)

    tau = jnp.arange(CHUNK + 1, dtype=f32)[:, None, None, None]
    pmag = jnp.exp(tau * zr[None])
    pr = pmag * jnp.cos(tau * zi[None])
    pi = pmag * jnp.sin(tau * zi[None])

    er = pr[..., None] * bbr[None] - pi[..., None] * bbi[None]
    ei = pr[..., None] * bbi[None] + pi[..., None] * bbr[None]
    gr = cr[None] * pr[:, :, :, None, :] - ci[None] * pi[:, :, :, None, :]
    gi = cr[None] * pi[:, :, :, None, :] + ci[None] * pr[:, :, :, None, :]

    kk = (jnp.einsum('dgcp,tdgpe->tdgce', cr, er, precision=HIGHEST)
          - jnp.einsum('dgcp,tdgpe->tdgce', ci, ei, precision=HIGHEST))
    kf = kk[:CHUNK, 0]
    kb = kk[:CHUNK, 1]
    dmat = d.astype(f32).reshape(N_GROUPS, SSM_CH)[:, :, None] * jnp.eye(SSM_CH, dtype=f32)[None]
    center = kf[0] + kb[0] + dmat
    kall = jnp.concatenate([kb[:0:-1], center[None], kf[1:]], axis=0)
    jj = jnp.arange(CHUNK)
    lag_idx = jj[None, :] - jj[:, None] + (CHUNK - 1)
    m5 = kall[lag_idx]
    m_intra = jnp.transpose(m5, (2, 0, 4, 1, 3)).reshape(N_GROUPS, CHUNK_VEC, CHUNK_VEC)

    ef_r = er[:CHUNK, 0][::-1]
    ef_i = ei[:CHUNK, 0][::-1]
    eb_r = er[:CHUNK, 1]
    eb_i = ei[:CHUNK, 1]

    def _rows(e):
        return jnp.transpose(e, (1, 0, 3, 2)).reshape(N_GROUPS, CHUNK_VEC, SSM_STATE)

    w_state = jnp.concatenate([_rows(ef_r), _rows(ef_i), _rows(ef_i), _rows(ef_r),
                               _rows(eb_r), _rows(eb_i), _rows(eb_i), _rows(eb_r)], axis=2)

    def _cols(g):
        return jnp.transpose(g, (1, 3, 0, 2)).reshape(N_GROUPS, SSM_STATE, CHUNK_VEC)

    gf_r = gr[1:, 0]
    gf_i = gi[1:, 0]
    gb_r = gr[1:, 1][::-1]
    gb_i = gi[1:, 1][::-1]
    w_out = jnp.concatenate([_cols(gf_r), -_cols(gf_i), _cols(gb_r), -_cols(gb_i)], axis=1)

    a_r = pr[CHUNK]
    a_i = pi[CHUNK]
    a1 = jnp.concatenate([a_r, a_r], axis=-1)
    a2 = jnp.concatenate([-a_i, a_i], axis=-1)
    a3 = -a2
    zero = jnp.zeros_like(a1[0])
    coef = jnp.stack([a1[0], a2[0], a3[0], a1[1], a2[1], a3[1], zero, zero], axis=1)
    return (m_intra.astype(jnp.bfloat16), w_state.astype(jnp.bfloat16),
            w_out.astype(jnp.bfloat16), coef)


def _rope_tables():
    half = HEAD_DIM // 2
    pos = jnp.arange(SEQ, dtype=jnp.int32)
    inv_freq = ROPE_THETA ** (-jnp.arange(half, dtype=jnp.float32) / half)
    ang = pos.astype(jnp.float32)[:, None] * inv_freq[None, :]
    cos = jnp.cos(ang)
    sin = jnp.sin(ang)
    return jnp.concatenate([cos, cos, cos, cos, -sin, sin, -sin, sin], axis=1)


def _rope_block(xb, cos_t, sin_t, first_half):
    swapped = jnp.where(first_half, pltpu.roll(xb, 96, 1), pltpu.roll(xb, 32, 1))
    return xb * cos_t + swapped * sin_t


def _inproj_kernel(x_ref, wq_ref, wkv_ref, wz_ref, wu_ref, rope_ref,
                   q_ref, kv_ref, z_ref, ut_ref, uslab_ref):
    step = pl.program_id(0)
    part = step % STEPS_PER_REGROUP
    lane = lax.broadcasted_iota(jnp.int32, (SUB_ROWS, LANES), 1)
    first_half = (lane % HEAD_DIM) < (HEAD_DIM // 2)

    for sub in range(TOK_BLOCK // SUB_ROWS):
        rows = slice(sub * SUB_ROWS, (sub + 1) * SUB_ROWS)
        xb = x_ref[rows, :].astype(jnp.bfloat16)
        cos_t = rope_ref[rows, :LANES]
        sin_t = rope_ref[rows, LANES:]
        q = jnp.dot(xb, wq_ref[...], preferred_element_type=jnp.float32)
        q_blocks = [_rope_block(q[:, v * LANES:(v + 1) * LANES], cos_t, sin_t, first_half)
                    for v in range(Q_COLS // LANES)]
        q_ref[rows, :] = jnp.concatenate(q_blocks, axis=1).astype(q_ref.dtype)
        kv = jnp.dot(xb, wkv_ref[...], preferred_element_type=jnp.float32)
        k_rot = _rope_block(kv[:, :LANES], cos_t, sin_t, first_half)
        kv_ref[rows, :] = jnp.concatenate([k_rot, kv[:, LANES:]], axis=1).astype(kv_ref.dtype)
        z_ref[rows, :] = jnp.dot(xb, wz_ref[...], preferred_element_type=jnp.float32).astype(z_ref.dtype)
        u = jnp.dot(xb, wu_ref[...], preferred_element_type=jnp.float32)
        for v in range(N_LANE_BLOCKS):
            dst = pl.ds(pl.multiple_of(part * TOK_BLOCK + sub * SUB_ROWS, SUB_ROWS), SUB_ROWS)
            uslab_ref[v, dst, :] = u[:, v * LANES:(v + 1) * LANES]

    @pl.when(part == STEPS_PER_REGROUP - 1)
    def _():
        for j in range(CHUNK):
            for v in range(N_LANE_BLOCKS):
                uj = uslab_ref[v, pl.ds(j, LANES, stride=CHUNK), :]
                ujt = jnp.transpose(uj).reshape(GROUPS_PER_LANE_BLOCK, SSM_CH, LANES)
                g0 = v * GROUPS_PER_LANE_BLOCK
                ut_ref[g0:g0 + GROUPS_PER_LANE_BLOCK, j * SSM_CH:(j + 1) * SSM_CH, :] = ujt


def _inproj(x2, wq, wkv, wz, wu, rope):
    n_steps = BATCH * SEQ // TOK_BLOCK
    blocks_per_seq = SEQ // TOK_BLOCK

    def tok(cols):
        return pl.BlockSpec((TOK_BLOCK, cols), lambda s: (s, 0))

    def whole(shape):
        return pl.BlockSpec(shape, lambda s: (0,) * len(shape))

    return pl.pallas_call(
        _inproj_kernel,
        grid=(n_steps,),
        in_specs=[tok(D_MODEL), whole(wq.shape), whole(wkv.shape), whole(wz.shape), whole(wu.shape),
                  pl.BlockSpec((TOK_BLOCK, 2 * LANES), lambda s: (s % blocks_per_seq, 0))],
        out_specs=[tok(Q_COLS), tok(2 * KV_COLS), tok(D_ATTN + D_SSM),
                   pl.BlockSpec((N_GROUPS, CHUNK_VEC, LANES), lambda s: (0, 0, s // STEPS_PER_REGROUP))],
        out_shape=[jax.ShapeDtypeStruct((BATCH * SEQ, Q_COLS), jnp.bfloat16),
                   jax.ShapeDtypeStruct((BATCH * SEQ, 2 * KV_COLS), jnp.bfloat16),
                   jax.ShapeDtypeStruct((BATCH * SEQ, D_ATTN + D_SSM), jnp.bfloat16),
                   jax.ShapeDtypeStruct((N_GROUPS, CHUNK_VEC, ROWS), jnp.float32)],
        scratch_shapes=[pltpu.VMEM((N_LANE_BLOCKS, REGROUP_TOK, LANES), jnp.float32)],
        compiler_params=pltpu.CompilerParams(dimension_semantics=("arbitrary",),
                                             vmem_limit_bytes=VMEM_LIMIT),
        name="inproj",
    )(x2, wq, wkv, wz, wu, rope)


def _attn_kernel(q_ref, kp_ref, kc_ref, kn_ref, sink_ref, y_ref):
    i = pl.program_id(1)
    n_keys = 3 * BLOCK
    kvw = jnp.concatenate([kp_ref[...], kc_ref[...], kn_ref[...]], axis=0).astype(jnp.float32)
    q = q_ref[...].astype(jnp.float32) * (HEAD_DIM ** -0.5)

    r = lax.broadcasted_iota(jnp.int32, (BLOCK, n_keys), 0)
    c = lax.broadcasted_iota(jnp.int32, (BLOCK, n_keys), 1)
    lo = jnp.where(i == 0, BLOCK, 0)
    hi = jnp.where(i == SEQ // BLOCK - 1, 2 * BLOCK, n_keys)
    valid1 = (c >= r) & (c - r <= 2 * WINDOW) & (c >= lo) & (c < hi)
    valid = jnp.concatenate([valid1, valid1], axis=0)

    lane = lax.broadcasted_iota(jnp.int32, (n_keys, LANES), 1)
    outs = []
    for hk in range(N_KV_HEADS):
        in_head = (lane >= hk * HEAD_DIM) & (lane < (hk + 1) * HEAD_DIM)
        k_nat = jnp.where(in_head, kvw[:, :LANES], 0.0)
        v_nat = jnp.where(in_head, kvw[:, LANES:], 0.0)
        k_oth = pltpu.roll(k_nat, HEAD_DIM, 1)
        v_oth = pltpu.roll(v_nat, HEAD_DIM, 1)
        k_lo, k_hi = (k_nat, k_oth) if hk == 0 else (k_oth, k_nat)
        v_lo, v_hi = (v_nat, v_oth) if hk == 0 else (v_oth, v_nat)
        qq = jnp.concatenate([q[:, (2 * hk) * LANES:(2 * hk + 1) * LANES],
                              q[:, (2 * hk + 1) * LANES:(2 * hk + 2) * LANES]], axis=0).astype(jnp.bfloat16)
        acc = None
        for par, (kz, vz) in enumerate(((k_lo, v_lo), (k_hi, v_hi))):
            s = lax.dot_general(qq, kz.astype(jnp.bfloat16), (((1,), (1,)), ((), ())),
                                preferred_element_type=jnp.float32)
            s = jnp.where(valid, s, NEG_INF)
            h0 = 4 * hk + par
            h1 = h0 + 2
            sink = jnp.concatenate([jnp.broadcast_to(sink_ref[h0:h0 + 1, 0:1], (BLOCK, 1)),
                                    jnp.broadcast_to(sink_ref[h1:h1 + 1, 0:1], (BLOCK, 1))], axis=0)
            m = jnp.maximum(jnp.max(s, axis=-1, keepdims=True), sink)
            p = jnp.exp(s - m)
            denom = jnp.sum(p, axis=-1, keepdims=True) + jnp.exp(sink - m)
            o = jnp.dot(p.astype(jnp.bfloat16), vz.astype(jnp.bfloat16),
                        preferred_element_type=jnp.float32)
            o = o * (1.0 / denom)
            acc = o if acc is None else acc + o
        outs.append(acc[:BLOCK])
        outs.append(acc[BLOCK:])
    y_ref[...] = jnp.concatenate(outs, axis=1).astype(y_ref.dtype)


def _attention(q, kv, sink_tab):
    nb = SEQ // BLOCK
    return pl.pallas_call(
        _attn_kernel,
        grid=(BATCH, nb),
        in_specs=[pl.BlockSpec((None, BLOCK, Q_COLS), lambda b, i: (b, i, 0)),
                  pl.BlockSpec((None, BLOCK, 2 * KV_COLS), lambda b, i: (b, jnp.maximum(i - 1, 0), 0)),
                  pl.BlockSpec((None, BLOCK, 2 * KV_COLS), lambda b, i: (b, i, 0)),
                  pl.BlockSpec((None, BLOCK, 2 * KV_COLS), lambda b, i: (b, jnp.minimum(i + 1, nb - 1), 0)),
                  pl.BlockSpec((N_Q_HEADS, LANES), lambda b, i: (0, 0))],
        out_specs=pl.BlockSpec((None, BLOCK, D_ATTN), lambda b, i: (b, i, 0)),
        out_shape=jax.ShapeDtypeStruct((BATCH, SEQ, D_ATTN), jnp.bfloat16),
        compiler_params=pltpu.CompilerParams(dimension_semantics=("arbitrary", "arbitrary"),
                                             vmem_limit_bytes=VMEM_LIMIT),
        name="attention",
    )(q, kv, kv, kv, sink_tab)


def _ssm_kernel(ut_ref, m_ref, ws_ref, wo_ref, coef_ref, yt_ref,
                sf_ref, sfs_ref, sb_ref, sbs_ref, xf_ref, xb_ref):
    u = jnp.transpose(ut_ref[...]).astype(jnp.bfloat16)
    y = jnp.dot(u, m_ref[...], preferred_element_type=jnp.float32)
    s = jnp.dot(u, ws_ref[...], preferred_element_type=jnp.float32)

    for b in range(BATCH):
        rows = slice(b * N_CHUNKS, (b + 1) * N_CHUNKS)
        dst = pl.ds(b * SLAB_PITCH + 8, N_CHUNKS)
        sf_ref[dst, :] = s[rows, 0 * LANES:1 * LANES]
        sfs_ref[dst, :] = s[rows, 1 * LANES:2 * LANES]
        sb_ref[dst, :] = s[rows, 2 * LANES:3 * LANES]
        sbs_ref[dst, :] = s[rows, 3 * LANES:4 * LANES]

    def bcast(row):
        return jnp.broadcast_to(coef_ref[row:row + 1, :], (BATCH, LANES))

    a1f, a2f, a3f, a1b, a2b, a3b = (bcast(i) for i in range(6))
    zero = jnp.zeros((BATCH, LANES), jnp.float32)
    xf_ref[pl.ds(8, BATCH, stride=SLAB_PITCH), :] = zero
    xb_ref[pl.ds(8 + N_CHUNKS - 1, BATCH, stride=SLAB_PITCH), :] = zero

    def step(k, carry):
        xf, xfs, xb, xbs = carry
        sf = sf_ref[pl.ds(8 + k, BATCH, stride=SLAB_PITCH), :]
        sfs = sfs_ref[pl.ds(8 + k, BATCH, stride=SLAB_PITCH), :]
        nxf = a1f * xf + a2f * xfs + sf
        nxfs = a1f * xfs + a3f * xf + sfs
        xf_ref[pl.ds(9 + k, BATCH, stride=SLAB_PITCH), :] = nxf
        kb = N_CHUNKS - 1 - k
        sb = sb_ref[pl.ds(8 + kb, BATCH, stride=SLAB_PITCH), :]
        sbs = sbs_ref[pl.ds(8 + kb, BATCH, stride=SLAB_PITCH), :]
        nxb = a1b * xb + a2b * xbs + sb
        nxbs = a1b * xbs + a3b * xb + sbs
        xb_ref[pl.ds(7 + kb, BATCH, stride=SLAB_PITCH), :] = nxb
        return nxf, nxfs, nxb, nxbs

    lax.fori_loop(0, N_CHUNKS, step, (zero, zero, zero, zero))

    xin_f = jnp.concatenate([xf_ref[pl.ds(b * SLAB_PITCH + 8, N_CHUNKS), :] for b in range(BATCH)], axis=0)
    xin_b = jnp.concatenate([xb_ref[pl.ds(b * SLAB_PITCH + 8, N_CHUNKS), :] for b in range(BATCH)], axis=0)
    xin = jnp.concatenate([xin_f, xin_b], axis=1).astype(jnp.bfloat16)
    y = y + jnp.dot(xin, wo_ref[...], preferred_element_type=jnp.float32)
    yt_ref[...] = jnp.transpose(y)


def _ssm(ut, m_intra, w_state, w_out, coef):
    def per_group(shape):
        return pl.BlockSpec((None,) + shape, lambda g: (g, 0, 0))

    slab = pltpu.VMEM((SLAB_ROWS, LANES), jnp.float32)
    return pl.pallas_call(
        _ssm_kernel,
        grid=(N_GROUPS,),
        in_specs=[per_group((CHUNK_VEC, ROWS)), per_group((CHUNK_VEC, CHUNK_VEC)),
                  per_group((CHUNK_VEC, 4 * LANES)), per_group((2 * STATE_VEC, CHUNK_VEC)),
                  per_group((8, LANES))],
        out_specs=per_group((CHUNK_VEC, ROWS)),
        out_shape=jax.ShapeDtypeStruct((N_GROUPS, CHUNK_VEC, ROWS), jnp.float32),
        scratch_shapes=[slab] * 6,
        compiler_params=pltpu.CompilerParams(dimension_semantics=("arbitrary",),
                                             vmem_limit_bytes=VMEM_LIMIT),
        name="ssm",
    )(ut, m_intra, w_state, w_out, coef)


def _rms(v, gain):
    return v * lax.rsqrt(jnp.mean(v * v, axis=-1, keepdims=True) + NORM_EPS) * gain


def _final_kernel(x_ref, ya_ref, z_ref, yt_ref, wglu_ref, bglu_ref, ga_ref, gs_ref,
                  woa_ref, wos_ref, lng_ref, lnb_ref, o_ref, yslab_ref):
    step = pl.program_id(0)
    part = step % STEPS_PER_REGROUP

    @pl.when(part == 0)
    def _():
        for j in range(CHUNK):
            for v in range(N_LANE_BLOCKS):
                g0 = v * GROUPS_PER_LANE_BLOCK
                yjt = yt_ref[g0:g0 + GROUPS_PER_LANE_BLOCK, j * SSM_CH:(j + 1) * SSM_CH, :]
                yj = jnp.transpose(yjt.reshape(LANES, LANES))
                yslab_ref[v, pl.ds(j, LANES, stride=CHUNK), :] = yj

    for sub in range(TOK_BLOCK // SUB_ROWS):
        rows = slice(sub * SUB_ROWS, (sub + 1) * SUB_ROWS)
        z = z_ref[rows, :].astype(jnp.float32)
        ya = ya_ref[rows, :].astype(jnp.float32) * jax.nn.silu(z[:, :D_ATTN])
        ya = _rms(ya, ga_ref[...])

        src = pl.ds(pl.multiple_of(part * TOK_BLOCK + sub * SUB_ROWS, SUB_ROWS), SUB_ROWS)
        ys = jnp.concatenate([yslab_ref[v, src, :] for v in range(N_LANE_BLOCKS)], axis=1)
        g = jax.nn.gelu(ys)
        gate = jnp.dot(g.astype(jnp.bfloat16), wglu_ref[...], preferred_element_type=jnp.float32) + bglu_ref[...]
        ys = g * jax.nn.sigmoid(gate)
        ys = ys * jax.nn.silu(z[:, D_ATTN:])
        ys = _rms(ys, gs_ref[...])

        out = (jnp.dot(ya.astype(jnp.bfloat16), woa_ref[...], preferred_element_type=jnp.float32)
               + jnp.dot(ys.astype(jnp.bfloat16), wos_ref[...], preferred_element_type=jnp.float32))
        h = DEEPNORM_ALPHA * x_ref[rows, :] + out
        mu = jnp.mean(h, axis=-1, keepdims=True)
        hc = h - mu
        var = jnp.mean(hc * hc, axis=-1, keepdims=True)
        o_ref[rows, :] = hc * lax.rsqrt(var + NORM_EPS) * lng_ref[...] + lnb_ref[...]


def _final(x2, ya, z, yt, wglu, bglu, ga, gs, woa, wos, lng, lnb):
    n_steps = BATCH * SEQ // TOK_BLOCK

    def tok(cols):
        return pl.BlockSpec((TOK_BLOCK, cols), lambda s: (s, 0))

    def whole(shape):
        return pl.BlockSpec(shape, lambda s: (0,) * len(shape))

    return pl.pallas_call(
        _final_kernel,
        grid=(n_steps,),
        in_specs=[tok(D_MODEL), tok(D_ATTN), tok(D_ATTN + D_SSM),
                  pl.BlockSpec((N_GROUPS, CHUNK_VEC, LANES), lambda s: (0, 0, s // STEPS_PER_REGROUP)),
                  whole(wglu.shape), whole(bglu.shape), whole(ga.shape), whole(gs.shape),
                  whole(woa.shape), whole(wos.shape), whole(lng.shape), whole(lnb.shape)],
        out_specs=tok(D_MODEL),
        out_shape=jax.ShapeDtypeStruct((BATCH * SEQ, D_MODEL), jnp.float32),
        scratch_shapes=[pltpu.VMEM((N_LANE_BLOCKS, REGROUP_TOK, LANES), jnp.float32)],
        compiler_params=pltpu.CompilerParams(dimension_semantics=("arbitrary",),
                                             vmem_limit_bytes=VMEM_LIMIT),
        name="final",
    )(x2, ya, z, yt, wglu, bglu, ga, gs, woa, wos, lng, lnb)


def kernel(x, w_in, attn_sink, ssm_a_re, ssm_a_im, ssm_log_dt, ssm_b_re, ssm_b_im, ssm_c_re, ssm_c_im,
           ssm_d, w_glu, b_glu, norm_attn_g, norm_ssm_g, w_out, ln_g, ln_b):
    assert x.shape == (BATCH, SEQ, D_MODEL) and w_in.shape[0] == 1
    bf16 = jnp.bfloat16
    w = w_in[0]
    c0 = Q_COLS
    c1 = c0 + 2 * KV_COLS
    c2 = c1 + D_ATTN
    c3 = c2 + D_SSM
    wq = w[:, :c0].astype(bf16)
    wkv = w[:, c0:c1].astype(bf16)
    wz = jnp.concatenate([w[:, c1:c2], w[:, c3:]], axis=1).astype(bf16)
    wu = w[:, c2:c3].astype(bf16)

    m_intra, w_state, w_so, coef = _ssm_tables(ssm_a_re[0], ssm_a_im[0], ssm_log_dt[0], ssm_b_re[0],
                                                ssm_b_im[0], ssm_c_re[0], ssm_c_im[0], ssm_d[0])
    rope = _rope_tables()
    sink_tab = jnp.broadcast_to(attn_sink[0].astype(jnp.float32)[:, None], (N_Q_HEADS, LANES))

    x2 = x.reshape(BATCH * SEQ, D_MODEL)
    q, kv, z, ut = _inproj(x2, wq, wkv, wz, wu, rope)
    ya = _attention(q.reshape(BATCH, SEQ, Q_COLS), kv.reshape(BATCH, SEQ, 2 * KV_COLS), sink_tab)
    yt = _ssm(ut, m_intra, w_state, w_so, coef)

    row = lambda v: v[0].astype(jnp.float32)[None, :]
    out = _final(x2, ya.reshape(BATCH * SEQ, D_ATTN), z, yt,
                 w_glu[0].astype(bf16), row(b_glu), row(norm_attn_g), row(norm_ssm_g),
                 w_out[0, :D_ATTN].astype(bf16), w_out[0, D_ATTN:].astype(bf16), row(ln_g), row(ln_b))
    return out.reshape(BATCH, SEQ, D_MODEL)
```

```python
import jax
import jax.numpy as jnp
from jax import lax
from jax.experimental import pallas as pl
from jax.experimental.pallas import tpu as pltpu

D_MODEL = 1024
BATCH = 4
SEQ = 4096
D_ATTN = 512
D_SSM = 512
HEAD_DIM = 64
N_Q_HEADS = 8
N_KV_HEADS = 2
WINDOW = 128
BLOCK = 128
ROPE_THETA = 10000.0
SSM_CH = 16
N_GROUPS = 32
SSM_STATE = 64
NORM_EPS = 1e-5
NEG_INF = -1e30
DEEPNORM_ALPHA = 2.0 ** 0.25

Q_COLS = N_Q_HEADS * HEAD_DIM
KV_COLS = N_KV_HEADS * HEAD_DIM
CHUNK = 16
N_CHUNKS = SEQ // CHUNK
CHUNK_VEC = CHUNK * SSM_CH
STATE_VEC = 2 * SSM_STATE
LANES = 128
ROWS = BATCH * N_CHUNKS
SLAB_PITCH = N_CHUNKS + 8
SLAB_ROWS = BATCH * SLAB_PITCH + 8
TOK_BLOCK = 1024
REGROUP_TOK = LANES * CHUNK
STEPS_PER_REGROUP = REGROUP_TOK // TOK_BLOCK
SUB_ROWS = 512
N_LANE_BLOCKS = D_SSM // LANES
GROUPS_PER_LANE_BLOCK = LANES // SSM_CH
VMEM_LIMIT = 56 * 1024 * 1024
HIGHEST = lax.Precision.HIGHEST


def _ssm_tables(a_re, a_im, log_dt, b_re, b_im, c_re, c_im, d):
    f32 = jnp.float32
    dt = jnp.exp(log_dt.astype(f32))[..., None]
    ar = a_re.astype(f32)
    ai = a_im.astype(f32)
    zr = dt * ar
    zi = dt * ai
    mag = jnp.exp(zr)
    lr = mag * jnp.cos(zi)
    li = mag * jnp.sin(zi)
    den = ar * ar + ai * ai
    fr = ((lr - 1.0) * ar + li * ai) / den
    fi = (li * ar - (lr - 1.0) * ai) / den
    br = b_re.astype(f32)
    bi = b_im.astype(f32)
    bbr = fr[..., None] * br - fi[..., None] * bi
    bbi = fr[..., None] * bi + fi[..., None] * br
    cr = c_re.astype(f32)
    ci = c_im.astype(f32)

    tau = jnp.arange(CHUNK + 1, dtype=f32)[:, None, None, None]
    pmag = jnp.exp(tau * zr[None])
    pr = pmag * jnp.cos(tau * zi[None])
    pi = pmag * jnp.sin(tau * zi[None])

    er = pr[..., None] * bbr[None] - pi[..., None] * bbi[None]
    ei = pr[..., None] * bbi[None] + pi[..., None] * bbr[None]
    gr = cr[None] * pr[:, :, :, None, :] - ci[None] * pi[:, :, :, None, :]
    gi = cr[None] * pi[:, :, :, None, :] + ci[None] * pr[:, :, :, None, :]

    kk = (jnp.einsum('dgcp,tdgpe->tdgce', cr, er, precision=HIGHEST)
          - jnp.einsum('dgcp,tdgpe->tdgce', ci, ei, precision=HIGHEST))
    kf = kk[:CHUNK, 0]
    kb = kk[:CHUNK, 1]
    dmat = d.astype(f32).reshape(N_GROUPS, SSM_CH)[:, :, None] * jnp.eye(SSM_CH, dtype=f32)[None]
    center = kf[0] + kb[0] + dmat
    kall = jnp.concatenate([kb[:0:-1], center[None], kf[1:]], axis=0)
    jj = jnp.arange(CHUNK)
    lag_idx = jj[None, :] - jj[:, None] + (CHUNK - 1)
    m5 = kall[lag_idx]
    m_intra = jnp.transpose(m5, (2, 0, 4, 1, 3)).reshape(N_GROUPS, CHUNK_VEC, CHUNK_VEC)

    ef_r = er[:CHUNK, 0][::-1]
    ef_i = ei[:CHUNK, 0][::-1]
    eb_r = er[:CHUNK, 1]
    eb_i = ei[:CHUNK, 1]

    def _rows(e):
        return jnp.transpose(e, (1, 0, 3, 2)).reshape(N_GROUPS, CHUNK_VEC, SSM_STATE)

    w_state = jnp.concatenate([_rows(ef_r), _rows(ef_i), _rows(ef_i), _rows(ef_r),
                               _rows(eb_r), _rows(eb_i), _rows(eb_i), _rows(eb_r)], axis=2)

    def _cols(g):
        return jnp.transpose(g, (1, 3, 0, 2)).reshape(N_GROUPS, SSM_STATE, CHUNK_VEC)

    gf_r = gr[1:, 0]
    gf_i = gi[1:, 0]
    gb_r = gr[1:, 1][::-1]
    gb_i = gi[1:, 1][::-1]
    w_out = jnp.concatenate([_cols(gf_r), -_cols(gf_i), _cols(gb_r), -_cols(gb_i)], axis=1)

    a_r = pr[CHUNK]
    a_i = pi[CHUNK]
    a1 = jnp.concatenate([a_r, a_r], axis=-1)
    a2 = jnp.concatenate([-a_i, a_i], axis=-1)
    a3 = -a2
    zero = jnp.zeros_like(a1[0])
    coef = jnp.stack([a1[0], a2[0], a3[0], a1[1], a2[1], a3[1], zero, zero], axis=1)
    return (m_intra.astype(jnp.bfloat16), w_state.astype(jnp.bfloat16),
            w_out.astype(jnp.bfloat16), coef)


def _rope_tables():
    half = HEAD_DIM // 2
    pos = jnp.arange(SEQ, dtype=jnp.int32)
    inv_freq = ROPE_THETA ** (-jnp.arange(half, dtype=jnp.float32) / half)
    ang = pos.astype(jnp.float32)[:, None] * inv_freq[None, :]
    cos = jnp.cos(ang)
    sin = jnp.sin(ang)
    return jnp.concatenate([cos, cos, cos, cos, -sin, sin, -sin, sin], axis=1)


def _rope_block(xb, cos_t, sin_t, first_half):
    swapped = jnp.where(first_half, pltpu.roll(xb, 96, 1), pltpu.roll(xb, 32, 1))
    return xb * cos_t + swapped * sin_t


def _inproj_kernel(x_ref, wq_ref, wkv_ref, wz_ref, wu_ref, rope_ref,
                   q_ref, kv_ref, z_ref, ut_ref, uslab_ref):
    step = pl.program_id(0)
    part = step % STEPS_PER_REGROUP
    lane = lax.broadcasted_iota(jnp.int32, (SUB_ROWS, LANES), 1)
    first_half = (lane % HEAD_DIM) < (HEAD_DIM // 2)

    for sub in range(TOK_BLOCK // SUB_ROWS):
        rows = slice(sub * SUB_ROWS, (sub + 1) * SUB_ROWS)
        xb = x_ref[rows, :].astype(jnp.bfloat16)
        cos_t = rope_ref[rows, :LANES]
        sin_t = rope_ref[rows, LANES:]
        q = jnp.dot(xb, wq_ref[...], preferred_element_type=jnp.float32)
        q_blocks = [_rope_block(q[:, v * LANES:(v + 1) * LANES], cos_t, sin_t, first_half)
                    for v in range(Q_COLS // LANES)]
        q_ref[rows, :] = jnp.concatenate(q_blocks, axis=1).astype(q_ref.dtype)
        kv = jnp.dot(xb, wkv_ref[...], preferred_element_type=jnp.float32)
        k_rot = _rope_block(kv[:, :LANES], cos_t, sin_t, first_half)
        kv_ref[rows, :] = jnp.concatenate([k_rot, kv[:, LANES:]], axis=1).astype(kv_ref.dtype)
        z_ref[rows, :] = jnp.dot(xb, wz_ref[...], preferred_element_type=jnp.float32).astype(z_ref.dtype)
        u = jnp.dot(xb, wu_ref[...], preferred_element_type=jnp.float32)
        for v in range(N_LANE_BLOCKS):
            dst = pl.ds(pl.multiple_of(part * TOK_BLOCK + sub * SUB_ROWS, SUB_ROWS), SUB_ROWS)
            uslab_ref[v, dst, :] = u[:, v * LANES:(v + 1) * LANES]

    @pl.when(part == STEPS_PER_REGROUP - 1)
    def _():
        for j in range(CHUNK):
            for v in range(N_LANE_BLOCKS):
                uj = uslab_ref[v, pl.ds(j, LANES, stride=CHUNK), :]
                ujt = jnp.transpose(uj).reshape(GROUPS_PER_LANE_BLOCK, SSM_CH, LANES)
                g0 = v * GROUPS_PER_LANE_BLOCK
                ut_ref[g0:g0 + GROUPS_PER_LANE_BLOCK, j * SSM_CH:(j + 1) * SSM_CH, :] = ujt


def _inproj(x2, wq, wkv, wz, wu, rope):
    n_steps = BATCH * SEQ // TOK_BLOCK
    blocks_per_seq = SEQ // TOK_BLOCK

    def tok(cols):
        return pl.BlockSpec((TOK_BLOCK, cols), lambda s: (s, 0))

    def whole(shape):
        return pl.BlockSpec(shape, lambda s: (0,) * len(shape))

    return pl.pallas_call(
        _inproj_kernel,
        grid=(n_steps,),
        in_specs=[tok(D_MODEL), whole(wq.shape), whole(wkv.shape), whole(wz.shape), whole(wu.shape),
                  pl.BlockSpec((TOK_BLOCK, 2 * LANES), lambda s: (s % blocks_per_seq, 0))],
        out_specs=[tok(Q_COLS), tok(2 * KV_COLS), tok(D_ATTN + D_SSM),
                   pl.BlockSpec((N_GROUPS, CHUNK_VEC, LANES), lambda s: (0, 0, s // STEPS_PER_REGROUP))],
        out_shape=[jax.ShapeDtypeStruct((BATCH * SEQ, Q_COLS), jnp.bfloat16),
                   jax.ShapeDtypeStruct((BATCH * SEQ, 2 * KV_COLS), jnp.bfloat16),
                   jax.ShapeDtypeStruct((BATCH * SEQ, D_ATTN + D_SSM), jnp.bfloat16),
                   jax.ShapeDtypeStruct((N_GROUPS, CHUNK_VEC, ROWS), jnp.float32)],
        scratch_shapes=[pltpu.VMEM((N_LANE_BLOCKS, REGROUP_TOK, LANES), jnp.float32)],
        compiler_params=pltpu.CompilerParams(dimension_semantics=("arbitrary",),
                                             vmem_limit_bytes=VMEM_LIMIT),
        name="inproj",
    )(x2, wq, wkv, wz, wu, rope)


ATTN_SUB = 4
ATTN_STEP = ATTN_SUB * BLOCK
N_KEYS = 3 * BLOCK


def _attn_kernel(q_ref, kp_ref, kc_ref, kn_ref, sink_ref, y_ref):
    i = pl.program_id(1)
    kvw = jnp.concatenate([kp_ref[...], kc_ref[...], kn_ref[...]], axis=0).astype(jnp.float32)

    lane_w = lax.broadcasted_iota(jnp.int32, (kvw.shape[0], LANES), 1)
    k_lo, k_hi, v_lo, v_hi = [], [], [], []
    for hk in range(N_KV_HEADS):
        in_head = (lane_w >= hk * HEAD_DIM) & (lane_w < (hk + 1) * HEAD_DIM)
        k_nat = jnp.where(in_head, kvw[:, :LANES], 0.0)
        v_nat = jnp.where(in_head, kvw[:, LANES:], 0.0)
        k_oth = pltpu.roll(k_nat, HEAD_DIM, 1)
        v_oth = pltpu.roll(v_nat, HEAD_DIM, 1)
        pairs = ((k_nat, k_oth), (v_nat, v_oth)) if hk == 0 else ((k_oth, k_nat), (v_oth, v_nat))
        k_lo.append(pairs[0][0].astype(jnp.bfloat16))
        k_hi.append(pairs[0][1].astype(jnp.bfloat16))
        v_lo.append(pairs[1][0].astype(jnp.bfloat16))
        v_hi.append(pairs[1][1].astype(jnp.bfloat16))

    r = lax.broadcasted_iota(jnp.int32, (BLOCK, N_KEYS), 0)
    c = lax.broadcasted_iota(jnp.int32, (BLOCK, N_KEYS), 1)
    band = (c >= r) & (c - r <= 2 * WINDOW)
    lane_o = lax.broadcasted_iota(jnp.int32, (2 * BLOCK, LANES), 1)
    n_blocks = SEQ // BLOCK

    for t in range(ATTN_SUB):
        blk = i * ATTN_SUB + t
        lo = jnp.where(blk == 0, BLOCK, 0)
        hi = jnp.where(blk == n_blocks - 1, 2 * BLOCK, N_KEYS)
        valid1 = band & (c >= lo) & (c < hi)
        valid = jnp.concatenate([valid1, valid1], axis=0)
        win = slice(t * BLOCK, t * BLOCK + N_KEYS)
        q = q_ref[t * BLOCK:(t + 1) * BLOCK, :].astype(jnp.float32) * (HEAD_DIM ** -0.5)
        for hk in range(N_KV_HEADS):
            qq = jnp.concatenate([q[:, (2 * hk) * LANES:(2 * hk + 1) * LANES],
                                  q[:, (2 * hk + 1) * LANES:(2 * hk + 2) * LANES]], axis=0).astype(jnp.bfloat16)
            kcat = jnp.concatenate([k_lo[hk][win], k_hi[hk][win]], axis=0)
            vcat = jnp.concatenate([v_lo[hk][win], v_hi[hk][win]], axis=0)
            s = lax.dot_general(qq, kcat, (((1,), (1,)), ((), ())),
                                preferred_element_type=jnp.float32)
            ps, inv = [], []
            for par in range(2):
                sp = jnp.where(valid, s[:, par * N_KEYS:(par + 1) * N_KEYS], NEG_INF)
                h0 = 4 * hk + par
                h1 = h0 + 2
                sink = jnp.concatenate([jnp.broadcast_to(sink_ref[h0:h0 + 1, 0:1], (BLOCK, 1)),
                                        jnp.broadcast_to(sink_ref[h1:h1 + 1, 0:1], (BLOCK, 1))], axis=0)
                m = jnp.maximum(jnp.max(sp, axis=-1, keepdims=True), sink)
                p = jnp.exp(sp - m)
                denom = jnp.sum(p, axis=-1, keepdims=True) + jnp.exp(sink - m)
                ps.append(p.astype(jnp.bfloat16))
                inv.append(1.0 / denom)
            o = jnp.dot(jnp.concatenate(ps, axis=1), vcat, preferred_element_type=jnp.float32)
            o = o * jnp.where(lane_o < HEAD_DIM, inv[0], inv[1])
            rows = slice(t * BLOCK, (t + 1) * BLOCK)
            y_ref[rows, (2 * hk) * LANES:(2 * hk + 1) * LANES] = o[:BLOCK].astype(y_ref.dtype)
            y_ref[rows, (2 * hk + 1) * LANES:(2 * hk + 2) * LANES] = o[BLOCK:].astype(y_ref.dtype)


def _attention(q, kv, sink_tab):
    nb = SEQ // BLOCK
    return pl.pallas_call(
        _attn_kernel,
        grid=(BATCH, SEQ // ATTN_STEP),
        in_specs=[pl.BlockSpec((None, ATTN_STEP, Q_COLS), lambda b, i: (b, i, 0)),
                  pl.BlockSpec((None, BLOCK, 2 * KV_COLS),
                               lambda b, i: (b, jnp.maximum(i * ATTN_SUB - 1, 0), 0)),
                  pl.BlockSpec((None, ATTN_STEP, 2 * KV_COLS), lambda b, i: (b, i, 0)),
                  pl.BlockSpec((None, BLOCK, 2 * KV_COLS),
                               lambda b, i: (b, jnp.minimum((i + 1) * ATTN_SUB, nb - 1), 0)),
                  pl.BlockSpec((N_Q_HEADS, LANES), lambda b, i: (0, 0))],
        out_specs=pl.BlockSpec((None, ATTN_STEP, D_ATTN), lambda b, i: (b, i, 0)),
        out_shape=jax.ShapeDtypeStruct((BATCH, SEQ, D_ATTN), jnp.bfloat16),
        compiler_params=pltpu.CompilerParams(dimension_semantics=("arbitrary", "arbitrary"),
                                             vmem_limit_bytes=VMEM_LIMIT),
        name="attention",
    )(q, kv, kv, kv, sink_tab)


def _ssm_kernel(ut_ref, m_ref, ws_ref, wo_ref, coef_ref, yt_ref,
                sf_ref, sfs_ref, sb_ref, sbs_ref, xf_ref, xb_ref):
    u = jnp.transpose(ut_ref[...]).astype(jnp.bfloat16)
    y = jnp.dot(u, m_ref[...], preferred_element_type=jnp.float32)
    s = jnp.dot(u, ws_ref[...], preferred_element_type=jnp.float32)

    for b in range(BATCH):
        rows = slice(b * N_CHUNKS, (b + 1) * N_CHUNKS)
        dst = pl.ds(b * SLAB_PITCH + 8, N_CHUNKS)
        sf_ref[dst, :] = s[rows, 0 * LANES:1 * LANES]
        sfs_ref[dst, :] = s[rows, 1 * LANES:2 * LANES]
        sb_ref[dst, :] = s[rows, 2 * LANES:3 * LANES]
        sbs_ref[dst, :] = s[rows, 3 * LANES:4 * LANES]

    def bcast(row):
        return jnp.broadcast_to(coef_ref[row:row + 1, :], (BATCH, LANES))

    a1f, a2f, a3f, a1b, a2b, a3b = (bcast(i) for i in range(6))
    zero = jnp.zeros((BATCH, LANES), jnp.float32)
    xf_ref[pl.ds(8, BATCH, stride=SLAB_PITCH), :] = zero
    xb_ref[pl.ds(8 + N_CHUNKS - 1, BATCH, stride=SLAB_PITCH), :] = zero

    def step(k, carry):
        xf, xfs, xb, xbs = carry
        sf = sf_ref[pl.ds(8 + k, BATCH, stride=SLAB_PITCH), :]
        sfs = sfs_ref[pl.ds(8 + k, BATCH, stride=SLAB_PITCH), :]
        nxf = a1f * xf + a2f * xfs + sf
        nxfs = a1f * xfs + a3f * xf + sfs
        xf_ref[pl.ds(9 + k, BATCH, stride=SLAB_PITCH), :] = nxf
        kb = N_CHUNKS - 1 - k
        sb = sb_ref[pl.ds(8 + kb, BATCH, stride=SLAB_PITCH), :]
        sbs = sbs_ref[pl.ds(8 + kb, BATCH, stride=SLAB_PITCH), :]
        nxb = a1b * xb + a2b * xbs + sb
        nxbs = a1b * xbs + a3b * xb + sbs
        xb_ref[pl.ds(7 + kb, BATCH, stride=SLAB_PITCH), :] = nxb
        return nxf, nxfs, nxb, nxbs

    lax.fori_loop(0, N_CHUNKS, step, (zero, zero, zero, zero))

    xin_f = jnp.concatenate([xf_ref[pl.ds(b * SLAB_PITCH + 8, N_CHUNKS), :] for b in range(BATCH)], axis=0)
    xin_b = jnp.concatenate([xb_ref[pl.ds(b * SLAB_PITCH + 8, N_CHUNKS), :] for b in range(BATCH)], axis=0)
    xin = jnp.concatenate([xin_f, xin_b], axis=1).astype(jnp.bfloat16)
    y = y + jnp.dot(xin, wo_ref[...], preferred_element_type=jnp.float32)
    yt_ref[...] = jnp.transpose(y)


def _ssm(ut, m_intra, w_state, w_out, coef):
    def per_group(shape):
        return pl.BlockSpec((None,) + shape, lambda g: (g, 0, 0))

    slab = pltpu.VMEM((SLAB_ROWS, LANES), jnp.float32)
    return pl.pallas_call(
        _ssm_kernel,
        grid=(N_GROUPS,),
        in_specs=[per_group((CHUNK_VEC, ROWS)), per_group((CHUNK_VEC, CHUNK_VEC)),
                  per_group((CHUNK_VEC, 4 * LANES)), per_group((2 * STATE_VEC, CHUNK_VEC)),
                  per_group((8, LANES))],
        out_specs=per_group((CHUNK_VEC, ROWS)),
        out_shape=jax.ShapeDtypeStruct((N_GROUPS, CHUNK_VEC, ROWS), jnp.float32),
        scratch_shapes=[slab] * 6,
        compiler_params=pltpu.CompilerParams(dimension_semantics=("arbitrary",),
                                             vmem_limit_bytes=VMEM_LIMIT),
        name="ssm",
    )(ut, m_intra, w_state, w_out, coef)


def _rms(v, gain):
    return v * lax.rsqrt(jnp.mean(v * v, axis=-1, keepdims=True) + NORM_EPS) * gain


def _final_kernel(x_ref, ya_ref, z_ref, yt_ref, wglu_ref, bglu_ref, ga_ref, gs_ref,
                  woa_ref, wos_ref, lng_ref, lnb_ref, o_ref, yslab_ref):
    step = pl.program_id(0)
    part = step % STEPS_PER_REGROUP

    @pl.when(part == 0)
    def _():
        for j in range(CHUNK):
            for v in range(N_LANE_BLOCKS):
                g0 = v * GROUPS_PER_LANE_BLOCK
                yjt = yt_ref[g0:g0 + GROUPS_PER_LANE_BLOCK, j * SSM_CH:(j + 1) * SSM_CH, :]
                yj = jnp.transpose(yjt.reshape(LANES, LANES))
                yslab_ref[v, pl.ds(j, LANES, stride=CHUNK), :] = yj

    for sub in range(TOK_BLOCK // SUB_ROWS):
        rows = slice(sub * SUB_ROWS, (sub + 1) * SUB_ROWS)
        z = z_ref[rows, :].astype(jnp.float32)
        ya = ya_ref[rows, :].astype(jnp.float32) * jax.nn.silu(z[:, :D_ATTN])
        ya = _rms(ya, ga_ref[...])

        src = pl.ds(pl.multiple_of(part * TOK_BLOCK + sub * SUB_ROWS, SUB_ROWS), SUB_ROWS)
        ys = jnp.concatenate([yslab_ref[v, src, :] for v in range(N_LANE_BLOCKS)], axis=1)
        g = jax.nn.gelu(ys)
        gate = jnp.dot(g.astype(jnp.bfloat16), wglu_ref[...], preferred_element_type=jnp.float32) + bglu_ref[...]
        ys = g * jax.nn.sigmoid(gate)
        ys = ys * jax.nn.silu(z[:, D_ATTN:])
        ys = _rms(ys, gs_ref[...])

        out = (jnp.dot(ya.astype(jnp.bfloat16), woa_ref[...], preferred_element_type=jnp.float32)
               + jnp.dot(ys.astype(jnp.bfloat16), wos_ref[...], preferred_element_type=jnp.float32))
        h = DEEPNORM_ALPHA * x_ref[rows, :] + out
        mu = jnp.mean(h, axis=-1, keepdims=True)
        hc = h - mu
        var = jnp.mean(hc * hc, axis=-1, keepdims=True)
        o_ref[rows, :] = hc * lax.rsqrt(var + NORM_EPS) * lng_ref[...] + lnb_ref[...]


def _final(x2, ya, z, yt, wglu, bglu, ga, gs, woa, wos, lng, lnb):
    n_steps = BATCH * SEQ // TOK_BLOCK

    def tok(cols):
        return pl.BlockSpec((TOK_BLOCK, cols), lambda s: (s, 0))

    def whole(shape):
        return pl.BlockSpec(shape, lambda s: (0,) * len(shape))

    return pl.pallas_call(
        _final_kernel,
        grid=(n_steps,),
        in_specs=[tok(D_MODEL), tok(D_ATTN), tok(D_ATTN + D_SSM),
                  pl.BlockSpec((N_GROUPS, CHUNK_VEC, LANES), lambda s: (0, 0, s // STEPS_PER_REGROUP)),
                  whole(wglu.shape), whole(bglu.shape), whole(ga.shape), whole(gs.shape),
                  whole(woa.shape), whole(wos.shape), whole(lng.shape), whole(lnb.shape)],
        out_specs=tok(D_MODEL),
        out_shape=jax.ShapeDtypeStruct((BATCH * SEQ, D_MODEL), jnp.float32),
        scratch_shapes=[pltpu.VMEM((N_LANE_BLOCKS, REGROUP_TOK, LANES), jnp.float32)],
        compiler_params=pltpu.CompilerParams(dimension_semantics=("arbitrary",),
                                             vmem_limit_bytes=VMEM_LIMIT),
        name="final",
    )(x2, ya, z, yt, wglu, bglu, ga, gs, woa, wos, lng, lnb)


def kernel(x, w_in, attn_sink, ssm_a_re, ssm_a_im, ssm_log_dt, ssm_b_re, ssm_b_im, ssm_c_re, ssm_c_im,
           ssm_d, w_glu, b_glu, norm_attn_g, norm_ssm_g, w_out, ln_g, ln_b):
    assert x.shape == (BATCH, SEQ, D_MODEL) and w_in.shape[0] == 1
    bf16 = jnp.bfloat16
    w = w_in[0]
    c0 = Q_COLS
    c1 = c0 + 2 * KV_COLS
    c2 = c1 + D_ATTN
    c3 = c2 + D_SSM
    wq = w[:, :c0].astype(bf16)
    wkv = w[:, c0:c1].astype(bf16)
    wz = jnp.concatenate([w[:, c1:c2], w[:, c3:]], axis=1).astype(bf16)
    wu = w[:, c2:c3].astype(bf16)

    m_intra, w_state, w_so, coef = _ssm_tables(ssm_a_re[0], ssm_a_im[0], ssm_log_dt[0], ssm_b_re[0],
                                                ssm_b_im[0], ssm_c_re[0], ssm_c_im[0], ssm_d[0])
    rope = _rope_tables()
    sink_tab = jnp.broadcast_to(attn_sink[0].astype(jnp.float32)[:, None], (N_Q_HEADS, LANES))

    x2 = x.reshape(BATCH * SEQ, D_MODEL)
    q, kv, z, ut = _inproj(x2, wq, wkv, wz, wu, rope)
    ya = _attention(q.reshape(BATCH, SEQ, Q_COLS), kv.reshape(BATCH, SEQ, 2 * KV_COLS), sink_tab)
    yt = _ssm(ut, m_intra, w_state, w_so, coef)

    row = lambda v: v[0].astype(jnp.float32)[None, :]
    out = _final(x2, ya.reshape(BATCH * SEQ, D_ATTN), z, yt,
                 w_glu[0].astype(bf16), row(b_glu), row(norm_attn_g), row(norm_ssm_g),
                 w_out[0, :D_ATTN].astype(bf16), w_out[0, D_ATTN:].astype(bf16), row(ln_g), row(ln_b))
    return out.reshape(BATCH, SEQ, D_MODEL)
```

```python
import jax
import jax.numpy as jnp
from jax import lax
from jax.experimental import pallas as pl
from jax.experimental.pallas import tpu as pltpu

D_MODEL = 1024
BATCH = 4
SEQ = 4096
D_ATTN = 512
D_SSM = 512
HEAD_DIM = 64
N_Q_HEADS = 8
N_KV_HEADS = 2
WINDOW = 128
BLOCK = 128
ROPE_THETA = 10000.0
SSM_CH = 16
N_GROUPS = 32
SSM_STATE = 64
NORM_EPS = 1e-5
NEG_INF = -1e30
DEEPNORM_ALPHA = 2.0 ** 0.25

Q_COLS = N_Q_HEADS * HEAD_DIM
KV_COLS = N_KV_HEADS * HEAD_DIM
CHUNK = 16
N_CHUNKS = SEQ // CHUNK
CHUNK_VEC = CHUNK * SSM_CH
STATE_VEC = 2 * SSM_STATE
LANES = 128
ROWS = BATCH * N_CHUNKS
SLAB_PITCH = N_CHUNKS + 8
SLAB_ROWS = BATCH * SLAB_PITCH + 8
TOK_BLOCK = 1024
REGROUP_TOK = LANES * CHUNK
STEPS_PER_REGROUP = REGROUP_TOK // TOK_BLOCK
SUB_ROWS = 512
N_LANE_BLOCKS = D_SSM // LANES
GROUPS_PER_LANE_BLOCK = LANES // SSM_CH
VMEM_LIMIT = 56 * 1024 * 1024
HIGHEST = lax.Precision.HIGHEST


LAG_ROWS = 512
TAU_ROWS = 24


def _prep_kernel(arow_ref, bt_ref, c_ref, dt_ref, m_ref, ws_ref, wo_ref, coef_ref):
    f32 = jnp.float32
    lo16 = lax.broadcasted_iota(jnp.int32, (SSM_CH, LANES), 1) < SSM_STATE
    lo24 = lax.broadcasted_iota(jnp.int32, (TAU_ROWS, LANES), 1) < SSM_STATE
    tau = lax.broadcasted_iota(jnp.int32, (TAU_ROWS, LANES), 0).astype(f32)
    arow = arow_ref[...]

    lag_tabs, kws, ws_parts, coef_rows = [], [], [], []
    for d in range(2):
        ar = arow[3 * d:3 * d + 1, :]
        ai = arow[3 * d + 1:3 * d + 2, :]
        dt = jnp.exp(arow[3 * d + 2:3 * d + 3, :])
        zr = dt * ar
        zi = dt * ai
        mag = jnp.exp(zr)
        lr = mag * jnp.cos(zi)
        li = mag * jnp.sin(zi)
        den = ar * ar + ai * ai
        fr = ((lr - 1.0) * ar + li * ai) / den
        fi = (li * ar - (lr - 1.0) * ai) / den
        btr = bt_ref[0, d]
        bti = bt_ref[1, d]
        bbr = fr * btr - fi * bti
        bbi = fr * bti + fi * btr
        cr = c_ref[0, d]
        ci = c_ref[1, d]

        pmag = jnp.exp(tau * zr)
        pr = pmag * jnp.cos(tau * zi)
        pi = pmag * jnp.sin(tau * zi)
        pa = jnp.where(lo24, pr, pi)
        pb = jnp.where(lo24, -pi, pr)
        pc = jnp.where(lo24, pi, pr)
        pd = jnp.where(lo24, pr, -pi)

        def row(t, tab):
            return tab[t:t + 1, :]

        order = range(CHUNK + 1) if d == 0 else range(CHUNK, -1, -1)
        blocks = [cr * row(t, pa) + ci * row(t, pb) for t in order]
        blocks.append(jnp.zeros((LAG_ROWS - (CHUNK + 1) * SSM_CH, LANES), f32))
        lag_tab = jnp.concatenate(blocks, axis=0)
        lag_tabs.append(lag_tab)
        lhs = jnp.where(lo16, bbr, -bbi)
        kws.append(lax.dot_general(lhs, lag_tab, (((1,), (1,)), ((), ())),
                                   precision=HIGHEST, preferred_element_type=f32))

        parts = []
        for j in range(CHUNK):
            t = CHUNK - 1 - j if d == 0 else j
            parts.append(jnp.concatenate([bbr * row(t, pa) + bbi * row(t, pb),
                                          bbr * row(t, pc) + bbi * row(t, pd)], axis=1))
        ws_parts.append(jnp.concatenate(parts, axis=0))

        a1 = row(CHUNK, pr)
        a2 = jnp.where(lo24, -pi, pi)[CHUNK:CHUNK + 1, :]
        coef_rows += [a1, a2, -a2]

    ws_ref[...] = jnp.concatenate(ws_parts, axis=1).astype(ws_ref.dtype)
    coef_ref[...] = jnp.concatenate(coef_rows + [jnp.zeros((2, LANES), f32)], axis=0)

    sign = jnp.where(lax.broadcasted_iota(jnp.int32, (LANES, CHUNK_VEC), 0) < SSM_STATE, 1.0, -1.0)
    wo_f = jnp.transpose(lag_tabs[0][SSM_CH:SSM_CH + CHUNK_VEC, :]) * sign
    wo_b = jnp.transpose(lag_tabs[1][:CHUNK_VEC, :]) * sign
    wo_ref[...] = jnp.concatenate([wo_f, wo_b], axis=0).astype(wo_ref.dtype)

    sub = lax.broadcasted_iota(jnp.int32, (SSM_CH, CHUNK_VEC), 0)
    lane = lax.broadcasted_iota(jnp.int32, (SSM_CH, CHUNK_VEC), 1)
    dtile = jnp.broadcast_to(dt_ref[...], (SSM_CH, CHUNK_VEC))
    for jp in range(CHUNK):
        fwd = kws[0] if jp == 0 else pltpu.roll(kws[0], SSM_CH * jp, 1)
        bwd = pltpu.roll(kws[1], (LAG_ROWS - SSM_CH * (CHUNK - jp)) % LAG_ROWS, 1)
        skip = jnp.where((lane // SSM_CH == jp) & (lane % SSM_CH == sub), dtile, 0.0)
        blk = fwd[:, :CHUNK_VEC] + bwd[:, :CHUNK_VEC] + skip
        m_ref[jp * SSM_CH:(jp + 1) * SSM_CH, :] = blk.astype(m_ref.dtype)


def _ssm_tables(a_re, a_im, log_dt, b_re, b_im, c_re, c_im, d):
    f32 = jnp.float32
    dup = lambda v: jnp.concatenate([v, v], axis=-1)
    ldt = jnp.broadcast_to(log_dt.astype(f32)[..., None], a_re.shape)
    zero = jnp.zeros_like(ldt[0])
    arow = dup(jnp.stack([a_re[0], a_im[0], ldt[0], a_re[1], a_im[1], ldt[1], zero, zero], axis=1).astype(f32))
    bt = dup(jnp.swapaxes(jnp.stack([b_re, b_im]).astype(f32), -1, -2))
    cc = dup(jnp.stack([c_re, c_im]).astype(f32))
    dtile = jnp.tile(d.astype(f32).reshape(N_GROUPS, 1, SSM_CH), (1, 1, CHUNK))

    bf16 = jnp.bfloat16
    return pl.pallas_call(
        _prep_kernel,
        grid=(N_GROUPS,),
        in_specs=[pl.BlockSpec((None, 8, LANES), lambda g: (g, 0, 0)),
                  pl.BlockSpec((2, 2, None, SSM_CH, LANES), lambda g: (0, 0, g, 0, 0)),
                  pl.BlockSpec((2, 2, None, SSM_CH, LANES), lambda g: (0, 0, g, 0, 0)),
                  pl.BlockSpec((None, 1, CHUNK_VEC), lambda g: (g, 0, 0))],
        out_specs=[pl.BlockSpec((None, CHUNK_VEC, CHUNK_VEC), lambda g: (g, 0, 0)),
                   pl.BlockSpec((None, CHUNK_VEC, 4 * LANES), lambda g: (g, 0, 0)),
                   pl.BlockSpec((None, 2 * STATE_VEC, CHUNK_VEC), lambda g: (g, 0, 0)),
                   pl.BlockSpec((None, 8, LANES), lambda g: (g, 0, 0))],
        out_shape=[jax.ShapeDtypeStruct((N_GROUPS, CHUNK_VEC, CHUNK_VEC), bf16),
                   jax.ShapeDtypeStruct((N_GROUPS, CHUNK_VEC, 4 * LANES), bf16),
                   jax.ShapeDtypeStruct((N_GROUPS, 2 * STATE_VEC, CHUNK_VEC), bf16),
                   jax.ShapeDtypeStruct((N_GROUPS, 8, LANES), f32)],
        compiler_params=pltpu.CompilerParams(dimension_semantics=("arbitrary",),
                                             vmem_limit_bytes=VMEM_LIMIT),
        name="ssm_tables",
    )(arow, bt, cc, dtile)


def _ssm_tables_xla(a_re, a_im, log_dt, b_re, b_im, c_re, c_im, d):
    f32 = jnp.float32
    dt = jnp.exp(log_dt.astype(f32))[..., None]
    ar = a_re.astype(f32)
    ai = a_im.astype(f32)
    zr = dt * ar
    zi = dt * ai
    mag = jnp.exp(zr)
    lr = mag * jnp.cos(zi)
    li = mag * jnp.sin(zi)
    den = ar * ar + ai * ai
    fr = ((lr - 1.0) * ar + li * ai) / den
    fi = (li * ar - (lr - 1.0) * ai) / den
    br = b_re.astype(f32)
    bi = b_im.astype(f32)
    bbr = fr[..., None] * br - fi[..., None] * bi
    bbi = fr[..., None] * bi + fi[..., None] * br
    cr = c_re.astype(f32)
    ci = c_im.astype(f32)

    tau = jnp.arange(CHUNK + 1, dtype=f32)[:, None, None, None]
    pmag = jnp.exp(tau * zr[None])
    pr = pmag * jnp.cos(tau * zi[None])
    pi = pmag * jnp.sin(tau * zi[None])

    er = pr[..., None] * bbr[None] - pi[..., None] * bbi[None]
    ei = pr[..., None] * bbi[None] + pi[..., None] * bbr[None]
    gr = cr[None] * pr[:, :, :, None, :] - ci[None] * pi[:, :, :, None, :]
    gi = cr[None] * pi[:, :, :, None, :] + ci[None] * pr[:, :, :, None, :]

    kk = (jnp.einsum('dgcp,tdgpe->tdgce', cr, er, precision=HIGHEST)
          - jnp.einsum('dgcp,tdgpe->tdgce', ci, ei, precision=HIGHEST))
    kf = kk[:CHUNK, 0]
    kb = kk[:CHUNK, 1]
    dmat = d.astype(f32).reshape(N_GROUPS, SSM_CH)[:, :, None] * jnp.eye(SSM_CH, dtype=f32)[None]
    center = kf[0] + kb[0] + dmat
    kall = jnp.concatenate([kb[:0:-1], center[None], kf[1:]], axis=0)
    jj = jnp.arange(CHUNK)
    lag_idx = jj[None, :] - jj[:, None] + (CHUNK - 1)
    m5 = kall[lag_idx]
    m_intra = jnp.transpose(m5, (2, 0, 4, 1, 3)).reshape(N_GROUPS, CHUNK_VEC, CHUNK_VEC)

    ef_r = er[:CHUNK, 0][::-1]
    ef_i = ei[:CHUNK, 0][::-1]
    eb_r = er[:CHUNK, 1]
    eb_i = ei[:CHUNK, 1]

    def _rows(e):
        return jnp.transpose(e, (1, 0, 3, 2)).reshape(N_GROUPS, CHUNK_VEC, SSM_STATE)

    w_state = jnp.concatenate([_rows(ef_r), _rows(ef_i), _rows(ef_i), _rows(ef_r),
                               _rows(eb_r), _rows(eb_i), _rows(eb_i), _rows(eb_r)], axis=2)

    def _cols(g):
        return jnp.transpose(g, (1, 3, 0, 2)).reshape(N_GROUPS, SSM_STATE, CHUNK_VEC)

    gf_r = gr[1:, 0]
    gf_i = gi[1:, 0]
    gb_r = gr[1:, 1][::-1]
    gb_i = gi[1:, 1][::-1]
    w_out = jnp.concatenate([_cols(gf_r), -_cols(gf_i), _cols(gb_r), -_cols(gb_i)], axis=1)

    a_r = pr[CHUNK]
    a_i = pi[CHUNK]
    a1 = jnp.concatenate([a_r, a_r], axis=-1)
    a2 = jnp.concatenate([-a_i, a_i], axis=-1)
    a3 = -a2
    zero = jnp.zeros_like(a1[0])
    coef = jnp.stack([a1[0], a2[0], a3[0], a1[1], a2[1], a3[1], zero, zero], axis=1)
    return (m_intra.astype(jnp.bfloat16), w_state.astype(jnp.bfloat16),
            w_out.astype(jnp.bfloat16), coef)


def _rope_tables():
    half = HEAD_DIM // 2
    pos = jnp.arange(SEQ, dtype=jnp.int32)
    inv_freq = ROPE_THETA ** (-jnp.arange(half, dtype=jnp.float32) / half)
    ang = pos.astype(jnp.float32)[:, None] * inv_freq[None, :]
    cos = jnp.cos(ang)
    sin = jnp.sin(ang)
    return jnp.concatenate([cos, cos, cos, cos, -sin, sin, -sin, sin], axis=1)


def _rope_block(xb, cos_t, sin_t, first_half):
    swapped = jnp.where(first_half, pltpu.roll(xb, 96, 1), pltpu.roll(xb, 32, 1))
    return xb * cos_t + swapped * sin_t


def _inproj_kernel(x_ref, wq_ref, wkv_ref, wz_ref, wu_ref, rope_ref,
                   q_ref, kv_ref, z_ref, ut_ref, uslab_ref):
    step = pl.program_id(0)
    part = step % STEPS_PER_REGROUP
    lane = lax.broadcasted_iota(jnp.int32, (SUB_ROWS, LANES), 1)
    first_half = (lane % HEAD_DIM) < (HEAD_DIM // 2)

    for sub in range(TOK_BLOCK // SUB_ROWS):
        rows = slice(sub * SUB_ROWS, (sub + 1) * SUB_ROWS)
        xb = x_ref[rows, :].astype(jnp.bfloat16)
        cos_t = rope_ref[rows, :LANES]
        sin_t = rope_ref[rows, LANES:]
        q = jnp.dot(xb, wq_ref[...], preferred_element_type=jnp.float32)
        q_blocks = [_rope_block(q[:, v * LANES:(v + 1) * LANES], cos_t, sin_t, first_half)
                    for v in range(Q_COLS // LANES)]
        q_ref[rows, :] = jnp.concatenate(q_blocks, axis=1).astype(q_ref.dtype)
        kv = jnp.dot(xb, wkv_ref[...], preferred_element_type=jnp.float32)
        k_rot = _rope_block(kv[:, :LANES], cos_t, sin_t, first_half)
        kv_ref[rows, :] = jnp.concatenate([k_rot, kv[:, LANES:]], axis=1).astype(kv_ref.dtype)
        z_ref[rows, :] = jnp.dot(xb, wz_ref[...], preferred_element_type=jnp.float32).astype(z_ref.dtype)
        u = jnp.dot(xb, wu_ref[...], preferred_element_type=jnp.float32)
        for v in range(N_LANE_BLOCKS):
            dst = pl.ds(pl.multiple_of(part * TOK_BLOCK + sub * SUB_ROWS, SUB_ROWS), SUB_ROWS)
            uslab_ref[v, dst, :] = u[:, v * LANES:(v + 1) * LANES]

    @pl.when(part == STEPS_PER_REGROUP - 1)
    def _():
        for j in range(CHUNK):
            for v in range(N_LANE_BLOCKS):
                uj = uslab_ref[v, pl.ds(j, LANES, stride=CHUNK), :]
                ujt = jnp.transpose(uj).reshape(GROUPS_PER_LANE_BLOCK, SSM_CH, LANES)
                g0 = v * GROUPS_PER_LANE_BLOCK
                ut_ref[g0:g0 + GROUPS_PER_LANE_BLOCK, j * SSM_CH:(j + 1) * SSM_CH, :] = ujt


def _inproj(x2, wq, wkv, wz, wu, rope):
    n_steps = BATCH * SEQ // TOK_BLOCK
    blocks_per_seq = SEQ // TOK_BLOCK

    def tok(cols):
        return pl.BlockSpec((TOK_BLOCK, cols), lambda s: (s, 0))

    def whole(shape):
        return pl.BlockSpec(shape, lambda s: (0,) * len(shape))

    return pl.pallas_call(
        _inproj_kernel,
        grid=(n_steps,),
        in_specs=[tok(D_MODEL), whole(wq.shape), whole(wkv.shape), whole(wz.shape), whole(wu.shape),
                  pl.BlockSpec((TOK_BLOCK, 2 * LANES), lambda s: (s % blocks_per_seq, 0))],
        out_specs=[tok(Q_COLS), tok(2 * KV_COLS), tok(D_ATTN + D_SSM),
                   pl.BlockSpec((N_GROUPS, CHUNK_VEC, LANES), lambda s: (0, 0, s // STEPS_PER_REGROUP))],
        out_shape=[jax.ShapeDtypeStruct((BATCH * SEQ, Q_COLS), jnp.bfloat16),
                   jax.ShapeDtypeStruct((BATCH * SEQ, 2 * KV_COLS), jnp.bfloat16),
                   jax.ShapeDtypeStruct((BATCH * SEQ, D_ATTN + D_SSM), jnp.bfloat16),
                   jax.ShapeDtypeStruct((N_GROUPS, CHUNK_VEC, ROWS), jnp.float32)],
        scratch_shapes=[pltpu.VMEM((N_LANE_BLOCKS, REGROUP_TOK, LANES), jnp.float32)],
        compiler_params=pltpu.CompilerParams(dimension_semantics=("arbitrary",),
                                             vmem_limit_bytes=VMEM_LIMIT),
        name="inproj",
    )(x2, wq, wkv, wz, wu, rope)


ATTN_SUB = 4
ATTN_STEP = ATTN_SUB * BLOCK
N_KEYS = 3 * BLOCK


def _attn_kernel(q_ref, kp_ref, kc_ref, kn_ref, sink_ref, y_ref):
    i = pl.program_id(1)
    kvw = jnp.concatenate([kp_ref[...], kc_ref[...], kn_ref[...]], axis=0).astype(jnp.float32)

    lane_w = lax.broadcasted_iota(jnp.int32, (kvw.shape[0], LANES), 1)
    k_lo, k_hi, v_lo, v_hi = [], [], [], []
    for hk in range(N_KV_HEADS):
        in_head = (lane_w >= hk * HEAD_DIM) & (lane_w < (hk + 1) * HEAD_DIM)
        k_nat = jnp.where(in_head, kvw[:, :LANES], 0.0)
        v_nat = jnp.where(in_head, kvw[:, LANES:], 0.0)
        k_oth = pltpu.roll(k_nat, HEAD_DIM, 1)
        v_oth = pltpu.roll(v_nat, HEAD_DIM, 1)
        pairs = ((k_nat, k_oth), (v_nat, v_oth)) if hk == 0 else ((k_oth, k_nat), (v_oth, v_nat))
        k_lo.append(pairs[0][0].astype(jnp.bfloat16))
        k_hi.append(pairs[0][1].astype(jnp.bfloat16))
        v_lo.append(pairs[1][0].astype(jnp.bfloat16))
        v_hi.append(pairs[1][1].astype(jnp.bfloat16))

    r = lax.broadcasted_iota(jnp.int32, (BLOCK, N_KEYS), 0)
    c = lax.broadcasted_iota(jnp.int32, (BLOCK, N_KEYS), 1)
    band = (c >= r) & (c - r <= 2 * WINDOW)
    lane_o = lax.broadcasted_iota(jnp.int32, (2 * BLOCK, LANES), 1)
    n_blocks = SEQ // BLOCK

    for t in range(ATTN_SUB):
        blk = i * ATTN_SUB + t
        lo = jnp.where(blk == 0, BLOCK, 0)
        hi = jnp.where(blk == n_blocks - 1, 2 * BLOCK, N_KEYS)
        valid1 = band & (c >= lo) & (c < hi)
        valid = jnp.concatenate([valid1, valid1], axis=0)
        win = slice(t * BLOCK, t * BLOCK + N_KEYS)
        q = q_ref[t * BLOCK:(t + 1) * BLOCK, :].astype(jnp.float32) * (HEAD_DIM ** -0.5)
        for hk in range(N_KV_HEADS):
            qq = jnp.concatenate([q[:, (2 * hk) * LANES:(2 * hk + 1) * LANES],
                                  q[:, (2 * hk + 1) * LANES:(2 * hk + 2) * LANES]], axis=0).astype(jnp.bfloat16)
            kcat = jnp.concatenate([k_lo[hk][win], k_hi[hk][win]], axis=0)
            vcat = jnp.concatenate([v_lo[hk][win], v_hi[hk][win]], axis=0)
            s = lax.dot_general(qq, kcat, (((1,), (1,)), ((), ())),
                                preferred_element_type=jnp.float32)
            ps, inv = [], []
            for par in range(2):
                sp = jnp.where(valid, s[:, par * N_KEYS:(par + 1) * N_KEYS], NEG_INF)
                h0 = 4 * hk + par
                h1 = h0 + 2
                sink = jnp.concatenate([jnp.broadcast_to(sink_ref[h0:h0 + 1, 0:1], (BLOCK, 1)),
                                        jnp.broadcast_to(sink_ref[h1:h1 + 1, 0:1], (BLOCK, 1))], axis=0)
                m = jnp.maximum(jnp.max(sp, axis=-1, keepdims=True), sink)
                p = jnp.exp(sp - m)
                denom = jnp.sum(p, axis=-1, keepdims=True) + jnp.exp(sink - m)
                ps.append(p.astype(jnp.bfloat16))
                inv.append(1.0 / denom)
            o = jnp.dot(jnp.concatenate(ps, axis=1), vcat, preferred_element_type=jnp.float32)
            o = o * jnp.where(lane_o < HEAD_DIM, inv[0], inv[1])
            rows = slice(t * BLOCK, (t + 1) * BLOCK)
            y_ref[rows, (2 * hk) * LANES:(2 * hk + 1) * LANES] = o[:BLOCK].astype(y_ref.dtype)
            y_ref[rows, (2 * hk + 1) * LANES:(2 * hk + 2) * LANES] = o[BLOCK:].astype(y_ref.dtype)


def _attention(q, kv, sink_tab):
    nb = SEQ // BLOCK
    return pl.pallas_call(
        _attn_kernel,
        grid=(BATCH, SEQ // ATTN_STEP),
        in_specs=[pl.BlockSpec((None, ATTN_STEP, Q_COLS), lambda b, i: (b, i, 0)),
                  pl.BlockSpec((None, BLOCK, 2 * KV_COLS),
                               lambda b, i: (b, jnp.maximum(i * ATTN_SUB - 1, 0), 0)),
                  pl.BlockSpec((None, ATTN_STEP, 2 * KV_COLS), lambda b, i: (b, i, 0)),
                  pl.BlockSpec((None, BLOCK, 2 * KV_COLS),
                               lambda b, i: (b, jnp.minimum((i + 1) * ATTN_SUB, nb - 1), 0)),
                  pl.BlockSpec((N_Q_HEADS, LANES), lambda b, i: (0, 0))],
        out_specs=pl.BlockSpec((None, ATTN_STEP, D_ATTN), lambda b, i: (b, i, 0)),
        out_shape=jax.ShapeDtypeStruct((BATCH, SEQ, D_ATTN), jnp.bfloat16),
        compiler_params=pltpu.CompilerParams(dimension_semantics=("arbitrary", "arbitrary"),
                                             vmem_limit_bytes=VMEM_LIMIT),
        name="attention",
    )(q, kv, kv, kv, sink_tab)


def _ssm_kernel(ut_ref, m_ref, ws_ref, wo_ref, coef_ref, yt_ref,
                sf_ref, sfs_ref, sb_ref, sbs_ref, xf_ref, xb_ref):
    u = jnp.transpose(ut_ref[...]).astype(jnp.bfloat16)
    y = jnp.dot(u, m_ref[...], preferred_element_type=jnp.float32)
    s = jnp.dot(u, ws_ref[...], preferred_element_type=jnp.float32)

    for b in range(BATCH):
        rows = slice(b * N_CHUNKS, (b + 1) * N_CHUNKS)
        dst = pl.ds(b * SLAB_PITCH + 8, N_CHUNKS)
        sf_ref[dst, :] = s[rows, 0 * LANES:1 * LANES]
        sfs_ref[dst, :] = s[rows, 1 * LANES:2 * LANES]
        sb_ref[dst, :] = s[rows, 2 * LANES:3 * LANES]
        sbs_ref[dst, :] = s[rows, 3 * LANES:4 * LANES]

    def bcast(row):
        return jnp.broadcast_to(coef_ref[row:row + 1, :], (BATCH, LANES))

    a1f, a2f, a3f, a1b, a2b, a3b = (bcast(i) for i in range(6))
    zero = jnp.zeros((BATCH, LANES), jnp.float32)
    xf_ref[pl.ds(8, BATCH, stride=SLAB_PITCH), :] = zero
    xb_ref[pl.ds(8 + N_CHUNKS - 1, BATCH, stride=SLAB_PITCH), :] = zero

    def step(k, carry):
        xf, xfs, xb, xbs = carry
        sf = sf_ref[pl.ds(8 + k, BATCH, stride=SLAB_PITCH), :]
        sfs = sfs_ref[pl.ds(8 + k, BATCH, stride=SLAB_PITCH), :]
        nxf = a1f * xf + a2f * xfs + sf
        nxfs = a1f * xfs + a3f * xf + sfs
        xf_ref[pl.ds(9 + k, BATCH, stride=SLAB_PITCH), :] = nxf
        kb = N_CHUNKS - 1 - k
        sb = sb_ref[pl.ds(8 + kb, BATCH, stride=SLAB_PITCH), :]
        sbs = sbs_ref[pl.ds(8 + kb, BATCH, stride=SLAB_PITCH), :]
        nxb = a1b * xb + a2b * xbs + sb
        nxbs = a1b * xbs + a3b * xb + sbs
        xb_ref[pl.ds(7 + kb, BATCH, stride=SLAB_PITCH), :] = nxb
        return nxf, nxfs, nxb, nxbs

    lax.fori_loop(0, N_CHUNKS, step, (zero, zero, zero, zero))

    xin_f = jnp.concatenate([xf_ref[pl.ds(b * SLAB_PITCH + 8, N_CHUNKS), :] for b in range(BATCH)], axis=0)
    xin_b = jnp.concatenate([xb_ref[pl.ds(b * SLAB_PITCH + 8, N_CHUNKS), :] for b in range(BATCH)], axis=0)
    xin = jnp.concatenate([xin_f, xin_b], axis=1).astype(jnp.bfloat16)
    y = y + jnp.dot(xin, wo_ref[...], preferred_element_type=jnp.float32)
    yt_ref[...] = jnp.transpose(y)


def _ssm(ut, m_intra, w_state, w_out, coef):
    def per_group(shape):
        return pl.BlockSpec((None,) + shape, lambda g: (g, 0, 0))

    slab = pltpu.VMEM((SLAB_ROWS, LANES), jnp.float32)
    return pl.pallas_call(
        _ssm_kernel,
        grid=(N_GROUPS,),
        in_specs=[per_group((CHUNK_VEC, ROWS)), per_group((CHUNK_VEC, CHUNK_VEC)),
                  per_group((CHUNK_VEC, 4 * LANES)), per_group((2 * STATE_VEC, CHUNK_VEC)),
                  per_group((8, LANES))],
        out_specs=per_group((CHUNK_VEC, ROWS)),
        out_shape=jax.ShapeDtypeStruct((N_GROUPS, CHUNK_VEC, ROWS), jnp.float32),
        scratch_shapes=[slab] * 6,
        compiler_params=pltpu.CompilerParams(dimension_semantics=("arbitrary",),
                                             vmem_limit_bytes=VMEM_LIMIT),
        name="ssm",
    )(ut, m_intra, w_state, w_out, coef)


def _rms(v, gain):
    return v * lax.rsqrt(jnp.mean(v * v, axis=-1, keepdims=True) + NORM_EPS) * gain


def _final_kernel(x_ref, ya_ref, z_ref, yt_ref, wglu_ref, bglu_ref, ga_ref, gs_ref,
                  woa_ref, wos_ref, lng_ref, lnb_ref, o_ref, yslab_ref):
    step = pl.program_id(0)
    part = step % STEPS_PER_REGROUP

    @pl.when(part == 0)
    def _():
        for j in range(CHUNK):
            for v in range(N_LANE_BLOCKS):
                g0 = v * GROUPS_PER_LANE_BLOCK
                yjt = yt_ref[g0:g0 + GROUPS_PER_LANE_BLOCK, j * SSM_CH:(j + 1) * SSM_CH, :]
                yj = jnp.transpose(yjt.reshape(LANES, LANES))
                yslab_ref[v, pl.ds(j, LANES, stride=CHUNK), :] = yj

    for sub in range(TOK_BLOCK // SUB_ROWS):
        rows = slice(sub * SUB_ROWS, (sub + 1) * SUB_ROWS)
        z = z_ref[rows, :].astype(jnp.float32)
        ya = ya_ref[rows, :].astype(jnp.float32) * jax.nn.silu(z[:, :D_ATTN])
        ya = _rms(ya, ga_ref[...])

        src = pl.ds(pl.multiple_of(part * TOK_BLOCK + sub * SUB_ROWS, SUB_ROWS), SUB_ROWS)
        ys = jnp.concatenate([yslab_ref[v, src, :] for v in range(N_LANE_BLOCKS)], axis=1)
        g = jax.nn.gelu(ys)
        gate = jnp.dot(g.astype(jnp.bfloat16), wglu_ref[...], preferred_element_type=jnp.float32) + bglu_ref[...]
        ys = g * jax.nn.sigmoid(gate)
        ys = ys * jax.nn.silu(z[:, D_ATTN:])
        ys = _rms(ys, gs_ref[...])

        out = (jnp.dot(ya.astype(jnp.bfloat16), woa_ref[...], preferred_element_type=jnp.float32)
               + jnp.dot(ys.astype(jnp.bfloat16), wos_ref[...], preferred_element_type=jnp.float32))
        h = DEEPNORM_ALPHA * x_ref[rows, :] + out
        mu = jnp.mean(h, axis=-1, keepdims=True)
        hc = h - mu
        var = jnp.mean(hc * hc, axis=-1, keepdims=True)
        o_ref[rows, :] = hc * lax.rsqrt(var + NORM_EPS) * lng_ref[...] + lnb_ref[...]


def _final(x2, ya, z, yt, wglu, bglu, ga, gs, woa, wos, lng, lnb):
    n_steps = BATCH * SEQ // TOK_BLOCK

    def tok(cols):
        return pl.BlockSpec((TOK_BLOCK, cols), lambda s: (s, 0))

    def whole(shape):
        return pl.BlockSpec(shape, lambda s: (0,) * len(shape))

    return pl.pallas_call(
        _final_kernel,
        grid=(n_steps,),
        in_specs=[tok(D_MODEL), tok(D_ATTN), tok(D_ATTN + D_SSM),
                  pl.BlockSpec((N_GROUPS, CHUNK_VEC, LANES), lambda s: (0, 0, s // STEPS_PER_REGROUP)),
                  whole(wglu.shape), whole(bglu.shape), whole(ga.shape), whole(gs.shape),
                  whole(woa.shape), whole(wos.shape), whole(lng.shape), whole(lnb.shape)],
        out_specs=tok(D_MODEL),
        out_shape=jax.ShapeDtypeStruct((BATCH * SEQ, D_MODEL), jnp.float32),
        scratch_shapes=[pltpu.VMEM((N_LANE_BLOCKS, REGROUP_TOK, LANES), jnp.float32)],
        compiler_params=pltpu.CompilerParams(dimension_semantics=("arbitrary",),
                                             vmem_limit_bytes=VMEM_LIMIT),
        name="final",
    )(x2, ya, z, yt, wglu, bglu, ga, gs, woa, wos, lng, lnb)


def kernel(x, w_in, attn_sink, ssm_a_re, ssm_a_im, ssm_log_dt, ssm_b_re, ssm_b_im, ssm_c_re, ssm_c_im,
           ssm_d, w_glu, b_glu, norm_attn_g, norm_ssm_g, w_out, ln_g, ln_b):
    assert x.shape == (BATCH, SEQ, D_MODEL) and w_in.shape[0] == 1
    bf16 = jnp.bfloat16
    w = w_in[0]
    c0 = Q_COLS
    c1 = c0 + 2 * KV_COLS
    c2 = c1 + D_ATTN
    c3 = c2 + D_SSM
    wq = w[:, :c0].astype(bf16)
    wkv = w[:, c0:c1].astype(bf16)
    wz = jnp.concatenate([w[:, c1:c2], w[:, c3:]], axis=1).astype(bf16)
    wu = w[:, c2:c3].astype(bf16)

    m_intra, w_state, w_so, coef = _ssm_tables(ssm_a_re[0], ssm_a_im[0], ssm_log_dt[0], ssm_b_re[0],
                                                ssm_b_im[0], ssm_c_re[0], ssm_c_im[0], ssm_d[0])
    rope = _rope_tables()
    sink_tab = jnp.broadcast_to(attn_sink[0].astype(jnp.float32)[:, None], (N_Q_HEADS, LANES))

    x2 = x.reshape(BATCH * SEQ, D_MODEL)
    q, kv, z, ut = _inproj(x2, wq, wkv, wz, wu, rope)
    ya = _attention(q.reshape(BATCH, SEQ, Q_COLS), kv.reshape(BATCH, SEQ, 2 * KV_COLS), sink_tab)
    yt = _ssm(ut, m_intra, w_state, w_so, coef)

    row = lambda v: v[0].astype(jnp.float32)[None, :]
    out = _final(x2, ya.reshape(BATCH * SEQ, D_ATTN), z, yt,
                 w_glu[0].astype(bf16), row(b_glu), row(norm_attn_g), row(norm_ssm_g),
                 w_out[0, :D_ATTN].astype(bf16), w_out[0, D_ATTN:].astype(bf16), row(ln_g), row(ln_b))
    return out.reshape(BATCH, SEQ, D_MODEL)
```

```python
import jax
import jax.numpy as jnp
from jax import lax
from jax.experimental import pallas as pl
from jax.experimental.pallas import tpu as pltpu

D_MODEL = 1024
BATCH = 4
SEQ = 4096
D_ATTN = 512
D_SSM = 512
HEAD_DIM = 64
N_Q_HEADS = 8
N_KV_HEADS = 2
WINDOW = 128
BLOCK = 128
ROPE_THETA = 10000.0
SSM_CH = 16
N_GROUPS = 32
SSM_STATE = 64
NORM_EPS = 1e-5
NEG_INF = -1e30
DEEPNORM_ALPHA = 2.0 ** 0.25

Q_COLS = N_Q_HEADS * HEAD_DIM
KV_COLS = N_KV_HEADS * HEAD_DIM
CHUNK = 16
N_CHUNKS = SEQ // CHUNK
CHUNK_VEC = CHUNK * SSM_CH
STATE_VEC = 2 * SSM_STATE
LANES = 128
ROWS = BATCH * N_CHUNKS
SLAB_PITCH = N_CHUNKS + 8
SLAB_ROWS = BATCH * SLAB_PITCH + 8
TOK_BLOCK = 1024
REGROUP_TOK = LANES * CHUNK
STEPS_PER_REGROUP = REGROUP_TOK // TOK_BLOCK
SUB_ROWS = 512
N_LANE_BLOCKS = D_SSM // LANES
GROUPS_PER_LANE_BLOCK = LANES // SSM_CH
VMEM_LIMIT = 56 * 1024 * 1024
HIGHEST = lax.Precision.HIGHEST


LAG_ROWS = 512
TAU_ROWS = 24


def _prep_kernel(arow_ref, bt_ref, c_ref, dt_ref, m_ref, ws_ref, wo_ref, coef_ref):
    f32 = jnp.float32
    lo16 = lax.broadcasted_iota(jnp.int32, (SSM_CH, LANES), 1) < SSM_STATE
    lo24 = lax.broadcasted_iota(jnp.int32, (TAU_ROWS, LANES), 1) < SSM_STATE
    tau = lax.broadcasted_iota(jnp.int32, (TAU_ROWS, LANES), 0).astype(f32)
    arow = arow_ref[...]

    lag_tabs, kws, ws_parts, coef_rows = [], [], [], []
    for d in range(2):
        ar = arow[3 * d:3 * d + 1, :]
        ai = arow[3 * d + 1:3 * d + 2, :]
        dt = jnp.exp(arow[3 * d + 2:3 * d + 3, :])
        zr = dt * ar
        zi = dt * ai
        mag = jnp.exp(zr)
        lr = mag * jnp.cos(zi)
        li = mag * jnp.sin(zi)
        den = ar * ar + ai * ai
        fr = ((lr - 1.0) * ar + li * ai) / den
        fi = (li * ar - (lr - 1.0) * ai) / den
        btr = bt_ref[0, d]
        bti = bt_ref[1, d]
        bbr = fr * btr - fi * bti
        bbi = fr * bti + fi * btr
        cr = c_ref[0, d]
        ci = c_ref[1, d]

        pmag = jnp.exp(tau * zr)
        pr = pmag * jnp.cos(tau * zi)
        pi = pmag * jnp.sin(tau * zi)
        pa = jnp.where(lo24, pr, pi)
        pb = jnp.where(lo24, -pi, pr)
        pc = jnp.where(lo24, pi, pr)
        pd = jnp.where(lo24, pr, -pi)

        def row(t, tab):
            return tab[t:t + 1, :]

        order = range(CHUNK + 1) if d == 0 else range(CHUNK, -1, -1)
        blocks = [cr * row(t, pa) + ci * row(t, pb) for t in order]
        blocks.append(jnp.zeros((LAG_ROWS - (CHUNK + 1) * SSM_CH, LANES), f32))
        lag_tab = jnp.concatenate(blocks, axis=0)
        lag_tabs.append(lag_tab)
        lhs = jnp.where(lo16, bbr, -bbi)
        kws.append(lax.dot_general(lhs, lag_tab, (((1,), (1,)), ((), ())),
                                   precision=HIGHEST, preferred_element_type=f32))

        parts = []
        for j in range(CHUNK):
            t = CHUNK - 1 - j if d == 0 else j
            parts.append(jnp.concatenate([bbr * row(t, pa) + bbi * row(t, pb),
                                          bbr * row(t, pc) + bbi * row(t, pd)], axis=1))
        ws_parts.append(jnp.concatenate(parts, axis=0))

        a1 = row(CHUNK, pr)
        a2 = jnp.where(lo24, -pi, pi)[CHUNK:CHUNK + 1, :]
        coef_rows += [a1, a2, -a2]

    ws_ref[...] = jnp.concatenate(ws_parts, axis=1).astype(ws_ref.dtype)
    coef_ref[...] = jnp.concatenate(coef_rows + [jnp.zeros((2, LANES), f32)], axis=0)

    sign = jnp.where(lax.broadcasted_iota(jnp.int32, (LANES, CHUNK_VEC), 0) < SSM_STATE, 1.0, -1.0)
    wo_f = jnp.transpose(lag_tabs[0][SSM_CH:SSM_CH + CHUNK_VEC, :]) * sign
    wo_b = jnp.transpose(lag_tabs[1][:CHUNK_VEC, :]) * sign
    wo_ref[...] = jnp.concatenate([wo_f, wo_b], axis=0).astype(wo_ref.dtype)

    sub = lax.broadcasted_iota(jnp.int32, (SSM_CH, CHUNK_VEC), 0)
    lane = lax.broadcasted_iota(jnp.int32, (SSM_CH, CHUNK_VEC), 1)
    dtile = jnp.broadcast_to(dt_ref[...], (SSM_CH, CHUNK_VEC))
    for jp in range(CHUNK):
        fwd = kws[0] if jp == 0 else pltpu.roll(kws[0], SSM_CH * jp, 1)
        bwd = pltpu.roll(kws[1], (LAG_ROWS - SSM_CH * (CHUNK - jp)) % LAG_ROWS, 1)
        skip = jnp.where((lane // SSM_CH == jp) & (lane % SSM_CH == sub), dtile, 0.0)
        blk = fwd[:, :CHUNK_VEC] + bwd[:, :CHUNK_VEC] + skip
        m_ref[jp * SSM_CH:(jp + 1) * SSM_CH, :] = blk.astype(m_ref.dtype)


def _ssm_tables(a_re, a_im, log_dt, b_re, b_im, c_re, c_im, d):
    f32 = jnp.float32
    dup = lambda v: jnp.concatenate([v, v], axis=-1)
    ldt = jnp.broadcast_to(log_dt.astype(f32)[..., None], a_re.shape)
    zero = jnp.zeros_like(ldt[0])
    arow = dup(jnp.stack([a_re[0], a_im[0], ldt[0], a_re[1], a_im[1], ldt[1], zero, zero], axis=1).astype(f32))
    bt = dup(jnp.swapaxes(jnp.stack([b_re, b_im]).astype(f32), -1, -2))
    cc = dup(jnp.stack([c_re, c_im]).astype(f32))
    dtile = jnp.tile(d.astype(f32).reshape(N_GROUPS, 1, SSM_CH), (1, 1, CHUNK))

    bf16 = jnp.bfloat16
    return pl.pallas_call(
        _prep_kernel,
        grid=(N_GROUPS,),
        in_specs=[pl.BlockSpec((None, 8, LANES), lambda g: (g, 0, 0)),
                  pl.BlockSpec((2, 2, None, SSM_CH, LANES), lambda g: (0, 0, g, 0, 0)),
                  pl.BlockSpec((2, 2, None, SSM_CH, LANES), lambda g: (0, 0, g, 0, 0)),
                  pl.BlockSpec((None, 1, CHUNK_VEC), lambda g: (g, 0, 0))],
        out_specs=[pl.BlockSpec((None, CHUNK_VEC, CHUNK_VEC), lambda g: (g, 0, 0)),
                   pl.BlockSpec((None, CHUNK_VEC, 4 * LANES), lambda g: (g, 0, 0)),
                   pl.BlockSpec((None, 2 * STATE_VEC, CHUNK_VEC), lambda g: (g, 0, 0)),
                   pl.BlockSpec((None, 8, LANES), lambda g: (g, 0, 0))],
        out_shape=[jax.ShapeDtypeStruct((N_GROUPS, CHUNK_VEC, CHUNK_VEC), bf16),
                   jax.ShapeDtypeStruct((N_GROUPS, CHUNK_VEC, 4 * LANES), bf16),
                   jax.ShapeDtypeStruct((N_GROUPS, 2 * STATE_VEC, CHUNK_VEC), bf16),
                   jax.ShapeDtypeStruct((N_GROUPS, 8, LANES), f32)],
        compiler_params=pltpu.CompilerParams(dimension_semantics=("arbitrary",),
                                             vmem_limit_bytes=VMEM_LIMIT),
        name="ssm_tables",
    )(arow, bt, cc, dtile)


def _ssm_tables_xla(a_re, a_im, log_dt, b_re, b_im, c_re, c_im, d):
    f32 = jnp.float32
    dt = jnp.exp(log_dt.astype(f32))[..., None]
    ar = a_re.astype(f32)
    ai = a_im.astype(f32)
    zr = dt * ar
    zi = dt * ai
    mag = jnp.exp(zr)
    lr = mag * jnp.cos(zi)
    li = mag * jnp.sin(zi)
    den = ar * ar + ai * ai
    fr = ((lr - 1.0) * ar + li * ai) / den
    fi = (li * ar - (lr - 1.0) * ai) / den
    br = b_re.astype(f32)
    bi = b_im.astype(f32)
    bbr = fr[..., None] * br - fi[..., None] * bi
    bbi = fr[..., None] * bi + fi[..., None] * br
    cr = c_re.astype(f32)
    ci = c_im.astype(f32)

    tau = jnp.arange(CHUNK + 1, dtype=f32)[:, None, None, None]
    pmag = jnp.exp(tau * zr[None])
    pr = pmag * jnp.cos(tau * zi[None])
    pi = pmag * jnp.sin(tau * zi[None])

    er = pr[..., None] * bbr[None] - pi[..., None] * bbi[None]
    ei = pr[..., None] * bbi[None] + pi[..., None] * bbr[None]
    gr = cr[None] * pr[:, :, :, None, :] - ci[None] * pi[:, :, :, None, :]
    gi = cr[None] * pi[:, :, :, None, :] + ci[None] * pr[:, :, :, None, :]

    kk = (jnp.einsum('dgcp,tdgpe->tdgce', cr, er, precision=HIGHEST)
          - jnp.einsum('dgcp,tdgpe->tdgce', ci, ei, precision=HIGHEST))
    kf = kk[:CHUNK, 0]
    kb = kk[:CHUNK, 1]
    dmat = d.astype(f32).reshape(N_GROUPS, SSM_CH)[:, :, None] * jnp.eye(SSM_CH, dtype=f32)[None]
    center = kf[0] + kb[0] + dmat
    kall = jnp.concatenate([kb[:0:-1], center[None], kf[1:]], axis=0)
    jj = jnp.arange(CHUNK)
    lag_idx = jj[None, :] - jj[:, None] + (CHUNK - 1)
    m5 = kall[lag_idx]
    m_intra = jnp.transpose(m5, (2, 0, 4, 1, 3)).reshape(N_GROUPS, CHUNK_VEC, CHUNK_VEC)

    ef_r = er[:CHUNK, 0][::-1]
    ef_i = ei[:CHUNK, 0][::-1]
    eb_r = er[:CHUNK, 1]
    eb_i = ei[:CHUNK, 1]

    def _rows(e):
        return jnp.transpose(e, (1, 0, 3, 2)).reshape(N_GROUPS, CHUNK_VEC, SSM_STATE)

    w_state = jnp.concatenate([_rows(ef_r), _rows(ef_i), _rows(ef_i), _rows(ef_r),
                               _rows(eb_r), _rows(eb_i), _rows(eb_i), _rows(eb_r)], axis=2)

    def _cols(g):
        return jnp.transpose(g, (1, 3, 0, 2)).reshape(N_GROUPS, SSM_STATE, CHUNK_VEC)

    gf_r = gr[1:, 0]
    gf_i = gi[1:, 0]
    gb_r = gr[1:, 1][::-1]
    gb_i = gi[1:, 1][::-1]
    w_out = jnp.concatenate([_cols(gf_r), -_cols(gf_i), _cols(gb_r), -_cols(gb_i)], axis=1)

    a_r = pr[CHUNK]
    a_i = pi[CHUNK]
    a1 = jnp.concatenate([a_r, a_r], axis=-1)
    a2 = jnp.concatenate([-a_i, a_i], axis=-1)
    a3 = -a2
    zero = jnp.zeros_like(a1[0])
    coef = jnp.stack([a1[0], a2[0], a3[0], a1[1], a2[1], a3[1], zero, zero], axis=1)
    return (m_intra.astype(jnp.bfloat16), w_state.astype(jnp.bfloat16),
            w_out.astype(jnp.bfloat16), coef)


def _rope_tables():
    half = HEAD_DIM // 2
    pos = jnp.arange(SEQ, dtype=jnp.int32)
    inv_freq = ROPE_THETA ** (-jnp.arange(half, dtype=jnp.float32) / half)
    ang = pos.astype(jnp.float32)[:, None] * inv_freq[None, :]
    cos = jnp.cos(ang)
    sin = jnp.sin(ang)
    return jnp.concatenate([cos, cos, cos, cos, -sin, sin, -sin, sin], axis=1)


def _rope_block(xb, cos_t, sin_t, first_half):
    swapped = jnp.where(first_half, pltpu.roll(xb, 96, 1), pltpu.roll(xb, 32, 1))
    return xb * cos_t + swapped * sin_t


def _inproj_kernel(x_ref, wq_ref, wkv_ref, wz_ref, wu_ref, rope_ref,
                   q_ref, kv_ref, z_ref, ut_ref, uslab_ref):
    step = pl.program_id(0)
    part = step % STEPS_PER_REGROUP
    lane = lax.broadcasted_iota(jnp.int32, (SUB_ROWS, LANES), 1)
    first_half = (lane % HEAD_DIM) < (HEAD_DIM // 2)

    for sub in range(TOK_BLOCK // SUB_ROWS):
        rows = slice(sub * SUB_ROWS, (sub + 1) * SUB_ROWS)
        xb = x_ref[rows, :].astype(jnp.bfloat16)
        cos_t = rope_ref[rows, :LANES]
        sin_t = rope_ref[rows, LANES:]
        q = jnp.dot(xb, wq_ref[...], preferred_element_type=jnp.float32)
        q_blocks = [_rope_block(q[:, v * LANES:(v + 1) * LANES], cos_t, sin_t, first_half)
                    for v in range(Q_COLS // LANES)]
        q_ref[rows, :] = jnp.concatenate(q_blocks, axis=1).astype(q_ref.dtype)
        kv = jnp.dot(xb, wkv_ref[...], preferred_element_type=jnp.float32)
        k_rot = _rope_block(kv[:, :LANES], cos_t, sin_t, first_half)
        kv_ref[rows, :] = jnp.concatenate([k_rot, kv[:, LANES:]], axis=1).astype(kv_ref.dtype)
        z_ref[rows, :] = jnp.dot(xb, wz_ref[...], preferred_element_type=jnp.float32).astype(z_ref.dtype)
        u = jnp.dot(xb, wu_ref[...], preferred_element_type=jnp.float32)
        for v in range(N_LANE_BLOCKS):
            dst = pl.ds(pl.multiple_of(part * TOK_BLOCK + sub * SUB_ROWS, SUB_ROWS), SUB_ROWS)
            uslab_ref[v, dst, :] = u[:, v * LANES:(v + 1) * LANES]

    @pl.when(part == STEPS_PER_REGROUP - 1)
    def _():
        for j in range(CHUNK):
            for v in range(N_LANE_BLOCKS):
                uj = uslab_ref[v, pl.ds(j, LANES, stride=CHUNK), :]
                ujt = jnp.transpose(uj).reshape(GROUPS_PER_LANE_BLOCK, SSM_CH, LANES)
                g0 = v * GROUPS_PER_LANE_BLOCK
                ut_ref[g0:g0 + GROUPS_PER_LANE_BLOCK, j * SSM_CH:(j + 1) * SSM_CH, :] = ujt


def _inproj(x2, wq, wkv, wz, wu, rope):
    n_steps = BATCH * SEQ // TOK_BLOCK
    blocks_per_seq = SEQ // TOK_BLOCK

    def tok(cols):
        return pl.BlockSpec((TOK_BLOCK, cols), lambda s: (s, 0))

    def whole(shape):
        return pl.BlockSpec(shape, lambda s: (0,) * len(shape))

    return pl.pallas_call(
        _inproj_kernel,
        grid=(n_steps,),
        in_specs=[tok(D_MODEL), whole(wq.shape), whole(wkv.shape), whole(wz.shape), whole(wu.shape),
                  pl.BlockSpec((TOK_BLOCK, 2 * LANES), lambda s: (s % blocks_per_seq, 0))],
        out_specs=[tok(Q_COLS), tok(2 * KV_COLS), tok(D_ATTN + D_SSM),
                   pl.BlockSpec((N_GROUPS, CHUNK_VEC, LANES), lambda s: (0, 0, s // STEPS_PER_REGROUP))],
        out_shape=[jax.ShapeDtypeStruct((BATCH * SEQ, Q_COLS), jnp.bfloat16),
                   jax.ShapeDtypeStruct((BATCH * SEQ, 2 * KV_COLS), jnp.bfloat16),
                   jax.ShapeDtypeStruct((BATCH * SEQ, D_ATTN + D_SSM), jnp.bfloat16),
                   jax.ShapeDtypeStruct((N_GROUPS, CHUNK_VEC, ROWS), jnp.float32)],
        scratch_shapes=[pltpu.VMEM((N_LANE_BLOCKS, REGROUP_TOK, LANES), jnp.float32)],
        compiler_params=pltpu.CompilerParams(dimension_semantics=("arbitrary",),
                                             vmem_limit_bytes=VMEM_LIMIT),
        name="inproj",
    )(x2, wq, wkv, wz, wu, rope)


ATTN_SUB = 4
ATTN_STEP = ATTN_SUB * BLOCK
N_KEYS = 3 * BLOCK


def _attn_kernel(q_ref, kp_ref, kc_ref, kn_ref, sink_ref, y_ref):
    i = pl.program_id(1)
    kvw = jnp.concatenate([kp_ref[...], kc_ref[...], kn_ref[...]], axis=0).astype(jnp.float32)

    n_win = kvw.shape[0]
    lane_w = lax.broadcasted_iota(jnp.int32, (n_win, LANES), 1)
    k_lo, k_hi, vt_lo, vt_hi = [], [], [], []
    for hk in range(N_KV_HEADS):
        in_head = (lane_w >= hk * HEAD_DIM) & (lane_w < (hk + 1) * HEAD_DIM)
        k_nat = jnp.where(in_head, kvw[:, :LANES], 0.0)
        v_nat = jnp.where(in_head, kvw[:, LANES:], 0.0)
        k_oth = pltpu.roll(k_nat, HEAD_DIM, 1)
        v_oth = pltpu.roll(v_nat, HEAD_DIM, 1)
        pairs = ((k_nat, k_oth), (v_nat, v_oth)) if hk == 0 else ((k_oth, k_nat), (v_oth, v_nat))
        k_lo.append(pairs[0][0].astype(jnp.bfloat16))
        k_hi.append(pairs[0][1].astype(jnp.bfloat16))
        vt_lo.append(jnp.transpose(pairs[1][0]).astype(jnp.bfloat16))
        vt_hi.append(jnp.transpose(pairs[1][1]).astype(jnp.bfloat16))

    c = lax.broadcasted_iota(jnp.int32, (N_KEYS, BLOCK), 0)
    r = lax.broadcasted_iota(jnp.int32, (N_KEYS, BLOCK), 1)
    band = (c >= r) & (c - r <= 2 * WINDOW)
    ones_row = lax.broadcasted_iota(jnp.int32, (16, 2 * N_KEYS), 0)
    ones_col = lax.broadcasted_iota(jnp.int32, (16, 2 * N_KEYS), 1)
    ones = jnp.where(((ones_row == 0) & (ones_col < N_KEYS)) | ((ones_row == 1) & (ones_col >= N_KEYS)),
                     1.0, 0.0).astype(jnp.bfloat16)
    row_o = lax.broadcasted_iota(jnp.int32, (LANES, 2 * BLOCK), 0)
    n_blocks = SEQ // BLOCK

    def scores(t, hk):
        blk = i * ATTN_SUB + t
        lo = jnp.where(blk == 0, BLOCK, 0)
        hi = jnp.where(blk == n_blocks - 1, 2 * BLOCK, N_KEYS)
        valid1 = band & (c >= lo) & (c < hi)
        valid = jnp.concatenate([valid1, valid1], axis=1)
        win = slice(t * BLOCK, t * BLOCK + N_KEYS)
        q = q_ref[t * BLOCK:(t + 1) * BLOCK, (2 * hk) * LANES:(2 * hk + 2) * LANES]
        q = q.astype(jnp.float32) * (HEAD_DIM ** -0.5)
        qq = jnp.concatenate([q[:, :LANES], q[:, LANES:]], axis=0).astype(jnp.bfloat16)
        kcat = jnp.concatenate([k_lo[hk][win], k_hi[hk][win]], axis=0)
        st = lax.dot_general(kcat, qq, (((1,), (1,)), ((), ())),
                             preferred_element_type=jnp.float32)
        sps, ms, sinks = [], [], []
        for par in range(2):
            sp = jnp.where(valid, st[par * N_KEYS:(par + 1) * N_KEYS, :], NEG_INF)
            h0 = 4 * hk + par
            h1 = h0 + 2
            sink = jnp.concatenate([sink_ref[h0:h0 + 1, :], sink_ref[h1:h1 + 1, :]], axis=1)
            sps.append(sp)
            ms.append(jnp.maximum(jnp.max(sp, axis=0, keepdims=True), sink))
            sinks.append(sink)
        return sps, ms, sinks

    def weights(sps, ms, sinks):
        return [jnp.exp(sp - m).astype(jnp.bfloat16) for sp, m in zip(sps, ms)], ms, sinks

    def finish(t, hk, ps, ms, sinks):
        win = slice(t * BLOCK, t * BLOCK + N_KEYS)
        vt = jnp.concatenate([jnp.concatenate([vt_lo[hk][:, win], vt_hi[hk][:, win]], axis=1), ones], axis=0)
        ot = jnp.dot(vt, jnp.concatenate(ps, axis=0), preferred_element_type=jnp.float32)
        inv0 = 1.0 / (ot[LANES:LANES + 1, :] + jnp.exp(sinks[0] - ms[0]))
        inv1 = 1.0 / (ot[LANES + 1:LANES + 2, :] + jnp.exp(sinks[1] - ms[1]))
        o = jnp.transpose(ot[:LANES, :] * jnp.where(row_o < HEAD_DIM, inv0, inv1))
        rows = slice(t * BLOCK, (t + 1) * BLOCK)
        y_ref[rows, (2 * hk) * LANES:(2 * hk + 1) * LANES] = o[:BLOCK].astype(y_ref.dtype)
        y_ref[rows, (2 * hk + 1) * LANES:(2 * hk + 2) * LANES] = o[BLOCK:].astype(y_ref.dtype)

    work = [(t, hk) for t in range(ATTN_SUB) for hk in range(N_KV_HEADS)]
    stage_a = {0: scores(*work[0])}
    stage_b = {}
    for n in range(len(work) + 1):
        if n + 1 < len(work):
            stage_a[n + 1] = scores(*work[n + 1])
        if n < len(work):
            stage_b[n] = weights(*stage_a.pop(n))
        if n >= 1:
            finish(*work[n - 1], *stage_b.pop(n - 1))


def _attention(q, kv, sink_tab):
    nb = SEQ // BLOCK
    return pl.pallas_call(
        _attn_kernel,
        grid=(BATCH, SEQ // ATTN_STEP),
        in_specs=[pl.BlockSpec((None, ATTN_STEP, Q_COLS), lambda b, i: (b, i, 0)),
                  pl.BlockSpec((None, BLOCK, 2 * KV_COLS),
                               lambda b, i: (b, jnp.maximum(i * ATTN_SUB - 1, 0), 0)),
                  pl.BlockSpec((None, ATTN_STEP, 2 * KV_COLS), lambda b, i: (b, i, 0)),
                  pl.BlockSpec((None, BLOCK, 2 * KV_COLS),
                               lambda b, i: (b, jnp.minimum((i + 1) * ATTN_SUB, nb - 1), 0)),
                  pl.BlockSpec((N_Q_HEADS, LANES), lambda b, i: (0, 0))],
        out_specs=pl.BlockSpec((None, ATTN_STEP, D_ATTN), lambda b, i: (b, i, 0)),
        out_shape=jax.ShapeDtypeStruct((BATCH, SEQ, D_ATTN), jnp.bfloat16),
        compiler_params=pltpu.CompilerParams(dimension_semantics=("arbitrary", "arbitrary"),
                                             vmem_limit_bytes=VMEM_LIMIT),
        name="attention",
    )(q, kv, kv, kv, sink_tab)


def _ssm_kernel(ut_ref, m_ref, ws_ref, wo_ref, coef_ref, yt_ref,
                sf_ref, sfs_ref, sb_ref, sbs_ref, xf_ref, xb_ref):
    u = jnp.transpose(ut_ref[...]).astype(jnp.bfloat16)
    y = jnp.dot(u, m_ref[...], preferred_element_type=jnp.float32)
    s = jnp.dot(u, ws_ref[...], preferred_element_type=jnp.float32)

    for b in range(BATCH):
        rows = slice(b * N_CHUNKS, (b + 1) * N_CHUNKS)
        dst = pl.ds(b * SLAB_PITCH + 8, N_CHUNKS)
        sf_ref[dst, :] = s[rows, 0 * LANES:1 * LANES]
        sfs_ref[dst, :] = s[rows, 1 * LANES:2 * LANES]
        sb_ref[dst, :] = s[rows, 2 * LANES:3 * LANES]
        sbs_ref[dst, :] = s[rows, 3 * LANES:4 * LANES]

    def bcast(row):
        return jnp.broadcast_to(coef_ref[row:row + 1, :], (BATCH, LANES))

    a1f, a2f, a3f, a1b, a2b, a3b = (bcast(i) for i in range(6))
    zero = jnp.zeros((BATCH, LANES), jnp.float32)
    xf_ref[pl.ds(8, BATCH, stride=SLAB_PITCH), :] = zero
    xb_ref[pl.ds(8 + N_CHUNKS - 1, BATCH, stride=SLAB_PITCH), :] = zero

    def step(k, carry):
        xf, xfs, xb, xbs = carry
        sf = sf_ref[pl.ds(8 + k, BATCH, stride=SLAB_PITCH), :]
        sfs = sfs_ref[pl.ds(8 + k, BATCH, stride=SLAB_PITCH), :]
        nxf = a1f * xf + a2f * xfs + sf
        nxfs = a1f * xfs + a3f * xf + sfs
        xf_ref[pl.ds(9 + k, BATCH, stride=SLAB_PITCH), :] = nxf
        kb = N_CHUNKS - 1 - k
        sb = sb_ref[pl.ds(8 + kb, BATCH, stride=SLAB_PITCH), :]
        sbs = sbs_ref[pl.ds(8 + kb, BATCH, stride=SLAB_PITCH), :]
        nxb = a1b * xb + a2b * xbs + sb
        nxbs = a1b * xbs + a3b * xb + sbs
        xb_ref[pl.ds(7 + kb, BATCH, stride=SLAB_PITCH), :] = nxb
        return nxf, nxfs, nxb, nxbs

    lax.fori_loop(0, N_CHUNKS, step, (zero, zero, zero, zero))

    xin_f = jnp.concatenate([xf_ref[pl.ds(b * SLAB_PITCH + 8, N_CHUNKS), :] for b in range(BATCH)], axis=0)
    xin_b = jnp.concatenate([xb_ref[pl.ds(b * SLAB_PITCH + 8, N_CHUNKS), :] for b in range(BATCH)], axis=0)
    xin = jnp.concatenate([xin_f, xin_b], axis=1).astype(jnp.bfloat16)
    y = y + jnp.dot(xin, wo_ref[...], preferred_element_type=jnp.float32)
    yt_ref[...] = jnp.transpose(y)


def _ssm(ut, m_intra, w_state, w_out, coef):
    def per_group(shape):
        return pl.BlockSpec((None,) + shape, lambda g: (g, 0, 0))

    slab = pltpu.VMEM((SLAB_ROWS, LANES), jnp.float32)
    return pl.pallas_call(
        _ssm_kernel,
        grid=(N_GROUPS,),
        in_specs=[per_group((CHUNK_VEC, ROWS)), per_group((CHUNK_VEC, CHUNK_VEC)),
                  per_group((CHUNK_VEC, 4 * LANES)), per_group((2 * STATE_VEC, CHUNK_VEC)),
                  per_group((8, LANES))],
        out_specs=per_group((CHUNK_VEC, ROWS)),
        out_shape=jax.ShapeDtypeStruct((N_GROUPS, CHUNK_VEC, ROWS), jnp.float32),
        scratch_shapes=[slab] * 6,
        compiler_params=pltpu.CompilerParams(dimension_semantics=("arbitrary",),
                                             vmem_limit_bytes=VMEM_LIMIT),
        name="ssm",
    )(ut, m_intra, w_state, w_out, coef)


def _rms(v, gain):
    return v * lax.rsqrt(jnp.mean(v * v, axis=-1, keepdims=True) + NORM_EPS) * gain


def _final_kernel(x_ref, ya_ref, z_ref, yt_ref, wglu_ref, bglu_ref, ga_ref, gs_ref,
                  woa_ref, wos_ref, lng_ref, lnb_ref, o_ref, yslab_ref):
    step = pl.program_id(0)
    part = step % STEPS_PER_REGROUP

    @pl.when(part == 0)
    def _():
        for j in range(CHUNK):
            for v in range(N_LANE_BLOCKS):
                g0 = v * GROUPS_PER_LANE_BLOCK
                yjt = yt_ref[g0:g0 + GROUPS_PER_LANE_BLOCK, j * SSM_CH:(j + 1) * SSM_CH, :]
                yj = jnp.transpose(yjt.reshape(LANES, LANES))
                yslab_ref[v, pl.ds(j, LANES, stride=CHUNK), :] = yj

    for sub in range(TOK_BLOCK // SUB_ROWS):
        rows = slice(sub * SUB_ROWS, (sub + 1) * SUB_ROWS)
        z = z_ref[rows, :].astype(jnp.float32)
        ya = ya_ref[rows, :].astype(jnp.float32) * jax.nn.silu(z[:, :D_ATTN])
        ya = _rms(ya, ga_ref[...])

        src = pl.ds(pl.multiple_of(part * TOK_BLOCK + sub * SUB_ROWS, SUB_ROWS), SUB_ROWS)
        ys = jnp.concatenate([yslab_ref[v, src, :] for v in range(N_LANE_BLOCKS)], axis=1)
        g = jax.nn.gelu(ys)
        gate = jnp.dot(g.astype(jnp.bfloat16), wglu_ref[...], preferred_element_type=jnp.float32) + bglu_ref[...]
        ys = g * jax.nn.sigmoid(gate)
        ys = ys * jax.nn.silu(z[:, D_ATTN:])
        ys = _rms(ys, gs_ref[...])

        out = (jnp.dot(ya.astype(jnp.bfloat16), woa_ref[...], preferred_element_type=jnp.float32)
               + jnp.dot(ys.astype(jnp.bfloat16), wos_ref[...], preferred_element_type=jnp.float32))
        h = DEEPNORM_ALPHA * x_ref[rows, :] + out
        mu = jnp.mean(h, axis=-1, keepdims=True)
        hc = h - mu
        var = jnp.mean(hc * hc, axis=-1, keepdims=True)
        o_ref[rows, :] = hc * lax.rsqrt(var + NORM_EPS) * lng_ref[...] + lnb_ref[...]


def _final(x2, ya, z, yt, wglu, bglu, ga, gs, woa, wos, lng, lnb):
    n_steps = BATCH * SEQ // TOK_BLOCK

    def tok(cols):
        return pl.BlockSpec((TOK_BLOCK, cols), lambda s: (s, 0))

    def whole(shape):
        return pl.BlockSpec(shape, lambda s: (0,) * len(shape))

    return pl.pallas_call(
        _final_kernel,
        grid=(n_steps,),
        in_specs=[tok(D_MODEL), tok(D_ATTN), tok(D_ATTN + D_SSM),
                  pl.BlockSpec((N_GROUPS, CHUNK_VEC, LANES), lambda s: (0, 0, s // STEPS_PER_REGROUP)),
                  whole(wglu.shape), whole(bglu.shape), whole(ga.shape), whole(gs.shape),
                  whole(woa.shape), whole(wos.shape), whole(lng.shape), whole(lnb.shape)],
        out_specs=tok(D_MODEL),
        out_shape=jax.ShapeDtypeStruct((BATCH * SEQ, D_MODEL), jnp.float32),
        scratch_shapes=[pltpu.VMEM((N_LANE_BLOCKS, REGROUP_TOK, LANES), jnp.float32)],
        compiler_params=pltpu.CompilerParams(dimension_semantics=("arbitrary",),
                                             vmem_limit_bytes=VMEM_LIMIT),
        name="final",
    )(x2, ya, z, yt, wglu, bglu, ga, gs, woa, wos, lng, lnb)


def kernel(x, w_in, attn_sink, ssm_a_re, ssm_a_im, ssm_log_dt, ssm_b_re, ssm_b_im, ssm_c_re, ssm_c_im,
           ssm_d, w_glu, b_glu, norm_attn_g, norm_ssm_g, w_out, ln_g, ln_b):
    assert x.shape == (BATCH, SEQ, D_MODEL) and w_in.shape[0] == 1
    bf16 = jnp.bfloat16
    w = w_in[0]
    c0 = Q_COLS
    c1 = c0 + 2 * KV_COLS
    c2 = c1 + D_ATTN
    c3 = c2 + D_SSM
    wq = w[:, :c0].astype(bf16)
    wkv = w[:, c0:c1].astype(bf16)
    wz = jnp.concatenate([w[:, c1:c2], w[:, c3:]], axis=1).astype(bf16)
    wu = w[:, c2:c3].astype(bf16)

    m_intra, w_state, w_so, coef = _ssm_tables(ssm_a_re[0], ssm_a_im[0], ssm_log_dt[0], ssm_b_re[0],
                                                ssm_b_im[0], ssm_c_re[0], ssm_c_im[0], ssm_d[0])
    rope = _rope_tables()
    sink_tab = jnp.broadcast_to(attn_sink[0].astype(jnp.float32)[:, None], (N_Q_HEADS, LANES))

    x2 = x.reshape(BATCH * SEQ, D_MODEL)
    q, kv, z, ut = _inproj(x2, wq, wkv, wz, wu, rope)
    ya = _attention(q.reshape(BATCH, SEQ, Q_COLS), kv.reshape(BATCH, SEQ, 2 * KV_COLS), sink_tab)
    yt = _ssm(ut, m_intra, w_state, w_so, coef)

    row = lambda v: v[0].astype(jnp.float32)[None, :]
    out = _final(x2, ya.reshape(BATCH * SEQ, D_ATTN), z, yt,
                 w_glu[0].astype(bf16), row(b_glu), row(norm_attn_g), row(norm_ssm_g),
                 w_out[0, :D_ATTN].astype(bf16), w_out[0, D_ATTN:].astype(bf16), row(ln_g), row(ln_b))
    return out.reshape(BATCH, SEQ, D_MODEL)
```

```python
import jax
import jax.numpy as jnp
from jax import lax
from jax.experimental import pallas as pl
from jax.experimental.pallas import tpu as pltpu

D_MODEL = 1024
BATCH = 4
SEQ = 4096
D_ATTN = 512
D_SSM = 512
HEAD_DIM = 64
N_Q_HEADS = 8
N_KV_HEADS = 2
WINDOW = 128
BLOCK = 128
ROPE_THETA = 10000.0
SSM_CH = 16
N_GROUPS = 32
SSM_STATE = 64
NORM_EPS = 1e-5
NEG_INF = -1e30
DEEPNORM_ALPHA = 2.0 ** 0.25

Q_COLS = N_Q_HEADS * HEAD_DIM
KV_COLS = N_KV_HEADS * HEAD_DIM
CHUNK = 16
N_CHUNKS = SEQ // CHUNK
CHUNK_VEC = CHUNK * SSM_CH
STATE_VEC = 2 * SSM_STATE
LANES = 128
ROWS = BATCH * N_CHUNKS
SLAB_PITCH = N_CHUNKS + 8
SLAB_ROWS = BATCH * SLAB_PITCH + 8
TOK_BLOCK = 1024
REGROUP_TOK = LANES * CHUNK
STEPS_PER_REGROUP = REGROUP_TOK // TOK_BLOCK
SUB_ROWS = 512
N_LANE_BLOCKS = D_SSM // LANES
GROUPS_PER_LANE_BLOCK = LANES // SSM_CH
VMEM_LIMIT = 56 * 1024 * 1024
HIGHEST = lax.Precision.HIGHEST


LAG_ROWS = 512
TAU_ROWS = 24


def _prep_kernel(arow_ref, bt_ref, c_ref, dt_ref, m_ref, ws_ref, wo_ref, coef_ref):
    f32 = jnp.float32
    lo16 = lax.broadcasted_iota(jnp.int32, (SSM_CH, LANES), 1) < SSM_STATE
    lo24 = lax.broadcasted_iota(jnp.int32, (TAU_ROWS, LANES), 1) < SSM_STATE
    tau = lax.broadcasted_iota(jnp.int32, (TAU_ROWS, LANES), 0).astype(f32)
    arow = arow_ref[...]

    lag_tabs, kws, ws_parts, coef_rows = [], [], [], []
    for d in range(2):
        ar = arow[3 * d:3 * d + 1, :]
        ai = arow[3 * d + 1:3 * d + 2, :]
        dt = jnp.exp(arow[3 * d + 2:3 * d + 3, :])
        zr = dt * ar
        zi = dt * ai
        mag = jnp.exp(zr)
        lr = mag * jnp.cos(zi)
        li = mag * jnp.sin(zi)
        den = ar * ar + ai * ai
        fr = ((lr - 1.0) * ar + li * ai) / den
        fi = (li * ar - (lr - 1.0) * ai) / den
        btr = bt_ref[0, d]
        bti = bt_ref[1, d]
        bbr = fr * btr - fi * bti
        bbi = fr * bti + fi * btr
        cr = c_ref[0, d]
        ci = c_ref[1, d]

        pmag = jnp.exp(tau * zr)
        pr = pmag * jnp.cos(tau * zi)
        pi = pmag * jnp.sin(tau * zi)
        pa = jnp.where(lo24, pr, pi)
        pb = jnp.where(lo24, -pi, pr)
        pc = jnp.where(lo24, pi, pr)
        pd = jnp.where(lo24, pr, -pi)

        def row(t, tab):
            return tab[t:t + 1, :]

        order = range(CHUNK + 1) if d == 0 else range(CHUNK, -1, -1)
        blocks = [cr * row(t, pa) + ci * row(t, pb) for t in order]
        blocks.append(jnp.zeros((LAG_ROWS - (CHUNK + 1) * SSM_CH, LANES), f32))
        lag_tab = jnp.concatenate(blocks, axis=0)
        lag_tabs.append(lag_tab)
        lhs = jnp.where(lo16, bbr, -bbi)
        kws.append(lax.dot_general(lhs, lag_tab, (((1,), (1,)), ((), ())),
                                   precision=HIGHEST, preferred_element_type=f32))

        parts = []
        for j in range(CHUNK):
            t = CHUNK - 1 - j if d == 0 else j
            parts.append(jnp.concatenate([bbr * row(t, pa) + bbi * row(t, pb),
                                          bbr * row(t, pc) + bbi * row(t, pd)], axis=1))
        ws_parts.append(jnp.concatenate(parts, axis=0))

        a1 = row(CHUNK, pr)
        a2 = jnp.where(lo24, -pi, pi)[CHUNK:CHUNK + 1, :]
        coef_rows += [a1, a2, -a2]

    ws_ref[...] = jnp.concatenate(ws_parts, axis=1).astype(ws_ref.dtype)
    coef_ref[...] = jnp.concatenate(coef_rows + [jnp.zeros((2, LANES), f32)], axis=0)

    sign = jnp.where(lax.broadcasted_iota(jnp.int32, (LANES, CHUNK_VEC), 0) < SSM_STATE, 1.0, -1.0)
    wo_f = jnp.transpose(lag_tabs[0][SSM_CH:SSM_CH + CHUNK_VEC, :]) * sign
    wo_b = jnp.transpose(lag_tabs[1][:CHUNK_VEC, :]) * sign
    wo_ref[...] = jnp.concatenate([wo_f, wo_b], axis=0).astype(wo_ref.dtype)

    sub = lax.broadcasted_iota(jnp.int32, (SSM_CH, CHUNK_VEC), 0)
    lane = lax.broadcasted_iota(jnp.int32, (SSM_CH, CHUNK_VEC), 1)
    dtile = jnp.broadcast_to(dt_ref[...], (SSM_CH, CHUNK_VEC))
    for jp in range(CHUNK):
        fwd = kws[0] if jp == 0 else pltpu.roll(kws[0], SSM_CH * jp, 1)
        bwd = pltpu.roll(kws[1], (LAG_ROWS - SSM_CH * (CHUNK - jp)) % LAG_ROWS, 1)
        skip = jnp.where((lane // SSM_CH == jp) & (lane % SSM_CH == sub), dtile, 0.0)
        blk = fwd[:, :CHUNK_VEC] + bwd[:, :CHUNK_VEC] + skip
        m_ref[jp * SSM_CH:(jp + 1) * SSM_CH, :] = blk.astype(m_ref.dtype)


def _ssm_tables(a_re, a_im, log_dt, b_re, b_im, c_re, c_im, d):
    f32 = jnp.float32
    dup = lambda v: jnp.concatenate([v, v], axis=-1)
    ldt = jnp.broadcast_to(log_dt.astype(f32)[..., None], a_re.shape)
    zero = jnp.zeros_like(ldt[0])
    arow = dup(jnp.stack([a_re[0], a_im[0], ldt[0], a_re[1], a_im[1], ldt[1], zero, zero], axis=1).astype(f32))
    bt = dup(jnp.swapaxes(jnp.stack([b_re, b_im]).astype(f32), -1, -2))
    cc = dup(jnp.stack([c_re, c_im]).astype(f32))
    dtile = jnp.tile(d.astype(f32).reshape(N_GROUPS, 1, SSM_CH), (1, 1, CHUNK))

    bf16 = jnp.bfloat16
    return pl.pallas_call(
        _prep_kernel,
        grid=(N_GROUPS,),
        in_specs=[pl.BlockSpec((None, 8, LANES), lambda g: (g, 0, 0)),
                  pl.BlockSpec((2, 2, None, SSM_CH, LANES), lambda g: (0, 0, g, 0, 0)),
                  pl.BlockSpec((2, 2, None, SSM_CH, LANES), lambda g: (0, 0, g, 0, 0)),
                  pl.BlockSpec((None, 1, CHUNK_VEC), lambda g: (g, 0, 0))],
        out_specs=[pl.BlockSpec((None, CHUNK_VEC, CHUNK_VEC), lambda g: (g, 0, 0)),
                   pl.BlockSpec((None, CHUNK_VEC, 4 * LANES), lambda g: (g, 0, 0)),
                   pl.BlockSpec((None, 2 * STATE_VEC, CHUNK_VEC), lambda g: (g, 0, 0)),
                   pl.BlockSpec((None, 8, LANES), lambda g: (g, 0, 0))],
        out_shape=[jax.ShapeDtypeStruct((N_GROUPS, CHUNK_VEC, CHUNK_VEC), bf16),
                   jax.ShapeDtypeStruct((N_GROUPS, CHUNK_VEC, 4 * LANES), bf16),
                   jax.ShapeDtypeStruct((N_GROUPS, 2 * STATE_VEC, CHUNK_VEC), bf16),
                   jax.ShapeDtypeStruct((N_GROUPS, 8, LANES), f32)],
        compiler_params=pltpu.CompilerParams(dimension_semantics=("arbitrary",),
                                             vmem_limit_bytes=VMEM_LIMIT),
        name="ssm_tables",
    )(arow, bt, cc, dtile)


def _ssm_tables_xla(a_re, a_im, log_dt, b_re, b_im, c_re, c_im, d):
    f32 = jnp.float32
    dt = jnp.exp(log_dt.astype(f32))[..., None]
    ar = a_re.astype(f32)
    ai = a_im.astype(f32)
    zr = dt * ar
    zi = dt * ai
    mag = jnp.exp(zr)
    lr = mag * jnp.cos(zi)
    li = mag * jnp.sin(zi)
    den = ar * ar + ai * ai
    fr = ((lr - 1.0) * ar + li * ai) / den
    fi = (li * ar - (lr - 1.0) * ai) / den
    br = b_re.astype(f32)
    bi = b_im.astype(f32)
    bbr = fr[..., None] * br - fi[..., None] * bi
    bbi = fr[..., None] * bi + fi[..., None] * br
    cr = c_re.astype(f32)
    ci = c_im.astype(f32)

    tau = jnp.arange(CHUNK + 1, dtype=f32)[:, None, None, None]
    pmag = jnp.exp(tau * zr[None])
    pr = pmag * jnp.cos(tau * zi[None])
    pi = pmag * jnp.sin(tau * zi[None])

    er = pr[..., None] * bbr[None] - pi[..., None] * bbi[None]
    ei = pr[..., None] * bbi[None] + pi[..., None] * bbr[None]
    gr = cr[None] * pr[:, :, :, None, :] - ci[None] * pi[:, :, :, None, :]
    gi = cr[None] * pi[:, :, :, None, :] + ci[None] * pr[:, :, :, None, :]

    kk = (jnp.einsum('dgcp,tdgpe->tdgce', cr, er, precision=HIGHEST)
          - jnp.einsum('dgcp,tdgpe->tdgce', ci, ei, precision=HIGHEST))
    kf = kk[:CHUNK, 0]
    kb = kk[:CHUNK, 1]
    dmat = d.astype(f32).reshape(N_GROUPS, SSM_CH)[:, :, None] * jnp.eye(SSM_CH, dtype=f32)[None]
    center = kf[0] + kb[0] + dmat
    kall = jnp.concatenate([kb[:0:-1], center[None], kf[1:]], axis=0)
    jj = jnp.arange(CHUNK)
    lag_idx = jj[None, :] - jj[:, None] + (CHUNK - 1)
    m5 = kall[lag_idx]
    m_intra = jnp.transpose(m5, (2, 0, 4, 1, 3)).reshape(N_GROUPS, CHUNK_VEC, CHUNK_VEC)

    ef_r = er[:CHUNK, 0][::-1]
    ef_i = ei[:CHUNK, 0][::-1]
    eb_r = er[:CHUNK, 1]
    eb_i = ei[:CHUNK, 1]

    def _rows(e):
        return jnp.transpose(e, (1, 0, 3, 2)).reshape(N_GROUPS, CHUNK_VEC, SSM_STATE)

    w_state = jnp.concatenate([_rows(ef_r), _rows(ef_i), _rows(ef_i), _rows(ef_r),
                               _rows(eb_r), _rows(eb_i), _rows(eb_i), _rows(eb_r)], axis=2)

    def _cols(g):
        return jnp.transpose(g, (1, 3, 0, 2)).reshape(N_GROUPS, SSM_STATE, CHUNK_VEC)

    gf_r = gr[1:, 0]
    gf_i = gi[1:, 0]
    gb_r = gr[1:, 1][::-1]
    gb_i = gi[1:, 1][::-1]
    w_out = jnp.concatenate([_cols(gf_r), -_cols(gf_i), _cols(gb_r), -_cols(gb_i)], axis=1)

    a_r = pr[CHUNK]
    a_i = pi[CHUNK]
    a1 = jnp.concatenate([a_r, a_r], axis=-1)
    a2 = jnp.concatenate([-a_i, a_i], axis=-1)
    a3 = -a2
    zero = jnp.zeros_like(a1[0])
    coef = jnp.stack([a1[0], a2[0], a3[0], a1[1], a2[1], a3[1], zero, zero], axis=1)
    return (m_intra.astype(jnp.bfloat16), w_state.astype(jnp.bfloat16),
            w_out.astype(jnp.bfloat16), coef)


def _rope_tables():
    half = HEAD_DIM // 2
    pos = jnp.arange(SEQ, dtype=jnp.int32)
    inv_freq = ROPE_THETA ** (-jnp.arange(half, dtype=jnp.float32) / half)
    ang = pos.astype(jnp.float32)[:, None] * inv_freq[None, :]
    cos = jnp.cos(ang)
    sin = jnp.sin(ang)
    return jnp.concatenate([cos, cos, cos, cos, -sin, sin, -sin, sin], axis=1)


def _rope_block(xb, cos_t, sin_t, first_half):
    swapped = jnp.where(first_half, pltpu.roll(xb, 96, 1), pltpu.roll(xb, 32, 1))
    return xb * cos_t + swapped * sin_t


def _inproj_kernel(x_ref, wq_ref, wkv_ref, wz_ref, wu_ref, rope_ref,
                   q_ref, kv_ref, z_ref, ut_ref, uslab_ref):
    step = pl.program_id(0)
    part = step % STEPS_PER_REGROUP
    lane = lax.broadcasted_iota(jnp.int32, (SUB_ROWS, LANES), 1)
    first_half = (lane % HEAD_DIM) < (HEAD_DIM // 2)

    for sub in range(TOK_BLOCK // SUB_ROWS):
        rows = slice(sub * SUB_ROWS, (sub + 1) * SUB_ROWS)
        xb = x_ref[rows, :].astype(jnp.bfloat16)
        cos_t = rope_ref[rows, :LANES]
        sin_t = rope_ref[rows, LANES:]
        q = jnp.dot(xb, wq_ref[...], preferred_element_type=jnp.float32)
        q_blocks = [_rope_block(q[:, v * LANES:(v + 1) * LANES], cos_t, sin_t, first_half)
                    for v in range(Q_COLS // LANES)]
        q_ref[rows, :] = jnp.concatenate(q_blocks, axis=1).astype(q_ref.dtype)
        kv = jnp.dot(xb, wkv_ref[...], preferred_element_type=jnp.float32)
        k_rot = _rope_block(kv[:, :LANES], cos_t, sin_t, first_half)
        kv_ref[rows, :] = jnp.concatenate([k_rot, kv[:, LANES:]], axis=1).astype(kv_ref.dtype)
        z_ref[rows, :] = jnp.dot(xb, wz_ref[...], preferred_element_type=jnp.float32).astype(z_ref.dtype)
        u = jnp.dot(xb, wu_ref[...], preferred_element_type=jnp.float32)
        for v in range(N_LANE_BLOCKS):
            dst = pl.ds(pl.multiple_of(part * TOK_BLOCK + sub * SUB_ROWS, SUB_ROWS), SUB_ROWS)
            uslab_ref[v, dst, :] = u[:, v * LANES:(v + 1) * LANES]

    @pl.when(part == STEPS_PER_REGROUP - 1)
    def _():
        for j in range(CHUNK):
            for v in range(N_LANE_BLOCKS):
                uj = uslab_ref[v, pl.ds(j, LANES, stride=CHUNK), :]
                ujt = jnp.transpose(uj).reshape(GROUPS_PER_LANE_BLOCK, SSM_CH, LANES)
                g0 = v * GROUPS_PER_LANE_BLOCK
                ut_ref[g0:g0 + GROUPS_PER_LANE_BLOCK, j * SSM_CH:(j + 1) * SSM_CH, :] = ujt


def _inproj(x2, wq, wkv, wz, wu, rope):
    n_steps = BATCH * SEQ // TOK_BLOCK
    blocks_per_seq = SEQ // TOK_BLOCK

    def tok(cols):
        return pl.BlockSpec((TOK_BLOCK, cols), lambda s: (s, 0))

    def whole(shape):
        return pl.BlockSpec(shape, lambda s: (0,) * len(shape))

    return pl.pallas_call(
        _inproj_kernel,
        grid=(n_steps,),
        in_specs=[tok(D_MODEL), whole(wq.shape), whole(wkv.shape), whole(wz.shape), whole(wu.shape),
                  pl.BlockSpec((TOK_BLOCK, 2 * LANES), lambda s: (s % blocks_per_seq, 0))],
        out_specs=[tok(Q_COLS), tok(2 * KV_COLS), tok(D_ATTN + D_SSM),
                   pl.BlockSpec((N_GROUPS, CHUNK_VEC, LANES), lambda s: (0, 0, s // STEPS_PER_REGROUP))],
        out_shape=[jax.ShapeDtypeStruct((BATCH * SEQ, Q_COLS), jnp.bfloat16),
                   jax.ShapeDtypeStruct((BATCH * SEQ, 2 * KV_COLS), jnp.bfloat16),
                   jax.ShapeDtypeStruct((BATCH * SEQ, D_ATTN + D_SSM), jnp.bfloat16),
                   jax.ShapeDtypeStruct((N_GROUPS, CHUNK_VEC, ROWS), jnp.float32)],
        scratch_shapes=[pltpu.VMEM((N_LANE_BLOCKS, REGROUP_TOK, LANES), jnp.float32)],
        compiler_params=pltpu.CompilerParams(dimension_semantics=("arbitrary",),
                                             vmem_limit_bytes=VMEM_LIMIT),
        name="inproj",
    )(x2, wq, wkv, wz, wu, rope)


ATTN_SUB = 4
ATTN_STEP = ATTN_SUB * BLOCK
N_KEYS = 3 * BLOCK


def _attn_kernel(q_ref, kp_ref, kc_ref, kn_ref, sink_ref, y_ref):
    i = pl.program_id(1)
    kvw = jnp.concatenate([kp_ref[...], kc_ref[...], kn_ref[...]], axis=0).astype(jnp.float32)

    n_win = kvw.shape[0]
    lane_w = lax.broadcasted_iota(jnp.int32, (n_win, LANES), 1)
    k_lo, k_hi, vt_lo, vt_hi = [], [], [], []
    for hk in range(N_KV_HEADS):
        in_head = (lane_w >= hk * HEAD_DIM) & (lane_w < (hk + 1) * HEAD_DIM)
        k_nat = jnp.where(in_head, kvw[:, :LANES], 0.0)
        v_nat = jnp.where(in_head, kvw[:, LANES:], 0.0)
        k_oth = pltpu.roll(k_nat, HEAD_DIM, 1)
        v_oth = pltpu.roll(v_nat, HEAD_DIM, 1)
        pairs = ((k_nat, k_oth), (v_nat, v_oth)) if hk == 0 else ((k_oth, k_nat), (v_oth, v_nat))
        k_lo.append(pairs[0][0].astype(jnp.bfloat16))
        k_hi.append(pairs[0][1].astype(jnp.bfloat16))
        vt_lo.append(jnp.transpose(pairs[1][0]).astype(jnp.bfloat16))
        vt_hi.append(jnp.transpose(pairs[1][1]).astype(jnp.bfloat16))

    c = lax.broadcasted_iota(jnp.int32, (N_KEYS, BLOCK), 0)
    r = lax.broadcasted_iota(jnp.int32, (N_KEYS, BLOCK), 1)
    band = (c >= r) & (c - r <= 2 * WINDOW)
    ones_row = lax.broadcasted_iota(jnp.int32, (16, 2 * N_KEYS), 0)
    ones_col = lax.broadcasted_iota(jnp.int32, (16, 2 * N_KEYS), 1)
    ones = jnp.where(((ones_row == 0) & (ones_col < N_KEYS)) | ((ones_row == 1) & (ones_col >= N_KEYS)),
                     1.0, 0.0).astype(jnp.bfloat16)
    row_o = lax.broadcasted_iota(jnp.int32, (LANES, 2 * BLOCK), 0)
    n_blocks = SEQ // BLOCK

    def scores(t, hk):
        blk = i * ATTN_SUB + t
        lo = jnp.where(blk == 0, BLOCK, 0)
        hi = jnp.where(blk == n_blocks - 1, 2 * BLOCK, N_KEYS)
        valid1 = band & (c >= lo) & (c < hi)
        valid = jnp.concatenate([valid1, valid1], axis=1)
        win = slice(t * BLOCK, t * BLOCK + N_KEYS)
        q = q_ref[t * BLOCK:(t + 1) * BLOCK, (2 * hk) * LANES:(2 * hk + 2) * LANES]
        q = q.astype(jnp.float32) * (HEAD_DIM ** -0.5)
        qq = jnp.concatenate([q[:, :LANES], q[:, LANES:]], axis=0).astype(jnp.bfloat16)
        kcat = jnp.concatenate([k_lo[hk][win], k_hi[hk][win]], axis=0)
        st = lax.dot_general(kcat, qq, (((1,), (1,)), ((), ())),
                             preferred_element_type=jnp.float32)
        sps, ms, sinks = [], [], []
        for par in range(2):
            sp = jnp.where(valid, st[par * N_KEYS:(par + 1) * N_KEYS, :], NEG_INF)
            h0 = 4 * hk + par
            h1 = h0 + 2
            sink = jnp.concatenate([sink_ref[h0:h0 + 1, :], sink_ref[h1:h1 + 1, :]], axis=1)
            sps.append(sp)
            ms.append(jnp.maximum(jnp.max(sp, axis=0, keepdims=True), sink))
            sinks.append(sink)
        return sps, ms, sinks

    def weights(sps, ms, sinks):
        return [jnp.exp(sp - m).astype(jnp.bfloat16) for sp, m in zip(sps, ms)], ms, sinks

    def finish(t, hk, ps, ms, sinks):
        win = slice(t * BLOCK, t * BLOCK + N_KEYS)
        vt = jnp.concatenate([jnp.concatenate([vt_lo[hk][:, win], vt_hi[hk][:, win]], axis=1), ones], axis=0)
        ot = jnp.dot(vt, jnp.concatenate(ps, axis=0), preferred_element_type=jnp.float32)
        inv0 = 1.0 / (ot[LANES:LANES + 1, :] + jnp.exp(sinks[0] - ms[0]))
        inv1 = 1.0 / (ot[LANES + 1:LANES + 2, :] + jnp.exp(sinks[1] - ms[1]))
        o = jnp.transpose(ot[:LANES, :] * jnp.where(row_o < HEAD_DIM, inv0, inv1))
        rows = slice(t * BLOCK, (t + 1) * BLOCK)
        y_ref[rows, (2 * hk) * LANES:(2 * hk + 1) * LANES] = o[:BLOCK].astype(y_ref.dtype)
        y_ref[rows, (2 * hk + 1) * LANES:(2 * hk + 2) * LANES] = o[BLOCK:].astype(y_ref.dtype)

    work = [(t, hk) for t in range(ATTN_SUB) for hk in range(N_KV_HEADS)]
    stage_a = {0: scores(*work[0])}
    stage_b = {}
    for n in range(len(work) + 1):
        if n + 1 < len(work):
            stage_a[n + 1] = scores(*work[n + 1])
        if n < len(work):
            stage_b[n] = weights(*stage_a.pop(n))
        if n >= 1:
            finish(*work[n - 1], *stage_b.pop(n - 1))


def _attention(q, kv, sink_tab):
    nb = SEQ // BLOCK
    return pl.pallas_call(
        _attn_kernel,
        grid=(BATCH, SEQ // ATTN_STEP),
        in_specs=[pl.BlockSpec((None, ATTN_STEP, Q_COLS), lambda b, i: (b, i, 0)),
                  pl.BlockSpec((None, BLOCK, 2 * KV_COLS),
                               lambda b, i: (b, jnp.maximum(i * ATTN_SUB - 1, 0), 0)),
                  pl.BlockSpec((None, ATTN_STEP, 2 * KV_COLS), lambda b, i: (b, i, 0)),
                  pl.BlockSpec((None, BLOCK, 2 * KV_COLS),
                               lambda b, i: (b, jnp.minimum((i + 1) * ATTN_SUB, nb - 1), 0)),
                  pl.BlockSpec((N_Q_HEADS, LANES), lambda b, i: (0, 0))],
        out_specs=pl.BlockSpec((None, ATTN_STEP, D_ATTN), lambda b, i: (b, i, 0)),
        out_shape=jax.ShapeDtypeStruct((BATCH, SEQ, D_ATTN), jnp.bfloat16),
        compiler_params=pltpu.CompilerParams(dimension_semantics=("arbitrary", "arbitrary"),
                                             vmem_limit_bytes=VMEM_LIMIT),
        name="attention",
    )(q, kv, kv, kv, sink_tab)


SSM_GB = 4


def _ssm_kernel(ut_ref, m_ref, ws_ref, wo_ref, coef_ref, yt_ref,
                sf_ref, sfs_ref, sb_ref, sbs_ref, xf_ref, xb_ref, u_ref):
    for gi in range(SSM_GB):
        u = jnp.transpose(ut_ref[gi]).astype(jnp.bfloat16)
        u_ref[gi] = u
        s = jnp.dot(u, ws_ref[gi], preferred_element_type=jnp.float32)
        for b in range(BATCH):
            rows = slice(b * N_CHUNKS, (b + 1) * N_CHUNKS)
            dst = pl.ds(b * SLAB_PITCH + 8, N_CHUNKS)
            sf_ref[gi, dst, :] = s[rows, 0 * LANES:1 * LANES]
            sfs_ref[gi, dst, :] = s[rows, 1 * LANES:2 * LANES]
            sb_ref[gi, dst, :] = s[rows, 2 * LANES:3 * LANES]
            sbs_ref[gi, dst, :] = s[rows, 3 * LANES:4 * LANES]

    zero = jnp.zeros((BATCH, LANES), jnp.float32)
    coefs = []
    for gi in range(SSM_GB):
        coefs.append([jnp.broadcast_to(coef_ref[gi, r:r + 1, :], (BATCH, LANES)) for r in range(6)])
        xf_ref[gi, pl.ds(8, BATCH, stride=SLAB_PITCH), :] = zero
        xb_ref[gi, pl.ds(8 + N_CHUNKS - 1, BATCH, stride=SLAB_PITCH), :] = zero

    def rows_at(r):
        return pl.ds(r, BATCH, stride=SLAB_PITCH)

    def step(k, carry):
        kb = N_CHUNKS - 1 - k
        out = []
        for gi in range(SSM_GB):
            a1f, a2f, a3f, a1b, a2b, a3b = coefs[gi]
            xf, xfs, xb, xbs = carry[4 * gi:4 * gi + 4]
            nxf = a1f * xf + a2f * xfs + sf_ref[gi, rows_at(8 + k), :]
            nxfs = a1f * xfs + a3f * xf + sfs_ref[gi, rows_at(8 + k), :]
            xf_ref[gi, rows_at(9 + k), :] = nxf
            nxb = a1b * xb + a2b * xbs + sb_ref[gi, rows_at(8 + kb), :]
            nxbs = a1b * xbs + a3b * xb + sbs_ref[gi, rows_at(8 + kb), :]
            xb_ref[gi, rows_at(7 + kb), :] = nxb
            out += [nxf, nxfs, nxb, nxbs]
        return tuple(out)

    lax.fori_loop(0, N_CHUNKS, step, (zero,) * (4 * SSM_GB))

    for gi in range(SSM_GB):
        xin_f = jnp.concatenate([xf_ref[gi, pl.ds(b * SLAB_PITCH + 8, N_CHUNKS), :] for b in range(BATCH)], axis=0)
        xin_b = jnp.concatenate([xb_ref[gi, pl.ds(b * SLAB_PITCH + 8, N_CHUNKS), :] for b in range(BATCH)], axis=0)
        lhs = jnp.concatenate([u_ref[gi], xin_f.astype(jnp.bfloat16), xin_b.astype(jnp.bfloat16)], axis=1)
        rhs = jnp.concatenate([m_ref[gi], wo_ref[gi]], axis=0)
        yt_ref[gi] = jnp.transpose(jnp.dot(lhs, rhs, preferred_element_type=jnp.float32))


def _ssm(ut, m_intra, w_state, w_out, coef):
    def per_group(shape):
        return pl.BlockSpec((SSM_GB,) + shape, lambda g: (g, 0, 0))

    slab = pltpu.VMEM((SSM_GB, SLAB_ROWS, LANES), jnp.float32)
    return pl.pallas_call(
        _ssm_kernel,
        grid=(N_GROUPS // SSM_GB,),
        in_specs=[per_group((CHUNK_VEC, ROWS)), per_group((CHUNK_VEC, CHUNK_VEC)),
                  per_group((CHUNK_VEC, 4 * LANES)), per_group((2 * STATE_VEC, CHUNK_VEC)),
                  per_group((8, LANES))],
        out_specs=per_group((CHUNK_VEC, ROWS)),
        out_shape=jax.ShapeDtypeStruct((N_GROUPS, CHUNK_VEC, ROWS), jnp.float32),
        scratch_shapes=[slab] * 6 + [pltpu.VMEM((SSM_GB, ROWS, CHUNK_VEC), jnp.bfloat16)],
        compiler_params=pltpu.CompilerParams(dimension_semantics=("arbitrary",),
                                             vmem_limit_bytes=VMEM_LIMIT),
        name="ssm",
    )(ut, m_intra, w_state, w_out, coef)


FINAL_SUB_ROWS = 256


def _rms(v, gain):
    return v * lax.rsqrt(jnp.mean(v * v, axis=-1, keepdims=True) + NORM_EPS) * gain


def _final_kernel(x_ref, ya_ref, z_ref, yt_ref, wglu_ref, bglu_ref, ga_ref, gs_ref,
                  woa_ref, wos_ref, lng_ref, lnb_ref, o_ref, yslab_ref):
    step = pl.program_id(0)
    part = step % STEPS_PER_REGROUP

    @pl.when(part == 0)
    def _():
        for j in range(CHUNK):
            for v in range(N_LANE_BLOCKS):
                g0 = v * GROUPS_PER_LANE_BLOCK
                yjt = yt_ref[g0:g0 + GROUPS_PER_LANE_BLOCK, j * SSM_CH:(j + 1) * SSM_CH, :]
                yj = jnp.transpose(yjt.reshape(LANES, LANES))
                yslab_ref[v, pl.ds(j, LANES, stride=CHUNK), :] = yj

    def rows_of(sub):
        return slice(sub * FINAL_SUB_ROWS, (sub + 1) * FINAL_SUB_ROWS)

    def gate_in(sub):
        rows = rows_of(sub)
        ya = ya_ref[rows, :].astype(jnp.float32) * jax.nn.silu(z_ref[rows, :D_ATTN].astype(jnp.float32))
        ya = _rms(ya, ga_ref[...]).astype(jnp.bfloat16)
        src = pl.ds(pl.multiple_of(part * TOK_BLOCK + sub * FINAL_SUB_ROWS, FINAL_SUB_ROWS), FINAL_SUB_ROWS)
        ys = jnp.concatenate([yslab_ref[v, src, :] for v in range(N_LANE_BLOCKS)], axis=1)
        return ya, jax.nn.gelu(ys)

    def glu(sub, ya, g):
        rows = rows_of(sub)
        gate = jnp.dot(g.astype(jnp.bfloat16), wglu_ref[...], preferred_element_type=jnp.float32) + bglu_ref[...]
        ys = g * jax.nn.sigmoid(gate)
        ys = ys * jax.nn.silu(z_ref[rows, D_ATTN:].astype(jnp.float32))
        return ya, _rms(ys, gs_ref[...]).astype(jnp.bfloat16)

    def project(sub, ya, ys):
        rows = rows_of(sub)
        out = (jnp.dot(ya, woa_ref[...], preferred_element_type=jnp.float32)
               + jnp.dot(ys, wos_ref[...], preferred_element_type=jnp.float32))
        h = DEEPNORM_ALPHA * x_ref[rows, :] + out
        mu = jnp.mean(h, axis=-1, keepdims=True)
        hc = h - mu
        var = jnp.mean(hc * hc, axis=-1, keepdims=True)
        o_ref[rows, :] = hc * lax.rsqrt(var + NORM_EPS) * lng_ref[...] + lnb_ref[...]

    n_sub = TOK_BLOCK // FINAL_SUB_ROWS
    stage_a, stage_b = {}, {}
    for n in range(n_sub + 2):
        if n < n_sub:
            stage_a[n] = gate_in(n)
        if 1 <= n <= n_sub:
            stage_b[n - 1] = glu(n - 1, *stage_a.pop(n - 1))
        if n >= 2:
            project(n - 2, *stage_b.pop(n - 2))


def _final(x2, ya, z, yt, wglu, bglu, ga, gs, woa, wos, lng, lnb):
    n_steps = BATCH * SEQ // TOK_BLOCK

    def tok(cols):
        return pl.BlockSpec((TOK_BLOCK, cols), lambda s: (s, 0))

    def whole(shape):
        return pl.BlockSpec(shape, lambda s: (0,) * len(shape))

    return pl.pallas_call(
        _final_kernel,
        grid=(n_steps,),
        in_specs=[tok(D_MODEL), tok(D_ATTN), tok(D_ATTN + D_SSM),
                  pl.BlockSpec((N_GROUPS, CHUNK_VEC, LANES), lambda s: (0, 0, s // STEPS_PER_REGROUP)),
                  whole(wglu.shape), whole(bglu.shape), whole(ga.shape), whole(gs.shape),
                  whole(woa.shape), whole(wos.shape), whole(lng.shape), whole(lnb.shape)],
        out_specs=tok(D_MODEL),
        out_shape=jax.ShapeDtypeStruct((BATCH * SEQ, D_MODEL), jnp.float32),
        scratch_shapes=[pltpu.VMEM((N_LANE_BLOCKS, REGROUP_TOK, LANES), jnp.float32)],
        compiler_params=pltpu.CompilerParams(dimension_semantics=("arbitrary",),
                                             vmem_limit_bytes=VMEM_LIMIT),
        name="final",
    )(x2, ya, z, yt, wglu, bglu, ga, gs, woa, wos, lng, lnb)


def kernel(x, w_in, attn_sink, ssm_a_re, ssm_a_im, ssm_log_dt, ssm_b_re, ssm_b_im, ssm_c_re, ssm_c_im,
           ssm_d, w_glu, b_glu, norm_attn_g, norm_ssm_g, w_out, ln_g, ln_b):
    assert x.shape == (BATCH, SEQ, D_MODEL) and w_in.shape[0] == 1
    bf16 = jnp.bfloat16
    w = w_in[0]
    c0 = Q_COLS
    c1 = c0 + 2 * KV_COLS
    c2 = c1 + D_ATTN
    c3 = c2 + D_SSM
    wq = w[:, :c0].astype(bf16)
    wkv = w[:, c0:c1].astype(bf16)
    wz = jnp.concatenate([w[:, c1:c2], w[:, c3:]], axis=1).astype(bf16)
    wu = w[:, c2:c3].astype(bf16)

    m_intra, w_state, w_so, coef = _ssm_tables(ssm_a_re[0], ssm_a_im[0], ssm_log_dt[0], ssm_b_re[0],
                                                ssm_b_im[0], ssm_c_re[0], ssm_c_im[0], ssm_d[0])
    rope = _rope_tables()
    sink_tab = jnp.broadcast_to(attn_sink[0].astype(jnp.float32)[:, None], (N_Q_HEADS, LANES))

    x2 = x.reshape(BATCH * SEQ, D_MODEL)
    q, kv, z, ut = _inproj(x2, wq, wkv, wz, wu, rope)
    ya = _attention(q.reshape(BATCH, SEQ, Q_COLS), kv.reshape(BATCH, SEQ, 2 * KV_COLS), sink_tab)
    yt = _ssm(ut, m_intra, w_state, w_so, coef)

    row = lambda v: v[0].astype(jnp.float32)[None, :]
    out = _final(x2, ya.reshape(BATCH * SEQ, D_ATTN), z, yt,
                 w_glu[0].astype(bf16), row(b_glu), row(norm_attn_g), row(norm_ssm_g),
                 w_out[0, :D_ATTN].astype(bf16), w_out[0, D_ATTN:].astype(bf16), row(ln_g), row(ln_b))
    return out.reshape(BATCH, SEQ, D_MODEL)
```

```python
import jax
import jax.numpy as jnp
from jax import lax
from jax.experimental import pallas as pl
from jax.experimental.pallas import tpu as pltpu

D_MODEL = 1024
BATCH = 4
SEQ = 4096
D_ATTN = 512
D_SSM = 512
HEAD_DIM = 64
N_Q_HEADS = 8
N_KV_HEADS = 2
WINDOW = 128
BLOCK = 128
ROPE_THETA = 10000.0
SSM_CH = 16
N_GROUPS = 32
SSM_STATE = 64
NORM_EPS = 1e-5
NEG_INF = -1e30
DEEPNORM_ALPHA = 2.0 ** 0.25

Q_COLS = N_Q_HEADS * HEAD_DIM
KV_COLS = N_KV_HEADS * HEAD_DIM
CHUNK = 16
N_CHUNKS = SEQ // CHUNK
CHUNK_VEC = CHUNK * SSM_CH
STATE_VEC = 2 * SSM_STATE
LANES = 128
ROWS = BATCH * N_CHUNKS
SLAB_PITCH = N_CHUNKS + 8
SLAB_ROWS = BATCH * SLAB_PITCH + 8
TOK_BLOCK = 1024
REGROUP_TOK = LANES * CHUNK
STEPS_PER_REGROUP = REGROUP_TOK // TOK_BLOCK
SUB_ROWS = 512
N_LANE_BLOCKS = D_SSM // LANES
GROUPS_PER_LANE_BLOCK = LANES // SSM_CH
VMEM_LIMIT = 56 * 1024 * 1024
HIGHEST = lax.Precision.HIGHEST


LAG_ROWS = 512
TAU_ROWS = 24


def _prep_kernel(arow_ref, bt_ref, c_ref, dt_ref, m_ref, ws_ref, wo_ref, coef_ref):
    f32 = jnp.float32
    lo16 = lax.broadcasted_iota(jnp.int32, (SSM_CH, LANES), 1) < SSM_STATE
    lo24 = lax.broadcasted_iota(jnp.int32, (TAU_ROWS, LANES), 1) < SSM_STATE
    tau = lax.broadcasted_iota(jnp.int32, (TAU_ROWS, LANES), 0).astype(f32)
    arow = arow_ref[...]

    lag_tabs, kws, ws_parts, coef_rows = [], [], [], []
    for d in range(2):
        ar = arow[3 * d:3 * d + 1, :]
        ai = arow[3 * d + 1:3 * d + 2, :]
        dt = jnp.exp(arow[3 * d + 2:3 * d + 3, :])
        zr = dt * ar
        zi = dt * ai
        mag = jnp.exp(zr)
        lr = mag * jnp.cos(zi)
        li = mag * jnp.sin(zi)
        den = ar * ar + ai * ai
        fr = ((lr - 1.0) * ar + li * ai) / den
        fi = (li * ar - (lr - 1.0) * ai) / den
        btr = bt_ref[0, d]
        bti = bt_ref[1, d]
        bbr = fr * btr - fi * bti
        bbi = fr * bti + fi * btr
        cr = c_ref[0, d]
        ci = c_ref[1, d]

        pmag = jnp.exp(tau * zr)
        pr = pmag * jnp.cos(tau * zi)
        pi = pmag * jnp.sin(tau * zi)
        pa = jnp.where(lo24, pr, pi)
        pb = jnp.where(lo24, -pi, pr)
        pc = jnp.where(lo24, pi, pr)
        pd = jnp.where(lo24, pr, -pi)

        def row(t, tab):
            return tab[t:t + 1, :]

        order = range(CHUNK + 1) if d == 0 else range(CHUNK, -1, -1)
        blocks = [cr * row(t, pa) + ci * row(t, pb) for t in order]
        blocks.append(jnp.zeros((LAG_ROWS - (CHUNK + 1) * SSM_CH, LANES), f32))
        lag_tab = jnp.concatenate(blocks, axis=0)
        lag_tabs.append(lag_tab)
        lhs = jnp.where(lo16, bbr, -bbi)
        kws.append(lax.dot_general(lhs, lag_tab, (((1,), (1,)), ((), ())),
                                   precision=HIGHEST, preferred_element_type=f32))

        parts = []
        for j in range(CHUNK):
            t = CHUNK - 1 - j if d == 0 else j
            parts.append(jnp.concatenate([bbr * row(t, pa) + bbi * row(t, pb),
                                          bbr * row(t, pc) + bbi * row(t, pd)], axis=1))
        ws_parts.append(jnp.concatenate(parts, axis=0))

        a1 = row(CHUNK, pr)
        a2 = jnp.where(lo24, -pi, pi)[CHUNK:CHUNK + 1, :]
        coef_rows += [a1, a2, -a2]

    ws_ref[...] = jnp.concatenate(ws_parts, axis=1).astype(ws_ref.dtype)
    coef_ref[...] = jnp.concatenate(coef_rows + [jnp.zeros((2, LANES), f32)], axis=0)

    sign = jnp.where(lax.broadcasted_iota(jnp.int32, (LANES, CHUNK_VEC), 0) < SSM_STATE, 1.0, -1.0)
    wo_f = jnp.transpose(lag_tabs[0][SSM_CH:SSM_CH + CHUNK_VEC, :]) * sign
    wo_b = jnp.transpose(lag_tabs[1][:CHUNK_VEC, :]) * sign
    wo_ref[...] = jnp.concatenate([wo_f, wo_b], axis=0).astype(wo_ref.dtype)

    sub = lax.broadcasted_iota(jnp.int32, (SSM_CH, CHUNK_VEC), 0)
    lane = lax.broadcasted_iota(jnp.int32, (SSM_CH, CHUNK_VEC), 1)
    dtile = jnp.broadcast_to(dt_ref[...], (SSM_CH, CHUNK_VEC))
    for jp in range(CHUNK):
        fwd = kws[0] if jp == 0 else pltpu.roll(kws[0], SSM_CH * jp, 1)
        bwd = pltpu.roll(kws[1], (LAG_ROWS - SSM_CH * (CHUNK - jp)) % LAG_ROWS, 1)
        skip = jnp.where((lane // SSM_CH == jp) & (lane % SSM_CH == sub), dtile, 0.0)
        blk = fwd[:, :CHUNK_VEC] + bwd[:, :CHUNK_VEC] + skip
        m_ref[jp * SSM_CH:(jp + 1) * SSM_CH, :] = blk.astype(m_ref.dtype)


def _ssm_tables(a_re, a_im, log_dt, b_re, b_im, c_re, c_im, d):
    f32 = jnp.float32
    dup = lambda v: jnp.concatenate([v, v], axis=-1)
    ldt = jnp.broadcast_to(log_dt.astype(f32)[..., None], a_re.shape)
    zero = jnp.zeros_like(ldt[0])
    arow = dup(jnp.stack([a_re[0], a_im[0], ldt[0], a_re[1], a_im[1], ldt[1], zero, zero], axis=1).astype(f32))
    bt = dup(jnp.swapaxes(jnp.stack([b_re, b_im]).astype(f32), -1, -2))
    cc = dup(jnp.stack([c_re, c_im]).astype(f32))
    dtile = jnp.tile(d.astype(f32).reshape(N_GROUPS, 1, SSM_CH), (1, 1, CHUNK))

    bf16 = jnp.bfloat16
    return pl.pallas_call(
        _prep_kernel,
        grid=(N_GROUPS,),
        in_specs=[pl.BlockSpec((None, 8, LANES), lambda g: (g, 0, 0)),
                  pl.BlockSpec((2, 2, None, SSM_CH, LANES), lambda g: (0, 0, g, 0, 0)),
                  pl.BlockSpec((2, 2, None, SSM_CH, LANES), lambda g: (0, 0, g, 0, 0)),
                  pl.BlockSpec((None, 1, CHUNK_VEC), lambda g: (g, 0, 0))],
        out_specs=[pl.BlockSpec((None, CHUNK_VEC, CHUNK_VEC), lambda g: (g, 0, 0)),
                   pl.BlockSpec((None, CHUNK_VEC, 4 * LANES), lambda g: (g, 0, 0)),
                   pl.BlockSpec((None, 2 * STATE_VEC, CHUNK_VEC), lambda g: (g, 0, 0)),
                   pl.BlockSpec((None, 8, LANES), lambda g: (g, 0, 0))],
        out_shape=[jax.ShapeDtypeStruct((N_GROUPS, CHUNK_VEC, CHUNK_VEC), bf16),
                   jax.ShapeDtypeStruct((N_GROUPS, CHUNK_VEC, 4 * LANES), bf16),
                   jax.ShapeDtypeStruct((N_GROUPS, 2 * STATE_VEC, CHUNK_VEC), bf16),
                   jax.ShapeDtypeStruct((N_GROUPS, 8, LANES), f32)],
        compiler_params=pltpu.CompilerParams(dimension_semantics=("arbitrary",),
                                             vmem_limit_bytes=VMEM_LIMIT),
        name="ssm_tables",
    )(arow, bt, cc, dtile)


def _ssm_tables_xla(a_re, a_im, log_dt, b_re, b_im, c_re, c_im, d):
    f32 = jnp.float32
    dt = jnp.exp(log_dt.astype(f32))[..., None]
    ar = a_re.astype(f32)
    ai = a_im.astype(f32)
    zr = dt * ar
    zi = dt * ai
    mag = jnp.exp(zr)
    lr = mag * jnp.cos(zi)
    li = mag * jnp.sin(zi)
    den = ar * ar + ai * ai
    fr = ((lr - 1.0) * ar + li * ai) / den
    fi = (li * ar - (lr - 1.0) * ai) / den
    br = b_re.astype(f32)
    bi = b_im.astype(f32)
    bbr = fr[..., None] * br - fi[..., None] * bi
    bbi = fr[..., None] * bi + fi[..., None] * br
    cr = c_re.astype(f32)
    ci = c_im.astype(f32)

    tau = jnp.arange(CHUNK + 1, dtype=f32)[:, None, None, None]
    pmag = jnp.exp(tau * zr[None])
    pr = pmag * jnp.cos(tau * zi[None])
    pi = pmag * jnp.sin(tau * zi[None])

    er = pr[..., None] * bbr[None] - pi[..., None] * bbi[None]
    ei = pr[..., None] * bbi[None] + pi[..., None] * bbr[None]
    gr = cr[None] * pr[:, :, :, None, :] - ci[None] * pi[:, :, :, None, :]
    gi = cr[None] * pi[:, :, :, None, :] + ci[None] * pr[:, :, :, None, :]

    kk = (jnp.einsum('dgcp,tdgpe->tdgce', cr, er, precision=HIGHEST)
          - jnp.einsum('dgcp,tdgpe->tdgce', ci, ei, precision=HIGHEST))
    kf = kk[:CHUNK, 0]
    kb = kk[:CHUNK, 1]
    dmat = d.astype(f32).reshape(N_GROUPS, SSM_CH)[:, :, None] * jnp.eye(SSM_CH, dtype=f32)[None]
    center = kf[0] + kb[0] + dmat
    kall = jnp.concatenate([kb[:0:-1], center[None], kf[1:]], axis=0)
    jj = jnp.arange(CHUNK)
    lag_idx = jj[None, :] - jj[:, None] + (CHUNK - 1)
    m5 = kall[lag_idx]
    m_intra = jnp.transpose(m5, (2, 0, 4, 1, 3)).reshape(N_GROUPS, CHUNK_VEC, CHUNK_VEC)

    ef_r = er[:CHUNK, 0][::-1]
    ef_i = ei[:CHUNK, 0][::-1]
    eb_r = er[:CHUNK, 1]
    eb_i = ei[:CHUNK, 1]

    def _rows(e):
        return jnp.transpose(e, (1, 0, 3, 2)).reshape(N_GROUPS, CHUNK_VEC, SSM_STATE)

    w_state = jnp.concatenate([_rows(ef_r), _rows(ef_i), _rows(ef_i), _rows(ef_r),
                               _rows(eb_r), _rows(eb_i), _rows(eb_i), _rows(eb_r)], axis=2)

    def _cols(g):
        return jnp.transpose(g, (1, 3, 0, 2)).reshape(N_GROUPS, SSM_STATE, CHUNK_VEC)

    gf_r = gr[1:, 0]
    gf_i = gi[1:, 0]
    gb_r = gr[1:, 1][::-1]
    gb_i = gi[1:, 1][::-1]
    w_out = jnp.concatenate([_cols(gf_r), -_cols(gf_i), _cols(gb_r), -_cols(gb_i)], axis=1)

    a_r = pr[CHUNK]
    a_i = pi[CHUNK]
    a1 = jnp.concatenate([a_r, a_r], axis=-1)
    a2 = jnp.concatenate([-a_i, a_i], axis=-1)
    a3 = -a2
    zero = jnp.zeros_like(a1[0])
    coef = jnp.stack([a1[0], a2[0], a3[0], a1[1], a2[1], a3[1], zero, zero], axis=1)
    return (m_intra.astype(jnp.bfloat16), w_state.astype(jnp.bfloat16),
            w_out.astype(jnp.bfloat16), coef)


def _rope_tables():
    half = HEAD_DIM // 2
    pos = jnp.arange(SEQ, dtype=jnp.int32)
    inv_freq = ROPE_THETA ** (-jnp.arange(half, dtype=jnp.float32) / half)
    ang = pos.astype(jnp.float32)[:, None] * inv_freq[None, :]
    cos, sin = lax.optimization_barrier((jnp.cos(ang), jnp.sin(ang)))
    return jnp.concatenate([cos, cos, cos, cos, -sin, sin, -sin, sin], axis=1)


def _rope_block(xb, cos_t, sin_t, first_half):
    swapped = jnp.where(first_half, pltpu.roll(xb, 96, 1), pltpu.roll(xb, 32, 1))
    return xb * cos_t + swapped * sin_t


COL_Q = (0, Q_COLS)
COL_KV = (COL_Q[1], COL_Q[1] + 2 * KV_COLS)
COL_ZA = (COL_KV[1], COL_KV[1] + D_ATTN)
COL_U = (COL_ZA[1], COL_ZA[1] + D_SSM)
COL_ZS = (COL_U[1], COL_U[1] + D_SSM)


def _silu(z):
    h = 0.5 * z
    return h + h * jnp.tanh(h)


def _inproj_kernel(x_ref, w_ref, rope_ref, q_ref, kv_ref, sz_ref, ut_ref, uslab_ref):
    step = pl.program_id(0)
    part = step % STEPS_PER_REGROUP
    lane = lax.broadcasted_iota(jnp.int32, (SUB_ROWS, LANES), 1)
    first_half = (lane % HEAD_DIM) < (HEAD_DIM // 2)

    def proj(xb, cols):
        return jnp.dot(xb, w_ref[:, cols[0]:cols[1]], preferred_element_type=jnp.float32)

    for sub in range(TOK_BLOCK // SUB_ROWS):
        rows = slice(sub * SUB_ROWS, (sub + 1) * SUB_ROWS)
        xb = x_ref[rows, :].astype(jnp.bfloat16)
        cos_t = rope_ref[rows, :LANES]
        sin_t = rope_ref[rows, LANES:]
        q = proj(xb, COL_Q)
        kv = proj(xb, COL_KV)
        q_blocks = [_rope_block(q[:, v * LANES:(v + 1) * LANES], cos_t, sin_t, first_half)
                    for v in range(Q_COLS // LANES)]
        q_ref[rows, :] = jnp.concatenate(q_blocks, axis=1).astype(q_ref.dtype)
        za = proj(xb, COL_ZA)
        k_rot = _rope_block(kv[:, :LANES], cos_t, sin_t, first_half)
        kv_ref[rows, :] = jnp.concatenate([k_rot, kv[:, LANES:]], axis=1).astype(kv_ref.dtype)
        u = proj(xb, COL_U)
        sz_ref[rows, :D_ATTN] = _silu(za).astype(sz_ref.dtype)
        zs = proj(xb, COL_ZS)
        for v in range(N_LANE_BLOCKS):
            dst = pl.ds(pl.multiple_of(part * TOK_BLOCK + sub * SUB_ROWS, SUB_ROWS), SUB_ROWS)
            uslab_ref[v, dst, :] = u[:, v * LANES:(v + 1) * LANES]
        sz_ref[rows, D_ATTN:] = _silu(zs).astype(sz_ref.dtype)

    @pl.when(part == STEPS_PER_REGROUP - 1)
    def _():
        for j in range(CHUNK):
            for v in range(N_LANE_BLOCKS):
                uj = uslab_ref[v, pl.ds(j, LANES, stride=CHUNK), :]
                ujt = jnp.transpose(uj).reshape(GROUPS_PER_LANE_BLOCK, SSM_CH, LANES)
                g0 = v * GROUPS_PER_LANE_BLOCK
                ut_ref[g0:g0 + GROUPS_PER_LANE_BLOCK, j * SSM_CH:(j + 1) * SSM_CH, :] = ujt.astype(ut_ref.dtype)


def _inproj(x2, w, rope):
    n_steps = BATCH * SEQ // TOK_BLOCK
    blocks_per_seq = SEQ // TOK_BLOCK

    def tok(cols):
        return pl.BlockSpec((TOK_BLOCK, cols), lambda s: (s, 0))

    return pl.pallas_call(
        _inproj_kernel,
        grid=(n_steps,),
        in_specs=[tok(D_MODEL), pl.BlockSpec(w.shape, lambda s: (0, 0)),
                  pl.BlockSpec((TOK_BLOCK, 2 * LANES), lambda s: (s % blocks_per_seq, 0))],
        out_specs=[tok(Q_COLS), tok(2 * KV_COLS), tok(D_ATTN + D_SSM),
                   pl.BlockSpec((N_GROUPS, CHUNK_VEC, LANES), lambda s: (0, 0, s // STEPS_PER_REGROUP))],
        out_shape=[jax.ShapeDtypeStruct((BATCH * SEQ, Q_COLS), jnp.bfloat16),
                   jax.ShapeDtypeStruct((BATCH * SEQ, 2 * KV_COLS), jnp.bfloat16),
                   jax.ShapeDtypeStruct((BATCH * SEQ, D_ATTN + D_SSM), jnp.bfloat16),
                   jax.ShapeDtypeStruct((N_GROUPS, CHUNK_VEC, ROWS), jnp.bfloat16)],
        scratch_shapes=[pltpu.VMEM((N_LANE_BLOCKS, REGROUP_TOK, LANES), jnp.float32)],
        compiler_params=pltpu.CompilerParams(dimension_semantics=("arbitrary",),
                                             vmem_limit_bytes=VMEM_LIMIT),
        name="inproj",
    )(x2, w, rope)


ATTN_SUB = 4
ATTN_STEP = ATTN_SUB * BLOCK
N_KEYS = 3 * BLOCK


def _attn_kernel(q_ref, kp_ref, kc_ref, kn_ref, sink_ref, y_ref):
    i = pl.program_id(1)
    kvw = jnp.concatenate([kp_ref[...], kc_ref[...], kn_ref[...]], axis=0).astype(jnp.float32)

    n_win = kvw.shape[0]
    lane_w = lax.broadcasted_iota(jnp.int32, (n_win, LANES), 1)
    k_lo, k_hi, vt_lo, vt_hi = [], [], [], []
    for hk in range(N_KV_HEADS):
        in_head = (lane_w >= hk * HEAD_DIM) & (lane_w < (hk + 1) * HEAD_DIM)
        k_nat = jnp.where(in_head, kvw[:, :LANES], 0.0)
        v_nat = jnp.where(in_head, kvw[:, LANES:], 0.0)
        k_oth = pltpu.roll(k_nat, HEAD_DIM, 1)
        v_oth = pltpu.roll(v_nat, HEAD_DIM, 1)
        pairs = ((k_nat, k_oth), (v_nat, v_oth)) if hk == 0 else ((k_oth, k_nat), (v_oth, v_nat))
        k_lo.append(pairs[0][0].astype(jnp.bfloat16))
        k_hi.append(pairs[0][1].astype(jnp.bfloat16))
        vt_lo.append(jnp.transpose(pairs[1][0]).astype(jnp.bfloat16))
        vt_hi.append(jnp.transpose(pairs[1][1]).astype(jnp.bfloat16))

    c = lax.broadcasted_iota(jnp.int32, (N_KEYS, BLOCK), 0)
    r = lax.broadcasted_iota(jnp.int32, (N_KEYS, BLOCK), 1)
    band = (c >= r) & (c - r <= 2 * WINDOW)
    ones_row = lax.broadcasted_iota(jnp.int32, (16, 2 * N_KEYS), 0)
    ones_col = lax.broadcasted_iota(jnp.int32, (16, 2 * N_KEYS), 1)
    ones = jnp.where(((ones_row == 0) & (ones_col < N_KEYS)) | ((ones_row == 1) & (ones_col >= N_KEYS)),
                     1.0, 0.0).astype(jnp.bfloat16)
    row_o = lax.broadcasted_iota(jnp.int32, (LANES, 2 * BLOCK), 0)
    n_blocks = SEQ // BLOCK

    def scores(t, hk):
        blk = i * ATTN_SUB + t
        lo = jnp.where(blk == 0, BLOCK, 0)
        hi = jnp.where(blk == n_blocks - 1, 2 * BLOCK, N_KEYS)
        valid1 = band & (c >= lo) & (c < hi)
        valid = jnp.concatenate([valid1, valid1], axis=1)
        win = slice(t * BLOCK, t * BLOCK + N_KEYS)
        q = q_ref[t * BLOCK:(t + 1) * BLOCK, (2 * hk) * LANES:(2 * hk + 2) * LANES]
        q = q.astype(jnp.float32) * (HEAD_DIM ** -0.5)
        qq = jnp.concatenate([q[:, :LANES], q[:, LANES:]], axis=0).astype(jnp.bfloat16)
        kcat = jnp.concatenate([k_lo[hk][win], k_hi[hk][win]], axis=0)
        st = lax.dot_general(kcat, qq, (((1,), (1,)), ((), ())),
                             preferred_element_type=jnp.float32)
        sps, ms, sinks = [], [], []
        for par in range(2):
            sp = jnp.where(valid, st[par * N_KEYS:(par + 1) * N_KEYS, :], NEG_INF)
            h0 = 4 * hk + par
            h1 = h0 + 2
            sink = jnp.concatenate([sink_ref[h0:h0 + 1, :], sink_ref[h1:h1 + 1, :]], axis=1)
            sps.append(sp)
            ms.append(jnp.maximum(jnp.max(sp, axis=0, keepdims=True), sink))
            sinks.append(sink)
        return sps, ms, sinks

    def weights(sps, ms, sinks):
        return [jnp.exp(sp - m).astype(jnp.bfloat16) for sp, m in zip(sps, ms)], ms, sinks

    def finish(t, hk, ps, ms, sinks):
        win = slice(t * BLOCK, t * BLOCK + N_KEYS)
        vt = jnp.concatenate([jnp.concatenate([vt_lo[hk][:, win], vt_hi[hk][:, win]], axis=1), ones], axis=0)
        ot = jnp.dot(vt, jnp.concatenate(ps, axis=0), preferred_element_type=jnp.float32)
        inv0 = 1.0 / (ot[LANES:LANES + 1, :] + jnp.exp(sinks[0] - ms[0]))
        inv1 = 1.0 / (ot[LANES + 1:LANES + 2, :] + jnp.exp(sinks[1] - ms[1]))
        o = jnp.transpose(ot[:LANES, :] * jnp.where(row_o < HEAD_DIM, inv0, inv1))
        rows = slice(t * BLOCK, (t + 1) * BLOCK)
        y_ref[rows, (2 * hk) * LANES:(2 * hk + 1) * LANES] = o[:BLOCK].astype(y_ref.dtype)
        y_ref[rows, (2 * hk + 1) * LANES:(2 * hk + 2) * LANES] = o[BLOCK:].astype(y_ref.dtype)

    work = [(t, hk) for t in range(ATTN_SUB) for hk in range(N_KV_HEADS)]
    stage_a = {0: scores(*work[0])}
    stage_b = {}
    for n in range(len(work) + 1):
        if n + 1 < len(work):
            stage_a[n + 1] = scores(*work[n + 1])
        if n < len(work):
            stage_b[n] = weights(*stage_a.pop(n))
        if n >= 1:
            finish(*work[n - 1], *stage_b.pop(n - 1))


def _attention(q, kv, sink_tab):
    nb = SEQ // BLOCK
    return pl.pallas_call(
        _attn_kernel,
        grid=(BATCH, SEQ // ATTN_STEP),
        in_specs=[pl.BlockSpec((None, ATTN_STEP, Q_COLS), lambda b, i: (b, i, 0)),
                  pl.BlockSpec((None, BLOCK, 2 * KV_COLS),
                               lambda b, i: (b, jnp.maximum(i * ATTN_SUB - 1, 0), 0)),
                  pl.BlockSpec((None, ATTN_STEP, 2 * KV_COLS), lambda b, i: (b, i, 0)),
                  pl.BlockSpec((None, BLOCK, 2 * KV_COLS),
                               lambda b, i: (b, jnp.minimum((i + 1) * ATTN_SUB, nb - 1), 0)),
                  pl.BlockSpec((N_Q_HEADS, LANES), lambda b, i: (0, 0))],
        out_specs=pl.BlockSpec((None, ATTN_STEP, D_ATTN), lambda b, i: (b, i, 0)),
        out_shape=jax.ShapeDtypeStruct((BATCH, SEQ, D_ATTN), jnp.bfloat16),
        compiler_params=pltpu.CompilerParams(dimension_semantics=("arbitrary", "arbitrary"),
                                             vmem_limit_bytes=VMEM_LIMIT),
        name="attention",
    )(q, kv, kv, kv, sink_tab)


SSM_GB = 4


def _ssm_kernel(ut_ref, m_ref, ws_ref, wo_ref, coef_ref, gt_ref,
                sf_ref, sfs_ref, sb_ref, sbs_ref, xf_ref, xb_ref, u_ref):
    for gi in range(SSM_GB):
        u = jnp.transpose(ut_ref[gi].astype(jnp.float32)).astype(jnp.bfloat16)
        u_ref[gi] = u
        s = jnp.dot(u, ws_ref[gi], preferred_element_type=jnp.float32)
        for b in range(BATCH):
            rows = slice(b * N_CHUNKS, (b + 1) * N_CHUNKS)
            dst = pl.ds(b * SLAB_PITCH + 8, N_CHUNKS)
            sf_ref[gi, dst, :] = s[rows, 0 * LANES:1 * LANES]
            sfs_ref[gi, dst, :] = s[rows, 1 * LANES:2 * LANES]
            sb_ref[gi, dst, :] = s[rows, 2 * LANES:3 * LANES]
            sbs_ref[gi, dst, :] = s[rows, 3 * LANES:4 * LANES]

    zero = jnp.zeros((BATCH, LANES), jnp.float32)
    coefs = []
    for gi in range(SSM_GB):
        coefs.append([jnp.broadcast_to(coef_ref[gi, r:r + 1, :], (BATCH, LANES)) for r in range(6)])
        xf_ref[gi, pl.ds(8, BATCH, stride=SLAB_PITCH), :] = zero
        xb_ref[gi, pl.ds(8 + N_CHUNKS - 1, BATCH, stride=SLAB_PITCH), :] = zero

    def rows_at(r):
        return pl.ds(r, BATCH, stride=SLAB_PITCH)

    def step(k, carry):
        kb = N_CHUNKS - 1 - k
        out = []
        for gi in range(SSM_GB):
            a1f, a2f, a3f, a1b, a2b, a3b = coefs[gi]
            xf, xfs, xb, xbs = carry[4 * gi:4 * gi + 4]
            nxf = a1f * xf + a2f * xfs + sf_ref[gi, rows_at(8 + k), :]
            nxfs = a1f * xfs + a3f * xf + sfs_ref[gi, rows_at(8 + k), :]
            xf_ref[gi, rows_at(9 + k), :] = nxf
            nxb = a1b * xb + a2b * xbs + sb_ref[gi, rows_at(8 + kb), :]
            nxbs = a1b * xbs + a3b * xb + sbs_ref[gi, rows_at(8 + kb), :]
            xb_ref[gi, rows_at(7 + kb), :] = nxb
            out += [nxf, nxfs, nxb, nxbs]
        return tuple(out)

    lax.fori_loop(0, N_CHUNKS, step, (zero,) * (4 * SSM_GB))

    def chunk_out(gi):
        xin_f = jnp.concatenate([xf_ref[gi, pl.ds(b * SLAB_PITCH + 8, N_CHUNKS), :] for b in range(BATCH)], axis=0)
        xin_b = jnp.concatenate([xb_ref[gi, pl.ds(b * SLAB_PITCH + 8, N_CHUNKS), :] for b in range(BATCH)], axis=0)
        lhs = jnp.concatenate([u_ref[gi], xin_f.astype(jnp.bfloat16), xin_b.astype(jnp.bfloat16)], axis=1)
        rhs = jnp.concatenate([m_ref[gi], wo_ref[gi]], axis=0)
        return jnp.dot(lhs, rhs, preferred_element_type=jnp.float32)

    y = chunk_out(0)
    for gi in range(SSM_GB):
        y_next = chunk_out(gi + 1) if gi + 1 < SSM_GB else None
        gt_ref[gi] = jnp.transpose(jax.nn.gelu(y)).astype(gt_ref.dtype)
        y = y_next


def _ssm(ut, m_intra, w_state, w_out, coef):
    def per_group(shape):
        return pl.BlockSpec((SSM_GB,) + shape, lambda g: (g, 0, 0))

    slab = pltpu.VMEM((SSM_GB, SLAB_ROWS, LANES), jnp.float32)
    return pl.pallas_call(
        _ssm_kernel,
        grid=(N_GROUPS // SSM_GB,),
        in_specs=[per_group((CHUNK_VEC, ROWS)), per_group((CHUNK_VEC, CHUNK_VEC)),
                  per_group((CHUNK_VEC, 4 * LANES)), per_group((2 * STATE_VEC, CHUNK_VEC)),
                  per_group((8, LANES))],
        out_specs=per_group((CHUNK_VEC, ROWS)),
        out_shape=jax.ShapeDtypeStruct((N_GROUPS, CHUNK_VEC, ROWS), jnp.bfloat16),
        scratch_shapes=[slab] * 6 + [pltpu.VMEM((SSM_GB, ROWS, CHUNK_VEC), jnp.bfloat16)],
        compiler_params=pltpu.CompilerParams(dimension_semantics=("arbitrary",),
                                             vmem_limit_bytes=VMEM_LIMIT),
        name="ssm",
    )(ut, m_intra, w_state, w_out, coef)


FINAL_SUB_ROWS = 256


def _unit_rms(v):
    return v * lax.rsqrt(jnp.mean(v * v, axis=-1, keepdims=True) + NORM_EPS)


def _final_kernel(x_ref, ya_ref, sz_ref, gt_ref, wglu_ref, bglu_ref,
                  woa_ref, wos_ref, lng_ref, lnb_ref, o_ref, gslab_ref):
    step = pl.program_id(0)
    part = step % STEPS_PER_REGROUP

    @pl.when(part == 0)
    def _():
        for j in range(CHUNK):
            for v in range(N_LANE_BLOCKS):
                g0 = v * GROUPS_PER_LANE_BLOCK
                gjt = gt_ref[g0:g0 + GROUPS_PER_LANE_BLOCK, j * SSM_CH:(j + 1) * SSM_CH, :].astype(jnp.float32)
                gj = jnp.transpose(gjt.reshape(LANES, LANES))
                gslab_ref[v, pl.ds(j, LANES, stride=CHUNK), :] = gj

    def rows_of(sub):
        return slice(sub * FINAL_SUB_ROWS, (sub + 1) * FINAL_SUB_ROWS)

    def gate_in(sub):
        rows = rows_of(sub)
        ya = ya_ref[rows, :].astype(jnp.float32) * sz_ref[rows, :D_ATTN].astype(jnp.float32)
        ya = _unit_rms(ya).astype(jnp.bfloat16)
        src = pl.ds(pl.multiple_of(part * TOK_BLOCK + sub * FINAL_SUB_ROWS, FINAL_SUB_ROWS), FINAL_SUB_ROWS)
        g = jnp.concatenate([gslab_ref[v, src, :] for v in range(N_LANE_BLOCKS)], axis=1)
        return ya, g

    def glu(sub, ya, g):
        rows = rows_of(sub)
        gate = jnp.dot(g.astype(jnp.bfloat16), wglu_ref[...], preferred_element_type=jnp.float32) + bglu_ref[...]
        ys = g * jax.nn.sigmoid(gate)
        ys = ys * sz_ref[rows, D_ATTN:].astype(jnp.float32)
        return ya, _unit_rms(ys).astype(jnp.bfloat16)

    def project(sub, ya, ys):
        rows = rows_of(sub)
        out = (jnp.dot(ya, woa_ref[...], preferred_element_type=jnp.float32)
               + jnp.dot(ys, wos_ref[...], preferred_element_type=jnp.float32))
        h = x_ref[rows, :] + out
        mu = jnp.mean(h, axis=-1, keepdims=True)
        hc = h - mu
        var = jnp.mean(hc * hc, axis=-1, keepdims=True)
        o_ref[rows, :] = hc * lax.rsqrt(var + NORM_EPS / DEEPNORM_ALPHA ** 2) * lng_ref[...] + lnb_ref[...]

    n_sub = TOK_BLOCK // FINAL_SUB_ROWS
    stage_a, stage_b = {}, {}
    for n in range(n_sub + 2):
        if n < n_sub:
            stage_a[n] = gate_in(n)
        if 1 <= n <= n_sub:
            stage_b[n - 1] = glu(n - 1, *stage_a.pop(n - 1))
        if n >= 2:
            project(n - 2, *stage_b.pop(n - 2))


def _final(x2, ya, sz, gt, wglu, bglu, woa, wos, lng, lnb):
    n_steps = BATCH * SEQ // TOK_BLOCK

    def tok(cols):
        return pl.BlockSpec((TOK_BLOCK, cols), lambda s: (s, 0))

    def whole(shape):
        return pl.BlockSpec(shape, lambda s: (0,) * len(shape))

    return pl.pallas_call(
        _final_kernel,
        grid=(n_steps,),
        in_specs=[tok(D_MODEL), tok(D_ATTN), tok(D_ATTN + D_SSM),
                  pl.BlockSpec((N_GROUPS, CHUNK_VEC, LANES), lambda s: (0, 0, s // STEPS_PER_REGROUP)),
                  whole(wglu.shape), whole(bglu.shape),
                  whole(woa.shape), whole(wos.shape), whole(lng.shape), whole(lnb.shape)],
        out_specs=tok(D_MODEL),
        out_shape=jax.ShapeDtypeStruct((BATCH * SEQ, D_MODEL), jnp.float32),
        scratch_shapes=[pltpu.VMEM((N_LANE_BLOCKS, REGROUP_TOK, LANES), jnp.float32)],
        compiler_params=pltpu.CompilerParams(dimension_semantics=("arbitrary",),
                                             vmem_limit_bytes=VMEM_LIMIT),
        name="final",
    )(x2, ya, sz, gt, wglu, bglu, woa, wos, lng, lnb)


def kernel(x, w_in, attn_sink, ssm_a_re, ssm_a_im, ssm_log_dt, ssm_b_re, ssm_b_im, ssm_c_re, ssm_c_im,
           ssm_d, w_glu, b_glu, norm_attn_g, norm_ssm_g, w_out, ln_g, ln_b):
    assert x.shape == (BATCH, SEQ, D_MODEL) and w_in.shape[0] == 1
    bf16 = jnp.bfloat16
    f32 = jnp.float32
    m_intra, w_state, w_so, coef = _ssm_tables(ssm_a_re[0], ssm_a_im[0], ssm_log_dt[0], ssm_b_re[0],
                                                ssm_b_im[0], ssm_c_re[0], ssm_c_im[0], ssm_d[0])
    rope = _rope_tables()
    sink_tab = jnp.broadcast_to(attn_sink[0].astype(f32)[:, None], (N_Q_HEADS, LANES))

    x2 = x.reshape(BATCH * SEQ, D_MODEL)
    q, kv, sz, ut = _inproj(x2, w_in[0].astype(bf16), rope)
    ya = _attention(q.reshape(BATCH, SEQ, Q_COLS), kv.reshape(BATCH, SEQ, 2 * KV_COLS), sink_tab)
    gt = _ssm(ut, m_intra, w_state, w_so, coef)

    gains = jnp.concatenate([norm_attn_g[0], norm_ssm_g[0]]).astype(f32)
    wo = (w_out[0].astype(f32) * (gains / DEEPNORM_ALPHA)[:, None]).astype(bf16)
    row = lambda v: v[0].astype(f32)[None, :]
    out = _final(x2, ya.reshape(BATCH * SEQ, D_ATTN), sz, gt, w_glu[0].astype(bf16), row(b_glu),
                 wo[:D_ATTN], wo[D_ATTN:], row(ln_g), row(ln_b))
    return out.reshape(BATCH, SEQ, D_MODEL)
```

```python
import jax
import jax.numpy as jnp
import numpy as np
from jax import lax
from jax.experimental import pallas as pl
from jax.experimental.pallas import tpu as pltpu

D_MODEL = 1024
BATCH = 4
SEQ = 4096
D_ATTN = 512
D_SSM = 512
HEAD_DIM = 64
N_Q_HEADS = 8
N_KV_HEADS = 2
WINDOW = 128
BLOCK = 128
ROPE_THETA = 10000.0
SSM_CH = 16
N_GROUPS = 32
SSM_STATE = 64
NORM_EPS = 1e-5
NEG_INF = -1e30
DEEPNORM_ALPHA = 2.0 ** 0.25

Q_COLS = N_Q_HEADS * HEAD_DIM
KV_COLS = N_KV_HEADS * HEAD_DIM
CHUNK = 16
N_CHUNKS = SEQ // CHUNK
CHUNK_VEC = CHUNK * SSM_CH
STATE_VEC = 2 * SSM_STATE
LANES = 128
ROWS = BATCH * N_CHUNKS
SLAB_PITCH = N_CHUNKS + 8
SLAB_ROWS = BATCH * SLAB_PITCH + 8
TOK_BLOCK = 1024
REGROUP_TOK = LANES * CHUNK
STEPS_PER_REGROUP = REGROUP_TOK // TOK_BLOCK
SUB_ROWS = 512
N_LANE_BLOCKS = D_SSM // LANES
GROUPS_PER_LANE_BLOCK = LANES // SSM_CH
VMEM_LIMIT = 56 * 1024 * 1024
HIGHEST = lax.Precision.HIGHEST


LAG_ROWS = 512
PREP_GB = 4


def _prep_kernel(arow_ref, bt_ref, c_ref, dt_ref, m_ref, ws_ref, wo_ref, coef_ref):
    for gi in range(PREP_GB):
        _prep_group(gi, arow_ref, bt_ref, c_ref, dt_ref, m_ref, ws_ref, wo_ref, coef_ref)


def _prep_group(gi, arow_ref, bt_ref, c_ref, dt_ref, m_ref, ws_ref, wo_ref, coef_ref):
    f32 = jnp.float32
    lo16 = lax.broadcasted_iota(jnp.int32, (SSM_CH, LANES), 1) < SSM_STATE
    lo1 = lax.broadcasted_iota(jnp.int32, (1, LANES), 1) < SSM_STATE
    arow = arow_ref[gi]

    lag_tabs, kws, ws_parts, coef_rows = [], [], [], []
    for d in range(2):
        ar = arow[3 * d:3 * d + 1, :]
        ai = arow[3 * d + 1:3 * d + 2, :]
        dt = jnp.exp(arow[3 * d + 2:3 * d + 3, :])
        zr = dt * ar
        zi = dt * ai
        mag = jnp.exp(zr)
        lr = mag * jnp.cos(zi)
        li = mag * jnp.sin(zi)
        den = ar * ar + ai * ai
        fr = ((lr - 1.0) * ar + li * ai) / den
        fi = (li * ar - (lr - 1.0) * ai) / den
        btr = bt_ref[0, d, gi]
        bti = bt_ref[1, d, gi]
        bbr = fr * btr - fi * bti
        bbi = fr * bti + fi * btr
        cr = c_ref[0, d, gi]
        ci = c_ref[1, d, gi]

        pr = [jnp.ones((1, LANES), f32)]
        pi = [jnp.zeros((1, LANES), f32)]
        for _ in range(CHUNK):
            pr.append(pr[-1] * lr - pi[-1] * li)
            pi.append(pr[-2] * li + pi[-1] * lr)
        pa = [jnp.where(lo1, r, i) for r, i in zip(pr, pi)]
        pb = [jnp.where(lo1, -i, r) for r, i in zip(pr, pi)]
        pc = [jnp.where(lo1, i, r) for r, i in zip(pr, pi)]
        pd = [jnp.where(lo1, r, -i) for r, i in zip(pr, pi)]

        def row(t, tab):
            return tab[t]

        order = range(CHUNK + 1) if d == 0 else range(CHUNK, -1, -1)
        blocks = [cr * row(t, pa) + ci * row(t, pb) for t in order]
        blocks.append(jnp.zeros((LAG_ROWS - (CHUNK + 1) * SSM_CH, LANES), f32))
        lag_tab = jnp.concatenate(blocks, axis=0)
        lag_tabs.append(lag_tab)
        lhs = jnp.where(lo16, bbr, -bbi)
        kws.append(lax.dot_general(lhs, lag_tab, (((1,), (1,)), ((), ())),
                                   precision=HIGHEST, preferred_element_type=f32))

        parts = []
        for j in range(CHUNK):
            t = CHUNK - 1 - j if d == 0 else j
            parts.append(jnp.concatenate([bbr * row(t, pa) + bbi * row(t, pb),
                                          bbr * row(t, pc) + bbi * row(t, pd)], axis=1))
        ws_parts.append(jnp.concatenate(parts, axis=0))

        a1 = pr[CHUNK]
        a2 = jnp.where(lo1, -pi[CHUNK], pi[CHUNK])
        coef_rows += [a1, a2, -a2]

    ws_ref[gi] = jnp.concatenate(ws_parts, axis=1).astype(ws_ref.dtype)
    coef_ref[gi] = jnp.concatenate(coef_rows + [jnp.zeros((2, LANES), f32)], axis=0)

    sign = jnp.where(lax.broadcasted_iota(jnp.int32, (LANES, CHUNK_VEC), 0) < SSM_STATE, 1.0, -1.0)
    wo_f = jnp.transpose(lag_tabs[0][SSM_CH:SSM_CH + CHUNK_VEC, :]) * sign
    wo_b = jnp.transpose(lag_tabs[1][:CHUNK_VEC, :]) * sign
    wo_ref[gi] = jnp.concatenate([wo_f, wo_b], axis=0).astype(wo_ref.dtype)

    sub = lax.broadcasted_iota(jnp.int32, (SSM_CH, CHUNK_VEC), 0)
    lane = lax.broadcasted_iota(jnp.int32, (SSM_CH, CHUNK_VEC), 1)
    dtile = jnp.broadcast_to(dt_ref[gi], (SSM_CH, CHUNK_VEC))
    for jp in range(CHUNK):
        fwd = kws[0] if jp == 0 else pltpu.roll(kws[0], SSM_CH * jp, 1)
        bwd = pltpu.roll(kws[1], (LAG_ROWS - SSM_CH * (CHUNK - jp)) % LAG_ROWS, 1)
        skip = jnp.where((lane // SSM_CH == jp) & (lane % SSM_CH == sub), dtile, 0.0)
        blk = fwd[:, :CHUNK_VEC] + bwd[:, :CHUNK_VEC] + skip
        m_ref[gi, jp * SSM_CH:(jp + 1) * SSM_CH, :] = blk.astype(m_ref.dtype)


def _ssm_tables(a_re, a_im, log_dt, b_re, b_im, c_re, c_im, d):
    f32 = jnp.float32
    dup = lambda v: jnp.concatenate([v, v], axis=-1)
    ldt = jnp.broadcast_to(log_dt.astype(f32)[..., None], a_re.shape)
    zero = jnp.zeros_like(ldt[0])
    arow = dup(jnp.stack([a_re[0], a_im[0], ldt[0], a_re[1], a_im[1], ldt[1], zero, zero], axis=1).astype(f32))
    bt = dup(jnp.swapaxes(jnp.stack([b_re, b_im]).astype(f32), -1, -2))
    cc = dup(jnp.stack([c_re, c_im]).astype(f32))
    dtile = jnp.tile(d.astype(f32).reshape(N_GROUPS, 1, SSM_CH), (1, 1, CHUNK))

    bf16 = jnp.bfloat16
    return pl.pallas_call(
        _prep_kernel,
        grid=(N_GROUPS // PREP_GB,),
        in_specs=[pl.BlockSpec((PREP_GB, 8, LANES), lambda g: (g, 0, 0)),
                  pl.BlockSpec((2, 2, PREP_GB, SSM_CH, LANES), lambda g: (0, 0, g, 0, 0)),
                  pl.BlockSpec((2, 2, PREP_GB, SSM_CH, LANES), lambda g: (0, 0, g, 0, 0)),
                  pl.BlockSpec((PREP_GB, 1, CHUNK_VEC), lambda g: (g, 0, 0))],
        out_specs=[pl.BlockSpec((PREP_GB, CHUNK_VEC, CHUNK_VEC), lambda g: (g, 0, 0)),
                   pl.BlockSpec((PREP_GB, CHUNK_VEC, 4 * LANES), lambda g: (g, 0, 0)),
                   pl.BlockSpec((PREP_GB, 2 * STATE_VEC, CHUNK_VEC), lambda g: (g, 0, 0)),
                   pl.BlockSpec((PREP_GB, 8, LANES), lambda g: (g, 0, 0))],
        out_shape=[jax.ShapeDtypeStruct((N_GROUPS, CHUNK_VEC, CHUNK_VEC), bf16),
                   jax.ShapeDtypeStruct((N_GROUPS, CHUNK_VEC, 4 * LANES), bf16),
                   jax.ShapeDtypeStruct((N_GROUPS, 2 * STATE_VEC, CHUNK_VEC), bf16),
                   jax.ShapeDtypeStruct((N_GROUPS, 8, LANES), f32)],
        compiler_params=pltpu.CompilerParams(dimension_semantics=("arbitrary",),
                                             vmem_limit_bytes=VMEM_LIMIT),
        name="ssm_tables",
    )(arow, bt, cc, dtile)


def _ssm_tables_xla(a_re, a_im, log_dt, b_re, b_im, c_re, c_im, d):
    f32 = jnp.float32
    dt = jnp.exp(log_dt.astype(f32))[..., None]
    ar = a_re.astype(f32)
    ai = a_im.astype(f32)
    zr = dt * ar
    zi = dt * ai
    mag = jnp.exp(zr)
    lr = mag * jnp.cos(zi)
    li = mag * jnp.sin(zi)
    den = ar * ar + ai * ai
    fr = ((lr - 1.0) * ar + li * ai) / den
    fi = (li * ar - (lr - 1.0) * ai) / den
    br = b_re.astype(f32)
    bi = b_im.astype(f32)
    bbr = fr[..., None] * br - fi[..., None] * bi
    bbi = fr[..., None] * bi + fi[..., None] * br
    cr = c_re.astype(f32)
    ci = c_im.astype(f32)

    tau = jnp.arange(CHUNK + 1, dtype=f32)[:, None, None, None]
    pmag = jnp.exp(tau * zr[None])
    pr = pmag * jnp.cos(tau * zi[None])
    pi = pmag * jnp.sin(tau * zi[None])

    er = pr[..., None] * bbr[None] - pi[..., None] * bbi[None]
    ei = pr[..., None] * bbi[None] + pi[..., None] * bbr[None]
    gr = cr[None] * pr[:, :, :, None, :] - ci[None] * pi[:, :, :, None, :]
    gi = cr[None] * pi[:, :, :, None, :] + ci[None] * pr[:, :, :, None, :]

    kk = (jnp.einsum('dgcp,tdgpe->tdgce', cr, er, precision=HIGHEST)
          - jnp.einsum('dgcp,tdgpe->tdgce', ci, ei, precision=HIGHEST))
    kf = kk[:CHUNK, 0]
    kb = kk[:CHUNK, 1]
    dmat = d.astype(f32).reshape(N_GROUPS, SSM_CH)[:, :, None] * jnp.eye(SSM_CH, dtype=f32)[None]
    center = kf[0] + kb[0] + dmat
    kall = jnp.concatenate([kb[:0:-1], center[None], kf[1:]], axis=0)
    jj = jnp.arange(CHUNK)
    lag_idx = jj[None, :] - jj[:, None] + (CHUNK - 1)
    m5 = kall[lag_idx]
    m_intra = jnp.transpose(m5, (2, 0, 4, 1, 3)).reshape(N_GROUPS, CHUNK_VEC, CHUNK_VEC)

    ef_r = er[:CHUNK, 0][::-1]
    ef_i = ei[:CHUNK, 0][::-1]
    eb_r = er[:CHUNK, 1]
    eb_i = ei[:CHUNK, 1]

    def _rows(e):
        return jnp.transpose(e, (1, 0, 3, 2)).reshape(N_GROUPS, CHUNK_VEC, SSM_STATE)

    w_state = jnp.concatenate([_rows(ef_r), _rows(ef_i), _rows(ef_i), _rows(ef_r),
                               _rows(eb_r), _rows(eb_i), _rows(eb_i), _rows(eb_r)], axis=2)

    def _cols(g):
        return jnp.transpose(g, (1, 3, 0, 2)).reshape(N_GROUPS, SSM_STATE, CHUNK_VEC)

    gf_r = gr[1:, 0]
    gf_i = gi[1:, 0]
    gb_r = gr[1:, 1][::-1]
    gb_i = gi[1:, 1][::-1]
    w_out = jnp.concatenate([_cols(gf_r), -_cols(gf_i), _cols(gb_r), -_cols(gb_i)], axis=1)

    a_r = pr[CHUNK]
    a_i = pi[CHUNK]
    a1 = jnp.concatenate([a_r, a_r], axis=-1)
    a2 = jnp.concatenate([-a_i, a_i], axis=-1)
    a3 = -a2
    zero = jnp.zeros_like(a1[0])
    coef = jnp.stack([a1[0], a2[0], a3[0], a1[1], a2[1], a3[1], zero, zero], axis=1)
    return (m_intra.astype(jnp.bfloat16), w_state.astype(jnp.bfloat16),
            w_out.astype(jnp.bfloat16), coef)


def _rope_tables():
    half = HEAD_DIM // 2
    inv_freq = ROPE_THETA ** (-np.arange(half, dtype=np.float64) / half)
    ang = np.arange(SEQ, dtype=np.float64)[:, None] * inv_freq[None, :]
    cos, sin = np.cos(ang), np.sin(ang)
    return jnp.asarray(np.concatenate([cos, cos, cos, cos, -sin, sin, -sin, sin], axis=1), jnp.float32)


def _rope_block(xb, cos_t, sin_t, first_half):
    swapped = jnp.where(first_half, pltpu.roll(xb, 96, 1), pltpu.roll(xb, 32, 1))
    return xb * cos_t + swapped * sin_t


COL_Q = (0, Q_COLS)
COL_KV = (COL_Q[1], COL_Q[1] + 2 * KV_COLS)
COL_ZA = (COL_KV[1], COL_KV[1] + D_ATTN)
COL_U = (COL_ZA[1], COL_ZA[1] + D_SSM)
COL_ZS = (COL_U[1], COL_U[1] + D_SSM)


def _silu(z):
    h = 0.5 * z
    return h + h * jnp.tanh(h)


def _inproj_kernel(x_ref, w_ref, rope_ref, q_ref, kv_ref, sz_ref, ut_ref, uslab_ref):
    step = pl.program_id(0)
    part = step % STEPS_PER_REGROUP
    lane = lax.broadcasted_iota(jnp.int32, (SUB_ROWS, LANES), 1)
    first_half = (lane % HEAD_DIM) < (HEAD_DIM // 2)

    def proj(xb, cols):
        return jnp.dot(xb, w_ref[:, cols[0]:cols[1]], preferred_element_type=jnp.float32)

    for sub in range(TOK_BLOCK // SUB_ROWS):
        rows = slice(sub * SUB_ROWS, (sub + 1) * SUB_ROWS)
        xb = x_ref[rows, :].astype(jnp.bfloat16)
        cos_t = rope_ref[rows, :LANES]
        sin_t = rope_ref[rows, LANES:]
        q = proj(xb, COL_Q)
        kv = proj(xb, COL_KV)
        q_blocks = [_rope_block(q[:, v * LANES:(v + 1) * LANES], cos_t, sin_t, first_half)
                    for v in range(Q_COLS // LANES)]
        q_ref[rows, :] = jnp.concatenate(q_blocks, axis=1).astype(q_ref.dtype)
        za = proj(xb, COL_ZA)
        k_rot = _rope_block(kv[:, :LANES], cos_t, sin_t, first_half)
        kv_ref[rows, :] = jnp.concatenate([k_rot, kv[:, LANES:]], axis=1).astype(kv_ref.dtype)
        u = proj(xb, COL_U)
        sz_ref[rows, :D_ATTN] = _silu(za).astype(sz_ref.dtype)
        zs = proj(xb, COL_ZS)
        for v in range(N_LANE_BLOCKS):
            dst = pl.ds(pl.multiple_of(part * TOK_BLOCK + sub * SUB_ROWS, SUB_ROWS), SUB_ROWS)
            uslab_ref[v, dst, :] = u[:, v * LANES:(v + 1) * LANES]
        sz_ref[rows, D_ATTN:] = _silu(zs).astype(sz_ref.dtype)

    @pl.when(part == STEPS_PER_REGROUP - 1)
    def _():
        for j in range(CHUNK):
            for v in range(N_LANE_BLOCKS):
                uj = uslab_ref[v, pl.ds(j, LANES, stride=CHUNK), :]
                ujt = jnp.transpose(uj).reshape(GROUPS_PER_LANE_BLOCK, SSM_CH, LANES)
                g0 = v * GROUPS_PER_LANE_BLOCK
                ut_ref[g0:g0 + GROUPS_PER_LANE_BLOCK, j * SSM_CH:(j + 1) * SSM_CH, :] = ujt.astype(ut_ref.dtype)


def _inproj(x2, w, rope):
    n_steps = BATCH * SEQ // TOK_BLOCK
    blocks_per_seq = SEQ // TOK_BLOCK

    def tok(cols):
        return pl.BlockSpec((TOK_BLOCK, cols), lambda s: (s, 0))

    return pl.pallas_call(
        _inproj_kernel,
        grid=(n_steps,),
        in_specs=[tok(D_MODEL), pl.BlockSpec(w.shape, lambda s: (0, 0)),
                  pl.BlockSpec((TOK_BLOCK, 2 * LANES), lambda s: (s % blocks_per_seq, 0))],
        out_specs=[tok(Q_COLS), tok(2 * KV_COLS), tok(D_ATTN + D_SSM),
                   pl.BlockSpec((N_GROUPS, CHUNK_VEC, LANES), lambda s: (0, 0, s // STEPS_PER_REGROUP))],
        out_shape=[jax.ShapeDtypeStruct((BATCH * SEQ, Q_COLS), jnp.bfloat16),
                   jax.ShapeDtypeStruct((BATCH * SEQ, 2 * KV_COLS), jnp.bfloat16),
                   jax.ShapeDtypeStruct((BATCH * SEQ, D_ATTN + D_SSM), jnp.bfloat16),
                   jax.ShapeDtypeStruct((N_GROUPS, CHUNK_VEC, ROWS), jnp.bfloat16)],
        scratch_shapes=[pltpu.VMEM((N_LANE_BLOCKS, REGROUP_TOK, LANES), jnp.float32)],
        compiler_params=pltpu.CompilerParams(dimension_semantics=("arbitrary",),
                                             vmem_limit_bytes=VMEM_LIMIT),
        name="inproj",
    )(x2, w, rope)


ATTN_SUB = 4
ATTN_STEP = ATTN_SUB * BLOCK
N_KEYS = 3 * BLOCK


def _attn_kernel(q_ref, kp_ref, kc_ref, kn_ref, sink_ref, y_ref):
    i = pl.program_id(1)
    kvw = jnp.concatenate([kp_ref[...], kc_ref[...], kn_ref[...]], axis=0).astype(jnp.float32)

    n_win = kvw.shape[0]
    lane_w = lax.broadcasted_iota(jnp.int32, (n_win, LANES), 1)
    k_lo, k_hi, vt_lo, vt_hi = [], [], [], []
    for hk in range(N_KV_HEADS):
        in_head = (lane_w >= hk * HEAD_DIM) & (lane_w < (hk + 1) * HEAD_DIM)
        k_nat = jnp.where(in_head, kvw[:, :LANES], 0.0)
        v_nat = jnp.where(in_head, kvw[:, LANES:], 0.0)
        k_oth = pltpu.roll(k_nat, HEAD_DIM, 1)
        v_oth = pltpu.roll(v_nat, HEAD_DIM, 1)
        pairs = ((k_nat, k_oth), (v_nat, v_oth)) if hk == 0 else ((k_oth, k_nat), (v_oth, v_nat))
        k_lo.append(pairs[0][0].astype(jnp.bfloat16))
        k_hi.append(pairs[0][1].astype(jnp.bfloat16))
        vt_lo.append(jnp.transpose(pairs[1][0]).astype(jnp.bfloat16))
        vt_hi.append(jnp.transpose(pairs[1][1]).astype(jnp.bfloat16))

    c = lax.broadcasted_iota(jnp.int32, (N_KEYS, BLOCK), 0)
    r = lax.broadcasted_iota(jnp.int32, (N_KEYS, BLOCK), 1)
    band = (c >= r) & (c - r <= 2 * WINDOW)
    ones_row = lax.broadcasted_iota(jnp.int32, (16, 2 * N_KEYS), 0)
    ones_col = lax.broadcasted_iota(jnp.int32, (16, 2 * N_KEYS), 1)
    ones = jnp.where(((ones_row == 0) & (ones_col < N_KEYS)) | ((ones_row == 1) & (ones_col >= N_KEYS)),
                     1.0, 0.0).astype(jnp.bfloat16)
    row_o = lax.broadcasted_iota(jnp.int32, (LANES, 2 * BLOCK), 0)
    n_blocks = SEQ // BLOCK

    def scores(t, hk):
        blk = i * ATTN_SUB + t
        lo = jnp.where(blk == 0, BLOCK, 0)
        hi = jnp.where(blk == n_blocks - 1, 2 * BLOCK, N_KEYS)
        valid1 = band & (c >= lo) & (c < hi)
        valid = jnp.concatenate([valid1, valid1], axis=1)
        win = slice(t * BLOCK, t * BLOCK + N_KEYS)
        q = q_ref[t * BLOCK:(t + 1) * BLOCK, (2 * hk) * LANES:(2 * hk + 2) * LANES]
        q = q.astype(jnp.float32) * (HEAD_DIM ** -0.5)
        qq = jnp.concatenate([q[:, :LANES], q[:, LANES:]], axis=0).astype(jnp.bfloat16)
        kcat = jnp.concatenate([k_lo[hk][win], k_hi[hk][win]], axis=0)
        st = lax.dot_general(kcat, qq, (((1,), (1,)), ((), ())),
                             preferred_element_type=jnp.float32)
        sps, ms, sinks = [], [], []
        for par in range(2):
            sp = jnp.where(valid, st[par * N_KEYS:(par + 1) * N_KEYS, :], NEG_INF)
            h0 = 4 * hk + par
            h1 = h0 + 2
            sink = jnp.concatenate([sink_ref[h0:h0 + 1, :], sink_ref[h1:h1 + 1, :]], axis=1)
            sps.append(sp)
            ms.append(jnp.maximum(jnp.max(sp, axis=0, keepdims=True), sink))
            sinks.append(sink)
        return sps, ms, sinks

    def weights(sps, ms, sinks):
        return [jnp.exp(sp - m).astype(jnp.bfloat16) for sp, m in zip(sps, ms)], ms, sinks

    def finish(t, hk, ps, ms, sinks):
        win = slice(t * BLOCK, t * BLOCK + N_KEYS)
        vt = jnp.concatenate([jnp.concatenate([vt_lo[hk][:, win], vt_hi[hk][:, win]], axis=1), ones], axis=0)
        ot = jnp.dot(vt, jnp.concatenate(ps, axis=0), preferred_element_type=jnp.float32)
        inv0 = 1.0 / (ot[LANES:LANES + 1, :] + jnp.exp(sinks[0] - ms[0]))
        inv1 = 1.0 / (ot[LANES + 1:LANES + 2, :] + jnp.exp(sinks[1] - ms[1]))
        o = jnp.transpose(ot[:LANES, :] * jnp.where(row_o < HEAD_DIM, inv0, inv1))
        rows = slice(t * BLOCK, (t + 1) * BLOCK)
        y_ref[rows, (2 * hk) * LANES:(2 * hk + 1) * LANES] = o[:BLOCK].astype(y_ref.dtype)
        y_ref[rows, (2 * hk + 1) * LANES:(2 * hk + 2) * LANES] = o[BLOCK:].astype(y_ref.dtype)

    work = [(t, hk) for t in range(ATTN_SUB) for hk in range(N_KV_HEADS)]
    stage_a = {0: scores(*work[0])}
    stage_b = {}
    for n in range(len(work) + 1):
        if n + 1 < len(work):
            stage_a[n + 1] = scores(*work[n + 1])
        if n < len(work):
            stage_b[n] = weights(*stage_a.pop(n))
        if n >= 1:
            finish(*work[n - 1], *stage_b.pop(n - 1))


def _attention(q, kv, sink_tab):
    nb = SEQ // BLOCK
    return pl.pallas_call(
        _attn_kernel,
        grid=(BATCH, SEQ // ATTN_STEP),
        in_specs=[pl.BlockSpec((None, ATTN_STEP, Q_COLS), lambda b, i: (b, i, 0)),
                  pl.BlockSpec((None, BLOCK, 2 * KV_COLS),
                               lambda b, i: (b, jnp.maximum(i * ATTN_SUB - 1, 0), 0)),
                  pl.BlockSpec((None, ATTN_STEP, 2 * KV_COLS), lambda b, i: (b, i, 0)),
                  pl.BlockSpec((None, BLOCK, 2 * KV_COLS),
                               lambda b, i: (b, jnp.minimum((i + 1) * ATTN_SUB, nb - 1), 0)),
                  pl.BlockSpec((N_Q_HEADS, LANES), lambda b, i: (0, 0))],
        out_specs=pl.BlockSpec((None, ATTN_STEP, D_ATTN), lambda b, i: (b, i, 0)),
        out_shape=jax.ShapeDtypeStruct((BATCH, SEQ, D_ATTN), jnp.bfloat16),
        compiler_params=pltpu.CompilerParams(dimension_semantics=("arbitrary", "arbitrary"),
                                             vmem_limit_bytes=VMEM_LIMIT),
        name="attention",
    )(q, kv, kv, kv, sink_tab)


SSM_GB = 4


def _ssm_kernel(ut_ref, m_ref, ws_ref, wo_ref, coef_ref, gt_ref,
                sf_ref, sfs_ref, sb_ref, sbs_ref, xf_ref, xb_ref, u_ref):
    for gi in range(SSM_GB):
        u = jnp.transpose(ut_ref[gi].astype(jnp.float32)).astype(jnp.bfloat16)
        u_ref[gi] = u
        s = jnp.dot(u, ws_ref[gi], preferred_element_type=jnp.float32)
        for b in range(BATCH):
            rows = slice(b * N_CHUNKS, (b + 1) * N_CHUNKS)
            dst = pl.ds(b * SLAB_PITCH + 8, N_CHUNKS)
            sf_ref[gi, dst, :] = s[rows, 0 * LANES:1 * LANES]
            sfs_ref[gi, dst, :] = s[rows, 1 * LANES:2 * LANES]
            sb_ref[gi, dst, :] = s[rows, 2 * LANES:3 * LANES]
            sbs_ref[gi, dst, :] = s[rows, 3 * LANES:4 * LANES]

    zero = jnp.zeros((BATCH, LANES), jnp.float32)
    coefs = []
    for gi in range(SSM_GB):
        coefs.append([jnp.broadcast_to(coef_ref[gi, r:r + 1, :], (BATCH, LANES)) for r in range(6)])
        xf_ref[gi, pl.ds(8, BATCH, stride=SLAB_PITCH), :] = zero
        xb_ref[gi, pl.ds(8 + N_CHUNKS - 1, BATCH, stride=SLAB_PITCH), :] = zero

    def rows_at(r):
        return pl.ds(r, BATCH, stride=SLAB_PITCH)

    def step(k, carry):
        kb = N_CHUNKS - 1 - k
        out = []
        for gi in range(SSM_GB):
            a1f, a2f, a3f, a1b, a2b, a3b = coefs[gi]
            xf, xfs, xb, xbs = carry[4 * gi:4 * gi + 4]
            nxf = a1f * xf + a2f * xfs + sf_ref[gi, rows_at(8 + k), :]
            nxfs = a1f * xfs + a3f * xf + sfs_ref[gi, rows_at(8 + k), :]
            xf_ref[gi, rows_at(9 + k), :] = nxf
            nxb = a1b * xb + a2b * xbs + sb_ref[gi, rows_at(8 + kb), :]
            nxbs = a1b * xbs + a3b * xb + sbs_ref[gi, rows_at(8 + kb), :]
            xb_ref[gi, rows_at(7 + kb), :] = nxb
            out += [nxf, nxfs, nxb, nxbs]
        return tuple(out)

    lax.fori_loop(0, N_CHUNKS, step, (zero,) * (4 * SSM_GB))

    def chunk_out(gi):
        xin_f = jnp.concatenate([xf_ref[gi, pl.ds(b * SLAB_PITCH + 8, N_CHUNKS), :] for b in range(BATCH)], axis=0)
        xin_b = jnp.concatenate([xb_ref[gi, pl.ds(b * SLAB_PITCH + 8, N_CHUNKS), :] for b in range(BATCH)], axis=0)
        lhs = jnp.concatenate([u_ref[gi], xin_f.astype(jnp.bfloat16), xin_b.astype(jnp.bfloat16)], axis=1)
        rhs = jnp.concatenate([m_ref[gi], wo_ref[gi]], axis=0)
        return jnp.dot(lhs, rhs, preferred_element_type=jnp.float32)

    y = chunk_out(0)
    for gi in range(SSM_GB):
        y_next = chunk_out(gi + 1) if gi + 1 < SSM_GB else None
        gt_ref[gi] = jnp.transpose(jax.nn.gelu(y)).astype(gt_ref.dtype)
        y = y_next


def _ssm(ut, m_intra, w_state, w_out, coef):
    def per_group(shape):
        return pl.BlockSpec((SSM_GB,) + shape, lambda g: (g, 0, 0))

    slab = pltpu.VMEM((SSM_GB, SLAB_ROWS, LANES), jnp.float32)
    return pl.pallas_call(
        _ssm_kernel,
        grid=(N_GROUPS // SSM_GB,),
        in_specs=[per_group((CHUNK_VEC, ROWS)), per_group((CHUNK_VEC, CHUNK_VEC)),
                  per_group((CHUNK_VEC, 4 * LANES)), per_group((2 * STATE_VEC, CHUNK_VEC)),
                  per_group((8, LANES))],
        out_specs=per_group((CHUNK_VEC, ROWS)),
        out_shape=jax.ShapeDtypeStruct((N_GROUPS, CHUNK_VEC, ROWS), jnp.bfloat16),
        scratch_shapes=[slab] * 6 + [pltpu.VMEM((SSM_GB, ROWS, CHUNK_VEC), jnp.bfloat16)],
        compiler_params=pltpu.CompilerParams(dimension_semantics=("arbitrary",),
                                             vmem_limit_bytes=VMEM_LIMIT),
        name="ssm",
    )(ut, m_intra, w_state, w_out, coef)


FINAL_SUB_ROWS = 256


def _unit_rms(v):
    return v * lax.rsqrt(jnp.mean(v * v, axis=-1, keepdims=True) + NORM_EPS)


def _final_kernel(x_ref, ya_ref, sz_ref, gt_ref, wglu_ref, bglu_ref,
                  woa_ref, wos_ref, lng_ref, lnb_ref, o_ref, gslab_ref):
    step = pl.program_id(0)
    part = step % STEPS_PER_REGROUP

    @pl.when(part == 0)
    def _():
        for j in range(CHUNK):
            for v in range(N_LANE_BLOCKS):
                g0 = v * GROUPS_PER_LANE_BLOCK
                gjt = gt_ref[g0:g0 + GROUPS_PER_LANE_BLOCK, j * SSM_CH:(j + 1) * SSM_CH, :].astype(jnp.float32)
                gj = jnp.transpose(gjt.reshape(LANES, LANES))
                gslab_ref[v, pl.ds(j, LANES, stride=CHUNK), :] = gj

    def rows_of(sub):
        return slice(sub * FINAL_SUB_ROWS, (sub + 1) * FINAL_SUB_ROWS)

    def gate_in(sub):
        rows = rows_of(sub)
        ya = ya_ref[rows, :].astype(jnp.float32) * sz_ref[rows, :D_ATTN].astype(jnp.float32)
        ya = _unit_rms(ya).astype(jnp.bfloat16)
        src = pl.ds(pl.multiple_of(part * TOK_BLOCK + sub * FINAL_SUB_ROWS, FINAL_SUB_ROWS), FINAL_SUB_ROWS)
        g = jnp.concatenate([gslab_ref[v, src, :] for v in range(N_LANE_BLOCKS)], axis=1)
        return ya, g

    def glu(sub, ya, g):
        rows = rows_of(sub)
        gate = jnp.dot(g.astype(jnp.bfloat16), wglu_ref[...], preferred_element_type=jnp.float32) + bglu_ref[...]
        ys = g * jax.nn.sigmoid(gate)
        ys = ys * sz_ref[rows, D_ATTN:].astype(jnp.float32)
        return ya, _unit_rms(ys).astype(jnp.bfloat16)

    def project(sub, ya, ys):
        rows = rows_of(sub)
        out = (jnp.dot(ya, woa_ref[...], preferred_element_type=jnp.float32)
               + jnp.dot(ys, wos_ref[...], preferred_element_type=jnp.float32))
        h = x_ref[rows, :] + out
        mu = jnp.mean(h, axis=-1, keepdims=True)
        hc = h - mu
        var = jnp.mean(hc * hc, axis=-1, keepdims=True)
        o_ref[rows, :] = hc * lax.rsqrt(var + NORM_EPS / DEEPNORM_ALPHA ** 2) * lng_ref[...] + lnb_ref[...]

    n_sub = TOK_BLOCK // FINAL_SUB_ROWS
    stage_a, stage_b = {}, {}
    for n in range(n_sub + 2):
        if n < n_sub:
            stage_a[n] = gate_in(n)
        if 1 <= n <= n_sub:
            stage_b[n - 1] = glu(n - 1, *stage_a.pop(n - 1))
        if n >= 2:
            project(n - 2, *stage_b.pop(n - 2))


def _final(x2, ya, sz, gt, wglu, bglu, woa, wos, lng, lnb):
    n_steps = BATCH * SEQ // TOK_BLOCK

    def tok(cols):
        return pl.BlockSpec((TOK_BLOCK, cols), lambda s: (s, 0))

    def whole(shape):
        return pl.BlockSpec(shape, lambda s: (0,) * len(shape))

    return pl.pallas_call(
        _final_kernel,
        grid=(n_steps,),
        in_specs=[tok(D_MODEL), tok(D_ATTN), tok(D_ATTN + D_SSM),
                  pl.BlockSpec((N_GROUPS, CHUNK_VEC, LANES), lambda s: (0, 0, s // STEPS_PER_REGROUP)),
                  whole(wglu.shape), whole(bglu.shape),
                  whole(woa.shape), whole(wos.shape), whole(lng.shape), whole(lnb.shape)],
        out_specs=tok(D_MODEL),
        out_shape=jax.ShapeDtypeStruct((BATCH * SEQ, D_MODEL), jnp.float32),
        scratch_shapes=[pltpu.VMEM((N_LANE_BLOCKS, REGROUP_TOK, LANES), jnp.float32)],
        compiler_params=pltpu.CompilerParams(dimension_semantics=("arbitrary",),
                                             vmem_limit_bytes=VMEM_LIMIT),
        name="final",
    )(x2, ya, sz, gt, wglu, bglu, woa, wos, lng, lnb)


def kernel(x, w_in, attn_sink, ssm_a_re, ssm_a_im, ssm_log_dt, ssm_b_re, ssm_b_im, ssm_c_re, ssm_c_im,
           ssm_d, w_glu, b_glu, norm_attn_g, norm_ssm_g, w_out, ln_g, ln_b):
    assert x.shape == (BATCH, SEQ, D_MODEL) and w_in.shape[0] == 1
    bf16 = jnp.bfloat16
    f32 = jnp.float32
    m_intra, w_state, w_so, coef = _ssm_tables(ssm_a_re[0], ssm_a_im[0], ssm_log_dt[0], ssm_b_re[0],
                                                ssm_b_im[0], ssm_c_re[0], ssm_c_im[0], ssm_d[0])
    rope = _rope_tables()
    sink_tab = jnp.broadcast_to(attn_sink[0].astype(f32)[:, None], (N_Q_HEADS, LANES))

    x2 = x.reshape(BATCH * SEQ, D_MODEL)
    q, kv, sz, ut = _inproj(x2, w_in[0].astype(bf16), rope)
    ya = _attention(q.reshape(BATCH, SEQ, Q_COLS), kv.reshape(BATCH, SEQ, 2 * KV_COLS), sink_tab)
    gt = _ssm(ut, m_intra, w_state, w_so, coef)

    gains = jnp.concatenate([norm_attn_g[0], norm_ssm_g[0]]).astype(f32)
    wo = (w_out[0].astype(f32) * (gains / DEEPNORM_ALPHA)[:, None]).astype(bf16)
    row = lambda v: v[0].astype(f32)[None, :]
    out = _final(x2, ya.reshape(BATCH * SEQ, D_ATTN), sz, gt, w_glu[0].astype(bf16), row(b_glu),
                 wo[:D_ATTN], wo[D_ATTN:], row(ln_g), row(ln_b))
    return out.reshape(BATCH, SEQ, D_MODEL)
```

```python
import jax
import jax.numpy as jnp
import numpy as np
from jax import lax
from jax.experimental import pallas as pl
from jax.experimental.pallas import tpu as pltpu

D_MODEL = 1024
BATCH = 4
SEQ = 4096
D_ATTN = 512
D_SSM = 512
HEAD_DIM = 64
N_Q_HEADS = 8
N_KV_HEADS = 2
WINDOW = 128
BLOCK = 128
ROPE_THETA = 10000.0
SSM_CH = 16
N_GROUPS = 32
SSM_STATE = 64
NORM_EPS = 1e-5
NEG_INF = -1e30
DEEPNORM_ALPHA = 2.0 ** 0.25

Q_COLS = N_Q_HEADS * HEAD_DIM
KV_COLS = N_KV_HEADS * HEAD_DIM
CHUNK = 16
N_CHUNKS = SEQ // CHUNK
CHUNK_VEC = CHUNK * SSM_CH
STATE_VEC = 2 * SSM_STATE
LANES = 128
ROWS = BATCH * N_CHUNKS
N_ROW_BLOCKS = ROWS // LANES
SLAB_PITCH = N_CHUNKS + 8
SLAB_ROWS = BATCH * SLAB_PITCH + 8
TOK_BLOCK = 1024
REGROUP_TOK = LANES * CHUNK
STEPS_PER_REGROUP = REGROUP_TOK // TOK_BLOCK
SUB_ROWS = 512
N_LANE_BLOCKS = D_SSM // LANES
GROUPS_PER_LANE_BLOCK = LANES // SSM_CH
VMEM_LIMIT = 56 * 1024 * 1024
HIGHEST = lax.Precision.HIGHEST


LAG_ROWS = 512
PREP_GB = 4


def _prep_kernel(arow_ref, bt_ref, c_ref, dt_ref, m_ref, ws_ref, wo_ref, coef_ref):
    for gi in range(PREP_GB):
        _prep_group(gi, arow_ref, bt_ref, c_ref, dt_ref, m_ref, ws_ref, wo_ref, coef_ref)


def _prep_group(gi, arow_ref, bt_ref, c_ref, dt_ref, m_ref, ws_ref, wo_ref, coef_ref):
    f32 = jnp.float32
    lo16 = lax.broadcasted_iota(jnp.int32, (SSM_CH, LANES), 1) < SSM_STATE
    lo1 = lax.broadcasted_iota(jnp.int32, (1, LANES), 1) < SSM_STATE
    arow = arow_ref[gi]

    lag_tabs, kws, ws_parts, coef_rows = [], [], [], []
    for d in range(2):
        ar = arow[3 * d:3 * d + 1, :]
        ai = arow[3 * d + 1:3 * d + 2, :]
        dt = jnp.exp(arow[3 * d + 2:3 * d + 3, :])
        zr = dt * ar
        zi = dt * ai
        mag = jnp.exp(zr)
        lr = mag * jnp.cos(zi)
        li = mag * jnp.sin(zi)
        den = ar * ar + ai * ai
        fr = ((lr - 1.0) * ar + li * ai) / den
        fi = (li * ar - (lr - 1.0) * ai) / den
        btr = bt_ref[0, d, gi]
        bti = bt_ref[1, d, gi]
        bbr = fr * btr - fi * bti
        bbi = fr * bti + fi * btr
        cr = c_ref[0, d, gi]
        ci = c_ref[1, d, gi]

        pr = [jnp.ones((1, LANES), f32)]
        pi = [jnp.zeros((1, LANES), f32)]
        for _ in range(CHUNK):
            pr.append(pr[-1] * lr - pi[-1] * li)
            pi.append(pr[-2] * li + pi[-1] * lr)
        pa = [jnp.where(lo1, r, i) for r, i in zip(pr, pi)]
        pb = [jnp.where(lo1, -i, r) for r, i in zip(pr, pi)]
        pc = [jnp.where(lo1, i, r) for r, i in zip(pr, pi)]
        pd = [jnp.where(lo1, r, -i) for r, i in zip(pr, pi)]

        def row(t, tab):
            return tab[t]

        order = range(CHUNK + 1) if d == 0 else range(CHUNK, -1, -1)
        blocks = [cr * row(t, pa) + ci * row(t, pb) for t in order]
        blocks.append(jnp.zeros((LAG_ROWS - (CHUNK + 1) * SSM_CH, LANES), f32))
        lag_tab = jnp.concatenate(blocks, axis=0)
        lag_tabs.append(lag_tab)
        lhs = jnp.where(lo16, bbr, -bbi)
        kws.append(lax.dot_general(lhs, lag_tab, (((1,), (1,)), ((), ())),
                                   precision=HIGHEST, preferred_element_type=f32))

        parts = []
        for j in range(CHUNK):
            t = CHUNK - 1 - j if d == 0 else j
            parts.append(jnp.concatenate([bbr * row(t, pa) + bbi * row(t, pb),
                                          bbr * row(t, pc) + bbi * row(t, pd)], axis=1))
        ws_parts.append(jnp.concatenate(parts, axis=0))

        a1 = pr[CHUNK]
        a2 = jnp.where(lo1, -pi[CHUNK], pi[CHUNK])
        coef_rows += [a1, a2, -a2]

    ws_ref[gi] = jnp.concatenate(ws_parts, axis=1).astype(ws_ref.dtype)
    coef_ref[gi] = jnp.concatenate(coef_rows + [jnp.zeros((2, LANES), f32)], axis=0)

    sign = jnp.where(lax.broadcasted_iota(jnp.int32, (LANES, CHUNK_VEC), 0) < SSM_STATE, 1.0, -1.0)
    wo_f = jnp.transpose(lag_tabs[0][SSM_CH:SSM_CH + CHUNK_VEC, :]) * sign
    wo_b = jnp.transpose(lag_tabs[1][:CHUNK_VEC, :]) * sign
    wo_ref[gi] = jnp.concatenate([wo_f, wo_b], axis=0).astype(wo_ref.dtype)

    sub = lax.broadcasted_iota(jnp.int32, (SSM_CH, CHUNK_VEC), 0)
    lane = lax.broadcasted_iota(jnp.int32, (SSM_CH, CHUNK_VEC), 1)
    dtile = jnp.broadcast_to(dt_ref[gi], (SSM_CH, CHUNK_VEC))
    for jp in range(CHUNK):
        fwd = kws[0] if jp == 0 else pltpu.roll(kws[0], SSM_CH * jp, 1)
        bwd = pltpu.roll(kws[1], (LAG_ROWS - SSM_CH * (CHUNK - jp)) % LAG_ROWS, 1)
        skip = jnp.where((lane // SSM_CH == jp) & (lane % SSM_CH == sub), dtile, 0.0)
        blk = fwd[:, :CHUNK_VEC] + bwd[:, :CHUNK_VEC] + skip
        m_ref[gi, jp * SSM_CH:(jp + 1) * SSM_CH, :] = blk.astype(m_ref.dtype)


def _ssm_tables(a_re, a_im, log_dt, b_re, b_im, c_re, c_im, d):
    f32 = jnp.float32
    dup = lambda v: jnp.concatenate([v, v], axis=-1)
    ldt = jnp.broadcast_to(log_dt.astype(f32)[..., None], a_re.shape)
    zero = jnp.zeros_like(ldt[0])
    arow = dup(jnp.stack([a_re[0], a_im[0], ldt[0], a_re[1], a_im[1], ldt[1], zero, zero], axis=1).astype(f32))
    bt = dup(jnp.swapaxes(jnp.stack([b_re, b_im]).astype(f32), -1, -2))
    cc = dup(jnp.stack([c_re, c_im]).astype(f32))
    dtile = jnp.tile(d.astype(f32).reshape(N_GROUPS, 1, SSM_CH), (1, 1, CHUNK))

    bf16 = jnp.bfloat16
    return pl.pallas_call(
        _prep_kernel,
        grid=(N_GROUPS // PREP_GB,),
        in_specs=[pl.BlockSpec((PREP_GB, 8, LANES), lambda g: (g, 0, 0)),
                  pl.BlockSpec((2, 2, PREP_GB, SSM_CH, LANES), lambda g: (0, 0, g, 0, 0)),
                  pl.BlockSpec((2, 2, PREP_GB, SSM_CH, LANES), lambda g: (0, 0, g, 0, 0)),
                  pl.BlockSpec((PREP_GB, 1, CHUNK_VEC), lambda g: (g, 0, 0))],
        out_specs=[pl.BlockSpec((PREP_GB, CHUNK_VEC, CHUNK_VEC), lambda g: (g, 0, 0)),
                   pl.BlockSpec((PREP_GB, CHUNK_VEC, 4 * LANES), lambda g: (g, 0, 0)),
                   pl.BlockSpec((PREP_GB, 2 * STATE_VEC, CHUNK_VEC), lambda g: (g, 0, 0)),
                   pl.BlockSpec((PREP_GB, 8, LANES), lambda g: (g, 0, 0))],
        out_shape=[jax.ShapeDtypeStruct((N_GROUPS, CHUNK_VEC, CHUNK_VEC), bf16),
                   jax.ShapeDtypeStruct((N_GROUPS, CHUNK_VEC, 4 * LANES), bf16),
                   jax.ShapeDtypeStruct((N_GROUPS, 2 * STATE_VEC, CHUNK_VEC), bf16),
                   jax.ShapeDtypeStruct((N_GROUPS, 8, LANES), f32)],
        compiler_params=pltpu.CompilerParams(dimension_semantics=("arbitrary",),
                                             vmem_limit_bytes=VMEM_LIMIT),
        name="ssm_tables",
    )(arow, bt, cc, dtile)


def _ssm_tables_xla(a_re, a_im, log_dt, b_re, b_im, c_re, c_im, d):
    f32 = jnp.float32
    dt = jnp.exp(log_dt.astype(f32))[..., None]
    ar = a_re.astype(f32)
    ai = a_im.astype(f32)
    zr = dt * ar
    zi = dt * ai
    mag = jnp.exp(zr)
    lr = mag * jnp.cos(zi)
    li = mag * jnp.sin(zi)
    den = ar * ar + ai * ai
    fr = ((lr - 1.0) * ar + li * ai) / den
    fi = (li * ar - (lr - 1.0) * ai) / den
    br = b_re.astype(f32)
    bi = b_im.astype(f32)
    bbr = fr[..., None] * br - fi[..., None] * bi
    bbi = fr[..., None] * bi + fi[..., None] * br
    cr = c_re.astype(f32)
    ci = c_im.astype(f32)

    tau = jnp.arange(CHUNK + 1, dtype=f32)[:, None, None, None]
    pmag = jnp.exp(tau * zr[None])
    pr = pmag * jnp.cos(tau * zi[None])
    pi = pmag * jnp.sin(tau * zi[None])

    er = pr[..., None] * bbr[None] - pi[..., None] * bbi[None]
    ei = pr[..., None] * bbi[None] + pi[..., None] * bbr[None]
    gr = cr[None] * pr[:, :, :, None, :] - ci[None] * pi[:, :, :, None, :]
    gi = cr[None] * pi[:, :, :, None, :] + ci[None] * pr[:, :, :, None, :]

    kk = (jnp.einsum('dgcp,tdgpe->tdgce', cr, er, precision=HIGHEST)
          - jnp.einsum('dgcp,tdgpe->tdgce', ci, ei, precision=HIGHEST))
    kf = kk[:CHUNK, 0]
    kb = kk[:CHUNK, 1]
    dmat = d.astype(f32).reshape(N_GROUPS, SSM_CH)[:, :, None] * jnp.eye(SSM_CH, dtype=f32)[None]
    center = kf[0] + kb[0] + dmat
    kall = jnp.concatenate([kb[:0:-1], center[None], kf[1:]], axis=0)
    jj = jnp.arange(CHUNK)
    lag_idx = jj[None, :] - jj[:, None] + (CHUNK - 1)
    m5 = kall[lag_idx]
    m_intra = jnp.transpose(m5, (2, 0, 4, 1, 3)).reshape(N_GROUPS, CHUNK_VEC, CHUNK_VEC)

    ef_r = er[:CHUNK, 0][::-1]
    ef_i = ei[:CHUNK, 0][::-1]
    eb_r = er[:CHUNK, 1]
    eb_i = ei[:CHUNK, 1]

    def _rows(e):
        return jnp.transpose(e, (1, 0, 3, 2)).reshape(N_GROUPS, CHUNK_VEC, SSM_STATE)

    w_state = jnp.concatenate([_rows(ef_r), _rows(ef_i), _rows(ef_i), _rows(ef_r),
                               _rows(eb_r), _rows(eb_i), _rows(eb_i), _rows(eb_r)], axis=2)

    def _cols(g):
        return jnp.transpose(g, (1, 3, 0, 2)).reshape(N_GROUPS, SSM_STATE, CHUNK_VEC)

    gf_r = gr[1:, 0]
    gf_i = gi[1:, 0]
    gb_r = gr[1:, 1][::-1]
    gb_i = gi[1:, 1][::-1]
    w_out = jnp.concatenate([_cols(gf_r), -_cols(gf_i), _cols(gb_r), -_cols(gb_i)], axis=1)

    a_r = pr[CHUNK]
    a_i = pi[CHUNK]
    a1 = jnp.concatenate([a_r, a_r], axis=-1)
    a2 = jnp.concatenate([-a_i, a_i], axis=-1)
    a3 = -a2
    zero = jnp.zeros_like(a1[0])
    coef = jnp.stack([a1[0], a2[0], a3[0], a1[1], a2[1], a3[1], zero, zero], axis=1)
    return (m_intra.astype(jnp.bfloat16), w_state.astype(jnp.bfloat16),
            w_out.astype(jnp.bfloat16), coef)


def _rope_tables():
    half = HEAD_DIM // 2
    inv_freq = ROPE_THETA ** (-np.arange(half, dtype=np.float64) / half)
    ang = np.arange(SEQ, dtype=np.float64)[:, None] * inv_freq[None, :]
    cos, sin = np.cos(ang), np.sin(ang)
    return jnp.asarray(np.concatenate([cos, cos, cos, cos, -sin, sin, -sin, sin], axis=1), jnp.float32)


def _rope_block(xb, cos_t, sin_t, first_half):
    swapped = jnp.where(first_half, pltpu.roll(xb, 96, 1), pltpu.roll(xb, 32, 1))
    return xb * cos_t + swapped * sin_t


COL_Q = (0, Q_COLS)
COL_KV = (COL_Q[1], COL_Q[1] + 2 * KV_COLS)
COL_ZA = (COL_KV[1], COL_KV[1] + D_ATTN)
COL_U = (COL_ZA[1], COL_ZA[1] + D_SSM)
COL_ZS = (COL_U[1], COL_U[1] + D_SSM)


def _silu(z):
    h = 0.5 * z
    return h + h * jnp.tanh(h)


def _inproj_kernel(x_ref, w_ref, rope_ref, q_ref, kv_ref, sz_ref, ut_ref, uslab_ref):
    step = pl.program_id(0)
    part = step % STEPS_PER_REGROUP
    lane = lax.broadcasted_iota(jnp.int32, (SUB_ROWS, LANES), 1)
    first_half = (lane % HEAD_DIM) < (HEAD_DIM // 2)

    def proj(xb, cols):
        return jnp.dot(xb, w_ref[:, cols[0]:cols[1]], preferred_element_type=jnp.float32)

    for sub in range(TOK_BLOCK // SUB_ROWS):
        rows = slice(sub * SUB_ROWS, (sub + 1) * SUB_ROWS)
        xb = x_ref[rows, :].astype(jnp.bfloat16)
        cos_t = rope_ref[rows, :LANES]
        sin_t = rope_ref[rows, LANES:]
        q = proj(xb, COL_Q)
        kv = proj(xb, COL_KV)
        q_blocks = [_rope_block(q[:, v * LANES:(v + 1) * LANES], cos_t, sin_t, first_half)
                    for v in range(Q_COLS // LANES)]
        q_ref[rows, :] = jnp.concatenate(q_blocks, axis=1).astype(q_ref.dtype)
        za = proj(xb, COL_ZA)
        k_rot = _rope_block(kv[:, :LANES], cos_t, sin_t, first_half)
        kv_ref[rows, :] = jnp.concatenate([k_rot, kv[:, LANES:]], axis=1).astype(kv_ref.dtype)
        u = proj(xb, COL_U)
        sz_ref[rows, :D_ATTN] = _silu(za).astype(sz_ref.dtype)
        zs = proj(xb, COL_ZS)
        for v in range(N_LANE_BLOCKS):
            dst = pl.ds(pl.multiple_of(part * TOK_BLOCK + sub * SUB_ROWS, SUB_ROWS), SUB_ROWS)
            uslab_ref[v, dst, :] = u[:, v * LANES:(v + 1) * LANES]
        sz_ref[rows, D_ATTN:] = _silu(zs).astype(sz_ref.dtype)

    @pl.when(part == STEPS_PER_REGROUP - 1)
    def _():
        for j in range(CHUNK):
            for v in range(N_LANE_BLOCKS):
                uj = uslab_ref[v, pl.ds(j, LANES, stride=CHUNK), :]
                ujt = jnp.transpose(uj).reshape(GROUPS_PER_LANE_BLOCK, SSM_CH, LANES)
                g0 = v * GROUPS_PER_LANE_BLOCK
                ut_ref[g0:g0 + GROUPS_PER_LANE_BLOCK, j * SSM_CH:(j + 1) * SSM_CH, :] = ujt.astype(ut_ref.dtype)


def _inproj(x2, w, rope):
    n_steps = BATCH * SEQ // TOK_BLOCK
    blocks_per_seq = SEQ // TOK_BLOCK

    def tok(cols):
        return pl.BlockSpec((TOK_BLOCK, cols), lambda s: (s, 0))

    return pl.pallas_call(
        _inproj_kernel,
        grid=(n_steps,),
        in_specs=[tok(D_MODEL), pl.BlockSpec(w.shape, lambda s: (0, 0)),
                  pl.BlockSpec((TOK_BLOCK, 2 * LANES), lambda s: (s % blocks_per_seq, 0))],
        out_specs=[tok(Q_COLS), tok(2 * KV_COLS), tok(D_ATTN + D_SSM),
                   pl.BlockSpec((None, N_GROUPS, CHUNK_VEC, LANES), lambda s: (s // STEPS_PER_REGROUP, 0, 0, 0))],
        out_shape=[jax.ShapeDtypeStruct((BATCH * SEQ, Q_COLS), jnp.bfloat16),
                   jax.ShapeDtypeStruct((BATCH * SEQ, 2 * KV_COLS), jnp.bfloat16),
                   jax.ShapeDtypeStruct((BATCH * SEQ, D_ATTN + D_SSM), jnp.bfloat16),
                   jax.ShapeDtypeStruct((N_ROW_BLOCKS, N_GROUPS, CHUNK_VEC, LANES), jnp.bfloat16)],
        scratch_shapes=[pltpu.VMEM((N_LANE_BLOCKS, REGROUP_TOK, LANES), jnp.float32)],
        compiler_params=pltpu.CompilerParams(dimension_semantics=("arbitrary",),
                                             vmem_limit_bytes=VMEM_LIMIT),
        name="inproj",
    )(x2, w, rope)


ATTN_SUB = 4
ATTN_STEP = ATTN_SUB * BLOCK
N_KEYS = 3 * BLOCK


def _attn_kernel(q_ref, kp_ref, kc_ref, kn_ref, sink_ref, y_ref):
    i = pl.program_id(1)
    kvw = jnp.concatenate([kp_ref[...], kc_ref[...], kn_ref[...]], axis=0).astype(jnp.float32)

    n_win = kvw.shape[0]
    lane_w = lax.broadcasted_iota(jnp.int32, (n_win, LANES), 1)
    k_lo, k_hi, vt_lo, vt_hi = [], [], [], []
    for hk in range(N_KV_HEADS):
        in_head = (lane_w >= hk * HEAD_DIM) & (lane_w < (hk + 1) * HEAD_DIM)
        k_nat = jnp.where(in_head, kvw[:, :LANES], 0.0)
        v_nat = jnp.where(in_head, kvw[:, LANES:], 0.0)
        k_oth = pltpu.roll(k_nat, HEAD_DIM, 1)
        v_oth = pltpu.roll(v_nat, HEAD_DIM, 1)
        pairs = ((k_nat, k_oth), (v_nat, v_oth)) if hk == 0 else ((k_oth, k_nat), (v_oth, v_nat))
        k_lo.append(pairs[0][0].astype(jnp.bfloat16))
        k_hi.append(pairs[0][1].astype(jnp.bfloat16))
        vt_lo.append(jnp.transpose(pairs[1][0]).astype(jnp.bfloat16))
        vt_hi.append(jnp.transpose(pairs[1][1]).astype(jnp.bfloat16))

    c = lax.broadcasted_iota(jnp.int32, (N_KEYS, BLOCK), 0)
    r = lax.broadcasted_iota(jnp.int32, (N_KEYS, BLOCK), 1)
    band = (c >= r) & (c - r <= 2 * WINDOW)
    ones_row = lax.broadcasted_iota(jnp.int32, (16, 2 * N_KEYS), 0)
    ones_col = lax.broadcasted_iota(jnp.int32, (16, 2 * N_KEYS), 1)
    ones = jnp.where(((ones_row == 0) & (ones_col < N_KEYS)) | ((ones_row == 1) & (ones_col >= N_KEYS)),
                     1.0, 0.0).astype(jnp.bfloat16)
    row_o = lax.broadcasted_iota(jnp.int32, (LANES, 2 * BLOCK), 0)
    n_blocks = SEQ // BLOCK

    def scores(t, hk):
        blk = i * ATTN_SUB + t
        lo = jnp.where(blk == 0, BLOCK, 0)
        hi = jnp.where(blk == n_blocks - 1, 2 * BLOCK, N_KEYS)
        valid1 = band & (c >= lo) & (c < hi)
        valid = jnp.concatenate([valid1, valid1], axis=1)
        win = slice(t * BLOCK, t * BLOCK + N_KEYS)
        q = q_ref[t * BLOCK:(t + 1) * BLOCK, (2 * hk) * LANES:(2 * hk + 2) * LANES]
        q = q.astype(jnp.float32) * (HEAD_DIM ** -0.5)
        qq = jnp.concatenate([q[:, :LANES], q[:, LANES:]], axis=0).astype(jnp.bfloat16)
        kcat = jnp.concatenate([k_lo[hk][win], k_hi[hk][win]], axis=0)
        st = lax.dot_general(kcat, qq, (((1,), (1,)), ((), ())),
                             preferred_element_type=jnp.float32)
        sps, ms, sinks = [], [], []
        for par in range(2):
            sp = jnp.where(valid, st[par * N_KEYS:(par + 1) * N_KEYS, :], NEG_INF)
            h0 = 4 * hk + par
            h1 = h0 + 2
            sink = jnp.concatenate([sink_ref[h0:h0 + 1, :], sink_ref[h1:h1 + 1, :]], axis=1)
            sps.append(sp)
            ms.append(jnp.maximum(jnp.max(sp, axis=0, keepdims=True), sink))
            sinks.append(sink)
        return sps, ms, sinks

    def weights(sps, ms, sinks):
        return [jnp.exp(sp - m).astype(jnp.bfloat16) for sp, m in zip(sps, ms)], ms, sinks

    def finish(t, hk, ps, ms, sinks):
        win = slice(t * BLOCK, t * BLOCK + N_KEYS)
        vt = jnp.concatenate([jnp.concatenate([vt_lo[hk][:, win], vt_hi[hk][:, win]], axis=1), ones], axis=0)
        ot = jnp.dot(vt, jnp.concatenate(ps, axis=0), preferred_element_type=jnp.float32)
        inv0 = 1.0 / (ot[LANES:LANES + 1, :] + jnp.exp(sinks[0] - ms[0]))
        inv1 = 1.0 / (ot[LANES + 1:LANES + 2, :] + jnp.exp(sinks[1] - ms[1]))
        o = jnp.transpose(ot[:LANES, :] * jnp.where(row_o < HEAD_DIM, inv0, inv1))
        rows = slice(t * BLOCK, (t + 1) * BLOCK)
        y_ref[rows, (2 * hk) * LANES:(2 * hk + 1) * LANES] = o[:BLOCK].astype(y_ref.dtype)
        y_ref[rows, (2 * hk + 1) * LANES:(2 * hk + 2) * LANES] = o[BLOCK:].astype(y_ref.dtype)

    work = [(t, hk) for t in range(ATTN_SUB) for hk in range(N_KV_HEADS)]
    stage_a = {0: scores(*work[0])}
    stage_b = {}
    for n in range(len(work) + 1):
        if n + 1 < len(work):
            stage_a[n + 1] = scores(*work[n + 1])
        if n < len(work):
            stage_b[n] = weights(*stage_a.pop(n))
        if n >= 1:
            finish(*work[n - 1], *stage_b.pop(n - 1))


def _attention(q, kv, sink_tab):
    nb = SEQ // BLOCK
    return pl.pallas_call(
        _attn_kernel,
        grid=(BATCH, SEQ // ATTN_STEP),
        in_specs=[pl.BlockSpec((None, ATTN_STEP, Q_COLS), lambda b, i: (b, i, 0)),
                  pl.BlockSpec((None, BLOCK, 2 * KV_COLS),
                               lambda b, i: (b, jnp.maximum(i * ATTN_SUB - 1, 0), 0)),
                  pl.BlockSpec((None, ATTN_STEP, 2 * KV_COLS), lambda b, i: (b, i, 0)),
                  pl.BlockSpec((None, BLOCK, 2 * KV_COLS),
                               lambda b, i: (b, jnp.minimum((i + 1) * ATTN_SUB, nb - 1), 0)),
                  pl.BlockSpec((N_Q_HEADS, LANES), lambda b, i: (0, 0))],
        out_specs=pl.BlockSpec((None, ATTN_STEP, D_ATTN), lambda b, i: (b, i, 0)),
        out_shape=jax.ShapeDtypeStruct((BATCH, SEQ, D_ATTN), jnp.bfloat16),
        compiler_params=pltpu.CompilerParams(dimension_semantics=("arbitrary", "arbitrary"),
                                             vmem_limit_bytes=VMEM_LIMIT),
        name="attention",
    )(q, kv, kv, kv, sink_tab)


SSM_GB = 4


def _ssm_kernel(ut_ref, m_ref, ws_ref, wo_ref, coef_ref, gt_ref,
                sf_ref, sfs_ref, sb_ref, sbs_ref, xf_ref, xb_ref, u_ref):
    for gi in range(SSM_GB):
        ut = jnp.concatenate([ut_ref[lb, gi] for lb in range(N_ROW_BLOCKS)], axis=1)
        u = jnp.transpose(ut.astype(jnp.float32)).astype(jnp.bfloat16)
        u_ref[gi] = u
        s = jnp.dot(u, ws_ref[gi], preferred_element_type=jnp.float32)
        for b in range(BATCH):
            rows = slice(b * N_CHUNKS, (b + 1) * N_CHUNKS)
            dst = pl.ds(b * SLAB_PITCH + 8, N_CHUNKS)
            sf_ref[gi, dst, :] = s[rows, 0 * LANES:1 * LANES]
            sfs_ref[gi, dst, :] = s[rows, 1 * LANES:2 * LANES]
            sb_ref[gi, dst, :] = s[rows, 2 * LANES:3 * LANES]
            sbs_ref[gi, dst, :] = s[rows, 3 * LANES:4 * LANES]

    zero = jnp.zeros((BATCH, LANES), jnp.float32)
    coefs = []
    for gi in range(SSM_GB):
        coefs.append([jnp.broadcast_to(coef_ref[gi, r:r + 1, :], (BATCH, LANES)) for r in range(6)])
        xf_ref[gi, pl.ds(8, BATCH, stride=SLAB_PITCH), :] = zero
        xb_ref[gi, pl.ds(8 + N_CHUNKS - 1, BATCH, stride=SLAB_PITCH), :] = zero

    def rows_at(r):
        return pl.ds(r, BATCH, stride=SLAB_PITCH)

    def step(k, carry):
        kb = N_CHUNKS - 1 - k
        out = []
        for gi in range(SSM_GB):
            a1f, a2f, a3f, a1b, a2b, a3b = coefs[gi]
            xf, xfs, xb, xbs = carry[4 * gi:4 * gi + 4]
            nxf = a1f * xf + a2f * xfs + sf_ref[gi, rows_at(8 + k), :]
            nxfs = a1f * xfs + a3f * xf + sfs_ref[gi, rows_at(8 + k), :]
            xf_ref[gi, rows_at(9 + k), :] = nxf
            nxb = a1b * xb + a2b * xbs + sb_ref[gi, rows_at(8 + kb), :]
            nxbs = a1b * xbs + a3b * xb + sbs_ref[gi, rows_at(8 + kb), :]
            xb_ref[gi, rows_at(7 + kb), :] = nxb
            out += [nxf, nxfs, nxb, nxbs]
        return tuple(out)

    lax.fori_loop(0, N_CHUNKS, step, (zero,) * (4 * SSM_GB), unroll=2)

    def chunk_out(gi):
        xin_f = jnp.concatenate([xf_ref[gi, pl.ds(b * SLAB_PITCH + 8, N_CHUNKS), :] for b in range(BATCH)], axis=0)
        xin_b = jnp.concatenate([xb_ref[gi, pl.ds(b * SLAB_PITCH + 8, N_CHUNKS), :] for b in range(BATCH)], axis=0)
        lhs = jnp.concatenate([u_ref[gi], xin_f.astype(jnp.bfloat16), xin_b.astype(jnp.bfloat16)], axis=1)
        rhs = jnp.concatenate([m_ref[gi], wo_ref[gi]], axis=0)
        return jnp.dot(lhs, rhs, preferred_element_type=jnp.float32)

    y = chunk_out(0)
    for gi in range(SSM_GB):
        y_next = chunk_out(gi + 1) if gi + 1 < SSM_GB else None
        gt = jnp.transpose(jax.nn.gelu(y)).astype(gt_ref.dtype)
        for lb in range(N_ROW_BLOCKS):
            gt_ref[lb, gi] = gt[:, lb * LANES:(lb + 1) * LANES]
        y = y_next


def _ssm(ut, m_intra, w_state, w_out, coef):
    def per_group(shape):
        return pl.BlockSpec((SSM_GB,) + shape, lambda g: (g, 0, 0))

    lane_blocked = pl.BlockSpec((N_ROW_BLOCKS, SSM_GB, CHUNK_VEC, LANES), lambda g: (0, g, 0, 0))
    slab = pltpu.VMEM((SSM_GB, SLAB_ROWS, LANES), jnp.float32)
    return pl.pallas_call(
        _ssm_kernel,
        grid=(N_GROUPS // SSM_GB,),
        in_specs=[lane_blocked, per_group((CHUNK_VEC, CHUNK_VEC)),
                  per_group((CHUNK_VEC, 4 * LANES)), per_group((2 * STATE_VEC, CHUNK_VEC)),
                  per_group((8, LANES))],
        out_specs=lane_blocked,
        out_shape=jax.ShapeDtypeStruct((N_ROW_BLOCKS, N_GROUPS, CHUNK_VEC, LANES), jnp.bfloat16),
        scratch_shapes=[slab] * 6 + [pltpu.VMEM((SSM_GB, ROWS, CHUNK_VEC), jnp.bfloat16)],
        compiler_params=pltpu.CompilerParams(dimension_semantics=("arbitrary",),
                                             vmem_limit_bytes=VMEM_LIMIT),
        name="ssm",
    )(ut, m_intra, w_state, w_out, coef)


FINAL_SUB_ROWS = 256


def _unit_rms(v):
    return v * lax.rsqrt(jnp.mean(v * v, axis=-1, keepdims=True) + NORM_EPS)


def _final_kernel(x_ref, ya_ref, sz_ref, gt_ref, wglu_ref, bglu_ref,
                  woa_ref, wos_ref, lng_ref, lnb_ref, o_ref, gslab_ref):
    step = pl.program_id(0)
    part = step % STEPS_PER_REGROUP

    @pl.when(part == 0)
    def _():
        for j in range(CHUNK):
            for v in range(N_LANE_BLOCKS):
                g0 = v * GROUPS_PER_LANE_BLOCK
                gjt = gt_ref[g0:g0 + GROUPS_PER_LANE_BLOCK, j * SSM_CH:(j + 1) * SSM_CH, :].astype(jnp.float32)
                gj = jnp.transpose(gjt.reshape(LANES, LANES))
                gslab_ref[v, pl.ds(j, LANES, stride=CHUNK), :] = gj

    def rows_of(sub):
        return slice(sub * FINAL_SUB_ROWS, (sub + 1) * FINAL_SUB_ROWS)

    def gate_in(sub):
        rows = rows_of(sub)
        ya = ya_ref[rows, :].astype(jnp.float32) * sz_ref[rows, :D_ATTN].astype(jnp.float32)
        ya = _unit_rms(ya).astype(jnp.bfloat16)
        src = pl.ds(pl.multiple_of(part * TOK_BLOCK + sub * FINAL_SUB_ROWS, FINAL_SUB_ROWS), FINAL_SUB_ROWS)
        g = jnp.concatenate([gslab_ref[v, src, :] for v in range(N_LANE_BLOCKS)], axis=1)
        return ya, g

    def glu(sub, ya, g):
        rows = rows_of(sub)
        gate = jnp.dot(g.astype(jnp.bfloat16), wglu_ref[...], preferred_element_type=jnp.float32) + bglu_ref[...]
        ys = g * jax.nn.sigmoid(gate)
        ys = ys * sz_ref[rows, D_ATTN:].astype(jnp.float32)
        return ya, _unit_rms(ys).astype(jnp.bfloat16)

    def project(sub, ya, ys):
        return (jnp.dot(ya, woa_ref[...], preferred_element_type=jnp.float32)
                + jnp.dot(ys, wos_ref[...], preferred_element_type=jnp.float32))

    def layer_norm(sub, out):
        rows = rows_of(sub)
        h = x_ref[rows, :] + out
        mu = jnp.mean(h, axis=-1, keepdims=True)
        hc = h - mu
        var = jnp.mean(hc * hc, axis=-1, keepdims=True)
        o_ref[rows, :] = hc * lax.rsqrt(var + NORM_EPS / DEEPNORM_ALPHA ** 2) * lng_ref[...] + lnb_ref[...]

    n_sub = TOK_BLOCK // FINAL_SUB_ROWS
    stage_a, stage_b = {}, {}
    for n in range(n_sub + 2):
        if n < n_sub:
            stage_a[n] = gate_in(n)
        if 1 <= n <= n_sub:
            stage_b[n - 1] = glu(n - 1, *stage_a.pop(n - 1))
        if n >= 2:
            layer_norm(n - 2, project(n - 2, *stage_b.pop(n - 2)))


def _final(x2, ya, sz, gt, wglu, bglu, woa, wos, lng, lnb):
    n_steps = BATCH * SEQ // TOK_BLOCK

    def tok(cols):
        return pl.BlockSpec((TOK_BLOCK, cols), lambda s: (s, 0))

    def whole(shape):
        return pl.BlockSpec(shape, lambda s: (0,) * len(shape))

    return pl.pallas_call(
        _final_kernel,
        grid=(n_steps,),
        in_specs=[tok(D_MODEL), tok(D_ATTN), tok(D_ATTN + D_SSM),
                  pl.BlockSpec((None, N_GROUPS, CHUNK_VEC, LANES), lambda s: (s // STEPS_PER_REGROUP, 0, 0, 0)),
                  whole(wglu.shape), whole(bglu.shape),
                  whole(woa.shape), whole(wos.shape), whole(lng.shape), whole(lnb.shape)],
        out_specs=tok(D_MODEL),
        out_shape=jax.ShapeDtypeStruct((BATCH * SEQ, D_MODEL), jnp.float32),
        scratch_shapes=[pltpu.VMEM((N_LANE_BLOCKS, REGROUP_TOK, LANES), jnp.float32)],
        compiler_params=pltpu.CompilerParams(dimension_semantics=("arbitrary",),
                                             vmem_limit_bytes=VMEM_LIMIT),
        name="final",
    )(x2, ya, sz, gt, wglu, bglu, woa, wos, lng, lnb)


def kernel(x, w_in, attn_sink, ssm_a_re, ssm_a_im, ssm_log_dt, ssm_b_re, ssm_b_im, ssm_c_re, ssm_c_im,
           ssm_d, w_glu, b_glu, norm_attn_g, norm_ssm_g, w_out, ln_g, ln_b):
    assert x.shape == (BATCH, SEQ, D_MODEL) and w_in.shape[0] == 1
    bf16 = jnp.bfloat16
    f32 = jnp.float32
    m_intra, w_state, w_so, coef = _ssm_tables(ssm_a_re[0], ssm_a_im[0], ssm_log_dt[0], ssm_b_re[0],
                                                ssm_b_im[0], ssm_c_re[0], ssm_c_im[0], ssm_d[0])
    rope = _rope_tables()
    sink_tab = jnp.broadcast_to(attn_sink[0].astype(f32)[:, None], (N_Q_HEADS, LANES))

    x2 = x.reshape(BATCH * SEQ, D_MODEL)
    q, kv, sz, ut = _inproj(x2, w_in[0].astype(bf16), rope)
    ya = _attention(q.reshape(BATCH, SEQ, Q_COLS), kv.reshape(BATCH, SEQ, 2 * KV_COLS), sink_tab)
    gt = _ssm(ut, m_intra, w_state, w_so, coef)

    gains = jnp.concatenate([norm_attn_g[0], norm_ssm_g[0]]).astype(f32)
    wo = (w_out[0].astype(f32) * (gains / DEEPNORM_ALPHA)[:, None]).astype(bf16)
    row = lambda v: v[0].astype(f32)[None, :]
    out = _final(x2, ya.reshape(BATCH * SEQ, D_ATTN), sz, gt, w_glu[0].astype(bf16), row(b_glu),
                 wo[:D_ATTN], wo[D_ATTN:], row(ln_g), row(ln_b))
    return out.reshape(BATCH, SEQ, D_MODEL)
```

```python
import jax
import jax.numpy as jnp
import numpy as np
from jax import lax
from jax.experimental import pallas as pl
from jax.experimental.pallas import tpu as pltpu

D_MODEL = 1024
BATCH = 4
SEQ = 4096
D_ATTN = 512
D_SSM = 512
HEAD_DIM = 64
N_Q_HEADS = 8
N_KV_HEADS = 2
WINDOW = 128
BLOCK = 128
ROPE_THETA = 10000.0
SSM_CH = 16
N_GROUPS = 32
SSM_STATE = 64
NORM_EPS = 1e-5
NEG_INF = -1e30
DEEPNORM_ALPHA = 2.0 ** 0.25
LOG2_E = 1.4426950408889634
Q_SCALE = HEAD_DIM ** -0.5 * LOG2_E

Q_COLS = N_Q_HEADS * HEAD_DIM
KV_COLS = N_KV_HEADS * HEAD_DIM
CHUNK = 16
N_CHUNKS = SEQ // CHUNK
CHUNK_VEC = CHUNK * SSM_CH
STATE_VEC = 2 * SSM_STATE
LANES = 128
ROWS = BATCH * N_CHUNKS
N_ROW_BLOCKS = ROWS // LANES
SLAB_PITCH = N_CHUNKS + 8
SLAB_ROWS = BATCH * SLAB_PITCH + 8
TOK_BLOCK = 1024
REGROUP_TOK = LANES * CHUNK
STEPS_PER_REGROUP = REGROUP_TOK // TOK_BLOCK
SUB_ROWS = 512
N_LANE_BLOCKS = D_SSM // LANES
GROUPS_PER_LANE_BLOCK = LANES // SSM_CH
VMEM_LIMIT = 56 * 1024 * 1024
HIGHEST = lax.Precision.HIGHEST


LAG_ROWS = 512
PREP_GB = 4


def _prep_kernel(arow_ref, bt_ref, c_ref, dt_ref, m_ref, ws_ref, wo_ref, coef_ref):
    for gi in range(PREP_GB):
        _prep_group(gi, arow_ref, bt_ref, c_ref, dt_ref, m_ref, ws_ref, wo_ref, coef_ref)


def _prep_group(gi, arow_ref, bt_ref, c_ref, dt_ref, m_ref, ws_ref, wo_ref, coef_ref):
    f32 = jnp.float32
    lo16 = lax.broadcasted_iota(jnp.int32, (SSM_CH, LANES), 1) < SSM_STATE
    lo1 = lax.broadcasted_iota(jnp.int32, (1, LANES), 1) < SSM_STATE
    arow = arow_ref[gi]

    lag_tabs, kws, ws_parts, coef_rows = [], [], [], []
    for d in range(2):
        ar = arow[3 * d:3 * d + 1, :]
        ai = arow[3 * d + 1:3 * d + 2, :]
        dt = jnp.exp(arow[3 * d + 2:3 * d + 3, :])
        zr = dt * ar
        zi = dt * ai
        mag = jnp.exp(zr)
        lr = mag * jnp.cos(zi)
        li = mag * jnp.sin(zi)
        den = ar * ar + ai * ai
        fr = ((lr - 1.0) * ar + li * ai) / den
        fi = (li * ar - (lr - 1.0) * ai) / den
        btr = bt_ref[0, d, gi]
        bti = bt_ref[1, d, gi]
        bbr = fr * btr - fi * bti
        bbi = fr * bti + fi * btr
        cr = c_ref[0, d, gi]
        ci = c_ref[1, d, gi]

        pr = [jnp.ones((1, LANES), f32)]
        pi = [jnp.zeros((1, LANES), f32)]
        for _ in range(CHUNK):
            pr.append(pr[-1] * lr - pi[-1] * li)
            pi.append(pr[-2] * li + pi[-1] * lr)
        pa = [jnp.where(lo1, r, i) for r, i in zip(pr, pi)]
        pb = [jnp.where(lo1, -i, r) for r, i in zip(pr, pi)]
        pc = [jnp.where(lo1, i, r) for r, i in zip(pr, pi)]
        pd = [jnp.where(lo1, r, -i) for r, i in zip(pr, pi)]

        def row(t, tab):
            return tab[t]

        order = range(CHUNK + 1) if d == 0 else range(CHUNK, -1, -1)
        blocks = [cr * row(t, pa) + ci * row(t, pb) for t in order]
        blocks.append(jnp.zeros((LAG_ROWS - (CHUNK + 1) * SSM_CH, LANES), f32))
        lag_tab = jnp.concatenate(blocks, axis=0)
        lag_tabs.append(lag_tab)
        lhs = jnp.where(lo16, bbr, -bbi)
        kws.append(lax.dot_general(lhs, lag_tab, (((1,), (1,)), ((), ())),
                                   precision=HIGHEST, preferred_element_type=f32))

        parts = []
        for j in range(CHUNK):
            t = CHUNK - 1 - j if d == 0 else j
            parts.append(jnp.concatenate([bbr * row(t, pa) + bbi * row(t, pb),
                                          bbr * row(t, pc) + bbi * row(t, pd)], axis=1))
        ws_parts.append(jnp.concatenate(parts, axis=0))

        a1 = pr[CHUNK]
        a2 = jnp.where(lo1, -pi[CHUNK], pi[CHUNK])
        coef_rows += [a1, a2, -a2]

    ws_ref[gi] = jnp.concatenate(ws_parts, axis=1).astype(ws_ref.dtype)
    coef_ref[gi] = jnp.concatenate(coef_rows + [jnp.zeros((2, LANES), f32)], axis=0)

    sign = jnp.where(lax.broadcasted_iota(jnp.int32, (LANES, CHUNK_VEC), 0) < SSM_STATE, 1.0, -1.0)
    wo_f = jnp.transpose(lag_tabs[0][SSM_CH:SSM_CH + CHUNK_VEC, :]) * sign
    wo_b = jnp.transpose(lag_tabs[1][:CHUNK_VEC, :]) * sign
    wo_ref[gi] = jnp.concatenate([wo_f, wo_b], axis=0).astype(wo_ref.dtype)

    sub = lax.broadcasted_iota(jnp.int32, (SSM_CH, CHUNK_VEC), 0)
    lane = lax.broadcasted_iota(jnp.int32, (SSM_CH, CHUNK_VEC), 1)
    dtile = jnp.broadcast_to(dt_ref[gi], (SSM_CH, CHUNK_VEC))
    for jp in range(CHUNK):
        fwd = kws[0] if jp == 0 else pltpu.roll(kws[0], SSM_CH * jp, 1)
        bwd = pltpu.roll(kws[1], (LAG_ROWS - SSM_CH * (CHUNK - jp)) % LAG_ROWS, 1)
        skip = jnp.where((lane // SSM_CH == jp) & (lane % SSM_CH == sub), dtile, 0.0)
        blk = fwd[:, :CHUNK_VEC] + bwd[:, :CHUNK_VEC] + skip
        m_ref[gi, jp * SSM_CH:(jp + 1) * SSM_CH, :] = blk.astype(m_ref.dtype)


def _ssm_tables(a_re, a_im, log_dt, b_re, b_im, c_re, c_im, d):
    f32 = jnp.float32
    dup = lambda v: jnp.concatenate([v, v], axis=-1)
    ldt = jnp.broadcast_to(log_dt.astype(f32)[..., None], a_re.shape)
    zero = jnp.zeros_like(ldt[0])
    arow = dup(jnp.stack([a_re[0], a_im[0], ldt[0], a_re[1], a_im[1], ldt[1], zero, zero], axis=1).astype(f32))
    bt = dup(jnp.swapaxes(jnp.stack([b_re, b_im]).astype(f32), -1, -2))
    cc = dup(jnp.stack([c_re, c_im]).astype(f32))
    dtile = jnp.tile(d.astype(f32).reshape(N_GROUPS, 1, SSM_CH), (1, 1, CHUNK))

    bf16 = jnp.bfloat16
    return pl.pallas_call(
        _prep_kernel,
        grid=(N_GROUPS // PREP_GB,),
        in_specs=[pl.BlockSpec((PREP_GB, 8, LANES), lambda g: (g, 0, 0)),
                  pl.BlockSpec((2, 2, PREP_GB, SSM_CH, LANES), lambda g: (0, 0, g, 0, 0)),
                  pl.BlockSpec((2, 2, PREP_GB, SSM_CH, LANES), lambda g: (0, 0, g, 0, 0)),
                  pl.BlockSpec((PREP_GB, 1, CHUNK_VEC), lambda g: (g, 0, 0))],
        out_specs=[pl.BlockSpec((PREP_GB, CHUNK_VEC, CHUNK_VEC), lambda g: (g, 0, 0)),
                   pl.BlockSpec((PREP_GB, CHUNK_VEC, 4 * LANES), lambda g: (g, 0, 0)),
                   pl.BlockSpec((PREP_GB, 2 * STATE_VEC, CHUNK_VEC), lambda g: (g, 0, 0)),
                   pl.BlockSpec((PREP_GB, 8, LANES), lambda g: (g, 0, 0))],
        out_shape=[jax.ShapeDtypeStruct((N_GROUPS, CHUNK_VEC, CHUNK_VEC), bf16),
                   jax.ShapeDtypeStruct((N_GROUPS, CHUNK_VEC, 4 * LANES), bf16),
                   jax.ShapeDtypeStruct((N_GROUPS, 2 * STATE_VEC, CHUNK_VEC), bf16),
                   jax.ShapeDtypeStruct((N_GROUPS, 8, LANES), f32)],
        compiler_params=pltpu.CompilerParams(dimension_semantics=("arbitrary",),
                                             vmem_limit_bytes=VMEM_LIMIT),
        name="ssm_tables",
    )(arow, bt, cc, dtile)


def _ssm_tables_xla(a_re, a_im, log_dt, b_re, b_im, c_re, c_im, d):
    f32 = jnp.float32
    dt = jnp.exp(log_dt.astype(f32))[..., None]
    ar = a_re.astype(f32)
    ai = a_im.astype(f32)
    zr = dt * ar
    zi = dt * ai
    mag = jnp.exp(zr)
    lr = mag * jnp.cos(zi)
    li = mag * jnp.sin(zi)
    den = ar * ar + ai * ai
    fr = ((lr - 1.0) * ar + li * ai) / den
    fi = (li * ar - (lr - 1.0) * ai) / den
    br = b_re.astype(f32)
    bi = b_im.astype(f32)
    bbr = fr[..., None] * br - fi[..., None] * bi
    bbi = fr[..., None] * bi + fi[..., None] * br
    cr = c_re.astype(f32)
    ci = c_im.astype(f32)

    tau = jnp.arange(CHUNK + 1, dtype=f32)[:, None, None, None]
    pmag = jnp.exp(tau * zr[None])
    pr = pmag * jnp.cos(tau * zi[None])
    pi = pmag * jnp.sin(tau * zi[None])

    er = pr[..., None] * bbr[None] - pi[..., None] * bbi[None]
    ei = pr[..., None] * bbi[None] + pi[..., None] * bbr[None]
    gr = cr[None] * pr[:, :, :, None, :] - ci[None] * pi[:, :, :, None, :]
    gi = cr[None] * pi[:, :, :, None, :] + ci[None] * pr[:, :, :, None, :]

    kk = (jnp.einsum('dgcp,tdgpe->tdgce', cr, er, precision=HIGHEST)
          - jnp.einsum('dgcp,tdgpe->tdgce', ci, ei, precision=HIGHEST))
    kf = kk[:CHUNK, 0]
    kb = kk[:CHUNK, 1]
    dmat = d.astype(f32).reshape(N_GROUPS, SSM_CH)[:, :, None] * jnp.eye(SSM_CH, dtype=f32)[None]
    center = kf[0] + kb[0] + dmat
    kall = jnp.concatenate([kb[:0:-1], center[None], kf[1:]], axis=0)
    jj = jnp.arange(CHUNK)
    lag_idx = jj[None, :] - jj[:, None] + (CHUNK - 1)
    m5 = kall[lag_idx]
    m_intra = jnp.transpose(m5, (2, 0, 4, 1, 3)).reshape(N_GROUPS, CHUNK_VEC, CHUNK_VEC)

    ef_r = er[:CHUNK, 0][::-1]
    ef_i = ei[:CHUNK, 0][::-1]
    eb_r = er[:CHUNK, 1]
    eb_i = ei[:CHUNK, 1]

    def _rows(e):
        return jnp.transpose(e, (1, 0, 3, 2)).reshape(N_GROUPS, CHUNK_VEC, SSM_STATE)

    w_state = jnp.concatenate([_rows(ef_r), _rows(ef_i), _rows(ef_i), _rows(ef_r),
                               _rows(eb_r), _rows(eb_i), _rows(eb_i), _rows(eb_r)], axis=2)

    def _cols(g):
        return jnp.transpose(g, (1, 3, 0, 2)).reshape(N_GROUPS, SSM_STATE, CHUNK_VEC)

    gf_r = gr[1:, 0]
    gf_i = gi[1:, 0]
    gb_r = gr[1:, 1][::-1]
    gb_i = gi[1:, 1][::-1]
    w_out = jnp.concatenate([_cols(gf_r), -_cols(gf_i), _cols(gb_r), -_cols(gb_i)], axis=1)

    a_r = pr[CHUNK]
    a_i = pi[CHUNK]
    a1 = jnp.concatenate([a_r, a_r], axis=-1)
    a2 = jnp.concatenate([-a_i, a_i], axis=-1)
    a3 = -a2
    zero = jnp.zeros_like(a1[0])
    coef = jnp.stack([a1[0], a2[0], a3[0], a1[1], a2[1], a3[1], zero, zero], axis=1)
    return (m_intra.astype(jnp.bfloat16), w_state.astype(jnp.bfloat16),
            w_out.astype(jnp.bfloat16), coef)


def _rope_tables():
    half = HEAD_DIM // 2
    inv_freq = ROPE_THETA ** (-np.arange(half, dtype=np.float64) / half)
    ang = np.arange(SEQ, dtype=np.float64)[:, None] * inv_freq[None, :]
    cos, sin = np.cos(ang), np.sin(ang)
    return jnp.asarray(np.concatenate([cos, cos, cos, cos, -sin, sin, -sin, sin], axis=1), jnp.float32)


def _rope_block(xb, cos_t, sin_t, first_half):
    swapped = jnp.where(first_half, pltpu.roll(xb, 96, 1), pltpu.roll(xb, 32, 1))
    return xb * cos_t + swapped * sin_t


COL_Q = (0, Q_COLS)
COL_KV = (COL_Q[1], COL_Q[1] + 2 * KV_COLS)
COL_ZA = (COL_KV[1], COL_KV[1] + D_ATTN)
COL_U = (COL_ZA[1], COL_ZA[1] + D_SSM)
COL_ZS = (COL_U[1], COL_U[1] + D_SSM)


def _silu(z):
    h = 0.5 * z
    return h + h * jnp.tanh(h)


def _inproj_kernel(x_ref, w_ref, rope_ref, q_ref, kv_ref, sz_ref, ut_ref, uslab_ref):
    step = pl.program_id(0)
    part = step % STEPS_PER_REGROUP
    lane = lax.broadcasted_iota(jnp.int32, (SUB_ROWS, LANES), 1)
    first_half = (lane % HEAD_DIM) < (HEAD_DIM // 2)

    def proj(xb, cols):
        return jnp.dot(xb, w_ref[:, cols[0]:cols[1]], preferred_element_type=jnp.float32)

    for sub in range(TOK_BLOCK // SUB_ROWS):
        rows = slice(sub * SUB_ROWS, (sub + 1) * SUB_ROWS)
        xb = x_ref[rows, :].astype(jnp.bfloat16)
        cos_t = rope_ref[rows, :LANES]
        sin_t = rope_ref[rows, LANES:]
        q = proj(xb, COL_Q)
        kv = proj(xb, COL_KV)
        q_blocks = [_rope_block(q[:, v * LANES:(v + 1) * LANES], cos_t, sin_t, first_half)
                    for v in range(Q_COLS // LANES)]
        q_ref[rows, :] = (jnp.concatenate(q_blocks, axis=1) * Q_SCALE).astype(q_ref.dtype)
        za = proj(xb, COL_ZA)
        k_rot = _rope_block(kv[:, :LANES], cos_t, sin_t, first_half)
        kv_ref[rows, :] = jnp.concatenate([k_rot, kv[:, LANES:]], axis=1).astype(kv_ref.dtype)
        u = proj(xb, COL_U)
        sz_ref[rows, :D_ATTN] = _silu(za).astype(sz_ref.dtype)
        zs = proj(xb, COL_ZS)
        for v in range(N_LANE_BLOCKS):
            dst = pl.ds(pl.multiple_of(part * TOK_BLOCK + sub * SUB_ROWS, SUB_ROWS), SUB_ROWS)
            uslab_ref[v, dst, :] = u[:, v * LANES:(v + 1) * LANES]
        sz_ref[rows, D_ATTN:] = _silu(zs).astype(sz_ref.dtype)

    @pl.when(part == STEPS_PER_REGROUP - 1)
    def _():
        for j in range(CHUNK):
            for v in range(N_LANE_BLOCKS):
                uj = uslab_ref[v, pl.ds(j, LANES, stride=CHUNK), :]
                ujt = jnp.transpose(uj.astype(ut_ref.dtype)).reshape(GROUPS_PER_LANE_BLOCK, SSM_CH, LANES)
                g0 = v * GROUPS_PER_LANE_BLOCK
                ut_ref[g0:g0 + GROUPS_PER_LANE_BLOCK, j * SSM_CH:(j + 1) * SSM_CH, :] = ujt


def _inproj(x2, w, rope):
    n_steps = BATCH * SEQ // TOK_BLOCK
    blocks_per_seq = SEQ // TOK_BLOCK

    def tok(cols):
        return pl.BlockSpec((TOK_BLOCK, cols), lambda s: (s, 0))

    return pl.pallas_call(
        _inproj_kernel,
        grid=(n_steps,),
        in_specs=[tok(D_MODEL), pl.BlockSpec(w.shape, lambda s: (0, 0)),
                  pl.BlockSpec((TOK_BLOCK, 2 * LANES), lambda s: (s % blocks_per_seq, 0))],
        out_specs=[tok(Q_COLS), tok(2 * KV_COLS), tok(D_ATTN + D_SSM),
                   pl.BlockSpec((None, N_GROUPS, CHUNK_VEC, LANES), lambda s: (s // STEPS_PER_REGROUP, 0, 0, 0))],
        out_shape=[jax.ShapeDtypeStruct((BATCH * SEQ, Q_COLS), jnp.bfloat16),
                   jax.ShapeDtypeStruct((BATCH * SEQ, 2 * KV_COLS), jnp.bfloat16),
                   jax.ShapeDtypeStruct((BATCH * SEQ, D_ATTN + D_SSM), jnp.bfloat16),
                   jax.ShapeDtypeStruct((N_ROW_BLOCKS, N_GROUPS, CHUNK_VEC, LANES), jnp.bfloat16)],
        scratch_shapes=[pltpu.VMEM((N_LANE_BLOCKS, REGROUP_TOK, LANES), jnp.float32)],
        compiler_params=pltpu.CompilerParams(dimension_semantics=("arbitrary",),
                                             vmem_limit_bytes=VMEM_LIMIT),
        name="inproj",
    )(x2, w, rope)


ATTN_SUB = 4
ATTN_STEP = ATTN_SUB * BLOCK
N_KEYS = 3 * BLOCK


def _attn_kernel(q_ref, kp_ref, kc_ref, kn_ref, sink_ref, y_ref):
    i = pl.program_id(1)
    kvw = jnp.concatenate([kp_ref[...], kc_ref[...], kn_ref[...]], axis=0).astype(jnp.float32)

    n_win = kvw.shape[0]
    lane_w = lax.broadcasted_iota(jnp.int32, (n_win, LANES), 1)
    k_lo, k_hi, vt_lo, vt_hi = [], [], [], []
    for hk in range(N_KV_HEADS):
        in_head = (lane_w >= hk * HEAD_DIM) & (lane_w < (hk + 1) * HEAD_DIM)
        k_nat = jnp.where(in_head, kvw[:, :LANES], 0.0)
        v_nat = jnp.where(in_head, kvw[:, LANES:], 0.0)
        k_oth = pltpu.roll(k_nat, HEAD_DIM, 1)
        v_oth = pltpu.roll(v_nat, HEAD_DIM, 1)
        pairs = ((k_nat, k_oth), (v_nat, v_oth)) if hk == 0 else ((k_oth, k_nat), (v_oth, v_nat))
        k_lo.append(pairs[0][0].astype(jnp.bfloat16))
        k_hi.append(pairs[0][1].astype(jnp.bfloat16))
        vt_lo.append(jnp.transpose(pairs[1][0]).astype(jnp.bfloat16))
        vt_hi.append(jnp.transpose(pairs[1][1]).astype(jnp.bfloat16))

    c = lax.broadcasted_iota(jnp.int32, (N_KEYS, BLOCK), 0)
    r = lax.broadcasted_iota(jnp.int32, (N_KEYS, BLOCK), 1)
    band = (c >= r) & (c - r <= 2 * WINDOW)
    ones_row = lax.broadcasted_iota(jnp.int32, (16, 2 * N_KEYS), 0)
    ones_col = lax.broadcasted_iota(jnp.int32, (16, 2 * N_KEYS), 1)
    ones = jnp.where(((ones_row == 0) & (ones_col < N_KEYS)) | ((ones_row == 1) & (ones_col >= N_KEYS)),
                     1.0, 0.0).astype(jnp.bfloat16)
    row_o = lax.broadcasted_iota(jnp.int32, (LANES, 2 * BLOCK), 0)
    n_blocks = SEQ // BLOCK

    def scores(t, hk):
        blk = i * ATTN_SUB + t
        lo = jnp.where(blk == 0, BLOCK, 0)
        hi = jnp.where(blk == n_blocks - 1, 2 * BLOCK, N_KEYS)
        valid1 = band & (c >= lo) & (c < hi)
        valid = jnp.concatenate([valid1, valid1], axis=1)
        win = slice(t * BLOCK, t * BLOCK + N_KEYS)
        q = q_ref[t * BLOCK:(t + 1) * BLOCK, (2 * hk) * LANES:(2 * hk + 2) * LANES]
        qq = jnp.concatenate([q[:, :LANES], q[:, LANES:]], axis=0)
        kcat = jnp.concatenate([k_lo[hk][win], k_hi[hk][win]], axis=0)
        st = lax.dot_general(kcat, qq, (((1,), (1,)), ((), ())),
                             preferred_element_type=jnp.float32)
        sps, ms, sinks = [], [], []
        for par in range(2):
            base = par * N_KEYS
            sp = jnp.concatenate(
                [jnp.where(valid[:BLOCK], st[base:base + BLOCK, :], NEG_INF),
                 st[base + BLOCK:base + 2 * BLOCK, :],
                 jnp.where(valid[2 * BLOCK:], st[base + 2 * BLOCK:base + N_KEYS, :], NEG_INF)], axis=0)
            h0 = 4 * hk + par
            h1 = h0 + 2
            sink = jnp.concatenate([sink_ref[h0:h0 + 1, :], sink_ref[h1:h1 + 1, :]], axis=1)
            sps.append(sp)
            ms.append(jnp.maximum(jnp.max(sp, axis=0, keepdims=True), sink))
            sinks.append(sink)
        return sps, ms, sinks

    def weights(sps, ms, sinks):
        return [jnp.exp2(sp - m).astype(jnp.bfloat16) for sp, m in zip(sps, ms)], ms, sinks

    def finish(t, hk, ps, ms, sinks):
        win = slice(t * BLOCK, t * BLOCK + N_KEYS)
        vt = jnp.concatenate([jnp.concatenate([vt_lo[hk][:, win], vt_hi[hk][:, win]], axis=1), ones], axis=0)
        ot = jnp.dot(vt, jnp.concatenate(ps, axis=0), preferred_element_type=jnp.float32)
        inv0 = 1.0 / (ot[LANES:LANES + 1, :] + jnp.exp2(sinks[0] - ms[0]))
        inv1 = 1.0 / (ot[LANES + 1:LANES + 2, :] + jnp.exp2(sinks[1] - ms[1]))
        o = jnp.transpose(ot[:LANES, :] * jnp.where(row_o < HEAD_DIM, inv0, inv1))
        rows = slice(t * BLOCK, (t + 1) * BLOCK)
        y_ref[rows, (2 * hk) * LANES:(2 * hk + 1) * LANES] = o[:BLOCK].astype(y_ref.dtype)
        y_ref[rows, (2 * hk + 1) * LANES:(2 * hk + 2) * LANES] = o[BLOCK:].astype(y_ref.dtype)

    work = [(t, hk) for t in range(ATTN_SUB) for hk in range(N_KV_HEADS)]
    stage_a = {0: scores(*work[0])}
    stage_b = {}
    for n in range(len(work) + 1):
        if n + 1 < len(work):
            stage_a[n + 1] = scores(*work[n + 1])
        if n < len(work):
            stage_b[n] = weights(*stage_a.pop(n))
        if n >= 1:
            finish(*work[n - 1], *stage_b.pop(n - 1))


def _attention(q, kv, sink_tab):
    nb = SEQ // BLOCK
    return pl.pallas_call(
        _attn_kernel,
        grid=(BATCH, SEQ // ATTN_STEP),
        in_specs=[pl.BlockSpec((None, ATTN_STEP, Q_COLS), lambda b, i: (b, i, 0)),
                  pl.BlockSpec((None, BLOCK, 2 * KV_COLS),
                               lambda b, i: (b, jnp.maximum(i * ATTN_SUB - 1, 0), 0)),
                  pl.BlockSpec((None, ATTN_STEP, 2 * KV_COLS), lambda b, i: (b, i, 0)),
                  pl.BlockSpec((None, BLOCK, 2 * KV_COLS),
                               lambda b, i: (b, jnp.minimum((i + 1) * ATTN_SUB, nb - 1), 0)),
                  pl.BlockSpec((N_Q_HEADS, LANES), lambda b, i: (0, 0))],
        out_specs=pl.BlockSpec((None, ATTN_STEP, D_ATTN), lambda b, i: (b, i, 0)),
        out_shape=jax.ShapeDtypeStruct((BATCH, SEQ, D_ATTN), jnp.bfloat16),
        compiler_params=pltpu.CompilerParams(dimension_semantics=("arbitrary", "arbitrary"),
                                             vmem_limit_bytes=VMEM_LIMIT),
        name="attention",
    )(q, kv, kv, kv, sink_tab)


SSM_GB = 4


def _ssm_kernel(ut_ref, m_ref, ws_ref, wo_ref, coef_ref, gt_ref,
                sf_ref, sfs_ref, sb_ref, sbs_ref, xf_ref, xb_ref, u_ref):
    for gi in range(SSM_GB):
        ut = jnp.concatenate([ut_ref[lb, gi] for lb in range(N_ROW_BLOCKS)], axis=1)
        u = jnp.transpose(ut.astype(jnp.float32)).astype(jnp.bfloat16)
        u_ref[gi] = u
        s = jnp.dot(u, ws_ref[gi], preferred_element_type=jnp.float32)
        for b in range(BATCH):
            rows = slice(b * N_CHUNKS, (b + 1) * N_CHUNKS)
            dst = pl.ds(b * SLAB_PITCH + 8, N_CHUNKS)
            sf_ref[gi, dst, :] = s[rows, 0 * LANES:1 * LANES]
            sfs_ref[gi, dst, :] = s[rows, 1 * LANES:2 * LANES]
            sb_ref[gi, dst, :] = s[rows, 2 * LANES:3 * LANES]
            sbs_ref[gi, dst, :] = s[rows, 3 * LANES:4 * LANES]

    zero = jnp.zeros((BATCH, LANES), jnp.float32)
    coefs = []
    for gi in range(SSM_GB):
        coefs.append([jnp.broadcast_to(coef_ref[gi, r:r + 1, :], (BATCH, LANES)) for r in range(6)])
        xf_ref[gi, pl.ds(8, BATCH, stride=SLAB_PITCH), :] = zero
        xb_ref[gi, pl.ds(8 + N_CHUNKS - 1, BATCH, stride=SLAB_PITCH), :] = zero

    def rows_at(r):
        return pl.ds(r, BATCH, stride=SLAB_PITCH)

    def step(k, carry):
        kb = N_CHUNKS - 1 - k
        out = []
        for gi in range(SSM_GB):
            a1f, a2f, a3f, a1b, a2b, a3b = coefs[gi]
            xf, xfs, xb, xbs = carry[4 * gi:4 * gi + 4]
            nxf = a1f * xf + a2f * xfs + sf_ref[gi, rows_at(8 + k), :]
            nxfs = a1f * xfs + a3f * xf + sfs_ref[gi, rows_at(8 + k), :]
            xf_ref[gi, rows_at(9 + k), :] = nxf
            nxb = a1b * xb + a2b * xbs + sb_ref[gi, rows_at(8 + kb), :]
            nxbs = a1b * xbs + a3b * xb + sbs_ref[gi, rows_at(8 + kb), :]
            xb_ref[gi, rows_at(7 + kb), :] = nxb
            out += [nxf, nxfs, nxb, nxbs]
        return tuple(out)

    lax.fori_loop(0, N_CHUNKS, step, (zero,) * (4 * SSM_GB), unroll=2)

    def chunk_out(gi):
        xin_f = jnp.concatenate([xf_ref[gi, pl.ds(b * SLAB_PITCH + 8, N_CHUNKS), :] for b in range(BATCH)], axis=0)
        xin_b = jnp.concatenate([xb_ref[gi, pl.ds(b * SLAB_PITCH + 8, N_CHUNKS), :] for b in range(BATCH)], axis=0)
        lhs = jnp.concatenate([u_ref[gi], xin_f.astype(jnp.bfloat16), xin_b.astype(jnp.bfloat16)], axis=1)
        rhs = jnp.concatenate([m_ref[gi], wo_ref[gi]], axis=0)
        return jnp.dot(lhs, rhs, preferred_element_type=jnp.float32)

    y = chunk_out(0)
    for gi in range(SSM_GB):
        y_next = chunk_out(gi + 1) if gi + 1 < SSM_GB else None
        gt = jnp.transpose(jax.nn.gelu(y)).astype(gt_ref.dtype)
        for lb in range(N_ROW_BLOCKS):
            gt_ref[lb, gi] = gt[:, lb * LANES:(lb + 1) * LANES]
        y = y_next


def _ssm(ut, m_intra, w_state, w_out, coef):
    def per_group(shape):
        return pl.BlockSpec((SSM_GB,) + shape, lambda g: (g, 0, 0))

    lane_blocked = pl.BlockSpec((N_ROW_BLOCKS, SSM_GB, CHUNK_VEC, LANES), lambda g: (0, g, 0, 0))
    slab = pltpu.VMEM((SSM_GB, SLAB_ROWS, LANES), jnp.float32)
    return pl.pallas_call(
        _ssm_kernel,
        grid=(N_GROUPS // SSM_GB,),
        in_specs=[lane_blocked, per_group((CHUNK_VEC, CHUNK_VEC)),
                  per_group((CHUNK_VEC, 4 * LANES)), per_group((2 * STATE_VEC, CHUNK_VEC)),
                  per_group((8, LANES))],
        out_specs=lane_blocked,
        out_shape=jax.ShapeDtypeStruct((N_ROW_BLOCKS, N_GROUPS, CHUNK_VEC, LANES), jnp.bfloat16),
        scratch_shapes=[slab] * 6 + [pltpu.VMEM((SSM_GB, ROWS, CHUNK_VEC), jnp.bfloat16)],
        compiler_params=pltpu.CompilerParams(dimension_semantics=("arbitrary",),
                                             vmem_limit_bytes=VMEM_LIMIT),
        name="ssm",
    )(ut, m_intra, w_state, w_out, coef)


FINAL_SUB_ROWS = 256


def _unit_rms(v):
    return v * lax.rsqrt(jnp.mean(v * v, axis=-1, keepdims=True) + NORM_EPS)


def _final_kernel(x_ref, ya_ref, sz_ref, gt_ref, wglu_ref, bglu_ref,
                  woa_ref, wos_ref, lng_ref, lnb_ref, o_ref, gslab_ref):
    step = pl.program_id(0)
    part = step % STEPS_PER_REGROUP

    @pl.when(part == 0)
    def _():
        for j in range(CHUNK):
            for v in range(N_LANE_BLOCKS):
                g0 = v * GROUPS_PER_LANE_BLOCK
                gjt = gt_ref[g0:g0 + GROUPS_PER_LANE_BLOCK, j * SSM_CH:(j + 1) * SSM_CH, :].astype(jnp.float32)
                gj = jnp.transpose(gjt.reshape(LANES, LANES))
                gslab_ref[v, pl.ds(j, LANES, stride=CHUNK), :] = gj

    def rows_of(sub):
        return slice(sub * FINAL_SUB_ROWS, (sub + 1) * FINAL_SUB_ROWS)

    def gate_in(sub):
        rows = rows_of(sub)
        ya = ya_ref[rows, :].astype(jnp.float32) * sz_ref[rows, :D_ATTN].astype(jnp.float32)
        ya = _unit_rms(ya).astype(jnp.bfloat16)
        src = pl.ds(pl.multiple_of(part * TOK_BLOCK + sub * FINAL_SUB_ROWS, FINAL_SUB_ROWS), FINAL_SUB_ROWS)
        g = jnp.concatenate([gslab_ref[v, src, :] for v in range(N_LANE_BLOCKS)], axis=1)
        return ya, g

    def glu(sub, ya, g):
        rows = rows_of(sub)
        gate = jnp.dot(g.astype(jnp.bfloat16), wglu_ref[...], preferred_element_type=jnp.float32) + bglu_ref[...]
        ys = g * jax.nn.sigmoid(gate)
        ys = ys * sz_ref[rows, D_ATTN:].astype(jnp.float32)
        return ya, _unit_rms(ys).astype(jnp.bfloat16)

    def project(sub, ya, ys):
        return (jnp.dot(ya, woa_ref[...], preferred_element_type=jnp.float32)
                + jnp.dot(ys, wos_ref[...], preferred_element_type=jnp.float32))

    def layer_norm(sub, out):
        rows = rows_of(sub)
        h = x_ref[rows, :] + out
        mu = jnp.mean(h, axis=-1, keepdims=True)
        hc = h - mu
        var = jnp.mean(hc * hc, axis=-1, keepdims=True)
        o_ref[rows, :] = hc * lax.rsqrt(var + NORM_EPS / DEEPNORM_ALPHA ** 2) * lng_ref[...] + lnb_ref[...]

    n_sub = TOK_BLOCK // FINAL_SUB_ROWS
    stage_a, stage_b = {}, {}
    for n in range(n_sub + 2):
        if n < n_sub:
            stage_a[n] = gate_in(n)
        if 1 <= n <= n_sub:
            stage_b[n - 1] = glu(n - 1, *stage_a.pop(n - 1))
        if n >= 2:
            layer_norm(n - 2, project(n - 2, *stage_b.pop(n - 2)))


def _final(x2, ya, sz, gt, wglu, bglu, woa, wos, lng, lnb):
    n_steps = BATCH * SEQ // TOK_BLOCK

    def tok(cols):
        return pl.BlockSpec((TOK_BLOCK, cols), lambda s: (s, 0))

    def whole(shape):
        return pl.BlockSpec(shape, lambda s: (0,) * len(shape))

    return pl.pallas_call(
        _final_kernel,
        grid=(n_steps,),
        in_specs=[tok(D_MODEL), tok(D_ATTN), tok(D_ATTN + D_SSM),
                  pl.BlockSpec((None, N_GROUPS, CHUNK_VEC, LANES), lambda s: (s // STEPS_PER_REGROUP, 0, 0, 0)),
                  whole(wglu.shape), whole(bglu.shape),
                  whole(woa.shape), whole(wos.shape), whole(lng.shape), whole(lnb.shape)],
        out_specs=tok(D_MODEL),
        out_shape=jax.ShapeDtypeStruct((BATCH * SEQ, D_MODEL), jnp.float32),
        scratch_shapes=[pltpu.VMEM((N_LANE_BLOCKS, REGROUP_TOK, LANES), jnp.float32)],
        compiler_params=pltpu.CompilerParams(dimension_semantics=("arbitrary",),
                                             vmem_limit_bytes=VMEM_LIMIT),
        name="final",
    )(x2, ya, sz, gt, wglu, bglu, woa, wos, lng, lnb)


def kernel(x, w_in, attn_sink, ssm_a_re, ssm_a_im, ssm_log_dt, ssm_b_re, ssm_b_im, ssm_c_re, ssm_c_im,
           ssm_d, w_glu, b_glu, norm_attn_g, norm_ssm_g, w_out, ln_g, ln_b):
    assert x.shape == (BATCH, SEQ, D_MODEL) and w_in.shape[0] == 1
    bf16 = jnp.bfloat16
    f32 = jnp.float32
    m_intra, w_state, w_so, coef = _ssm_tables(ssm_a_re[0], ssm_a_im[0], ssm_log_dt[0], ssm_b_re[0],
                                                ssm_b_im[0], ssm_c_re[0], ssm_c_im[0], ssm_d[0])
    rope = _rope_tables()
    sink_tab = jnp.broadcast_to((attn_sink[0].astype(f32) * LOG2_E)[:, None], (N_Q_HEADS, LANES))

    x2 = x.reshape(BATCH * SEQ, D_MODEL)
    q, kv, sz, ut = _inproj(x2, w_in[0].astype(bf16), rope)
    ya = _attention(q.reshape(BATCH, SEQ, Q_COLS), kv.reshape(BATCH, SEQ, 2 * KV_COLS), sink_tab)
    gt = _ssm(ut, m_intra, w_state, w_so, coef)

    gains = jnp.concatenate([norm_attn_g[0], norm_ssm_g[0]]).astype(f32)
    wo = (w_out[0].astype(f32) * (gains / DEEPNORM_ALPHA)[:, None]).astype(bf16)
    row = lambda v: v[0].astype(f32)[None, :]
    out = _final(x2, ya.reshape(BATCH * SEQ, D_ATTN), sz, gt, w_glu[0].astype(bf16), row(b_glu),
                 wo[:D_ATTN], wo[D_ATTN:], row(ln_g), row(ln_b))
    return out.reshape(BATCH, SEQ, D_MODEL)
```

```python
import jax
import jax.numpy as jnp
import numpy as np
from jax import lax
from jax.experimental import pallas as pl
from jax.experimental.pallas import tpu as pltpu

D_MODEL = 1024
BATCH = 4
SEQ = 4096
D_ATTN = 512
D_SSM = 512
HEAD_DIM = 64
N_Q_HEADS = 8
N_KV_HEADS = 2
WINDOW = 128
BLOCK = 128
ROPE_THETA = 10000.0
SSM_CH = 16
N_GROUPS = 32
SSM_STATE = 64
NORM_EPS = 1e-5
NEG_INF = -1e30
DEEPNORM_ALPHA = 2.0 ** 0.25
LOG2_E = 1.4426950408889634
Q_SCALE = HEAD_DIM ** -0.5 * LOG2_E

Q_COLS = N_Q_HEADS * HEAD_DIM
KV_COLS = N_KV_HEADS * HEAD_DIM
CHUNK = 16
N_CHUNKS = SEQ // CHUNK
CHUNK_VEC = CHUNK * SSM_CH
STATE_VEC = 2 * SSM_STATE
LANES = 128
ROWS = BATCH * N_CHUNKS
N_ROW_BLOCKS = ROWS // LANES
SLAB_PITCH = N_CHUNKS + 8
SLAB_ROWS = BATCH * SLAB_PITCH + 8
TOK_BLOCK = 512
REGROUP_TOK = LANES * CHUNK
STEPS_PER_REGROUP = REGROUP_TOK // TOK_BLOCK
SUB_ROWS = 512
N_LANE_BLOCKS = D_SSM // LANES
GROUPS_PER_LANE_BLOCK = LANES // SSM_CH
VMEM_LIMIT = 56 * 1024 * 1024
HIGHEST = lax.Precision.HIGHEST


LAG_ROWS = 512
PREP_GB = 4


def _prep_kernel(arow_ref, bt_ref, c_ref, dt_ref, m_ref, ws_ref, wo_ref, coef_ref):
    for gi in range(PREP_GB):
        _prep_group(gi, arow_ref, bt_ref, c_ref, dt_ref, m_ref, ws_ref, wo_ref, coef_ref)


def _prep_group(gi, arow_ref, bt_ref, c_ref, dt_ref, m_ref, ws_ref, wo_ref, coef_ref):
    f32 = jnp.float32
    lo16 = lax.broadcasted_iota(jnp.int32, (SSM_CH, LANES), 1) < SSM_STATE
    lo1 = lax.broadcasted_iota(jnp.int32, (1, LANES), 1) < SSM_STATE
    arow = arow_ref[gi]

    lag_tabs, kws, ws_parts, coef_rows = [], [], [], []
    for d in range(2):
        ar = arow[3 * d:3 * d + 1, :]
        ai = arow[3 * d + 1:3 * d + 2, :]
        dt = jnp.exp(arow[3 * d + 2:3 * d + 3, :])
        zr = dt * ar
        zi = dt * ai
        mag = jnp.exp(zr)
        lr = mag * jnp.cos(zi)
        li = mag * jnp.sin(zi)
        den = ar * ar + ai * ai
        fr = ((lr - 1.0) * ar + li * ai) / den
        fi = (li * ar - (lr - 1.0) * ai) / den
        btr = bt_ref[0, d, gi]
        bti = bt_ref[1, d, gi]
        bbr = fr * btr - fi * bti
        bbi = fr * bti + fi * btr
        cr = c_ref[0, d, gi]
        ci = c_ref[1, d, gi]

        pr = [jnp.ones((1, LANES), f32)]
        pi = [jnp.zeros((1, LANES), f32)]
        for _ in range(CHUNK):
            pr.append(pr[-1] * lr - pi[-1] * li)
            pi.append(pr[-2] * li + pi[-1] * lr)
        pa = [jnp.where(lo1, r, i) for r, i in zip(pr, pi)]
        pb = [jnp.where(lo1, -i, r) for r, i in zip(pr, pi)]
        pc = [jnp.where(lo1, i, r) for r, i in zip(pr, pi)]
        pd = [jnp.where(lo1, r, -i) for r, i in zip(pr, pi)]

        def row(t, tab):
            return tab[t]

        order = range(CHUNK + 1) if d == 0 else range(CHUNK, -1, -1)
        blocks = [cr * row(t, pa) + ci * row(t, pb) for t in order]
        blocks.append(jnp.zeros((LAG_ROWS - (CHUNK + 1) * SSM_CH, LANES), f32))
        lag_tab = jnp.concatenate(blocks, axis=0)
        lag_tabs.append(lag_tab)
        lhs = jnp.where(lo16, bbr, -bbi)
        kws.append(lax.dot_general(lhs, lag_tab, (((1,), (1,)), ((), ())),
                                   precision=HIGHEST, preferred_element_type=f32))

        parts = []
        for j in range(CHUNK):
            t = CHUNK - 1 - j if d == 0 else j
            parts.append(jnp.concatenate([bbr * row(t, pa) + bbi * row(t, pb),
                                          bbr * row(t, pc) + bbi * row(t, pd)], axis=1))
        ws_parts.append(jnp.concatenate(parts, axis=0))

        a1 = pr[CHUNK]
        a2 = jnp.where(lo1, -pi[CHUNK], pi[CHUNK])
        coef_rows += [a1, a2, -a2]

    ws_ref[gi] = jnp.concatenate(ws_parts, axis=1).astype(ws_ref.dtype)
    coef_ref[gi] = jnp.concatenate(coef_rows + [jnp.zeros((2, LANES), f32)], axis=0)

    sign = jnp.where(lax.broadcasted_iota(jnp.int32, (LANES, CHUNK_VEC), 0) < SSM_STATE, 1.0, -1.0)
    wo_f = jnp.transpose(lag_tabs[0][SSM_CH:SSM_CH + CHUNK_VEC, :]) * sign
    wo_b = jnp.transpose(lag_tabs[1][:CHUNK_VEC, :]) * sign
    wo_ref[gi] = jnp.concatenate([wo_f, wo_b], axis=0).astype(wo_ref.dtype)

    sub = lax.broadcasted_iota(jnp.int32, (SSM_CH, CHUNK_VEC), 0)
    lane = lax.broadcasted_iota(jnp.int32, (SSM_CH, CHUNK_VEC), 1)
    dtile = jnp.broadcast_to(dt_ref[gi], (SSM_CH, CHUNK_VEC))
    for jp in range(CHUNK):
        fwd = kws[0] if jp == 0 else pltpu.roll(kws[0], SSM_CH * jp, 1)
        bwd = pltpu.roll(kws[1], (LAG_ROWS - SSM_CH * (CHUNK - jp)) % LAG_ROWS, 1)
        skip = jnp.where((lane // SSM_CH == jp) & (lane % SSM_CH == sub), dtile, 0.0)
        blk = fwd[:, :CHUNK_VEC] + bwd[:, :CHUNK_VEC] + skip
        m_ref[gi, jp * SSM_CH:(jp + 1) * SSM_CH, :] = blk.astype(m_ref.dtype)


def _ssm_tables(a_re, a_im, log_dt, b_re, b_im, c_re, c_im, d):
    f32 = jnp.float32
    dup = lambda v: jnp.concatenate([v, v], axis=-1)
    ldt = jnp.broadcast_to(log_dt.astype(f32)[..., None], a_re.shape)
    zero = jnp.zeros_like(ldt[0])
    arow = dup(jnp.stack([a_re[0], a_im[0], ldt[0], a_re[1], a_im[1], ldt[1], zero, zero], axis=1).astype(f32))
    bt = dup(jnp.swapaxes(jnp.stack([b_re, b_im]).astype(f32), -1, -2))
    cc = dup(jnp.stack([c_re, c_im]).astype(f32))
    dtile = jnp.tile(d.astype(f32).reshape(N_GROUPS, 1, SSM_CH), (1, 1, CHUNK))

    bf16 = jnp.bfloat16
    return pl.pallas_call(
        _prep_kernel,
        grid=(N_GROUPS // PREP_GB,),
        in_specs=[pl.BlockSpec((PREP_GB, 8, LANES), lambda g: (g, 0, 0)),
                  pl.BlockSpec((2, 2, PREP_GB, SSM_CH, LANES), lambda g: (0, 0, g, 0, 0)),
                  pl.BlockSpec((2, 2, PREP_GB, SSM_CH, LANES), lambda g: (0, 0, g, 0, 0)),
                  pl.BlockSpec((PREP_GB, 1, CHUNK_VEC), lambda g: (g, 0, 0))],
        out_specs=[pl.BlockSpec((PREP_GB, CHUNK_VEC, CHUNK_VEC), lambda g: (g, 0, 0)),
                   pl.BlockSpec((PREP_GB, CHUNK_VEC, 4 * LANES), lambda g: (g, 0, 0)),
                   pl.BlockSpec((PREP_GB, 2 * STATE_VEC, CHUNK_VEC), lambda g: (g, 0, 0)),
                   pl.BlockSpec((PREP_GB, 8, LANES), lambda g: (g, 0, 0))],
        out_shape=[jax.ShapeDtypeStruct((N_GROUPS, CHUNK_VEC, CHUNK_VEC), bf16),
                   jax.ShapeDtypeStruct((N_GROUPS, CHUNK_VEC, 4 * LANES), bf16),
                   jax.ShapeDtypeStruct((N_GROUPS, 2 * STATE_VEC, CHUNK_VEC), bf16),
                   jax.ShapeDtypeStruct((N_GROUPS, 8, LANES), f32)],
        compiler_params=pltpu.CompilerParams(dimension_semantics=("arbitrary",),
                                             vmem_limit_bytes=VMEM_LIMIT),
        name="ssm_tables",
    )(arow, bt, cc, dtile)


def _ssm_tables_xla(a_re, a_im, log_dt, b_re, b_im, c_re, c_im, d):
    f32 = jnp.float32
    dt = jnp.exp(log_dt.astype(f32))[..., None]
    ar = a_re.astype(f32)
    ai = a_im.astype(f32)
    zr = dt * ar
    zi = dt * ai
    mag = jnp.exp(zr)
    lr = mag * jnp.cos(zi)
    li = mag * jnp.sin(zi)
    den = ar * ar + ai * ai
    fr = ((lr - 1.0) * ar + li * ai) / den
    fi = (li * ar - (lr - 1.0) * ai) / den
    br = b_re.astype(f32)
    bi = b_im.astype(f32)
    bbr = fr[..., None] * br - fi[..., None] * bi
    bbi = fr[..., None] * bi + fi[..., None] * br
    cr = c_re.astype(f32)
    ci = c_im.astype(f32)

    tau = jnp.arange(CHUNK + 1, dtype=f32)[:, None, None, None]
    pmag = jnp.exp(tau * zr[None])
    pr = pmag * jnp.cos(tau * zi[None])
    pi = pmag * jnp.sin(tau * zi[None])

    er = pr[..., None] * bbr[None] - pi[..., None] * bbi[None]
    ei = pr[..., None] * bbi[None] + pi[..., None] * bbr[None]
    gr = cr[None] * pr[:, :, :, None, :] - ci[None] * pi[:, :, :, None, :]
    gi = cr[None] * pi[:, :, :, None, :] + ci[None] * pr[:, :, :, None, :]

    kk = (jnp.einsum('dgcp,tdgpe->tdgce', cr, er, precision=HIGHEST)
          - jnp.einsum('dgcp,tdgpe->tdgce', ci, ei, precision=HIGHEST))
    kf = kk[:CHUNK, 0]
    kb = kk[:CHUNK, 1]
    dmat = d.astype(f32).reshape(N_GROUPS, SSM_CH)[:, :, None] * jnp.eye(SSM_CH, dtype=f32)[None]
    center = kf[0] + kb[0] + dmat
    kall = jnp.concatenate([kb[:0:-1], center[None], kf[1:]], axis=0)
    jj = jnp.arange(CHUNK)
    lag_idx = jj[None, :] - jj[:, None] + (CHUNK - 1)
    m5 = kall[lag_idx]
    m_intra = jnp.transpose(m5, (2, 0, 4, 1, 3)).reshape(N_GROUPS, CHUNK_VEC, CHUNK_VEC)

    ef_r = er[:CHUNK, 0][::-1]
    ef_i = ei[:CHUNK, 0][::-1]
    eb_r = er[:CHUNK, 1]
    eb_i = ei[:CHUNK, 1]

    def _rows(e):
        return jnp.transpose(e, (1, 0, 3, 2)).reshape(N_GROUPS, CHUNK_VEC, SSM_STATE)

    w_state = jnp.concatenate([_rows(ef_r), _rows(ef_i), _rows(ef_i), _rows(ef_r),
                               _rows(eb_r), _rows(eb_i), _rows(eb_i), _rows(eb_r)], axis=2)

    def _cols(g):
        return jnp.transpose(g, (1, 3, 0, 2)).reshape(N_GROUPS, SSM_STATE, CHUNK_VEC)

    gf_r = gr[1:, 0]
    gf_i = gi[1:, 0]
    gb_r = gr[1:, 1][::-1]
    gb_i = gi[1:, 1][::-1]
    w_out = jnp.concatenate([_cols(gf_r), -_cols(gf_i), _cols(gb_r), -_cols(gb_i)], axis=1)

    a_r = pr[CHUNK]
    a_i = pi[CHUNK]
    a1 = jnp.concatenate([a_r, a_r], axis=-1)
    a2 = jnp.concatenate([-a_i, a_i], axis=-1)
    a3 = -a2
    zero = jnp.zeros_like(a1[0])
    coef = jnp.stack([a1[0], a2[0], a3[0], a1[1], a2[1], a3[1], zero, zero], axis=1)
    return (m_intra.astype(jnp.bfloat16), w_state.astype(jnp.bfloat16),
            w_out.astype(jnp.bfloat16), coef)


def _rope_tables():
    half = HEAD_DIM // 2
    inv_freq = ROPE_THETA ** (-np.arange(half, dtype=np.float64) / half)
    ang = np.arange(SEQ, dtype=np.float64)[:, None] * inv_freq[None, :]
    cos, sin = np.cos(ang), np.sin(ang)
    return jnp.asarray(np.concatenate([cos, cos, cos, cos, -sin, sin, -sin, sin], axis=1), jnp.float32)


def _rope_block(xb, cos_t, sin_t, first_half):
    swapped = jnp.where(first_half, pltpu.roll(xb, 96, 1), pltpu.roll(xb, 32, 1))
    return xb * cos_t + swapped * sin_t


COL_Q = (0, Q_COLS)
COL_KV = (COL_Q[1], COL_Q[1] + 2 * KV_COLS)
COL_ZA = (COL_KV[1], COL_KV[1] + D_ATTN)
COL_U = (COL_ZA[1], COL_ZA[1] + D_SSM)
COL_ZS = (COL_U[1], COL_U[1] + D_SSM)


def _silu(z):
    h = 0.5 * z
    return h + h * jnp.tanh(h)


def _inproj_kernel(x_ref, w_ref, rope_ref, q_ref, kv_ref, sz_ref, ut_ref, uslab_ref):
    step = pl.program_id(0)
    part = step % STEPS_PER_REGROUP
    lane = lax.broadcasted_iota(jnp.int32, (SUB_ROWS, LANES), 1)
    first_half = (lane % HEAD_DIM) < (HEAD_DIM // 2)

    def proj(xb, cols):
        return jnp.dot(xb, w_ref[:, cols[0]:cols[1]], preferred_element_type=jnp.float32)

    for sub in range(TOK_BLOCK // SUB_ROWS):
        rows = slice(sub * SUB_ROWS, (sub + 1) * SUB_ROWS)
        xb = x_ref[rows, :].astype(jnp.bfloat16)
        cos_t = rope_ref[rows, :LANES]
        sin_t = rope_ref[rows, LANES:]
        q = proj(xb, COL_Q)
        kv = proj(xb, COL_KV)
        q_blocks = [_rope_block(q[:, v * LANES:(v + 1) * LANES], cos_t, sin_t, first_half)
                    for v in range(Q_COLS // LANES)]
        q_ref[rows, :] = (jnp.concatenate(q_blocks, axis=1) * Q_SCALE).astype(q_ref.dtype)
        za = proj(xb, COL_ZA)
        k_rot = _rope_block(kv[:, :LANES], cos_t, sin_t, first_half)
        kv_ref[rows, :] = jnp.concatenate([k_rot, kv[:, LANES:]], axis=1).astype(kv_ref.dtype)
        u = proj(xb, COL_U)
        sz_ref[rows, :D_ATTN] = _silu(za).astype(sz_ref.dtype)
        zs = proj(xb, COL_ZS)
        for v in range(N_LANE_BLOCKS):
            dst = pl.ds(pl.multiple_of(part * TOK_BLOCK + sub * SUB_ROWS, SUB_ROWS), SUB_ROWS)
            uslab_ref[v, dst, :] = u[:, v * LANES:(v + 1) * LANES]
        sz_ref[rows, D_ATTN:] = _silu(zs).astype(sz_ref.dtype)

    @pl.when(part == STEPS_PER_REGROUP - 1)
    def _():
        for j in range(CHUNK):
            for v in range(N_LANE_BLOCKS):
                uj = uslab_ref[v, pl.ds(j, LANES, stride=CHUNK), :]
                ujt = jnp.transpose(uj.astype(ut_ref.dtype)).reshape(GROUPS_PER_LANE_BLOCK, SSM_CH, LANES)
                g0 = v * GROUPS_PER_LANE_BLOCK
                ut_ref[g0:g0 + GROUPS_PER_LANE_BLOCK, j * SSM_CH:(j + 1) * SSM_CH, :] = ujt


def _inproj(x2, w, rope):
    n_steps = BATCH * SEQ // TOK_BLOCK
    blocks_per_seq = SEQ // TOK_BLOCK

    def tok(cols):
        return pl.BlockSpec((TOK_BLOCK, cols), lambda s: (s, 0))

    return pl.pallas_call(
        _inproj_kernel,
        grid=(n_steps,),
        in_specs=[tok(D_MODEL), pl.BlockSpec(w.shape, lambda s: (0, 0)),
                  pl.BlockSpec((TOK_BLOCK, 2 * LANES), lambda s: (s % blocks_per_seq, 0))],
        out_specs=[tok(Q_COLS), tok(2 * KV_COLS), tok(D_ATTN + D_SSM),
                   pl.BlockSpec((None, N_GROUPS, CHUNK_VEC, LANES), lambda s: (s // STEPS_PER_REGROUP, 0, 0, 0))],
        out_shape=[jax.ShapeDtypeStruct((BATCH * SEQ, Q_COLS), jnp.bfloat16),
                   jax.ShapeDtypeStruct((BATCH * SEQ, 2 * KV_COLS), jnp.bfloat16),
                   jax.ShapeDtypeStruct((BATCH * SEQ, D_ATTN + D_SSM), jnp.bfloat16),
                   jax.ShapeDtypeStruct((N_ROW_BLOCKS, N_GROUPS, CHUNK_VEC, LANES), jnp.bfloat16)],
        scratch_shapes=[pltpu.VMEM((N_LANE_BLOCKS, REGROUP_TOK, LANES), jnp.float32)],
        compiler_params=pltpu.CompilerParams(dimension_semantics=("arbitrary",),
                                             vmem_limit_bytes=VMEM_LIMIT),
        name="inproj",
    )(x2, w, rope)


ATTN_SUB = 4
ATTN_STEP = ATTN_SUB * BLOCK
N_KEYS = 3 * BLOCK


def _attn_kernel(q_ref, kp_ref, kc_ref, kn_ref, sink_ref, y_ref):
    i = pl.program_id(1)
    kvw = jnp.concatenate([kp_ref[...], kc_ref[...], kn_ref[...]], axis=0).astype(jnp.float32)

    n_win = kvw.shape[0]
    lane_w = lax.broadcasted_iota(jnp.int32, (n_win, LANES), 1)
    k_lo, k_hi, vt_lo, vt_hi = [], [], [], []
    for hk in range(N_KV_HEADS):
        in_head = (lane_w >= hk * HEAD_DIM) & (lane_w < (hk + 1) * HEAD_DIM)
        k_nat = jnp.where(in_head, kvw[:, :LANES], 0.0)
        v_nat = jnp.where(in_head, kvw[:, LANES:], 0.0)
        k_oth = pltpu.roll(k_nat, HEAD_DIM, 1)
        v_oth = pltpu.roll(v_nat, HEAD_DIM, 1)
        pairs = ((k_nat, k_oth), (v_nat, v_oth)) if hk == 0 else ((k_oth, k_nat), (v_oth, v_nat))
        k_lo.append(pairs[0][0].astype(jnp.bfloat16))
        k_hi.append(pairs[0][1].astype(jnp.bfloat16))
        vt_lo.append(jnp.transpose(pairs[1][0]).astype(jnp.bfloat16))
        vt_hi.append(jnp.transpose(pairs[1][1]).astype(jnp.bfloat16))

    c = lax.broadcasted_iota(jnp.int32, (N_KEYS, BLOCK), 0)
    r = lax.broadcasted_iota(jnp.int32, (N_KEYS, BLOCK), 1)
    band = (c >= r) & (c - r <= 2 * WINDOW)
    ones_row = lax.broadcasted_iota(jnp.int32, (16, 2 * N_KEYS), 0)
    ones_col = lax.broadcasted_iota(jnp.int32, (16, 2 * N_KEYS), 1)
    ones = jnp.where(((ones_row == 0) & (ones_col < N_KEYS)) | ((ones_row == 1) & (ones_col >= N_KEYS)),
                     1.0, 0.0).astype(jnp.bfloat16)
    row_o = lax.broadcasted_iota(jnp.int32, (LANES, 2 * BLOCK), 0)
    n_blocks = SEQ // BLOCK

    def scores(t, hk):
        blk = i * ATTN_SUB + t
        lo = jnp.where(blk == 0, BLOCK, 0)
        hi = jnp.where(blk == n_blocks - 1, 2 * BLOCK, N_KEYS)
        valid1 = band & (c >= lo) & (c < hi)
        valid = jnp.concatenate([valid1, valid1], axis=1)
        win = slice(t * BLOCK, t * BLOCK + N_KEYS)
        q = q_ref[t * BLOCK:(t + 1) * BLOCK, (2 * hk) * LANES:(2 * hk + 2) * LANES]
        qq = jnp.concatenate([q[:, :LANES], q[:, LANES:]], axis=0)
        kcat = jnp.concatenate([k_lo[hk][win], k_hi[hk][win]], axis=0)
        st = lax.dot_general(kcat, qq, (((1,), (1,)), ((), ())),
                             preferred_element_type=jnp.float32)
        sps, ms, sinks = [], [], []
        for par in range(2):
            base = par * N_KEYS
            sp = jnp.concatenate(
                [jnp.where(valid[:BLOCK], st[base:base + BLOCK, :], NEG_INF),
                 st[base + BLOCK:base + 2 * BLOCK, :],
                 jnp.where(valid[2 * BLOCK:], st[base + 2 * BLOCK:base + N_KEYS, :], NEG_INF)], axis=0)
            h0 = 4 * hk + par
            h1 = h0 + 2
            sink = jnp.concatenate([sink_ref[h0:h0 + 1, :], sink_ref[h1:h1 + 1, :]], axis=1)
            sps.append(sp)
            ms.append(jnp.maximum(jnp.max(sp, axis=0, keepdims=True), sink))
            sinks.append(sink)
        return sps, ms, sinks

    def weights(sps, ms, sinks):
        return [jnp.exp2(sp - m).astype(jnp.bfloat16) for sp, m in zip(sps, ms)], ms, sinks

    def finish(t, hk, ps, ms, sinks):
        win = slice(t * BLOCK, t * BLOCK + N_KEYS)
        vt = jnp.concatenate([jnp.concatenate([vt_lo[hk][:, win], vt_hi[hk][:, win]], axis=1), ones], axis=0)
        ot = jnp.dot(vt, jnp.concatenate(ps, axis=0), preferred_element_type=jnp.float32)
        inv0 = 1.0 / (ot[LANES:LANES + 1, :] + jnp.exp2(sinks[0] - ms[0]))
        inv1 = 1.0 / (ot[LANES + 1:LANES + 2, :] + jnp.exp2(sinks[1] - ms[1]))
        o = jnp.transpose(ot[:LANES, :] * jnp.where(row_o < HEAD_DIM, inv0, inv1))
        rows = slice(t * BLOCK, (t + 1) * BLOCK)
        y_ref[rows, (2 * hk) * LANES:(2 * hk + 1) * LANES] = o[:BLOCK].astype(y_ref.dtype)
        y_ref[rows, (2 * hk + 1) * LANES:(2 * hk + 2) * LANES] = o[BLOCK:].astype(y_ref.dtype)

    work = [(t, hk) for t in range(ATTN_SUB) for hk in range(N_KV_HEADS)]
    stage_a = {0: scores(*work[0])}
    stage_b = {}
    for n in range(len(work) + 1):
        if n + 1 < len(work):
            stage_a[n + 1] = scores(*work[n + 1])
        if n < len(work):
            stage_b[n] = weights(*stage_a.pop(n))
        if n >= 1:
            finish(*work[n - 1], *stage_b.pop(n - 1))


def _attention(q, kv, sink_tab):
    nb = SEQ // BLOCK
    return pl.pallas_call(
        _attn_kernel,
        grid=(BATCH, SEQ // ATTN_STEP),
        in_specs=[pl.BlockSpec((None, ATTN_STEP, Q_COLS), lambda b, i: (b, i, 0)),
                  pl.BlockSpec((None, BLOCK, 2 * KV_COLS),
                               lambda b, i: (b, jnp.maximum(i * ATTN_SUB - 1, 0), 0)),
                  pl.BlockSpec((None, ATTN_STEP, 2 * KV_COLS), lambda b, i: (b, i, 0)),
                  pl.BlockSpec((None, BLOCK, 2 * KV_COLS),
                               lambda b, i: (b, jnp.minimum((i + 1) * ATTN_SUB, nb - 1), 0)),
                  pl.BlockSpec((N_Q_HEADS, LANES), lambda b, i: (0, 0))],
        out_specs=pl.BlockSpec((None, ATTN_STEP, D_ATTN), lambda b, i: (b, i, 0)),
        out_shape=jax.ShapeDtypeStruct((BATCH, SEQ, D_ATTN), jnp.bfloat16),
        compiler_params=pltpu.CompilerParams(dimension_semantics=("arbitrary", "arbitrary"),
                                             vmem_limit_bytes=VMEM_LIMIT),
        name="attention",
    )(q, kv, kv, kv, sink_tab)


SSM_GB = 4


def _ssm_kernel(ut_ref, m_ref, ws_ref, wo_ref, coef_ref, gt_ref,
                sf_ref, sfs_ref, sb_ref, sbs_ref, xf_ref, xb_ref, u_ref):
    for gi in range(SSM_GB):
        ut = jnp.concatenate([ut_ref[lb, gi] for lb in range(N_ROW_BLOCKS)], axis=1)
        u = jnp.transpose(ut.astype(jnp.float32)).astype(jnp.bfloat16)
        u_ref[gi] = u
        s = jnp.dot(u, ws_ref[gi], preferred_element_type=jnp.float32)
        for b in range(BATCH):
            rows = slice(b * N_CHUNKS, (b + 1) * N_CHUNKS)
            dst = pl.ds(b * SLAB_PITCH + 8, N_CHUNKS)
            sf_ref[gi, dst, :] = s[rows, 0 * LANES:1 * LANES]
            sfs_ref[gi, dst, :] = s[rows, 1 * LANES:2 * LANES]
            sb_ref[gi, dst, :] = s[rows, 2 * LANES:3 * LANES]
            sbs_ref[gi, dst, :] = s[rows, 3 * LANES:4 * LANES]

    zero = jnp.zeros((BATCH, LANES), jnp.float32)
    coefs = []
    for gi in range(SSM_GB):
        coefs.append([jnp.broadcast_to(coef_ref[gi, r:r + 1, :], (BATCH, LANES)) for r in range(6)])
        xf_ref[gi, pl.ds(8, BATCH, stride=SLAB_PITCH), :] = zero
        xb_ref[gi, pl.ds(8 + N_CHUNKS - 1, BATCH, stride=SLAB_PITCH), :] = zero

    def rows_at(r):
        return pl.ds(r, BATCH, stride=SLAB_PITCH)

    def step(k, carry):
        kb = N_CHUNKS - 1 - k
        out = []
        for gi in range(SSM_GB):
            a1f, a2f, a3f, a1b, a2b, a3b = coefs[gi]
            xf, xfs, xb, xbs = carry[4 * gi:4 * gi + 4]
            nxf = a1f * xf + a2f * xfs + sf_ref[gi, rows_at(8 + k), :]
            nxfs = a1f * xfs + a3f * xf + sfs_ref[gi, rows_at(8 + k), :]
            xf_ref[gi, rows_at(9 + k), :] = nxf
            nxb = a1b * xb + a2b * xbs + sb_ref[gi, rows_at(8 + kb), :]
            nxbs = a1b * xbs + a3b * xb + sbs_ref[gi, rows_at(8 + kb), :]
            xb_ref[gi, rows_at(7 + kb), :] = nxb
            out += [nxf, nxfs, nxb, nxbs]
        return tuple(out)

    lax.fori_loop(0, N_CHUNKS, step, (zero,) * (4 * SSM_GB), unroll=2)

    def chunk_out(gi):
        xin_f = jnp.concatenate([xf_ref[gi, pl.ds(b * SLAB_PITCH + 8, N_CHUNKS), :] for b in range(BATCH)], axis=0)
        xin_b = jnp.concatenate([xb_ref[gi, pl.ds(b * SLAB_PITCH + 8, N_CHUNKS), :] for b in range(BATCH)], axis=0)
        lhs = jnp.concatenate([u_ref[gi], xin_f.astype(jnp.bfloat16), xin_b.astype(jnp.bfloat16)], axis=1)
        rhs = jnp.concatenate([m_ref[gi], wo_ref[gi]], axis=0)
        return jnp.dot(lhs, rhs, preferred_element_type=jnp.float32)

    y = chunk_out(0)
    for gi in range(SSM_GB):
        y_next = chunk_out(gi + 1) if gi + 1 < SSM_GB else None
        gt = jnp.transpose(jax.nn.gelu(y)).astype(gt_ref.dtype)
        for lb in range(N_ROW_BLOCKS):
            gt_ref[lb, gi] = gt[:, lb * LANES:(lb + 1) * LANES]
        y = y_next


def _ssm(ut, m_intra, w_state, w_out, coef):
    def per_group(shape):
        return pl.BlockSpec((SSM_GB,) + shape, lambda g: (g, 0, 0))

    lane_blocked = pl.BlockSpec((N_ROW_BLOCKS, SSM_GB, CHUNK_VEC, LANES), lambda g: (0, g, 0, 0))
    slab = pltpu.VMEM((SSM_GB, SLAB_ROWS, LANES), jnp.float32)
    return pl.pallas_call(
        _ssm_kernel,
        grid=(N_GROUPS // SSM_GB,),
        in_specs=[lane_blocked, per_group((CHUNK_VEC, CHUNK_VEC)),
                  per_group((CHUNK_VEC, 4 * LANES)), per_group((2 * STATE_VEC, CHUNK_VEC)),
                  per_group((8, LANES))],
        out_specs=lane_blocked,
        out_shape=jax.ShapeDtypeStruct((N_ROW_BLOCKS, N_GROUPS, CHUNK_VEC, LANES), jnp.bfloat16),
        scratch_shapes=[slab] * 6 + [pltpu.VMEM((SSM_GB, ROWS, CHUNK_VEC), jnp.bfloat16)],
        compiler_params=pltpu.CompilerParams(dimension_semantics=("arbitrary",),
                                             vmem_limit_bytes=VMEM_LIMIT),
        name="ssm",
    )(ut, m_intra, w_state, w_out, coef)


FINAL_SUB_ROWS = 256


def _unit_rms(v):
    return v * lax.rsqrt(jnp.mean(v * v, axis=-1, keepdims=True) + NORM_EPS)


def _final_kernel(x_ref, ya_ref, sz_ref, gt_ref, wglu_ref, bglu_ref,
                  woa_ref, wos_ref, lng_ref, lnb_ref, o_ref, gslab_ref):
    step = pl.program_id(0)
    part = step % STEPS_PER_REGROUP

    @pl.when(part == 0)
    def _():
        for j in range(CHUNK):
            for v in range(N_LANE_BLOCKS):
                g0 = v * GROUPS_PER_LANE_BLOCK
                gjt = gt_ref[g0:g0 + GROUPS_PER_LANE_BLOCK, j * SSM_CH:(j + 1) * SSM_CH, :].astype(jnp.float32)
                gj = jnp.transpose(gjt.reshape(LANES, LANES))
                gslab_ref[v, pl.ds(j, LANES, stride=CHUNK), :] = gj

    def rows_of(sub):
        return slice(sub * FINAL_SUB_ROWS, (sub + 1) * FINAL_SUB_ROWS)

    def gate_in(sub):
        rows = rows_of(sub)
        ya = ya_ref[rows, :].astype(jnp.float32) * sz_ref[rows, :D_ATTN].astype(jnp.float32)
        ya = _unit_rms(ya).astype(jnp.bfloat16)
        src = pl.ds(pl.multiple_of(part * TOK_BLOCK + sub * FINAL_SUB_ROWS, FINAL_SUB_ROWS), FINAL_SUB_ROWS)
        g = jnp.concatenate([gslab_ref[v, src, :] for v in range(N_LANE_BLOCKS)], axis=1)
        return ya, g

    def glu(sub, ya, g):
        rows = rows_of(sub)
        gate = jnp.dot(g.astype(jnp.bfloat16), wglu_ref[...], preferred_element_type=jnp.float32) + bglu_ref[...]
        ys = g * jax.nn.sigmoid(gate)
        ys = ys * sz_ref[rows, D_ATTN:].astype(jnp.float32)
        return ya, _unit_rms(ys).astype(jnp.bfloat16)

    def project(sub, ya, ys):
        return (jnp.dot(ya, woa_ref[...], preferred_element_type=jnp.float32)
                + jnp.dot(ys, wos_ref[...], preferred_element_type=jnp.float32))

    def layer_norm(sub, out):
        rows = rows_of(sub)
        h = x_ref[rows, :] + out
        mu = jnp.mean(h, axis=-1, keepdims=True)
        hc = h - mu
        var = jnp.mean(hc * hc, axis=-1, keepdims=True)
        o_ref[rows, :] = hc * lax.rsqrt(var + NORM_EPS / DEEPNORM_ALPHA ** 2) * lng_ref[...] + lnb_ref[...]

    n_sub = TOK_BLOCK // FINAL_SUB_ROWS
    stage_a, stage_b = {}, {}
    for n in range(n_sub + 2):
        if n < n_sub:
            stage_a[n] = gate_in(n)
        if 1 <= n <= n_sub:
            stage_b[n - 1] = glu(n - 1, *stage_a.pop(n - 1))
        if n >= 2:
            layer_norm(n - 2, project(n - 2, *stage_b.pop(n - 2)))


def _final(x2, ya, sz, gt, wglu, bglu, woa, wos, lng, lnb):
    n_steps = BATCH * SEQ // TOK_BLOCK

    def tok(cols):
        return pl.BlockSpec((TOK_BLOCK, cols), lambda s: (s, 0))

    def whole(shape):
        return pl.BlockSpec(shape, lambda s: (0,) * len(shape))

    return pl.pallas_call(
        _final_kernel,
        grid=(n_steps,),
        in_specs=[tok(D_MODEL), tok(D_ATTN), tok(D_ATTN + D_SSM),
                  pl.BlockSpec((None, N_GROUPS, CHUNK_VEC, LANES), lambda s: (s // STEPS_PER_REGROUP, 0, 0, 0)),
                  whole(wglu.shape), whole(bglu.shape),
                  whole(woa.shape), whole(wos.shape), whole(lng.shape), whole(lnb.shape)],
        out_specs=tok(D_MODEL),
        out_shape=jax.ShapeDtypeStruct((BATCH * SEQ, D_MODEL), jnp.float32),
        scratch_shapes=[pltpu.VMEM((N_LANE_BLOCKS, REGROUP_TOK, LANES), jnp.float32)],
        compiler_params=pltpu.CompilerParams(dimension_semantics=("arbitrary",),
                                             vmem_limit_bytes=VMEM_LIMIT),
        name="final",
    )(x2, ya, sz, gt, wglu, bglu, woa, wos, lng, lnb)


def kernel(x, w_in, attn_sink, ssm_a_re, ssm_a_im, ssm_log_dt, ssm_b_re, ssm_b_im, ssm_c_re, ssm_c_im,
           ssm_d, w_glu, b_glu, norm_attn_g, norm_ssm_g, w_out, ln_g, ln_b):
    assert x.shape == (BATCH, SEQ, D_MODEL) and w_in.shape[0] == 1
    bf16 = jnp.bfloat16
    f32 = jnp.float32
    m_intra, w_state, w_so, coef = _ssm_tables(ssm_a_re[0], ssm_a_im[0], ssm_log_dt[0], ssm_b_re[0],
                                                ssm_b_im[0], ssm_c_re[0], ssm_c_im[0], ssm_d[0])
    rope = _rope_tables()
    sink_tab = jnp.broadcast_to((attn_sink[0].astype(f32) * LOG2_E)[:, None], (N_Q_HEADS, LANES))

    x2 = x.reshape(BATCH * SEQ, D_MODEL)
    q, kv, sz, ut = _inproj(x2, w_in[0].astype(bf16), rope)
    ya = _attention(q.reshape(BATCH, SEQ, Q_COLS), kv.reshape(BATCH, SEQ, 2 * KV_COLS), sink_tab)
    gt = _ssm(ut, m_intra, w_state, w_so, coef)

    gains = jnp.concatenate([norm_attn_g[0], norm_ssm_g[0]]).astype(f32)
    wo = (w_out[0].astype(f32) * (gains / DEEPNORM_ALPHA)[:, None]).astype(bf16)
    row = lambda v: v[0].astype(f32)[None, :]
    out = _final(x2, ya.reshape(BATCH * SEQ, D_ATTN), sz, gt, w_glu[0].astype(bf16), row(b_glu),
                 wo[:D_ATTN], wo[D_ATTN:], row(ln_g), row(ln_b))
    return out.reshape(BATCH, SEQ, D_MODEL)
```

```python
import jax
import jax.numpy as jnp
import numpy as np
from jax import lax
from jax.experimental import pallas as pl
from jax.experimental.pallas import tpu as pltpu

D_MODEL = 1024
BATCH = 4
SEQ = 4096
D_ATTN = 512
D_SSM = 512
HEAD_DIM = 64
N_Q_HEADS = 8
N_KV_HEADS = 2
WINDOW = 128
BLOCK = 128
ROPE_THETA = 10000.0
SSM_CH = 16
N_GROUPS = 32
SSM_STATE = 64
NORM_EPS = 1e-5
NEG_INF = -1e30
DEEPNORM_ALPHA = 2.0 ** 0.25
LOG2_E = 1.4426950408889634
Q_SCALE = HEAD_DIM ** -0.5 * LOG2_E

Q_COLS = N_Q_HEADS * HEAD_DIM
KV_COLS = N_KV_HEADS * HEAD_DIM
CHUNK = 16
N_CHUNKS = SEQ // CHUNK
CHUNK_VEC = CHUNK * SSM_CH
STATE_VEC = 2 * SSM_STATE
LANES = 128
ROWS = BATCH * N_CHUNKS
N_ROW_BLOCKS = ROWS // LANES
SLAB_PITCH = N_CHUNKS + 8
SLAB_ROWS = BATCH * SLAB_PITCH + 8
TOK_BLOCK = 1024
REGROUP_TOK = LANES * CHUNK
STEPS_PER_REGROUP = REGROUP_TOK // TOK_BLOCK
SUB_ROWS = 512
N_LANE_BLOCKS = D_SSM // LANES
GROUPS_PER_LANE_BLOCK = LANES // SSM_CH
VMEM_LIMIT = 56 * 1024 * 1024
HIGHEST = lax.Precision.HIGHEST


LAG_ROWS = 512
PREP_GB = 4


def _prep_kernel(arow_ref, bt_ref, c_ref, dt_ref, m_ref, ws_ref, wo_ref, coef_ref):
    for gi in range(PREP_GB):
        _prep_group(gi, arow_ref, bt_ref, c_ref, dt_ref, m_ref, ws_ref, wo_ref, coef_ref)


def _prep_group(gi, arow_ref, bt_ref, c_ref, dt_ref, m_ref, ws_ref, wo_ref, coef_ref):
    f32 = jnp.float32
    lo16 = lax.broadcasted_iota(jnp.int32, (SSM_CH, LANES), 1) < SSM_STATE
    lo1 = lax.broadcasted_iota(jnp.int32, (1, LANES), 1) < SSM_STATE
    arow = arow_ref[gi]

    lag_tabs, kws, ws_parts, coef_rows = [], [], [], []
    for d in range(2):
        ar = arow[3 * d:3 * d + 1, :]
        ai = arow[3 * d + 1:3 * d + 2, :]
        dt = jnp.exp(arow[3 * d + 2:3 * d + 3, :])
        zr = dt * ar
        zi = dt * ai
        mag = jnp.exp(zr)
        lr = mag * jnp.cos(zi)
        li = mag * jnp.sin(zi)
        den = ar * ar + ai * ai
        fr = ((lr - 1.0) * ar + li * ai) / den
        fi = (li * ar - (lr - 1.0) * ai) / den
        btr = bt_ref[0, d, gi]
        bti = bt_ref[1, d, gi]
        bbr = fr * btr - fi * bti
        bbi = fr * bti + fi * btr
        cr = c_ref[0, d, gi]
        ci = c_ref[1, d, gi]

        pr = [jnp.ones((1, LANES), f32)]
        pi = [jnp.zeros((1, LANES), f32)]
        for _ in range(CHUNK):
            pr.append(pr[-1] * lr - pi[-1] * li)
            pi.append(pr[-2] * li + pi[-1] * lr)
        pa = [jnp.where(lo1, r, i) for r, i in zip(pr, pi)]
        pb = [jnp.where(lo1, -i, r) for r, i in zip(pr, pi)]
        pc = [jnp.where(lo1, i, r) for r, i in zip(pr, pi)]
        pd = [jnp.where(lo1, r, -i) for r, i in zip(pr, pi)]

        def row(t, tab):
            return tab[t]

        order = range(CHUNK + 1) if d == 0 else range(CHUNK, -1, -1)
        blocks = [cr * row(t, pa) + ci * row(t, pb) for t in order]
        blocks.append(jnp.zeros((LAG_ROWS - (CHUNK + 1) * SSM_CH, LANES), f32))
        lag_tab = jnp.concatenate(blocks, axis=0)
        lag_tabs.append(lag_tab)
        lhs = jnp.where(lo16, bbr, -bbi)
        kws.append(lax.dot_general(lhs, lag_tab, (((1,), (1,)), ((), ())),
                                   precision=HIGHEST, preferred_element_type=f32))

        parts = []
        for j in range(CHUNK):
            t = CHUNK - 1 - j if d == 0 else j
            parts.append(jnp.concatenate([bbr * row(t, pa) + bbi * row(t, pb),
                                          bbr * row(t, pc) + bbi * row(t, pd)], axis=1))
        ws_parts.append(jnp.concatenate(parts, axis=0))

        a1 = pr[CHUNK]
        a2 = jnp.where(lo1, -pi[CHUNK], pi[CHUNK])
        coef_rows += [a1, a2, -a2]

    ws_ref[gi] = jnp.concatenate(ws_parts, axis=1).astype(ws_ref.dtype)
    coef_ref[gi] = jnp.concatenate(coef_rows + [jnp.zeros((2, LANES), f32)], axis=0)

    sign = jnp.where(lax.broadcasted_iota(jnp.int32, (LANES, CHUNK_VEC), 0) < SSM_STATE, 1.0, -1.0)
    wo_f = jnp.transpose(lag_tabs[0][SSM_CH:SSM_CH + CHUNK_VEC, :]) * sign
    wo_b = jnp.transpose(lag_tabs[1][:CHUNK_VEC, :]) * sign
    wo_ref[gi] = jnp.concatenate([wo_f, wo_b], axis=0).astype(wo_ref.dtype)

    sub = lax.broadcasted_iota(jnp.int32, (SSM_CH, CHUNK_VEC), 0)
    lane = lax.broadcasted_iota(jnp.int32, (SSM_CH, CHUNK_VEC), 1)
    dtile = jnp.broadcast_to(dt_ref[gi], (SSM_CH, CHUNK_VEC))
    for jp in range(CHUNK):
        fwd = kws[0] if jp == 0 else pltpu.roll(kws[0], SSM_CH * jp, 1)
        bwd = pltpu.roll(kws[1], (LAG_ROWS - SSM_CH * (CHUNK - jp)) % LAG_ROWS, 1)
        skip = jnp.where((lane // SSM_CH == jp) & (lane % SSM_CH == sub), dtile, 0.0)
        blk = fwd[:, :CHUNK_VEC] + bwd[:, :CHUNK_VEC] + skip
        m_ref[gi, jp * SSM_CH:(jp + 1) * SSM_CH, :] = blk.astype(m_ref.dtype)


def _ssm_tables(a_re, a_im, log_dt, b_re, b_im, c_re, c_im, d):
    f32 = jnp.float32
    dup = lambda v: jnp.concatenate([v, v], axis=-1)
    ldt = jnp.broadcast_to(log_dt.astype(f32)[..., None], a_re.shape)
    zero = jnp.zeros_like(ldt[0])
    arow = dup(jnp.stack([a_re[0], a_im[0], ldt[0], a_re[1], a_im[1], ldt[1], zero, zero], axis=1).astype(f32))
    bt = dup(jnp.swapaxes(jnp.stack([b_re, b_im]).astype(f32), -1, -2))
    cc = dup(jnp.stack([c_re, c_im]).astype(f32))
    dtile = jnp.tile(d.astype(f32).reshape(N_GROUPS, 1, SSM_CH), (1, 1, CHUNK))

    bf16 = jnp.bfloat16
    return pl.pallas_call(
        _prep_kernel,
        grid=(N_GROUPS // PREP_GB,),
        in_specs=[pl.BlockSpec((PREP_GB, 8, LANES), lambda g: (g, 0, 0)),
                  pl.BlockSpec((2, 2, PREP_GB, SSM_CH, LANES), lambda g: (0, 0, g, 0, 0)),
                  pl.BlockSpec((2, 2, PREP_GB, SSM_CH, LANES), lambda g: (0, 0, g, 0, 0)),
                  pl.BlockSpec((PREP_GB, 1, CHUNK_VEC), lambda g: (g, 0, 0))],
        out_specs=[pl.BlockSpec((PREP_GB, CHUNK_VEC, CHUNK_VEC), lambda g: (g, 0, 0)),
                   pl.BlockSpec((PREP_GB, CHUNK_VEC, 4 * LANES), lambda g: (g, 0, 0)),
                   pl.BlockSpec((PREP_GB, 2 * STATE_VEC, CHUNK_VEC), lambda g: (g, 0, 0)),
                   pl.BlockSpec((PREP_GB, 8, LANES), lambda g: (g, 0, 0))],
        out_shape=[jax.ShapeDtypeStruct((N_GROUPS, CHUNK_VEC, CHUNK_VEC), bf16),
                   jax.ShapeDtypeStruct((N_GROUPS, CHUNK_VEC, 4 * LANES), bf16),
                   jax.ShapeDtypeStruct((N_GROUPS, 2 * STATE_VEC, CHUNK_VEC), bf16),
                   jax.ShapeDtypeStruct((N_GROUPS, 8, LANES), f32)],
        compiler_params=pltpu.CompilerParams(dimension_semantics=("arbitrary",),
                                             vmem_limit_bytes=VMEM_LIMIT),
        name="ssm_tables",
    )(arow, bt, cc, dtile)


def _ssm_tables_xla(a_re, a_im, log_dt, b_re, b_im, c_re, c_im, d):
    f32 = jnp.float32
    dt = jnp.exp(log_dt.astype(f32))[..., None]
    ar = a_re.astype(f32)
    ai = a_im.astype(f32)
    zr = dt * ar
    zi = dt * ai
    mag = jnp.exp(zr)
    lr = mag * jnp.cos(zi)
    li = mag * jnp.sin(zi)
    den = ar * ar + ai * ai
    fr = ((lr - 1.0) * ar + li * ai) / den
    fi = (li * ar - (lr - 1.0) * ai) / den
    br = b_re.astype(f32)
    bi = b_im.astype(f32)
    bbr = fr[..., None] * br - fi[..., None] * bi
    bbi = fr[..., None] * bi + fi[..., None] * br
    cr = c_re.astype(f32)
    ci = c_im.astype(f32)

    tau = jnp.arange(CHUNK + 1, dtype=f32)[:, None, None, None]
    pmag = jnp.exp(tau * zr[None])
    pr = pmag * jnp.cos(tau * zi[None])
    pi = pmag * jnp.sin(tau * zi[None])

    er = pr[..., None] * bbr[None] - pi[..., None] * bbi[None]
    ei = pr[..., None] * bbi[None] + pi[..., None] * bbr[None]
    gr = cr[None] * pr[:, :, :, None, :] - ci[None] * pi[:, :, :, None, :]
    gi = cr[None] * pi[:, :, :, None, :] + ci[None] * pr[:, :, :, None, :]

    kk = (jnp.einsum('dgcp,tdgpe->tdgce', cr, er, precision=HIGHEST)
          - jnp.einsum('dgcp,tdgpe->tdgce', ci, ei, precision=HIGHEST))
    kf = kk[:CHUNK, 0]
    kb = kk[:CHUNK, 1]
    dmat = d.astype(f32).reshape(N_GROUPS, SSM_CH)[:, :, None] * jnp.eye(SSM_CH, dtype=f32)[None]
    center = kf[0] + kb[0] + dmat
    kall = jnp.concatenate([kb[:0:-1], center[None], kf[1:]], axis=0)
    jj = jnp.arange(CHUNK)
    lag_idx = jj[None, :] - jj[:, None] + (CHUNK - 1)
    m5 = kall[lag_idx]
    m_intra = jnp.transpose(m5, (2, 0, 4, 1, 3)).reshape(N_GROUPS, CHUNK_VEC, CHUNK_VEC)

    ef_r = er[:CHUNK, 0][::-1]
    ef_i = ei[:CHUNK, 0][::-1]
    eb_r = er[:CHUNK, 1]
    eb_i = ei[:CHUNK, 1]

    def _rows(e):
        return jnp.transpose(e, (1, 0, 3, 2)).reshape(N_GROUPS, CHUNK_VEC, SSM_STATE)

    w_state = jnp.concatenate([_rows(ef_r), _rows(ef_i), _rows(ef_i), _rows(ef_r),
                               _rows(eb_r), _rows(eb_i), _rows(eb_i), _rows(eb_r)], axis=2)

    def _cols(g):
        return jnp.transpose(g, (1, 3, 0, 2)).reshape(N_GROUPS, SSM_STATE, CHUNK_VEC)

    gf_r = gr[1:, 0]
    gf_i = gi[1:, 0]
    gb_r = gr[1:, 1][::-1]
    gb_i = gi[1:, 1][::-1]
    w_out = jnp.concatenate([_cols(gf_r), -_cols(gf_i), _cols(gb_r), -_cols(gb_i)], axis=1)

    a_r = pr[CHUNK]
    a_i = pi[CHUNK]
    a1 = jnp.concatenate([a_r, a_r], axis=-1)
    a2 = jnp.concatenate([-a_i, a_i], axis=-1)
    a3 = -a2
    zero = jnp.zeros_like(a1[0])
    coef = jnp.stack([a1[0], a2[0], a3[0], a1[1], a2[1], a3[1], zero, zero], axis=1)
    return (m_intra.astype(jnp.bfloat16), w_state.astype(jnp.bfloat16),
            w_out.astype(jnp.bfloat16), coef)


def _rope_tables():
    half = HEAD_DIM // 2
    inv_freq = ROPE_THETA ** (-np.arange(half, dtype=np.float64) / half)
    ang = np.arange(SEQ, dtype=np.float64)[:, None] * inv_freq[None, :]
    cos, sin = np.cos(ang), np.sin(ang)
    return jnp.asarray(np.concatenate([cos, cos, cos, cos, -sin, sin, -sin, sin], axis=1), jnp.float32)


def _rope_block(xb, cos_t, sin_t, first_half):
    swapped = jnp.where(first_half, pltpu.roll(xb, 96, 1), pltpu.roll(xb, 32, 1))
    return xb * cos_t + swapped * sin_t


COL_Q = (0, Q_COLS)
COL_KV = (COL_Q[1], COL_Q[1] + 2 * KV_COLS)
COL_ZA = (COL_KV[1], COL_KV[1] + D_ATTN)
COL_U = (COL_ZA[1], COL_ZA[1] + D_SSM)
COL_ZS = (COL_U[1], COL_U[1] + D_SSM)


def _silu(z):
    h = 0.5 * z
    return h + h * jnp.tanh(h)


def _inproj_kernel(x_ref, w_ref, rope_ref, q_ref, kv_ref, sz_ref, ut_ref, uslab_ref):
    step = pl.program_id(0)
    part = step % STEPS_PER_REGROUP
    lane = lax.broadcasted_iota(jnp.int32, (SUB_ROWS, LANES), 1)
    first_half = (lane % HEAD_DIM) < (HEAD_DIM // 2)

    def proj(xb, cols):
        return jnp.dot(xb, w_ref[:, cols[0]:cols[1]], preferred_element_type=jnp.float32)

    for sub in range(TOK_BLOCK // SUB_ROWS):
        rows = slice(sub * SUB_ROWS, (sub + 1) * SUB_ROWS)
        xb = x_ref[rows, :].astype(jnp.bfloat16)
        cos_t = rope_ref[rows, :LANES]
        sin_t = rope_ref[rows, LANES:]
        q = proj(xb, COL_Q)
        kv = proj(xb, COL_KV)
        q_blocks = [_rope_block(q[:, v * LANES:(v + 1) * LANES], cos_t, sin_t, first_half)
                    for v in range(Q_COLS // LANES)]
        q_ref[rows, :] = (jnp.concatenate(q_blocks, axis=1) * Q_SCALE).astype(q_ref.dtype)
        za = proj(xb, COL_ZA)
        k_rot = _rope_block(kv[:, :LANES], cos_t, sin_t, first_half)
        kv_ref[rows, :] = jnp.concatenate([k_rot, kv[:, LANES:]], axis=1).astype(kv_ref.dtype)
        u = proj(xb, COL_U)
        sz_ref[rows, :D_ATTN] = _silu(za).astype(sz_ref.dtype)
        zs = proj(xb, COL_ZS)
        for v in range(N_LANE_BLOCKS):
            dst = pl.ds(pl.multiple_of(part * TOK_BLOCK + sub * SUB_ROWS, SUB_ROWS), SUB_ROWS)
            uslab_ref[v, dst, :] = u[:, v * LANES:(v + 1) * LANES]
        sz_ref[rows, D_ATTN:] = _silu(zs).astype(sz_ref.dtype)

    @pl.when(part == STEPS_PER_REGROUP - 1)
    def _():
        for j in range(CHUNK):
            for v in range(N_LANE_BLOCKS):
                uj = uslab_ref[v, pl.ds(j, LANES, stride=CHUNK), :]
                ujt = jnp.transpose(uj.astype(ut_ref.dtype)).reshape(GROUPS_PER_LANE_BLOCK, SSM_CH, LANES)
                g0 = v * GROUPS_PER_LANE_BLOCK
                ut_ref[g0:g0 + GROUPS_PER_LANE_BLOCK, j * SSM_CH:(j + 1) * SSM_CH, :] = ujt


def _inproj(x2, w, rope):
    n_steps = BATCH * SEQ // TOK_BLOCK
    blocks_per_seq = SEQ // TOK_BLOCK

    def tok(cols):
        return pl.BlockSpec((TOK_BLOCK, cols), lambda s: (s, 0))

    return pl.pallas_call(
        _inproj_kernel,
        grid=(n_steps,),
        in_specs=[tok(D_MODEL), pl.BlockSpec(w.shape, lambda s: (0, 0)),
                  pl.BlockSpec((TOK_BLOCK, 2 * LANES), lambda s: (s % blocks_per_seq, 0))],
        out_specs=[tok(Q_COLS), tok(2 * KV_COLS), tok(D_ATTN + D_SSM),
                   pl.BlockSpec((None, N_GROUPS, CHUNK_VEC, LANES), lambda s: (s // STEPS_PER_REGROUP, 0, 0, 0))],
        out_shape=[jax.ShapeDtypeStruct((BATCH * SEQ, Q_COLS), jnp.bfloat16),
                   jax.ShapeDtypeStruct((BATCH * SEQ, 2 * KV_COLS), jnp.bfloat16),
                   jax.ShapeDtypeStruct((BATCH * SEQ, D_ATTN + D_SSM), jnp.bfloat16),
                   jax.ShapeDtypeStruct((N_ROW_BLOCKS, N_GROUPS, CHUNK_VEC, LANES), jnp.bfloat16)],
        scratch_shapes=[pltpu.VMEM((N_LANE_BLOCKS, REGROUP_TOK, LANES), jnp.float32)],
        compiler_params=pltpu.CompilerParams(dimension_semantics=("arbitrary",),
                                             vmem_limit_bytes=VMEM_LIMIT),
        name="inproj",
    )(x2, w, rope)


ATTN_SUB = 16
ATTN_STEP = ATTN_SUB * BLOCK
N_KEYS = 3 * BLOCK


def _attn_kernel(q_ref, kp_ref, kc_ref, kn_ref, sink_ref, y_ref):
    i = pl.program_id(1)
    kvw = jnp.concatenate([kp_ref[...], kc_ref[...], kn_ref[...]], axis=0).astype(jnp.float32)

    n_win = kvw.shape[0]
    lane_w = lax.broadcasted_iota(jnp.int32, (n_win, LANES), 1)
    k_lo, k_hi, vt_lo, vt_hi = [], [], [], []
    for hk in range(N_KV_HEADS):
        in_head = (lane_w >= hk * HEAD_DIM) & (lane_w < (hk + 1) * HEAD_DIM)
        k_nat = jnp.where(in_head, kvw[:, :LANES], 0.0)
        v_nat = jnp.where(in_head, kvw[:, LANES:], 0.0)
        k_oth = pltpu.roll(k_nat, HEAD_DIM, 1)
        v_oth = pltpu.roll(v_nat, HEAD_DIM, 1)
        pairs = ((k_nat, k_oth), (v_nat, v_oth)) if hk == 0 else ((k_oth, k_nat), (v_oth, v_nat))
        k_lo.append(pairs[0][0].astype(jnp.bfloat16))
        k_hi.append(pairs[0][1].astype(jnp.bfloat16))
        vt_lo.append(jnp.transpose(pairs[1][0]).astype(jnp.bfloat16))
        vt_hi.append(jnp.transpose(pairs[1][1]).astype(jnp.bfloat16))

    c = lax.broadcasted_iota(jnp.int32, (N_KEYS, BLOCK), 0)
    r = lax.broadcasted_iota(jnp.int32, (N_KEYS, BLOCK), 1)
    band = (c >= r) & (c - r <= 2 * WINDOW)
    ones_row = lax.broadcasted_iota(jnp.int32, (16, 2 * N_KEYS), 0)
    ones_col = lax.broadcasted_iota(jnp.int32, (16, 2 * N_KEYS), 1)
    ones = jnp.where(((ones_row == 0) & (ones_col < N_KEYS)) | ((ones_row == 1) & (ones_col >= N_KEYS)),
                     1.0, 0.0).astype(jnp.bfloat16)
    row_o = lax.broadcasted_iota(jnp.int32, (LANES, 2 * BLOCK), 0)
    n_blocks = SEQ // BLOCK

    def scores(t, hk):
        blk = i * ATTN_SUB + t
        lo = jnp.where(blk == 0, BLOCK, 0)
        hi = jnp.where(blk == n_blocks - 1, 2 * BLOCK, N_KEYS)
        valid1 = band & (c >= lo) & (c < hi)
        valid = jnp.concatenate([valid1, valid1], axis=1)
        win = slice(t * BLOCK, t * BLOCK + N_KEYS)
        q = q_ref[t * BLOCK:(t + 1) * BLOCK, (2 * hk) * LANES:(2 * hk + 2) * LANES]
        qq = jnp.concatenate([q[:, :LANES], q[:, LANES:]], axis=0)
        kcat = jnp.concatenate([k_lo[hk][win], k_hi[hk][win]], axis=0)
        st = lax.dot_general(kcat, qq, (((1,), (1,)), ((), ())),
                             preferred_element_type=jnp.float32)
        sps, ms, sinks = [], [], []
        for par in range(2):
            base = par * N_KEYS
            sp = jnp.concatenate(
                [jnp.where(valid[:BLOCK], st[base:base + BLOCK, :], NEG_INF),
                 st[base + BLOCK:base + 2 * BLOCK, :],
                 jnp.where(valid[2 * BLOCK:], st[base + 2 * BLOCK:base + N_KEYS, :], NEG_INF)], axis=0)
            h0 = 4 * hk + par
            h1 = h0 + 2
            sink = jnp.concatenate([sink_ref[h0:h0 + 1, :], sink_ref[h1:h1 + 1, :]], axis=1)
            sps.append(sp)
            ms.append(jnp.maximum(jnp.max(sp, axis=0, keepdims=True), sink))
            sinks.append(sink)
        return sps, ms, sinks

    def weights(sps, ms, sinks):
        return [jnp.exp2((sp - m).astype(jnp.bfloat16)) for sp, m in zip(sps, ms)], ms, sinks

    def finish(t, hk, ps, ms, sinks):
        win = slice(t * BLOCK, t * BLOCK + N_KEYS)
        vt = jnp.concatenate([jnp.concatenate([vt_lo[hk][:, win], vt_hi[hk][:, win]], axis=1), ones], axis=0)
        ot = jnp.dot(vt, jnp.concatenate(ps, axis=0), preferred_element_type=jnp.float32)
        inv0 = 1.0 / (ot[LANES:LANES + 1, :] + jnp.exp2(sinks[0] - ms[0]))
        inv1 = 1.0 / (ot[LANES + 1:LANES + 2, :] + jnp.exp2(sinks[1] - ms[1]))
        o = jnp.transpose(ot[:LANES, :] * jnp.where(row_o < HEAD_DIM, inv0, inv1))
        rows = slice(t * BLOCK, (t + 1) * BLOCK)
        y_ref[rows, (2 * hk) * LANES:(2 * hk + 1) * LANES] = o[:BLOCK].astype(y_ref.dtype)
        y_ref[rows, (2 * hk + 1) * LANES:(2 * hk + 2) * LANES] = o[BLOCK:].astype(y_ref.dtype)

    work = [(t, hk) for t in range(ATTN_SUB) for hk in range(N_KV_HEADS)]
    stage_a = {0: scores(*work[0])}
    stage_b = {}
    for n in range(len(work) + 1):
        if n + 1 < len(work):
            stage_a[n + 1] = scores(*work[n + 1])
        if n < len(work):
            stage_b[n] = weights(*stage_a.pop(n))
        if n >= 1:
            finish(*work[n - 1], *stage_b.pop(n - 1))


def _attention(q, kv, sink_tab):
    nb = SEQ // BLOCK
    return pl.pallas_call(
        _attn_kernel,
        grid=(BATCH, SEQ // ATTN_STEP),
        in_specs=[pl.BlockSpec((None, ATTN_STEP, Q_COLS), lambda b, i: (b, i, 0)),
                  pl.BlockSpec((None, BLOCK, 2 * KV_COLS),
                               lambda b, i: (b, jnp.maximum(i * ATTN_SUB - 1, 0), 0)),
                  pl.BlockSpec((None, ATTN_STEP, 2 * KV_COLS), lambda b, i: (b, i, 0)),
                  pl.BlockSpec((None, BLOCK, 2 * KV_COLS),
                               lambda b, i: (b, jnp.minimum((i + 1) * ATTN_SUB, nb - 1), 0)),
                  pl.BlockSpec((N_Q_HEADS, LANES), lambda b, i: (0, 0))],
        out_specs=pl.BlockSpec((None, ATTN_STEP, D_ATTN), lambda b, i: (b, i, 0)),
        out_shape=jax.ShapeDtypeStruct((BATCH, SEQ, D_ATTN), jnp.bfloat16),
        compiler_params=pltpu.CompilerParams(dimension_semantics=("arbitrary", "arbitrary"),
                                             vmem_limit_bytes=VMEM_LIMIT),
        name="attention",
    )(q, kv, kv, kv, sink_tab)


SSM_GB = 4


def _ssm_kernel(ut_ref, m_ref, ws_ref, wo_ref, coef_ref, gt_ref,
                sf_ref, sfs_ref, sb_ref, sbs_ref, xf_ref, xb_ref, u_ref):
    for gi in range(SSM_GB):
        ut = jnp.concatenate([ut_ref[lb, gi] for lb in range(N_ROW_BLOCKS)], axis=1)
        u = jnp.transpose(ut.astype(jnp.float32)).astype(jnp.bfloat16)
        u_ref[gi] = u
        s = jnp.dot(u, ws_ref[gi], preferred_element_type=jnp.float32)
        for b in range(BATCH):
            rows = slice(b * N_CHUNKS, (b + 1) * N_CHUNKS)
            dst = pl.ds(b * SLAB_PITCH + 8, N_CHUNKS)
            sf_ref[gi, dst, :] = s[rows, 0 * LANES:1 * LANES]
            sfs_ref[gi, dst, :] = s[rows, 1 * LANES:2 * LANES]
            sb_ref[gi, dst, :] = s[rows, 2 * LANES:3 * LANES]
            sbs_ref[gi, dst, :] = s[rows, 3 * LANES:4 * LANES]

    zero = jnp.zeros((BATCH, LANES), jnp.float32)
    coefs = []
    for gi in range(SSM_GB):
        coefs.append([jnp.broadcast_to(coef_ref[gi, r:r + 1, :], (BATCH, LANES)) for r in range(6)])
        xf_ref[gi, pl.ds(8, BATCH, stride=SLAB_PITCH), :] = zero
        xb_ref[gi, pl.ds(8 + N_CHUNKS - 1, BATCH, stride=SLAB_PITCH), :] = zero

    def rows_at(r):
        return pl.ds(r, BATCH, stride=SLAB_PITCH)

    def step(k, carry):
        kb = N_CHUNKS - 1 - k
        out = []
        for gi in range(SSM_GB):
            a1f, a2f, a3f, a1b, a2b, a3b = coefs[gi]
            xf, xfs, xb, xbs = carry[4 * gi:4 * gi + 4]
            nxf = a1f * xf + a2f * xfs + sf_ref[gi, rows_at(8 + k), :]
            nxfs = a1f * xfs + a3f * xf + sfs_ref[gi, rows_at(8 + k), :]
            xf_ref[gi, rows_at(9 + k), :] = nxf
            nxb = a1b * xb + a2b * xbs + sb_ref[gi, rows_at(8 + kb), :]
            nxbs = a1b * xbs + a3b * xb + sbs_ref[gi, rows_at(8 + kb), :]
            xb_ref[gi, rows_at(7 + kb), :] = nxb
            out += [nxf, nxfs, nxb, nxbs]
        return tuple(out)

    lax.fori_loop(0, N_CHUNKS, step, (zero,) * (4 * SSM_GB), unroll=2)

    def chunk_out(gi):
        xin_f = jnp.concatenate([xf_ref[gi, pl.ds(b * SLAB_PITCH + 8, N_CHUNKS), :] for b in range(BATCH)], axis=0)
        xin_b = jnp.concatenate([xb_ref[gi, pl.ds(b * SLAB_PITCH + 8, N_CHUNKS), :] for b in range(BATCH)], axis=0)
        lhs = jnp.concatenate([u_ref[gi], xin_f.astype(jnp.bfloat16), xin_b.astype(jnp.bfloat16)], axis=1)
        rhs = jnp.concatenate([m_ref[gi], wo_ref[gi]], axis=0)
        return jnp.dot(lhs, rhs, preferred_element_type=jnp.float32)

    y = chunk_out(0)
    for gi in range(SSM_GB):
        y_next = chunk_out(gi + 1) if gi + 1 < SSM_GB else None
        gt = jnp.transpose(jax.nn.gelu(y)).astype(gt_ref.dtype)
        for lb in range(N_ROW_BLOCKS):
            gt_ref[lb, gi] = gt[:, lb * LANES:(lb + 1) * LANES]
        y = y_next


def _ssm(ut, m_intra, w_state, w_out, coef):
    def per_group(shape):
        return pl.BlockSpec((SSM_GB,) + shape, lambda g: (g, 0, 0))

    lane_blocked = pl.BlockSpec((N_ROW_BLOCKS, SSM_GB, CHUNK_VEC, LANES), lambda g: (0, g, 0, 0))
    slab = pltpu.VMEM((SSM_GB, SLAB_ROWS, LANES), jnp.float32)
    return pl.pallas_call(
        _ssm_kernel,
        grid=(N_GROUPS // SSM_GB,),
        in_specs=[lane_blocked, per_group((CHUNK_VEC, CHUNK_VEC)),
                  per_group((CHUNK_VEC, 4 * LANES)), per_group((2 * STATE_VEC, CHUNK_VEC)),
                  per_group((8, LANES))],
        out_specs=lane_blocked,
        out_shape=jax.ShapeDtypeStruct((N_ROW_BLOCKS, N_GROUPS, CHUNK_VEC, LANES), jnp.bfloat16),
        scratch_shapes=[slab] * 6 + [pltpu.VMEM((SSM_GB, ROWS, CHUNK_VEC), jnp.bfloat16)],
        compiler_params=pltpu.CompilerParams(dimension_semantics=("arbitrary",),
                                             vmem_limit_bytes=VMEM_LIMIT),
        name="ssm",
    )(ut, m_intra, w_state, w_out, coef)


FINAL_SUB_ROWS = 256


def _unit_rms(v):
    return v * lax.rsqrt(jnp.mean(v * v, axis=-1, keepdims=True) + NORM_EPS)


def _final_kernel(x_ref, ya_ref, sz_ref, gt_ref, wglu_ref, bglu_ref,
                  woa_ref, wos_ref, lng_ref, lnb_ref, o_ref, gslab_ref):
    step = pl.program_id(0)
    part = step % STEPS_PER_REGROUP

    @pl.when(part == 0)
    def _():
        for j in range(CHUNK):
            for v in range(N_LANE_BLOCKS):
                g0 = v * GROUPS_PER_LANE_BLOCK
                gjt = gt_ref[g0:g0 + GROUPS_PER_LANE_BLOCK, j * SSM_CH:(j + 1) * SSM_CH, :].astype(jnp.float32)
                gj = jnp.transpose(gjt.reshape(LANES, LANES))
                gslab_ref[v, pl.ds(j, LANES, stride=CHUNK), :] = gj

    def rows_of(sub):
        return slice(sub * FINAL_SUB_ROWS, (sub + 1) * FINAL_SUB_ROWS)

    def gate_in(sub):
        rows = rows_of(sub)
        ya = ya_ref[rows, :].astype(jnp.float32) * sz_ref[rows, :D_ATTN].astype(jnp.float32)
        ya = _unit_rms(ya).astype(jnp.bfloat16)
        src = pl.ds(pl.multiple_of(part * TOK_BLOCK + sub * FINAL_SUB_ROWS, FINAL_SUB_ROWS), FINAL_SUB_ROWS)
        g = jnp.concatenate([gslab_ref[v, src, :] for v in range(N_LANE_BLOCKS)], axis=1)
        return ya, g

    def glu(sub, ya, g):
        rows = rows_of(sub)
        gate = jnp.dot(g.astype(jnp.bfloat16), wglu_ref[...], preferred_element_type=jnp.float32) + bglu_ref[...]
        ys = g * jax.nn.sigmoid(gate)
        ys = ys * sz_ref[rows, D_ATTN:].astype(jnp.float32)
        return ya, _unit_rms(ys).astype(jnp.bfloat16)

    def project(sub, ya, ys):
        return (jnp.dot(ya, woa_ref[...], preferred_element_type=jnp.float32)
                + jnp.dot(ys, wos_ref[...], preferred_element_type=jnp.float32))

    def layer_norm(sub, out):
        rows = rows_of(sub)
        h = x_ref[rows, :] + out
        mu = jnp.mean(h, axis=-1, keepdims=True)
        hc = h - mu
        var = jnp.mean(hc * hc, axis=-1, keepdims=True)
        o_ref[rows, :] = hc * lax.rsqrt(var + NORM_EPS / DEEPNORM_ALPHA ** 2) * lng_ref[...] + lnb_ref[...]

    n_sub = TOK_BLOCK // FINAL_SUB_ROWS
    stage_a, stage_b = {}, {}
    for n in range(n_sub + 2):
        if n < n_sub:
            stage_a[n] = gate_in(n)
        if 1 <= n <= n_sub:
            stage_b[n - 1] = glu(n - 1, *stage_a.pop(n - 1))
        if n >= 2:
            layer_norm(n - 2, project(n - 2, *stage_b.pop(n - 2)))


def _final(x2, ya, sz, gt, wglu, bglu, woa, wos, lng, lnb):
    n_steps = BATCH * SEQ // TOK_BLOCK

    def tok(cols):
        return pl.BlockSpec((TOK_BLOCK, cols), lambda s: (s, 0))

    def whole(shape):
        return pl.BlockSpec(shape, lambda s: (0,) * len(shape))

    return pl.pallas_call(
        _final_kernel,
        grid=(n_steps,),
        in_specs=[tok(D_MODEL), tok(D_ATTN), tok(D_ATTN + D_SSM),
                  pl.BlockSpec((None, N_GROUPS, CHUNK_VEC, LANES), lambda s: (s // STEPS_PER_REGROUP, 0, 0, 0)),
                  whole(wglu.shape), whole(bglu.shape),
                  whole(woa.shape), whole(wos.shape), whole(lng.shape), whole(lnb.shape)],
        out_specs=tok(D_MODEL),
        out_shape=jax.ShapeDtypeStruct((BATCH * SEQ, D_MODEL), jnp.float32),
        scratch_shapes=[pltpu.VMEM((N_LANE_BLOCKS, REGROUP_TOK, LANES), jnp.float32)],
        compiler_params=pltpu.CompilerParams(dimension_semantics=("arbitrary",),
                                             vmem_limit_bytes=VMEM_LIMIT),
        name="final",
    )(x2, ya, sz, gt, wglu, bglu, woa, wos, lng, lnb)


def kernel(x, w_in, attn_sink, ssm_a_re, ssm_a_im, ssm_log_dt, ssm_b_re, ssm_b_im, ssm_c_re, ssm_c_im,
           ssm_d, w_glu, b_glu, norm_attn_g, norm_ssm_g, w_out, ln_g, ln_b):
    assert x.shape == (BATCH, SEQ, D_MODEL) and w_in.shape[0] == 1
    bf16 = jnp.bfloat16
    f32 = jnp.float32
    m_intra, w_state, w_so, coef = _ssm_tables(ssm_a_re[0], ssm_a_im[0], ssm_log_dt[0], ssm_b_re[0],
                                                ssm_b_im[0], ssm_c_re[0], ssm_c_im[0], ssm_d[0])
    rope = _rope_tables()
    sink_tab = jnp.broadcast_to((attn_sink[0].astype(f32) * LOG2_E)[:, None], (N_Q_HEADS, LANES))

    x2 = x.reshape(BATCH * SEQ, D_MODEL)
    q, kv, sz, ut = _inproj(x2, w_in[0].astype(bf16), rope)
    ya = _attention(q.reshape(BATCH, SEQ, Q_COLS), kv.reshape(BATCH, SEQ, 2 * KV_COLS), sink_tab)
    gt = _ssm(ut, m_intra, w_state, w_so, coef)

    gains = jnp.concatenate([norm_attn_g[0], norm_ssm_g[0]]).astype(f32)
    wo = (w_out[0].astype(f32) * (gains / DEEPNORM_ALPHA)[:, None]).astype(bf16)
    row = lambda v: v[0].astype(f32)[None, :]
    out = _final(x2, ya.reshape(BATCH * SEQ, D_ATTN), sz, gt, w_glu[0].astype(bf16), row(b_glu),
                 wo[:D_ATTN], wo[D_ATTN:], row(ln_g), row(ln_b))
    return out.reshape(BATCH, SEQ, D_MODEL)
```

```python
import jax
import jax.numpy as jnp
import numpy as np
from jax import lax
from jax.experimental import pallas as pl
from jax.experimental.pallas import tpu as pltpu

D_MODEL = 1024
BATCH = 4
SEQ = 4096
D_ATTN = 512
D_SSM = 512
HEAD_DIM = 64
N_Q_HEADS = 8
N_KV_HEADS = 2
WINDOW = 128
BLOCK = 128
ROPE_THETA = 10000.0
SSM_CH = 16
N_GROUPS = 32
SSM_STATE = 64
NORM_EPS = 1e-5
NEG_INF = -1e30
DEEPNORM_ALPHA = 2.0 ** 0.25
LOG2_E = 1.4426950408889634
Q_SCALE = HEAD_DIM ** -0.5 * LOG2_E

Q_COLS = N_Q_HEADS * HEAD_DIM
KV_COLS = N_KV_HEADS * HEAD_DIM
CHUNK = 16
N_CHUNKS = SEQ // CHUNK
CHUNK_VEC = CHUNK * SSM_CH
STATE_VEC = 2 * SSM_STATE
LANES = 128
ROWS = BATCH * N_CHUNKS
N_ROW_BLOCKS = ROWS // LANES
SLAB_PITCH = N_CHUNKS + 8
SLAB_ROWS = BATCH * SLAB_PITCH + 8
TOK_BLOCK = 1024
REGROUP_TOK = LANES * CHUNK
STEPS_PER_REGROUP = REGROUP_TOK // TOK_BLOCK
SUB_ROWS = 512
N_LANE_BLOCKS = D_SSM // LANES
SLAB_CHUNK_PITCH = 20
GROUPS_PER_LANE_BLOCK = LANES // SSM_CH
VMEM_LIMIT = 56 * 1024 * 1024
HIGHEST = lax.Precision.HIGHEST


LAG_ROWS = 512
PREP_GB = 4


def _prep_kernel(arow_ref, bt_ref, c_ref, dt_ref, m_ref, ws_ref, wo_ref, coef_ref):
    for gi in range(PREP_GB):
        _prep_group(gi, arow_ref, bt_ref, c_ref, dt_ref, m_ref, ws_ref, wo_ref, coef_ref)


def _prep_group(gi, arow_ref, bt_ref, c_ref, dt_ref, m_ref, ws_ref, wo_ref, coef_ref):
    f32 = jnp.float32
    lo16 = lax.broadcasted_iota(jnp.int32, (SSM_CH, LANES), 1) < SSM_STATE
    lo1 = lax.broadcasted_iota(jnp.int32, (1, LANES), 1) < SSM_STATE
    arow = arow_ref[gi]

    lag_tabs, kws, ws_parts, coef_rows = [], [], [], []
    for d in range(2):
        ar = arow[3 * d:3 * d + 1, :]
        ai = arow[3 * d + 1:3 * d + 2, :]
        dt = jnp.exp(arow[3 * d + 2:3 * d + 3, :])
        zr = dt * ar
        zi = dt * ai
        mag = jnp.exp(zr)
        lr = mag * jnp.cos(zi)
        li = mag * jnp.sin(zi)
        den = ar * ar + ai * ai
        fr = ((lr - 1.0) * ar + li * ai) / den
        fi = (li * ar - (lr - 1.0) * ai) / den
        btr = bt_ref[0, d, gi]
        bti = bt_ref[1, d, gi]
        bbr = fr * btr - fi * bti
        bbi = fr * bti + fi * btr
        cr = c_ref[0, d, gi]
        ci = c_ref[1, d, gi]

        pr = [jnp.ones((1, LANES), f32)]
        pi = [jnp.zeros((1, LANES), f32)]
        for _ in range(CHUNK):
            pr.append(pr[-1] * lr - pi[-1] * li)
            pi.append(pr[-2] * li + pi[-1] * lr)
        pa = [jnp.where(lo1, r, i) for r, i in zip(pr, pi)]
        pb = [jnp.where(lo1, -i, r) for r, i in zip(pr, pi)]
        pc = [jnp.where(lo1, i, r) for r, i in zip(pr, pi)]
        pd = [jnp.where(lo1, r, -i) for r, i in zip(pr, pi)]

        def row(t, tab):
            return tab[t]

        order = range(CHUNK + 1) if d == 0 else range(CHUNK, -1, -1)
        blocks = [cr * row(t, pa) + ci * row(t, pb) for t in order]
        blocks.append(jnp.zeros((LAG_ROWS - (CHUNK + 1) * SSM_CH, LANES), f32))
        lag_tab = jnp.concatenate(blocks, axis=0)
        lag_tabs.append(lag_tab)
        lhs = jnp.where(lo16, bbr, -bbi)
        kws.append(lax.dot_general(lhs, lag_tab, (((1,), (1,)), ((), ())),
                                   precision=HIGHEST, preferred_element_type=f32))

        parts = []
        for j in range(CHUNK):
            t = CHUNK - 1 - j if d == 0 else j
            parts.append(jnp.concatenate([bbr * row(t, pa) + bbi * row(t, pb),
                                          bbr * row(t, pc) + bbi * row(t, pd)], axis=1))
        ws_parts.append(jnp.concatenate(parts, axis=0))

        a1 = pr[CHUNK]
        a2 = jnp.where(lo1, -pi[CHUNK], pi[CHUNK])
        coef_rows += [a1, a2, -a2]

    ws_ref[gi] = jnp.concatenate(ws_parts, axis=1).astype(ws_ref.dtype)
    coef_ref[gi] = jnp.concatenate(coef_rows + [jnp.zeros((2, LANES), f32)], axis=0)

    sign = jnp.where(lax.broadcasted_iota(jnp.int32, (LANES, CHUNK_VEC), 0) < SSM_STATE, 1.0, -1.0)
    wo_f = jnp.transpose(lag_tabs[0][SSM_CH:SSM_CH + CHUNK_VEC, :]) * sign
    wo_b = jnp.transpose(lag_tabs[1][:CHUNK_VEC, :]) * sign
    wo_ref[gi] = jnp.concatenate([wo_f, wo_b], axis=0).astype(wo_ref.dtype)

    sub = lax.broadcasted_iota(jnp.int32, (SSM_CH, CHUNK_VEC), 0)
    lane = lax.broadcasted_iota(jnp.int32, (SSM_CH, CHUNK_VEC), 1)
    dtile = jnp.broadcast_to(dt_ref[gi], (SSM_CH, CHUNK_VEC))
    for jp in range(CHUNK):
        fwd = kws[0] if jp == 0 else pltpu.roll(kws[0], SSM_CH * jp, 1)
        bwd = pltpu.roll(kws[1], (LAG_ROWS - SSM_CH * (CHUNK - jp)) % LAG_ROWS, 1)
        skip = jnp.where((lane // SSM_CH == jp) & (lane % SSM_CH == sub), dtile, 0.0)
        blk = fwd[:, :CHUNK_VEC] + bwd[:, :CHUNK_VEC] + skip
        m_ref[gi, jp * SSM_CH:(jp + 1) * SSM_CH, :] = blk.astype(m_ref.dtype)


def _ssm_tables(a_re, a_im, log_dt, b_re, b_im, c_re, c_im, d):
    f32 = jnp.float32
    dup = lambda v: jnp.concatenate([v, v], axis=-1)
    ldt = jnp.broadcast_to(log_dt.astype(f32)[..., None], a_re.shape)
    zero = jnp.zeros_like(ldt[0])
    arow = dup(jnp.stack([a_re[0], a_im[0], ldt[0], a_re[1], a_im[1], ldt[1], zero, zero], axis=1).astype(f32))
    bt = dup(jnp.swapaxes(jnp.stack([b_re, b_im]).astype(f32), -1, -2))
    cc = dup(jnp.stack([c_re, c_im]).astype(f32))
    dtile = jnp.tile(d.astype(f32).reshape(N_GROUPS, 1, SSM_CH), (1, 1, CHUNK))

    bf16 = jnp.bfloat16
    return pl.pallas_call(
        _prep_kernel,
        grid=(N_GROUPS // PREP_GB,),
        in_specs=[pl.BlockSpec((PREP_GB, 8, LANES), lambda g: (g, 0, 0)),
                  pl.BlockSpec((2, 2, PREP_GB, SSM_CH, LANES), lambda g: (0, 0, g, 0, 0)),
                  pl.BlockSpec((2, 2, PREP_GB, SSM_CH, LANES), lambda g: (0, 0, g, 0, 0)),
                  pl.BlockSpec((PREP_GB, 1, CHUNK_VEC), lambda g: (g, 0, 0))],
        out_specs=[pl.BlockSpec((PREP_GB, CHUNK_VEC, CHUNK_VEC), lambda g: (g, 0, 0)),
                   pl.BlockSpec((PREP_GB, CHUNK_VEC, 4 * LANES), lambda g: (g, 0, 0)),
                   pl.BlockSpec((PREP_GB, 2 * STATE_VEC, CHUNK_VEC), lambda g: (g, 0, 0)),
                   pl.BlockSpec((PREP_GB, 8, LANES), lambda g: (g, 0, 0))],
        out_shape=[jax.ShapeDtypeStruct((N_GROUPS, CHUNK_VEC, CHUNK_VEC), bf16),
                   jax.ShapeDtypeStruct((N_GROUPS, CHUNK_VEC, 4 * LANES), bf16),
                   jax.ShapeDtypeStruct((N_GROUPS, 2 * STATE_VEC, CHUNK_VEC), bf16),
                   jax.ShapeDtypeStruct((N_GROUPS, 8, LANES), f32)],
        compiler_params=pltpu.CompilerParams(dimension_semantics=("arbitrary",),
                                             vmem_limit_bytes=VMEM_LIMIT),
        name="ssm_tables",
    )(arow, bt, cc, dtile)


def _ssm_tables_xla(a_re, a_im, log_dt, b_re, b_im, c_re, c_im, d):
    f32 = jnp.float32
    dt = jnp.exp(log_dt.astype(f32))[..., None]
    ar = a_re.astype(f32)
    ai = a_im.astype(f32)
    zr = dt * ar
    zi = dt * ai
    mag = jnp.exp(zr)
    lr = mag * jnp.cos(zi)
    li = mag * jnp.sin(zi)
    den = ar * ar + ai * ai
    fr = ((lr - 1.0) * ar + li * ai) / den
    fi = (li * ar - (lr - 1.0) * ai) / den
    br = b_re.astype(f32)
    bi = b_im.astype(f32)
    bbr = fr[..., None] * br - fi[..., None] * bi
    bbi = fr[..., None] * bi + fi[..., None] * br
    cr = c_re.astype(f32)
    ci = c_im.astype(f32)

    tau = jnp.arange(CHUNK + 1, dtype=f32)[:, None, None, None]
    pmag = jnp.exp(tau * zr[None])
    pr = pmag * jnp.cos(tau * zi[None])
    pi = pmag * jnp.sin(tau * zi[None])

    er = pr[..., None] * bbr[None] - pi[..., None] * bbi[None]
    ei = pr[..., None] * bbi[None] + pi[..., None] * bbr[None]
    gr = cr[None] * pr[:, :, :, None, :] - ci[None] * pi[:, :, :, None, :]
    gi = cr[None] * pi[:, :, :, None, :] + ci[None] * pr[:, :, :, None, :]

    kk = (jnp.einsum('dgcp,tdgpe->tdgce', cr, er, precision=HIGHEST)
          - jnp.einsum('dgcp,tdgpe->tdgce', ci, ei, precision=HIGHEST))
    kf = kk[:CHUNK, 0]
    kb = kk[:CHUNK, 1]
    dmat = d.astype(f32).reshape(N_GROUPS, SSM_CH)[:, :, None] * jnp.eye(SSM_CH, dtype=f32)[None]
    center = kf[0] + kb[0] + dmat
    kall = jnp.concatenate([kb[:0:-1], center[None], kf[1:]], axis=0)
    jj = jnp.arange(CHUNK)
    lag_idx = jj[None, :] - jj[:, None] + (CHUNK - 1)
    m5 = kall[lag_idx]
    m_intra = jnp.transpose(m5, (2, 0, 4, 1, 3)).reshape(N_GROUPS, CHUNK_VEC, CHUNK_VEC)

    ef_r = er[:CHUNK, 0][::-1]
    ef_i = ei[:CHUNK, 0][::-1]
    eb_r = er[:CHUNK, 1]
    eb_i = ei[:CHUNK, 1]

    def _rows(e):
        return jnp.transpose(e, (1, 0, 3, 2)).reshape(N_GROUPS, CHUNK_VEC, SSM_STATE)

    w_state = jnp.concatenate([_rows(ef_r), _rows(ef_i), _rows(ef_i), _rows(ef_r),
                               _rows(eb_r), _rows(eb_i), _rows(eb_i), _rows(eb_r)], axis=2)

    def _cols(g):
        return jnp.transpose(g, (1, 3, 0, 2)).reshape(N_GROUPS, SSM_STATE, CHUNK_VEC)

    gf_r = gr[1:, 0]
    gf_i = gi[1:, 0]
    gb_r = gr[1:, 1][::-1]
    gb_i = gi[1:, 1][::-1]
    w_out = jnp.concatenate([_cols(gf_r), -_cols(gf_i), _cols(gb_r), -_cols(gb_i)], axis=1)

    a_r = pr[CHUNK]
    a_i = pi[CHUNK]
    a1 = jnp.concatenate([a_r, a_r], axis=-1)
    a2 = jnp.concatenate([-a_i, a_i], axis=-1)
    a3 = -a2
    zero = jnp.zeros_like(a1[0])
    coef = jnp.stack([a1[0], a2[0], a3[0], a1[1], a2[1], a3[1], zero, zero], axis=1)
    return (m_intra.astype(jnp.bfloat16), w_state.astype(jnp.bfloat16),
            w_out.astype(jnp.bfloat16), coef)


def _rope_tables():
    half = HEAD_DIM // 2
    inv_freq = ROPE_THETA ** (-np.arange(half, dtype=np.float64) / half)
    ang = np.arange(SEQ, dtype=np.float64)[:, None] * inv_freq[None, :]
    cos, sin = np.cos(ang), np.sin(ang)
    return jnp.asarray(np.concatenate([cos, cos, cos, cos, -sin, sin, -sin, sin], axis=1), jnp.float32)


def _rope_block(xb, cos_t, sin_t, first_half):
    swapped = jnp.where(first_half, pltpu.roll(xb, 96, 1), pltpu.roll(xb, 32, 1))
    return xb * cos_t + swapped * sin_t


COL_Q = (0, Q_COLS)
COL_KV = (COL_Q[1], COL_Q[1] + 2 * KV_COLS)
COL_ZA = (COL_KV[1], COL_KV[1] + D_ATTN)
COL_U = (COL_ZA[1], COL_ZA[1] + D_SSM)
COL_ZS = (COL_U[1], COL_U[1] + D_SSM)


def _silu(z):
    h = 0.5 * z
    return h + h * jnp.tanh(h)


def _inproj_kernel(x_ref, w_ref, rope_ref, q_ref, kv_ref, sz_ref, ut_ref, uslab_ref):
    step = pl.program_id(0)
    part = step % STEPS_PER_REGROUP
    lane = lax.broadcasted_iota(jnp.int32, (SUB_ROWS, LANES), 1)
    first_half = (lane % HEAD_DIM) < (HEAD_DIM // 2)

    def proj(xb, cols):
        return jnp.dot(xb, w_ref[:, cols[0]:cols[1]], preferred_element_type=jnp.float32)

    for sub in range(TOK_BLOCK // SUB_ROWS):
        rows = slice(sub * SUB_ROWS, (sub + 1) * SUB_ROWS)
        xb = x_ref[rows, :].astype(jnp.bfloat16)
        cos_t = rope_ref[rows, :LANES]
        sin_t = rope_ref[rows, LANES:]
        q = proj(xb, COL_Q)
        kv = proj(xb, COL_KV)
        q_blocks = [_rope_block(q[:, v * LANES:(v + 1) * LANES], cos_t, sin_t, first_half)
                    for v in range(Q_COLS // LANES)]
        q_ref[rows, :] = (jnp.concatenate(q_blocks, axis=1) * Q_SCALE).astype(q_ref.dtype)
        za = proj(xb, COL_ZA)
        k_rot = _rope_block(kv[:, :LANES], cos_t, sin_t, first_half)
        kv_ref[rows, :] = jnp.concatenate([k_rot, kv[:, LANES:]], axis=1).astype(kv_ref.dtype)
        u = proj(xb, COL_U)
        sz_ref[rows, :D_ATTN] = _silu(za).astype(sz_ref.dtype)
        zs = proj(xb, COL_ZS)
        for kk in range(SUB_ROWS // CHUNK):
            base = pl.multiple_of((part * (TOK_BLOCK // CHUNK) + sub * (SUB_ROWS // CHUNK)) * SLAB_CHUNK_PITCH, 8)
            for v in range(N_LANE_BLOCKS):
                uslab_ref[v, pl.ds(base + kk * SLAB_CHUNK_PITCH, CHUNK), :] = (
                    u[kk * CHUNK:(kk + 1) * CHUNK, v * LANES:(v + 1) * LANES])
        sz_ref[rows, D_ATTN:] = _silu(zs).astype(sz_ref.dtype)

    @pl.when(part == STEPS_PER_REGROUP - 1)
    def _():
        units = [(j, v) for j in range(CHUNK) for v in range(N_LANE_BLOCKS)]

        def gather(j, v):
            return uslab_ref[v, pl.ds(j, LANES, stride=SLAB_CHUNK_PITCH), :].astype(ut_ref.dtype)

        nxt = gather(*units[0])
        for n, (j, v) in enumerate(units):
            uj, nxt = nxt, (gather(*units[n + 1]) if n + 1 < len(units) else None)
            ujt = jnp.transpose(uj).reshape(GROUPS_PER_LANE_BLOCK, SSM_CH, LANES)
            g0 = v * GROUPS_PER_LANE_BLOCK
            ut_ref[g0:g0 + GROUPS_PER_LANE_BLOCK, j * SSM_CH:(j + 1) * SSM_CH, :] = ujt


def _inproj(x2, w, rope):
    n_steps = BATCH * SEQ // TOK_BLOCK
    blocks_per_seq = SEQ // TOK_BLOCK

    def tok(cols):
        return pl.BlockSpec((TOK_BLOCK, cols), lambda s: (s, 0))

    return pl.pallas_call(
        _inproj_kernel,
        grid=(n_steps,),
        in_specs=[tok(D_MODEL), pl.BlockSpec(w.shape, lambda s: (0, 0)),
                  pl.BlockSpec((TOK_BLOCK, 2 * LANES), lambda s: (s % blocks_per_seq, 0))],
        out_specs=[tok(Q_COLS), tok(2 * KV_COLS), tok(D_ATTN + D_SSM),
                   pl.BlockSpec((None, N_GROUPS, CHUNK_VEC, LANES), lambda s: (s // STEPS_PER_REGROUP, 0, 0, 0))],
        out_shape=[jax.ShapeDtypeStruct((BATCH * SEQ, Q_COLS), jnp.bfloat16),
                   jax.ShapeDtypeStruct((BATCH * SEQ, 2 * KV_COLS), jnp.bfloat16),
                   jax.ShapeDtypeStruct((BATCH * SEQ, D_ATTN + D_SSM), jnp.bfloat16),
                   jax.ShapeDtypeStruct((N_ROW_BLOCKS, N_GROUPS, CHUNK_VEC, LANES), jnp.bfloat16)],
        scratch_shapes=[pltpu.VMEM((N_LANE_BLOCKS, LANES * SLAB_CHUNK_PITCH, LANES), jnp.float32)],
        compiler_params=pltpu.CompilerParams(dimension_semantics=("arbitrary",),
                                             vmem_limit_bytes=VMEM_LIMIT),
        name="inproj",
    )(x2, w, rope)


ATTN_SUB = 16
ATTN_STEP = ATTN_SUB * BLOCK
N_KEYS = 3 * BLOCK


def _attn_kernel(q_ref, kp_ref, kc_ref, kn_ref, sink_ref, y_ref):
    i = pl.program_id(1)
    kvw = jnp.concatenate([kp_ref[...], kc_ref[...], kn_ref[...]], axis=0).astype(jnp.float32)

    n_win = kvw.shape[0]
    lane_w = lax.broadcasted_iota(jnp.int32, (n_win, LANES), 1)
    k_lo, k_hi, vt_lo, vt_hi = [], [], [], []
    for hk in range(N_KV_HEADS):
        in_head = (lane_w >= hk * HEAD_DIM) & (lane_w < (hk + 1) * HEAD_DIM)
        k_nat = jnp.where(in_head, kvw[:, :LANES], 0.0)
        v_nat = jnp.where(in_head, kvw[:, LANES:], 0.0)
        k_oth = pltpu.roll(k_nat, HEAD_DIM, 1)
        v_oth = pltpu.roll(v_nat, HEAD_DIM, 1)
        pairs = ((k_nat, k_oth), (v_nat, v_oth)) if hk == 0 else ((k_oth, k_nat), (v_oth, v_nat))
        k_lo.append(pairs[0][0].astype(jnp.bfloat16))
        k_hi.append(pairs[0][1].astype(jnp.bfloat16))
        vt_lo.append(jnp.transpose(pairs[1][0]).astype(jnp.bfloat16))
        vt_hi.append(jnp.transpose(pairs[1][1]).astype(jnp.bfloat16))

    c = lax.broadcasted_iota(jnp.int32, (N_KEYS, BLOCK), 0)
    r = lax.broadcasted_iota(jnp.int32, (N_KEYS, BLOCK), 1)
    band = (c >= r) & (c - r <= 2 * WINDOW)
    ones_row = lax.broadcasted_iota(jnp.int32, (16, 2 * N_KEYS), 0)
    ones_col = lax.broadcasted_iota(jnp.int32, (16, 2 * N_KEYS), 1)
    ones = jnp.where(((ones_row == 0) & (ones_col < N_KEYS)) | ((ones_row == 1) & (ones_col >= N_KEYS)),
                     1.0, 0.0).astype(jnp.bfloat16)
    row_o = lax.broadcasted_iota(jnp.int32, (LANES, 2 * BLOCK), 0)
    n_blocks = SEQ // BLOCK

    def scores(t, hk):
        blk = i * ATTN_SUB + t
        lo = jnp.where(blk == 0, BLOCK, 0)
        hi = jnp.where(blk == n_blocks - 1, 2 * BLOCK, N_KEYS)
        valid1 = band & (c >= lo) & (c < hi)
        valid = jnp.concatenate([valid1, valid1], axis=1)
        win = slice(t * BLOCK, t * BLOCK + N_KEYS)
        q = q_ref[t * BLOCK:(t + 1) * BLOCK, (2 * hk) * LANES:(2 * hk + 2) * LANES]
        qq = jnp.concatenate([q[:, :LANES], q[:, LANES:]], axis=0)
        kcat = jnp.concatenate([k_lo[hk][win], k_hi[hk][win]], axis=0)
        st = lax.dot_general(kcat, qq, (((1,), (1,)), ((), ())),
                             preferred_element_type=jnp.float32)
        sps, ms, sinks = [], [], []
        for par in range(2):
            base = par * N_KEYS
            sp = jnp.concatenate(
                [jnp.where(valid[:BLOCK], st[base:base + BLOCK, :], NEG_INF),
                 st[base + BLOCK:base + 2 * BLOCK, :],
                 jnp.where(valid[2 * BLOCK:], st[base + 2 * BLOCK:base + N_KEYS, :], NEG_INF)], axis=0)
            h0 = 4 * hk + par
            h1 = h0 + 2
            sink = jnp.concatenate([sink_ref[h0:h0 + 1, :], sink_ref[h1:h1 + 1, :]], axis=1)
            sps.append(sp)
            ms.append(jnp.maximum(jnp.max(sp, axis=0, keepdims=True), sink))
            sinks.append(sink)
        return sps, ms, sinks

    def weights(sps, ms, sinks):
        return [jnp.exp2((sp - m).astype(jnp.bfloat16)) for sp, m in zip(sps, ms)], ms, sinks

    def finish(t, hk, ps, ms, sinks):
        win = slice(t * BLOCK, t * BLOCK + N_KEYS)
        vt = jnp.concatenate([jnp.concatenate([vt_lo[hk][:, win], vt_hi[hk][:, win]], axis=1), ones], axis=0)
        ot = jnp.dot(vt, jnp.concatenate(ps, axis=0), preferred_element_type=jnp.float32)
        inv0 = 1.0 / (ot[LANES:LANES + 1, :] + jnp.exp2(sinks[0] - ms[0]))
        inv1 = 1.0 / (ot[LANES + 1:LANES + 2, :] + jnp.exp2(sinks[1] - ms[1]))
        o = jnp.transpose(ot[:LANES, :] * jnp.where(row_o < HEAD_DIM, inv0, inv1))
        rows = slice(t * BLOCK, (t + 1) * BLOCK)
        y_ref[rows, (2 * hk) * LANES:(2 * hk + 1) * LANES] = o[:BLOCK].astype(y_ref.dtype)
        y_ref[rows, (2 * hk + 1) * LANES:(2 * hk + 2) * LANES] = o[BLOCK:].astype(y_ref.dtype)

    work = [(t, hk) for t in range(ATTN_SUB) for hk in range(N_KV_HEADS)]
    stage_a = {0: scores(*work[0])}
    stage_b = {}
    for n in range(len(work) + 1):
        if n + 1 < len(work):
            stage_a[n + 1] = scores(*work[n + 1])
        if n < len(work):
            stage_b[n] = weights(*stage_a.pop(n))
        if n >= 1:
            finish(*work[n - 1], *stage_b.pop(n - 1))


def _attention(q, kv, sink_tab):
    nb = SEQ // BLOCK
    return pl.pallas_call(
        _attn_kernel,
        grid=(BATCH, SEQ // ATTN_STEP),
        in_specs=[pl.BlockSpec((None, ATTN_STEP, Q_COLS), lambda b, i: (b, i, 0)),
                  pl.BlockSpec((None, BLOCK, 2 * KV_COLS),
                               lambda b, i: (b, jnp.maximum(i * ATTN_SUB - 1, 0), 0)),
                  pl.BlockSpec((None, ATTN_STEP, 2 * KV_COLS), lambda b, i: (b, i, 0)),
                  pl.BlockSpec((None, BLOCK, 2 * KV_COLS),
                               lambda b, i: (b, jnp.minimum((i + 1) * ATTN_SUB, nb - 1), 0)),
                  pl.BlockSpec((N_Q_HEADS, LANES), lambda b, i: (0, 0))],
        out_specs=pl.BlockSpec((None, ATTN_STEP, D_ATTN), lambda b, i: (b, i, 0)),
        out_shape=jax.ShapeDtypeStruct((BATCH, SEQ, D_ATTN), jnp.bfloat16),
        compiler_params=pltpu.CompilerParams(dimension_semantics=("arbitrary", "arbitrary"),
                                             vmem_limit_bytes=VMEM_LIMIT),
        name="attention",
    )(q, kv, kv, kv, sink_tab)


SSM_GB = 4


def _ssm_kernel(ut_ref, m_ref, ws_ref, wo_ref, coef_ref, gt_ref,
                sf_ref, sfs_ref, sb_ref, sbs_ref, xf_ref, xb_ref, u_ref):
    for gi in range(SSM_GB):
        ut = jnp.concatenate([ut_ref[lb, gi] for lb in range(N_ROW_BLOCKS)], axis=1)
        u = jnp.transpose(ut.astype(jnp.float32)).astype(jnp.bfloat16)
        u_ref[gi] = u
        s = jnp.dot(u, ws_ref[gi], preferred_element_type=jnp.float32)
        for b in range(BATCH):
            rows = slice(b * N_CHUNKS, (b + 1) * N_CHUNKS)
            dst = pl.ds(b * SLAB_PITCH + 8, N_CHUNKS)
            sf_ref[gi, dst, :] = s[rows, 0 * LANES:1 * LANES]
            sfs_ref[gi, dst, :] = s[rows, 1 * LANES:2 * LANES]
            sb_ref[gi, dst, :] = s[rows, 2 * LANES:3 * LANES]
            sbs_ref[gi, dst, :] = s[rows, 3 * LANES:4 * LANES]

    zero = jnp.zeros((BATCH, LANES), jnp.float32)
    coefs = []
    for gi in range(SSM_GB):
        coefs.append([jnp.broadcast_to(coef_ref[gi, r:r + 1, :], (BATCH, LANES)) for r in range(6)])
        xf_ref[gi, pl.ds(8, BATCH, stride=SLAB_PITCH), :] = zero
        xb_ref[gi, pl.ds(8 + N_CHUNKS - 1, BATCH, stride=SLAB_PITCH), :] = zero

    def rows_at(r):
        return pl.ds(r, BATCH, stride=SLAB_PITCH)

    def step(k, carry):
        kb = N_CHUNKS - 1 - k
        out = []
        for gi in range(SSM_GB):
            a1f, a2f, a3f, a1b, a2b, a3b = coefs[gi]
            xf, xfs, xb, xbs = carry[4 * gi:4 * gi + 4]
            nxf = a1f * xf + a2f * xfs + sf_ref[gi, rows_at(8 + k), :]
            nxfs = a1f * xfs + a3f * xf + sfs_ref[gi, rows_at(8 + k), :]
            xf_ref[gi, rows_at(9 + k), :] = nxf
            nxb = a1b * xb + a2b * xbs + sb_ref[gi, rows_at(8 + kb), :]
            nxbs = a1b * xbs + a3b * xb + sbs_ref[gi, rows_at(8 + kb), :]
            xb_ref[gi, rows_at(7 + kb), :] = nxb
            out += [nxf, nxfs, nxb, nxbs]
        return tuple(out)

    lax.fori_loop(0, N_CHUNKS, step, (zero,) * (4 * SSM_GB), unroll=2)

    def chunk_out(gi):
        xin_f = jnp.concatenate([xf_ref[gi, pl.ds(b * SLAB_PITCH + 8, N_CHUNKS), :] for b in range(BATCH)], axis=0)
        xin_b = jnp.concatenate([xb_ref[gi, pl.ds(b * SLAB_PITCH + 8, N_CHUNKS), :] for b in range(BATCH)], axis=0)
        lhs = jnp.concatenate([u_ref[gi], xin_f.astype(jnp.bfloat16), xin_b.astype(jnp.bfloat16)], axis=1)
        rhs = jnp.concatenate([m_ref[gi], wo_ref[gi]], axis=0)
        return jnp.dot(lhs, rhs, preferred_element_type=jnp.float32)

    y = chunk_out(0)
    for gi in range(SSM_GB):
        y_next = chunk_out(gi + 1) if gi + 1 < SSM_GB else None
        gt = jnp.transpose(jax.nn.gelu(y)).astype(gt_ref.dtype)
        for lb in range(N_ROW_BLOCKS):
            gt_ref[lb, gi] = gt[:, lb * LANES:(lb + 1) * LANES]
        y = y_next


def _ssm(ut, m_intra, w_state, w_out, coef):
    def per_group(shape):
        return pl.BlockSpec((SSM_GB,) + shape, lambda g: (g, 0, 0))

    lane_blocked = pl.BlockSpec((N_ROW_BLOCKS, SSM_GB, CHUNK_VEC, LANES), lambda g: (0, g, 0, 0))
    slab = pltpu.VMEM((SSM_GB, SLAB_ROWS, LANES), jnp.float32)
    return pl.pallas_call(
        _ssm_kernel,
        grid=(N_GROUPS // SSM_GB,),
        in_specs=[lane_blocked, per_group((CHUNK_VEC, CHUNK_VEC)),
                  per_group((CHUNK_VEC, 4 * LANES)), per_group((2 * STATE_VEC, CHUNK_VEC)),
                  per_group((8, LANES))],
        out_specs=lane_blocked,
        out_shape=jax.ShapeDtypeStruct((N_ROW_BLOCKS, N_GROUPS, CHUNK_VEC, LANES), jnp.bfloat16),
        scratch_shapes=[slab] * 6 + [pltpu.VMEM((SSM_GB, ROWS, CHUNK_VEC), jnp.bfloat16)],
        compiler_params=pltpu.CompilerParams(dimension_semantics=("arbitrary",),
                                             vmem_limit_bytes=VMEM_LIMIT),
        name="ssm",
    )(ut, m_intra, w_state, w_out, coef)


FINAL_SUB_ROWS = 256


def _unit_rms(v):
    return v * lax.rsqrt(jnp.mean(v * v, axis=-1, keepdims=True) + NORM_EPS)


def _final_kernel(x_ref, ya_ref, sz_ref, gt_ref, wglu_ref, bglu_ref,
                  woa_ref, wos_ref, lng_ref, lnb_ref, o_ref, gslab_ref):
    step = pl.program_id(0)
    part = step % STEPS_PER_REGROUP

    @pl.when(part == 0)
    def _():
        for j in range(CHUNK):
            for v in range(N_LANE_BLOCKS):
                g0 = v * GROUPS_PER_LANE_BLOCK
                gjt = gt_ref[g0:g0 + GROUPS_PER_LANE_BLOCK, j * SSM_CH:(j + 1) * SSM_CH, :].astype(jnp.float32)
                gj = jnp.transpose(gjt.reshape(LANES, LANES))
                gslab_ref[v, pl.ds(j, LANES, stride=SLAB_CHUNK_PITCH), :] = gj

    def rows_of(sub):
        return slice(sub * FINAL_SUB_ROWS, (sub + 1) * FINAL_SUB_ROWS)

    def gate_in(sub):
        rows = rows_of(sub)
        ya = ya_ref[rows, :].astype(jnp.float32) * sz_ref[rows, :D_ATTN].astype(jnp.float32)
        ya = _unit_rms(ya).astype(jnp.bfloat16)
        base = pl.multiple_of((part * (TOK_BLOCK // CHUNK) + sub * (FINAL_SUB_ROWS // CHUNK)) * SLAB_CHUNK_PITCH, 8)
        g = jnp.concatenate(
            [jnp.concatenate([gslab_ref[v, pl.ds(base + kk * SLAB_CHUNK_PITCH, CHUNK), :]
                              for kk in range(FINAL_SUB_ROWS // CHUNK)], axis=0)
             for v in range(N_LANE_BLOCKS)], axis=1)
        return ya, g

    def glu(sub, ya, g):
        rows = rows_of(sub)
        gate = jnp.dot(g.astype(jnp.bfloat16), wglu_ref[...], preferred_element_type=jnp.float32) + bglu_ref[...]
        ys = g * jax.nn.sigmoid(gate)
        ys = ys * sz_ref[rows, D_ATTN:].astype(jnp.float32)
        return ya, _unit_rms(ys).astype(jnp.bfloat16)

    def project(sub, ya, ys):
        return (jnp.dot(ya, woa_ref[...], preferred_element_type=jnp.float32)
                + jnp.dot(ys, wos_ref[...], preferred_element_type=jnp.float32))

    def layer_norm(sub, out):
        rows = rows_of(sub)
        h = x_ref[rows, :] + out
        mu = jnp.mean(h, axis=-1, keepdims=True)
        hc = h - mu
        var = jnp.mean(hc * hc, axis=-1, keepdims=True)
        o_ref[rows, :] = hc * lax.rsqrt(var + NORM_EPS / DEEPNORM_ALPHA ** 2) * lng_ref[...] + lnb_ref[...]

    n_sub = TOK_BLOCK // FINAL_SUB_ROWS
    stage_a, stage_b = {}, {}
    for n in range(n_sub + 2):
        if n < n_sub:
            stage_a[n] = gate_in(n)
        if 1 <= n <= n_sub:
            stage_b[n - 1] = glu(n - 1, *stage_a.pop(n - 1))
        if n >= 2:
            layer_norm(n - 2, project(n - 2, *stage_b.pop(n - 2)))


def _final(x2, ya, sz, gt, wglu, bglu, woa, wos, lng, lnb):
    n_steps = BATCH * SEQ // TOK_BLOCK

    def tok(cols):
        return pl.BlockSpec((TOK_BLOCK, cols), lambda s: (s, 0))

    def whole(shape):
        return pl.BlockSpec(shape, lambda s: (0,) * len(shape))

    return pl.pallas_call(
        _final_kernel,
        grid=(n_steps,),
        in_specs=[tok(D_MODEL), tok(D_ATTN), tok(D_ATTN + D_SSM),
                  pl.BlockSpec((None, N_GROUPS, CHUNK_VEC, LANES), lambda s: (s // STEPS_PER_REGROUP, 0, 0, 0)),
                  whole(wglu.shape), whole(bglu.shape),
                  whole(woa.shape), whole(wos.shape), whole(lng.shape), whole(lnb.shape)],
        out_specs=tok(D_MODEL),
        out_shape=jax.ShapeDtypeStruct((BATCH * SEQ, D_MODEL), jnp.float32),
        scratch_shapes=[pltpu.VMEM((N_LANE_BLOCKS, LANES * SLAB_CHUNK_PITCH, LANES), jnp.float32)],
        compiler_params=pltpu.CompilerParams(dimension_semantics=("arbitrary",),
                                             vmem_limit_bytes=VMEM_LIMIT),
        name="final",
    )(x2, ya, sz, gt, wglu, bglu, woa, wos, lng, lnb)


def kernel(x, w_in, attn_sink, ssm_a_re, ssm_a_im, ssm_log_dt, ssm_b_re, ssm_b_im, ssm_c_re, ssm_c_im,
           ssm_d, w_glu, b_glu, norm_attn_g, norm_ssm_g, w_out, ln_g, ln_b):
    assert x.shape == (BATCH, SEQ, D_MODEL) and w_in.shape[0] == 1
    bf16 = jnp.bfloat16
    f32 = jnp.float32
    m_intra, w_state, w_so, coef = _ssm_tables(ssm_a_re[0], ssm_a_im[0], ssm_log_dt[0], ssm_b_re[0],
                                                ssm_b_im[0], ssm_c_re[0], ssm_c_im[0], ssm_d[0])
    rope = _rope_tables()
    sink_tab = jnp.broadcast_to((attn_sink[0].astype(f32) * LOG2_E)[:, None], (N_Q_HEADS, LANES))

    x2 = x.reshape(BATCH * SEQ, D_MODEL)
    q, kv, sz, ut = _inproj(x2, w_in[0].astype(bf16), rope)
    ya = _attention(q.reshape(BATCH, SEQ, Q_COLS), kv.reshape(BATCH, SEQ, 2 * KV_COLS), sink_tab)
    gt = _ssm(ut, m_intra, w_state, w_so, coef)

    gains = jnp.concatenate([norm_attn_g[0], norm_ssm_g[0]]).astype(f32)
    wo = (w_out[0].astype(f32) * (gains / DEEPNORM_ALPHA)[:, None]).astype(bf16)
    row = lambda v: v[0].astype(f32)[None, :]
    out = _final(x2, ya.reshape(BATCH * SEQ, D_ATTN), sz, gt, w_glu[0].astype(bf16), row(b_glu),
                 wo[:D_ATTN], wo[D_ATTN:], row(ln_g), row(ln_b))
    return out.reshape(BATCH, SEQ, D_MODEL)
```

```python
import jax
import jax.numpy as jnp
import numpy as np
from jax import lax
from jax.experimental import pallas as pl
from jax.experimental.pallas import tpu as pltpu

D_MODEL = 1024
BATCH = 4
SEQ = 4096
D_ATTN = 512
D_SSM = 512
HEAD_DIM = 64
N_Q_HEADS = 8
N_KV_HEADS = 2
WINDOW = 128
BLOCK = 128
ROPE_THETA = 10000.0
SSM_CH = 16
N_GROUPS = 32
SSM_STATE = 64
NORM_EPS = 1e-5
NEG_INF = -1e30
DEEPNORM_ALPHA = 2.0 ** 0.25
LOG2_E = 1.4426950408889634
Q_SCALE = HEAD_DIM ** -0.5 * LOG2_E

Q_COLS = N_Q_HEADS * HEAD_DIM
KV_COLS = N_KV_HEADS * HEAD_DIM
CHUNK = 16
N_CHUNKS = SEQ // CHUNK
CHUNK_VEC = CHUNK * SSM_CH
STATE_VEC = 2 * SSM_STATE
LANES = 128
ROWS = BATCH * N_CHUNKS
N_ROW_BLOCKS = ROWS // LANES
SLAB_PITCH = N_CHUNKS + 8
SLAB_ROWS = BATCH * SLAB_PITCH + 8
TOK_BLOCK = 1024
REGROUP_TOK = LANES * CHUNK
STEPS_PER_REGROUP = REGROUP_TOK // TOK_BLOCK
SUB_ROWS = 512
N_LANE_BLOCKS = D_SSM // LANES
SLAB_CHUNK_PITCH = 20
GROUPS_PER_LANE_BLOCK = LANES // SSM_CH
VMEM_LIMIT = 56 * 1024 * 1024
HIGHEST = lax.Precision.HIGHEST


LAG_ROWS = 512
PREP_GB = 4


def _prep_kernel(ldt_ref, are_ref, aim_ref, bre_ref, bim_ref, cre_ref, cim_ref, d_ref,
                 m_ref, ws_ref, wo_ref, coef_ref):
    for gi in range(PREP_GB):
        _prep_group(gi, ldt_ref, are_ref, aim_ref, bre_ref, bim_ref, cre_ref, cim_ref, d_ref,
                    m_ref, ws_ref, wo_ref, coef_ref)


def _prep_group(gi, ldt_ref, are_ref, aim_ref, bre_ref, bim_ref, cre_ref, cim_ref, d_ref,
                m_ref, ws_ref, wo_ref, coef_ref):
    f32 = jnp.float32
    g = pl.program_id(0) * PREP_GB + gi
    lo16 = lax.broadcasted_iota(jnp.int32, (SSM_CH, LANES), 1) < SSM_STATE
    lo1 = lax.broadcasted_iota(jnp.int32, (1, LANES), 1) < SSM_STATE

    def dup(v):
        return jnp.concatenate([v, v], axis=1)

    lag_tabs, kws, ws_parts, coef_rows = [], [], [], []
    for d in range(2):
        ar = dup(are_ref[d, pl.ds(g, 1), :])
        ai = dup(aim_ref[d, pl.ds(g, 1), :])
        dt = jnp.exp(jnp.full((1, LANES), ldt_ref[d, g], f32))
        zr = dt * ar
        zi = dt * ai
        mag = jnp.exp(zr)
        lr = mag * jnp.cos(zi)
        li = mag * jnp.sin(zi)
        den = ar * ar + ai * ai
        fr = ((lr - 1.0) * ar + li * ai) / den
        fi = (li * ar - (lr - 1.0) * ai) / den
        btr = dup(jnp.transpose(bre_ref[d, gi]))
        bti = dup(jnp.transpose(bim_ref[d, gi]))
        bbr = fr * btr - fi * bti
        bbi = fr * bti + fi * btr
        cr = dup(cre_ref[d, gi])
        ci = dup(cim_ref[d, gi])

        pr = [jnp.ones((1, LANES), f32)]
        pi = [jnp.zeros((1, LANES), f32)]
        for _ in range(CHUNK):
            pr.append(pr[-1] * lr - pi[-1] * li)
            pi.append(pr[-2] * li + pi[-1] * lr)
        pa = [jnp.where(lo1, r, i) for r, i in zip(pr, pi)]
        pb = [jnp.where(lo1, -i, r) for r, i in zip(pr, pi)]
        pc = [jnp.where(lo1, i, r) for r, i in zip(pr, pi)]
        pd = [jnp.where(lo1, r, -i) for r, i in zip(pr, pi)]

        def row(t, tab):
            return tab[t]

        order = range(CHUNK + 1) if d == 0 else range(CHUNK, -1, -1)
        blocks = [cr * row(t, pa) + ci * row(t, pb) for t in order]
        blocks.append(jnp.zeros((LAG_ROWS - (CHUNK + 1) * SSM_CH, LANES), f32))
        lag_tab = jnp.concatenate(blocks, axis=0)
        lag_tabs.append(lag_tab)
        lhs = jnp.where(lo16, bbr, -bbi)
        kws.append(lax.dot_general(lhs, lag_tab, (((1,), (1,)), ((), ())),
                                   precision=HIGHEST, preferred_element_type=f32))

        parts = []
        for j in range(CHUNK):
            t = CHUNK - 1 - j if d == 0 else j
            parts.append(jnp.concatenate([bbr * row(t, pa) + bbi * row(t, pb),
                                          bbr * row(t, pc) + bbi * row(t, pd)], axis=1))
        ws_parts.append(jnp.concatenate(parts, axis=0))

        a1 = pr[CHUNK]
        a2 = jnp.where(lo1, -pi[CHUNK], pi[CHUNK])
        coef_rows += [a1, a2, -a2]

    ws_ref[gi] = jnp.concatenate(ws_parts, axis=1).astype(ws_ref.dtype)
    coef_ref[gi] = jnp.concatenate(coef_rows + [jnp.zeros((2, LANES), f32)], axis=0)

    sign = jnp.where(lax.broadcasted_iota(jnp.int32, (LANES, CHUNK_VEC), 0) < SSM_STATE, 1.0, -1.0)
    wo_f = jnp.transpose(lag_tabs[0][SSM_CH:SSM_CH + CHUNK_VEC, :]) * sign
    wo_b = jnp.transpose(lag_tabs[1][:CHUNK_VEC, :]) * sign
    wo_ref[gi] = jnp.concatenate([wo_f, wo_b], axis=0).astype(wo_ref.dtype)

    sub = lax.broadcasted_iota(jnp.int32, (SSM_CH, CHUNK_VEC), 0)
    lane = lax.broadcasted_iota(jnp.int32, (SSM_CH, CHUNK_VEC), 1)
    dblk = d_ref[:, pl.ds(pl.multiple_of((g // GROUPS_PER_LANE_BLOCK) * LANES, LANES), LANES)]
    dblk = pltpu.roll(dblk, (LANES - (g % GROUPS_PER_LANE_BLOCK) * SSM_CH) % LANES, 1)
    dblk = jnp.where(lax.broadcasted_iota(jnp.int32, (1, LANES), 1) < SSM_CH, dblk, 0.0)
    for shift in (SSM_CH, 2 * SSM_CH, 4 * SSM_CH):
        dblk = dblk + pltpu.roll(dblk, shift, 1)
    dtile = jnp.broadcast_to(jnp.concatenate([dblk, dblk], axis=1), (SSM_CH, CHUNK_VEC))
    for jp in range(CHUNK):
        fwd = kws[0] if jp == 0 else pltpu.roll(kws[0], SSM_CH * jp, 1)
        bwd = pltpu.roll(kws[1], (LAG_ROWS - SSM_CH * (CHUNK - jp)) % LAG_ROWS, 1)
        skip = jnp.where((lane // SSM_CH == jp) & (lane % SSM_CH == sub), dtile, 0.0)
        blk = fwd[:, :CHUNK_VEC] + bwd[:, :CHUNK_VEC] + skip
        m_ref[gi, jp * SSM_CH:(jp + 1) * SSM_CH, :] = blk.astype(m_ref.dtype)


def _ssm_tables(a_re, a_im, log_dt, b_re, b_im, c_re, c_im, d):
    f32 = jnp.float32
    bf16 = jnp.bfloat16
    whole = lambda a: pl.BlockSpec(a.shape, lambda g: (0,) * a.ndim)
    per_group = lambda a: pl.BlockSpec((2, PREP_GB) + a.shape[2:], lambda g: (0, g, 0, 0))
    return pl.pallas_call(
        _prep_kernel,
        grid=(N_GROUPS // PREP_GB,),
        in_specs=[pl.BlockSpec(memory_space=pltpu.SMEM), whole(a_re), whole(a_im),
                  per_group(b_re), per_group(b_im), per_group(c_re), per_group(c_im),
                  pl.BlockSpec((1, D_SSM), lambda g: (0, 0))],
        out_specs=[pl.BlockSpec((PREP_GB, CHUNK_VEC, CHUNK_VEC), lambda g: (g, 0, 0)),
                   pl.BlockSpec((PREP_GB, CHUNK_VEC, 4 * LANES), lambda g: (g, 0, 0)),
                   pl.BlockSpec((PREP_GB, 2 * STATE_VEC, CHUNK_VEC), lambda g: (g, 0, 0)),
                   pl.BlockSpec((PREP_GB, 8, LANES), lambda g: (g, 0, 0))],
        out_shape=[jax.ShapeDtypeStruct((N_GROUPS, CHUNK_VEC, CHUNK_VEC), bf16),
                   jax.ShapeDtypeStruct((N_GROUPS, CHUNK_VEC, 4 * LANES), bf16),
                   jax.ShapeDtypeStruct((N_GROUPS, 2 * STATE_VEC, CHUNK_VEC), bf16),
                   jax.ShapeDtypeStruct((N_GROUPS, 8, LANES), f32)],
        compiler_params=pltpu.CompilerParams(dimension_semantics=("arbitrary",),
                                             vmem_limit_bytes=VMEM_LIMIT),
        name="ssm_tables",
    )(log_dt.astype(f32), a_re.astype(f32), a_im.astype(f32), b_re.astype(f32), b_im.astype(f32),
      c_re.astype(f32), c_im.astype(f32), d.astype(f32).reshape(1, D_SSM))


def _ssm_tables_xla(a_re, a_im, log_dt, b_re, b_im, c_re, c_im, d):
    f32 = jnp.float32
    dt = jnp.exp(log_dt.astype(f32))[..., None]
    ar = a_re.astype(f32)
    ai = a_im.astype(f32)
    zr = dt * ar
    zi = dt * ai
    mag = jnp.exp(zr)
    lr = mag * jnp.cos(zi)
    li = mag * jnp.sin(zi)
    den = ar * ar + ai * ai
    fr = ((lr - 1.0) * ar + li * ai) / den
    fi = (li * ar - (lr - 1.0) * ai) / den
    br = b_re.astype(f32)
    bi = b_im.astype(f32)
    bbr = fr[..., None] * br - fi[..., None] * bi
    bbi = fr[..., None] * bi + fi[..., None] * br
    cr = c_re.astype(f32)
    ci = c_im.astype(f32)

    tau = jnp.arange(CHUNK + 1, dtype=f32)[:, None, None, None]
    pmag = jnp.exp(tau * zr[None])
    pr = pmag * jnp.cos(tau * zi[None])
    pi = pmag * jnp.sin(tau * zi[None])

    er = pr[..., None] * bbr[None] - pi[..., None] * bbi[None]
    ei = pr[..., None] * bbi[None] + pi[..., None] * bbr[None]
    gr = cr[None] * pr[:, :, :, None, :] - ci[None] * pi[:, :, :, None, :]
    gi = cr[None] * pi[:, :, :, None, :] + ci[None] * pr[:, :, :, None, :]

    kk = (jnp.einsum('dgcp,tdgpe->tdgce', cr, er, precision=HIGHEST)
          - jnp.einsum('dgcp,tdgpe->tdgce', ci, ei, precision=HIGHEST))
    kf = kk[:CHUNK, 0]
    kb = kk[:CHUNK, 1]
    dmat = d.astype(f32).reshape(N_GROUPS, SSM_CH)[:, :, None] * jnp.eye(SSM_CH, dtype=f32)[None]
    center = kf[0] + kb[0] + dmat
    kall = jnp.concatenate([kb[:0:-1], center[None], kf[1:]], axis=0)
    jj = jnp.arange(CHUNK)
    lag_idx = jj[None, :] - jj[:, None] + (CHUNK - 1)
    m5 = kall[lag_idx]
    m_intra = jnp.transpose(m5, (2, 0, 4, 1, 3)).reshape(N_GROUPS, CHUNK_VEC, CHUNK_VEC)

    ef_r = er[:CHUNK, 0][::-1]
    ef_i = ei[:CHUNK, 0][::-1]
    eb_r = er[:CHUNK, 1]
    eb_i = ei[:CHUNK, 1]

    def _rows(e):
        return jnp.transpose(e, (1, 0, 3, 2)).reshape(N_GROUPS, CHUNK_VEC, SSM_STATE)

    w_state = jnp.concatenate([_rows(ef_r), _rows(ef_i), _rows(ef_i), _rows(ef_r),
                               _rows(eb_r), _rows(eb_i), _rows(eb_i), _rows(eb_r)], axis=2)

    def _cols(g):
        return jnp.transpose(g, (1, 3, 0, 2)).reshape(N_GROUPS, SSM_STATE, CHUNK_VEC)

    gf_r = gr[1:, 0]
    gf_i = gi[1:, 0]
    gb_r = gr[1:, 1][::-1]
    gb_i = gi[1:, 1][::-1]
    w_out = jnp.concatenate([_cols(gf_r), -_cols(gf_i), _cols(gb_r), -_cols(gb_i)], axis=1)

    a_r = pr[CHUNK]
    a_i = pi[CHUNK]
    a1 = jnp.concatenate([a_r, a_r], axis=-1)
    a2 = jnp.concatenate([-a_i, a_i], axis=-1)
    a3 = -a2
    zero = jnp.zeros_like(a1[0])
    coef = jnp.stack([a1[0], a2[0], a3[0], a1[1], a2[1], a3[1], zero, zero], axis=1)
    return (m_intra.astype(jnp.bfloat16), w_state.astype(jnp.bfloat16),
            w_out.astype(jnp.bfloat16), coef)


def _rope_tables():
    half = HEAD_DIM // 2
    inv_freq = ROPE_THETA ** (-np.arange(half, dtype=np.float64) / half)
    ang = np.arange(SEQ, dtype=np.float64)[:, None] * inv_freq[None, :]
    cos, sin = np.cos(ang), np.sin(ang)
    return jnp.asarray(np.concatenate([cos, cos, cos, cos, -sin, sin, -sin, sin], axis=1), jnp.float32)


def _rope_block(xb, cos_t, sin_t, first_half):
    swapped = jnp.where(first_half, pltpu.roll(xb, 96, 1), pltpu.roll(xb, 32, 1))
    return xb * cos_t + swapped * sin_t


COL_Q = (0, Q_COLS)
COL_KV = (COL_Q[1], COL_Q[1] + 2 * KV_COLS)
COL_ZA = (COL_KV[1], COL_KV[1] + D_ATTN)
COL_U = (COL_ZA[1], COL_ZA[1] + D_SSM)
COL_ZS = (COL_U[1], COL_U[1] + D_SSM)


def _silu(z):
    h = 0.5 * z
    return h + h * jnp.tanh(h)


def _inproj_kernel(x_ref, w32_ref, rope_ref, q_ref, kv_ref, sz_ref, ut_ref, uslab_ref, w_ref):
    step = pl.program_id(0)
    part = step % STEPS_PER_REGROUP
    lane = lax.broadcasted_iota(jnp.int32, (SUB_ROWS, LANES), 1)
    first_half = (lane % HEAD_DIM) < (HEAD_DIM // 2)

    @pl.when(step == 0)
    def _():
        for c0 in range(0, w_ref.shape[1], 2 * LANES):
            w_ref[:, c0:c0 + 2 * LANES] = w32_ref[:, c0:c0 + 2 * LANES].astype(w_ref.dtype)

    def proj(xb, cols):
        return jnp.dot(xb, w_ref[:, cols[0]:cols[1]], preferred_element_type=jnp.float32)

    for sub in range(TOK_BLOCK // SUB_ROWS):
        rows = slice(sub * SUB_ROWS, (sub + 1) * SUB_ROWS)
        xb = x_ref[rows, :].astype(jnp.bfloat16)
        cos_t = rope_ref[rows, :LANES]
        sin_t = rope_ref[rows, LANES:]
        q = proj(xb, COL_Q)
        kv = proj(xb, COL_KV)
        q_blocks = [_rope_block(q[:, v * LANES:(v + 1) * LANES], cos_t, sin_t, first_half)
                    for v in range(Q_COLS // LANES)]
        q_ref[rows, :] = (jnp.concatenate(q_blocks, axis=1) * Q_SCALE).astype(q_ref.dtype)
        za = proj(xb, COL_ZA)
        k_rot = _rope_block(kv[:, :LANES], cos_t, sin_t, first_half)
        kv_ref[rows, :] = jnp.concatenate([k_rot, kv[:, LANES:]], axis=1).astype(kv_ref.dtype)
        u = proj(xb, COL_U)
        sz_ref[rows, :D_ATTN] = _silu(za).astype(sz_ref.dtype)
        zs = proj(xb, COL_ZS)
        for kk in range(SUB_ROWS // CHUNK):
            base = pl.multiple_of((part * (TOK_BLOCK // CHUNK) + sub * (SUB_ROWS // CHUNK)) * SLAB_CHUNK_PITCH, 8)
            for v in range(N_LANE_BLOCKS):
                uslab_ref[v, pl.ds(base + kk * SLAB_CHUNK_PITCH, CHUNK), :] = (
                    u[kk * CHUNK:(kk + 1) * CHUNK, v * LANES:(v + 1) * LANES])
        sz_ref[rows, D_ATTN:] = _silu(zs).astype(sz_ref.dtype)

    @pl.when(part == STEPS_PER_REGROUP - 1)
    def _():
        units = [(j, v) for j in range(CHUNK) for v in range(N_LANE_BLOCKS)]

        def gather(j, v):
            return uslab_ref[v, pl.ds(j, LANES, stride=SLAB_CHUNK_PITCH), :].astype(ut_ref.dtype)

        nxt = gather(*units[0])
        for n, (j, v) in enumerate(units):
            uj, nxt = nxt, (gather(*units[n + 1]) if n + 1 < len(units) else None)
            ujt = jnp.transpose(uj).reshape(GROUPS_PER_LANE_BLOCK, SSM_CH, LANES)
            g0 = v * GROUPS_PER_LANE_BLOCK
            ut_ref[g0:g0 + GROUPS_PER_LANE_BLOCK, j * SSM_CH:(j + 1) * SSM_CH, :] = ujt


def _inproj(x2, w, rope):
    n_steps = BATCH * SEQ // TOK_BLOCK
    blocks_per_seq = SEQ // TOK_BLOCK

    def tok(cols):
        return pl.BlockSpec((TOK_BLOCK, cols), lambda s: (s, 0))

    return pl.pallas_call(
        _inproj_kernel,
        grid=(n_steps,),
        in_specs=[tok(D_MODEL), pl.BlockSpec(w.shape, lambda s: (0, 0), pipeline_mode=pl.Buffered(1)),
                  pl.BlockSpec((TOK_BLOCK, 2 * LANES), lambda s: (s % blocks_per_seq, 0))],
        out_specs=[tok(Q_COLS), tok(2 * KV_COLS), tok(D_ATTN + D_SSM),
                   pl.BlockSpec((None, N_GROUPS, CHUNK_VEC, LANES), lambda s: (s // STEPS_PER_REGROUP, 0, 0, 0))],
        out_shape=[jax.ShapeDtypeStruct((BATCH * SEQ, Q_COLS), jnp.bfloat16),
                   jax.ShapeDtypeStruct((BATCH * SEQ, 2 * KV_COLS), jnp.bfloat16),
                   jax.ShapeDtypeStruct((BATCH * SEQ, D_ATTN + D_SSM), jnp.bfloat16),
                   jax.ShapeDtypeStruct((N_ROW_BLOCKS, N_GROUPS, CHUNK_VEC, LANES), jnp.bfloat16)],
        scratch_shapes=[pltpu.VMEM((N_LANE_BLOCKS, LANES * SLAB_CHUNK_PITCH, LANES), jnp.float32),
                        pltpu.VMEM(w.shape, jnp.bfloat16)],
        compiler_params=pltpu.CompilerParams(dimension_semantics=("arbitrary",),
                                             vmem_limit_bytes=VMEM_LIMIT),
        name="inproj",
    )(x2, w, rope)


ATTN_SUB = 16
ATTN_STEP = ATTN_SUB * BLOCK
N_KEYS = 3 * BLOCK


def _attn_kernel(q_ref, kp_ref, kc_ref, kn_ref, sink_ref, y_ref):
    i = pl.program_id(1)
    kvw = jnp.concatenate([kp_ref[...], kc_ref[...], kn_ref[...]], axis=0).astype(jnp.float32)

    n_win = kvw.shape[0]
    lane_w = lax.broadcasted_iota(jnp.int32, (n_win, LANES), 1)
    k_lo, k_hi, vt_lo, vt_hi = [], [], [], []
    for hk in range(N_KV_HEADS):
        in_head = (lane_w >= hk * HEAD_DIM) & (lane_w < (hk + 1) * HEAD_DIM)
        k_nat = jnp.where(in_head, kvw[:, :LANES], 0.0)
        v_nat = jnp.where(in_head, kvw[:, LANES:], 0.0)
        k_oth = pltpu.roll(k_nat, HEAD_DIM, 1)
        v_oth = pltpu.roll(v_nat, HEAD_DIM, 1)
        pairs = ((k_nat, k_oth), (v_nat, v_oth)) if hk == 0 else ((k_oth, k_nat), (v_oth, v_nat))
        k_lo.append(pairs[0][0].astype(jnp.bfloat16))
        k_hi.append(pairs[0][1].astype(jnp.bfloat16))
        vt_lo.append(jnp.transpose(pairs[1][0]).astype(jnp.bfloat16))
        vt_hi.append(jnp.transpose(pairs[1][1]).astype(jnp.bfloat16))

    c = lax.broadcasted_iota(jnp.int32, (N_KEYS, BLOCK), 0)
    r = lax.broadcasted_iota(jnp.int32, (N_KEYS, BLOCK), 1)
    band = (c >= r) & (c - r <= 2 * WINDOW)
    ones_row = lax.broadcasted_iota(jnp.int32, (16, 2 * N_KEYS), 0)
    ones_col = lax.broadcasted_iota(jnp.int32, (16, 2 * N_KEYS), 1)
    ones = jnp.where(((ones_row == 0) & (ones_col < N_KEYS)) | ((ones_row == 1) & (ones_col >= N_KEYS)),
                     1.0, 0.0).astype(jnp.bfloat16)
    row_o = lax.broadcasted_iota(jnp.int32, (LANES, 2 * BLOCK), 0)
    n_blocks = SEQ // BLOCK

    def scores(t, hk):
        blk = i * ATTN_SUB + t
        lo = jnp.where(blk == 0, BLOCK, 0)
        hi = jnp.where(blk == n_blocks - 1, 2 * BLOCK, N_KEYS)
        valid1 = band & (c >= lo) & (c < hi)
        valid = jnp.concatenate([valid1, valid1], axis=1)
        win = slice(t * BLOCK, t * BLOCK + N_KEYS)
        q = q_ref[t * BLOCK:(t + 1) * BLOCK, (2 * hk) * LANES:(2 * hk + 2) * LANES]
        qq = jnp.concatenate([q[:, :LANES], q[:, LANES:]], axis=0)
        kcat = jnp.concatenate([k_lo[hk][win], k_hi[hk][win]], axis=0)
        st = lax.dot_general(kcat, qq, (((1,), (1,)), ((), ())),
                             preferred_element_type=jnp.float32)
        sps, ms, sinks = [], [], []
        for par in range(2):
            base = par * N_KEYS
            sp = jnp.concatenate(
                [jnp.where(valid[:BLOCK], st[base:base + BLOCK, :], NEG_INF),
                 st[base + BLOCK:base + 2 * BLOCK, :],
                 jnp.where(valid[2 * BLOCK:], st[base + 2 * BLOCK:base + N_KEYS, :], NEG_INF)], axis=0)
            h0 = 4 * hk + par
            h1 = h0 + 2
            sink = jnp.concatenate([sink_ref[h0:h0 + 1, :], sink_ref[h1:h1 + 1, :]], axis=1)
            sps.append(sp)
            ms.append(jnp.maximum(jnp.max(sp, axis=0, keepdims=True), sink))
            sinks.append(sink)
        return sps, ms, sinks

    def weights(sps, ms, sinks):
        return [jnp.exp2((sp - m).astype(jnp.bfloat16)) for sp, m in zip(sps, ms)], ms, sinks

    def finish(t, hk, ps, ms, sinks):
        win = slice(t * BLOCK, t * BLOCK + N_KEYS)
        vt = jnp.concatenate([jnp.concatenate([vt_lo[hk][:, win], vt_hi[hk][:, win]], axis=1), ones], axis=0)
        ot = jnp.dot(vt, jnp.concatenate(ps, axis=0), preferred_element_type=jnp.float32)
        inv0 = 1.0 / (ot[LANES:LANES + 1, :] + jnp.exp2(sinks[0] - ms[0]))
        inv1 = 1.0 / (ot[LANES + 1:LANES + 2, :] + jnp.exp2(sinks[1] - ms[1]))
        o = jnp.transpose(ot[:LANES, :] * jnp.where(row_o < HEAD_DIM, inv0, inv1))
        rows = slice(t * BLOCK, (t + 1) * BLOCK)
        y_ref[rows, (2 * hk) * LANES:(2 * hk + 1) * LANES] = o[:BLOCK].astype(y_ref.dtype)
        y_ref[rows, (2 * hk + 1) * LANES:(2 * hk + 2) * LANES] = o[BLOCK:].astype(y_ref.dtype)

    work = [(t, hk) for t in range(ATTN_SUB) for hk in range(N_KV_HEADS)]
    stage_a = {0: scores(*work[0])}
    stage_b = {}
    for n in range(len(work) + 1):
        if n + 1 < len(work):
            stage_a[n + 1] = scores(*work[n + 1])
        if n < len(work):
            stage_b[n] = weights(*stage_a.pop(n))
        if n >= 1:
            finish(*work[n - 1], *stage_b.pop(n - 1))


def _attention(q, kv, sink_tab):
    nb = SEQ // BLOCK
    return pl.pallas_call(
        _attn_kernel,
        grid=(BATCH, SEQ // ATTN_STEP),
        in_specs=[pl.BlockSpec((None, ATTN_STEP, Q_COLS), lambda b, i: (b, i, 0)),
                  pl.BlockSpec((None, BLOCK, 2 * KV_COLS),
                               lambda b, i: (b, jnp.maximum(i * ATTN_SUB - 1, 0), 0)),
                  pl.BlockSpec((None, ATTN_STEP, 2 * KV_COLS), lambda b, i: (b, i, 0)),
                  pl.BlockSpec((None, BLOCK, 2 * KV_COLS),
                               lambda b, i: (b, jnp.minimum((i + 1) * ATTN_SUB, nb - 1), 0)),
                  pl.BlockSpec((N_Q_HEADS, LANES), lambda b, i: (0, 0))],
        out_specs=pl.BlockSpec((None, ATTN_STEP, D_ATTN), lambda b, i: (b, i, 0)),
        out_shape=jax.ShapeDtypeStruct((BATCH, SEQ, D_ATTN), jnp.bfloat16),
        compiler_params=pltpu.CompilerParams(dimension_semantics=("arbitrary", "arbitrary"),
                                             vmem_limit_bytes=VMEM_LIMIT),
        name="attention",
    )(q, kv, kv, kv, sink_tab)


SSM_GB = 4


def _ssm_kernel(ut_ref, m_ref, ws_ref, wo_ref, coef_ref, gt_ref,
                sf_ref, sfs_ref, sb_ref, sbs_ref, xf_ref, xb_ref, u_ref):
    for gi in range(SSM_GB):
        ut = jnp.concatenate([ut_ref[lb, gi] for lb in range(N_ROW_BLOCKS)], axis=1)
        u = jnp.transpose(ut)
        u_ref[gi] = u
        s = jnp.dot(u, ws_ref[gi], preferred_element_type=jnp.float32)
        for b in range(BATCH):
            rows = slice(b * N_CHUNKS, (b + 1) * N_CHUNKS)
            dst = pl.ds(b * SLAB_PITCH + 8, N_CHUNKS)
            sf_ref[gi, dst, :] = s[rows, 0 * LANES:1 * LANES]
            sfs_ref[gi, dst, :] = s[rows, 1 * LANES:2 * LANES]
            sb_ref[gi, dst, :] = s[rows, 2 * LANES:3 * LANES]
            sbs_ref[gi, dst, :] = s[rows, 3 * LANES:4 * LANES]

    zero = jnp.zeros((BATCH, LANES), jnp.float32)
    coefs = []
    for gi in range(SSM_GB):
        coefs.append([jnp.broadcast_to(coef_ref[gi, r:r + 1, :], (BATCH, LANES)) for r in range(6)])
        xf_ref[gi, pl.ds(8, BATCH, stride=SLAB_PITCH), :] = zero
        xb_ref[gi, pl.ds(8 + N_CHUNKS - 1, BATCH, stride=SLAB_PITCH), :] = zero

    def rows_at(r):
        return pl.ds(r, BATCH, stride=SLAB_PITCH)

    def step(k, carry):
        kb = N_CHUNKS - 1 - k
        out = []
        for gi in range(SSM_GB):
            a1f, a2f, a3f, a1b, a2b, a3b = coefs[gi]
            xf, xfs, xb, xbs = carry[4 * gi:4 * gi + 4]
            nxf = a1f * xf + a2f * xfs + sf_ref[gi, rows_at(8 + k), :]
            nxfs = a1f * xfs + a3f * xf + sfs_ref[gi, rows_at(8 + k), :]
            xf_ref[gi, rows_at(9 + k), :] = nxf
            nxb = a1b * xb + a2b * xbs + sb_ref[gi, rows_at(8 + kb), :]
            nxbs = a1b * xbs + a3b * xb + sbs_ref[gi, rows_at(8 + kb), :]
            xb_ref[gi, rows_at(7 + kb), :] = nxb
            out += [nxf, nxfs, nxb, nxbs]
        return tuple(out)

    lax.fori_loop(0, N_CHUNKS, step, (zero,) * (4 * SSM_GB), unroll=2)

    def chunk_out(gi):
        xin_f = jnp.concatenate([xf_ref[gi, pl.ds(b * SLAB_PITCH + 8, N_CHUNKS), :] for b in range(BATCH)], axis=0)
        xin_b = jnp.concatenate([xb_ref[gi, pl.ds(b * SLAB_PITCH + 8, N_CHUNKS), :] for b in range(BATCH)], axis=0)
        lhs = jnp.concatenate([u_ref[gi], xin_f.astype(jnp.bfloat16), xin_b.astype(jnp.bfloat16)], axis=1)
        rhs = jnp.concatenate([m_ref[gi], wo_ref[gi]], axis=0)
        return jnp.dot(lhs, rhs, preferred_element_type=jnp.float32)

    y = chunk_out(0)
    for gi in range(SSM_GB):
        y_next = chunk_out(gi + 1) if gi + 1 < SSM_GB else None
        gt = jnp.transpose(jax.nn.gelu(y).astype(gt_ref.dtype))
        for lb in range(N_ROW_BLOCKS):
            gt_ref[lb, gi] = gt[:, lb * LANES:(lb + 1) * LANES]
        y = y_next


def _ssm(ut, m_intra, w_state, w_out, coef):
    def per_group(shape):
        return pl.BlockSpec((SSM_GB,) + shape, lambda g: (g, 0, 0))

    lane_blocked = pl.BlockSpec((N_ROW_BLOCKS, SSM_GB, CHUNK_VEC, LANES), lambda g: (0, g, 0, 0))
    slab = pltpu.VMEM((SSM_GB, SLAB_ROWS, LANES), jnp.float32)
    return pl.pallas_call(
        _ssm_kernel,
        grid=(N_GROUPS // SSM_GB,),
        in_specs=[lane_blocked, per_group((CHUNK_VEC, CHUNK_VEC)),
                  per_group((CHUNK_VEC, 4 * LANES)), per_group((2 * STATE_VEC, CHUNK_VEC)),
                  per_group((8, LANES))],
        out_specs=lane_blocked,
        out_shape=jax.ShapeDtypeStruct((N_ROW_BLOCKS, N_GROUPS, CHUNK_VEC, LANES), jnp.bfloat16),
        scratch_shapes=[slab] * 6 + [pltpu.VMEM((SSM_GB, ROWS, CHUNK_VEC), jnp.bfloat16)],
        compiler_params=pltpu.CompilerParams(dimension_semantics=("arbitrary",),
                                             vmem_limit_bytes=VMEM_LIMIT),
        name="ssm",
    )(ut, m_intra, w_state, w_out, coef)


FINAL_SUB_ROWS = 256


def _unit_rms(v):
    return v * lax.rsqrt(jnp.mean(v * v, axis=-1, keepdims=True) + NORM_EPS)


def _final_kernel(x_ref, ya_ref, sz_ref, gt_ref, wglu_ref, bglu_ref,
                  wo_ref, lng_ref, lnb_ref, o_ref, gslab_ref):
    step = pl.program_id(0)
    part = step % STEPS_PER_REGROUP

    @pl.when(part == 0)
    def _():
        for j in range(CHUNK):
            for v in range(N_LANE_BLOCKS):
                g0 = v * GROUPS_PER_LANE_BLOCK
                gjt = gt_ref[g0:g0 + GROUPS_PER_LANE_BLOCK, j * SSM_CH:(j + 1) * SSM_CH, :]
                gj = jnp.transpose(gjt.reshape(LANES, LANES))
                gslab_ref[v, pl.ds(j, LANES, stride=SLAB_CHUNK_PITCH), :] = gj.astype(jnp.float32)

    def rows_of(sub):
        return slice(sub * FINAL_SUB_ROWS, (sub + 1) * FINAL_SUB_ROWS)

    def gate_in(sub):
        rows = rows_of(sub)
        ya = ya_ref[rows, :].astype(jnp.float32) * sz_ref[rows, :D_ATTN].astype(jnp.float32)
        ya = _unit_rms(ya).astype(jnp.bfloat16)
        base = pl.multiple_of((part * (TOK_BLOCK // CHUNK) + sub * (FINAL_SUB_ROWS // CHUNK)) * SLAB_CHUNK_PITCH, 8)
        g = jnp.concatenate(
            [jnp.concatenate([gslab_ref[v, pl.ds(base + kk * SLAB_CHUNK_PITCH, CHUNK), :]
                              for kk in range(FINAL_SUB_ROWS // CHUNK)], axis=0)
             for v in range(N_LANE_BLOCKS)], axis=1)
        return ya, g

    def glu(sub, ya, g):
        rows = rows_of(sub)
        gate = jnp.dot(g.astype(jnp.bfloat16), wglu_ref[...], preferred_element_type=jnp.float32) + bglu_ref[...]
        ys = g * jax.nn.sigmoid(gate)
        ys = ys * sz_ref[rows, D_ATTN:].astype(jnp.float32)
        return ya, _unit_rms(ys).astype(jnp.bfloat16)

    def project(sub, ya, ys):
        return (jnp.dot(ya, wo_ref[:D_ATTN, :], preferred_element_type=jnp.float32)
                + jnp.dot(ys, wo_ref[D_ATTN:, :], preferred_element_type=jnp.float32))

    def layer_norm(sub, out):
        rows = rows_of(sub)
        h = x_ref[rows, :] + out
        mu = jnp.mean(h, axis=-1, keepdims=True)
        hc = h - mu
        var = jnp.mean(hc * hc, axis=-1, keepdims=True)
        o_ref[rows, :] = hc * lax.rsqrt(var + NORM_EPS / DEEPNORM_ALPHA ** 2) * lng_ref[...] + lnb_ref[...]

    n_sub = TOK_BLOCK // FINAL_SUB_ROWS
    stage_a, stage_b = {}, {}
    for n in range(n_sub + 2):
        if n < n_sub:
            stage_a[n] = gate_in(n)
        if 1 <= n <= n_sub:
            stage_b[n - 1] = glu(n - 1, *stage_a.pop(n - 1))
        if n >= 2:
            layer_norm(n - 2, project(n - 2, *stage_b.pop(n - 2)))


def _final(x2, ya, sz, gt, wglu, bglu, wo, lng, lnb):
    n_steps = BATCH * SEQ // TOK_BLOCK

    def tok(cols):
        return pl.BlockSpec((TOK_BLOCK, cols), lambda s: (s, 0))

    def whole(shape):
        return pl.BlockSpec(shape, lambda s: (0,) * len(shape))

    return pl.pallas_call(
        _final_kernel,
        grid=(n_steps,),
        in_specs=[tok(D_MODEL), tok(D_ATTN), tok(D_ATTN + D_SSM),
                  pl.BlockSpec((None, N_GROUPS, CHUNK_VEC, LANES), lambda s: (s // STEPS_PER_REGROUP, 0, 0, 0)),
                  whole(wglu.shape), whole(bglu.shape),
                  whole(wo.shape), whole(lng.shape), whole(lnb.shape)],
        out_specs=tok(D_MODEL),
        out_shape=jax.ShapeDtypeStruct((BATCH * SEQ, D_MODEL), jnp.float32),
        scratch_shapes=[pltpu.VMEM((N_LANE_BLOCKS, LANES * SLAB_CHUNK_PITCH, LANES), jnp.float32)],
        compiler_params=pltpu.CompilerParams(dimension_semantics=("arbitrary",),
                                             vmem_limit_bytes=VMEM_LIMIT),
        name="final",
    )(x2, ya, sz, gt, wglu, bglu, wo, lng, lnb)


def kernel(x, w_in, attn_sink, ssm_a_re, ssm_a_im, ssm_log_dt, ssm_b_re, ssm_b_im, ssm_c_re, ssm_c_im,
           ssm_d, w_glu, b_glu, norm_attn_g, norm_ssm_g, w_out, ln_g, ln_b):
    assert x.shape == (BATCH, SEQ, D_MODEL) and w_in.shape[0] == 1
    bf16 = jnp.bfloat16
    f32 = jnp.float32
    m_intra, w_state, w_so, coef = _ssm_tables(ssm_a_re[0], ssm_a_im[0], ssm_log_dt[0], ssm_b_re[0],
                                                ssm_b_im[0], ssm_c_re[0], ssm_c_im[0], ssm_d[0])
    rope = _rope_tables()
    sink_tab = jnp.broadcast_to((attn_sink[0].astype(f32) * LOG2_E)[:, None], (N_Q_HEADS, LANES))

    x2 = x.reshape(BATCH * SEQ, D_MODEL)
    q, kv, sz, ut = _inproj(x2, w_in[0], rope)
    ya = _attention(q.reshape(BATCH, SEQ, Q_COLS), kv.reshape(BATCH, SEQ, 2 * KV_COLS), sink_tab)
    gt = _ssm(ut, m_intra, w_state, w_so, coef)

    gains = jnp.concatenate([norm_attn_g[0], norm_ssm_g[0]]).astype(f32)
    wo = (w_out[0].astype(f32) * (gains / DEEPNORM_ALPHA)[:, None]).astype(bf16)
    row = lambda v: v[0].astype(f32)[None, :]
    out = _final(x2, ya.reshape(BATCH * SEQ, D_ATTN), sz, gt, w_glu[0].astype(bf16), row(b_glu),
                 wo, row(ln_g), row(ln_b))
    return out.reshape(BATCH, SEQ, D_MODEL)
```

```python
import jax
import jax.numpy as jnp
import numpy as np
from jax import lax
from jax.experimental import pallas as pl
from jax.experimental.pallas import tpu as pltpu

D_MODEL = 1024
BATCH = 4
SEQ = 4096
D_ATTN = 512
D_SSM = 512
HEAD_DIM = 64
N_Q_HEADS = 8
N_KV_HEADS = 2
WINDOW = 128
BLOCK = 128
ROPE_THETA = 10000.0
SSM_CH = 16
N_GROUPS = 32
SSM_STATE = 64
NORM_EPS = 1e-5
NEG_INF = -1e30
DEEPNORM_ALPHA = 2.0 ** 0.25
LOG2_E = 1.4426950408889634
Q_SCALE = HEAD_DIM ** -0.5 * LOG2_E

Q_COLS = N_Q_HEADS * HEAD_DIM
KV_COLS = N_KV_HEADS * HEAD_DIM
CHUNK = 16
N_CHUNKS = SEQ // CHUNK
CHUNK_VEC = CHUNK * SSM_CH
STATE_VEC = 2 * SSM_STATE
LANES = 128
ROWS = BATCH * N_CHUNKS
N_ROW_BLOCKS = ROWS // LANES
SLAB_PITCH = N_CHUNKS + 8
SLAB_ROWS = BATCH * SLAB_PITCH + 8
TOK_BLOCK = 1024
REGROUP_TOK = LANES * CHUNK
STEPS_PER_REGROUP = REGROUP_TOK // TOK_BLOCK
SUB_ROWS = 512
N_LANE_BLOCKS = D_SSM // LANES
SLAB_CHUNK_PITCH = 20
GROUPS_PER_LANE_BLOCK = LANES // SSM_CH
VMEM_LIMIT = 56 * 1024 * 1024
HIGHEST = lax.Precision.HIGHEST


LAG_ROWS = 512
PREP_GB = 4


def _prep_kernel(ldt_ref, are_ref, aim_ref, bre_ref, bim_ref, cre_ref, cim_ref, d_ref,
                 m_ref, ws_ref, wo_ref, coef_ref):
    f32 = jnp.float32
    lo16 = lax.broadcasted_iota(jnp.int32, (SSM_CH, LANES), 1) < SSM_STATE
    lo1 = lax.broadcasted_iota(jnp.int32, (1, LANES), 1) < SSM_STATE
    units = [(gi, d) for gi in range(PREP_GB) for d in range(2)]

    def group(gi):
        return pl.program_id(0) * PREP_GB + gi

    def dup(v):
        return jnp.concatenate([v, v], axis=1)

    lam, zoh = {}, {}
    for gi, d in units:
        ar = dup(are_ref[d, pl.ds(group(gi), 1), :])
        ai = dup(aim_ref[d, pl.ds(group(gi), 1), :])
        dt = jnp.exp(jnp.full((1, LANES), ldt_ref[d, group(gi)], f32))
        zr = dt * ar
        zi = dt * ai
        mag = jnp.exp(zr)
        lr = mag * jnp.cos(zi)
        li = mag * jnp.sin(zi)
        den = ar * ar + ai * ai
        lam[gi, d] = (lr, li)
        zoh[gi, d] = (((lr - 1.0) * ar + li * ai) / den, (li * ar - (lr - 1.0) * ai) / den)

    pr = {u: [jnp.ones((1, LANES), f32)] for u in units}
    pi = {u: [jnp.zeros((1, LANES), f32)] for u in units}
    for _ in range(CHUNK):
        for u in units:
            lr, li = lam[u]
            pr[u].append(pr[u][-1] * lr - pi[u][-1] * li)
            pi[u].append(pr[u][-2] * li + pi[u][-1] * lr)

    lag_tabs, kws = {}, {}
    for gi in range(PREP_GB):
        ws_parts, coef_rows = [], []
        for d in range(2):
            u = (gi, d)
            fr, fi = zoh[u]
            btr = dup(jnp.transpose(bre_ref[d, gi]))
            bti = dup(jnp.transpose(bim_ref[d, gi]))
            bbr = fr * btr - fi * bti
            bbi = fr * bti + fi * btr
            cr = dup(cre_ref[d, gi])
            ci = dup(cim_ref[d, gi])
            pa = [jnp.where(lo1, r, i) for r, i in zip(pr[u], pi[u])]
            pb = [jnp.where(lo1, -i, r) for r, i in zip(pr[u], pi[u])]
            pc = [jnp.where(lo1, i, r) for r, i in zip(pr[u], pi[u])]
            pd = [jnp.where(lo1, r, -i) for r, i in zip(pr[u], pi[u])]

            order = range(CHUNK + 1) if d == 0 else range(CHUNK, -1, -1)
            blocks = [cr * pa[t] + ci * pb[t] for t in order]
            blocks.append(jnp.zeros((LAG_ROWS - (CHUNK + 1) * SSM_CH, LANES), f32))
            lag_tabs[u] = jnp.concatenate(blocks, axis=0)
            lhs = jnp.where(lo16, bbr, -bbi)
            kws[u] = lax.dot_general(lhs, lag_tabs[u], (((1,), (1,)), ((), ())),
                                     precision=HIGHEST, preferred_element_type=f32)

            parts = []
            for j in range(CHUNK):
                t = CHUNK - 1 - j if d == 0 else j
                parts.append(jnp.concatenate([bbr * pa[t] + bbi * pb[t],
                                              bbr * pc[t] + bbi * pd[t]], axis=1))
            ws_parts.append(jnp.concatenate(parts, axis=0))

            a2 = jnp.where(lo1, -pi[u][CHUNK], pi[u][CHUNK])
            coef_rows += [pr[u][CHUNK], a2, -a2]

        ws_ref[gi] = jnp.concatenate(ws_parts, axis=1).astype(ws_ref.dtype)
        coef_ref[gi] = jnp.concatenate(coef_rows + [jnp.zeros((2, LANES), f32)], axis=0)

    sign = jnp.where(lax.broadcasted_iota(jnp.int32, (LANES, CHUNK_VEC), 0) < SSM_STATE, 1.0, -1.0)
    sub = lax.broadcasted_iota(jnp.int32, (SSM_CH, CHUNK_VEC), 0)
    lane = lax.broadcasted_iota(jnp.int32, (SSM_CH, CHUNK_VEC), 1)
    for gi in range(PREP_GB):
        wo_f = jnp.transpose(lag_tabs[gi, 0][SSM_CH:SSM_CH + CHUNK_VEC, :]) * sign
        wo_b = jnp.transpose(lag_tabs[gi, 1][:CHUNK_VEC, :]) * sign
        wo_ref[gi] = jnp.concatenate([wo_f, wo_b], axis=0).astype(wo_ref.dtype)

        g = group(gi)
        dblk = d_ref[:, pl.ds(pl.multiple_of((g // GROUPS_PER_LANE_BLOCK) * LANES, LANES), LANES)]
        dblk = pltpu.roll(dblk, (LANES - (g % GROUPS_PER_LANE_BLOCK) * SSM_CH) % LANES, 1)
        dblk = jnp.where(lax.broadcasted_iota(jnp.int32, (1, LANES), 1) < SSM_CH, dblk, 0.0)
        for shift in (SSM_CH, 2 * SSM_CH, 4 * SSM_CH):
            dblk = dblk + pltpu.roll(dblk, shift, 1)
        dtile = jnp.broadcast_to(jnp.concatenate([dblk, dblk], axis=1), (SSM_CH, CHUNK_VEC))
        for jp in range(CHUNK):
            fwd = kws[gi, 0] if jp == 0 else pltpu.roll(kws[gi, 0], SSM_CH * jp, 1)
            bwd = pltpu.roll(kws[gi, 1], (LAG_ROWS - SSM_CH * (CHUNK - jp)) % LAG_ROWS, 1)
            skip = jnp.where((lane // SSM_CH == jp) & (lane % SSM_CH == sub), dtile, 0.0)
            blk = fwd[:, :CHUNK_VEC] + bwd[:, :CHUNK_VEC] + skip
            m_ref[gi, jp * SSM_CH:(jp + 1) * SSM_CH, :] = blk.astype(m_ref.dtype)


def _ssm_tables(a_re, a_im, log_dt, b_re, b_im, c_re, c_im, d):
    f32 = jnp.float32
    bf16 = jnp.bfloat16
    whole = lambda a: pl.BlockSpec(a.shape, lambda g: (0,) * a.ndim)
    per_group = lambda a: pl.BlockSpec((2, PREP_GB) + a.shape[2:], lambda g: (0, g, 0, 0))
    return pl.pallas_call(
        _prep_kernel,
        grid=(N_GROUPS // PREP_GB,),
        in_specs=[pl.BlockSpec(memory_space=pltpu.SMEM), whole(a_re), whole(a_im),
                  per_group(b_re), per_group(b_im), per_group(c_re), per_group(c_im),
                  pl.BlockSpec((1, D_SSM), lambda g: (0, 0))],
        out_specs=[pl.BlockSpec((PREP_GB, CHUNK_VEC, CHUNK_VEC), lambda g: (g, 0, 0)),
                   pl.BlockSpec((PREP_GB, CHUNK_VEC, 4 * LANES), lambda g: (g, 0, 0)),
                   pl.BlockSpec((PREP_GB, 2 * STATE_VEC, CHUNK_VEC), lambda g: (g, 0, 0)),
                   pl.BlockSpec((PREP_GB, 8, LANES), lambda g: (g, 0, 0))],
        out_shape=[jax.ShapeDtypeStruct((N_GROUPS, CHUNK_VEC, CHUNK_VEC), bf16),
                   jax.ShapeDtypeStruct((N_GROUPS, CHUNK_VEC, 4 * LANES), bf16),
                   jax.ShapeDtypeStruct((N_GROUPS, 2 * STATE_VEC, CHUNK_VEC), bf16),
                   jax.ShapeDtypeStruct((N_GROUPS, 8, LANES), f32)],
        compiler_params=pltpu.CompilerParams(dimension_semantics=("arbitrary",),
                                             vmem_limit_bytes=VMEM_LIMIT),
        name="ssm_tables",
    )(log_dt.astype(f32), a_re.astype(f32), a_im.astype(f32), b_re.astype(f32), b_im.astype(f32),
      c_re.astype(f32), c_im.astype(f32), d.astype(f32).reshape(1, D_SSM))


def _rope_tables():
    half = HEAD_DIM // 2
    inv_freq = ROPE_THETA ** (-np.arange(half, dtype=np.float64) / half)
    ang = np.arange(SEQ, dtype=np.float64)[:, None] * inv_freq[None, :]
    cos, sin = np.cos(ang), np.sin(ang)
    return jnp.asarray(np.concatenate([cos, cos, cos, cos, -sin, sin, -sin, sin], axis=1), jnp.float32)


def _rope_block(xb, cos_t, sin_t, first_half):
    swapped = jnp.where(first_half, pltpu.roll(xb, 96, 1), pltpu.roll(xb, 32, 1))
    return xb * cos_t + swapped * sin_t


COL_Q = (0, Q_COLS)
COL_KV = (COL_Q[1], COL_Q[1] + 2 * KV_COLS)
COL_ZA = (COL_KV[1], COL_KV[1] + D_ATTN)
COL_U = (COL_ZA[1], COL_ZA[1] + D_SSM)
COL_ZS = (COL_U[1], COL_U[1] + D_SSM)


def _silu(z):
    h = 0.5 * z
    return h + h * jnp.tanh(h)


def _inproj_kernel(x_ref, w32_ref, rope_ref, q_ref, kv_ref, sz_ref, ut_ref, uslab_ref, w_ref):
    step = pl.program_id(0)
    part = step % STEPS_PER_REGROUP
    lane = lax.broadcasted_iota(jnp.int32, (SUB_ROWS, LANES), 1)
    first_half = (lane % HEAD_DIM) < (HEAD_DIM // 2)

    @pl.when(step == 0)
    def _():
        for c0 in range(0, w_ref.shape[1], 2 * LANES):
            w_ref[:, c0:c0 + 2 * LANES] = w32_ref[:, c0:c0 + 2 * LANES].astype(w_ref.dtype)

    def proj(xb, cols):
        return jnp.dot(xb, w_ref[:, cols[0]:cols[1]], preferred_element_type=jnp.float32)

    for sub in range(TOK_BLOCK // SUB_ROWS):
        rows = slice(sub * SUB_ROWS, (sub + 1) * SUB_ROWS)
        xb = x_ref[rows, :].astype(jnp.bfloat16)
        cos_t = rope_ref[rows, :LANES]
        sin_t = rope_ref[rows, LANES:]
        q = proj(xb, COL_Q)
        kv = proj(xb, COL_KV)
        q_blocks = [_rope_block(q[:, v * LANES:(v + 1) * LANES], cos_t, sin_t, first_half)
                    for v in range(Q_COLS // LANES)]
        q_ref[rows, :] = (jnp.concatenate(q_blocks, axis=1) * Q_SCALE).astype(q_ref.dtype)
        za = proj(xb, COL_ZA)
        k_rot = _rope_block(kv[:, :LANES], cos_t, sin_t, first_half)
        kv_ref[rows, :] = jnp.concatenate([k_rot, kv[:, LANES:]], axis=1).astype(kv_ref.dtype)
        u = proj(xb, COL_U)
        sz_ref[rows, :D_ATTN] = _silu(za).astype(sz_ref.dtype)
        zs = proj(xb, COL_ZS)
        for kk in range(SUB_ROWS // CHUNK):
            base = pl.multiple_of((part * (TOK_BLOCK // CHUNK) + sub * (SUB_ROWS // CHUNK)) * SLAB_CHUNK_PITCH, 8)
            for v in range(N_LANE_BLOCKS):
                uslab_ref[v, pl.ds(base + kk * SLAB_CHUNK_PITCH, CHUNK), :] = (
                    u[kk * CHUNK:(kk + 1) * CHUNK, v * LANES:(v + 1) * LANES])
        sz_ref[rows, D_ATTN:] = _silu(zs).astype(sz_ref.dtype)

    @pl.when(part == STEPS_PER_REGROUP - 1)
    def _():
        for j in range(CHUNK):
            for v in range(N_LANE_BLOCKS):
                uj = uslab_ref[v, pl.ds(j, LANES, stride=SLAB_CHUNK_PITCH), :]
                ujt = jnp.transpose(uj.astype(ut_ref.dtype)).reshape(GROUPS_PER_LANE_BLOCK, SSM_CH, LANES)
                g0 = v * GROUPS_PER_LANE_BLOCK
                ut_ref[g0:g0 + GROUPS_PER_LANE_BLOCK, j * SSM_CH:(j + 1) * SSM_CH, :] = ujt


def _inproj(x2, w, rope):
    n_steps = BATCH * SEQ // TOK_BLOCK
    blocks_per_seq = SEQ // TOK_BLOCK

    def tok(cols):
        return pl.BlockSpec((TOK_BLOCK, cols), lambda s: (s, 0))

    return pl.pallas_call(
        _inproj_kernel,
        grid=(n_steps,),
        in_specs=[tok(D_MODEL), pl.BlockSpec(w.shape, lambda s: (0, 0), pipeline_mode=pl.Buffered(1)),
                  pl.BlockSpec((TOK_BLOCK, 2 * LANES), lambda s: (s % blocks_per_seq, 0))],
        out_specs=[tok(Q_COLS), tok(2 * KV_COLS), tok(D_ATTN + D_SSM),
                   pl.BlockSpec((None, N_GROUPS, CHUNK_VEC, LANES), lambda s: (s // STEPS_PER_REGROUP, 0, 0, 0))],
        out_shape=[jax.ShapeDtypeStruct((BATCH * SEQ, Q_COLS), jnp.bfloat16),
                   jax.ShapeDtypeStruct((BATCH * SEQ, 2 * KV_COLS), jnp.bfloat16),
                   jax.ShapeDtypeStruct((BATCH * SEQ, D_ATTN + D_SSM), jnp.bfloat16),
                   jax.ShapeDtypeStruct((N_ROW_BLOCKS, N_GROUPS, CHUNK_VEC, LANES), jnp.bfloat16)],
        scratch_shapes=[pltpu.VMEM((N_LANE_BLOCKS, LANES * SLAB_CHUNK_PITCH, LANES), jnp.float32),
                        pltpu.VMEM(w.shape, jnp.bfloat16)],
        compiler_params=pltpu.CompilerParams(dimension_semantics=("arbitrary",),
                                             vmem_limit_bytes=VMEM_LIMIT),
        name="inproj",
    )(x2, w, rope)


ATTN_SUB = 16
ATTN_STEP = ATTN_SUB * BLOCK
N_KEYS = 3 * BLOCK


def _attn_kernel(q_ref, kp_ref, kc_ref, kn_ref, sink_ref, y_ref):
    i = pl.program_id(1)
    kvw = jnp.concatenate([kp_ref[...], kc_ref[...], kn_ref[...]], axis=0).astype(jnp.float32)

    n_win = kvw.shape[0]
    lane_w = lax.broadcasted_iota(jnp.int32, (n_win, LANES), 1)
    k_lo, k_hi, vt_lo, vt_hi = [], [], [], []
    for hk in range(N_KV_HEADS):
        in_head = (lane_w >= hk * HEAD_DIM) & (lane_w < (hk + 1) * HEAD_DIM)
        k_nat = jnp.where(in_head, kvw[:, :LANES], 0.0)
        v_nat = jnp.where(in_head, kvw[:, LANES:], 0.0)
        k_oth = pltpu.roll(k_nat, HEAD_DIM, 1)
        v_oth = pltpu.roll(v_nat, HEAD_DIM, 1)
        pairs = ((k_nat, k_oth), (v_nat, v_oth)) if hk == 0 else ((k_oth, k_nat), (v_oth, v_nat))
        k_lo.append(pairs[0][0].astype(jnp.bfloat16))
        k_hi.append(pairs[0][1].astype(jnp.bfloat16))
        vt_lo.append(jnp.transpose(pairs[1][0]).astype(jnp.bfloat16))
        vt_hi.append(jnp.transpose(pairs[1][1]).astype(jnp.bfloat16))

    c = lax.broadcasted_iota(jnp.int32, (N_KEYS, BLOCK), 0)
    r = lax.broadcasted_iota(jnp.int32, (N_KEYS, BLOCK), 1)
    band = (c >= r) & (c - r <= 2 * WINDOW)
    ones_row = lax.broadcasted_iota(jnp.int32, (16, 2 * N_KEYS), 0)
    ones_col = lax.broadcasted_iota(jnp.int32, (16, 2 * N_KEYS), 1)
    ones = jnp.where(((ones_row == 0) & (ones_col < N_KEYS)) | ((ones_row == 1) & (ones_col >= N_KEYS)),
                     1.0, 0.0).astype(jnp.bfloat16)
    row_o = lax.broadcasted_iota(jnp.int32, (LANES, 2 * BLOCK), 0)
    n_blocks = SEQ // BLOCK

    def scores(t, hk):
        blk = i * ATTN_SUB + t
        lo = jnp.where(blk == 0, BLOCK, 0)
        hi = jnp.where(blk == n_blocks - 1, 2 * BLOCK, N_KEYS)
        valid1 = band & (c >= lo) & (c < hi)
        valid = jnp.concatenate([valid1, valid1], axis=1)
        win = slice(t * BLOCK, t * BLOCK + N_KEYS)
        q = q_ref[t * BLOCK:(t + 1) * BLOCK, (2 * hk) * LANES:(2 * hk + 2) * LANES]
        qq = jnp.concatenate([q[:, :LANES], q[:, LANES:]], axis=0)
        kcat = jnp.concatenate([k_lo[hk][win], k_hi[hk][win]], axis=0)
        st = lax.dot_general(kcat, qq, (((1,), (1,)), ((), ())),
                             preferred_element_type=jnp.float32)
        sps, ms, sinks = [], [], []
        for par in range(2):
            base = par * N_KEYS
            sp = jnp.concatenate(
                [jnp.where(valid[:BLOCK], st[base:base + BLOCK, :], NEG_INF),
                 st[base + BLOCK:base + 2 * BLOCK, :],
                 jnp.where(valid[2 * BLOCK:], st[base + 2 * BLOCK:base + N_KEYS, :], NEG_INF)], axis=0)
            h0 = 4 * hk + par
            h1 = h0 + 2
            sink = jnp.concatenate([sink_ref[h0:h0 + 1, :], sink_ref[h1:h1 + 1, :]], axis=1)
            sps.append(sp)
            ms.append(jnp.maximum(jnp.max(sp, axis=0, keepdims=True), sink))
            sinks.append(sink)
        return sps, ms, sinks

    def weights(sps, ms, sinks):
        return [jnp.exp2((sp - m).astype(jnp.bfloat16)) for sp, m in zip(sps, ms)], ms, sinks

    def finish(t, hk, ps, ms, sinks):
        win = slice(t * BLOCK, t * BLOCK + N_KEYS)
        vt = jnp.concatenate([jnp.concatenate([vt_lo[hk][:, win], vt_hi[hk][:, win]], axis=1), ones], axis=0)
        ot = jnp.dot(vt, jnp.concatenate(ps, axis=0), preferred_element_type=jnp.float32)
        inv0 = 1.0 / (ot[LANES:LANES + 1, :] + jnp.exp2(sinks[0] - ms[0]))
        inv1 = 1.0 / (ot[LANES + 1:LANES + 2, :] + jnp.exp2(sinks[1] - ms[1]))
        o = jnp.transpose(ot[:LANES, :] * jnp.where(row_o < HEAD_DIM, inv0, inv1))
        rows = slice(t * BLOCK, (t + 1) * BLOCK)
        y_ref[rows, (2 * hk) * LANES:(2 * hk + 1) * LANES] = o[:BLOCK].astype(y_ref.dtype)
        y_ref[rows, (2 * hk + 1) * LANES:(2 * hk + 2) * LANES] = o[BLOCK:].astype(y_ref.dtype)

    work = [(t, hk) for t in range(ATTN_SUB) for hk in range(N_KV_HEADS)]
    stage_a = {0: scores(*work[0])}
    stage_b = {}
    for n in range(len(work) + 1):
        if n + 1 < len(work):
            stage_a[n + 1] = scores(*work[n + 1])
        if n < len(work):
            stage_b[n] = weights(*stage_a.pop(n))
        if n >= 1:
            finish(*work[n - 1], *stage_b.pop(n - 1))


def _attention(q, kv, sink_tab):
    nb = SEQ // BLOCK
    return pl.pallas_call(
        _attn_kernel,
        grid=(BATCH, SEQ // ATTN_STEP),
        in_specs=[pl.BlockSpec((None, ATTN_STEP, Q_COLS), lambda b, i: (b, i, 0)),
                  pl.BlockSpec((None, BLOCK, 2 * KV_COLS),
                               lambda b, i: (b, jnp.maximum(i * ATTN_SUB - 1, 0), 0)),
                  pl.BlockSpec((None, ATTN_STEP, 2 * KV_COLS), lambda b, i: (b, i, 0)),
                  pl.BlockSpec((None, BLOCK, 2 * KV_COLS),
                               lambda b, i: (b, jnp.minimum((i + 1) * ATTN_SUB, nb - 1), 0)),
                  pl.BlockSpec((N_Q_HEADS, LANES), lambda b, i: (0, 0))],
        out_specs=pl.BlockSpec((None, ATTN_STEP, D_ATTN), lambda b, i: (b, i, 0)),
        out_shape=jax.ShapeDtypeStruct((BATCH, SEQ, D_ATTN), jnp.bfloat16),
        compiler_params=pltpu.CompilerParams(dimension_semantics=("arbitrary", "arbitrary"),
                                             vmem_limit_bytes=VMEM_LIMIT),
        name="attention",
    )(q, kv, kv, kv, sink_tab)


SSM_GB = 4


def _ssm_kernel(ut_ref, m_ref, ws_ref, wo_ref, coef_ref, gt_ref,
                sf_ref, sfs_ref, sb_ref, sbs_ref, xf_ref, xb_ref, u_ref):
    for gi in range(SSM_GB):
        ut = jnp.concatenate([ut_ref[lb, gi] for lb in range(N_ROW_BLOCKS)], axis=1)
        u = jnp.transpose(ut)
        u_ref[gi] = u
        s = jnp.dot(u, ws_ref[gi], preferred_element_type=jnp.float32)
        for b in range(BATCH):
            rows = slice(b * N_CHUNKS, (b + 1) * N_CHUNKS)
            dst = pl.ds(b * SLAB_PITCH + 8, N_CHUNKS)
            sf_ref[gi, dst, :] = s[rows, 0 * LANES:1 * LANES]
            sfs_ref[gi, dst, :] = s[rows, 1 * LANES:2 * LANES]
            sb_ref[gi, dst, :] = s[rows, 2 * LANES:3 * LANES]
            sbs_ref[gi, dst, :] = s[rows, 3 * LANES:4 * LANES]

    zero = jnp.zeros((BATCH, LANES), jnp.float32)
    coefs = []
    for gi in range(SSM_GB):
        coefs.append([jnp.broadcast_to(coef_ref[gi, r:r + 1, :], (BATCH, LANES)) for r in range(6)])
        xf_ref[gi, pl.ds(8, BATCH, stride=SLAB_PITCH), :] = zero
        xb_ref[gi, pl.ds(8 + N_CHUNKS - 1, BATCH, stride=SLAB_PITCH), :] = zero

    def rows_at(r):
        return pl.ds(r, BATCH, stride=SLAB_PITCH)

    def step(k, carry):
        kb = N_CHUNKS - 1 - k
        out = []
        for gi in range(SSM_GB):
            a1f, a2f, a3f, a1b, a2b, a3b = coefs[gi]
            xf, xfs, xb, xbs = carry[4 * gi:4 * gi + 4]
            nxf = a1f * xf + a2f * xfs + sf_ref[gi, rows_at(8 + k), :]
            nxfs = a1f * xfs + a3f * xf + sfs_ref[gi, rows_at(8 + k), :]
            xf_ref[gi, rows_at(9 + k), :] = nxf
            nxb = a1b * xb + a2b * xbs + sb_ref[gi, rows_at(8 + kb), :]
            nxbs = a1b * xbs + a3b * xb + sbs_ref[gi, rows_at(8 + kb), :]
            xb_ref[gi, rows_at(7 + kb), :] = nxb
            out += [nxf, nxfs, nxb, nxbs]
        return tuple(out)

    lax.fori_loop(0, N_CHUNKS, step, (zero,) * (4 * SSM_GB), unroll=2)

    def chunk_out(gi):
        xin_f = jnp.concatenate([xf_ref[gi, pl.ds(b * SLAB_PITCH + 8, N_CHUNKS), :] for b in range(BATCH)], axis=0)
        xin_b = jnp.concatenate([xb_ref[gi, pl.ds(b * SLAB_PITCH + 8, N_CHUNKS), :] for b in range(BATCH)], axis=0)
        lhs = jnp.concatenate([u_ref[gi], xin_f.astype(jnp.bfloat16), xin_b.astype(jnp.bfloat16)], axis=1)
        rhs = jnp.concatenate([m_ref[gi], wo_ref[gi]], axis=0)
        return jnp.dot(lhs, rhs, preferred_element_type=jnp.float32)

    y = chunk_out(0)
    for gi in range(SSM_GB):
        y_next = chunk_out(gi + 1) if gi + 1 < SSM_GB else None
        gt = jnp.transpose(jax.nn.gelu(y).astype(gt_ref.dtype))
        for lb in range(N_ROW_BLOCKS):
            gt_ref[lb, gi] = gt[:, lb * LANES:(lb + 1) * LANES]
        y = y_next


def _ssm(ut, m_intra, w_state, w_out, coef):
    def per_group(shape):
        return pl.BlockSpec((SSM_GB,) + shape, lambda g: (g, 0, 0))

    lane_blocked = pl.BlockSpec((N_ROW_BLOCKS, SSM_GB, CHUNK_VEC, LANES), lambda g: (0, g, 0, 0))
    slab = pltpu.VMEM((SSM_GB, SLAB_ROWS, LANES), jnp.float32)
    return pl.pallas_call(
        _ssm_kernel,
        grid=(N_GROUPS // SSM_GB,),
        in_specs=[lane_blocked, per_group((CHUNK_VEC, CHUNK_VEC)),
                  per_group((CHUNK_VEC, 4 * LANES)), per_group((2 * STATE_VEC, CHUNK_VEC)),
                  per_group((8, LANES))],
        out_specs=lane_blocked,
        out_shape=jax.ShapeDtypeStruct((N_ROW_BLOCKS, N_GROUPS, CHUNK_VEC, LANES), jnp.bfloat16),
        scratch_shapes=[slab] * 6 + [pltpu.VMEM((SSM_GB, ROWS, CHUNK_VEC), jnp.bfloat16)],
        compiler_params=pltpu.CompilerParams(dimension_semantics=("arbitrary",),
                                             vmem_limit_bytes=VMEM_LIMIT),
        name="ssm",
    )(ut, m_intra, w_state, w_out, coef)


FINAL_SUB_ROWS = 256


def _unit_rms(v):
    return v * lax.rsqrt(jnp.mean(v * v, axis=-1, keepdims=True) + NORM_EPS)


def _final_kernel(x_ref, ya_ref, sz_ref, gt_ref, wglu_ref, bglu_ref,
                  wo_ref, lng_ref, lnb_ref, o_ref, gslab_ref):
    step = pl.program_id(0)
    part = step % STEPS_PER_REGROUP

    @pl.when(part == 0)
    def _():
        for j in range(CHUNK):
            for v in range(N_LANE_BLOCKS):
                g0 = v * GROUPS_PER_LANE_BLOCK
                gjt = gt_ref[g0:g0 + GROUPS_PER_LANE_BLOCK, j * SSM_CH:(j + 1) * SSM_CH, :]
                gj = jnp.transpose(gjt.reshape(LANES, LANES))
                gslab_ref[v, pl.ds(j, LANES, stride=SLAB_CHUNK_PITCH), :] = gj.astype(jnp.float32)

    def rows_of(sub):
        return slice(sub * FINAL_SUB_ROWS, (sub + 1) * FINAL_SUB_ROWS)

    def gate_in(sub):
        rows = rows_of(sub)
        ya = ya_ref[rows, :].astype(jnp.float32) * sz_ref[rows, :D_ATTN].astype(jnp.float32)
        ya = _unit_rms(ya).astype(jnp.bfloat16)
        base = pl.multiple_of((part * (TOK_BLOCK // CHUNK) + sub * (FINAL_SUB_ROWS // CHUNK)) * SLAB_CHUNK_PITCH, 8)
        g = jnp.concatenate(
            [jnp.concatenate([gslab_ref[v, pl.ds(base + kk * SLAB_CHUNK_PITCH, CHUNK), :]
                              for kk in range(FINAL_SUB_ROWS // CHUNK)], axis=0)
             for v in range(N_LANE_BLOCKS)], axis=1)
        return ya, g

    def glu(sub, ya, g):
        rows = rows_of(sub)
        gate = jnp.dot(g.astype(jnp.bfloat16), wglu_ref[...], preferred_element_type=jnp.float32) + bglu_ref[...]
        ys = g * jax.nn.sigmoid(gate)
        ys = ys * sz_ref[rows, D_ATTN:].astype(jnp.float32)
        return ya, _unit_rms(ys).astype(jnp.bfloat16)

    def project(sub, ya, ys):
        return (jnp.dot(ya, wo_ref[:D_ATTN, :], preferred_element_type=jnp.float32)
                + jnp.dot(ys, wo_ref[D_ATTN:, :], preferred_element_type=jnp.float32))

    def layer_norm(sub, out):
        rows = rows_of(sub)
        h = x_ref[rows, :] + out
        mu = jnp.mean(h, axis=-1, keepdims=True)
        hc = h - mu
        var = jnp.mean(hc * hc, axis=-1, keepdims=True)
        o_ref[rows, :] = hc * lax.rsqrt(var + NORM_EPS / DEEPNORM_ALPHA ** 2) * lng_ref[...] + lnb_ref[...]

    n_sub = TOK_BLOCK // FINAL_SUB_ROWS
    stage_a, stage_b = {}, {}
    for n in range(n_sub + 2):
        if n < n_sub:
            stage_a[n] = gate_in(n)
        if 1 <= n <= n_sub:
            stage_b[n - 1] = glu(n - 1, *stage_a.pop(n - 1))
        if n >= 2:
            layer_norm(n - 2, project(n - 2, *stage_b.pop(n - 2)))


def _final(x2, ya, sz, gt, wglu, bglu, wo, lng, lnb):
    n_steps = BATCH * SEQ // TOK_BLOCK

    def tok(cols):
        return pl.BlockSpec((TOK_BLOCK, cols), lambda s: (s, 0))

    def whole(shape):
        return pl.BlockSpec(shape, lambda s: (0,) * len(shape))

    return pl.pallas_call(
        _final_kernel,
        grid=(n_steps,),
        in_specs=[tok(D_MODEL), tok(D_ATTN), tok(D_ATTN + D_SSM),
                  pl.BlockSpec((None, N_GROUPS, CHUNK_VEC, LANES), lambda s: (s // STEPS_PER_REGROUP, 0, 0, 0)),
                  whole(wglu.shape), whole(bglu.shape),
                  whole(wo.shape), whole(lng.shape), whole(lnb.shape)],
        out_specs=tok(D_MODEL),
        out_shape=jax.ShapeDtypeStruct((BATCH * SEQ, D_MODEL), jnp.float32),
        scratch_shapes=[pltpu.VMEM((N_LANE_BLOCKS, LANES * SLAB_CHUNK_PITCH, LANES), jnp.float32)],
        compiler_params=pltpu.CompilerParams(dimension_semantics=("arbitrary",),
                                             vmem_limit_bytes=VMEM_LIMIT),
        name="final",
    )(x2, ya, sz, gt, wglu, bglu, wo, lng, lnb)


def kernel(x, w_in, attn_sink, ssm_a_re, ssm_a_im, ssm_log_dt, ssm_b_re, ssm_b_im, ssm_c_re, ssm_c_im,
           ssm_d, w_glu, b_glu, norm_attn_g, norm_ssm_g, w_out, ln_g, ln_b):
    assert x.shape == (BATCH, SEQ, D_MODEL) and w_in.shape[0] == 1
    bf16 = jnp.bfloat16
    f32 = jnp.float32
    m_intra, w_state, w_so, coef = _ssm_tables(ssm_a_re[0], ssm_a_im[0], ssm_log_dt[0], ssm_b_re[0],
                                                ssm_b_im[0], ssm_c_re[0], ssm_c_im[0], ssm_d[0])
    rope = _rope_tables()
    sink_tab = jnp.broadcast_to((attn_sink[0].astype(f32) * LOG2_E)[:, None], (N_Q_HEADS, LANES))

    x2 = x.reshape(BATCH * SEQ, D_MODEL)
    q, kv, sz, ut = _inproj(x2, w_in[0], rope)
    ya = _attention(q.reshape(BATCH, SEQ, Q_COLS), kv.reshape(BATCH, SEQ, 2 * KV_COLS), sink_tab)
    gt = _ssm(ut, m_intra, w_state, w_so, coef)

    gains = jnp.concatenate([norm_attn_g[0], norm_ssm_g[0]]).astype(f32)
    wo = (w_out[0].astype(f32) * (gains / DEEPNORM_ALPHA)[:, None]).astype(bf16)
    row = lambda v: v[0].astype(f32)[None, :]
    out = _final(x2, ya.reshape(BATCH * SEQ, D_ATTN), sz, gt, w_glu[0].astype(bf16), row(b_glu),
                 wo, row(ln_g), row(ln_b))
    return out.reshape(BATCH, SEQ, D_MODEL)
```

```python
import jax
import jax.numpy as jnp
import numpy as np
from jax import lax
from jax.experimental import pallas as pl
from jax.experimental.pallas import tpu as pltpu

D_MODEL = 1024
BATCH = 4
SEQ = 4096
D_ATTN = 512
D_SSM = 512
HEAD_DIM = 64
N_Q_HEADS = 8
N_KV_HEADS = 2
WINDOW = 128
BLOCK = 128
ROPE_THETA = 10000.0
SSM_CH = 16
N_GROUPS = 32
SSM_STATE = 64
NORM_EPS = 1e-5
NEG_INF = -1e30
DEEPNORM_ALPHA = 2.0 ** 0.25
LOG2_E = 1.4426950408889634
Q_SCALE = HEAD_DIM ** -0.5 * LOG2_E

Q_COLS = N_Q_HEADS * HEAD_DIM
KV_COLS = N_KV_HEADS * HEAD_DIM
CHUNK = 16
N_CHUNKS = SEQ // CHUNK
CHUNK_VEC = CHUNK * SSM_CH
STATE_VEC = 2 * SSM_STATE
LANES = 128
ROWS = BATCH * N_CHUNKS
N_ROW_BLOCKS = ROWS // LANES
SLAB_PITCH = N_CHUNKS + 8
SLAB_ROWS = BATCH * SLAB_PITCH + 8
TOK_BLOCK = 1024
REGROUP_TOK = LANES * CHUNK
STEPS_PER_REGROUP = REGROUP_TOK // TOK_BLOCK
SUB_ROWS = 512
N_LANE_BLOCKS = D_SSM // LANES
SLAB_CHUNK_PITCH = 20
GROUPS_PER_LANE_BLOCK = LANES // SSM_CH
VMEM_LIMIT = 56 * 1024 * 1024
HIGHEST = lax.Precision.HIGHEST


LAG_ROWS = 512
PREP_GB = 4


def _prep_kernel(ldt_ref, are_ref, aim_ref, bre_ref, bim_ref, cre_ref, cim_ref, d_ref,
                 m_ref, ws_ref, wo_ref, coef_ref):
    f32 = jnp.float32
    lo16 = lax.broadcasted_iota(jnp.int32, (SSM_CH, LANES), 1) < SSM_STATE
    lo1 = lax.broadcasted_iota(jnp.int32, (1, LANES), 1) < SSM_STATE
    units = [(gi, d) for gi in range(PREP_GB) for d in range(2)]

    def group(gi):
        return pl.program_id(0) * PREP_GB + gi

    def dup(v):
        return jnp.concatenate([v, v], axis=1)

    lam, zoh = {}, {}
    for gi, d in units:
        ar = dup(are_ref[d, pl.ds(group(gi), 1), :])
        ai = dup(aim_ref[d, pl.ds(group(gi), 1), :])
        dt = jnp.exp(jnp.full((1, LANES), ldt_ref[d, group(gi)], f32))
        zr = dt * ar
        zi = dt * ai
        mag = jnp.exp(zr)
        lr = mag * jnp.cos(zi)
        li = mag * jnp.sin(zi)
        den = ar * ar + ai * ai
        lam[gi, d] = (lr, li)
        zoh[gi, d] = (((lr - 1.0) * ar + li * ai) / den, (li * ar - (lr - 1.0) * ai) / den)

    pr = {u: [jnp.ones((1, LANES), f32)] for u in units}
    pi = {u: [jnp.zeros((1, LANES), f32)] for u in units}
    for _ in range(CHUNK):
        for u in units:
            lr, li = lam[u]
            pr[u].append(pr[u][-1] * lr - pi[u][-1] * li)
            pi[u].append(pr[u][-2] * li + pi[u][-1] * lr)

    lag_tabs, kws = {}, {}
    for gi in range(PREP_GB):
        ws_parts, coef_rows = [], []
        for d in range(2):
            u = (gi, d)
            fr, fi = zoh[u]
            btr = dup(jnp.transpose(bre_ref[d, gi]))
            bti = dup(jnp.transpose(bim_ref[d, gi]))
            bbr = fr * btr - fi * bti
            bbi = fr * bti + fi * btr
            cr = dup(cre_ref[d, gi])
            ci = dup(cim_ref[d, gi])
            pa = [jnp.where(lo1, r, i) for r, i in zip(pr[u], pi[u])]
            pb = [jnp.where(lo1, -i, r) for r, i in zip(pr[u], pi[u])]
            pc = [jnp.where(lo1, i, r) for r, i in zip(pr[u], pi[u])]
            pd = [jnp.where(lo1, r, -i) for r, i in zip(pr[u], pi[u])]

            order = range(CHUNK + 1) if d == 0 else range(CHUNK, -1, -1)
            blocks = [cr * pa[t] + ci * pb[t] for t in order]
            blocks.append(jnp.zeros((LAG_ROWS - (CHUNK + 1) * SSM_CH, LANES), f32))
            lag_tabs[u] = jnp.concatenate(blocks, axis=0)
            lhs = jnp.where(lo16, bbr, -bbi)
            kws[u] = lax.dot_general(lhs, lag_tabs[u], (((1,), (1,)), ((), ())),
                                     precision=HIGHEST, preferred_element_type=f32)

            parts = []
            for j in range(CHUNK):
                t = CHUNK - 1 - j if d == 0 else j
                parts.append(jnp.concatenate([bbr * pa[t] + bbi * pb[t],
                                              bbr * pc[t] + bbi * pd[t]], axis=1))
            ws_parts.append(jnp.concatenate(parts, axis=0))

            a2 = jnp.where(lo1, -pi[u][CHUNK], pi[u][CHUNK])
            coef_rows += [pr[u][CHUNK], a2, -a2]

        ws_ref[gi] = jnp.concatenate(ws_parts, axis=1).astype(ws_ref.dtype)
        coef_ref[gi] = jnp.concatenate(coef_rows + [jnp.zeros((2, LANES), f32)], axis=0)

    sign = jnp.where(lax.broadcasted_iota(jnp.int32, (LANES, CHUNK_VEC), 0) < SSM_STATE, 1.0, -1.0)
    sub = lax.broadcasted_iota(jnp.int32, (SSM_CH, CHUNK_VEC), 0)
    lane = lax.broadcasted_iota(jnp.int32, (SSM_CH, CHUNK_VEC), 1)
    for gi in range(PREP_GB):
        wo_f = jnp.transpose(lag_tabs[gi, 0][SSM_CH:SSM_CH + CHUNK_VEC, :]) * sign
        wo_b = jnp.transpose(lag_tabs[gi, 1][:CHUNK_VEC, :]) * sign
        wo_ref[gi] = jnp.concatenate([wo_f, wo_b], axis=0).astype(wo_ref.dtype)

        g = group(gi)
        dblk = d_ref[:, pl.ds(pl.multiple_of((g // GROUPS_PER_LANE_BLOCK) * LANES, LANES), LANES)]
        dblk = pltpu.roll(dblk, (LANES - (g % GROUPS_PER_LANE_BLOCK) * SSM_CH) % LANES, 1)
        dblk = jnp.where(lax.broadcasted_iota(jnp.int32, (1, LANES), 1) < SSM_CH, dblk, 0.0)
        for shift in (SSM_CH, 2 * SSM_CH, 4 * SSM_CH):
            dblk = dblk + pltpu.roll(dblk, shift, 1)
        dtile = jnp.broadcast_to(jnp.concatenate([dblk, dblk], axis=1), (SSM_CH, CHUNK_VEC))
        for jp in range(CHUNK):
            fwd = kws[gi, 0] if jp == 0 else pltpu.roll(kws[gi, 0], SSM_CH * jp, 1)
            bwd = pltpu.roll(kws[gi, 1], (LAG_ROWS - SSM_CH * (CHUNK - jp)) % LAG_ROWS, 1)
            skip = jnp.where((lane // SSM_CH == jp) & (lane % SSM_CH == sub), dtile, 0.0)
            blk = fwd[:, :CHUNK_VEC] + bwd[:, :CHUNK_VEC] + skip
            m_ref[gi, jp * SSM_CH:(jp + 1) * SSM_CH, :] = blk.astype(m_ref.dtype)


def _ssm_tables(a_re, a_im, log_dt, b_re, b_im, c_re, c_im, d):
    f32 = jnp.float32
    bf16 = jnp.bfloat16
    whole = lambda a: pl.BlockSpec(a.shape, lambda g: (0,) * a.ndim)
    per_group = lambda a: pl.BlockSpec((2, PREP_GB) + a.shape[2:], lambda g: (0, g, 0, 0))
    return pl.pallas_call(
        _prep_kernel,
        grid=(N_GROUPS // PREP_GB,),
        in_specs=[pl.BlockSpec(memory_space=pltpu.SMEM), whole(a_re), whole(a_im),
                  per_group(b_re), per_group(b_im), per_group(c_re), per_group(c_im),
                  pl.BlockSpec((1, D_SSM), lambda g: (0, 0))],
        out_specs=[pl.BlockSpec((PREP_GB, CHUNK_VEC, CHUNK_VEC), lambda g: (g, 0, 0)),
                   pl.BlockSpec((PREP_GB, CHUNK_VEC, 4 * LANES), lambda g: (g, 0, 0)),
                   pl.BlockSpec((PREP_GB, 2 * STATE_VEC, CHUNK_VEC), lambda g: (g, 0, 0)),
                   pl.BlockSpec((PREP_GB, 8, LANES), lambda g: (g, 0, 0))],
        out_shape=[jax.ShapeDtypeStruct((N_GROUPS, CHUNK_VEC, CHUNK_VEC), bf16),
                   jax.ShapeDtypeStruct((N_GROUPS, CHUNK_VEC, 4 * LANES), bf16),
                   jax.ShapeDtypeStruct((N_GROUPS, 2 * STATE_VEC, CHUNK_VEC), bf16),
                   jax.ShapeDtypeStruct((N_GROUPS, 8, LANES), f32)],
        compiler_params=pltpu.CompilerParams(dimension_semantics=("arbitrary",),
                                             vmem_limit_bytes=VMEM_LIMIT),
        name="ssm_tables",
    )(log_dt.astype(f32), a_re.astype(f32), a_im.astype(f32), b_re.astype(f32), b_im.astype(f32),
      c_re.astype(f32), c_im.astype(f32), d.astype(f32).reshape(1, D_SSM))


def _rope_tables():
    half = HEAD_DIM // 2
    inv_freq = ROPE_THETA ** (-np.arange(half, dtype=np.float64) / half)
    ang = np.arange(SEQ, dtype=np.float64)[:, None] * inv_freq[None, :]
    cos, sin = np.cos(ang), np.sin(ang)
    return jnp.asarray(np.concatenate([cos, cos, cos, cos, -sin, sin, -sin, sin], axis=1), jnp.float32)


def _rope_block(xb, cos_t, sin_t, first_half):
    swapped = jnp.where(first_half, pltpu.roll(xb, 96, 1), pltpu.roll(xb, 32, 1))
    return xb * cos_t + swapped * sin_t


COL_Q = (0, Q_COLS)
COL_KV = (COL_Q[1], COL_Q[1] + 2 * KV_COLS)
COL_ZA = (COL_KV[1], COL_KV[1] + D_ATTN)
COL_U = (COL_ZA[1], COL_ZA[1] + D_SSM)
COL_ZS = (COL_U[1], COL_U[1] + D_SSM)


def _silu(z):
    h = 0.5 * z
    return h + h * jnp.tanh(h)


def _inproj_kernel(x_ref, w32_ref, rope_ref, q_ref, kv_ref, sz_ref, ut_ref, uslab_ref, w_ref):
    step = pl.program_id(0)
    part = step % STEPS_PER_REGROUP
    lane = lax.broadcasted_iota(jnp.int32, (SUB_ROWS, LANES), 1)
    first_half = (lane % HEAD_DIM) < (HEAD_DIM // 2)

    @pl.when(step == 0)
    def _():
        for c0 in range(0, w_ref.shape[1], 2 * LANES):
            w_ref[:, c0:c0 + 2 * LANES] = w32_ref[:, c0:c0 + 2 * LANES].astype(w_ref.dtype)

    def proj(xb, cols):
        return jnp.dot(xb, w_ref[:, cols[0]:cols[1]], preferred_element_type=jnp.float32)

    for sub in range(TOK_BLOCK // SUB_ROWS):
        rows = slice(sub * SUB_ROWS, (sub + 1) * SUB_ROWS)
        xb = x_ref[rows, :].astype(jnp.bfloat16)
        cos_t = rope_ref[rows, :LANES]
        sin_t = rope_ref[rows, LANES:]
        q = proj(xb, COL_Q)
        kv = proj(xb, COL_KV)
        q_blocks = [_rope_block(q[:, v * LANES:(v + 1) * LANES], cos_t, sin_t, first_half)
                    for v in range(Q_COLS // LANES)]
        q_ref[rows, :] = (jnp.concatenate(q_blocks, axis=1) * Q_SCALE).astype(q_ref.dtype)
        za = proj(xb, COL_ZA)
        k_rot = _rope_block(kv[:, :LANES], cos_t, sin_t, first_half)
        kv_ref[rows, :] = jnp.concatenate([k_rot, kv[:, LANES:]], axis=1).astype(kv_ref.dtype)
        u = proj(xb, COL_U)
        sz_ref[rows, :D_ATTN] = _silu(za).astype(sz_ref.dtype)
        zs = proj(xb, COL_ZS)
        for kk in range(SUB_ROWS // CHUNK):
            base = pl.multiple_of((part * (TOK_BLOCK // CHUNK) + sub * (SUB_ROWS // CHUNK)) * SLAB_CHUNK_PITCH, 8)
            for v in range(N_LANE_BLOCKS):
                uslab_ref[v, pl.ds(base + kk * SLAB_CHUNK_PITCH, CHUNK), :] = (
                    u[kk * CHUNK:(kk + 1) * CHUNK, v * LANES:(v + 1) * LANES])
        sz_ref[rows, D_ATTN:] = _silu(zs).astype(sz_ref.dtype)

    @pl.when(part == STEPS_PER_REGROUP - 1)
    def _():
        for j in range(CHUNK):
            for v in range(N_LANE_BLOCKS):
                uj = uslab_ref[v, pl.ds(j, LANES, stride=SLAB_CHUNK_PITCH), :]
                ujt = jnp.transpose(uj.astype(ut_ref.dtype)).reshape(GROUPS_PER_LANE_BLOCK, SSM_CH, LANES)
                g0 = v * GROUPS_PER_LANE_BLOCK
                ut_ref[g0:g0 + GROUPS_PER_LANE_BLOCK, j * SSM_CH:(j + 1) * SSM_CH, :] = ujt


def _inproj(x2, w, rope):
    n_steps = BATCH * SEQ // TOK_BLOCK
    blocks_per_seq = SEQ // TOK_BLOCK

    def tok(cols):
        return pl.BlockSpec((TOK_BLOCK, cols), lambda s: (s, 0))

    return pl.pallas_call(
        _inproj_kernel,
        grid=(n_steps,),
        in_specs=[tok(D_MODEL), pl.BlockSpec(w.shape, lambda s: (0, 0), pipeline_mode=pl.Buffered(1)),
                  pl.BlockSpec((TOK_BLOCK, 2 * LANES), lambda s: (s % blocks_per_seq, 0))],
        out_specs=[tok(Q_COLS), tok(2 * KV_COLS), tok(D_ATTN + D_SSM),
                   pl.BlockSpec((None, N_GROUPS, CHUNK_VEC, LANES), lambda s: (s // STEPS_PER_REGROUP, 0, 0, 0))],
        out_shape=[jax.ShapeDtypeStruct((BATCH * SEQ, Q_COLS), jnp.bfloat16),
                   jax.ShapeDtypeStruct((BATCH * SEQ, 2 * KV_COLS), jnp.bfloat16),
                   jax.ShapeDtypeStruct((BATCH * SEQ, D_ATTN + D_SSM), jnp.bfloat16),
                   jax.ShapeDtypeStruct((N_ROW_BLOCKS, N_GROUPS, CHUNK_VEC, LANES), jnp.bfloat16)],
        scratch_shapes=[pltpu.VMEM((N_LANE_BLOCKS, LANES * SLAB_CHUNK_PITCH, LANES), jnp.float32),
                        pltpu.VMEM(w.shape, jnp.bfloat16)],
        compiler_params=pltpu.CompilerParams(dimension_semantics=("arbitrary",),
                                             vmem_limit_bytes=VMEM_LIMIT),
        name="inproj",
    )(x2, w, rope)


ATTN_SUB = 32
ATTN_STEP = ATTN_SUB * BLOCK
N_KEYS = 3 * BLOCK


def _attn_kernel(q_ref, kp_ref, kc_ref, kn_ref, sink_ref, y_ref):
    i = pl.program_id(1)
    kvw = jnp.concatenate([kp_ref[...], kc_ref[...], kn_ref[...]], axis=0).astype(jnp.float32)

    n_win = kvw.shape[0]
    lane_w = lax.broadcasted_iota(jnp.int32, (n_win, LANES), 1)
    k_lo, k_hi, vt_lo, vt_hi = [], [], [], []
    for hk in range(N_KV_HEADS):
        in_head = (lane_w >= hk * HEAD_DIM) & (lane_w < (hk + 1) * HEAD_DIM)
        k_nat = jnp.where(in_head, kvw[:, :LANES], 0.0)
        v_nat = jnp.where(in_head, kvw[:, LANES:], 0.0)
        k_oth = pltpu.roll(k_nat, HEAD_DIM, 1)
        v_oth = pltpu.roll(v_nat, HEAD_DIM, 1)
        pairs = ((k_nat, k_oth), (v_nat, v_oth)) if hk == 0 else ((k_oth, k_nat), (v_oth, v_nat))
        k_lo.append(pairs[0][0].astype(jnp.bfloat16))
        k_hi.append(pairs[0][1].astype(jnp.bfloat16))
        vt_lo.append(jnp.transpose(pairs[1][0]).astype(jnp.bfloat16))
        vt_hi.append(jnp.transpose(pairs[1][1]).astype(jnp.bfloat16))

    c = lax.broadcasted_iota(jnp.int32, (N_KEYS, BLOCK), 0)
    r = lax.broadcasted_iota(jnp.int32, (N_KEYS, BLOCK), 1)
    band = (c >= r) & (c - r <= 2 * WINDOW)
    ones_row = lax.broadcasted_iota(jnp.int32, (16, 2 * N_KEYS), 0)
    ones_col = lax.broadcasted_iota(jnp.int32, (16, 2 * N_KEYS), 1)
    ones = jnp.where(((ones_row == 0) & (ones_col < N_KEYS)) | ((ones_row == 1) & (ones_col >= N_KEYS)),
                     1.0, 0.0).astype(jnp.bfloat16)
    row_o = lax.broadcasted_iota(jnp.int32, (LANES, 2 * BLOCK), 0)
    n_blocks = SEQ // BLOCK

    def scores(t, hk):
        blk = i * ATTN_SUB + t
        lo = jnp.where(blk == 0, BLOCK, 0)
        hi = jnp.where(blk == n_blocks - 1, 2 * BLOCK, N_KEYS)
        valid1 = band & (c >= lo) & (c < hi)
        valid = jnp.concatenate([valid1, valid1], axis=1)
        win = slice(t * BLOCK, t * BLOCK + N_KEYS)
        q = q_ref[t * BLOCK:(t + 1) * BLOCK, (2 * hk) * LANES:(2 * hk + 2) * LANES]
        qq = jnp.concatenate([q[:, :LANES], q[:, LANES:]], axis=0)
        kcat = jnp.concatenate([k_lo[hk][win], k_hi[hk][win]], axis=0)
        st = lax.dot_general(kcat, qq, (((1,), (1,)), ((), ())),
                             preferred_element_type=jnp.float32)
        sps, ms, sinks = [], [], []
        for par in range(2):
            base = par * N_KEYS
            sp = jnp.concatenate(
                [jnp.where(valid[:BLOCK], st[base:base + BLOCK, :], NEG_INF),
                 st[base + BLOCK:base + 2 * BLOCK, :],
                 jnp.where(valid[2 * BLOCK:], st[base + 2 * BLOCK:base + N_KEYS, :], NEG_INF)], axis=0)
            h0 = 4 * hk + par
            h1 = h0 + 2
            sink = jnp.concatenate([sink_ref[h0:h0 + 1, :], sink_ref[h1:h1 + 1, :]], axis=1)
            sps.append(sp)
            ms.append(jnp.maximum(jnp.max(sp, axis=0, keepdims=True), sink))
            sinks.append(sink)
        return sps, ms, sinks

    def weights(sps, ms, sinks):
        return [jnp.exp2((sp - m).astype(jnp.bfloat16)) for sp, m in zip(sps, ms)], ms, sinks

    def finish(t, hk, ps, ms, sinks):
        win = slice(t * BLOCK, t * BLOCK + N_KEYS)
        vt = jnp.concatenate([jnp.concatenate([vt_lo[hk][:, win], vt_hi[hk][:, win]], axis=1), ones], axis=0)
        ot = jnp.dot(vt, jnp.concatenate(ps, axis=0), preferred_element_type=jnp.float32)
        inv0 = 1.0 / (ot[LANES:LANES + 1, :] + jnp.exp2(sinks[0] - ms[0]))
        inv1 = 1.0 / (ot[LANES + 1:LANES + 2, :] + jnp.exp2(sinks[1] - ms[1]))
        o = jnp.transpose(ot[:LANES, :] * jnp.where(row_o < HEAD_DIM, inv0, inv1))
        rows = slice(t * BLOCK, (t + 1) * BLOCK)
        y_ref[rows, (2 * hk) * LANES:(2 * hk + 1) * LANES] = o[:BLOCK].astype(y_ref.dtype)
        y_ref[rows, (2 * hk + 1) * LANES:(2 * hk + 2) * LANES] = o[BLOCK:].astype(y_ref.dtype)

    work = [(t, hk) for t in range(ATTN_SUB) for hk in range(N_KV_HEADS)]
    stage_a = {0: scores(*work[0])}
    stage_b = {}
    for n in range(len(work) + 1):
        if n + 1 < len(work):
            stage_a[n + 1] = scores(*work[n + 1])
        if n < len(work):
            stage_b[n] = weights(*stage_a.pop(n))
        if n >= 1:
            finish(*work[n - 1], *stage_b.pop(n - 1))


def _attention(q, kv, sink_tab):
    nb = SEQ // BLOCK
    return pl.pallas_call(
        _attn_kernel,
        grid=(BATCH, SEQ // ATTN_STEP),
        in_specs=[pl.BlockSpec((None, ATTN_STEP, Q_COLS), lambda b, i: (b, i, 0)),
                  pl.BlockSpec((None, BLOCK, 2 * KV_COLS),
                               lambda b, i: (b, jnp.maximum(i * ATTN_SUB - 1, 0), 0)),
                  pl.BlockSpec((None, ATTN_STEP, 2 * KV_COLS), lambda b, i: (b, i, 0)),
                  pl.BlockSpec((None, BLOCK, 2 * KV_COLS),
                               lambda b, i: (b, jnp.minimum((i + 1) * ATTN_SUB, nb - 1), 0)),
                  pl.BlockSpec((N_Q_HEADS, LANES), lambda b, i: (0, 0))],
        out_specs=pl.BlockSpec((None, ATTN_STEP, D_ATTN), lambda b, i: (b, i, 0)),
        out_shape=jax.ShapeDtypeStruct((BATCH, SEQ, D_ATTN), jnp.bfloat16),
        compiler_params=pltpu.CompilerParams(dimension_semantics=("arbitrary", "arbitrary"),
                                             vmem_limit_bytes=VMEM_LIMIT),
        name="attention",
    )(q, kv, kv, kv, sink_tab)


SSM_GB = 4


def _ssm_kernel(ut_ref, m_ref, ws_ref, wo_ref, coef_ref, gt_ref,
                sf_ref, sfs_ref, sb_ref, sbs_ref, xf_ref, xb_ref, u_ref):
    for gi in range(SSM_GB):
        ut = jnp.concatenate([ut_ref[lb, gi] for lb in range(N_ROW_BLOCKS)], axis=1)
        u = jnp.transpose(ut)
        u_ref[gi] = u
        s = jnp.dot(u, ws_ref[gi], preferred_element_type=jnp.float32)
        for b in range(BATCH):
            rows = slice(b * N_CHUNKS, (b + 1) * N_CHUNKS)
            dst = pl.ds(b * SLAB_PITCH + 8, N_CHUNKS)
            sf_ref[gi, dst, :] = s[rows, 0 * LANES:1 * LANES]
            sfs_ref[gi, dst, :] = s[rows, 1 * LANES:2 * LANES]
            sb_ref[gi, dst, :] = s[rows, 2 * LANES:3 * LANES]
            sbs_ref[gi, dst, :] = s[rows, 3 * LANES:4 * LANES]

    zero = jnp.zeros((BATCH, LANES), jnp.float32)
    coefs = []
    for gi in range(SSM_GB):
        coefs.append([jnp.broadcast_to(coef_ref[gi, r:r + 1, :], (BATCH, LANES)) for r in range(6)])
        xf_ref[gi, pl.ds(8, BATCH, stride=SLAB_PITCH), :] = zero
        xb_ref[gi, pl.ds(8 + N_CHUNKS - 1, BATCH, stride=SLAB_PITCH), :] = zero

    def rows_at(r):
        return pl.ds(r, BATCH, stride=SLAB_PITCH)

    def step(k, carry):
        kb = N_CHUNKS - 1 - k
        out = []
        for gi in range(SSM_GB):
            a1f, a2f, a3f, a1b, a2b, a3b = coefs[gi]
            xf, xfs, xb, xbs = carry[4 * gi:4 * gi + 4]
            nxf = a1f * xf + a2f * xfs + sf_ref[gi, rows_at(8 + k), :]
            nxfs = a1f * xfs + a3f * xf + sfs_ref[gi, rows_at(8 + k), :]
            xf_ref[gi, rows_at(9 + k), :] = nxf
            nxb = a1b * xb + a2b * xbs + sb_ref[gi, rows_at(8 + kb), :]
            nxbs = a1b * xbs + a3b * xb + sbs_ref[gi, rows_at(8 + kb), :]
            xb_ref[gi, rows_at(7 + kb), :] = nxb
            out += [nxf, nxfs, nxb, nxbs]
        return tuple(out)

    lax.fori_loop(0, N_CHUNKS, step, (zero,) * (4 * SSM_GB), unroll=2)

    def chunk_out(gi):
        xin_f = jnp.concatenate([xf_ref[gi, pl.ds(b * SLAB_PITCH + 8, N_CHUNKS), :] for b in range(BATCH)], axis=0)
        xin_b = jnp.concatenate([xb_ref[gi, pl.ds(b * SLAB_PITCH + 8, N_CHUNKS), :] for b in range(BATCH)], axis=0)
        lhs = jnp.concatenate([u_ref[gi], xin_f.astype(jnp.bfloat16), xin_b.astype(jnp.bfloat16)], axis=1)
        rhs = jnp.concatenate([m_ref[gi], wo_ref[gi]], axis=0)
        return jnp.dot(lhs, rhs, preferred_element_type=jnp.float32)

    y = chunk_out(0)
    for gi in range(SSM_GB):
        y_next = chunk_out(gi + 1) if gi + 1 < SSM_GB else None
        gt = jnp.transpose(jax.nn.gelu(y).astype(gt_ref.dtype))
        for lb in range(N_ROW_BLOCKS):
            gt_ref[lb, gi] = gt[:, lb * LANES:(lb + 1) * LANES]
        y = y_next


def _ssm(ut, m_intra, w_state, w_out, coef):
    def per_group(shape):
        return pl.BlockSpec((SSM_GB,) + shape, lambda g: (g, 0, 0))

    lane_blocked = pl.BlockSpec((N_ROW_BLOCKS, SSM_GB, CHUNK_VEC, LANES), lambda g: (0, g, 0, 0))
    slab = pltpu.VMEM((SSM_GB, SLAB_ROWS, LANES), jnp.float32)
    return pl.pallas_call(
        _ssm_kernel,
        grid=(N_GROUPS // SSM_GB,),
        in_specs=[lane_blocked, per_group((CHUNK_VEC, CHUNK_VEC)),
                  per_group((CHUNK_VEC, 4 * LANES)), per_group((2 * STATE_VEC, CHUNK_VEC)),
                  per_group((8, LANES))],
        out_specs=lane_blocked,
        out_shape=jax.ShapeDtypeStruct((N_ROW_BLOCKS, N_GROUPS, CHUNK_VEC, LANES), jnp.bfloat16),
        scratch_shapes=[slab] * 6 + [pltpu.VMEM((SSM_GB, ROWS, CHUNK_VEC), jnp.bfloat16)],
        compiler_params=pltpu.CompilerParams(dimension_semantics=("arbitrary",),
                                             vmem_limit_bytes=VMEM_LIMIT),
        name="ssm",
    )(ut, m_intra, w_state, w_out, coef)


FINAL_SUB_ROWS = 256


def _unit_rms(v):
    return v * lax.rsqrt(jnp.mean(v * v, axis=-1, keepdims=True) + NORM_EPS)


def _final_kernel(x_ref, ya_ref, sz_ref, gt_ref, wglu_ref, bglu_ref,
                  wo_ref, lng_ref, lnb_ref, o_ref, gslab_ref):
    step = pl.program_id(0)
    part = step % STEPS_PER_REGROUP

    @pl.when(part == 0)
    def _():
        for j in range(CHUNK):
            for v in range(N_LANE_BLOCKS):
                g0 = v * GROUPS_PER_LANE_BLOCK
                gjt = gt_ref[g0:g0 + GROUPS_PER_LANE_BLOCK, j * SSM_CH:(j + 1) * SSM_CH, :]
                gj = jnp.transpose(gjt.reshape(LANES, LANES))
                gslab_ref[v, pl.ds(j, LANES, stride=SLAB_CHUNK_PITCH), :] = gj.astype(jnp.float32)

    def rows_of(sub):
        return slice(sub * FINAL_SUB_ROWS, (sub + 1) * FINAL_SUB_ROWS)

    def gate_in(sub):
        rows = rows_of(sub)
        ya = ya_ref[rows, :].astype(jnp.float32) * sz_ref[rows, :D_ATTN].astype(jnp.float32)
        ya = _unit_rms(ya).astype(jnp.bfloat16)
        base = pl.multiple_of((part * (TOK_BLOCK // CHUNK) + sub * (FINAL_SUB_ROWS // CHUNK)) * SLAB_CHUNK_PITCH, 8)
        g = jnp.concatenate(
            [jnp.concatenate([gslab_ref[v, pl.ds(base + kk * SLAB_CHUNK_PITCH, CHUNK), :]
                              for kk in range(FINAL_SUB_ROWS // CHUNK)], axis=0)
             for v in range(N_LANE_BLOCKS)], axis=1)
        return ya, g

    def glu(sub, ya, g):
        rows = rows_of(sub)
        gate = jnp.dot(g.astype(jnp.bfloat16), wglu_ref[...], preferred_element_type=jnp.float32) + bglu_ref[...]
        ys = g * jax.nn.sigmoid(gate)
        ys = ys * sz_ref[rows, D_ATTN:].astype(jnp.float32)
        return ya, _unit_rms(ys).astype(jnp.bfloat16)

    def project(sub, ya, ys):
        return (jnp.dot(ya, wo_ref[:D_ATTN, :], preferred_element_type=jnp.float32)
                + jnp.dot(ys, wo_ref[D_ATTN:, :], preferred_element_type=jnp.float32))

    def layer_norm(sub, out):
        rows = rows_of(sub)
        h = x_ref[rows, :] + out
        mu = jnp.mean(h, axis=-1, keepdims=True)
        hc = h - mu
        var = jnp.mean(hc * hc, axis=-1, keepdims=True)
        o_ref[rows, :] = hc * lax.rsqrt(var + NORM_EPS / DEEPNORM_ALPHA ** 2) * lng_ref[...] + lnb_ref[...]

    n_sub = TOK_BLOCK // FINAL_SUB_ROWS
    stage_a, stage_b = {}, {}
    for n in range(n_sub + 2):
        if n < n_sub:
            stage_a[n] = gate_in(n)
        if 1 <= n <= n_sub:
            stage_b[n - 1] = glu(n - 1, *stage_a.pop(n - 1))
        if n >= 2:
            layer_norm(n - 2, project(n - 2, *stage_b.pop(n - 2)))


def _final(x2, ya, sz, gt, wglu, bglu, wo, lng, lnb):
    n_steps = BATCH * SEQ // TOK_BLOCK

    def tok(cols):
        return pl.BlockSpec((TOK_BLOCK, cols), lambda s: (s, 0))

    def whole(shape):
        return pl.BlockSpec(shape, lambda s: (0,) * len(shape))

    return pl.pallas_call(
        _final_kernel,
        grid=(n_steps,),
        in_specs=[tok(D_MODEL), tok(D_ATTN), tok(D_ATTN + D_SSM),
                  pl.BlockSpec((None, N_GROUPS, CHUNK_VEC, LANES), lambda s: (s // STEPS_PER_REGROUP, 0, 0, 0)),
                  whole(wglu.shape), whole(bglu.shape),
                  whole(wo.shape), whole(lng.shape), whole(lnb.shape)],
        out_specs=tok(D_MODEL),
        out_shape=jax.ShapeDtypeStruct((BATCH * SEQ, D_MODEL), jnp.float32),
        scratch_shapes=[pltpu.VMEM((N_LANE_BLOCKS, LANES * SLAB_CHUNK_PITCH, LANES), jnp.float32)],
        compiler_params=pltpu.CompilerParams(dimension_semantics=("arbitrary",),
                                             vmem_limit_bytes=VMEM_LIMIT),
        name="final",
    )(x2, ya, sz, gt, wglu, bglu, wo, lng, lnb)


def kernel(x, w_in, attn_sink, ssm_a_re, ssm_a_im, ssm_log_dt, ssm_b_re, ssm_b_im, ssm_c_re, ssm_c_im,
           ssm_d, w_glu, b_glu, norm_attn_g, norm_ssm_g, w_out, ln_g, ln_b):
    assert x.shape == (BATCH, SEQ, D_MODEL) and w_in.shape[0] == 1
    bf16 = jnp.bfloat16
    f32 = jnp.float32
    m_intra, w_state, w_so, coef = _ssm_tables(ssm_a_re[0], ssm_a_im[0], ssm_log_dt[0], ssm_b_re[0],
                                                ssm_b_im[0], ssm_c_re[0], ssm_c_im[0], ssm_d[0])
    rope = _rope_tables()
    sink_tab = jnp.broadcast_to((attn_sink[0].astype(f32) * LOG2_E)[:, None], (N_Q_HEADS, LANES))

    x2 = x.reshape(BATCH * SEQ, D_MODEL)
    q, kv, sz, ut = _inproj(x2, w_in[0], rope)
    ya = _attention(q.reshape(BATCH, SEQ, Q_COLS), kv.reshape(BATCH, SEQ, 2 * KV_COLS), sink_tab)
    gt = _ssm(ut, m_intra, w_state, w_so, coef)

    gains = jnp.concatenate([norm_attn_g[0], norm_ssm_g[0]]).astype(f32)
    wo = (w_out[0].astype(f32) * (gains / DEEPNORM_ALPHA)[:, None]).astype(bf16)
    row = lambda v: v[0].astype(f32)[None, :]
    out = _final(x2, ya.reshape(BATCH * SEQ, D_ATTN), sz, gt, w_glu[0].astype(bf16), row(b_glu),
                 wo, row(ln_g), row(ln_b))
    return out.reshape(BATCH, SEQ, D_MODEL)
```

```python
import jax
import jax.numpy as jnp
import numpy as np
from jax import lax
from jax.experimental import pallas as pl
from jax.experimental.pallas import tpu as pltpu

D_MODEL = 1024
BATCH = 4
SEQ = 4096
D_ATTN = 512
D_SSM = 512
HEAD_DIM = 64
N_Q_HEADS = 8
N_KV_HEADS = 2
WINDOW = 128
BLOCK = 128
ROPE_THETA = 10000.0
SSM_CH = 16
N_GROUPS = 32
SSM_STATE = 64
NORM_EPS = 1e-5
NEG_INF = -1e30
DEEPNORM_ALPHA = 2.0 ** 0.25
LOG2_E = 1.4426950408889634
Q_SCALE = HEAD_DIM ** -0.5 * LOG2_E

Q_COLS = N_Q_HEADS * HEAD_DIM
KV_COLS = N_KV_HEADS * HEAD_DIM
CHUNK = 16
N_CHUNKS = SEQ // CHUNK
CHUNK_VEC = CHUNK * SSM_CH
STATE_VEC = 2 * SSM_STATE
LANES = 128
ROWS = BATCH * N_CHUNKS
N_ROW_BLOCKS = ROWS // LANES
SLAB_PITCH = N_CHUNKS + 8
SLAB_ROWS = BATCH * SLAB_PITCH + 8
TOK_BLOCK = 1024
REGROUP_TOK = LANES * CHUNK
STEPS_PER_REGROUP = REGROUP_TOK // TOK_BLOCK
SUB_ROWS = 512
N_LANE_BLOCKS = D_SSM // LANES
SLAB_CHUNK_PITCH = 20
GROUPS_PER_LANE_BLOCK = LANES // SSM_CH
VMEM_LIMIT = 56 * 1024 * 1024
HIGHEST = lax.Precision.HIGHEST


LAG_ROWS = 512
PREP_GB = N_GROUPS // (BATCH * SEQ // TOK_BLOCK)


def _ssm_tables(ldt_ref, are_ref, aim_ref, bre_ref, bim_ref, cre_ref, cim_ref, d_ref,
                m_ref, ws_ref, wo_ref, coef_ref):
    f32 = jnp.float32
    lo16 = lax.broadcasted_iota(jnp.int32, (SSM_CH, LANES), 1) < SSM_STATE
    lo1 = lax.broadcasted_iota(jnp.int32, (1, LANES), 1) < SSM_STATE
    units = [(gi, d) for gi in range(PREP_GB) for d in range(2)]

    def group(gi):
        return pl.program_id(0) * PREP_GB + gi

    def dup(v):
        return jnp.concatenate([v, v], axis=1)

    lam, zoh = {}, {}
    for gi, d in units:
        ar = dup(are_ref[d, pl.ds(group(gi), 1), :])
        ai = dup(aim_ref[d, pl.ds(group(gi), 1), :])
        dt = jnp.exp(jnp.full((1, LANES), ldt_ref[d, group(gi)], f32))
        zr = dt * ar
        zi = dt * ai
        mag = jnp.exp(zr)
        lr = mag * jnp.cos(zi)
        li = mag * jnp.sin(zi)
        den = ar * ar + ai * ai
        lam[gi, d] = (lr, li)
        zoh[gi, d] = (((lr - 1.0) * ar + li * ai) / den, (li * ar - (lr - 1.0) * ai) / den)

    pr = {u: [jnp.ones((1, LANES), f32)] for u in units}
    pi = {u: [jnp.zeros((1, LANES), f32)] for u in units}
    for _ in range(CHUNK):
        for u in units:
            lr, li = lam[u]
            pr[u].append(pr[u][-1] * lr - pi[u][-1] * li)
            pi[u].append(pr[u][-2] * li + pi[u][-1] * lr)

    lag_tabs, kws = {}, {}
    for gi in range(PREP_GB):
        ws_parts, coef_rows = [], []
        for d in range(2):
            u = (gi, d)
            fr, fi = zoh[u]
            btr = dup(jnp.transpose(bre_ref[d, gi]))
            bti = dup(jnp.transpose(bim_ref[d, gi]))
            bbr = fr * btr - fi * bti
            bbi = fr * bti + fi * btr
            cr = dup(cre_ref[d, gi])
            ci = dup(cim_ref[d, gi])
            pa = [jnp.where(lo1, r, i) for r, i in zip(pr[u], pi[u])]
            pb = [jnp.where(lo1, -i, r) for r, i in zip(pr[u], pi[u])]
            pc = [jnp.where(lo1, i, r) for r, i in zip(pr[u], pi[u])]
            pd = [jnp.where(lo1, r, -i) for r, i in zip(pr[u], pi[u])]

            order = range(CHUNK + 1) if d == 0 else range(CHUNK, -1, -1)
            blocks = [cr * pa[t] + ci * pb[t] for t in order]
            blocks.append(jnp.zeros((LAG_ROWS - (CHUNK + 1) * SSM_CH, LANES), f32))
            lag_tabs[u] = jnp.concatenate(blocks, axis=0)
            lhs = jnp.where(lo16, bbr, -bbi)
            kws[u] = lax.dot_general(lhs, lag_tabs[u], (((1,), (1,)), ((), ())),
                                     precision=HIGHEST, preferred_element_type=f32)

            parts = []
            for j in range(CHUNK):
                t = CHUNK - 1 - j if d == 0 else j
                parts.append(jnp.concatenate([bbr * pa[t] + bbi * pb[t],
                                              bbr * pc[t] + bbi * pd[t]], axis=1))
            ws_parts.append(jnp.concatenate(parts, axis=0))

            a2 = jnp.where(lo1, -pi[u][CHUNK], pi[u][CHUNK])
            coef_rows += [pr[u][CHUNK], a2, -a2]

        ws_ref[gi] = jnp.concatenate(ws_parts, axis=1).astype(ws_ref.dtype)
        coef_ref[gi] = jnp.concatenate(coef_rows + [jnp.zeros((2, LANES), f32)], axis=0)

    sign = jnp.where(lax.broadcasted_iota(jnp.int32, (LANES, CHUNK_VEC), 0) < SSM_STATE, 1.0, -1.0)
    sub = lax.broadcasted_iota(jnp.int32, (SSM_CH, CHUNK_VEC), 0)
    lane = lax.broadcasted_iota(jnp.int32, (SSM_CH, CHUNK_VEC), 1)
    for gi in range(PREP_GB):
        wo_f = jnp.transpose(lag_tabs[gi, 0][SSM_CH:SSM_CH + CHUNK_VEC, :]) * sign
        wo_b = jnp.transpose(lag_tabs[gi, 1][:CHUNK_VEC, :]) * sign
        wo_ref[gi] = jnp.concatenate([wo_f, wo_b], axis=0).astype(wo_ref.dtype)

        g = group(gi)
        dblk = d_ref[:, pl.ds(pl.multiple_of((g // GROUPS_PER_LANE_BLOCK) * LANES, LANES), LANES)]
        dblk = pltpu.roll(dblk, (LANES - (g % GROUPS_PER_LANE_BLOCK) * SSM_CH) % LANES, 1)
        dblk = jnp.where(lax.broadcasted_iota(jnp.int32, (1, LANES), 1) < SSM_CH, dblk, 0.0)
        for shift in (SSM_CH, 2 * SSM_CH, 4 * SSM_CH):
            dblk = dblk + pltpu.roll(dblk, shift, 1)
        dtile = jnp.broadcast_to(jnp.concatenate([dblk, dblk], axis=1), (SSM_CH, CHUNK_VEC))
        for jp in range(CHUNK):
            fwd = kws[gi, 0] if jp == 0 else pltpu.roll(kws[gi, 0], SSM_CH * jp, 1)
            bwd = pltpu.roll(kws[gi, 1], (LAG_ROWS - SSM_CH * (CHUNK - jp)) % LAG_ROWS, 1)
            skip = jnp.where((lane // SSM_CH == jp) & (lane % SSM_CH == sub), dtile, 0.0)
            blk = fwd[:, :CHUNK_VEC] + bwd[:, :CHUNK_VEC] + skip
            m_ref[gi, jp * SSM_CH:(jp + 1) * SSM_CH, :] = blk.astype(m_ref.dtype)


def _rope_tables():
    half = HEAD_DIM // 2
    inv_freq = ROPE_THETA ** (-np.arange(half, dtype=np.float64) / half)
    ang = np.arange(SEQ, dtype=np.float64)[:, None] * inv_freq[None, :]
    cos, sin = np.cos(ang), np.sin(ang)
    return jnp.asarray(np.concatenate([cos, cos, cos, cos, -sin, sin, -sin, sin], axis=1), jnp.float32)


def _rope_block(xb, cos_t, sin_t, first_half):
    swapped = jnp.where(first_half, pltpu.roll(xb, 96, 1), pltpu.roll(xb, 32, 1))
    return xb * cos_t + swapped * sin_t


COL_Q = (0, Q_COLS)
COL_KV = (COL_Q[1], COL_Q[1] + 2 * KV_COLS)
COL_ZA = (COL_KV[1], COL_KV[1] + D_ATTN)
COL_U = (COL_ZA[1], COL_ZA[1] + D_SSM)
COL_ZS = (COL_U[1], COL_U[1] + D_SSM)


def _silu(z):
    h = 0.5 * z
    return h + h * jnp.tanh(h)


def _inproj_kernel(x_ref, w32_ref, rope_ref, ldt_ref, are_ref, aim_ref, bre_ref, bim_ref, cre_ref, cim_ref, d_ref,
                   q_ref, kv_ref, sz_ref, ut_ref, m_ref, ws_ref, wo_ref, coef_ref, uslab_ref, w_ref):
    step = pl.program_id(0)
    part = step % STEPS_PER_REGROUP
    lane = lax.broadcasted_iota(jnp.int32, (SUB_ROWS, LANES), 1)
    first_half = (lane % HEAD_DIM) < (HEAD_DIM // 2)

    @pl.when(step == 0)
    def _():
        for c0 in range(0, w_ref.shape[1], 2 * LANES):
            w_ref[:, c0:c0 + 2 * LANES] = w32_ref[:, c0:c0 + 2 * LANES].astype(w_ref.dtype)

    def proj(xb, cols):
        return jnp.dot(xb, w_ref[:, cols[0]:cols[1]], preferred_element_type=jnp.float32)

    for sub in range(TOK_BLOCK // SUB_ROWS):
        rows = slice(sub * SUB_ROWS, (sub + 1) * SUB_ROWS)
        xb = x_ref[rows, :].astype(jnp.bfloat16)
        cos_t = rope_ref[rows, :LANES]
        sin_t = rope_ref[rows, LANES:]
        q = proj(xb, COL_Q)
        kv = proj(xb, COL_KV)
        q_blocks = [_rope_block(q[:, v * LANES:(v + 1) * LANES], cos_t, sin_t, first_half)
                    for v in range(Q_COLS // LANES)]
        q_ref[rows, :] = (jnp.concatenate(q_blocks, axis=1) * Q_SCALE).astype(q_ref.dtype)
        za = proj(xb, COL_ZA)
        k_rot = _rope_block(kv[:, :LANES], cos_t, sin_t, first_half)
        kv_ref[rows, :] = jnp.concatenate([k_rot, kv[:, LANES:]], axis=1).astype(kv_ref.dtype)
        u = proj(xb, COL_U)
        sz_ref[rows, :D_ATTN] = _silu(za).astype(sz_ref.dtype)
        zs = proj(xb, COL_ZS)
        for kk in range(SUB_ROWS // CHUNK):
            base = pl.multiple_of((part * (TOK_BLOCK // CHUNK) + sub * (SUB_ROWS // CHUNK)) * SLAB_CHUNK_PITCH, 8)
            for v in range(N_LANE_BLOCKS):
                uslab_ref[v, pl.ds(base + kk * SLAB_CHUNK_PITCH, CHUNK), :] = (
                    u[kk * CHUNK:(kk + 1) * CHUNK, v * LANES:(v + 1) * LANES])
        sz_ref[rows, D_ATTN:] = _silu(zs).astype(sz_ref.dtype)
        if sub == 0:
            _ssm_tables(ldt_ref, are_ref, aim_ref, bre_ref, bim_ref, cre_ref, cim_ref, d_ref,
                        m_ref, ws_ref, wo_ref, coef_ref)

    @pl.when(part == STEPS_PER_REGROUP - 1)
    def _():
        for j in range(CHUNK):
            for v in range(N_LANE_BLOCKS):
                uj = uslab_ref[v, pl.ds(j, LANES, stride=SLAB_CHUNK_PITCH), :]
                ujt = jnp.transpose(uj.astype(ut_ref.dtype)).reshape(GROUPS_PER_LANE_BLOCK, SSM_CH, LANES)
                g0 = v * GROUPS_PER_LANE_BLOCK
                ut_ref[g0:g0 + GROUPS_PER_LANE_BLOCK, j * SSM_CH:(j + 1) * SSM_CH, :] = ujt


def _inproj(x2, w, rope, log_dt, a_re, a_im, b_re, b_im, c_re, c_im, d):
    n_steps = BATCH * SEQ // TOK_BLOCK
    blocks_per_seq = SEQ // TOK_BLOCK
    bf16 = jnp.bfloat16

    def tok(cols):
        return pl.BlockSpec((TOK_BLOCK, cols), lambda s: (s, 0))

    def whole(a):
        return pl.BlockSpec(a.shape, lambda s: (0,) * a.ndim)

    def group_in(a):
        return pl.BlockSpec((2, PREP_GB) + a.shape[2:], lambda s: (0, s, 0, 0))

    def group_out(rows, cols):
        return pl.BlockSpec((PREP_GB, rows, cols), lambda s: (s, 0, 0))

    return pl.pallas_call(
        _inproj_kernel,
        grid=(n_steps,),
        in_specs=[tok(D_MODEL), pl.BlockSpec(w.shape, lambda s: (0, 0), pipeline_mode=pl.Buffered(1)),
                  pl.BlockSpec((TOK_BLOCK, 2 * LANES), lambda s: (s % blocks_per_seq, 0)),
                  pl.BlockSpec(memory_space=pltpu.SMEM), whole(a_re), whole(a_im),
                  group_in(b_re), group_in(b_im), group_in(c_re), group_in(c_im), whole(d)],
        out_specs=[tok(Q_COLS), tok(2 * KV_COLS), tok(D_ATTN + D_SSM),
                   pl.BlockSpec((None, N_GROUPS, CHUNK_VEC, LANES), lambda s: (s // STEPS_PER_REGROUP, 0, 0, 0)),
                   group_out(CHUNK_VEC, CHUNK_VEC), group_out(CHUNK_VEC, 4 * LANES),
                   group_out(2 * STATE_VEC, CHUNK_VEC), group_out(8, LANES)],
        out_shape=[jax.ShapeDtypeStruct((BATCH * SEQ, Q_COLS), bf16),
                   jax.ShapeDtypeStruct((BATCH * SEQ, 2 * KV_COLS), bf16),
                   jax.ShapeDtypeStruct((BATCH * SEQ, D_ATTN + D_SSM), bf16),
                   jax.ShapeDtypeStruct((N_ROW_BLOCKS, N_GROUPS, CHUNK_VEC, LANES), bf16),
                   jax.ShapeDtypeStruct((N_GROUPS, CHUNK_VEC, CHUNK_VEC), bf16),
                   jax.ShapeDtypeStruct((N_GROUPS, CHUNK_VEC, 4 * LANES), bf16),
                   jax.ShapeDtypeStruct((N_GROUPS, 2 * STATE_VEC, CHUNK_VEC), bf16),
                   jax.ShapeDtypeStruct((N_GROUPS, 8, LANES), jnp.float32)],
        scratch_shapes=[pltpu.VMEM((N_LANE_BLOCKS, LANES * SLAB_CHUNK_PITCH, LANES), jnp.float32),
                        pltpu.VMEM(w.shape, bf16)],
        compiler_params=pltpu.CompilerParams(dimension_semantics=("arbitrary",),
                                             vmem_limit_bytes=VMEM_LIMIT),
        name="inproj",
    )(x2, w, rope, log_dt, a_re, a_im, b_re, b_im, c_re, c_im, d)


ATTN_SUB = 32
ATTN_STEP = ATTN_SUB * BLOCK
N_KEYS = 3 * BLOCK


def _attn_kernel(q_ref, kp_ref, kc_ref, kn_ref, sink_ref, y_ref):
    i = pl.program_id(1)
    kvw = jnp.concatenate([kp_ref[...], kc_ref[...], kn_ref[...]], axis=0).astype(jnp.float32)

    n_win = kvw.shape[0]
    lane_w = lax.broadcasted_iota(jnp.int32, (n_win, LANES), 1)
    k_lo, k_hi, vt_lo, vt_hi = [], [], [], []
    for hk in range(N_KV_HEADS):
        in_head = (lane_w >= hk * HEAD_DIM) & (lane_w < (hk + 1) * HEAD_DIM)
        k_nat = jnp.where(in_head, kvw[:, :LANES], 0.0)
        v_nat = jnp.where(in_head, kvw[:, LANES:], 0.0)
        k_oth = pltpu.roll(k_nat, HEAD_DIM, 1)
        v_oth = pltpu.roll(v_nat, HEAD_DIM, 1)
        pairs = ((k_nat, k_oth), (v_nat, v_oth)) if hk == 0 else ((k_oth, k_nat), (v_oth, v_nat))
        k_lo.append(pairs[0][0].astype(jnp.bfloat16))
        k_hi.append(pairs[0][1].astype(jnp.bfloat16))
        vt_lo.append(jnp.transpose(pairs[1][0]).astype(jnp.bfloat16))
        vt_hi.append(jnp.transpose(pairs[1][1]).astype(jnp.bfloat16))

    c = lax.broadcasted_iota(jnp.int32, (N_KEYS, BLOCK), 0)
    r = lax.broadcasted_iota(jnp.int32, (N_KEYS, BLOCK), 1)
    band = (c >= r) & (c - r <= 2 * WINDOW)
    ones_row = lax.broadcasted_iota(jnp.int32, (16, 2 * N_KEYS), 0)
    ones_col = lax.broadcasted_iota(jnp.int32, (16, 2 * N_KEYS), 1)
    ones = jnp.where(((ones_row == 0) & (ones_col < N_KEYS)) | ((ones_row == 1) & (ones_col >= N_KEYS)),
                     1.0, 0.0).astype(jnp.bfloat16)
    row_o = lax.broadcasted_iota(jnp.int32, (LANES, 2 * BLOCK), 0)
    n_blocks = SEQ // BLOCK

    def scores(t, hk):
        blk = i * ATTN_SUB + t
        lo = jnp.where(blk == 0, BLOCK, 0)
        hi = jnp.where(blk == n_blocks - 1, 2 * BLOCK, N_KEYS)
        valid1 = band & (c >= lo) & (c < hi)
        valid = jnp.concatenate([valid1, valid1], axis=1)
        win = slice(t * BLOCK, t * BLOCK + N_KEYS)
        q = q_ref[t * BLOCK:(t + 1) * BLOCK, (2 * hk) * LANES:(2 * hk + 2) * LANES]
        qq = jnp.concatenate([q[:, :LANES], q[:, LANES:]], axis=0)
        kcat = jnp.concatenate([k_lo[hk][win], k_hi[hk][win]], axis=0)
        st = lax.dot_general(kcat, qq, (((1,), (1,)), ((), ())),
                             preferred_element_type=jnp.float32)
        sps, ms, sinks = [], [], []
        for par in range(2):
            base = par * N_KEYS
            sp = jnp.concatenate(
                [jnp.where(valid[:BLOCK], st[base:base + BLOCK, :], NEG_INF),
                 st[base + BLOCK:base + 2 * BLOCK, :],
                 jnp.where(valid[2 * BLOCK:], st[base + 2 * BLOCK:base + N_KEYS, :], NEG_INF)], axis=0)
            h0 = 4 * hk + par
            h1 = h0 + 2
            sink = jnp.concatenate([sink_ref[h0:h0 + 1, :], sink_ref[h1:h1 + 1, :]], axis=1)
            sps.append(sp)
            ms.append(jnp.maximum(jnp.max(sp, axis=0, keepdims=True), sink))
            sinks.append(sink)
        return sps, ms, sinks

    def weights(sps, ms, sinks):
        return [jnp.exp2((sp - m).astype(jnp.bfloat16)) for sp, m in zip(sps, ms)], ms, sinks

    def finish(t, hk, ps, ms, sinks):
        win = slice(t * BLOCK, t * BLOCK + N_KEYS)
        vt = jnp.concatenate([jnp.concatenate([vt_lo[hk][:, win], vt_hi[hk][:, win]], axis=1), ones], axis=0)
        ot = jnp.dot(vt, jnp.concatenate(ps, axis=0), preferred_element_type=jnp.float32)
        inv0 = 1.0 / (ot[LANES:LANES + 1, :] + jnp.exp2(sinks[0] - ms[0]))
        inv1 = 1.0 / (ot[LANES + 1:LANES + 2, :] + jnp.exp2(sinks[1] - ms[1]))
        o = jnp.transpose(ot[:LANES, :] * jnp.where(row_o < HEAD_DIM, inv0, inv1))
        rows = slice(t * BLOCK, (t + 1) * BLOCK)
        y_ref[rows, (2 * hk) * LANES:(2 * hk + 1) * LANES] = o[:BLOCK].astype(y_ref.dtype)
        y_ref[rows, (2 * hk + 1) * LANES:(2 * hk + 2) * LANES] = o[BLOCK:].astype(y_ref.dtype)

    work = [(t, hk) for t in range(ATTN_SUB) for hk in range(N_KV_HEADS)]
    stage_a = {0: scores(*work[0])}
    stage_b = {}
    for n in range(len(work) + 1):
        if n + 1 < len(work):
            stage_a[n + 1] = scores(*work[n + 1])
        if n < len(work):
            stage_b[n] = weights(*stage_a.pop(n))
        if n >= 1:
            finish(*work[n - 1], *stage_b.pop(n - 1))


def _attention(q, kv, sink_tab):
    nb = SEQ // BLOCK
    return pl.pallas_call(
        _attn_kernel,
        grid=(BATCH, SEQ // ATTN_STEP),
        in_specs=[pl.BlockSpec((None, ATTN_STEP, Q_COLS), lambda b, i: (b, i, 0)),
                  pl.BlockSpec((None, BLOCK, 2 * KV_COLS),
                               lambda b, i: (b, jnp.maximum(i * ATTN_SUB - 1, 0), 0)),
                  pl.BlockSpec((None, ATTN_STEP, 2 * KV_COLS), lambda b, i: (b, i, 0)),
                  pl.BlockSpec((None, BLOCK, 2 * KV_COLS),
                               lambda b, i: (b, jnp.minimum((i + 1) * ATTN_SUB, nb - 1), 0)),
                  pl.BlockSpec((N_Q_HEADS, LANES), lambda b, i: (0, 0))],
        out_specs=pl.BlockSpec((None, ATTN_STEP, D_ATTN), lambda b, i: (b, i, 0)),
        out_shape=jax.ShapeDtypeStruct((BATCH, SEQ, D_ATTN), jnp.bfloat16),
        compiler_params=pltpu.CompilerParams(dimension_semantics=("arbitrary", "arbitrary"),
                                             vmem_limit_bytes=VMEM_LIMIT),
        name="attention",
    )(q, kv, kv, kv, sink_tab)


SSM_GB = 4


def _ssm_kernel(ut_ref, m_ref, ws_ref, wo_ref, coef_ref, gt_ref,
                sf_ref, sfs_ref, sb_ref, sbs_ref, xf_ref, xb_ref, u_ref):
    for gi in range(SSM_GB):
        ut = jnp.concatenate([ut_ref[lb, gi] for lb in range(N_ROW_BLOCKS)], axis=1)
        u = jnp.transpose(ut)
        u_ref[gi] = u
        s = jnp.dot(u, ws_ref[gi], preferred_element_type=jnp.float32)
        for b in range(BATCH):
            rows = slice(b * N_CHUNKS, (b + 1) * N_CHUNKS)
            dst = pl.ds(b * SLAB_PITCH + 8, N_CHUNKS)
            sf_ref[gi, dst, :] = s[rows, 0 * LANES:1 * LANES]
            sfs_ref[gi, dst, :] = s[rows, 1 * LANES:2 * LANES]
            sb_ref[gi, dst, :] = s[rows, 2 * LANES:3 * LANES]
            sbs_ref[gi, dst, :] = s[rows, 3 * LANES:4 * LANES]

    zero = jnp.zeros((BATCH, LANES), jnp.float32)
    coefs = []
    for gi in range(SSM_GB):
        coefs.append([jnp.broadcast_to(coef_ref[gi, r:r + 1, :], (BATCH, LANES)) for r in range(6)])
        xf_ref[gi, pl.ds(8, BATCH, stride=SLAB_PITCH), :] = zero
        xb_ref[gi, pl.ds(8 + N_CHUNKS - 1, BATCH, stride=SLAB_PITCH), :] = zero

    def rows_at(r):
        return pl.ds(r, BATCH, stride=SLAB_PITCH)

    def step(k, carry):
        kb = N_CHUNKS - 1 - k
        out = []
        for gi in range(SSM_GB):
            a1f, a2f, a3f, a1b, a2b, a3b = coefs[gi]
            xf, xfs, xb, xbs = carry[4 * gi:4 * gi + 4]
            nxf = a1f * xf + a2f * xfs + sf_ref[gi, rows_at(8 + k), :]
            nxfs = a1f * xfs + a3f * xf + sfs_ref[gi, rows_at(8 + k), :]
            xf_ref[gi, rows_at(9 + k), :] = nxf
            nxb = a1b * xb + a2b * xbs + sb_ref[gi, rows_at(8 + kb), :]
            nxbs = a1b * xbs + a3b * xb + sbs_ref[gi, rows_at(8 + kb), :]
            xb_ref[gi, rows_at(7 + kb), :] = nxb
            out += [nxf, nxfs, nxb, nxbs]
        return tuple(out)

    lax.fori_loop(0, N_CHUNKS, step, (zero,) * (4 * SSM_GB), unroll=2)

    def chunk_out(gi):
        xin_f = jnp.concatenate([xf_ref[gi, pl.ds(b * SLAB_PITCH + 8, N_CHUNKS), :] for b in range(BATCH)], axis=0)
        xin_b = jnp.concatenate([xb_ref[gi, pl.ds(b * SLAB_PITCH + 8, N_CHUNKS), :] for b in range(BATCH)], axis=0)
        lhs = jnp.concatenate([u_ref[gi], xin_f.astype(jnp.bfloat16), xin_b.astype(jnp.bfloat16)], axis=1)
        rhs = jnp.concatenate([m_ref[gi], wo_ref[gi]], axis=0)
        return jnp.dot(lhs, rhs, preferred_element_type=jnp.float32)

    y = chunk_out(0)
    for gi in range(SSM_GB):
        y_next = chunk_out(gi + 1) if gi + 1 < SSM_GB else None
        gt = jnp.transpose(jax.nn.gelu(y).astype(gt_ref.dtype))
        for lb in range(N_ROW_BLOCKS):
            gt_ref[lb, gi] = gt[:, lb * LANES:(lb + 1) * LANES]
        y = y_next


def _ssm(ut, m_intra, w_state, w_out, coef):
    def per_group(shape):
        return pl.BlockSpec((SSM_GB,) + shape, lambda g: (g, 0, 0))

    lane_blocked = pl.BlockSpec((N_ROW_BLOCKS, SSM_GB, CHUNK_VEC, LANES), lambda g: (0, g, 0, 0))
    slab = pltpu.VMEM((SSM_GB, SLAB_ROWS, LANES), jnp.float32)
    return pl.pallas_call(
        _ssm_kernel,
        grid=(N_GROUPS // SSM_GB,),
        in_specs=[lane_blocked, per_group((CHUNK_VEC, CHUNK_VEC)),
                  per_group((CHUNK_VEC, 4 * LANES)), per_group((2 * STATE_VEC, CHUNK_VEC)),
                  per_group((8, LANES))],
        out_specs=lane_blocked,
        out_shape=jax.ShapeDtypeStruct((N_ROW_BLOCKS, N_GROUPS, CHUNK_VEC, LANES), jnp.bfloat16),
        scratch_shapes=[slab] * 6 + [pltpu.VMEM((SSM_GB, ROWS, CHUNK_VEC), jnp.bfloat16)],
        compiler_params=pltpu.CompilerParams(dimension_semantics=("arbitrary",),
                                             vmem_limit_bytes=VMEM_LIMIT),
        name="ssm",
    )(ut, m_intra, w_state, w_out, coef)


FINAL_SUB_ROWS = 256


def _unit_rms(v):
    return v * lax.rsqrt(jnp.mean(v * v, axis=-1, keepdims=True) + NORM_EPS)


def _final_kernel(x_ref, ya_ref, sz_ref, gt_ref, wglu_ref, bglu_ref,
                  wo_ref, lng_ref, lnb_ref, o_ref, gslab_ref):
    step = pl.program_id(0)
    part = step % STEPS_PER_REGROUP

    @pl.when(part == 0)
    def _():
        for j in range(CHUNK):
            for v in range(N_LANE_BLOCKS):
                g0 = v * GROUPS_PER_LANE_BLOCK
                gjt = gt_ref[g0:g0 + GROUPS_PER_LANE_BLOCK, j * SSM_CH:(j + 1) * SSM_CH, :]
                gj = jnp.transpose(gjt.reshape(LANES, LANES))
                gslab_ref[v, pl.ds(j, LANES, stride=SLAB_CHUNK_PITCH), :] = gj.astype(jnp.float32)

    def rows_of(sub):
        return slice(sub * FINAL_SUB_ROWS, (sub + 1) * FINAL_SUB_ROWS)

    def gate_in(sub):
        rows = rows_of(sub)
        ya = ya_ref[rows, :].astype(jnp.float32) * sz_ref[rows, :D_ATTN].astype(jnp.float32)
        ya = _unit_rms(ya).astype(jnp.bfloat16)
        base = pl.multiple_of((part * (TOK_BLOCK // CHUNK) + sub * (FINAL_SUB_ROWS // CHUNK)) * SLAB_CHUNK_PITCH, 8)
        g = jnp.concatenate(
            [jnp.concatenate([gslab_ref[v, pl.ds(base + kk * SLAB_CHUNK_PITCH, CHUNK), :]
                              for kk in range(FINAL_SUB_ROWS // CHUNK)], axis=0)
             for v in range(N_LANE_BLOCKS)], axis=1)
        return ya, g

    def glu(sub, ya, g):
        rows = rows_of(sub)
        gate = jnp.dot(g.astype(jnp.bfloat16), wglu_ref[...], preferred_element_type=jnp.float32) + bglu_ref[...]
        ys = g * jax.nn.sigmoid(gate)
        ys = ys * sz_ref[rows, D_ATTN:].astype(jnp.float32)
        return ya, _unit_rms(ys).astype(jnp.bfloat16)

    def project(sub, ya, ys):
        return (jnp.dot(ya, wo_ref[:D_ATTN, :], preferred_element_type=jnp.float32)
                + jnp.dot(ys, wo_ref[D_ATTN:, :], preferred_element_type=jnp.float32))

    def layer_norm(sub, out):
        rows = rows_of(sub)
        h = x_ref[rows, :] + out
        mu = jnp.mean(h, axis=-1, keepdims=True)
        hc = h - mu
        var = jnp.mean(hc * hc, axis=-1, keepdims=True)
        o_ref[rows, :] = hc * lax.rsqrt(var + NORM_EPS / DEEPNORM_ALPHA ** 2) * lng_ref[...] + lnb_ref[...]

    n_sub = TOK_BLOCK // FINAL_SUB_ROWS
    stage_a, stage_b = {}, {}
    for n in range(n_sub + 2):
        if n < n_sub:
            stage_a[n] = gate_in(n)
        if 1 <= n <= n_sub:
            stage_b[n - 1] = glu(n - 1, *stage_a.pop(n - 1))
        if n >= 2:
            layer_norm(n - 2, project(n - 2, *stage_b.pop(n - 2)))


def _final(x2, ya, sz, gt, wglu, bglu, wo, lng, lnb):
    n_steps = BATCH * SEQ // TOK_BLOCK

    def tok(cols):
        return pl.BlockSpec((TOK_BLOCK, cols), lambda s: (s, 0))

    def whole(shape):
        return pl.BlockSpec(shape, lambda s: (0,) * len(shape))

    return pl.pallas_call(
        _final_kernel,
        grid=(n_steps,),
        in_specs=[tok(D_MODEL), tok(D_ATTN), tok(D_ATTN + D_SSM),
                  pl.BlockSpec((None, N_GROUPS, CHUNK_VEC, LANES), lambda s: (s // STEPS_PER_REGROUP, 0, 0, 0)),
                  whole(wglu.shape), whole(bglu.shape),
                  whole(wo.shape), whole(lng.shape), whole(lnb.shape)],
        out_specs=tok(D_MODEL),
        out_shape=jax.ShapeDtypeStruct((BATCH * SEQ, D_MODEL), jnp.float32),
        scratch_shapes=[pltpu.VMEM((N_LANE_BLOCKS, LANES * SLAB_CHUNK_PITCH, LANES), jnp.float32)],
        compiler_params=pltpu.CompilerParams(dimension_semantics=("arbitrary",),
                                             vmem_limit_bytes=VMEM_LIMIT),
        name="final",
    )(x2, ya, sz, gt, wglu, bglu, wo, lng, lnb)


def kernel(x, w_in, attn_sink, ssm_a_re, ssm_a_im, ssm_log_dt, ssm_b_re, ssm_b_im, ssm_c_re, ssm_c_im,
           ssm_d, w_glu, b_glu, norm_attn_g, norm_ssm_g, w_out, ln_g, ln_b):
    assert x.shape == (BATCH, SEQ, D_MODEL) and w_in.shape[0] == 1
    bf16 = jnp.bfloat16
    f32 = jnp.float32
    rope = _rope_tables()
    sink_tab = jnp.broadcast_to((attn_sink[0].astype(f32) * LOG2_E)[:, None], (N_Q_HEADS, LANES))

    x2 = x.reshape(BATCH * SEQ, D_MODEL)
    params = [p[0].astype(f32) for p in (ssm_log_dt, ssm_a_re, ssm_a_im, ssm_b_re, ssm_b_im, ssm_c_re, ssm_c_im)]
    q, kv, sz, ut, m_intra, w_state, w_so, coef = _inproj(x2, w_in[0], rope, *params,
                                                           ssm_d[0].astype(f32).reshape(1, D_SSM))
    ya = _attention(q.reshape(BATCH, SEQ, Q_COLS), kv.reshape(BATCH, SEQ, 2 * KV_COLS), sink_tab)
    gt = _ssm(ut, m_intra, w_state, w_so, coef)

    gains = jnp.concatenate([norm_attn_g[0], norm_ssm_g[0]]).astype(f32)
    wo = (w_out[0].astype(f32) * (gains / DEEPNORM_ALPHA)[:, None]).astype(bf16)
    row = lambda v: v[0].astype(f32)[None, :]
    out = _final(x2, ya.reshape(BATCH * SEQ, D_ATTN), sz, gt, w_glu[0].astype(bf16), row(b_glu),
                 wo, row(ln_g), row(ln_b))
    return out.reshape(BATCH, SEQ, D_MODEL)
```

```python
import jax
import jax.numpy as jnp
import numpy as np
from jax import lax
from jax.experimental import pallas as pl
from jax.experimental.pallas import tpu as pltpu

D_MODEL = 1024
BATCH = 4
SEQ = 4096
D_ATTN = 512
D_SSM = 512
HEAD_DIM = 64
N_Q_HEADS = 8
N_KV_HEADS = 2
WINDOW = 128
BLOCK = 128
ROPE_THETA = 10000.0
SSM_CH = 16
N_GROUPS = 32
SSM_STATE = 64
NORM_EPS = 1e-5
NEG_INF = -1e30
DEEPNORM_ALPHA = 2.0 ** 0.25
LOG2_E = 1.4426950408889634
Q_SCALE = HEAD_DIM ** -0.5 * LOG2_E

Q_COLS = N_Q_HEADS * HEAD_DIM
KV_COLS = N_KV_HEADS * HEAD_DIM
CHUNK = 16
N_CHUNKS = SEQ // CHUNK
CHUNK_VEC = CHUNK * SSM_CH
STATE_VEC = 2 * SSM_STATE
LANES = 128
ROWS = BATCH * N_CHUNKS
N_ROW_BLOCKS = ROWS // LANES
SLAB_PITCH = N_CHUNKS + 8
SLAB_ROWS = BATCH * SLAB_PITCH + 8
TOK_BLOCK = 1024
REGROUP_TOK = LANES * CHUNK
STEPS_PER_REGROUP = REGROUP_TOK // TOK_BLOCK
SUB_ROWS = 512
N_LANE_BLOCKS = D_SSM // LANES
SLAB_CHUNK_PITCH = 20
GROUPS_PER_LANE_BLOCK = LANES // SSM_CH
VMEM_LIMIT = 56 * 1024 * 1024
HIGHEST = lax.Precision.HIGHEST


LAG_ROWS = 512
PREP_GB = N_GROUPS // (BATCH * SEQ // TOK_BLOCK)


def _ssm_tables(ldt_ref, are_ref, aim_ref, bre_ref, bim_ref, cre_ref, cim_ref, d_ref,
                m_ref, ws_ref, wo_ref, coef_ref):
    f32 = jnp.float32
    lo16 = lax.broadcasted_iota(jnp.int32, (SSM_CH, LANES), 1) < SSM_STATE
    lo1 = lax.broadcasted_iota(jnp.int32, (1, LANES), 1) < SSM_STATE
    units = [(gi, d) for gi in range(PREP_GB) for d in range(2)]

    def group(gi):
        return pl.program_id(0) * PREP_GB + gi

    def dup(v):
        return jnp.concatenate([v, v], axis=1)

    lam, zoh = {}, {}
    for gi, d in units:
        ar = dup(are_ref[d, pl.ds(group(gi), 1), :])
        ai = dup(aim_ref[d, pl.ds(group(gi), 1), :])
        dt = jnp.exp(jnp.full((1, LANES), ldt_ref[d, group(gi)], f32))
        zr = dt * ar
        zi = dt * ai
        mag = jnp.exp(zr)
        lr = mag * jnp.cos(zi)
        li = mag * jnp.sin(zi)
        den = ar * ar + ai * ai
        lam[gi, d] = (lr, li)
        zoh[gi, d] = (((lr - 1.0) * ar + li * ai) / den, (li * ar - (lr - 1.0) * ai) / den)

    pr = {u: [jnp.ones((1, LANES), f32)] for u in units}
    pi = {u: [jnp.zeros((1, LANES), f32)] for u in units}
    for _ in range(CHUNK):
        for u in units:
            lr, li = lam[u]
            pr[u].append(pr[u][-1] * lr - pi[u][-1] * li)
            pi[u].append(pr[u][-2] * li + pi[u][-1] * lr)

    lag_tabs, kws = {}, {}
    for gi in range(PREP_GB):
        ws_parts, coef_rows = [], []
        for d in range(2):
            u = (gi, d)
            fr, fi = zoh[u]
            btr = dup(jnp.transpose(bre_ref[d, gi]))
            bti = dup(jnp.transpose(bim_ref[d, gi]))
            bbr = fr * btr - fi * bti
            bbi = fr * bti + fi * btr
            cr = dup(cre_ref[d, gi])
            ci = dup(cim_ref[d, gi])
            pa = [jnp.where(lo1, r, i) for r, i in zip(pr[u], pi[u])]
            pb = [jnp.where(lo1, -i, r) for r, i in zip(pr[u], pi[u])]
            pc = [jnp.where(lo1, i, r) for r, i in zip(pr[u], pi[u])]
            pd = [jnp.where(lo1, r, -i) for r, i in zip(pr[u], pi[u])]

            order = range(CHUNK + 1) if d == 0 else range(CHUNK, -1, -1)
            blocks = [cr * pa[t] + ci * pb[t] for t in order]
            blocks.append(jnp.zeros((LAG_ROWS - (CHUNK + 1) * SSM_CH, LANES), f32))
            lag_tabs[u] = jnp.concatenate(blocks, axis=0)
            lhs = jnp.where(lo16, bbr, -bbi)
            kws[u] = lax.dot_general(lhs, lag_tabs[u], (((1,), (1,)), ((), ())),
                                     precision=HIGHEST, preferred_element_type=f32)

            parts = []
            for j in range(CHUNK):
                t = CHUNK - 1 - j if d == 0 else j
                parts.append(jnp.concatenate([bbr * pa[t] + bbi * pb[t],
                                              bbr * pc[t] + bbi * pd[t]], axis=1))
            ws_parts.append(jnp.concatenate(parts, axis=0))

            a2 = jnp.where(lo1, -pi[u][CHUNK], pi[u][CHUNK])
            coef_rows += [pr[u][CHUNK], a2, -a2]

        ws_ref[gi] = jnp.concatenate(ws_parts, axis=1).astype(ws_ref.dtype)
        coef_ref[gi] = jnp.concatenate(coef_rows + [jnp.zeros((2, LANES), f32)], axis=0)

    sign = jnp.where(lax.broadcasted_iota(jnp.int32, (LANES, CHUNK_VEC), 0) < SSM_STATE, 1.0, -1.0)
    sub = lax.broadcasted_iota(jnp.int32, (SSM_CH, CHUNK_VEC), 0)
    lane = lax.broadcasted_iota(jnp.int32, (SSM_CH, CHUNK_VEC), 1)
    for gi in range(PREP_GB):
        wo_f = jnp.transpose(lag_tabs[gi, 0][SSM_CH:SSM_CH + CHUNK_VEC, :]) * sign
        wo_b = jnp.transpose(lag_tabs[gi, 1][:CHUNK_VEC, :]) * sign
        wo_ref[gi] = jnp.concatenate([wo_f, wo_b], axis=0).astype(wo_ref.dtype)

        g = group(gi)
        dblk = d_ref[:, pl.ds(pl.multiple_of((g // GROUPS_PER_LANE_BLOCK) * LANES, LANES), LANES)]
        dblk = pltpu.roll(dblk, (LANES - (g % GROUPS_PER_LANE_BLOCK) * SSM_CH) % LANES, 1)
        dblk = jnp.where(lax.broadcasted_iota(jnp.int32, (1, LANES), 1) < SSM_CH, dblk, 0.0)
        for shift in (SSM_CH, 2 * SSM_CH, 4 * SSM_CH):
            dblk = dblk + pltpu.roll(dblk, shift, 1)
        dtile = jnp.broadcast_to(jnp.concatenate([dblk, dblk], axis=1), (SSM_CH, CHUNK_VEC))
        for jp in range(CHUNK):
            fwd = kws[gi, 0] if jp == 0 else pltpu.roll(kws[gi, 0], SSM_CH * jp, 1)
            bwd = pltpu.roll(kws[gi, 1], (LAG_ROWS - SSM_CH * (CHUNK - jp)) % LAG_ROWS, 1)
            skip = jnp.where((lane // SSM_CH == jp) & (lane % SSM_CH == sub), dtile, 0.0)
            blk = fwd[:, :CHUNK_VEC] + bwd[:, :CHUNK_VEC] + skip
            m_ref[gi, jp * SSM_CH:(jp + 1) * SSM_CH, :] = blk.astype(m_ref.dtype)


def _rope_tables():
    half = HEAD_DIM // 2
    inv_freq = ROPE_THETA ** (-np.arange(half, dtype=np.float64) / half)
    ang = np.arange(SEQ, dtype=np.float64)[:, None] * inv_freq[None, :]
    cos, sin = np.cos(ang), np.sin(ang)
    return jnp.asarray(np.concatenate([cos, cos, cos, cos, -sin, sin, -sin, sin], axis=1), jnp.float32)


def _rope_block(xb, cos_t, sin_t, first_half):
    swapped = jnp.where(first_half, pltpu.roll(xb, 96, 1), pltpu.roll(xb, 32, 1))
    return xb * cos_t + swapped * sin_t


COL_Q = (0, Q_COLS)
COL_KV = (COL_Q[1], COL_Q[1] + 2 * KV_COLS)
COL_ZA = (COL_KV[1], COL_KV[1] + D_ATTN)
COL_U = (COL_ZA[1], COL_ZA[1] + D_SSM)
COL_ZS = (COL_U[1], COL_U[1] + D_SSM)


def _silu(z):
    h = 0.5 * z
    return h + h * jnp.tanh(h)


def _inproj_kernel(x_ref, w32_ref, rope_ref, ldt_ref, are_ref, aim_ref, bre_ref, bim_ref, cre_ref, cim_ref, d_ref,
                   q_ref, kv_ref, sz_ref, ut_ref, m_ref, ws_ref, wo_ref, coef_ref, uslab_ref, w_ref):
    step = pl.program_id(0)
    part = step % STEPS_PER_REGROUP
    lane = lax.broadcasted_iota(jnp.int32, (SUB_ROWS, LANES), 1)
    first_half = (lane % HEAD_DIM) < (HEAD_DIM // 2)

    @pl.when(step == 0)
    def _():
        for c0 in range(0, w_ref.shape[1], 2 * LANES):
            w_ref[:, c0:c0 + 2 * LANES] = w32_ref[:, c0:c0 + 2 * LANES].astype(w_ref.dtype)

    def proj(xb, cols):
        return jnp.dot(xb, w_ref[:, cols[0]:cols[1]], preferred_element_type=jnp.float32)

    for sub in range(TOK_BLOCK // SUB_ROWS):
        rows = slice(sub * SUB_ROWS, (sub + 1) * SUB_ROWS)
        xb = x_ref[rows, :].astype(jnp.bfloat16)
        cos_t = rope_ref[rows, :LANES]
        sin_t = rope_ref[rows, LANES:]
        q = proj(xb, COL_Q)
        kv = proj(xb, COL_KV)
        q_blocks = [_rope_block(q[:, v * LANES:(v + 1) * LANES], cos_t, sin_t, first_half)
                    for v in range(Q_COLS // LANES)]
        q_ref[rows, :] = (jnp.concatenate(q_blocks, axis=1) * Q_SCALE).astype(q_ref.dtype)
        za = proj(xb, COL_ZA)
        k_rot = _rope_block(kv[:, :LANES], cos_t, sin_t, first_half)
        kv_ref[rows, :] = jnp.concatenate([k_rot, kv[:, LANES:]], axis=1).astype(kv_ref.dtype)
        u = proj(xb, COL_U)
        sz_ref[rows, :D_ATTN] = _silu(za).astype(sz_ref.dtype)
        zs = proj(xb, COL_ZS)
        for kk in range(SUB_ROWS // CHUNK):
            base = pl.multiple_of((part * (TOK_BLOCK // CHUNK) + sub * (SUB_ROWS // CHUNK)) * SLAB_CHUNK_PITCH, 8)
            for v in range(N_LANE_BLOCKS):
                uslab_ref[v, pl.ds(base + kk * SLAB_CHUNK_PITCH, CHUNK), :] = (
                    u[kk * CHUNK:(kk + 1) * CHUNK, v * LANES:(v + 1) * LANES])
        sz_ref[rows, D_ATTN:] = _silu(zs).astype(sz_ref.dtype)
        if sub == 0:
            _ssm_tables(ldt_ref, are_ref, aim_ref, bre_ref, bim_ref, cre_ref, cim_ref, d_ref,
                        m_ref, ws_ref, wo_ref, coef_ref)

    @pl.when(part == STEPS_PER_REGROUP - 1)
    def _():
        for j in range(CHUNK):
            for v in range(N_LANE_BLOCKS):
                uj = uslab_ref[v, pl.ds(j, LANES, stride=SLAB_CHUNK_PITCH), :]
                ujt = jnp.transpose(uj.astype(ut_ref.dtype)).reshape(GROUPS_PER_LANE_BLOCK, SSM_CH, LANES)
                g0 = v * GROUPS_PER_LANE_BLOCK
                ut_ref[g0:g0 + GROUPS_PER_LANE_BLOCK, j * SSM_CH:(j + 1) * SSM_CH, :] = ujt


def _inproj(x2, w, rope, log_dt, a_re, a_im, b_re, b_im, c_re, c_im, d):
    n_steps = BATCH * SEQ // TOK_BLOCK
    blocks_per_seq = SEQ // TOK_BLOCK
    bf16 = jnp.bfloat16

    def tok(cols):
        return pl.BlockSpec((TOK_BLOCK, cols), lambda s: (s, 0))

    def whole(a):
        return pl.BlockSpec(a.shape, lambda s: (0,) * a.ndim)

    def group_in(a):
        return pl.BlockSpec((2, PREP_GB) + a.shape[2:], lambda s: (0, s, 0, 0))

    def group_out(rows, cols):
        return pl.BlockSpec((PREP_GB, rows, cols), lambda s: (s, 0, 0))

    return pl.pallas_call(
        _inproj_kernel,
        grid=(n_steps,),
        in_specs=[tok(D_MODEL), pl.BlockSpec(w.shape, lambda s: (0, 0), pipeline_mode=pl.Buffered(1)),
                  pl.BlockSpec((TOK_BLOCK, 2 * LANES), lambda s: (s % blocks_per_seq, 0)),
                  pl.BlockSpec(memory_space=pltpu.SMEM), whole(a_re), whole(a_im),
                  group_in(b_re), group_in(b_im), group_in(c_re), group_in(c_im), whole(d)],
        out_specs=[tok(Q_COLS), tok(2 * KV_COLS), tok(D_ATTN + D_SSM),
                   pl.BlockSpec((None, N_GROUPS, CHUNK_VEC, LANES), lambda s: (s // STEPS_PER_REGROUP, 0, 0, 0)),
                   group_out(CHUNK_VEC, CHUNK_VEC), group_out(CHUNK_VEC, 4 * LANES),
                   group_out(2 * STATE_VEC, CHUNK_VEC), group_out(8, LANES)],
        out_shape=[jax.ShapeDtypeStruct((BATCH * SEQ, Q_COLS), bf16),
                   jax.ShapeDtypeStruct((BATCH * SEQ, 2 * KV_COLS), bf16),
                   jax.ShapeDtypeStruct((BATCH * SEQ, D_ATTN + D_SSM), bf16),
                   jax.ShapeDtypeStruct((N_ROW_BLOCKS, N_GROUPS, CHUNK_VEC, LANES), bf16),
                   jax.ShapeDtypeStruct((N_GROUPS, CHUNK_VEC, CHUNK_VEC), bf16),
                   jax.ShapeDtypeStruct((N_GROUPS, CHUNK_VEC, 4 * LANES), bf16),
                   jax.ShapeDtypeStruct((N_GROUPS, 2 * STATE_VEC, CHUNK_VEC), bf16),
                   jax.ShapeDtypeStruct((N_GROUPS, 8, LANES), jnp.float32)],
        scratch_shapes=[pltpu.VMEM((N_LANE_BLOCKS, LANES * SLAB_CHUNK_PITCH, LANES), jnp.float32),
                        pltpu.VMEM(w.shape, bf16)],
        compiler_params=pltpu.CompilerParams(dimension_semantics=("arbitrary",),
                                             vmem_limit_bytes=VMEM_LIMIT),
        name="inproj",
    )(x2, w, rope, log_dt, a_re, a_im, b_re, b_im, c_re, c_im, d)


ATTN_SUB = 32
ATTN_STEP = ATTN_SUB * BLOCK
N_KEYS = 3 * BLOCK


def _attn_kernel(q_ref, kp_ref, kc_ref, kn_ref, sink_ref, y_ref):
    i = pl.program_id(1)
    kvw = jnp.concatenate([kp_ref[...], kc_ref[...], kn_ref[...]], axis=0).astype(jnp.float32)

    n_win = kvw.shape[0]
    lane_w = lax.broadcasted_iota(jnp.int32, (n_win, LANES), 1)
    k_lo, k_hi, vt_lo, vt_hi = [], [], [], []
    for hk in range(N_KV_HEADS):
        in_head = (lane_w >= hk * HEAD_DIM) & (lane_w < (hk + 1) * HEAD_DIM)
        k_nat = jnp.where(in_head, kvw[:, :LANES], 0.0)
        v_nat = jnp.where(in_head, kvw[:, LANES:], 0.0)
        k_oth = pltpu.roll(k_nat, HEAD_DIM, 1)
        v_oth = pltpu.roll(v_nat, HEAD_DIM, 1)
        pairs = ((k_nat, k_oth), (v_nat, v_oth)) if hk == 0 else ((k_oth, k_nat), (v_oth, v_nat))
        k_lo.append(pairs[0][0].astype(jnp.bfloat16))
        k_hi.append(pairs[0][1].astype(jnp.bfloat16))
        vt_lo.append(jnp.transpose(pairs[1][0]).astype(jnp.bfloat16))
        vt_hi.append(jnp.transpose(pairs[1][1]).astype(jnp.bfloat16))

    c = lax.broadcasted_iota(jnp.int32, (N_KEYS, BLOCK), 0)
    r = lax.broadcasted_iota(jnp.int32, (N_KEYS, BLOCK), 1)
    band = (c >= r) & (c - r <= 2 * WINDOW)
    ones_row = lax.broadcasted_iota(jnp.int32, (16, 2 * N_KEYS), 0)
    ones_col = lax.broadcasted_iota(jnp.int32, (16, 2 * N_KEYS), 1)
    ones = jnp.where(((ones_row == 0) & (ones_col < N_KEYS)) | ((ones_row == 1) & (ones_col >= N_KEYS)),
                     1.0, 0.0).astype(jnp.bfloat16)
    row_o = lax.broadcasted_iota(jnp.int32, (LANES, 2 * BLOCK), 0)
    n_blocks = SEQ // BLOCK

    def scores(t, hk):
        blk = i * ATTN_SUB + t
        lo = jnp.where(blk == 0, BLOCK, 0)
        hi = jnp.where(blk == n_blocks - 1, 2 * BLOCK, N_KEYS)
        valid1 = band & (c >= lo) & (c < hi)
        valid = jnp.concatenate([valid1, valid1], axis=1)
        win = slice(t * BLOCK, t * BLOCK + N_KEYS)
        q = q_ref[t * BLOCK:(t + 1) * BLOCK, (2 * hk) * LANES:(2 * hk + 2) * LANES]
        qq = jnp.concatenate([q[:, :LANES], q[:, LANES:]], axis=0)
        kcat = jnp.concatenate([k_lo[hk][win], k_hi[hk][win]], axis=0)
        st = lax.dot_general(kcat, qq, (((1,), (1,)), ((), ())),
                             preferred_element_type=jnp.float32)
        sps, ms, sinks = [], [], []
        for par in range(2):
            base = par * N_KEYS
            sp = jnp.concatenate(
                [jnp.where(valid[:BLOCK], st[base:base + BLOCK, :], NEG_INF),
                 st[base + BLOCK:base + 2 * BLOCK, :],
                 jnp.where(valid[2 * BLOCK:], st[base + 2 * BLOCK:base + N_KEYS, :], NEG_INF)], axis=0)
            h0 = 4 * hk + par
            h1 = h0 + 2
            sink = jnp.concatenate([jnp.full((1, BLOCK), sink_ref[h] * LOG2_E, jnp.float32)
                                    for h in (h0, h1)], axis=1)
            sps.append(sp)
            ms.append(jnp.maximum(jnp.max(sp, axis=0, keepdims=True), sink))
            sinks.append(sink)
        return sps, ms, sinks

    def weights(sps, ms, sinks):
        return [jnp.exp2((sp - m).astype(jnp.bfloat16)) for sp, m in zip(sps, ms)], ms, sinks

    def finish(t, hk, ps, ms, sinks):
        win = slice(t * BLOCK, t * BLOCK + N_KEYS)
        vt = jnp.concatenate([jnp.concatenate([vt_lo[hk][:, win], vt_hi[hk][:, win]], axis=1), ones], axis=0)
        ot = jnp.dot(vt, jnp.concatenate(ps, axis=0), preferred_element_type=jnp.float32)
        inv0 = 1.0 / (ot[LANES:LANES + 1, :] + jnp.exp2(sinks[0] - ms[0]))
        inv1 = 1.0 / (ot[LANES + 1:LANES + 2, :] + jnp.exp2(sinks[1] - ms[1]))
        o = jnp.transpose(ot[:LANES, :] * jnp.where(row_o < HEAD_DIM, inv0, inv1))
        rows = slice(t * BLOCK, (t + 1) * BLOCK)
        y_ref[rows, (2 * hk) * LANES:(2 * hk + 1) * LANES] = o[:BLOCK].astype(y_ref.dtype)
        y_ref[rows, (2 * hk + 1) * LANES:(2 * hk + 2) * LANES] = o[BLOCK:].astype(y_ref.dtype)

    work = [(t, hk) for t in range(ATTN_SUB) for hk in range(N_KV_HEADS)]
    stage_a = {0: scores(*work[0])}
    stage_b = {}
    for n in range(len(work) + 1):
        if n + 1 < len(work):
            stage_a[n + 1] = scores(*work[n + 1])
        if n < len(work):
            stage_b[n] = weights(*stage_a.pop(n))
        if n >= 1:
            finish(*work[n - 1], *stage_b.pop(n - 1))


def _attention(q, kv, sink):
    nb = SEQ // BLOCK
    return pl.pallas_call(
        _attn_kernel,
        grid=(BATCH, SEQ // ATTN_STEP),
        in_specs=[pl.BlockSpec((None, ATTN_STEP, Q_COLS), lambda b, i: (b, i, 0)),
                  pl.BlockSpec((None, BLOCK, 2 * KV_COLS),
                               lambda b, i: (b, jnp.maximum(i * ATTN_SUB - 1, 0), 0)),
                  pl.BlockSpec((None, ATTN_STEP, 2 * KV_COLS), lambda b, i: (b, i, 0)),
                  pl.BlockSpec((None, BLOCK, 2 * KV_COLS),
                               lambda b, i: (b, jnp.minimum((i + 1) * ATTN_SUB, nb - 1), 0)),
                  pl.BlockSpec(memory_space=pltpu.SMEM)],
        out_specs=pl.BlockSpec((None, ATTN_STEP, D_ATTN), lambda b, i: (b, i, 0)),
        out_shape=jax.ShapeDtypeStruct((BATCH, SEQ, D_ATTN), jnp.bfloat16),
        compiler_params=pltpu.CompilerParams(dimension_semantics=("arbitrary", "arbitrary"),
                                             vmem_limit_bytes=VMEM_LIMIT),
        name="attention",
    )(q, kv, kv, kv, sink)


SSM_GB = 4


def _ssm_kernel(ut_ref, m_ref, ws_ref, wo_ref, coef_ref, gt_ref,
                sf_ref, sfs_ref, sb_ref, sbs_ref, xf_ref, xb_ref, u_ref):
    for gi in range(SSM_GB):
        ut = jnp.concatenate([ut_ref[lb, gi] for lb in range(N_ROW_BLOCKS)], axis=1)
        u = jnp.transpose(ut)
        u_ref[gi] = u
        s = jnp.dot(u, ws_ref[gi], preferred_element_type=jnp.float32)
        for b in range(BATCH):
            rows = slice(b * N_CHUNKS, (b + 1) * N_CHUNKS)
            dst = pl.ds(b * SLAB_PITCH + 8, N_CHUNKS)
            sf_ref[gi, dst, :] = s[rows, 0 * LANES:1 * LANES]
            sfs_ref[gi, dst, :] = s[rows, 1 * LANES:2 * LANES]
            sb_ref[gi, dst, :] = s[rows, 2 * LANES:3 * LANES]
            sbs_ref[gi, dst, :] = s[rows, 3 * LANES:4 * LANES]

    zero = jnp.zeros((BATCH, LANES), jnp.float32)
    coefs = []
    for gi in range(SSM_GB):
        coefs.append([jnp.broadcast_to(coef_ref[gi, r:r + 1, :], (BATCH, LANES)) for r in range(6)])
        xf_ref[gi, pl.ds(8, BATCH, stride=SLAB_PITCH), :] = zero
        xb_ref[gi, pl.ds(8 + N_CHUNKS - 1, BATCH, stride=SLAB_PITCH), :] = zero

    def rows_at(r):
        return pl.ds(r, BATCH, stride=SLAB_PITCH)

    def step(k, carry):
        kb = N_CHUNKS - 1 - k
        out = []
        for gi in range(SSM_GB):
            a1f, a2f, a3f, a1b, a2b, a3b = coefs[gi]
            xf, xfs, xb, xbs = carry[4 * gi:4 * gi + 4]
            nxf = a1f * xf + a2f * xfs + sf_ref[gi, rows_at(8 + k), :]
            nxfs = a1f * xfs + a3f * xf + sfs_ref[gi, rows_at(8 + k), :]
            xf_ref[gi, rows_at(9 + k), :] = nxf
            nxb = a1b * xb + a2b * xbs + sb_ref[gi, rows_at(8 + kb), :]
            nxbs = a1b * xbs + a3b * xb + sbs_ref[gi, rows_at(8 + kb), :]
            xb_ref[gi, rows_at(7 + kb), :] = nxb
            out += [nxf, nxfs, nxb, nxbs]
        return tuple(out)

    lax.fori_loop(0, N_CHUNKS, step, (zero,) * (4 * SSM_GB), unroll=2)

    def chunk_out(gi):
        xin_f = jnp.concatenate([xf_ref[gi, pl.ds(b * SLAB_PITCH + 8, N_CHUNKS), :] for b in range(BATCH)], axis=0)
        xin_b = jnp.concatenate([xb_ref[gi, pl.ds(b * SLAB_PITCH + 8, N_CHUNKS), :] for b in range(BATCH)], axis=0)
        lhs = jnp.concatenate([u_ref[gi], xin_f.astype(jnp.bfloat16), xin_b.astype(jnp.bfloat16)], axis=1)
        rhs = jnp.concatenate([m_ref[gi], wo_ref[gi]], axis=0)
        return jnp.dot(lhs, rhs, preferred_element_type=jnp.float32)

    y = chunk_out(0)
    for gi in range(SSM_GB):
        y_next = chunk_out(gi + 1) if gi + 1 < SSM_GB else None
        gt = jnp.transpose(jax.nn.gelu(y).astype(gt_ref.dtype))
        for lb in range(N_ROW_BLOCKS):
            gt_ref[lb, gi] = gt[:, lb * LANES:(lb + 1) * LANES]
        y = y_next


def _ssm(ut, m_intra, w_state, w_out, coef):
    def per_group(shape):
        return pl.BlockSpec((SSM_GB,) + shape, lambda g: (g, 0, 0))

    lane_blocked = pl.BlockSpec((N_ROW_BLOCKS, SSM_GB, CHUNK_VEC, LANES), lambda g: (0, g, 0, 0))
    slab = pltpu.VMEM((SSM_GB, SLAB_ROWS, LANES), jnp.float32)
    return pl.pallas_call(
        _ssm_kernel,
        grid=(N_GROUPS // SSM_GB,),
        in_specs=[lane_blocked, per_group((CHUNK_VEC, CHUNK_VEC)),
                  per_group((CHUNK_VEC, 4 * LANES)), per_group((2 * STATE_VEC, CHUNK_VEC)),
                  per_group((8, LANES))],
        out_specs=lane_blocked,
        out_shape=jax.ShapeDtypeStruct((N_ROW_BLOCKS, N_GROUPS, CHUNK_VEC, LANES), jnp.bfloat16),
        scratch_shapes=[slab] * 6 + [pltpu.VMEM((SSM_GB, ROWS, CHUNK_VEC), jnp.bfloat16)],
        compiler_params=pltpu.CompilerParams(dimension_semantics=("arbitrary",),
                                             vmem_limit_bytes=VMEM_LIMIT),
        name="ssm",
    )(ut, m_intra, w_state, w_out, coef)


FINAL_SUB_ROWS = 256


def _unit_rms(v):
    return v * lax.rsqrt(jnp.mean(v * v, axis=-1, keepdims=True) + NORM_EPS)


def _final_kernel(x_ref, ya_ref, sz_ref, gt_ref, wglu32_ref, bglu_ref, wo32_ref, ga_ref, gs_ref,
                  lng_ref, lnb_ref, o_ref, gslab_ref, wglu_ref, wo_ref):
    step = pl.program_id(0)
    part = step % STEPS_PER_REGROUP

    @pl.when(step == 0)
    def _():
        wglu_ref[...] = wglu32_ref[...].astype(wglu_ref.dtype)
        for gain_ref, row0 in ((ga_ref, 0), (gs_ref, D_ATTN)):
            for v in range(gain_ref.shape[1] // LANES):
                gain = gain_ref[:, v * LANES:(v + 1) * LANES] / DEEPNORM_ALPHA
                col = jnp.transpose(jnp.broadcast_to(gain, (LANES, LANES)))
                rows = slice(row0 + v * LANES, row0 + (v + 1) * LANES)
                for c0 in range(0, D_MODEL, LANES):
                    wo_ref[rows, c0:c0 + LANES] = (wo32_ref[rows, c0:c0 + LANES] * col).astype(wo_ref.dtype)

    @pl.when(part == 0)
    def _():
        for j in range(CHUNK):
            for v in range(N_LANE_BLOCKS):
                g0 = v * GROUPS_PER_LANE_BLOCK
                gjt = gt_ref[g0:g0 + GROUPS_PER_LANE_BLOCK, j * SSM_CH:(j + 1) * SSM_CH, :]
                gj = jnp.transpose(gjt.reshape(LANES, LANES))
                gslab_ref[v, pl.ds(j, LANES, stride=SLAB_CHUNK_PITCH), :] = gj.astype(jnp.float32)

    def rows_of(sub):
        return slice(sub * FINAL_SUB_ROWS, (sub + 1) * FINAL_SUB_ROWS)

    def gate_in(sub):
        rows = rows_of(sub)
        ya = ya_ref[rows, :].astype(jnp.float32) * sz_ref[rows, :D_ATTN].astype(jnp.float32)
        ya = _unit_rms(ya).astype(jnp.bfloat16)
        base = pl.multiple_of((part * (TOK_BLOCK // CHUNK) + sub * (FINAL_SUB_ROWS // CHUNK)) * SLAB_CHUNK_PITCH, 8)
        g = jnp.concatenate(
            [jnp.concatenate([gslab_ref[v, pl.ds(base + kk * SLAB_CHUNK_PITCH, CHUNK), :]
                              for kk in range(FINAL_SUB_ROWS // CHUNK)], axis=0)
             for v in range(N_LANE_BLOCKS)], axis=1)
        return ya, g

    def glu(sub, ya, g):
        rows = rows_of(sub)
        gate = jnp.dot(g.astype(jnp.bfloat16), wglu_ref[...], preferred_element_type=jnp.float32) + bglu_ref[...]
        ys = g * jax.nn.sigmoid(gate)
        ys = ys * sz_ref[rows, D_ATTN:].astype(jnp.float32)
        return ya, _unit_rms(ys).astype(jnp.bfloat16)

    def project(sub, ya, ys):
        return (jnp.dot(ya, wo_ref[:D_ATTN, :], preferred_element_type=jnp.float32)
                + jnp.dot(ys, wo_ref[D_ATTN:, :], preferred_element_type=jnp.float32))

    def layer_norm(sub, out):
        rows = rows_of(sub)
        h = x_ref[rows, :] + out
        mu = jnp.mean(h, axis=-1, keepdims=True)
        hc = h - mu
        var = jnp.mean(hc * hc, axis=-1, keepdims=True)
        o_ref[rows, :] = hc * lax.rsqrt(var + NORM_EPS / DEEPNORM_ALPHA ** 2) * lng_ref[...] + lnb_ref[...]

    n_sub = TOK_BLOCK // FINAL_SUB_ROWS
    stage_a, stage_b = {}, {}
    for n in range(n_sub + 2):
        if n < n_sub:
            stage_a[n] = gate_in(n)
        if 1 <= n <= n_sub:
            stage_b[n - 1] = glu(n - 1, *stage_a.pop(n - 1))
        if n >= 2:
            layer_norm(n - 2, project(n - 2, *stage_b.pop(n - 2)))


def _final(x2, ya, sz, gt, wglu, bglu, wo, gain_a, gain_s, lng, lnb):
    n_steps = BATCH * SEQ // TOK_BLOCK

    def tok(cols):
        return pl.BlockSpec((TOK_BLOCK, cols), lambda s: (s, 0))

    def whole(a, **kw):
        return pl.BlockSpec(a.shape, lambda s: (0,) * a.ndim, **kw)

    return pl.pallas_call(
        _final_kernel,
        grid=(n_steps,),
        in_specs=[tok(D_MODEL), tok(D_ATTN), tok(D_ATTN + D_SSM),
                  pl.BlockSpec((None, N_GROUPS, CHUNK_VEC, LANES), lambda s: (s // STEPS_PER_REGROUP, 0, 0, 0)),
                  whole(wglu, pipeline_mode=pl.Buffered(1)), whole(bglu),
                  whole(wo, pipeline_mode=pl.Buffered(1)), whole(gain_a), whole(gain_s), whole(lng), whole(lnb)],
        out_specs=tok(D_MODEL),
        out_shape=jax.ShapeDtypeStruct((BATCH * SEQ, D_MODEL), jnp.float32),
        scratch_shapes=[pltpu.VMEM((N_LANE_BLOCKS, LANES * SLAB_CHUNK_PITCH, LANES), jnp.float32),
                        pltpu.VMEM(wglu.shape, jnp.bfloat16), pltpu.VMEM(wo.shape, jnp.bfloat16)],
        compiler_params=pltpu.CompilerParams(dimension_semantics=("arbitrary",),
                                             vmem_limit_bytes=VMEM_LIMIT),
        name="final",
    )(x2, ya, sz, gt, wglu, bglu, wo, gain_a, gain_s, lng, lnb)


def kernel(x, w_in, attn_sink, ssm_a_re, ssm_a_im, ssm_log_dt, ssm_b_re, ssm_b_im, ssm_c_re, ssm_c_im,
           ssm_d, w_glu, b_glu, norm_attn_g, norm_ssm_g, w_out, ln_g, ln_b):
    assert x.shape == (BATCH, SEQ, D_MODEL) and w_in.shape[0] == 1
    f32 = jnp.float32
    rope = _rope_tables()

    x2 = x.reshape(BATCH * SEQ, D_MODEL)
    params = [p[0].astype(f32) for p in (ssm_log_dt, ssm_a_re, ssm_a_im, ssm_b_re, ssm_b_im, ssm_c_re, ssm_c_im)]
    q, kv, sz, ut, m_intra, w_state, w_so, coef = _inproj(x2, w_in[0], rope, *params,
                                                           ssm_d[0].astype(f32).reshape(1, D_SSM))
    ya = _attention(q.reshape(BATCH, SEQ, Q_COLS), kv.reshape(BATCH, SEQ, 2 * KV_COLS),
                    attn_sink[0].astype(f32))
    gt = _ssm(ut, m_intra, w_state, w_so, coef)

    row = lambda v: v[0].astype(f32)[None, :]
    out = _final(x2, ya.reshape(BATCH * SEQ, D_ATTN), sz, gt, w_glu[0].astype(f32), row(b_glu),
                 w_out[0].astype(f32), row(norm_attn_g), row(norm_ssm_g), row(ln_g), row(ln_b))
    return out.reshape(BATCH, SEQ, D_MODEL)
```

```python
import jax
import jax.numpy as jnp
import numpy as np
from jax import lax
from jax.experimental import pallas as pl
from jax.experimental.pallas import tpu as pltpu

D_MODEL = 1024
BATCH = 4
SEQ = 4096
D_ATTN = 512
D_SSM = 512
HEAD_DIM = 64
N_Q_HEADS = 8
N_KV_HEADS = 2
WINDOW = 128
BLOCK = 128
ROPE_THETA = 10000.0
SSM_CH = 16
N_GROUPS = 32
SSM_STATE = 64
NORM_EPS = 1e-5
NEG_INF = -1e30
DEEPNORM_ALPHA = 2.0 ** 0.25
LOG2_E = 1.4426950408889634
Q_SCALE = HEAD_DIM ** -0.5 * LOG2_E

Q_COLS = N_Q_HEADS * HEAD_DIM
KV_COLS = N_KV_HEADS * HEAD_DIM
CHUNK = 16
N_CHUNKS = SEQ // CHUNK
CHUNK_VEC = CHUNK * SSM_CH
STATE_VEC = 2 * SSM_STATE
LANES = 128
ROWS = BATCH * N_CHUNKS
N_ROW_BLOCKS = ROWS // LANES
SLAB_PITCH = N_CHUNKS + 8
SLAB_ROWS = BATCH * SLAB_PITCH + 8
TOK_BLOCK = 1024
REGROUP_TOK = LANES * CHUNK
STEPS_PER_REGROUP = REGROUP_TOK // TOK_BLOCK
SUB_ROWS = 512
N_LANE_BLOCKS = D_SSM // LANES
SLAB_CHUNK_PITCH = 20
GROUPS_PER_LANE_BLOCK = LANES // SSM_CH
VMEM_LIMIT = 56 * 1024 * 1024
HIGHEST = lax.Precision.HIGHEST


LAG_ROWS = 512
PREP_GB = N_GROUPS // (BATCH * SEQ // TOK_BLOCK)


def _ssm_tables(ldt_ref, are_ref, aim_ref, bre_ref, bim_ref, cre_ref, cim_ref, d_ref,
                m_ref, ws_ref, wo_ref, coef_ref):
    f32 = jnp.float32
    lo16 = lax.broadcasted_iota(jnp.int32, (SSM_CH, LANES), 1) < SSM_STATE
    lo1 = lax.broadcasted_iota(jnp.int32, (1, LANES), 1) < SSM_STATE
    units = [(gi, d) for gi in range(PREP_GB) for d in range(2)]

    def group(gi):
        return pl.program_id(0) * PREP_GB + gi

    def dup(v):
        return jnp.concatenate([v, v], axis=1)

    lam, zoh = {}, {}
    for gi, d in units:
        ar = dup(are_ref[d, pl.ds(group(gi), 1), :])
        ai = dup(aim_ref[d, pl.ds(group(gi), 1), :])
        dt = jnp.exp(jnp.full((1, LANES), ldt_ref[d, group(gi)], f32))
        zr = dt * ar
        zi = dt * ai
        mag = jnp.exp(zr)
        lr = mag * jnp.cos(zi)
        li = mag * jnp.sin(zi)
        den = ar * ar + ai * ai
        lam[gi, d] = (lr, li)
        zoh[gi, d] = (((lr - 1.0) * ar + li * ai) / den, (li * ar - (lr - 1.0) * ai) / den)

    pr = {u: [jnp.ones((1, LANES), f32)] for u in units}
    pi = {u: [jnp.zeros((1, LANES), f32)] for u in units}
    for _ in range(CHUNK):
        for u in units:
            lr, li = lam[u]
            pr[u].append(pr[u][-1] * lr - pi[u][-1] * li)
            pi[u].append(pr[u][-2] * li + pi[u][-1] * lr)

    lag_tabs, kws = {}, {}
    for gi in range(PREP_GB):
        ws_parts, coef_rows = [], []
        for d in range(2):
            u = (gi, d)
            fr, fi = zoh[u]
            btr = dup(bre_ref[d, gi])
            bti = dup(bim_ref[d, gi])
            bbr = fr * btr - fi * bti
            bbi = fr * bti + fi * btr
            cr = dup(cre_ref[d, gi])
            ci = dup(cim_ref[d, gi])
            pa = [jnp.where(lo1, r, i) for r, i in zip(pr[u], pi[u])]
            pb = [jnp.where(lo1, -i, r) for r, i in zip(pr[u], pi[u])]
            pc = [jnp.where(lo1, i, r) for r, i in zip(pr[u], pi[u])]
            pd = [jnp.where(lo1, r, -i) for r, i in zip(pr[u], pi[u])]

            order = range(CHUNK + 1) if d == 0 else range(CHUNK, -1, -1)
            blocks = [cr * pa[t] + ci * pb[t] for t in order]
            blocks.append(jnp.zeros((LAG_ROWS - (CHUNK + 1) * SSM_CH, LANES), f32))
            lag_tabs[u] = jnp.concatenate(blocks, axis=0)
            lhs = jnp.where(lo16, bbr, -bbi)
            kws[u] = lax.dot_general(lhs, lag_tabs[u], (((1,), (1,)), ((), ())),
                                     precision=HIGHEST, preferred_element_type=f32)

            parts = []
            for j in range(CHUNK):
                t = CHUNK - 1 - j if d == 0 else j
                parts.append(jnp.concatenate([bbr * pa[t] + bbi * pb[t],
                                              bbr * pc[t] + bbi * pd[t]], axis=1))
            ws_parts.append(jnp.concatenate(parts, axis=0))

            a2 = jnp.where(lo1, -pi[u][CHUNK], pi[u][CHUNK])
            coef_rows += [pr[u][CHUNK], a2, -a2]

        ws_ref[gi] = jnp.concatenate(ws_parts, axis=1).astype(ws_ref.dtype)
        coef_ref[gi] = jnp.concatenate(coef_rows + [jnp.zeros((2, LANES), f32)], axis=0)

    sign = jnp.where(lax.broadcasted_iota(jnp.int32, (LANES, CHUNK_VEC), 0) < SSM_STATE, 1.0, -1.0)
    sub = lax.broadcasted_iota(jnp.int32, (SSM_CH, CHUNK_VEC), 0)
    lane = lax.broadcasted_iota(jnp.int32, (SSM_CH, CHUNK_VEC), 1)
    for gi in range(PREP_GB):
        wo_f = jnp.transpose(lag_tabs[gi, 0][SSM_CH:SSM_CH + CHUNK_VEC, :]) * sign
        wo_b = jnp.transpose(lag_tabs[gi, 1][:CHUNK_VEC, :]) * sign
        wo_ref[gi] = jnp.concatenate([wo_f, wo_b], axis=0).astype(wo_ref.dtype)

        g = group(gi)
        dblk = d_ref[:, pl.ds(pl.multiple_of((g // GROUPS_PER_LANE_BLOCK) * LANES, LANES), LANES)]
        dblk = pltpu.roll(dblk, (LANES - (g % GROUPS_PER_LANE_BLOCK) * SSM_CH) % LANES, 1)
        dblk = jnp.where(lax.broadcasted_iota(jnp.int32, (1, LANES), 1) < SSM_CH, dblk, 0.0)
        for shift in (SSM_CH, 2 * SSM_CH, 4 * SSM_CH):
            dblk = dblk + pltpu.roll(dblk, shift, 1)
        dtile = jnp.broadcast_to(jnp.concatenate([dblk, dblk], axis=1), (SSM_CH, CHUNK_VEC))
        for jp in range(CHUNK):
            fwd = kws[gi, 0] if jp == 0 else pltpu.roll(kws[gi, 0], SSM_CH * jp, 1)
            bwd = pltpu.roll(kws[gi, 1], (LAG_ROWS - SSM_CH * (CHUNK - jp)) % LAG_ROWS, 1)
            skip = jnp.where((lane // SSM_CH == jp) & (lane % SSM_CH == sub), dtile, 0.0)
            blk = fwd[:, :CHUNK_VEC] + bwd[:, :CHUNK_VEC] + skip
            m_ref[gi, jp * SSM_CH:(jp + 1) * SSM_CH, :] = blk.astype(m_ref.dtype)


def _rope_tables():
    half = HEAD_DIM // 2
    inv_freq = ROPE_THETA ** (-np.arange(half, dtype=np.float64) / half)
    ang = np.arange(SEQ, dtype=np.float64)[:, None] * inv_freq[None, :]
    cos, sin = np.cos(ang), np.sin(ang)
    return jnp.asarray(np.concatenate([cos, cos, cos, cos, -sin, sin, -sin, sin], axis=1), jnp.float32)


def _rope_block(xb, cos_t, sin_t, first_half):
    swapped = jnp.where(first_half, pltpu.roll(xb, 96, 1), pltpu.roll(xb, 32, 1))
    return xb * cos_t + swapped * sin_t


COL_Q = (0, Q_COLS)
COL_KV = (COL_Q[1], COL_Q[1] + 2 * KV_COLS)
COL_ZA = (COL_KV[1], COL_KV[1] + D_ATTN)
COL_U = (COL_ZA[1], COL_ZA[1] + D_SSM)
COL_ZS = (COL_U[1], COL_U[1] + D_SSM)


def _silu(z):
    h = 0.5 * z
    return h + h * jnp.tanh(h)


def _inproj_kernel(x_ref, w32_ref, rope_ref, ldt_ref, are_ref, aim_ref, bre_ref, bim_ref, cre_ref, cim_ref, d_ref,
                   q_ref, kv_ref, sz_ref, ut_ref, m_ref, ws_ref, wo_ref, coef_ref, uslab_ref, w_ref):
    step = pl.program_id(0)
    part = step % STEPS_PER_REGROUP
    lane = lax.broadcasted_iota(jnp.int32, (SUB_ROWS, LANES), 1)
    first_half = (lane % HEAD_DIM) < (HEAD_DIM // 2)

    @pl.when(step == 0)
    def _():
        for c0 in range(0, w_ref.shape[1], 2 * LANES):
            w_ref[:, c0:c0 + 2 * LANES] = w32_ref[:, c0:c0 + 2 * LANES].astype(w_ref.dtype)

    def proj(xb, cols):
        return jnp.dot(xb, w_ref[:, cols[0]:cols[1]], preferred_element_type=jnp.float32)

    for sub in range(TOK_BLOCK // SUB_ROWS):
        rows = slice(sub * SUB_ROWS, (sub + 1) * SUB_ROWS)
        xb = x_ref[rows, :].astype(jnp.bfloat16)
        cos_t = rope_ref[rows, :LANES]
        sin_t = rope_ref[rows, LANES:]
        q = proj(xb, COL_Q)
        kv = proj(xb, COL_KV)
        q_blocks = [_rope_block(q[:, v * LANES:(v + 1) * LANES], cos_t, sin_t, first_half)
                    for v in range(Q_COLS // LANES)]
        q_ref[rows, :] = (jnp.concatenate(q_blocks, axis=1) * Q_SCALE).astype(q_ref.dtype)
        za = proj(xb, COL_ZA)
        k_rot = _rope_block(kv[:, :LANES], cos_t, sin_t, first_half)
        kv_ref[rows, :] = jnp.concatenate([k_rot, kv[:, LANES:]], axis=1).astype(kv_ref.dtype)
        u = proj(xb, COL_U)
        sz_ref[rows, :D_ATTN] = _silu(za).astype(sz_ref.dtype)
        zs = proj(xb, COL_ZS)
        for kk in range(SUB_ROWS // CHUNK):
            base = pl.multiple_of((part * (TOK_BLOCK // CHUNK) + sub * (SUB_ROWS // CHUNK)) * SLAB_CHUNK_PITCH, 8)
            for v in range(N_LANE_BLOCKS):
                uslab_ref[v, pl.ds(base + kk * SLAB_CHUNK_PITCH, CHUNK), :] = (
                    u[kk * CHUNK:(kk + 1) * CHUNK, v * LANES:(v + 1) * LANES])
        sz_ref[rows, D_ATTN:] = _silu(zs).astype(sz_ref.dtype)
        if sub == 0:
            _ssm_tables(ldt_ref, are_ref, aim_ref, bre_ref, bim_ref, cre_ref, cim_ref, d_ref,
                        m_ref, ws_ref, wo_ref, coef_ref)

    @pl.when(part == STEPS_PER_REGROUP - 1)
    def _():
        for j in range(CHUNK):
            for v in range(N_LANE_BLOCKS):
                uj = uslab_ref[v, pl.ds(j, LANES, stride=SLAB_CHUNK_PITCH), :]
                ujt = jnp.transpose(uj.astype(ut_ref.dtype)).reshape(GROUPS_PER_LANE_BLOCK, SSM_CH, LANES)
                g0 = v * GROUPS_PER_LANE_BLOCK
                ut_ref[g0:g0 + GROUPS_PER_LANE_BLOCK, j * SSM_CH:(j + 1) * SSM_CH, :] = ujt


def _inproj(x2, w, rope, log_dt, a_re, a_im, b_re, b_im, c_re, c_im, d):
    n_steps = BATCH * SEQ // TOK_BLOCK
    blocks_per_seq = SEQ // TOK_BLOCK
    bf16 = jnp.bfloat16

    def tok(cols):
        return pl.BlockSpec((TOK_BLOCK, cols), lambda s: (s, 0))

    def whole(a):
        return pl.BlockSpec(a.shape, lambda s: (0,) * a.ndim)

    def group_in(a):
        return pl.BlockSpec((2, PREP_GB) + a.shape[2:], lambda s: (0, s, 0, 0))

    def group_out(rows, cols):
        return pl.BlockSpec((PREP_GB, rows, cols), lambda s: (s, 0, 0))

    return pl.pallas_call(
        _inproj_kernel,
        grid=(n_steps,),
        in_specs=[tok(D_MODEL), pl.BlockSpec(w.shape, lambda s: (0, 0), pipeline_mode=pl.Buffered(1)),
                  pl.BlockSpec((TOK_BLOCK, 2 * LANES), lambda s: (s % blocks_per_seq, 0)),
                  pl.BlockSpec(memory_space=pltpu.SMEM), whole(a_re), whole(a_im),
                  group_in(b_re), group_in(b_im), group_in(c_re), group_in(c_im), whole(d)],
        out_specs=[tok(Q_COLS), tok(2 * KV_COLS), tok(D_ATTN + D_SSM),
                   pl.BlockSpec((None, N_GROUPS, CHUNK_VEC, LANES), lambda s: (s // STEPS_PER_REGROUP, 0, 0, 0)),
                   group_out(CHUNK_VEC, CHUNK_VEC), group_out(CHUNK_VEC, 4 * LANES),
                   group_out(2 * STATE_VEC, CHUNK_VEC), group_out(8, LANES)],
        out_shape=[jax.ShapeDtypeStruct((BATCH * SEQ, Q_COLS), bf16),
                   jax.ShapeDtypeStruct((BATCH * SEQ, 2 * KV_COLS), bf16),
                   jax.ShapeDtypeStruct((BATCH * SEQ, D_ATTN + D_SSM), bf16),
                   jax.ShapeDtypeStruct((N_ROW_BLOCKS, N_GROUPS, CHUNK_VEC, LANES), bf16),
                   jax.ShapeDtypeStruct((N_GROUPS, CHUNK_VEC, CHUNK_VEC), bf16),
                   jax.ShapeDtypeStruct((N_GROUPS, CHUNK_VEC, 4 * LANES), bf16),
                   jax.ShapeDtypeStruct((N_GROUPS, 2 * STATE_VEC, CHUNK_VEC), bf16),
                   jax.ShapeDtypeStruct((N_GROUPS, 8, LANES), jnp.float32)],
        scratch_shapes=[pltpu.VMEM((N_LANE_BLOCKS, LANES * SLAB_CHUNK_PITCH, LANES), jnp.float32),
                        pltpu.VMEM(w.shape, bf16)],
        compiler_params=pltpu.CompilerParams(dimension_semantics=("arbitrary",),
                                             vmem_limit_bytes=VMEM_LIMIT),
        name="inproj",
    )(x2, w, rope, log_dt, a_re, a_im, b_re, b_im, c_re, c_im, d)


ATTN_SUB = 32
ATTN_STEP = ATTN_SUB * BLOCK
N_KEYS = 3 * BLOCK


def _attn_kernel(q_ref, kp_ref, kc_ref, kn_ref, sink_ref, y_ref):
    i = pl.program_id(1)
    kvw = jnp.concatenate([kp_ref[...], kc_ref[...], kn_ref[...]], axis=0).astype(jnp.float32)

    n_win = kvw.shape[0]
    lane_w = lax.broadcasted_iota(jnp.int32, (n_win, LANES), 1)
    k_lo, k_hi, vt_lo, vt_hi = [], [], [], []
    for hk in range(N_KV_HEADS):
        in_head = (lane_w >= hk * HEAD_DIM) & (lane_w < (hk + 1) * HEAD_DIM)
        k_nat = jnp.where(in_head, kvw[:, :LANES], 0.0)
        v_nat = jnp.where(in_head, kvw[:, LANES:], 0.0)
        k_oth = pltpu.roll(k_nat, HEAD_DIM, 1)
        v_oth = pltpu.roll(v_nat, HEAD_DIM, 1)
        pairs = ((k_nat, k_oth), (v_nat, v_oth)) if hk == 0 else ((k_oth, k_nat), (v_oth, v_nat))
        k_lo.append(pairs[0][0].astype(jnp.bfloat16))
        k_hi.append(pairs[0][1].astype(jnp.bfloat16))
        vt_lo.append(jnp.transpose(pairs[1][0]).astype(jnp.bfloat16))
        vt_hi.append(jnp.transpose(pairs[1][1]).astype(jnp.bfloat16))

    c = lax.broadcasted_iota(jnp.int32, (N_KEYS, BLOCK), 0)
    r = lax.broadcasted_iota(jnp.int32, (N_KEYS, BLOCK), 1)
    band = (c >= r) & (c - r <= 2 * WINDOW)
    ones_row = lax.broadcasted_iota(jnp.int32, (16, 2 * N_KEYS), 0)
    ones_col = lax.broadcasted_iota(jnp.int32, (16, 2 * N_KEYS), 1)
    ones = jnp.where(((ones_row == 0) & (ones_col < N_KEYS)) | ((ones_row == 1) & (ones_col >= N_KEYS)),
                     1.0, 0.0).astype(jnp.bfloat16)
    row_o = lax.broadcasted_iota(jnp.int32, (LANES, 2 * BLOCK), 0)
    n_blocks = SEQ // BLOCK

    def scores(t, hk):
        blk = i * ATTN_SUB + t
        lo = jnp.where(blk == 0, BLOCK, 0)
        hi = jnp.where(blk == n_blocks - 1, 2 * BLOCK, N_KEYS)
        valid1 = band & (c >= lo) & (c < hi)
        valid = jnp.concatenate([valid1, valid1], axis=1)
        win = slice(t * BLOCK, t * BLOCK + N_KEYS)
        q = q_ref[t * BLOCK:(t + 1) * BLOCK, (2 * hk) * LANES:(2 * hk + 2) * LANES]
        qq = jnp.concatenate([q[:, :LANES], q[:, LANES:]], axis=0)
        kcat = jnp.concatenate([k_lo[hk][win], k_hi[hk][win]], axis=0)
        st = lax.dot_general(kcat, qq, (((1,), (1,)), ((), ())),
                             preferred_element_type=jnp.float32)
        sps, ms, sinks = [], [], []
        for par in range(2):
            base = par * N_KEYS
            sp = jnp.concatenate(
                [jnp.where(valid[:BLOCK], st[base:base + BLOCK, :], NEG_INF),
                 st[base + BLOCK:base + 2 * BLOCK, :],
                 jnp.where(valid[2 * BLOCK:], st[base + 2 * BLOCK:base + N_KEYS, :], NEG_INF)], axis=0)
            h0 = 4 * hk + par
            h1 = h0 + 2
            sink = jnp.concatenate([jnp.full((1, BLOCK), sink_ref[h] * LOG2_E, jnp.float32)
                                    for h in (h0, h1)], axis=1)
            sps.append(sp)
            ms.append(jnp.maximum(jnp.max(sp, axis=0, keepdims=True), sink))
            sinks.append(sink)
        return sps, ms, sinks

    def weights(sps, ms, sinks):
        return [jnp.exp2((sp - m).astype(jnp.bfloat16)) for sp, m in zip(sps, ms)], ms, sinks

    def finish(t, hk, ps, ms, sinks):
        win = slice(t * BLOCK, t * BLOCK + N_KEYS)
        vt = jnp.concatenate([jnp.concatenate([vt_lo[hk][:, win], vt_hi[hk][:, win]], axis=1), ones], axis=0)
        ot = jnp.dot(vt, jnp.concatenate(ps, axis=0), preferred_element_type=jnp.float32)
        inv0 = 1.0 / (ot[LANES:LANES + 1, :] + jnp.exp2(sinks[0] - ms[0]))
        inv1 = 1.0 / (ot[LANES + 1:LANES + 2, :] + jnp.exp2(sinks[1] - ms[1]))
        o = jnp.transpose(ot[:LANES, :] * jnp.where(row_o < HEAD_DIM, inv0, inv1))
        rows = slice(t * BLOCK, (t + 1) * BLOCK)
        y_ref[rows, (2 * hk) * LANES:(2 * hk + 1) * LANES] = o[:BLOCK].astype(y_ref.dtype)
        y_ref[rows, (2 * hk + 1) * LANES:(2 * hk + 2) * LANES] = o[BLOCK:].astype(y_ref.dtype)

    work = [(t, hk) for t in range(ATTN_SUB) for hk in range(N_KV_HEADS)]
    stage_a = {0: scores(*work[0])}
    stage_b = {}
    for n in range(len(work) + 1):
        if n + 1 < len(work):
            stage_a[n + 1] = scores(*work[n + 1])
        if n < len(work):
            stage_b[n] = weights(*stage_a.pop(n))
        if n >= 1:
            finish(*work[n - 1], *stage_b.pop(n - 1))


def _attention(q, kv, sink):
    nb = SEQ // BLOCK
    return pl.pallas_call(
        _attn_kernel,
        grid=(BATCH, SEQ // ATTN_STEP),
        in_specs=[pl.BlockSpec((None, ATTN_STEP, Q_COLS), lambda b, i: (b, i, 0)),
                  pl.BlockSpec((None, BLOCK, 2 * KV_COLS),
                               lambda b, i: (b, jnp.maximum(i * ATTN_SUB - 1, 0), 0)),
                  pl.BlockSpec((None, ATTN_STEP, 2 * KV_COLS), lambda b, i: (b, i, 0)),
                  pl.BlockSpec((None, BLOCK, 2 * KV_COLS),
                               lambda b, i: (b, jnp.minimum((i + 1) * ATTN_SUB, nb - 1), 0)),
                  pl.BlockSpec(memory_space=pltpu.SMEM)],
        out_specs=pl.BlockSpec((None, ATTN_STEP, D_ATTN), lambda b, i: (b, i, 0)),
        out_shape=jax.ShapeDtypeStruct((BATCH, SEQ, D_ATTN), jnp.bfloat16),
        compiler_params=pltpu.CompilerParams(dimension_semantics=("arbitrary", "arbitrary"),
                                             vmem_limit_bytes=VMEM_LIMIT),
        name="attention",
    )(q, kv, kv, kv, sink)


SSM_GB = 4


def _ssm_kernel(ut_ref, m_ref, ws_ref, wo_ref, coef_ref, gt_ref,
                sf_ref, sfs_ref, sb_ref, sbs_ref, xf_ref, xb_ref, u_ref):
    for gi in range(SSM_GB):
        ut = jnp.concatenate([ut_ref[lb, gi] for lb in range(N_ROW_BLOCKS)], axis=1)
        u = jnp.transpose(ut)
        u_ref[gi] = u
        s = jnp.dot(u, ws_ref[gi], preferred_element_type=jnp.float32)
        for b in range(BATCH):
            rows = slice(b * N_CHUNKS, (b + 1) * N_CHUNKS)
            dst = pl.ds(b * SLAB_PITCH + 8, N_CHUNKS)
            sf_ref[gi, dst, :] = s[rows, 0 * LANES:1 * LANES]
            sfs_ref[gi, dst, :] = s[rows, 1 * LANES:2 * LANES]
            sb_ref[gi, dst, :] = s[rows, 2 * LANES:3 * LANES]
            sbs_ref[gi, dst, :] = s[rows, 3 * LANES:4 * LANES]

    zero = jnp.zeros((BATCH, LANES), jnp.float32)
    coefs = []
    for gi in range(SSM_GB):
        coefs.append([jnp.broadcast_to(coef_ref[gi, r:r + 1, :], (BATCH, LANES)) for r in range(6)])
        xf_ref[gi, pl.ds(8, BATCH, stride=SLAB_PITCH), :] = zero
        xb_ref[gi, pl.ds(8 + N_CHUNKS - 1, BATCH, stride=SLAB_PITCH), :] = zero

    def rows_at(r):
        return pl.ds(r, BATCH, stride=SLAB_PITCH)

    def step(k, carry):
        kb = N_CHUNKS - 1 - k
        out = []
        for gi in range(SSM_GB):
            a1f, a2f, a3f, a1b, a2b, a3b = coefs[gi]
            xf, xfs, xb, xbs = carry[4 * gi:4 * gi + 4]
            nxf = a1f * xf + a2f * xfs + sf_ref[gi, rows_at(8 + k), :]
            nxfs = a1f * xfs + a3f * xf + sfs_ref[gi, rows_at(8 + k), :]
            xf_ref[gi, rows_at(9 + k), :] = nxf
            nxb = a1b * xb + a2b * xbs + sb_ref[gi, rows_at(8 + kb), :]
            nxbs = a1b * xbs + a3b * xb + sbs_ref[gi, rows_at(8 + kb), :]
            xb_ref[gi, rows_at(7 + kb), :] = nxb
            out += [nxf, nxfs, nxb, nxbs]
        return tuple(out)

    lax.fori_loop(0, N_CHUNKS, step, (zero,) * (4 * SSM_GB), unroll=2)

    def chunk_out(gi):
        xin_f = jnp.concatenate([xf_ref[gi, pl.ds(b * SLAB_PITCH + 8, N_CHUNKS), :] for b in range(BATCH)], axis=0)
        xin_b = jnp.concatenate([xb_ref[gi, pl.ds(b * SLAB_PITCH + 8, N_CHUNKS), :] for b in range(BATCH)], axis=0)
        lhs = jnp.concatenate([u_ref[gi], xin_f.astype(jnp.bfloat16), xin_b.astype(jnp.bfloat16)], axis=1)
        rhs = jnp.concatenate([m_ref[gi], wo_ref[gi]], axis=0)
        return jnp.dot(lhs, rhs, preferred_element_type=jnp.float32)

    y = chunk_out(0)
    for gi in range(SSM_GB):
        y_next = chunk_out(gi + 1) if gi + 1 < SSM_GB else None
        gt = jnp.transpose(jax.nn.gelu(y).astype(gt_ref.dtype))
        for lb in range(N_ROW_BLOCKS):
            gt_ref[lb, gi] = gt[:, lb * LANES:(lb + 1) * LANES]
        y = y_next


def _ssm(ut, m_intra, w_state, w_out, coef):
    def per_group(shape):
        return pl.BlockSpec((SSM_GB,) + shape, lambda g: (g, 0, 0))

    lane_blocked = pl.BlockSpec((N_ROW_BLOCKS, SSM_GB, CHUNK_VEC, LANES), lambda g: (0, g, 0, 0))
    slab = pltpu.VMEM((SSM_GB, SLAB_ROWS, LANES), jnp.float32)
    return pl.pallas_call(
        _ssm_kernel,
        grid=(N_GROUPS // SSM_GB,),
        in_specs=[lane_blocked, per_group((CHUNK_VEC, CHUNK_VEC)),
                  per_group((CHUNK_VEC, 4 * LANES)), per_group((2 * STATE_VEC, CHUNK_VEC)),
                  per_group((8, LANES))],
        out_specs=lane_blocked,
        out_shape=jax.ShapeDtypeStruct((N_ROW_BLOCKS, N_GROUPS, CHUNK_VEC, LANES), jnp.bfloat16),
        scratch_shapes=[slab] * 6 + [pltpu.VMEM((SSM_GB, ROWS, CHUNK_VEC), jnp.bfloat16)],
        compiler_params=pltpu.CompilerParams(dimension_semantics=("arbitrary",),
                                             vmem_limit_bytes=VMEM_LIMIT),
        name="ssm",
    )(ut, m_intra, w_state, w_out, coef)


FINAL_SUB_ROWS = 256


def _unit_rms(v):
    return v * lax.rsqrt(jnp.mean(v * v, axis=-1, keepdims=True) + NORM_EPS)


def _final_kernel(x_ref, ya_ref, sz_ref, gt_ref, wglu32_ref, bglu_ref, wo32_ref, ga_ref, gs_ref,
                  lng_ref, lnb_ref, o_ref, gslab_ref, wglu_ref, wo_ref):
    step = pl.program_id(0)
    part = step % STEPS_PER_REGROUP

    @pl.when(step == 0)
    def _():
        wglu_ref[...] = wglu32_ref[...].astype(wglu_ref.dtype)
        for gain_ref, row0 in ((ga_ref, 0), (gs_ref, D_ATTN)):
            for v in range(gain_ref.shape[1] // LANES):
                gain = gain_ref[:, v * LANES:(v + 1) * LANES] / DEEPNORM_ALPHA
                col = jnp.transpose(jnp.broadcast_to(gain, (LANES, LANES)))
                rows = slice(row0 + v * LANES, row0 + (v + 1) * LANES)
                for c0 in range(0, D_MODEL, LANES):
                    wo_ref[rows, c0:c0 + LANES] = (wo32_ref[rows, c0:c0 + LANES] * col).astype(wo_ref.dtype)

    @pl.when(part == 0)
    def _():
        for j in range(CHUNK):
            for v in range(N_LANE_BLOCKS):
                g0 = v * GROUPS_PER_LANE_BLOCK
                gjt = gt_ref[g0:g0 + GROUPS_PER_LANE_BLOCK, j * SSM_CH:(j + 1) * SSM_CH, :]
                gj = jnp.transpose(gjt.reshape(LANES, LANES))
                gslab_ref[v, pl.ds(j, LANES, stride=SLAB_CHUNK_PITCH), :] = gj.astype(jnp.float32)

    def rows_of(sub):
        return slice(sub * FINAL_SUB_ROWS, (sub + 1) * FINAL_SUB_ROWS)

    def gate_in(sub):
        rows = rows_of(sub)
        ya = ya_ref[rows, :].astype(jnp.float32) * sz_ref[rows, :D_ATTN].astype(jnp.float32)
        ya = _unit_rms(ya).astype(jnp.bfloat16)
        base = pl.multiple_of((part * (TOK_BLOCK // CHUNK) + sub * (FINAL_SUB_ROWS // CHUNK)) * SLAB_CHUNK_PITCH, 8)
        g = jnp.concatenate(
            [jnp.concatenate([gslab_ref[v, pl.ds(base + kk * SLAB_CHUNK_PITCH, CHUNK), :]
                              for kk in range(FINAL_SUB_ROWS // CHUNK)], axis=0)
             for v in range(N_LANE_BLOCKS)], axis=1)
        return ya, g

    def glu(sub, ya, g):
        rows = rows_of(sub)
        gate = jnp.dot(g.astype(jnp.bfloat16), wglu_ref[...], preferred_element_type=jnp.float32) + bglu_ref[...]
        ys = g * jax.nn.sigmoid(gate)
        ys = ys * sz_ref[rows, D_ATTN:].astype(jnp.float32)
        return ya, _unit_rms(ys).astype(jnp.bfloat16)

    def project(sub, ya, ys):
        return (jnp.dot(ya, wo_ref[:D_ATTN, :], preferred_element_type=jnp.float32)
                + jnp.dot(ys, wo_ref[D_ATTN:, :], preferred_element_type=jnp.float32))

    def layer_norm(sub, out):
        rows = rows_of(sub)
        h = x_ref[rows, :] + out
        mu = jnp.mean(h, axis=-1, keepdims=True)
        hc = h - mu
        var = jnp.mean(hc * hc, axis=-1, keepdims=True)
        o_ref[rows, :] = hc * lax.rsqrt(var + NORM_EPS / DEEPNORM_ALPHA ** 2) * lng_ref[...] + lnb_ref[...]

    n_sub = TOK_BLOCK // FINAL_SUB_ROWS
    stage_a, stage_b = {}, {}
    for n in range(n_sub + 2):
        if n < n_sub:
            stage_a[n] = gate_in(n)
        if 1 <= n <= n_sub:
            stage_b[n - 1] = glu(n - 1, *stage_a.pop(n - 1))
        if n >= 2:
            layer_norm(n - 2, project(n - 2, *stage_b.pop(n - 2)))


def _final(x2, ya, sz, gt, wglu, bglu, wo, gain_a, gain_s, lng, lnb):
    n_steps = BATCH * SEQ // TOK_BLOCK

    def tok(cols):
        return pl.BlockSpec((TOK_BLOCK, cols), lambda s: (s, 0))

    def whole(a, **kw):
        return pl.BlockSpec(a.shape, lambda s: (0,) * a.ndim, **kw)

    return pl.pallas_call(
        _final_kernel,
        grid=(n_steps,),
        in_specs=[tok(D_MODEL), tok(D_ATTN), tok(D_ATTN + D_SSM),
                  pl.BlockSpec((None, N_GROUPS, CHUNK_VEC, LANES), lambda s: (s // STEPS_PER_REGROUP, 0, 0, 0)),
                  whole(wglu, pipeline_mode=pl.Buffered(1)), whole(bglu),
                  whole(wo, pipeline_mode=pl.Buffered(1)), whole(gain_a), whole(gain_s), whole(lng), whole(lnb)],
        out_specs=tok(D_MODEL),
        out_shape=jax.ShapeDtypeStruct((BATCH * SEQ, D_MODEL), jnp.float32),
        scratch_shapes=[pltpu.VMEM((N_LANE_BLOCKS, LANES * SLAB_CHUNK_PITCH, LANES), jnp.float32),
                        pltpu.VMEM(wglu.shape, jnp.bfloat16), pltpu.VMEM(wo.shape, jnp.bfloat16)],
        compiler_params=pltpu.CompilerParams(dimension_semantics=("arbitrary",),
                                             vmem_limit_bytes=VMEM_LIMIT),
        name="final",
    )(x2, ya, sz, gt, wglu, bglu, wo, gain_a, gain_s, lng, lnb)


def kernel(x, w_in, attn_sink, ssm_a_re, ssm_a_im, ssm_log_dt, ssm_b_re, ssm_b_im, ssm_c_re, ssm_c_im,
           ssm_d, w_glu, b_glu, norm_attn_g, norm_ssm_g, w_out, ln_g, ln_b):
    assert x.shape == (BATCH, SEQ, D_MODEL) and w_in.shape[0] == 1
    f32 = jnp.float32
    rope = _rope_tables()

    x2 = x.reshape(BATCH * SEQ, D_MODEL)
    b_t = [jnp.swapaxes(p, -1, -2) for p in (ssm_b_re, ssm_b_im)]
    params = [p[0].astype(f32) for p in (ssm_log_dt, ssm_a_re, ssm_a_im, *b_t, ssm_c_re, ssm_c_im)]
    q, kv, sz, ut, m_intra, w_state, w_so, coef = _inproj(x2, w_in[0], rope, *params,
                                                           ssm_d[0].astype(f32).reshape(1, D_SSM))
    ya = _attention(q.reshape(BATCH, SEQ, Q_COLS), kv.reshape(BATCH, SEQ, 2 * KV_COLS),
                    attn_sink[0].astype(f32))
    gt = _ssm(ut, m_intra, w_state, w_so, coef)

    row = lambda v: v[0].astype(f32)[None, :]
    out = _final(x2, ya.reshape(BATCH * SEQ, D_ATTN), sz, gt, w_glu[0].astype(f32), row(b_glu),
                 w_out[0].astype(f32), row(norm_attn_g), row(norm_ssm_g), row(ln_g), row(ln_b))
    return out.reshape(BATCH, SEQ, D_MODEL)
```

```python
import jax
import jax.numpy as jnp
import numpy as np
from jax import lax
from jax.experimental import pallas as pl
from jax.experimental.pallas import tpu as pltpu

D_MODEL = 1024
BATCH = 4
SEQ = 4096
D_ATTN = 512
D_SSM = 512
HEAD_DIM = 64
N_Q_HEADS = 8
N_KV_HEADS = 2
WINDOW = 128
BLOCK = 128
ROPE_THETA = 10000.0
SSM_CH = 16
N_GROUPS = 32
SSM_STATE = 64
NORM_EPS = 1e-5
NEG_INF = -1e30
DEEPNORM_ALPHA = 2.0 ** 0.25
LOG2_E = 1.4426950408889634
Q_SCALE = HEAD_DIM ** -0.5 * LOG2_E

Q_COLS = N_Q_HEADS * HEAD_DIM
KV_COLS = N_KV_HEADS * HEAD_DIM
CHUNK = 16
N_CHUNKS = SEQ // CHUNK
CHUNK_VEC = CHUNK * SSM_CH
STATE_VEC = 2 * SSM_STATE
LANES = 128
ROWS = BATCH * N_CHUNKS
N_ROW_BLOCKS = ROWS // LANES
SLAB_PITCH = N_CHUNKS + 8
SLAB_ROWS = BATCH * SLAB_PITCH + 8
TOK_BLOCK = 1024
REGROUP_TOK = LANES * CHUNK
STEPS_PER_REGROUP = REGROUP_TOK // TOK_BLOCK
SUB_ROWS = 512
N_LANE_BLOCKS = D_SSM // LANES
SLAB_CHUNK_PITCH = 20
GROUPS_PER_LANE_BLOCK = LANES // SSM_CH
VMEM_LIMIT = 56 * 1024 * 1024
HIGHEST = lax.Precision.HIGHEST


LAG_ROWS = 512
PREP_GB = N_GROUPS // (BATCH * SEQ // TOK_BLOCK)


def _ssm_tables(ldt_ref, are_ref, aim_ref, bre_ref, bim_ref, cre_ref, cim_ref, d_ref,
                m_ref, ws_ref, wo_ref, coef_ref):
    f32 = jnp.float32
    lo16 = lax.broadcasted_iota(jnp.int32, (SSM_CH, LANES), 1) < SSM_STATE
    lo1 = lax.broadcasted_iota(jnp.int32, (1, LANES), 1) < SSM_STATE
    units = [(gi, d) for gi in range(PREP_GB) for d in range(2)]

    def group(gi):
        return pl.program_id(0) * PREP_GB + gi

    def dup(v):
        return jnp.concatenate([v, v], axis=1)

    lam, zoh = {}, {}
    for gi, d in units:
        ar = dup(are_ref[d, pl.ds(group(gi), 1), :])
        ai = dup(aim_ref[d, pl.ds(group(gi), 1), :])
        dt = jnp.exp(jnp.full((1, LANES), ldt_ref[d, group(gi)], f32))
        zr = dt * ar
        zi = dt * ai
        mag = jnp.exp(zr)
        lr = mag * jnp.cos(zi)
        li = mag * jnp.sin(zi)
        den = ar * ar + ai * ai
        lam[gi, d] = (lr, li)
        zoh[gi, d] = (((lr - 1.0) * ar + li * ai) / den, (li * ar - (lr - 1.0) * ai) / den)

    pr = {u: [jnp.ones((1, LANES), f32)] for u in units}
    pi = {u: [jnp.zeros((1, LANES), f32)] for u in units}
    for _ in range(CHUNK):
        for u in units:
            lr, li = lam[u]
            pr[u].append(pr[u][-1] * lr - pi[u][-1] * li)
            pi[u].append(pr[u][-2] * li + pi[u][-1] * lr)

    lag_tabs, kws = {}, {}
    for gi in range(PREP_GB):
        ws_parts, coef_rows = [], []
        for d in range(2):
            u = (gi, d)
            fr, fi = zoh[u]
            btr = dup(bre_ref[d, gi])
            bti = dup(bim_ref[d, gi])
            bbr = fr * btr - fi * bti
            bbi = fr * bti + fi * btr
            cr = dup(cre_ref[d, gi])
            ci = dup(cim_ref[d, gi])
            pa = [jnp.where(lo1, r, i) for r, i in zip(pr[u], pi[u])]
            pb = [jnp.where(lo1, -i, r) for r, i in zip(pr[u], pi[u])]
            pc = [jnp.where(lo1, i, r) for r, i in zip(pr[u], pi[u])]
            pd = [jnp.where(lo1, r, -i) for r, i in zip(pr[u], pi[u])]

            order = range(CHUNK + 1) if d == 0 else range(CHUNK, -1, -1)
            blocks = [cr * pa[t] + ci * pb[t] for t in order]
            blocks.append(jnp.zeros((LAG_ROWS - (CHUNK + 1) * SSM_CH, LANES), f32))
            lag_tabs[u] = jnp.concatenate(blocks, axis=0)
            lhs = jnp.where(lo16, bbr, -bbi)
            kws[u] = lax.dot_general(lhs, lag_tabs[u], (((1,), (1,)), ((), ())),
                                     precision=HIGHEST, preferred_element_type=f32)

            parts = []
            for j in range(CHUNK):
                t = CHUNK - 1 - j if d == 0 else j
                parts.append(jnp.concatenate([bbr * pa[t] + bbi * pb[t],
                                              bbr * pc[t] + bbi * pd[t]], axis=1))
            ws_parts.append(jnp.concatenate(parts, axis=0))

            a2 = jnp.where(lo1, -pi[u][CHUNK], pi[u][CHUNK])
            coef_rows += [pr[u][CHUNK], a2, -a2]

        ws_ref[gi] = jnp.concatenate(ws_parts, axis=1).astype(ws_ref.dtype)
        coef_ref[gi] = jnp.concatenate(coef_rows + [jnp.zeros((2, LANES), f32)], axis=0)

    sign = jnp.where(lax.broadcasted_iota(jnp.int32, (LANES, CHUNK_VEC), 0) < SSM_STATE, 1.0, -1.0)
    sub = lax.broadcasted_iota(jnp.int32, (SSM_CH, CHUNK_VEC), 0)
    lane = lax.broadcasted_iota(jnp.int32, (SSM_CH, CHUNK_VEC), 1)
    for gi in range(PREP_GB):
        wo_f = jnp.transpose(lag_tabs[gi, 0][SSM_CH:SSM_CH + CHUNK_VEC, :]) * sign
        wo_b = jnp.transpose(lag_tabs[gi, 1][:CHUNK_VEC, :]) * sign
        wo_ref[gi] = jnp.concatenate([wo_f, wo_b], axis=0).astype(wo_ref.dtype)

        g = group(gi)
        dblk = d_ref[:, pl.ds(pl.multiple_of((g // GROUPS_PER_LANE_BLOCK) * LANES, LANES), LANES)]
        dblk = pltpu.roll(dblk, (LANES - (g % GROUPS_PER_LANE_BLOCK) * SSM_CH) % LANES, 1)
        dblk = jnp.where(lax.broadcasted_iota(jnp.int32, (1, LANES), 1) < SSM_CH, dblk, 0.0)
        for shift in (SSM_CH, 2 * SSM_CH, 4 * SSM_CH):
            dblk = dblk + pltpu.roll(dblk, shift, 1)
        dtile = jnp.broadcast_to(jnp.concatenate([dblk, dblk], axis=1), (SSM_CH, CHUNK_VEC))
        for jp in range(CHUNK):
            fwd = kws[gi, 0] if jp == 0 else pltpu.roll(kws[gi, 0], SSM_CH * jp, 1)
            bwd = pltpu.roll(kws[gi, 1], (LAG_ROWS - SSM_CH * (CHUNK - jp)) % LAG_ROWS, 1)
            skip = jnp.where((lane // SSM_CH == jp) & (lane % SSM_CH == sub), dtile, 0.0)
            blk = fwd[:, :CHUNK_VEC] + bwd[:, :CHUNK_VEC] + skip
            m_ref[gi, jp * SSM_CH:(jp + 1) * SSM_CH, :] = blk.astype(m_ref.dtype)


def _rope_tables():
    half = HEAD_DIM // 2
    inv_freq = ROPE_THETA ** (-np.arange(half, dtype=np.float64) / half)
    ang = np.arange(SEQ, dtype=np.float64)[:, None] * inv_freq[None, :]
    cos, sin = np.cos(ang), np.sin(ang)
    return jnp.asarray(np.concatenate([cos, cos, cos, cos, -sin, sin, -sin, sin], axis=1), jnp.float32)


def _rope_block(xb, cos_t, sin_t, first_half):
    swapped = jnp.where(first_half, pltpu.roll(xb, 96, 1), pltpu.roll(xb, 32, 1))
    return xb * cos_t + swapped * sin_t


COL_Q = (0, Q_COLS)
COL_KV = (COL_Q[1], COL_Q[1] + 2 * KV_COLS)
COL_ZA = (COL_KV[1], COL_KV[1] + D_ATTN)
COL_U = (COL_ZA[1], COL_ZA[1] + D_SSM)
COL_ZS = (COL_U[1], COL_U[1] + D_SSM)


X_SLICES = 2


def _x_slice_specs():
    rows = TOK_BLOCK // X_SLICES
    return [pl.BlockSpec((rows, D_MODEL), lambda s, i=i: (s * X_SLICES + i, 0)) for i in range(X_SLICES)]


def _x_rows(x_refs, row0, n):
    rows = TOK_BLOCK // X_SLICES
    return x_refs[row0 // rows][row0 % rows:row0 % rows + n, :]


def _silu(z):
    h = 0.5 * z
    return h + h * jnp.tanh(h)


def _inproj_kernel(x0_ref, x1_ref, w32_ref, rope_ref,
                   ldt_ref, are_ref, aim_ref, bre_ref, bim_ref, cre_ref, cim_ref, d_ref,
                   q_ref, kv_ref, sz_ref, ut_ref, m_ref, ws_ref, wo_ref, coef_ref, uslab_ref, w_ref):
    step = pl.program_id(0)
    part = step % STEPS_PER_REGROUP
    lane = lax.broadcasted_iota(jnp.int32, (SUB_ROWS, LANES), 1)
    first_half = (lane % HEAD_DIM) < (HEAD_DIM // 2)

    @pl.when(step == 0)
    def _():
        for c0 in range(0, w_ref.shape[1], 2 * LANES):
            w_ref[:, c0:c0 + 2 * LANES] = w32_ref[:, c0:c0 + 2 * LANES].astype(w_ref.dtype)

    def proj(xb, cols):
        return jnp.dot(xb, w_ref[:, cols[0]:cols[1]], preferred_element_type=jnp.float32)

    for sub in range(TOK_BLOCK // SUB_ROWS):
        rows = slice(sub * SUB_ROWS, (sub + 1) * SUB_ROWS)
        xb = _x_rows((x0_ref, x1_ref), sub * SUB_ROWS, SUB_ROWS).astype(jnp.bfloat16)
        cos_t = rope_ref[rows, :LANES]
        sin_t = rope_ref[rows, LANES:]
        q = proj(xb, COL_Q)
        kv = proj(xb, COL_KV)
        q_blocks = [_rope_block(q[:, v * LANES:(v + 1) * LANES], cos_t, sin_t, first_half)
                    for v in range(Q_COLS // LANES)]
        q_ref[rows, :] = (jnp.concatenate(q_blocks, axis=1) * Q_SCALE).astype(q_ref.dtype)
        za = proj(xb, COL_ZA)
        k_rot = _rope_block(kv[:, :LANES], cos_t, sin_t, first_half)
        kv_ref[rows, :] = jnp.concatenate([k_rot, kv[:, LANES:]], axis=1).astype(kv_ref.dtype)
        u = proj(xb, COL_U)
        sz_ref[rows, :D_ATTN] = _silu(za).astype(sz_ref.dtype)
        zs = proj(xb, COL_ZS)
        for kk in range(SUB_ROWS // CHUNK):
            base = pl.multiple_of((part * (TOK_BLOCK // CHUNK) + sub * (SUB_ROWS // CHUNK)) * SLAB_CHUNK_PITCH, 8)
            for v in range(N_LANE_BLOCKS):
                uslab_ref[v, pl.ds(base + kk * SLAB_CHUNK_PITCH, CHUNK), :] = (
                    u[kk * CHUNK:(kk + 1) * CHUNK, v * LANES:(v + 1) * LANES])
        sz_ref[rows, D_ATTN:] = _silu(zs).astype(sz_ref.dtype)
        if sub == 0:
            _ssm_tables(ldt_ref, are_ref, aim_ref, bre_ref, bim_ref, cre_ref, cim_ref, d_ref,
                        m_ref, ws_ref, wo_ref, coef_ref)

    @pl.when(part == STEPS_PER_REGROUP - 1)
    def _():
        for j in range(CHUNK):
            for v in range(N_LANE_BLOCKS):
                uj = uslab_ref[v, pl.ds(j, LANES, stride=SLAB_CHUNK_PITCH), :]
                ujt = jnp.transpose(uj.astype(ut_ref.dtype)).reshape(GROUPS_PER_LANE_BLOCK, SSM_CH, LANES)
                g0 = v * GROUPS_PER_LANE_BLOCK
                ut_ref[g0:g0 + GROUPS_PER_LANE_BLOCK, j * SSM_CH:(j + 1) * SSM_CH, :] = ujt


def _inproj(x2, w, rope, log_dt, a_re, a_im, b_re, b_im, c_re, c_im, d):
    n_steps = BATCH * SEQ // TOK_BLOCK
    blocks_per_seq = SEQ // TOK_BLOCK
    bf16 = jnp.bfloat16

    def tok(cols):
        return pl.BlockSpec((TOK_BLOCK, cols), lambda s: (s, 0))

    def whole(a):
        return pl.BlockSpec(a.shape, lambda s: (0,) * a.ndim)

    def group_in(a):
        return pl.BlockSpec((2, PREP_GB) + a.shape[2:], lambda s: (0, s, 0, 0))

    def group_out(rows, cols):
        return pl.BlockSpec((PREP_GB, rows, cols), lambda s: (s, 0, 0))

    return pl.pallas_call(
        _inproj_kernel,
        grid=(n_steps,),
        in_specs=[*_x_slice_specs(), pl.BlockSpec(w.shape, lambda s: (0, 0), pipeline_mode=pl.Buffered(1)),
                  pl.BlockSpec((TOK_BLOCK, 2 * LANES), lambda s: (s % blocks_per_seq, 0)),
                  pl.BlockSpec(memory_space=pltpu.SMEM), whole(a_re), whole(a_im),
                  group_in(b_re), group_in(b_im), group_in(c_re), group_in(c_im), whole(d)],
        out_specs=[tok(Q_COLS), tok(2 * KV_COLS), tok(D_ATTN + D_SSM),
                   pl.BlockSpec((None, N_GROUPS, CHUNK_VEC, LANES), lambda s: (s // STEPS_PER_REGROUP, 0, 0, 0)),
                   group_out(CHUNK_VEC, CHUNK_VEC), group_out(CHUNK_VEC, 4 * LANES),
                   group_out(2 * STATE_VEC, CHUNK_VEC), group_out(8, LANES)],
        out_shape=[jax.ShapeDtypeStruct((BATCH * SEQ, Q_COLS), bf16),
                   jax.ShapeDtypeStruct((BATCH * SEQ, 2 * KV_COLS), bf16),
                   jax.ShapeDtypeStruct((BATCH * SEQ, D_ATTN + D_SSM), bf16),
                   jax.ShapeDtypeStruct((N_ROW_BLOCKS, N_GROUPS, CHUNK_VEC, LANES), bf16),
                   jax.ShapeDtypeStruct((N_GROUPS, CHUNK_VEC, CHUNK_VEC), bf16),
                   jax.ShapeDtypeStruct((N_GROUPS, CHUNK_VEC, 4 * LANES), bf16),
                   jax.ShapeDtypeStruct((N_GROUPS, 2 * STATE_VEC, CHUNK_VEC), bf16),
                   jax.ShapeDtypeStruct((N_GROUPS, 8, LANES), jnp.float32)],
        scratch_shapes=[pltpu.VMEM((N_LANE_BLOCKS, LANES * SLAB_CHUNK_PITCH, LANES), jnp.float32),
                        pltpu.VMEM(w.shape, bf16)],
        compiler_params=pltpu.CompilerParams(dimension_semantics=("arbitrary",),
                                             vmem_limit_bytes=VMEM_LIMIT),
        name="inproj",
    )(x2, x2, w, rope, log_dt, a_re, a_im, b_re, b_im, c_re, c_im, d)


ATTN_SUB = 32
ATTN_STEP = ATTN_SUB * BLOCK
N_KEYS = 3 * BLOCK


def _attn_kernel(q_ref, kp_ref, kc_ref, kn_ref, sink_ref, y_ref):
    i = pl.program_id(1)
    kvw = jnp.concatenate([kp_ref[...], kc_ref[...], kn_ref[...]], axis=0).astype(jnp.float32)

    n_win = kvw.shape[0]
    lane_w = lax.broadcasted_iota(jnp.int32, (n_win, LANES), 1)
    k_lo, k_hi, vt_lo, vt_hi = [], [], [], []
    for hk in range(N_KV_HEADS):
        in_head = (lane_w >= hk * HEAD_DIM) & (lane_w < (hk + 1) * HEAD_DIM)
        k_nat = jnp.where(in_head, kvw[:, :LANES], 0.0)
        v_nat = jnp.where(in_head, kvw[:, LANES:], 0.0)
        k_oth = pltpu.roll(k_nat, HEAD_DIM, 1)
        v_oth = pltpu.roll(v_nat, HEAD_DIM, 1)
        pairs = ((k_nat, k_oth), (v_nat, v_oth)) if hk == 0 else ((k_oth, k_nat), (v_oth, v_nat))
        k_lo.append(pairs[0][0].astype(jnp.bfloat16))
        k_hi.append(pairs[0][1].astype(jnp.bfloat16))
        vt_lo.append(jnp.transpose(pairs[1][0]).astype(jnp.bfloat16))
        vt_hi.append(jnp.transpose(pairs[1][1]).astype(jnp.bfloat16))

    c = lax.broadcasted_iota(jnp.int32, (N_KEYS, BLOCK), 0)
    r = lax.broadcasted_iota(jnp.int32, (N_KEYS, BLOCK), 1)
    band = (c >= r) & (c - r <= 2 * WINDOW)
    ones_row = lax.broadcasted_iota(jnp.int32, (16, 2 * N_KEYS), 0)
    ones_col = lax.broadcasted_iota(jnp.int32, (16, 2 * N_KEYS), 1)
    ones = jnp.where(((ones_row == 0) & (ones_col < N_KEYS)) | ((ones_row == 1) & (ones_col >= N_KEYS)),
                     1.0, 0.0).astype(jnp.bfloat16)
    row_o = lax.broadcasted_iota(jnp.int32, (LANES, 2 * BLOCK), 0)
    n_blocks = SEQ // BLOCK

    def scores(t, hk):
        blk = i * ATTN_SUB + t
        lo = jnp.where(blk == 0, BLOCK, 0)
        hi = jnp.where(blk == n_blocks - 1, 2 * BLOCK, N_KEYS)
        valid1 = band & (c >= lo) & (c < hi)
        valid = jnp.concatenate([valid1, valid1], axis=1)
        win = slice(t * BLOCK, t * BLOCK + N_KEYS)
        q = q_ref[t * BLOCK:(t + 1) * BLOCK, (2 * hk) * LANES:(2 * hk + 2) * LANES]
        qq = jnp.concatenate([q[:, :LANES], q[:, LANES:]], axis=0)
        kcat = jnp.concatenate([k_lo[hk][win], k_hi[hk][win]], axis=0)
        st = lax.dot_general(kcat, qq, (((1,), (1,)), ((), ())),
                             preferred_element_type=jnp.float32)
        sps, ms, sinks = [], [], []
        for par in range(2):
            base = par * N_KEYS
            sp = jnp.concatenate(
                [jnp.where(valid[:BLOCK], st[base:base + BLOCK, :], NEG_INF),
                 st[base + BLOCK:base + 2 * BLOCK, :],
                 jnp.where(valid[2 * BLOCK:], st[base + 2 * BLOCK:base + N_KEYS, :], NEG_INF)], axis=0)
            h0 = 4 * hk + par
            h1 = h0 + 2
            sink = jnp.concatenate([jnp.full((1, BLOCK), sink_ref[h] * LOG2_E, jnp.float32)
                                    for h in (h0, h1)], axis=1)
            sps.append(sp)
            ms.append(jnp.maximum(jnp.max(sp, axis=0, keepdims=True), sink))
            sinks.append(sink)
        return sps, ms, sinks

    def weights(sps, ms, sinks):
        return [jnp.exp2((sp - m).astype(jnp.bfloat16)) for sp, m in zip(sps, ms)], ms, sinks

    def finish(t, hk, ps, ms, sinks):
        win = slice(t * BLOCK, t * BLOCK + N_KEYS)
        vt = jnp.concatenate([jnp.concatenate([vt_lo[hk][:, win], vt_hi[hk][:, win]], axis=1), ones], axis=0)
        ot = jnp.dot(vt, jnp.concatenate(ps, axis=0), preferred_element_type=jnp.float32)
        inv0 = 1.0 / (ot[LANES:LANES + 1, :] + jnp.exp2(sinks[0] - ms[0]))
        inv1 = 1.0 / (ot[LANES + 1:LANES + 2, :] + jnp.exp2(sinks[1] - ms[1]))
        o = jnp.transpose(ot[:LANES, :] * jnp.where(row_o < HEAD_DIM, inv0, inv1))
        rows = slice(t * BLOCK, (t + 1) * BLOCK)
        y_ref[rows, (2 * hk) * LANES:(2 * hk + 1) * LANES] = o[:BLOCK].astype(y_ref.dtype)
        y_ref[rows, (2 * hk + 1) * LANES:(2 * hk + 2) * LANES] = o[BLOCK:].astype(y_ref.dtype)

    work = [(t, hk) for t in range(ATTN_SUB) for hk in range(N_KV_HEADS)]
    stage_a = {0: scores(*work[0])}
    stage_b = {}
    for n in range(len(work) + 1):
        if n + 1 < len(work):
            stage_a[n + 1] = scores(*work[n + 1])
        if n < len(work):
            stage_b[n] = weights(*stage_a.pop(n))
        if n >= 1:
            finish(*work[n - 1], *stage_b.pop(n - 1))


def _attention(q, kv, sink):
    nb = SEQ // BLOCK
    return pl.pallas_call(
        _attn_kernel,
        grid=(BATCH, SEQ // ATTN_STEP),
        in_specs=[pl.BlockSpec((None, ATTN_STEP, Q_COLS), lambda b, i: (b, i, 0)),
                  pl.BlockSpec((None, BLOCK, 2 * KV_COLS),
                               lambda b, i: (b, jnp.maximum(i * ATTN_SUB - 1, 0), 0)),
                  pl.BlockSpec((None, ATTN_STEP, 2 * KV_COLS), lambda b, i: (b, i, 0)),
                  pl.BlockSpec((None, BLOCK, 2 * KV_COLS),
                               lambda b, i: (b, jnp.minimum((i + 1) * ATTN_SUB, nb - 1), 0)),
                  pl.BlockSpec(memory_space=pltpu.SMEM)],
        out_specs=pl.BlockSpec((None, ATTN_STEP, D_ATTN), lambda b, i: (b, i, 0)),
        out_shape=jax.ShapeDtypeStruct((BATCH, SEQ, D_ATTN), jnp.bfloat16),
        compiler_params=pltpu.CompilerParams(dimension_semantics=("arbitrary", "arbitrary"),
                                             vmem_limit_bytes=VMEM_LIMIT),
        name="attention",
    )(q, kv, kv, kv, sink)


SSM_GB = 4


def _ssm_kernel(ut_ref, m_ref, ws_ref, wo_ref, coef_ref, gt_ref,
                sf_ref, sfs_ref, sb_ref, sbs_ref, xf_ref, xb_ref, u_ref):
    for gi in range(SSM_GB):
        ut = jnp.concatenate([ut_ref[lb, gi] for lb in range(N_ROW_BLOCKS)], axis=1)
        u = jnp.transpose(ut)
        u_ref[gi] = u
        s = jnp.dot(u, ws_ref[gi], preferred_element_type=jnp.float32)
        for b in range(BATCH):
            rows = slice(b * N_CHUNKS, (b + 1) * N_CHUNKS)
            dst = pl.ds(b * SLAB_PITCH + 8, N_CHUNKS)
            sf_ref[gi, dst, :] = s[rows, 0 * LANES:1 * LANES]
            sfs_ref[gi, dst, :] = s[rows, 1 * LANES:2 * LANES]
            sb_ref[gi, dst, :] = s[rows, 2 * LANES:3 * LANES]
            sbs_ref[gi, dst, :] = s[rows, 3 * LANES:4 * LANES]

    zero = jnp.zeros((BATCH, LANES), jnp.float32)
    coefs = []
    for gi in range(SSM_GB):
        coefs.append([jnp.broadcast_to(coef_ref[gi, r:r + 1, :], (BATCH, LANES)) for r in range(6)])
        xf_ref[gi, pl.ds(8, BATCH, stride=SLAB_PITCH), :] = zero
        xb_ref[gi, pl.ds(8 + N_CHUNKS - 1, BATCH, stride=SLAB_PITCH), :] = zero

    def rows_at(r):
        return pl.ds(r, BATCH, stride=SLAB_PITCH)

    def step(k, carry):
        kb = N_CHUNKS - 1 - k
        out = []
        for gi in range(SSM_GB):
            a1f, a2f, a3f, a1b, a2b, a3b = coefs[gi]
            xf, xfs, xb, xbs = carry[4 * gi:4 * gi + 4]
            nxf = a1f * xf + a2f * xfs + sf_ref[gi, rows_at(8 + k), :]
            nxfs = a1f * xfs + a3f * xf + sfs_ref[gi, rows_at(8 + k), :]
            xf_ref[gi, rows_at(9 + k), :] = nxf
            nxb = a1b * xb + a2b * xbs + sb_ref[gi, rows_at(8 + kb), :]
            nxbs = a1b * xbs + a3b * xb + sbs_ref[gi, rows_at(8 + kb), :]
            xb_ref[gi, rows_at(7 + kb), :] = nxb
            out += [nxf, nxfs, nxb, nxbs]
        return tuple(out)

    lax.fori_loop(0, N_CHUNKS, step, (zero,) * (4 * SSM_GB), unroll=2)

    def chunk_out(gi):
        xin_f = jnp.concatenate([xf_ref[gi, pl.ds(b * SLAB_PITCH + 8, N_CHUNKS), :] for b in range(BATCH)], axis=0)
        xin_b = jnp.concatenate([xb_ref[gi, pl.ds(b * SLAB_PITCH + 8, N_CHUNKS), :] for b in range(BATCH)], axis=0)
        lhs = jnp.concatenate([u_ref[gi], xin_f.astype(jnp.bfloat16), xin_b.astype(jnp.bfloat16)], axis=1)
        rhs = jnp.concatenate([m_ref[gi], wo_ref[gi]], axis=0)
        return jnp.dot(lhs, rhs, preferred_element_type=jnp.float32)

    y = chunk_out(0)
    for gi in range(SSM_GB):
        y_next = chunk_out(gi + 1) if gi + 1 < SSM_GB else None
        gt = jnp.transpose(jax.nn.gelu(y).astype(gt_ref.dtype))
        for lb in range(N_ROW_BLOCKS):
            gt_ref[lb, gi] = gt[:, lb * LANES:(lb + 1) * LANES]
        y = y_next


def _ssm(ut, m_intra, w_state, w_out, coef):
    def per_group(shape):
        return pl.BlockSpec((SSM_GB,) + shape, lambda g: (g, 0, 0))

    lane_blocked = pl.BlockSpec((N_ROW_BLOCKS, SSM_GB, CHUNK_VEC, LANES), lambda g: (0, g, 0, 0))
    slab = pltpu.VMEM((SSM_GB, SLAB_ROWS, LANES), jnp.float32)
    return pl.pallas_call(
        _ssm_kernel,
        grid=(N_GROUPS // SSM_GB,),
        in_specs=[lane_blocked, per_group((CHUNK_VEC, CHUNK_VEC)),
                  per_group((CHUNK_VEC, 4 * LANES)), per_group((2 * STATE_VEC, CHUNK_VEC)),
                  per_group((8, LANES))],
        out_specs=lane_blocked,
        out_shape=jax.ShapeDtypeStruct((N_ROW_BLOCKS, N_GROUPS, CHUNK_VEC, LANES), jnp.bfloat16),
        scratch_shapes=[slab] * 6 + [pltpu.VMEM((SSM_GB, ROWS, CHUNK_VEC), jnp.bfloat16)],
        compiler_params=pltpu.CompilerParams(dimension_semantics=("arbitrary",),
                                             vmem_limit_bytes=VMEM_LIMIT),
        name="ssm",
    )(ut, m_intra, w_state, w_out, coef)


FINAL_SUB_ROWS = 256


def _unit_rms(v):
    return v * lax.rsqrt(jnp.mean(v * v, axis=-1, keepdims=True) + NORM_EPS)


def _final_kernel(x0_ref, x1_ref, ya_ref, sz_ref, gt_ref, wglu32_ref, bglu_ref, wo32_ref, ga_ref, gs_ref,
                  lng_ref, lnb_ref, o_ref, gslab_ref, wglu_ref, wo_ref):
    step = pl.program_id(0)
    part = step % STEPS_PER_REGROUP

    @pl.when(step == 0)
    def _():
        wglu_ref[...] = wglu32_ref[...].astype(wglu_ref.dtype)
        for gain_ref, row0 in ((ga_ref, 0), (gs_ref, D_ATTN)):
            for v in range(gain_ref.shape[1] // LANES):
                gain = gain_ref[:, v * LANES:(v + 1) * LANES] / DEEPNORM_ALPHA
                col = jnp.transpose(jnp.broadcast_to(gain, (LANES, LANES)))
                rows = slice(row0 + v * LANES, row0 + (v + 1) * LANES)
                for c0 in range(0, D_MODEL, LANES):
                    wo_ref[rows, c0:c0 + LANES] = (wo32_ref[rows, c0:c0 + LANES] * col).astype(wo_ref.dtype)

    @pl.when(part == 0)
    def _():
        for j in range(CHUNK):
            for v in range(N_LANE_BLOCKS):
                g0 = v * GROUPS_PER_LANE_BLOCK
                gjt = gt_ref[g0:g0 + GROUPS_PER_LANE_BLOCK, j * SSM_CH:(j + 1) * SSM_CH, :]
                gj = jnp.transpose(gjt.reshape(LANES, LANES))
                gslab_ref[v, pl.ds(j, LANES, stride=SLAB_CHUNK_PITCH), :] = gj.astype(jnp.float32)

    def rows_of(sub):
        return slice(sub * FINAL_SUB_ROWS, (sub + 1) * FINAL_SUB_ROWS)

    def gate_in(sub):
        rows = rows_of(sub)
        ya = ya_ref[rows, :].astype(jnp.float32) * sz_ref[rows, :D_ATTN].astype(jnp.float32)
        ya = _unit_rms(ya).astype(jnp.bfloat16)
        base = pl.multiple_of((part * (TOK_BLOCK // CHUNK) + sub * (FINAL_SUB_ROWS // CHUNK)) * SLAB_CHUNK_PITCH, 8)
        g = jnp.concatenate(
            [jnp.concatenate([gslab_ref[v, pl.ds(base + kk * SLAB_CHUNK_PITCH, CHUNK), :]
                              for kk in range(FINAL_SUB_ROWS // CHUNK)], axis=0)
             for v in range(N_LANE_BLOCKS)], axis=1)
        return ya, g

    def glu(sub, ya, g):
        rows = rows_of(sub)
        gate = jnp.dot(g.astype(jnp.bfloat16), wglu_ref[...], preferred_element_type=jnp.float32) + bglu_ref[...]
        ys = g * jax.nn.sigmoid(gate)
        ys = ys * sz_ref[rows, D_ATTN:].astype(jnp.float32)
        return ya, _unit_rms(ys).astype(jnp.bfloat16)

    def project(sub, ya, ys):
        return (jnp.dot(ya, wo_ref[:D_ATTN, :], preferred_element_type=jnp.float32)
                + jnp.dot(ys, wo_ref[D_ATTN:, :], preferred_element_type=jnp.float32))

    def layer_norm(sub, out):
        rows = rows_of(sub)
        h = _x_rows((x0_ref, x1_ref), sub * FINAL_SUB_ROWS, FINAL_SUB_ROWS) + out
        mu = jnp.mean(h, axis=-1, keepdims=True)
        hc = h - mu
        var = jnp.mean(hc * hc, axis=-1, keepdims=True)
        o_ref[rows, :] = hc * lax.rsqrt(var + NORM_EPS / DEEPNORM_ALPHA ** 2) * lng_ref[...] + lnb_ref[...]

    n_sub = TOK_BLOCK // FINAL_SUB_ROWS
    stage_a, stage_b = {}, {}
    for n in range(n_sub + 2):
        if n < n_sub:
            stage_a[n] = gate_in(n)
        if 1 <= n <= n_sub:
            stage_b[n - 1] = glu(n - 1, *stage_a.pop(n - 1))
        if n >= 2:
            layer_norm(n - 2, project(n - 2, *stage_b.pop(n - 2)))


def _final(x2, ya, sz, gt, wglu, bglu, wo, gain_a, gain_s, lng, lnb):
    n_steps = BATCH * SEQ // TOK_BLOCK

    def tok(cols):
        return pl.BlockSpec((TOK_BLOCK, cols), lambda s: (s, 0))

    def whole(a, **kw):
        return pl.BlockSpec(a.shape, lambda s: (0,) * a.ndim, **kw)

    return pl.pallas_call(
        _final_kernel,
        grid=(n_steps,),
        in_specs=[*_x_slice_specs(), tok(D_ATTN), tok(D_ATTN + D_SSM),
                  pl.BlockSpec((None, N_GROUPS, CHUNK_VEC, LANES), lambda s: (s // STEPS_PER_REGROUP, 0, 0, 0)),
                  whole(wglu, pipeline_mode=pl.Buffered(1)), whole(bglu),
                  whole(wo, pipeline_mode=pl.Buffered(1)), whole(gain_a), whole(gain_s), whole(lng), whole(lnb)],
        out_specs=tok(D_MODEL),
        out_shape=jax.ShapeDtypeStruct((BATCH * SEQ, D_MODEL), jnp.float32),
        scratch_shapes=[pltpu.VMEM((N_LANE_BLOCKS, LANES * SLAB_CHUNK_PITCH, LANES), jnp.float32),
                        pltpu.VMEM(wglu.shape, jnp.bfloat16), pltpu.VMEM(wo.shape, jnp.bfloat16)],
        compiler_params=pltpu.CompilerParams(dimension_semantics=("arbitrary",),
                                             vmem_limit_bytes=VMEM_LIMIT),
        name="final",
    )(x2, x2, ya, sz, gt, wglu, bglu, wo, gain_a, gain_s, lng, lnb)


def kernel(x, w_in, attn_sink, ssm_a_re, ssm_a_im, ssm_log_dt, ssm_b_re, ssm_b_im, ssm_c_re, ssm_c_im,
           ssm_d, w_glu, b_glu, norm_attn_g, norm_ssm_g, w_out, ln_g, ln_b):
    assert x.shape == (BATCH, SEQ, D_MODEL) and w_in.shape[0] == 1
    f32 = jnp.float32
    rope = _rope_tables()

    x2 = x.reshape(BATCH * SEQ, D_MODEL)
    b_t = [jnp.swapaxes(p, -1, -2) for p in (ssm_b_re, ssm_b_im)]
    params = [p[0].astype(f32) for p in (ssm_log_dt, ssm_a_re, ssm_a_im, *b_t, ssm_c_re, ssm_c_im)]
    q, kv, sz, ut, m_intra, w_state, w_so, coef = _inproj(x2, w_in[0], rope, *params,
                                                           ssm_d[0].astype(f32).reshape(1, D_SSM))
    ya = _attention(q.reshape(BATCH, SEQ, Q_COLS), kv.reshape(BATCH, SEQ, 2 * KV_COLS),
                    attn_sink[0].astype(f32))
    gt = _ssm(ut, m_intra, w_state, w_so, coef)

    row = lambda v: v[0].astype(f32)[None, :]
    out = _final(x2, ya.reshape(BATCH * SEQ, D_ATTN), sz, gt, w_glu[0].astype(f32), row(b_glu),
                 w_out[0].astype(f32), row(norm_attn_g), row(norm_ssm_g), row(ln_g), row(ln_b))
    return out.reshape(BATCH, SEQ, D_MODEL)
```

```python
import jax
import jax.numpy as jnp
import numpy as np
from jax import lax
from jax.experimental import pallas as pl
from jax.experimental.pallas import tpu as pltpu

D_MODEL = 1024
BATCH = 4
SEQ = 4096
D_ATTN = 512
D_SSM = 512
HEAD_DIM = 64
N_Q_HEADS = 8
N_KV_HEADS = 2
WINDOW = 128
BLOCK = 128
ROPE_THETA = 10000.0
SSM_CH = 16
N_GROUPS = 32
SSM_STATE = 64
NORM_EPS = 1e-5
NEG_INF = -1e30
DEEPNORM_ALPHA = 2.0 ** 0.25
LOG2_E = 1.4426950408889634
Q_SCALE = HEAD_DIM ** -0.5 * LOG2_E

Q_COLS = N_Q_HEADS * HEAD_DIM
KV_COLS = N_KV_HEADS * HEAD_DIM
CHUNK = 16
N_CHUNKS = SEQ // CHUNK
CHUNK_VEC = CHUNK * SSM_CH
STATE_VEC = 2 * SSM_STATE
LANES = 128
ROWS = BATCH * N_CHUNKS
N_ROW_BLOCKS = ROWS // LANES
SLAB_PITCH = N_CHUNKS + 8
SLAB_ROWS = BATCH * SLAB_PITCH + 8
TOK_BLOCK = 1024
REGROUP_TOK = LANES * CHUNK
STEPS_PER_REGROUP = REGROUP_TOK // TOK_BLOCK
SUB_ROWS = 512
N_LANE_BLOCKS = D_SSM // LANES
SLAB_CHUNK_PITCH = 20
GROUPS_PER_LANE_BLOCK = LANES // SSM_CH
VMEM_LIMIT = 56 * 1024 * 1024
HIGHEST = lax.Precision.HIGHEST


LAG_ROWS = 512
PREP_GB = N_GROUPS // (BATCH * SEQ // TOK_BLOCK)


def _ssm_tables(ldt_ref, are_ref, aim_ref, bre_ref, bim_ref, cre_ref, cim_ref, d_ref,
                m_ref, ws_ref, wo_ref, coef_ref):
    f32 = jnp.float32
    lo16 = lax.broadcasted_iota(jnp.int32, (SSM_CH, LANES), 1) < SSM_STATE
    lo1 = lax.broadcasted_iota(jnp.int32, (1, LANES), 1) < SSM_STATE
    units = [(gi, d) for gi in range(PREP_GB) for d in range(2)]

    def group(gi):
        return pl.program_id(0) * PREP_GB + gi

    def dup(v):
        return jnp.concatenate([v, v], axis=1)

    lam, zoh = {}, {}
    for gi, d in units:
        ar = dup(are_ref[d, pl.ds(group(gi), 1), :])
        ai = dup(aim_ref[d, pl.ds(group(gi), 1), :])
        dt = jnp.exp(jnp.full((1, LANES), ldt_ref[d, group(gi)], f32))
        zr = dt * ar
        zi = dt * ai
        mag = jnp.exp(zr)
        lr = mag * jnp.cos(zi)
        li = mag * jnp.sin(zi)
        den = ar * ar + ai * ai
        lam[gi, d] = (lr, li)
        zoh[gi, d] = (((lr - 1.0) * ar + li * ai) / den, (li * ar - (lr - 1.0) * ai) / den)

    pr = {u: [jnp.ones((1, LANES), f32)] for u in units}
    pi = {u: [jnp.zeros((1, LANES), f32)] for u in units}
    for _ in range(CHUNK):
        for u in units:
            lr, li = lam[u]
            pr[u].append(pr[u][-1] * lr - pi[u][-1] * li)
            pi[u].append(pr[u][-2] * li + pi[u][-1] * lr)

    lag_tabs, kws = {}, {}
    for gi in range(PREP_GB):
        ws_parts, coef_rows = [], []
        for d in range(2):
            u = (gi, d)
            fr, fi = zoh[u]
            btr = dup(bre_ref[d, gi])
            bti = dup(bim_ref[d, gi])
            bbr = fr * btr - fi * bti
            bbi = fr * bti + fi * btr
            cr = dup(cre_ref[d, gi])
            ci = dup(cim_ref[d, gi])
            pa = [jnp.where(lo1, r, i) for r, i in zip(pr[u], pi[u])]
            pb = [jnp.where(lo1, -i, r) for r, i in zip(pr[u], pi[u])]
            pc = [jnp.where(lo1, i, r) for r, i in zip(pr[u], pi[u])]
            pd = [jnp.where(lo1, r, -i) for r, i in zip(pr[u], pi[u])]

            order = range(CHUNK + 1) if d == 0 else range(CHUNK, -1, -1)
            blocks = [cr * pa[t] + ci * pb[t] for t in order]
            blocks.append(jnp.zeros((LAG_ROWS - (CHUNK + 1) * SSM_CH, LANES), f32))
            lag_tabs[u] = jnp.concatenate(blocks, axis=0)
            lhs = jnp.where(lo16, bbr, -bbi)
            kws[u] = lax.dot_general(lhs, lag_tabs[u], (((1,), (1,)), ((), ())),
                                     precision=HIGHEST, preferred_element_type=f32)

            parts = []
            for j in range(CHUNK):
                t = CHUNK - 1 - j if d == 0 else j
                parts.append(jnp.concatenate([bbr * pa[t] + bbi * pb[t],
                                              bbr * pc[t] + bbi * pd[t]], axis=1))
            ws_parts.append(jnp.concatenate(parts, axis=0))

            a2 = jnp.where(lo1, -pi[u][CHUNK], pi[u][CHUNK])
            coef_rows += [pr[u][CHUNK], a2, -a2]

        ws_ref[gi] = jnp.concatenate(ws_parts, axis=1).astype(ws_ref.dtype)
        coef_ref[gi] = jnp.concatenate(coef_rows + [jnp.zeros((2, LANES), f32)], axis=0)

    sign = jnp.where(lax.broadcasted_iota(jnp.int32, (LANES, CHUNK_VEC), 0) < SSM_STATE, 1.0, -1.0)
    sub = lax.broadcasted_iota(jnp.int32, (SSM_CH, CHUNK_VEC), 0)
    lane = lax.broadcasted_iota(jnp.int32, (SSM_CH, CHUNK_VEC), 1)
    for gi in range(PREP_GB):
        wo_f = jnp.transpose(lag_tabs[gi, 0][SSM_CH:SSM_CH + CHUNK_VEC, :]) * sign
        wo_b = jnp.transpose(lag_tabs[gi, 1][:CHUNK_VEC, :]) * sign
        wo_ref[gi] = jnp.concatenate([wo_f, wo_b], axis=0).astype(wo_ref.dtype)

        g = group(gi)
        dblk = d_ref[:, pl.ds(pl.multiple_of((g // GROUPS_PER_LANE_BLOCK) * LANES, LANES), LANES)]
        dblk = pltpu.roll(dblk, (LANES - (g % GROUPS_PER_LANE_BLOCK) * SSM_CH) % LANES, 1)
        dblk = jnp.where(lax.broadcasted_iota(jnp.int32, (1, LANES), 1) < SSM_CH, dblk, 0.0)
        for shift in (SSM_CH, 2 * SSM_CH, 4 * SSM_CH):
            dblk = dblk + pltpu.roll(dblk, shift, 1)
        dtile = jnp.broadcast_to(jnp.concatenate([dblk, dblk], axis=1), (SSM_CH, CHUNK_VEC))
        for jp in range(CHUNK):
            fwd = kws[gi, 0] if jp == 0 else pltpu.roll(kws[gi, 0], SSM_CH * jp, 1)
            bwd = pltpu.roll(kws[gi, 1], (LAG_ROWS - SSM_CH * (CHUNK - jp)) % LAG_ROWS, 1)
            skip = jnp.where((lane // SSM_CH == jp) & (lane % SSM_CH == sub), dtile, 0.0)
            blk = fwd[:, :CHUNK_VEC] + bwd[:, :CHUNK_VEC] + skip
            m_ref[gi, jp * SSM_CH:(jp + 1) * SSM_CH, :] = blk.astype(m_ref.dtype)


def _rope_tables():
    half = HEAD_DIM // 2
    inv_freq = ROPE_THETA ** (-np.arange(half, dtype=np.float64) / half)
    ang = np.arange(SEQ, dtype=np.float64)[:, None] * inv_freq[None, :]
    cos, sin = np.cos(ang), np.sin(ang)
    return jnp.asarray(np.concatenate([cos, cos, cos, cos, -sin, sin, -sin, sin], axis=1), jnp.float32)


def _rope_block(xb, cos_t, sin_t, first_half):
    swapped = jnp.where(first_half, pltpu.roll(xb, 96, 1), pltpu.roll(xb, 32, 1))
    return xb * cos_t + swapped * sin_t


COL_Q = (0, Q_COLS)
COL_KV = (COL_Q[1], COL_Q[1] + 2 * KV_COLS)
COL_ZA = (COL_KV[1], COL_KV[1] + D_ATTN)
COL_U = (COL_ZA[1], COL_ZA[1] + D_SSM)
COL_ZS = (COL_U[1], COL_U[1] + D_SSM)


def _silu(z):
    h = 0.5 * z
    return h + h * jnp.tanh(h)


def _inproj_kernel(x_ref, w32_ref, rope_ref, ldt_ref, are_ref, aim_ref, bre_ref, bim_ref, cre_ref, cim_ref, d_ref,
                   q_ref, kv_ref, sz_ref, ut_ref, m_ref, ws_ref, wo_ref, coef_ref, uslab_ref, w_ref):
    step = pl.program_id(0)
    part = step % STEPS_PER_REGROUP
    lane = lax.broadcasted_iota(jnp.int32, (SUB_ROWS, LANES), 1)
    first_half = (lane % HEAD_DIM) < (HEAD_DIM // 2)

    @pl.when(step == 0)
    def _():
        for c0 in range(0, w_ref.shape[1], 2 * LANES):
            w_ref[:, c0:c0 + 2 * LANES] = w32_ref[:, c0:c0 + 2 * LANES].astype(w_ref.dtype)

    def proj(xb, cols):
        return jnp.dot(xb, w_ref[:, cols[0]:cols[1]], preferred_element_type=jnp.float32)

    for sub in range(TOK_BLOCK // SUB_ROWS):
        rows = slice(sub * SUB_ROWS, (sub + 1) * SUB_ROWS)
        xb = x_ref[rows, :].astype(jnp.bfloat16)
        cos_t = rope_ref[rows, :LANES]
        sin_t = rope_ref[rows, LANES:]
        q = proj(xb, COL_Q)
        kv = proj(xb, COL_KV)
        q_blocks = [_rope_block(q[:, v * LANES:(v + 1) * LANES], cos_t, sin_t, first_half)
                    for v in range(Q_COLS // LANES)]
        q_ref[rows, :] = (jnp.concatenate(q_blocks, axis=1) * Q_SCALE).astype(q_ref.dtype)
        za = proj(xb, COL_ZA)
        k_rot = _rope_block(kv[:, :LANES], cos_t, sin_t, first_half)
        kv_ref[rows, :] = jnp.concatenate([k_rot, kv[:, LANES:]], axis=1).astype(kv_ref.dtype)
        u = proj(xb, COL_U)
        sz_ref[rows, :D_ATTN] = _silu(za).astype(sz_ref.dtype)
        zs = proj(xb, COL_ZS)
        for kk in range(SUB_ROWS // CHUNK):
            base = pl.multiple_of((part * (TOK_BLOCK // CHUNK) + sub * (SUB_ROWS // CHUNK)) * SLAB_CHUNK_PITCH, 8)
            for v in range(N_LANE_BLOCKS):
                uslab_ref[v, pl.ds(base + kk * SLAB_CHUNK_PITCH, CHUNK), :] = (
                    u[kk * CHUNK:(kk + 1) * CHUNK, v * LANES:(v + 1) * LANES])
        sz_ref[rows, D_ATTN:] = _silu(zs).astype(sz_ref.dtype)
        if sub == 0:
            _ssm_tables(ldt_ref, are_ref, aim_ref, bre_ref, bim_ref, cre_ref, cim_ref, d_ref,
                        m_ref, ws_ref, wo_ref, coef_ref)

    @pl.when(part == STEPS_PER_REGROUP - 1)
    def _():
        for j in range(CHUNK):
            for v in range(N_LANE_BLOCKS):
                uj = uslab_ref[v, pl.ds(j, LANES, stride=SLAB_CHUNK_PITCH), :]
                ujt = jnp.transpose(uj.astype(ut_ref.dtype)).reshape(GROUPS_PER_LANE_BLOCK, SSM_CH, LANES)
                g0 = v * GROUPS_PER_LANE_BLOCK
                ut_ref[g0:g0 + GROUPS_PER_LANE_BLOCK, j * SSM_CH:(j + 1) * SSM_CH, :] = ujt


def _inproj(x2, w, rope, log_dt, a_re, a_im, b_re, b_im, c_re, c_im, d):
    n_steps = BATCH * SEQ // TOK_BLOCK
    blocks_per_seq = SEQ // TOK_BLOCK
    bf16 = jnp.bfloat16

    def tok(cols):
        return pl.BlockSpec((TOK_BLOCK, cols), lambda s: (s, 0))

    def whole(a):
        return pl.BlockSpec(a.shape, lambda s: (0,) * a.ndim)

    def group_in(a):
        return pl.BlockSpec((2, PREP_GB) + a.shape[2:], lambda s: (0, s, 0, 0))

    def group_out(rows, cols):
        return pl.BlockSpec((PREP_GB, rows, cols), lambda s: (s, 0, 0))

    return pl.pallas_call(
        _inproj_kernel,
        grid=(n_steps,),
        in_specs=[tok(D_MODEL), pl.BlockSpec(w.shape, lambda s: (0, 0), pipeline_mode=pl.Buffered(1)),
                  pl.BlockSpec((TOK_BLOCK, 2 * LANES), lambda s: (s % blocks_per_seq, 0)),
                  pl.BlockSpec(memory_space=pltpu.SMEM), whole(a_re), whole(a_im),
                  group_in(b_re), group_in(b_im), group_in(c_re), group_in(c_im), whole(d)],
        out_specs=[tok(Q_COLS), tok(2 * KV_COLS), tok(D_ATTN + D_SSM),
                   pl.BlockSpec((None, N_GROUPS, CHUNK_VEC, LANES), lambda s: (s // STEPS_PER_REGROUP, 0, 0, 0)),
                   group_out(CHUNK_VEC, CHUNK_VEC), group_out(CHUNK_VEC, 4 * LANES),
                   group_out(2 * STATE_VEC, CHUNK_VEC), group_out(8, LANES)],
        out_shape=[jax.ShapeDtypeStruct((BATCH * SEQ, Q_COLS), bf16),
                   jax.ShapeDtypeStruct((BATCH * SEQ, 2 * KV_COLS), bf16),
                   jax.ShapeDtypeStruct((BATCH * SEQ, D_ATTN + D_SSM), bf16),
                   jax.ShapeDtypeStruct((N_ROW_BLOCKS, N_GROUPS, CHUNK_VEC, LANES), bf16),
                   jax.ShapeDtypeStruct((N_GROUPS, CHUNK_VEC, CHUNK_VEC), bf16),
                   jax.ShapeDtypeStruct((N_GROUPS, CHUNK_VEC, 4 * LANES), bf16),
                   jax.ShapeDtypeStruct((N_GROUPS, 2 * STATE_VEC, CHUNK_VEC), bf16),
                   jax.ShapeDtypeStruct((N_GROUPS, 8, LANES), jnp.float32)],
        scratch_shapes=[pltpu.VMEM((N_LANE_BLOCKS, LANES * SLAB_CHUNK_PITCH, LANES), jnp.float32),
                        pltpu.VMEM(w.shape, bf16)],
        compiler_params=pltpu.CompilerParams(dimension_semantics=("arbitrary",),
                                             vmem_limit_bytes=VMEM_LIMIT),
        name="inproj",
    )(x2, w, rope, log_dt, a_re, a_im, b_re, b_im, c_re, c_im, d)


ATTN_SUB = 32
ATTN_STEP = ATTN_SUB * BLOCK
N_KEYS = 3 * BLOCK


def _attn_kernel(q_ref, kp_ref, kc_ref, kn_ref, sink_ref, y_ref):
    i = pl.program_id(1)
    kvw = jnp.concatenate([kp_ref[...], kc_ref[...], kn_ref[...]], axis=0).astype(jnp.float32)

    n_win = kvw.shape[0]
    lane_w = lax.broadcasted_iota(jnp.int32, (n_win, LANES), 1)
    k_lo, k_hi, vt_lo, vt_hi = [], [], [], []
    for hk in range(N_KV_HEADS):
        in_head = (lane_w >= hk * HEAD_DIM) & (lane_w < (hk + 1) * HEAD_DIM)
        k_nat = jnp.where(in_head, kvw[:, :LANES], 0.0)
        v_nat = jnp.where(in_head, kvw[:, LANES:], 0.0)
        k_oth = pltpu.roll(k_nat, HEAD_DIM, 1)
        v_oth = pltpu.roll(v_nat, HEAD_DIM, 1)
        pairs = ((k_nat, k_oth), (v_nat, v_oth)) if hk == 0 else ((k_oth, k_nat), (v_oth, v_nat))
        k_lo.append(pairs[0][0].astype(jnp.bfloat16))
        k_hi.append(pairs[0][1].astype(jnp.bfloat16))
        vt_lo.append(jnp.transpose(pairs[1][0]).astype(jnp.bfloat16))
        vt_hi.append(jnp.transpose(pairs[1][1]).astype(jnp.bfloat16))

    c = lax.broadcasted_iota(jnp.int32, (N_KEYS, BLOCK), 0)
    r = lax.broadcasted_iota(jnp.int32, (N_KEYS, BLOCK), 1)
    band = (c >= r) & (c - r <= 2 * WINDOW)
    ones_row = lax.broadcasted_iota(jnp.int32, (16, 2 * N_KEYS), 0)
    ones_col = lax.broadcasted_iota(jnp.int32, (16, 2 * N_KEYS), 1)
    ones = jnp.where(((ones_row == 0) & (ones_col < N_KEYS)) | ((ones_row == 1) & (ones_col >= N_KEYS)),
                     1.0, 0.0).astype(jnp.bfloat16)
    row_o = lax.broadcasted_iota(jnp.int32, (LANES, 2 * BLOCK), 0)
    n_blocks = SEQ // BLOCK

    def scores(t, hk):
        blk = i * ATTN_SUB + t
        lo = jnp.where(blk == 0, BLOCK, 0)
        hi = jnp.where(blk == n_blocks - 1, 2 * BLOCK, N_KEYS)
        valid1 = band & (c >= lo) & (c < hi)
        valid = jnp.concatenate([valid1, valid1], axis=1)
        win = slice(t * BLOCK, t * BLOCK + N_KEYS)
        q = q_ref[t * BLOCK:(t + 1) * BLOCK, (2 * hk) * LANES:(2 * hk + 2) * LANES]
        qq = jnp.concatenate([q[:, :LANES], q[:, LANES:]], axis=0)
        kcat = jnp.concatenate([k_lo[hk][win], k_hi[hk][win]], axis=0)
        st = lax.dot_general(kcat, qq, (((1,), (1,)), ((), ())),
                             preferred_element_type=jnp.float32)
        sps, ms, sinks = [], [], []
        for par in range(2):
            base = par * N_KEYS
            sp = jnp.concatenate(
                [jnp.where(valid[:BLOCK], st[base:base + BLOCK, :], NEG_INF),
                 st[base + BLOCK:base + 2 * BLOCK, :],
                 jnp.where(valid[2 * BLOCK:], st[base + 2 * BLOCK:base + N_KEYS, :], NEG_INF)], axis=0)
            h0 = 4 * hk + par
            h1 = h0 + 2
            sink = jnp.concatenate([jnp.full((1, BLOCK), sink_ref[h] * LOG2_E, jnp.float32)
                                    for h in (h0, h1)], axis=1)
            sps.append(sp)
            ms.append(jnp.maximum(jnp.max(sp, axis=0, keepdims=True), sink))
            sinks.append(sink)
        return sps, ms, sinks

    def weights(sps, ms, sinks):
        return [jnp.exp2((sp - m).astype(jnp.bfloat16)) for sp, m in zip(sps, ms)], ms, sinks

    def finish(t, hk, ps, ms, sinks):
        win = slice(t * BLOCK, t * BLOCK + N_KEYS)
        vt = jnp.concatenate([jnp.concatenate([vt_lo[hk][:, win], vt_hi[hk][:, win]], axis=1), ones], axis=0)
        ot = jnp.dot(vt, jnp.concatenate(ps, axis=0), preferred_element_type=jnp.float32)
        inv0 = 1.0 / (ot[LANES:LANES + 1, :] + jnp.exp2(sinks[0] - ms[0]))
        inv1 = 1.0 / (ot[LANES + 1:LANES + 2, :] + jnp.exp2(sinks[1] - ms[1]))
        o = jnp.transpose(ot[:LANES, :] * jnp.where(row_o < HEAD_DIM, inv0, inv1))
        rows = slice(t * BLOCK, (t + 1) * BLOCK)
        y_ref[rows, (2 * hk) * LANES:(2 * hk + 1) * LANES] = o[:BLOCK].astype(y_ref.dtype)
        y_ref[rows, (2 * hk + 1) * LANES:(2 * hk + 2) * LANES] = o[BLOCK:].astype(y_ref.dtype)

    work = [(t, hk) for t in range(ATTN_SUB) for hk in range(N_KV_HEADS)]
    stage_a = {0: scores(*work[0])}
    stage_b = {}
    for n in range(len(work) + 1):
        if n + 1 < len(work):
            stage_a[n + 1] = scores(*work[n + 1])
        if n < len(work):
            stage_b[n] = weights(*stage_a.pop(n))
        if n >= 1:
            finish(*work[n - 1], *stage_b.pop(n - 1))


def _attention(q, kv, sink):
    nb = SEQ // BLOCK
    return pl.pallas_call(
        _attn_kernel,
        grid=(BATCH, SEQ // ATTN_STEP),
        in_specs=[pl.BlockSpec((None, ATTN_STEP, Q_COLS), lambda b, i: (b, i, 0)),
                  pl.BlockSpec((None, BLOCK, 2 * KV_COLS),
                               lambda b, i: (b, jnp.maximum(i * ATTN_SUB - 1, 0), 0)),
                  pl.BlockSpec((None, ATTN_STEP, 2 * KV_COLS), lambda b, i: (b, i, 0)),
                  pl.BlockSpec((None, BLOCK, 2 * KV_COLS),
                               lambda b, i: (b, jnp.minimum((i + 1) * ATTN_SUB, nb - 1), 0)),
                  pl.BlockSpec(memory_space=pltpu.SMEM)],
        out_specs=pl.BlockSpec((None, ATTN_STEP, D_ATTN), lambda b, i: (b, i, 0)),
        out_shape=jax.ShapeDtypeStruct((BATCH, SEQ, D_ATTN), jnp.bfloat16),
        compiler_params=pltpu.CompilerParams(dimension_semantics=("arbitrary", "arbitrary"),
                                             vmem_limit_bytes=VMEM_LIMIT),
        name="attention",
    )(q, kv, kv, kv, sink)


SSM_GB = 4
SSM_STEPS = N_GROUPS // SSM_GB


def _ssm_kernel(ut_ref, m_ref, ws_ref, wo_ref, coef_ref, gt_ref,
                sf_ref, sfs_ref, sb_ref, sbs_ref, xf_ref, xb_ref, u_even_ref, u_odd_ref):
    s = pl.program_id(0)

    def head(gi, u_ref):
        ut = jnp.concatenate([ut_ref[lb, gi] for lb in range(N_ROW_BLOCKS)], axis=1)
        u = jnp.transpose(ut)
        u_ref[gi] = u
        st = jnp.dot(u, ws_ref[gi], preferred_element_type=jnp.float32)
        for b in range(BATCH):
            rows = slice(b * N_CHUNKS, (b + 1) * N_CHUNKS)
            dst = pl.ds(b * SLAB_PITCH + 8, N_CHUNKS)
            sf_ref[gi, dst, :] = st[rows, 0 * LANES:1 * LANES]
            sfs_ref[gi, dst, :] = st[rows, 1 * LANES:2 * LANES]
            sb_ref[gi, dst, :] = st[rows, 2 * LANES:3 * LANES]
            sbs_ref[gi, dst, :] = st[rows, 3 * LANES:4 * LANES]

    def chunk_out(gi, u_ref):
        xin_f = jnp.concatenate([xf_ref[gi, pl.ds(b * SLAB_PITCH + 8, N_CHUNKS), :] for b in range(BATCH)], axis=0)
        xin_b = jnp.concatenate([xb_ref[gi, pl.ds(b * SLAB_PITCH + 8, N_CHUNKS), :] for b in range(BATCH)], axis=0)
        lhs = jnp.concatenate([u_ref[gi], xin_f.astype(jnp.bfloat16), xin_b.astype(jnp.bfloat16)], axis=1)
        rhs = jnp.concatenate([m_ref[gi], wo_ref[gi]], axis=0)
        return jnp.dot(lhs, rhs, preferred_element_type=jnp.float32)

    def store_out(gi, y):
        gt = jnp.transpose(jax.nn.gelu(y).astype(gt_ref.dtype))
        for lb in range(N_ROW_BLOCKS):
            gt_ref[lb, gi] = gt[:, lb * LANES:(lb + 1) * LANES]

    def head_and_tail(u_head_ref, u_tail_ref):
        y = chunk_out(0, u_tail_ref) if u_tail_ref is not None else None
        for gi in range(SSM_GB):
            if u_head_ref is not None:
                head(gi, u_head_ref)
            if u_tail_ref is not None:
                y_next = chunk_out(gi + 1, u_tail_ref) if gi + 1 < SSM_GB else None
                store_out(gi, y)
                y = y_next

    u_refs = (u_even_ref, u_odd_ref)
    inner = (s > 0) & (s < SSM_STEPS)
    pl.when(s == 0)(lambda: head_and_tail(u_refs[0], None))
    pl.when(inner & (s % 2 == 0))(lambda: head_and_tail(u_refs[0], u_refs[1]))
    pl.when(inner & (s % 2 == 1))(lambda: head_and_tail(u_refs[1], u_refs[0]))
    pl.when(s == SSM_STEPS)(lambda: head_and_tail(None, u_refs[(SSM_STEPS - 1) % 2]))

    @pl.when(s < SSM_STEPS)
    def _():
        zero = jnp.zeros((BATCH, LANES), jnp.float32)
        coefs = []
        for gi in range(SSM_GB):
            coefs.append([jnp.broadcast_to(coef_ref[gi, r:r + 1, :], (BATCH, LANES)) for r in range(6)])
            xf_ref[gi, pl.ds(8, BATCH, stride=SLAB_PITCH), :] = zero
            xb_ref[gi, pl.ds(8 + N_CHUNKS - 1, BATCH, stride=SLAB_PITCH), :] = zero

        def rows_at(r):
            return pl.ds(r, BATCH, stride=SLAB_PITCH)

        def step(k, carry):
            kb = N_CHUNKS - 1 - k
            out = []
            for gi in range(SSM_GB):
                a1f, a2f, a3f, a1b, a2b, a3b = coefs[gi]
                xf, xfs, xb, xbs = carry[4 * gi:4 * gi + 4]
                nxf = a1f * xf + a2f * xfs + sf_ref[gi, rows_at(8 + k), :]
                nxfs = a1f * xfs + a3f * xf + sfs_ref[gi, rows_at(8 + k), :]
                xf_ref[gi, rows_at(9 + k), :] = nxf
                nxb = a1b * xb + a2b * xbs + sb_ref[gi, rows_at(8 + kb), :]
                nxbs = a1b * xbs + a3b * xb + sbs_ref[gi, rows_at(8 + kb), :]
                xb_ref[gi, rows_at(7 + kb), :] = nxb
                out += [nxf, nxfs, nxb, nxbs]
            return tuple(out)

        lax.fori_loop(0, N_CHUNKS, step, (zero,) * (4 * SSM_GB), unroll=2)


def _ssm(ut, m_intra, w_state, w_out, coef):
    def head_block(s):
        return jnp.minimum(s, SSM_STEPS - 1)

    def tail_block(s):
        return jnp.maximum(s - 1, 0)

    def per_group(shape, block):
        return pl.BlockSpec((SSM_GB,) + shape, lambda s: (block(s), 0, 0))

    def lane_blocked(block):
        return pl.BlockSpec((N_ROW_BLOCKS, SSM_GB, CHUNK_VEC, LANES), lambda s: (0, block(s), 0, 0))

    slab = pltpu.VMEM((SSM_GB, SLAB_ROWS, LANES), jnp.float32)
    return pl.pallas_call(
        _ssm_kernel,
        grid=(SSM_STEPS + 1,),
        in_specs=[lane_blocked(head_block), per_group((CHUNK_VEC, CHUNK_VEC), tail_block),
                  per_group((CHUNK_VEC, 4 * LANES), head_block), per_group((2 * STATE_VEC, CHUNK_VEC), tail_block),
                  per_group((8, LANES), head_block)],
        out_specs=lane_blocked(tail_block),
        out_shape=jax.ShapeDtypeStruct((N_ROW_BLOCKS, N_GROUPS, CHUNK_VEC, LANES), jnp.bfloat16),
        scratch_shapes=[slab] * 6 + [pltpu.VMEM((SSM_GB, ROWS, CHUNK_VEC), jnp.bfloat16)] * 2,
        compiler_params=pltpu.CompilerParams(dimension_semantics=("arbitrary",),
                                             vmem_limit_bytes=VMEM_LIMIT),
        name="ssm",
    )(ut, m_intra, w_state, w_out, coef)


FINAL_SUB_ROWS = 256


def _unit_rms(v):
    return v * lax.rsqrt(jnp.mean(v * v, axis=-1, keepdims=True) + NORM_EPS)


def _final_kernel(x_ref, ya_ref, sz_ref, gt_ref, wglu32_ref, bglu_ref, wo32_ref, ga_ref, gs_ref,
                  lng_ref, lnb_ref, o_ref, gslab_ref, wglu_ref, wo_ref):
    step = pl.program_id(0)
    part = step % STEPS_PER_REGROUP

    @pl.when(step == 0)
    def _():
        wglu_ref[...] = wglu32_ref[...].astype(wglu_ref.dtype)
        for gain_ref, row0 in ((ga_ref, 0), (gs_ref, D_ATTN)):
            for v in range(gain_ref.shape[1] // LANES):
                gain = gain_ref[:, v * LANES:(v + 1) * LANES] / DEEPNORM_ALPHA
                col = jnp.transpose(jnp.broadcast_to(gain, (LANES, LANES)))
                rows = slice(row0 + v * LANES, row0 + (v + 1) * LANES)
                for c0 in range(0, D_MODEL, LANES):
                    wo_ref[rows, c0:c0 + LANES] = (wo32_ref[rows, c0:c0 + LANES] * col).astype(wo_ref.dtype)

    @pl.when(part == 0)
    def _():
        for j in range(CHUNK):
            for v in range(N_LANE_BLOCKS):
                g0 = v * GROUPS_PER_LANE_BLOCK
                gjt = gt_ref[g0:g0 + GROUPS_PER_LANE_BLOCK, j * SSM_CH:(j + 1) * SSM_CH, :]
                gj = jnp.transpose(gjt.reshape(LANES, LANES))
                gslab_ref[v, pl.ds(j, LANES, stride=SLAB_CHUNK_PITCH), :] = gj.astype(jnp.float32)

    def rows_of(sub):
        return slice(sub * FINAL_SUB_ROWS, (sub + 1) * FINAL_SUB_ROWS)

    def gate_in(sub):
        rows = rows_of(sub)
        ya = ya_ref[rows, :].astype(jnp.float32) * sz_ref[rows, :D_ATTN].astype(jnp.float32)
        ya = _unit_rms(ya).astype(jnp.bfloat16)
        base = pl.multiple_of((part * (TOK_BLOCK // CHUNK) + sub * (FINAL_SUB_ROWS // CHUNK)) * SLAB_CHUNK_PITCH, 8)
        g = jnp.concatenate(
            [jnp.concatenate([gslab_ref[v, pl.ds(base + kk * SLAB_CHUNK_PITCH, CHUNK), :]
                              for kk in range(FINAL_SUB_ROWS // CHUNK)], axis=0)
             for v in range(N_LANE_BLOCKS)], axis=1)
        return ya, g

    def glu(sub, ya, g):
        rows = rows_of(sub)
        gate = jnp.dot(g.astype(jnp.bfloat16), wglu_ref[...], preferred_element_type=jnp.float32) + bglu_ref[...]
        ys = g * jax.nn.sigmoid(gate)
        ys = ys * sz_ref[rows, D_ATTN:].astype(jnp.float32)
        return ya, _unit_rms(ys).astype(jnp.bfloat16)

    def project(sub, ya, ys):
        return (jnp.dot(ya, wo_ref[:D_ATTN, :], preferred_element_type=jnp.float32)
                + jnp.dot(ys, wo_ref[D_ATTN:, :], preferred_element_type=jnp.float32))

    def layer_norm(sub, out):
        rows = rows_of(sub)
        h = x_ref[rows, :] + out
        mu = jnp.mean(h, axis=-1, keepdims=True)
        hc = h - mu
        var = jnp.mean(hc * hc, axis=-1, keepdims=True)
        o_ref[rows, :] = hc * lax.rsqrt(var + NORM_EPS / DEEPNORM_ALPHA ** 2) * lng_ref[...] + lnb_ref[...]

    n_sub = TOK_BLOCK // FINAL_SUB_ROWS
    stage_a, stage_b = {}, {}
    for n in range(n_sub + 2):
        if n < n_sub:
            stage_a[n] = gate_in(n)
        if 1 <= n <= n_sub:
            stage_b[n - 1] = glu(n - 1, *stage_a.pop(n - 1))
        if n >= 2:
            layer_norm(n - 2, project(n - 2, *stage_b.pop(n - 2)))


def _final(x2, ya, sz, gt, wglu, bglu, wo, gain_a, gain_s, lng, lnb):
    n_steps = BATCH * SEQ // TOK_BLOCK

    def tok(cols):
        return pl.BlockSpec((TOK_BLOCK, cols), lambda s: (s, 0))

    def whole(a, **kw):
        return pl.BlockSpec(a.shape, lambda s: (0,) * a.ndim, **kw)

    return pl.pallas_call(
        _final_kernel,
        grid=(n_steps,),
        in_specs=[tok(D_MODEL), tok(D_ATTN), tok(D_ATTN + D_SSM),
                  pl.BlockSpec((None, N_GROUPS, CHUNK_VEC, LANES), lambda s: (s // STEPS_PER_REGROUP, 0, 0, 0)),
                  whole(wglu, pipeline_mode=pl.Buffered(1)), whole(bglu),
                  whole(wo, pipeline_mode=pl.Buffered(1)), whole(gain_a), whole(gain_s), whole(lng), whole(lnb)],
        out_specs=tok(D_MODEL),
        out_shape=jax.ShapeDtypeStruct((BATCH * SEQ, D_MODEL), jnp.float32),
        scratch_shapes=[pltpu.VMEM((N_LANE_BLOCKS, LANES * SLAB_CHUNK_PITCH, LANES), jnp.float32),
                        pltpu.VMEM(wglu.shape, jnp.bfloat16), pltpu.VMEM(wo.shape, jnp.bfloat16)],
        compiler_params=pltpu.CompilerParams(dimension_semantics=("arbitrary",),
                                             vmem_limit_bytes=VMEM_LIMIT),
        name="final",
    )(x2, ya, sz, gt, wglu, bglu, wo, gain_a, gain_s, lng, lnb)


def kernel(x, w_in, attn_sink, ssm_a_re, ssm_a_im, ssm_log_dt, ssm_b_re, ssm_b_im, ssm_c_re, ssm_c_im,
           ssm_d, w_glu, b_glu, norm_attn_g, norm_ssm_g, w_out, ln_g, ln_b):
    assert x.shape == (BATCH, SEQ, D_MODEL) and w_in.shape[0] == 1
    f32 = jnp.float32
    rope = _rope_tables()

    x2 = x.reshape(BATCH * SEQ, D_MODEL)
    b_t = [jnp.swapaxes(p, -1, -2) for p in (ssm_b_re, ssm_b_im)]
    params = [p[0].astype(f32) for p in (ssm_log_dt, ssm_a_re, ssm_a_im, *b_t, ssm_c_re, ssm_c_im)]
    q, kv, sz, ut, m_intra, w_state, w_so, coef = _inproj(x2, w_in[0], rope, *params,
                                                           ssm_d[0].astype(f32).reshape(1, D_SSM))
    ya = _attention(q.reshape(BATCH, SEQ, Q_COLS), kv.reshape(BATCH, SEQ, 2 * KV_COLS),
                    attn_sink[0].astype(f32))
    gt = _ssm(ut, m_intra, w_state, w_so, coef)

    row = lambda v: v[0].astype(f32)[None, :]
    out = _final(x2, ya.reshape(BATCH * SEQ, D_ATTN), sz, gt, w_glu[0].astype(f32), row(b_glu),
                 w_out[0].astype(f32), row(norm_attn_g), row(norm_ssm_g), row(ln_g), row(ln_b))
    return out.reshape(BATCH, SEQ, D_MODEL)
```

```python
import jax
import jax.numpy as jnp
import numpy as np
from jax import lax
from jax.experimental import pallas as pl
from jax.experimental.pallas import tpu as pltpu

D_MODEL = 1024
BATCH = 4
SEQ = 4096
D_ATTN = 512
D_SSM = 512
HEAD_DIM = 64
N_Q_HEADS = 8
N_KV_HEADS = 2
WINDOW = 128
BLOCK = 128
ROPE_THETA = 10000.0
SSM_CH = 16
N_GROUPS = 32
SSM_STATE = 64
NORM_EPS = 1e-5
NEG_INF = -1e30
DEEPNORM_ALPHA = 2.0 ** 0.25
LOG2_E = 1.4426950408889634
Q_SCALE = HEAD_DIM ** -0.5 * LOG2_E

Q_COLS = N_Q_HEADS * HEAD_DIM
KV_COLS = N_KV_HEADS * HEAD_DIM
CHUNK = 16
N_CHUNKS = SEQ // CHUNK
CHUNK_VEC = CHUNK * SSM_CH
STATE_VEC = 2 * SSM_STATE
LANES = 128
ROWS = BATCH * N_CHUNKS
N_ROW_BLOCKS = ROWS // LANES
SLAB_PITCH = N_CHUNKS + 8
SLAB_ROWS = BATCH * SLAB_PITCH + 8
TOK_BLOCK = 1024
REGROUP_TOK = LANES * CHUNK
STEPS_PER_REGROUP = REGROUP_TOK // TOK_BLOCK
SUB_ROWS = 512
N_LANE_BLOCKS = D_SSM // LANES
SLAB_CHUNK_PITCH = 20
GROUPS_PER_LANE_BLOCK = LANES // SSM_CH
VMEM_LIMIT = 56 * 1024 * 1024
HIGHEST = lax.Precision.HIGHEST


LAG_ROWS = 512
PREP_GB = N_GROUPS // (BATCH * SEQ // TOK_BLOCK)


def _ssm_tables(ldt_ref, are_ref, aim_ref, bre_ref, bim_ref, cre_ref, cim_ref, d_ref,
                m_ref, ws_ref, wo_ref, coef_ref):
    f32 = jnp.float32
    lo16 = lax.broadcasted_iota(jnp.int32, (SSM_CH, LANES), 1) < SSM_STATE
    lo1 = lax.broadcasted_iota(jnp.int32, (1, LANES), 1) < SSM_STATE
    units = [(gi, d) for gi in range(PREP_GB) for d in range(2)]

    def group(gi):
        return pl.program_id(0) * PREP_GB + gi

    def dup(v):
        return jnp.concatenate([v, v], axis=1)

    lam, zoh = {}, {}
    for gi, d in units:
        ar = dup(are_ref[d, pl.ds(group(gi), 1), :])
        ai = dup(aim_ref[d, pl.ds(group(gi), 1), :])
        dt = jnp.exp(jnp.full((1, LANES), ldt_ref[d, group(gi)], f32))
        zr = dt * ar
        zi = dt * ai
        mag = jnp.exp(zr)
        lr = mag * jnp.cos(zi)
        li = mag * jnp.sin(zi)
        den = ar * ar + ai * ai
        lam[gi, d] = (lr, li)
        zoh[gi, d] = (((lr - 1.0) * ar + li * ai) / den, (li * ar - (lr - 1.0) * ai) / den)

    pr = {u: [jnp.ones((1, LANES), f32)] for u in units}
    pi = {u: [jnp.zeros((1, LANES), f32)] for u in units}
    for _ in range(CHUNK):
        for u in units:
            lr, li = lam[u]
            pr[u].append(pr[u][-1] * lr - pi[u][-1] * li)
            pi[u].append(pr[u][-2] * li + pi[u][-1] * lr)

    lag_tabs, kws = {}, {}
    for gi in range(PREP_GB):
        ws_parts, coef_rows = [], []
        for d in range(2):
            u = (gi, d)
            fr, fi = zoh[u]
            btr = dup(bre_ref[d, gi])
            bti = dup(bim_ref[d, gi])
            bbr = fr * btr - fi * bti
            bbi = fr * bti + fi * btr
            cr = dup(cre_ref[d, gi])
            ci = dup(cim_ref[d, gi])
            pa = [jnp.where(lo1, r, i) for r, i in zip(pr[u], pi[u])]
            pb = [jnp.where(lo1, -i, r) for r, i in zip(pr[u], pi[u])]
            pc = [jnp.where(lo1, i, r) for r, i in zip(pr[u], pi[u])]
            pd = [jnp.where(lo1, r, -i) for r, i in zip(pr[u], pi[u])]

            order = range(CHUNK + 1) if d == 0 else range(CHUNK, -1, -1)
            blocks = [cr * pa[t] + ci * pb[t] for t in order]
            blocks.append(jnp.zeros((LAG_ROWS - (CHUNK + 1) * SSM_CH, LANES), f32))
            lag_tabs[u] = jnp.concatenate(blocks, axis=0)
            lhs = jnp.where(lo16, bbr, -bbi)
            kws[u] = lax.dot_general(lhs, lag_tabs[u], (((1,), (1,)), ((), ())),
                                     precision=HIGHEST, preferred_element_type=f32)

            parts = []
            for j in range(CHUNK):
                t = CHUNK - 1 - j if d == 0 else j
                parts.append(jnp.concatenate([bbr * pa[t] + bbi * pb[t],
                                              bbr * pc[t] + bbi * pd[t]], axis=1))
            ws_parts.append(jnp.concatenate(parts, axis=0))

            a2 = jnp.where(lo1, -pi[u][CHUNK], pi[u][CHUNK])
            coef_rows += [pr[u][CHUNK], a2, -a2]

        ws_ref[gi] = jnp.concatenate(ws_parts, axis=1).astype(ws_ref.dtype)
        coef_ref[gi] = jnp.concatenate(coef_rows + [jnp.zeros((2, LANES), f32)], axis=0)

    sign = jnp.where(lax.broadcasted_iota(jnp.int32, (LANES, CHUNK_VEC), 0) < SSM_STATE, 1.0, -1.0)
    sub = lax.broadcasted_iota(jnp.int32, (SSM_CH, CHUNK_VEC), 0)
    lane = lax.broadcasted_iota(jnp.int32, (SSM_CH, CHUNK_VEC), 1)
    for gi in range(PREP_GB):
        wo_f = jnp.transpose(lag_tabs[gi, 0][SSM_CH:SSM_CH + CHUNK_VEC, :]) * sign
        wo_b = jnp.transpose(lag_tabs[gi, 1][:CHUNK_VEC, :]) * sign
        wo_ref[gi] = jnp.concatenate([wo_f, wo_b], axis=0).astype(wo_ref.dtype)

        g = group(gi)
        dblk = d_ref[:, pl.ds(pl.multiple_of((g // GROUPS_PER_LANE_BLOCK) * LANES, LANES), LANES)]
        dblk = pltpu.roll(dblk, (LANES - (g % GROUPS_PER_LANE_BLOCK) * SSM_CH) % LANES, 1)
        dblk = jnp.where(lax.broadcasted_iota(jnp.int32, (1, LANES), 1) < SSM_CH, dblk, 0.0)
        for shift in (SSM_CH, 2 * SSM_CH, 4 * SSM_CH):
            dblk = dblk + pltpu.roll(dblk, shift, 1)
        dtile = jnp.broadcast_to(jnp.concatenate([dblk, dblk], axis=1), (SSM_CH, CHUNK_VEC))
        for jp in range(CHUNK):
            fwd = kws[gi, 0] if jp == 0 else pltpu.roll(kws[gi, 0], SSM_CH * jp, 1)
            bwd = pltpu.roll(kws[gi, 1], (LAG_ROWS - SSM_CH * (CHUNK - jp)) % LAG_ROWS, 1)
            skip = jnp.where((lane // SSM_CH == jp) & (lane % SSM_CH == sub), dtile, 0.0)
            blk = fwd[:, :CHUNK_VEC] + bwd[:, :CHUNK_VEC] + skip
            m_ref[gi, jp * SSM_CH:(jp + 1) * SSM_CH, :] = blk.astype(m_ref.dtype)


def _rope_tables():
    half = HEAD_DIM // 2
    inv_freq = ROPE_THETA ** (-np.arange(half, dtype=np.float64) / half)
    ang = np.arange(SEQ, dtype=np.float64)[:, None] * inv_freq[None, :]
    cos, sin = np.cos(ang), np.sin(ang)
    return jnp.asarray(np.concatenate([cos, cos, cos, cos, -sin, sin, -sin, sin], axis=1), jnp.float32)


def _rope_block(xb, cos_t, sin_t, first_half):
    swapped = jnp.where(first_half, pltpu.roll(xb, 96, 1), pltpu.roll(xb, 32, 1))
    return xb * cos_t + swapped * sin_t


COL_Q = (0, Q_COLS)
COL_KV = (COL_Q[1], COL_Q[1] + 2 * KV_COLS)
COL_ZA = (COL_KV[1], COL_KV[1] + D_ATTN)
COL_U = (COL_ZA[1], COL_ZA[1] + D_SSM)
COL_ZS = (COL_U[1], COL_U[1] + D_SSM)


def _silu(z):
    h = 0.5 * z
    return h + h * jnp.tanh(h)


def _inproj_kernel(x_ref, w32_ref, rope_ref, ldt_ref, are_ref, aim_ref, bre_ref, bim_ref, cre_ref, cim_ref, d_ref,
                   q_ref, kv_ref, sz_ref, ut_ref, m_ref, ws_ref, wo_ref, coef_ref, uslab_ref, w_ref):
    step = pl.program_id(0)
    part = step % STEPS_PER_REGROUP
    lane = lax.broadcasted_iota(jnp.int32, (SUB_ROWS, LANES), 1)
    first_half = (lane % HEAD_DIM) < (HEAD_DIM // 2)

    @pl.when(step == 0)
    def _():
        for c0 in range(0, w_ref.shape[1], 2 * LANES):
            w_ref[:, c0:c0 + 2 * LANES] = w32_ref[:, c0:c0 + 2 * LANES].astype(w_ref.dtype)

    def proj(xb, cols):
        return jnp.dot(xb, w_ref[:, cols[0]:cols[1]], preferred_element_type=jnp.float32)

    for sub in range(TOK_BLOCK // SUB_ROWS):
        rows = slice(sub * SUB_ROWS, (sub + 1) * SUB_ROWS)
        xb = x_ref[rows, :].astype(jnp.bfloat16)
        cos_t = rope_ref[rows, :LANES]
        sin_t = rope_ref[rows, LANES:]
        q = proj(xb, COL_Q)
        kv = proj(xb, COL_KV)
        q_blocks = [_rope_block(q[:, v * LANES:(v + 1) * LANES], cos_t, sin_t, first_half)
                    for v in range(Q_COLS // LANES)]
        q_ref[rows, :] = (jnp.concatenate(q_blocks, axis=1) * Q_SCALE).astype(q_ref.dtype)
        za = proj(xb, COL_ZA)
        k_rot = _rope_block(kv[:, :LANES], cos_t, sin_t, first_half)
        kv_ref[rows, :] = jnp.concatenate([k_rot, kv[:, LANES:]], axis=1).astype(kv_ref.dtype)
        u = proj(xb, COL_U)
        sz_ref[rows, :D_ATTN] = _silu(za).astype(sz_ref.dtype)
        zs = proj(xb, COL_ZS)
        for kk in range(SUB_ROWS // CHUNK):
            base = pl.multiple_of((part * (TOK_BLOCK // CHUNK) + sub * (SUB_ROWS // CHUNK)) * SLAB_CHUNK_PITCH, 8)
            for v in range(N_LANE_BLOCKS):
                uslab_ref[v, pl.ds(base + kk * SLAB_CHUNK_PITCH, CHUNK), :] = (
                    u[kk * CHUNK:(kk + 1) * CHUNK, v * LANES:(v + 1) * LANES])
        sz_ref[rows, D_ATTN:] = _silu(zs).astype(sz_ref.dtype)

    _ssm_tables(ldt_ref, are_ref, aim_ref, bre_ref, bim_ref, cre_ref, cim_ref, d_ref, m_ref, ws_ref, wo_ref, coef_ref)

    @pl.when(part == STEPS_PER_REGROUP - 1)
    def _():
        for j in range(CHUNK):
            for v in range(N_LANE_BLOCKS):
                uj = uslab_ref[v, pl.ds(j, LANES, stride=SLAB_CHUNK_PITCH), :]
                ujt = jnp.transpose(uj.astype(ut_ref.dtype)).reshape(GROUPS_PER_LANE_BLOCK, SSM_CH, LANES)
                g0 = v * GROUPS_PER_LANE_BLOCK
                ut_ref[g0:g0 + GROUPS_PER_LANE_BLOCK, j * SSM_CH:(j + 1) * SSM_CH, :] = ujt


def _inproj(x2, w, rope, log_dt, a_re, a_im, b_re, b_im, c_re, c_im, d):
    n_steps = BATCH * SEQ // TOK_BLOCK
    blocks_per_seq = SEQ // TOK_BLOCK
    bf16 = jnp.bfloat16

    def tok(cols):
        return pl.BlockSpec((TOK_BLOCK, cols), lambda s: (s, 0))

    def whole(a):
        return pl.BlockSpec(a.shape, lambda s: (0,) * a.ndim)

    def group_in(a):
        return pl.BlockSpec((2, PREP_GB) + a.shape[2:], lambda s: (0, s, 0, 0))

    def group_out(rows, cols):
        return pl.BlockSpec((PREP_GB, rows, cols), lambda s: (s, 0, 0))

    return pl.pallas_call(
        _inproj_kernel,
        grid=(n_steps,),
        in_specs=[tok(D_MODEL), pl.BlockSpec(w.shape, lambda s: (0, 0), pipeline_mode=pl.Buffered(1)),
                  pl.BlockSpec((TOK_BLOCK, 2 * LANES), lambda s: (s % blocks_per_seq, 0)),
                  pl.BlockSpec(memory_space=pltpu.SMEM), whole(a_re), whole(a_im),
                  group_in(b_re), group_in(b_im), group_in(c_re), group_in(c_im), whole(d)],
        out_specs=[tok(Q_COLS), tok(2 * KV_COLS), tok(D_ATTN + D_SSM),
                   pl.BlockSpec((None, N_GROUPS, CHUNK_VEC, LANES), lambda s: (s // STEPS_PER_REGROUP, 0, 0, 0)),
                   group_out(CHUNK_VEC, CHUNK_VEC), group_out(CHUNK_VEC, 4 * LANES),
                   group_out(2 * STATE_VEC, CHUNK_VEC), group_out(8, LANES)],
        out_shape=[jax.ShapeDtypeStruct((BATCH * SEQ, Q_COLS), bf16),
                   jax.ShapeDtypeStruct((BATCH * SEQ, 2 * KV_COLS), bf16),
                   jax.ShapeDtypeStruct((BATCH * SEQ, D_ATTN + D_SSM), bf16),
                   jax.ShapeDtypeStruct((N_ROW_BLOCKS, N_GROUPS, CHUNK_VEC, LANES), bf16),
                   jax.ShapeDtypeStruct((N_GROUPS, CHUNK_VEC, CHUNK_VEC), bf16),
                   jax.ShapeDtypeStruct((N_GROUPS, CHUNK_VEC, 4 * LANES), bf16),
                   jax.ShapeDtypeStruct((N_GROUPS, 2 * STATE_VEC, CHUNK_VEC), bf16),
                   jax.ShapeDtypeStruct((N_GROUPS, 8, LANES), jnp.float32)],
        scratch_shapes=[pltpu.VMEM((N_LANE_BLOCKS, LANES * SLAB_CHUNK_PITCH, LANES), jnp.float32),
                        pltpu.VMEM(w.shape, bf16)],
        compiler_params=pltpu.CompilerParams(dimension_semantics=("arbitrary",),
                                             vmem_limit_bytes=VMEM_LIMIT),
        name="inproj",
    )(x2, w, rope, log_dt, a_re, a_im, b_re, b_im, c_re, c_im, d)


ATTN_SUB = 32
ATTN_STEP = ATTN_SUB * BLOCK
N_KEYS = 3 * BLOCK


def _attn_kernel(q_ref, kp_ref, kc_ref, kn_ref, sink_ref, y_ref):
    i = pl.program_id(1)
    kvw = jnp.concatenate([kp_ref[...], kc_ref[...], kn_ref[...]], axis=0).astype(jnp.float32)

    n_win = kvw.shape[0]
    lane_w = lax.broadcasted_iota(jnp.int32, (n_win, LANES), 1)
    k_lo, k_hi, vt_lo, vt_hi = [], [], [], []
    for hk in range(N_KV_HEADS):
        in_head = (lane_w >= hk * HEAD_DIM) & (lane_w < (hk + 1) * HEAD_DIM)
        k_nat = jnp.where(in_head, kvw[:, :LANES], 0.0)
        v_nat = jnp.where(in_head, kvw[:, LANES:], 0.0)
        k_oth = pltpu.roll(k_nat, HEAD_DIM, 1)
        v_oth = pltpu.roll(v_nat, HEAD_DIM, 1)
        pairs = ((k_nat, k_oth), (v_nat, v_oth)) if hk == 0 else ((k_oth, k_nat), (v_oth, v_nat))
        k_lo.append(pairs[0][0].astype(jnp.bfloat16))
        k_hi.append(pairs[0][1].astype(jnp.bfloat16))
        vt_lo.append(jnp.transpose(pairs[1][0]).astype(jnp.bfloat16))
        vt_hi.append(jnp.transpose(pairs[1][1]).astype(jnp.bfloat16))

    c = lax.broadcasted_iota(jnp.int32, (N_KEYS, BLOCK), 0)
    r = lax.broadcasted_iota(jnp.int32, (N_KEYS, BLOCK), 1)
    band = (c >= r) & (c - r <= 2 * WINDOW)
    ones_row = lax.broadcasted_iota(jnp.int32, (16, 2 * N_KEYS), 0)
    ones_col = lax.broadcasted_iota(jnp.int32, (16, 2 * N_KEYS), 1)
    ones = jnp.where(((ones_row == 0) & (ones_col < N_KEYS)) | ((ones_row == 1) & (ones_col >= N_KEYS)),
                     1.0, 0.0).astype(jnp.bfloat16)
    row_o = lax.broadcasted_iota(jnp.int32, (LANES, 2 * BLOCK), 0)
    n_blocks = SEQ // BLOCK

    def scores(t, hk):
        blk = i * ATTN_SUB + t
        lo = jnp.where(blk == 0, BLOCK, 0)
        hi = jnp.where(blk == n_blocks - 1, 2 * BLOCK, N_KEYS)
        valid1 = band & (c >= lo) & (c < hi)
        valid = jnp.concatenate([valid1, valid1], axis=1)
        win = slice(t * BLOCK, t * BLOCK + N_KEYS)
        q = q_ref[t * BLOCK:(t + 1) * BLOCK, (2 * hk) * LANES:(2 * hk + 2) * LANES]
        qq = jnp.concatenate([q[:, :LANES], q[:, LANES:]], axis=0)
        kcat = jnp.concatenate([k_lo[hk][win], k_hi[hk][win]], axis=0)
        st = lax.dot_general(kcat, qq, (((1,), (1,)), ((), ())),
                             preferred_element_type=jnp.float32)
        sps, ms, sinks = [], [], []
        for par in range(2):
            base = par * N_KEYS
            sp = jnp.concatenate(
                [jnp.where(valid[:BLOCK], st[base:base + BLOCK, :], NEG_INF),
                 st[base + BLOCK:base + 2 * BLOCK, :],
                 jnp.where(valid[2 * BLOCK:], st[base + 2 * BLOCK:base + N_KEYS, :], NEG_INF)], axis=0)
            h0 = 4 * hk + par
            h1 = h0 + 2
            sink = jnp.concatenate([jnp.full((1, BLOCK), sink_ref[h] * LOG2_E, jnp.float32)
                                    for h in (h0, h1)], axis=1)
            sps.append(sp)
            ms.append(jnp.maximum(jnp.max(sp, axis=0, keepdims=True), sink))
            sinks.append(sink)
        return sps, ms, sinks

    def weights(sps, ms, sinks):
        return [jnp.exp2((sp - m).astype(jnp.bfloat16)) for sp, m in zip(sps, ms)], ms, sinks

    def finish(t, hk, ps, ms, sinks):
        win = slice(t * BLOCK, t * BLOCK + N_KEYS)
        vt = jnp.concatenate([jnp.concatenate([vt_lo[hk][:, win], vt_hi[hk][:, win]], axis=1), ones], axis=0)
        ot = jnp.dot(vt, jnp.concatenate(ps, axis=0), preferred_element_type=jnp.float32)
        inv0 = 1.0 / (ot[LANES:LANES + 1, :] + jnp.exp2(sinks[0] - ms[0]))
        inv1 = 1.0 / (ot[LANES + 1:LANES + 2, :] + jnp.exp2(sinks[1] - ms[1]))
        o = jnp.transpose(ot[:LANES, :] * jnp.where(row_o < HEAD_DIM, inv0, inv1))
        rows = slice(t * BLOCK, (t + 1) * BLOCK)
        y_ref[rows, (2 * hk) * LANES:(2 * hk + 1) * LANES] = o[:BLOCK].astype(y_ref.dtype)
        y_ref[rows, (2 * hk + 1) * LANES:(2 * hk + 2) * LANES] = o[BLOCK:].astype(y_ref.dtype)

    work = [(t, hk) for t in range(ATTN_SUB) for hk in range(N_KV_HEADS)]
    stage_a = {0: scores(*work[0])}
    stage_b = {}
    for n in range(len(work) + 1):
        if n + 1 < len(work):
            stage_a[n + 1] = scores(*work[n + 1])
        if n < len(work):
            stage_b[n] = weights(*stage_a.pop(n))
        if n >= 1:
            finish(*work[n - 1], *stage_b.pop(n - 1))


def _attention(q, kv, sink):
    nb = SEQ // BLOCK
    return pl.pallas_call(
        _attn_kernel,
        grid=(BATCH, SEQ // ATTN_STEP),
        in_specs=[pl.BlockSpec((None, ATTN_STEP, Q_COLS), lambda b, i: (b, i, 0)),
                  pl.BlockSpec((None, BLOCK, 2 * KV_COLS),
                               lambda b, i: (b, jnp.maximum(i * ATTN_SUB - 1, 0), 0)),
                  pl.BlockSpec((None, ATTN_STEP, 2 * KV_COLS), lambda b, i: (b, i, 0)),
                  pl.BlockSpec((None, BLOCK, 2 * KV_COLS),
                               lambda b, i: (b, jnp.minimum((i + 1) * ATTN_SUB, nb - 1), 0)),
                  pl.BlockSpec(memory_space=pltpu.SMEM)],
        out_specs=pl.BlockSpec((None, ATTN_STEP, D_ATTN), lambda b, i: (b, i, 0)),
        out_shape=jax.ShapeDtypeStruct((BATCH, SEQ, D_ATTN), jnp.bfloat16),
        compiler_params=pltpu.CompilerParams(dimension_semantics=("arbitrary", "arbitrary"),
                                             vmem_limit_bytes=VMEM_LIMIT),
        name="attention",
    )(q, kv, kv, kv, sink)


SSM_GB = 4
SSM_STEPS = N_GROUPS // SSM_GB


def _ssm_kernel(ut_ref, m_ref, ws_ref, wo_ref, coef_ref, gt_ref,
                sf_ref, sfs_ref, sb_ref, sbs_ref, xf_ref, xb_ref, u_even_ref, u_odd_ref):
    s = pl.program_id(0)

    def head(gi, u_ref):
        ut = jnp.concatenate([ut_ref[lb, gi] for lb in range(N_ROW_BLOCKS)], axis=1)
        u = jnp.transpose(ut)
        u_ref[gi] = u
        st = jnp.dot(u, ws_ref[gi], preferred_element_type=jnp.float32)
        for b in range(BATCH):
            rows = slice(b * N_CHUNKS, (b + 1) * N_CHUNKS)
            dst = pl.ds(b * SLAB_PITCH + 8, N_CHUNKS)
            sf_ref[gi, dst, :] = st[rows, 0 * LANES:1 * LANES]
            sfs_ref[gi, dst, :] = st[rows, 1 * LANES:2 * LANES]
            sb_ref[gi, dst, :] = st[rows, 2 * LANES:3 * LANES]
            sbs_ref[gi, dst, :] = st[rows, 3 * LANES:4 * LANES]

    def chunk_out(gi, u_ref):
        xin_f = jnp.concatenate([xf_ref[gi, pl.ds(b * SLAB_PITCH + 8, N_CHUNKS), :] for b in range(BATCH)], axis=0)
        xin_b = jnp.concatenate([xb_ref[gi, pl.ds(b * SLAB_PITCH + 8, N_CHUNKS), :] for b in range(BATCH)], axis=0)
        lhs = jnp.concatenate([u_ref[gi], xin_f.astype(jnp.bfloat16), xin_b.astype(jnp.bfloat16)], axis=1)
        rhs = jnp.concatenate([m_ref[gi], wo_ref[gi]], axis=0)
        return jnp.dot(lhs, rhs, preferred_element_type=jnp.float32)

    def store_out(gi, y):
        gt = jnp.transpose(jax.nn.gelu(y).astype(gt_ref.dtype))
        for lb in range(N_ROW_BLOCKS):
            gt_ref[lb, gi] = gt[:, lb * LANES:(lb + 1) * LANES]

    def head_and_tail(u_head_ref, u_tail_ref):
        y = chunk_out(0, u_tail_ref) if u_tail_ref is not None else None
        for gi in range(SSM_GB):
            if u_head_ref is not None:
                head(gi, u_head_ref)
            if u_tail_ref is not None:
                y_next = chunk_out(gi + 1, u_tail_ref) if gi + 1 < SSM_GB else None
                store_out(gi, y)
                y = y_next

    u_refs = (u_even_ref, u_odd_ref)
    inner = (s > 0) & (s < SSM_STEPS)
    pl.when(s == 0)(lambda: head_and_tail(u_refs[0], None))
    pl.when(inner & (s % 2 == 0))(lambda: head_and_tail(u_refs[0], u_refs[1]))
    pl.when(inner & (s % 2 == 1))(lambda: head_and_tail(u_refs[1], u_refs[0]))
    pl.when(s == SSM_STEPS)(lambda: head_and_tail(None, u_refs[(SSM_STEPS - 1) % 2]))

    @pl.when(s < SSM_STEPS)
    def _():
        zero = jnp.zeros((BATCH, LANES), jnp.float32)
        coefs = []
        for gi in range(SSM_GB):
            coefs.append([jnp.broadcast_to(coef_ref[gi, r:r + 1, :], (BATCH, LANES)) for r in range(6)])
            xf_ref[gi, pl.ds(8, BATCH, stride=SLAB_PITCH), :] = zero
            xb_ref[gi, pl.ds(8 + N_CHUNKS - 1, BATCH, stride=SLAB_PITCH), :] = zero

        def rows_at(r):
            return pl.ds(r, BATCH, stride=SLAB_PITCH)

        def step(k, carry):
            kb = N_CHUNKS - 1 - k
            out = []
            for gi in range(SSM_GB):
                a1f, a2f, a3f, a1b, a2b, a3b = coefs[gi]
                xf, xfs, xb, xbs = carry[4 * gi:4 * gi + 4]
                nxf = a1f * xf + a2f * xfs + sf_ref[gi, rows_at(8 + k), :]
                nxfs = a1f * xfs + a3f * xf + sfs_ref[gi, rows_at(8 + k), :]
                xf_ref[gi, rows_at(9 + k), :] = nxf
                nxb = a1b * xb + a2b * xbs + sb_ref[gi, rows_at(8 + kb), :]
                nxbs = a1b * xbs + a3b * xb + sbs_ref[gi, rows_at(8 + kb), :]
                xb_ref[gi, rows_at(7 + kb), :] = nxb
                out += [nxf, nxfs, nxb, nxbs]
            return tuple(out)

        lax.fori_loop(0, N_CHUNKS, step, (zero,) * (4 * SSM_GB), unroll=8)


def _ssm(ut, m_intra, w_state, w_out, coef):
    def head_block(s):
        return jnp.minimum(s, SSM_STEPS - 1)

    def tail_block(s):
        return jnp.maximum(s - 1, 0)

    def per_group(shape, block):
        return pl.BlockSpec((SSM_GB,) + shape, lambda s: (block(s), 0, 0))

    def lane_blocked(block):
        return pl.BlockSpec((N_ROW_BLOCKS, SSM_GB, CHUNK_VEC, LANES), lambda s: (0, block(s), 0, 0))

    slab = pltpu.VMEM((SSM_GB, SLAB_ROWS, LANES), jnp.float32)
    return pl.pallas_call(
        _ssm_kernel,
        grid=(SSM_STEPS + 1,),
        in_specs=[lane_blocked(head_block), per_group((CHUNK_VEC, CHUNK_VEC), tail_block),
                  per_group((CHUNK_VEC, 4 * LANES), head_block), per_group((2 * STATE_VEC, CHUNK_VEC), tail_block),
                  per_group((8, LANES), head_block)],
        out_specs=lane_blocked(tail_block),
        out_shape=jax.ShapeDtypeStruct((N_ROW_BLOCKS, N_GROUPS, CHUNK_VEC, LANES), jnp.bfloat16),
        scratch_shapes=[slab] * 6 + [pltpu.VMEM((SSM_GB, ROWS, CHUNK_VEC), jnp.bfloat16)] * 2,
        compiler_params=pltpu.CompilerParams(dimension_semantics=("arbitrary",),
                                             vmem_limit_bytes=VMEM_LIMIT),
        name="ssm",
    )(ut, m_intra, w_state, w_out, coef)


FINAL_SUB_ROWS = 256


def _unit_rms(v):
    return v * lax.rsqrt(jnp.mean(v * v, axis=-1, keepdims=True) + NORM_EPS)


def _final_kernel(x_ref, ya_ref, sz_ref, gt_ref, wglu32_ref, bglu_ref, wo32_ref, ga_ref, gs_ref,
                  lng_ref, lnb_ref, o_ref, gslab_ref, wglu_ref, wo_ref):
    step = pl.program_id(0)
    part = step % STEPS_PER_REGROUP

    @pl.when(step == 0)
    def _():
        wglu_ref[...] = wglu32_ref[...].astype(wglu_ref.dtype)
        for gain_ref, row0 in ((ga_ref, 0), (gs_ref, D_ATTN)):
            for v in range(gain_ref.shape[1] // LANES):
                gain = gain_ref[:, v * LANES:(v + 1) * LANES] / DEEPNORM_ALPHA
                col = jnp.transpose(jnp.broadcast_to(gain, (LANES, LANES)))
                rows = slice(row0 + v * LANES, row0 + (v + 1) * LANES)
                for c0 in range(0, D_MODEL, LANES):
                    wo_ref[rows, c0:c0 + LANES] = (wo32_ref[rows, c0:c0 + LANES] * col).astype(wo_ref.dtype)

    @pl.when(part == 0)
    def _():
        for j in range(CHUNK):
            for v in range(N_LANE_BLOCKS):
                g0 = v * GROUPS_PER_LANE_BLOCK
                gjt = gt_ref[g0:g0 + GROUPS_PER_LANE_BLOCK, j * SSM_CH:(j + 1) * SSM_CH, :]
                gj = jnp.transpose(gjt.reshape(LANES, LANES))
                gslab_ref[v, pl.ds(j, LANES, stride=SLAB_CHUNK_PITCH), :] = gj.astype(jnp.float32)

    def rows_of(sub):
        return slice(sub * FINAL_SUB_ROWS, (sub + 1) * FINAL_SUB_ROWS)

    def gate_in(sub):
        rows = rows_of(sub)
        ya = ya_ref[rows, :].astype(jnp.float32) * sz_ref[rows, :D_ATTN].astype(jnp.float32)
        ya = _unit_rms(ya).astype(jnp.bfloat16)
        base = pl.multiple_of((part * (TOK_BLOCK // CHUNK) + sub * (FINAL_SUB_ROWS // CHUNK)) * SLAB_CHUNK_PITCH, 8)
        g = jnp.concatenate(
            [jnp.concatenate([gslab_ref[v, pl.ds(base + kk * SLAB_CHUNK_PITCH, CHUNK), :]
                              for kk in range(FINAL_SUB_ROWS // CHUNK)], axis=0)
             for v in range(N_LANE_BLOCKS)], axis=1)
        return ya, g

    def glu(sub, ya, g):
        rows = rows_of(sub)
        gate = jnp.dot(g.astype(jnp.bfloat16), wglu_ref[...], preferred_element_type=jnp.float32) + bglu_ref[...]
        ys = g * jax.nn.sigmoid(gate)
        ys = ys * sz_ref[rows, D_ATTN:].astype(jnp.float32)
        return ya, _unit_rms(ys).astype(jnp.bfloat16)

    def project(sub, ya, ys):
        return (jnp.dot(ya, wo_ref[:D_ATTN, :], preferred_element_type=jnp.float32)
                + jnp.dot(ys, wo_ref[D_ATTN:, :], preferred_element_type=jnp.float32))

    def layer_norm(sub, out):
        rows = rows_of(sub)
        h = x_ref[rows, :] + out
        mu = jnp.mean(h, axis=-1, keepdims=True)
        hc = h - mu
        var = jnp.mean(hc * hc, axis=-1, keepdims=True)
        o_ref[rows, :] = hc * lax.rsqrt(var + NORM_EPS / DEEPNORM_ALPHA ** 2) * lng_ref[...] + lnb_ref[...]

    n_sub = TOK_BLOCK // FINAL_SUB_ROWS
    stage_a, stage_b = {}, {}
    for n in range(n_sub + 2):
        if n < n_sub:
            stage_a[n] = gate_in(n)
        if 1 <= n <= n_sub:
            stage_b[n - 1] = glu(n - 1, *stage_a.pop(n - 1))
        if n >= 2:
            layer_norm(n - 2, project(n - 2, *stage_b.pop(n - 2)))


def _final(x2, ya, sz, gt, wglu, bglu, wo, gain_a, gain_s, lng, lnb):
    n_steps = BATCH * SEQ // TOK_BLOCK

    def tok(cols):
        return pl.BlockSpec((TOK_BLOCK, cols), lambda s: (s, 0))

    def whole(a, **kw):
        return pl.BlockSpec(a.shape, lambda s: (0,) * a.ndim, **kw)

    return pl.pallas_call(
        _final_kernel,
        grid=(n_steps,),
        in_specs=[tok(D_MODEL), tok(D_ATTN), tok(D_ATTN + D_SSM),
                  pl.BlockSpec((None, N_GROUPS, CHUNK_VEC, LANES), lambda s: (s // STEPS_PER_REGROUP, 0, 0, 0)),
                  whole(wglu, pipeline_mode=pl.Buffered(1)), whole(bglu),
                  whole(wo, pipeline_mode=pl.Buffered(1)), whole(gain_a), whole(gain_s), whole(lng), whole(lnb)],
        out_specs=tok(D_MODEL),
        out_shape=jax.ShapeDtypeStruct((BATCH * SEQ, D_MODEL), jnp.float32),
        scratch_shapes=[pltpu.VMEM((N_LANE_BLOCKS, LANES * SLAB_CHUNK_PITCH, LANES), jnp.float32),
                        pltpu.VMEM(wglu.shape, jnp.bfloat16), pltpu.VMEM(wo.shape, jnp.bfloat16)],
        compiler_params=pltpu.CompilerParams(dimension_semantics=("arbitrary",),
                                             vmem_limit_bytes=VMEM_LIMIT),
        name="final",
    )(x2, ya, sz, gt, wglu, bglu, wo, gain_a, gain_s, lng, lnb)


def kernel(x, w_in, attn_sink, ssm_a_re, ssm_a_im, ssm_log_dt, ssm_b_re, ssm_b_im, ssm_c_re, ssm_c_im,
           ssm_d, w_glu, b_glu, norm_attn_g, norm_ssm_g, w_out, ln_g, ln_b):
    assert x.shape == (BATCH, SEQ, D_MODEL) and w_in.shape[0] == 1
    f32 = jnp.float32
    rope = _rope_tables()

    x2 = x.reshape(BATCH * SEQ, D_MODEL)
    b_t = [jnp.swapaxes(p, -1, -2) for p in (ssm_b_re, ssm_b_im)]
    params = [p[0].astype(f32) for p in (ssm_log_dt, ssm_a_re, ssm_a_im, *b_t, ssm_c_re, ssm_c_im)]
    q, kv, sz, ut, m_intra, w_state, w_so, coef = _inproj(x2, w_in[0], rope, *params,
                                                           ssm_d[0].astype(f32).reshape(1, D_SSM))
    ya = _attention(q.reshape(BATCH, SEQ, Q_COLS), kv.reshape(BATCH, SEQ, 2 * KV_COLS),
                    attn_sink[0].astype(f32))
    gt = _ssm(ut, m_intra, w_state, w_so, coef)

    row = lambda v: v[0].astype(f32)[None, :]
    out = _final(x2, ya.reshape(BATCH * SEQ, D_ATTN), sz, gt, w_glu[0].astype(f32), row(b_glu),
                 w_out[0].astype(f32), row(norm_attn_g), row(norm_ssm_g), row(ln_g), row(ln_b))
    return out.reshape(BATCH, SEQ, D_MODEL)
```

```python
import jax
import jax.numpy as jnp
import numpy as np
from jax import lax
from jax.experimental import pallas as pl
from jax.experimental.pallas import tpu as pltpu

D_MODEL = 1024
BATCH = 4
SEQ = 4096
D_ATTN = 512
D_SSM = 512
HEAD_DIM = 64
N_Q_HEADS = 8
N_KV_HEADS = 2
WINDOW = 128
BLOCK = 128
ROPE_THETA = 10000.0
SSM_CH = 16
N_GROUPS = 32
SSM_STATE = 64
NORM_EPS = 1e-5
NEG_INF = -1e30
DEEPNORM_ALPHA = 2.0 ** 0.25
LOG2_E = 1.4426950408889634
Q_SCALE = HEAD_DIM ** -0.5 * LOG2_E

Q_COLS = N_Q_HEADS * HEAD_DIM
KV_COLS = N_KV_HEADS * HEAD_DIM
CHUNK = 16
N_CHUNKS = SEQ // CHUNK
CHUNK_VEC = CHUNK * SSM_CH
STATE_VEC = 2 * SSM_STATE
LANES = 128
ROWS = BATCH * N_CHUNKS
N_ROW_BLOCKS = ROWS // LANES
SLAB_PITCH = N_CHUNKS + 8
SLAB_ROWS = BATCH * SLAB_PITCH + 8
TOK_BLOCK = 1024
REGROUP_TOK = LANES * CHUNK
STEPS_PER_REGROUP = REGROUP_TOK // TOK_BLOCK
SUB_ROWS = 1024
N_LANE_BLOCKS = D_SSM // LANES
SLAB_CHUNK_PITCH = 20
GROUPS_PER_LANE_BLOCK = LANES // SSM_CH
VMEM_LIMIT = 56 * 1024 * 1024
HIGHEST = lax.Precision.HIGHEST


LAG_ROWS = 512
PREP_GB = N_GROUPS // (BATCH * SEQ // TOK_BLOCK)


def _ssm_tables(ldt_ref, are_ref, aim_ref, bre_ref, bim_ref, cre_ref, cim_ref, d_ref,
                m_ref, ws_ref, wo_ref, coef_ref):
    f32 = jnp.float32
    lo16 = lax.broadcasted_iota(jnp.int32, (SSM_CH, LANES), 1) < SSM_STATE
    lo1 = lax.broadcasted_iota(jnp.int32, (1, LANES), 1) < SSM_STATE
    units = [(gi, d) for gi in range(PREP_GB) for d in range(2)]

    def group(gi):
        return pl.program_id(0) * PREP_GB + gi

    def dup(v):
        return jnp.concatenate([v, v], axis=1)

    lam, zoh = {}, {}
    for gi, d in units:
        ar = dup(are_ref[d, pl.ds(group(gi), 1), :])
        ai = dup(aim_ref[d, pl.ds(group(gi), 1), :])
        dt = jnp.exp(jnp.full((1, LANES), ldt_ref[d, group(gi)], f32))
        zr = dt * ar
        zi = dt * ai
        mag = jnp.exp(zr)
        lr = mag * jnp.cos(zi)
        li = mag * jnp.sin(zi)
        den = ar * ar + ai * ai
        lam[gi, d] = (lr, li)
        zoh[gi, d] = (((lr - 1.0) * ar + li * ai) / den, (li * ar - (lr - 1.0) * ai) / den)

    pr = {u: [jnp.ones((1, LANES), f32)] for u in units}
    pi = {u: [jnp.zeros((1, LANES), f32)] for u in units}
    for _ in range(CHUNK):
        for u in units:
            lr, li = lam[u]
            pr[u].append(pr[u][-1] * lr - pi[u][-1] * li)
            pi[u].append(pr[u][-2] * li + pi[u][-1] * lr)

    lag_tabs, kws = {}, {}
    for gi in range(PREP_GB):
        ws_parts, coef_rows = [], []
        for d in range(2):
            u = (gi, d)
            fr, fi = zoh[u]
            btr = dup(bre_ref[d, gi])
            bti = dup(bim_ref[d, gi])
            bbr = fr * btr - fi * bti
            bbi = fr * bti + fi * btr
            cr = dup(cre_ref[d, gi])
            ci = dup(cim_ref[d, gi])
            pa = [jnp.where(lo1, r, i) for r, i in zip(pr[u], pi[u])]
            pb = [jnp.where(lo1, -i, r) for r, i in zip(pr[u], pi[u])]
            pc = [jnp.where(lo1, i, r) for r, i in zip(pr[u], pi[u])]
            pd = [jnp.where(lo1, r, -i) for r, i in zip(pr[u], pi[u])]

            order = range(CHUNK + 1) if d == 0 else range(CHUNK, -1, -1)
            blocks = [cr * pa[t] + ci * pb[t] for t in order]
            blocks.append(jnp.zeros((LAG_ROWS - (CHUNK + 1) * SSM_CH, LANES), f32))
            lag_tabs[u] = jnp.concatenate(blocks, axis=0)
            lhs = jnp.where(lo16, bbr, -bbi)
            kws[u] = lax.dot_general(lhs, lag_tabs[u], (((1,), (1,)), ((), ())),
                                     precision=HIGHEST, preferred_element_type=f32)

            parts = []
            for j in range(CHUNK):
                t = CHUNK - 1 - j if d == 0 else j
                parts.append(jnp.concatenate([bbr * pa[t] + bbi * pb[t],
                                              bbr * pc[t] + bbi * pd[t]], axis=1))
            ws_parts.append(jnp.concatenate(parts, axis=0))

            a2 = jnp.where(lo1, -pi[u][CHUNK], pi[u][CHUNK])
            coef_rows += [pr[u][CHUNK], a2, -a2]

        ws_ref[gi] = jnp.concatenate(ws_parts, axis=1).astype(ws_ref.dtype)
        coef_ref[gi] = jnp.concatenate(coef_rows + [jnp.zeros((2, LANES), f32)], axis=0)

    sign = jnp.where(lax.broadcasted_iota(jnp.int32, (LANES, CHUNK_VEC), 0) < SSM_STATE, 1.0, -1.0)
    sub = lax.broadcasted_iota(jnp.int32, (SSM_CH, CHUNK_VEC), 0)
    lane = lax.broadcasted_iota(jnp.int32, (SSM_CH, CHUNK_VEC), 1)
    for gi in range(PREP_GB):
        wo_f = jnp.transpose(lag_tabs[gi, 0][SSM_CH:SSM_CH + CHUNK_VEC, :]) * sign
        wo_b = jnp.transpose(lag_tabs[gi, 1][:CHUNK_VEC, :]) * sign
        wo_ref[gi] = jnp.concatenate([wo_f, wo_b], axis=0).astype(wo_ref.dtype)

        g = group(gi)
        dblk = d_ref[:, pl.ds(pl.multiple_of((g // GROUPS_PER_LANE_BLOCK) * LANES, LANES), LANES)]
        dblk = pltpu.roll(dblk, (LANES - (g % GROUPS_PER_LANE_BLOCK) * SSM_CH) % LANES, 1)
        dblk = jnp.where(lax.broadcasted_iota(jnp.int32, (1, LANES), 1) < SSM_CH, dblk, 0.0)
        for shift in (SSM_CH, 2 * SSM_CH, 4 * SSM_CH):
            dblk = dblk + pltpu.roll(dblk, shift, 1)
        dtile = jnp.broadcast_to(jnp.concatenate([dblk, dblk], axis=1), (SSM_CH, CHUNK_VEC))
        for jp in range(CHUNK):
            fwd = kws[gi, 0] if jp == 0 else pltpu.roll(kws[gi, 0], SSM_CH * jp, 1)
            bwd = pltpu.roll(kws[gi, 1], (LAG_ROWS - SSM_CH * (CHUNK - jp)) % LAG_ROWS, 1)
            skip = jnp.where((lane // SSM_CH == jp) & (lane % SSM_CH == sub), dtile, 0.0)
            blk = fwd[:, :CHUNK_VEC] + bwd[:, :CHUNK_VEC] + skip
            m_ref[gi, jp * SSM_CH:(jp + 1) * SSM_CH, :] = blk.astype(m_ref.dtype)


def _rope_tables():
    half = HEAD_DIM // 2
    inv_freq = ROPE_THETA ** (-np.arange(half, dtype=np.float64) / half)
    ang = np.arange(SEQ, dtype=np.float64)[:, None] * inv_freq[None, :]
    cos, sin = np.cos(ang), np.sin(ang)
    return jnp.asarray(np.concatenate([cos, cos, cos, cos, -sin, sin, -sin, sin], axis=1), jnp.float32)


def _rope_block(xb, cos_t, sin_t, first_half):
    swapped = jnp.where(first_half, pltpu.roll(xb, 96, 1), pltpu.roll(xb, 32, 1))
    return xb * cos_t + swapped * sin_t


COL_Q = (0, Q_COLS)
COL_KV = (COL_Q[1], COL_Q[1] + 2 * KV_COLS)
COL_ZA = (COL_KV[1], COL_KV[1] + D_ATTN)
COL_U = (COL_ZA[1], COL_ZA[1] + D_SSM)
COL_ZS = (COL_U[1], COL_U[1] + D_SSM)


def _silu(z):
    h = 0.5 * z
    return h + h * jnp.tanh(h)


def _inproj_kernel(x_ref, w32_ref, rope_ref, ldt_ref, are_ref, aim_ref, bre_ref, bim_ref, cre_ref, cim_ref, d_ref,
                   q_ref, kv_ref, sz_ref, ut_ref, m_ref, ws_ref, wo_ref, coef_ref, uslab_ref, w_ref):
    step = pl.program_id(0)
    part = step % STEPS_PER_REGROUP
    lane = lax.broadcasted_iota(jnp.int32, (SUB_ROWS, LANES), 1)
    first_half = (lane % HEAD_DIM) < (HEAD_DIM // 2)

    @pl.when(step == 0)
    def _():
        for c0 in range(0, w_ref.shape[1], 2 * LANES):
            w_ref[:, c0:c0 + 2 * LANES] = w32_ref[:, c0:c0 + 2 * LANES].astype(w_ref.dtype)

    def proj(xb, cols):
        return jnp.dot(xb, w_ref[:, cols[0]:cols[1]], preferred_element_type=jnp.float32)

    for sub in range(TOK_BLOCK // SUB_ROWS):
        rows = slice(sub * SUB_ROWS, (sub + 1) * SUB_ROWS)
        xb = x_ref[rows, :].astype(jnp.bfloat16)
        cos_t = rope_ref[rows, :LANES]
        sin_t = rope_ref[rows, LANES:]
        q = proj(xb, COL_Q)
        kv = proj(xb, COL_KV)
        q_blocks = [_rope_block(q[:, v * LANES:(v + 1) * LANES], cos_t, sin_t, first_half)
                    for v in range(Q_COLS // LANES)]
        q_ref[rows, :] = (jnp.concatenate(q_blocks, axis=1) * Q_SCALE).astype(q_ref.dtype)
        za = proj(xb, COL_ZA)
        k_rot = _rope_block(kv[:, :LANES], cos_t, sin_t, first_half)
        kv_ref[rows, :] = jnp.concatenate([k_rot, kv[:, LANES:]], axis=1).astype(kv_ref.dtype)
        u = proj(xb, COL_U)
        sz_ref[rows, :D_ATTN] = _silu(za).astype(sz_ref.dtype)
        zs = proj(xb, COL_ZS)
        for kk in range(SUB_ROWS // CHUNK):
            base = pl.multiple_of((part * (TOK_BLOCK // CHUNK) + sub * (SUB_ROWS // CHUNK)) * SLAB_CHUNK_PITCH, 8)
            for v in range(N_LANE_BLOCKS):
                uslab_ref[v, pl.ds(base + kk * SLAB_CHUNK_PITCH, CHUNK), :] = (
                    u[kk * CHUNK:(kk + 1) * CHUNK, v * LANES:(v + 1) * LANES])
        sz_ref[rows, D_ATTN:] = _silu(zs).astype(sz_ref.dtype)

    _ssm_tables(ldt_ref, are_ref, aim_ref, bre_ref, bim_ref, cre_ref, cim_ref, d_ref, m_ref, ws_ref, wo_ref, coef_ref)

    @pl.when(part == STEPS_PER_REGROUP - 1)
    def _():
        for j in range(CHUNK):
            for v in range(N_LANE_BLOCKS):
                uj = uslab_ref[v, pl.ds(j, LANES, stride=SLAB_CHUNK_PITCH), :]
                ujt = jnp.transpose(uj.astype(ut_ref.dtype)).reshape(GROUPS_PER_LANE_BLOCK, SSM_CH, LANES)
                g0 = v * GROUPS_PER_LANE_BLOCK
                ut_ref[g0:g0 + GROUPS_PER_LANE_BLOCK, j * SSM_CH:(j + 1) * SSM_CH, :] = ujt


def _inproj(x2, w, rope, log_dt, a_re, a_im, b_re, b_im, c_re, c_im, d):
    n_steps = BATCH * SEQ // TOK_BLOCK
    blocks_per_seq = SEQ // TOK_BLOCK
    bf16 = jnp.bfloat16

    def tok(cols):
        return pl.BlockSpec((TOK_BLOCK, cols), lambda s: (s, 0))

    def whole(a):
        return pl.BlockSpec(a.shape, lambda s: (0,) * a.ndim)

    def group_in(a):
        return pl.BlockSpec((2, PREP_GB) + a.shape[2:], lambda s: (0, s, 0, 0))

    def group_out(rows, cols):
        return pl.BlockSpec((PREP_GB, rows, cols), lambda s: (s, 0, 0))

    return pl.pallas_call(
        _inproj_kernel,
        grid=(n_steps,),
        in_specs=[tok(D_MODEL), pl.BlockSpec(w.shape, lambda s: (0, 0), pipeline_mode=pl.Buffered(1)),
                  pl.BlockSpec((TOK_BLOCK, 2 * LANES), lambda s: (s % blocks_per_seq, 0)),
                  pl.BlockSpec(memory_space=pltpu.SMEM), whole(a_re), whole(a_im),
                  group_in(b_re), group_in(b_im), group_in(c_re), group_in(c_im), whole(d)],
        out_specs=[tok(Q_COLS), tok(2 * KV_COLS), tok(D_ATTN + D_SSM),
                   pl.BlockSpec((None, N_GROUPS, CHUNK_VEC, LANES), lambda s: (s // STEPS_PER_REGROUP, 0, 0, 0)),
                   group_out(CHUNK_VEC, CHUNK_VEC), group_out(CHUNK_VEC, 4 * LANES),
                   group_out(2 * STATE_VEC, CHUNK_VEC), group_out(8, LANES)],
        out_shape=[jax.ShapeDtypeStruct((BATCH * SEQ, Q_COLS), bf16),
                   jax.ShapeDtypeStruct((BATCH * SEQ, 2 * KV_COLS), bf16),
                   jax.ShapeDtypeStruct((BATCH * SEQ, D_ATTN + D_SSM), bf16),
                   jax.ShapeDtypeStruct((N_ROW_BLOCKS, N_GROUPS, CHUNK_VEC, LANES), bf16),
                   jax.ShapeDtypeStruct((N_GROUPS, CHUNK_VEC, CHUNK_VEC), bf16),
                   jax.ShapeDtypeStruct((N_GROUPS, CHUNK_VEC, 4 * LANES), bf16),
                   jax.ShapeDtypeStruct((N_GROUPS, 2 * STATE_VEC, CHUNK_VEC), bf16),
                   jax.ShapeDtypeStruct((N_GROUPS, 8, LANES), jnp.float32)],
        scratch_shapes=[pltpu.VMEM((N_LANE_BLOCKS, LANES * SLAB_CHUNK_PITCH, LANES), jnp.float32),
                        pltpu.VMEM(w.shape, bf16)],
        compiler_params=pltpu.CompilerParams(dimension_semantics=("arbitrary",),
                                             vmem_limit_bytes=VMEM_LIMIT),
        name="inproj",
    )(x2, w, rope, log_dt, a_re, a_im, b_re, b_im, c_re, c_im, d)


ATTN_SUB = 32
ATTN_STEP = ATTN_SUB * BLOCK
N_KEYS = 3 * BLOCK


def _attn_kernel(q_ref, kp_ref, kc_ref, kn_ref, sink_ref, y_ref):
    i = pl.program_id(1)
    kvw = jnp.concatenate([kp_ref[...], kc_ref[...], kn_ref[...]], axis=0).astype(jnp.float32)

    n_win = kvw.shape[0]
    lane_w = lax.broadcasted_iota(jnp.int32, (n_win, LANES), 1)
    k_lo, k_hi, vt_lo, vt_hi = [], [], [], []
    for hk in range(N_KV_HEADS):
        in_head = (lane_w >= hk * HEAD_DIM) & (lane_w < (hk + 1) * HEAD_DIM)
        k_nat = jnp.where(in_head, kvw[:, :LANES], 0.0)
        v_nat = jnp.where(in_head, kvw[:, LANES:], 0.0)
        k_oth = pltpu.roll(k_nat, HEAD_DIM, 1)
        v_oth = pltpu.roll(v_nat, HEAD_DIM, 1)
        pairs = ((k_nat, k_oth), (v_nat, v_oth)) if hk == 0 else ((k_oth, k_nat), (v_oth, v_nat))
        k_lo.append(pairs[0][0].astype(jnp.bfloat16))
        k_hi.append(pairs[0][1].astype(jnp.bfloat16))
        vt_lo.append(jnp.transpose(pairs[1][0]).astype(jnp.bfloat16))
        vt_hi.append(jnp.transpose(pairs[1][1]).astype(jnp.bfloat16))

    c = lax.broadcasted_iota(jnp.int32, (N_KEYS, BLOCK), 0)
    r = lax.broadcasted_iota(jnp.int32, (N_KEYS, BLOCK), 1)
    band = (c >= r) & (c - r <= 2 * WINDOW)
    ones_row = lax.broadcasted_iota(jnp.int32, (16, 2 * N_KEYS), 0)
    ones_col = lax.broadcasted_iota(jnp.int32, (16, 2 * N_KEYS), 1)
    ones = jnp.where(((ones_row == 0) & (ones_col < N_KEYS)) | ((ones_row == 1) & (ones_col >= N_KEYS)),
                     1.0, 0.0).astype(jnp.bfloat16)
    row_o = lax.broadcasted_iota(jnp.int32, (LANES, 2 * BLOCK), 0)
    n_blocks = SEQ // BLOCK

    def scores(t, hk):
        blk = i * ATTN_SUB + t
        lo = jnp.where(blk == 0, BLOCK, 0)
        hi = jnp.where(blk == n_blocks - 1, 2 * BLOCK, N_KEYS)
        valid1 = band & (c >= lo) & (c < hi)
        valid = jnp.concatenate([valid1, valid1], axis=1)
        win = slice(t * BLOCK, t * BLOCK + N_KEYS)
        q = q_ref[t * BLOCK:(t + 1) * BLOCK, (2 * hk) * LANES:(2 * hk + 2) * LANES]
        qq = jnp.concatenate([q[:, :LANES], q[:, LANES:]], axis=0)
        kcat = jnp.concatenate([k_lo[hk][win], k_hi[hk][win]], axis=0)
        st = lax.dot_general(kcat, qq, (((1,), (1,)), ((), ())),
                             preferred_element_type=jnp.float32)
        sps, ms, sinks = [], [], []
        for par in range(2):
            base = par * N_KEYS
            sp = jnp.concatenate(
                [jnp.where(valid[:BLOCK], st[base:base + BLOCK, :], NEG_INF),
                 st[base + BLOCK:base + 2 * BLOCK, :],
                 jnp.where(valid[2 * BLOCK:], st[base + 2 * BLOCK:base + N_KEYS, :], NEG_INF)], axis=0)
            h0 = 4 * hk + par
            h1 = h0 + 2
            sink = jnp.concatenate([jnp.full((1, BLOCK), sink_ref[h] * LOG2_E, jnp.float32)
                                    for h in (h0, h1)], axis=1)
            sps.append(sp)
            ms.append(jnp.maximum(jnp.max(sp, axis=0, keepdims=True), sink))
            sinks.append(sink)
        return sps, ms, sinks

    def weights(sps, ms, sinks):
        return [jnp.exp2((sp - m).astype(jnp.bfloat16)) for sp, m in zip(sps, ms)], ms, sinks

    def finish(t, hk, ps, ms, sinks):
        win = slice(t * BLOCK, t * BLOCK + N_KEYS)
        vt = jnp.concatenate([jnp.concatenate([vt_lo[hk][:, win], vt_hi[hk][:, win]], axis=1), ones], axis=0)
        ot = jnp.dot(vt, jnp.concatenate(ps, axis=0), preferred_element_type=jnp.float32)
        inv0 = 1.0 / (ot[LANES:LANES + 1, :] + jnp.exp2(sinks[0] - ms[0]))
        inv1 = 1.0 / (ot[LANES + 1:LANES + 2, :] + jnp.exp2(sinks[1] - ms[1]))
        o = jnp.transpose(ot[:LANES, :] * jnp.where(row_o < HEAD_DIM, inv0, inv1))
        rows = slice(t * BLOCK, (t + 1) * BLOCK)
        y_ref[rows, (2 * hk) * LANES:(2 * hk + 1) * LANES] = o[:BLOCK].astype(y_ref.dtype)
        y_ref[rows, (2 * hk + 1) * LANES:(2 * hk + 2) * LANES] = o[BLOCK:].astype(y_ref.dtype)

    work = [(t, hk) for t in range(ATTN_SUB) for hk in range(N_KV_HEADS)]
    stage_a = {0: scores(*work[0])}
    stage_b = {}
    for n in range(len(work) + 1):
        if n + 1 < len(work):
            stage_a[n + 1] = scores(*work[n + 1])
        if n < len(work):
            stage_b[n] = weights(*stage_a.pop(n))
        if n >= 1:
            finish(*work[n - 1], *stage_b.pop(n - 1))


def _attention(q, kv, sink):
    nb = SEQ // BLOCK
    return pl.pallas_call(
        _attn_kernel,
        grid=(BATCH, SEQ // ATTN_STEP),
        in_specs=[pl.BlockSpec((None, ATTN_STEP, Q_COLS), lambda b, i: (b, i, 0)),
                  pl.BlockSpec((None, BLOCK, 2 * KV_COLS),
                               lambda b, i: (b, jnp.maximum(i * ATTN_SUB - 1, 0), 0)),
                  pl.BlockSpec((None, ATTN_STEP, 2 * KV_COLS), lambda b, i: (b, i, 0)),
                  pl.BlockSpec((None, BLOCK, 2 * KV_COLS),
                               lambda b, i: (b, jnp.minimum((i + 1) * ATTN_SUB, nb - 1), 0)),
                  pl.BlockSpec(memory_space=pltpu.SMEM)],
        out_specs=pl.BlockSpec((None, ATTN_STEP, D_ATTN), lambda b, i: (b, i, 0)),
        out_shape=jax.ShapeDtypeStruct((BATCH, SEQ, D_ATTN), jnp.bfloat16),
        compiler_params=pltpu.CompilerParams(dimension_semantics=("arbitrary", "arbitrary"),
                                             vmem_limit_bytes=VMEM_LIMIT),
        name="attention",
    )(q, kv, kv, kv, sink)


SSM_GB = 4
SSM_STEPS = N_GROUPS // SSM_GB


def _ssm_kernel(ut_ref, m_ref, ws_ref, wo_ref, coef_ref, gt_ref,
                sf_ref, sfs_ref, sb_ref, sbs_ref, xf_ref, xb_ref, u_even_ref, u_odd_ref):
    s = pl.program_id(0)

    def head(gi, u_ref):
        ut = jnp.concatenate([ut_ref[lb, gi] for lb in range(N_ROW_BLOCKS)], axis=1)
        u = jnp.transpose(ut)
        u_ref[gi] = u
        st = jnp.dot(u, ws_ref[gi], preferred_element_type=jnp.float32)
        for b in range(BATCH):
            rows = slice(b * N_CHUNKS, (b + 1) * N_CHUNKS)
            dst = pl.ds(b * SLAB_PITCH + 8, N_CHUNKS)
            sf_ref[gi, dst, :] = st[rows, 0 * LANES:1 * LANES]
            sfs_ref[gi, dst, :] = st[rows, 1 * LANES:2 * LANES]
            sb_ref[gi, dst, :] = st[rows, 2 * LANES:3 * LANES]
            sbs_ref[gi, dst, :] = st[rows, 3 * LANES:4 * LANES]

    def chunk_out(gi, u_ref):
        xin_f = jnp.concatenate([xf_ref[gi, pl.ds(b * SLAB_PITCH + 8, N_CHUNKS), :] for b in range(BATCH)], axis=0)
        xin_b = jnp.concatenate([xb_ref[gi, pl.ds(b * SLAB_PITCH + 8, N_CHUNKS), :] for b in range(BATCH)], axis=0)
        lhs = jnp.concatenate([u_ref[gi], xin_f.astype(jnp.bfloat16), xin_b.astype(jnp.bfloat16)], axis=1)
        rhs = jnp.concatenate([m_ref[gi], wo_ref[gi]], axis=0)
        return jnp.dot(lhs, rhs, preferred_element_type=jnp.float32)

    def store_out(gi, y):
        gt = jnp.transpose(jax.nn.gelu(y).astype(gt_ref.dtype))
        for lb in range(N_ROW_BLOCKS):
            gt_ref[lb, gi] = gt[:, lb * LANES:(lb + 1) * LANES]

    def head_and_tail(u_head_ref, u_tail_ref):
        y = chunk_out(0, u_tail_ref) if u_tail_ref is not None else None
        for gi in range(SSM_GB):
            if u_head_ref is not None:
                head(gi, u_head_ref)
            if u_tail_ref is not None:
                y_next = chunk_out(gi + 1, u_tail_ref) if gi + 1 < SSM_GB else None
                store_out(gi, y)
                y = y_next

    u_refs = (u_even_ref, u_odd_ref)
    inner = (s > 0) & (s < SSM_STEPS)
    pl.when(s == 0)(lambda: head_and_tail(u_refs[0], None))
    pl.when(inner & (s % 2 == 0))(lambda: head_and_tail(u_refs[0], u_refs[1]))
    pl.when(inner & (s % 2 == 1))(lambda: head_and_tail(u_refs[1], u_refs[0]))
    pl.when(s == SSM_STEPS)(lambda: head_and_tail(None, u_refs[(SSM_STEPS - 1) % 2]))

    @pl.when(s < SSM_STEPS)
    def _():
        zero = jnp.zeros((BATCH, LANES), jnp.float32)
        coefs = []
        for gi in range(SSM_GB):
            coefs.append([jnp.broadcast_to(coef_ref[gi, r:r + 1, :], (BATCH, LANES)) for r in range(6)])
            xf_ref[gi, pl.ds(8, BATCH, stride=SLAB_PITCH), :] = zero
            xb_ref[gi, pl.ds(8 + N_CHUNKS - 1, BATCH, stride=SLAB_PITCH), :] = zero

        def rows_at(r):
            return pl.ds(r, BATCH, stride=SLAB_PITCH)

        def step(k, carry):
            kb = N_CHUNKS - 1 - k
            out = []
            for gi in range(SSM_GB):
                a1f, a2f, a3f, a1b, a2b, a3b = coefs[gi]
                xf, xfs, xb, xbs = carry[4 * gi:4 * gi + 4]
                nxf = a1f * xf + a2f * xfs + sf_ref[gi, rows_at(8 + k), :]
                nxfs = a1f * xfs + a3f * xf + sfs_ref[gi, rows_at(8 + k), :]
                xf_ref[gi, rows_at(9 + k), :] = nxf
                nxb = a1b * xb + a2b * xbs + sb_ref[gi, rows_at(8 + kb), :]
                nxbs = a1b * xbs + a3b * xb + sbs_ref[gi, rows_at(8 + kb), :]
                xb_ref[gi, rows_at(7 + kb), :] = nxb
                out += [nxf, nxfs, nxb, nxbs]
            return tuple(out)

        lax.fori_loop(0, N_CHUNKS, step, (zero,) * (4 * SSM_GB), unroll=8)


def _ssm(ut, m_intra, w_state, w_out, coef):
    def head_block(s):
        return jnp.minimum(s, SSM_STEPS - 1)

    def tail_block(s):
        return jnp.maximum(s - 1, 0)

    def per_group(shape, block):
        return pl.BlockSpec((SSM_GB,) + shape, lambda s: (block(s), 0, 0))

    def lane_blocked(block):
        return pl.BlockSpec((N_ROW_BLOCKS, SSM_GB, CHUNK_VEC, LANES), lambda s: (0, block(s), 0, 0))

    slab = pltpu.VMEM((SSM_GB, SLAB_ROWS, LANES), jnp.float32)
    return pl.pallas_call(
        _ssm_kernel,
        grid=(SSM_STEPS + 1,),
        in_specs=[lane_blocked(head_block), per_group((CHUNK_VEC, CHUNK_VEC), tail_block),
                  per_group((CHUNK_VEC, 4 * LANES), head_block), per_group((2 * STATE_VEC, CHUNK_VEC), tail_block),
                  per_group((8, LANES), head_block)],
        out_specs=lane_blocked(tail_block),
        out_shape=jax.ShapeDtypeStruct((N_ROW_BLOCKS, N_GROUPS, CHUNK_VEC, LANES), jnp.bfloat16),
        scratch_shapes=[slab] * 6 + [pltpu.VMEM((SSM_GB, ROWS, CHUNK_VEC), jnp.bfloat16)] * 2,
        compiler_params=pltpu.CompilerParams(dimension_semantics=("arbitrary",),
                                             vmem_limit_bytes=VMEM_LIMIT),
        name="ssm",
    )(ut, m_intra, w_state, w_out, coef)


FINAL_SUB_ROWS = 256


def _unit_rms(v):
    return v * lax.rsqrt(jnp.mean(v * v, axis=-1, keepdims=True) + NORM_EPS)


def _final_kernel(x_ref, ya_ref, sz_ref, gt_ref, wglu32_ref, bglu_ref, wo32_ref, ga_ref, gs_ref,
                  lng_ref, lnb_ref, o_ref, gslab_ref, wglu_ref, wo_ref):
    step = pl.program_id(0)
    part = step % STEPS_PER_REGROUP

    @pl.when(step == 0)
    def _():
        wglu_ref[...] = wglu32_ref[...].astype(wglu_ref.dtype)
        for gain_ref, row0 in ((ga_ref, 0), (gs_ref, D_ATTN)):
            for v in range(gain_ref.shape[1] // LANES):
                gain = gain_ref[:, v * LANES:(v + 1) * LANES] / DEEPNORM_ALPHA
                col = jnp.transpose(jnp.broadcast_to(gain, (LANES, LANES)))
                rows = slice(row0 + v * LANES, row0 + (v + 1) * LANES)
                for c0 in range(0, D_MODEL, LANES):
                    wo_ref[rows, c0:c0 + LANES] = (wo32_ref[rows, c0:c0 + LANES] * col).astype(wo_ref.dtype)

    @pl.when(part == 0)
    def _():
        for j in range(CHUNK):
            for v in range(N_LANE_BLOCKS):
                g0 = v * GROUPS_PER_LANE_BLOCK
                gjt = gt_ref[g0:g0 + GROUPS_PER_LANE_BLOCK, j * SSM_CH:(j + 1) * SSM_CH, :]
                gj = jnp.transpose(gjt.reshape(LANES, LANES))
                gslab_ref[v, pl.ds(j, LANES, stride=SLAB_CHUNK_PITCH), :] = gj.astype(jnp.float32)

    def rows_of(sub):
        return slice(sub * FINAL_SUB_ROWS, (sub + 1) * FINAL_SUB_ROWS)

    def gate_in(sub):
        rows = rows_of(sub)
        ya = ya_ref[rows, :].astype(jnp.float32) * sz_ref[rows, :D_ATTN].astype(jnp.float32)
        ya = _unit_rms(ya).astype(jnp.bfloat16)
        base = pl.multiple_of((part * (TOK_BLOCK // CHUNK) + sub * (FINAL_SUB_ROWS // CHUNK)) * SLAB_CHUNK_PITCH, 8)
        g = jnp.concatenate(
            [jnp.concatenate([gslab_ref[v, pl.ds(base + kk * SLAB_CHUNK_PITCH, CHUNK), :]
                              for kk in range(FINAL_SUB_ROWS // CHUNK)], axis=0)
             for v in range(N_LANE_BLOCKS)], axis=1)
        return ya, g

    def glu(sub, ya, g):
        rows = rows_of(sub)
        gate = jnp.dot(g.astype(jnp.bfloat16), wglu_ref[...], preferred_element_type=jnp.float32) + bglu_ref[...]
        ys = g * jax.nn.sigmoid(gate)
        ys = ys * sz_ref[rows, D_ATTN:].astype(jnp.float32)
        return ya, _unit_rms(ys).astype(jnp.bfloat16)

    def project(sub, ya, ys):
        return (jnp.dot(ya, wo_ref[:D_ATTN, :], preferred_element_type=jnp.float32)
                + jnp.dot(ys, wo_ref[D_ATTN:, :], preferred_element_type=jnp.float32))

    def layer_norm(sub, out):
        rows = rows_of(sub)
        h = x_ref[rows, :] + out
        mu = jnp.mean(h, axis=-1, keepdims=True)
        hc = h - mu
        var = jnp.mean(hc * hc, axis=-1, keepdims=True)
        o_ref[rows, :] = hc * lax.rsqrt(var + NORM_EPS / DEEPNORM_ALPHA ** 2) * lng_ref[...] + lnb_ref[...]

    n_sub = TOK_BLOCK // FINAL_SUB_ROWS
    stage_a, stage_b = {}, {}
    for n in range(n_sub + 2):
        if n < n_sub:
            stage_a[n] = gate_in(n)
        if 1 <= n <= n_sub:
            stage_b[n - 1] = glu(n - 1, *stage_a.pop(n - 1))
        if n >= 2:
            layer_norm(n - 2, project(n - 2, *stage_b.pop(n - 2)))


def _final(x2, ya, sz, gt, wglu, bglu, wo, gain_a, gain_s, lng, lnb):
    n_steps = BATCH * SEQ // TOK_BLOCK

    def tok(cols):
        return pl.BlockSpec((TOK_BLOCK, cols), lambda s: (s, 0))

    def whole(a, **kw):
        return pl.BlockSpec(a.shape, lambda s: (0,) * a.ndim, **kw)

    return pl.pallas_call(
        _final_kernel,
        grid=(n_steps,),
        in_specs=[tok(D_MODEL), tok(D_ATTN), tok(D_ATTN + D_SSM),
                  pl.BlockSpec((None, N_GROUPS, CHUNK_VEC, LANES), lambda s: (s // STEPS_PER_REGROUP, 0, 0, 0)),
                  whole(wglu, pipeline_mode=pl.Buffered(1)), whole(bglu),
                  whole(wo, pipeline_mode=pl.Buffered(1)), whole(gain_a), whole(gain_s), whole(lng), whole(lnb)],
        out_specs=tok(D_MODEL),
        out_shape=jax.ShapeDtypeStruct((BATCH * SEQ, D_MODEL), jnp.float32),
        scratch_shapes=[pltpu.VMEM((N_LANE_BLOCKS, LANES * SLAB_CHUNK_PITCH, LANES), jnp.float32),
                        pltpu.VMEM(wglu.shape, jnp.bfloat16), pltpu.VMEM(wo.shape, jnp.bfloat16)],
        compiler_params=pltpu.CompilerParams(dimension_semantics=("arbitrary",),
                                             vmem_limit_bytes=VMEM_LIMIT),
        name="final",
    )(x2, ya, sz, gt, wglu, bglu, wo, gain_a, gain_s, lng, lnb)


def kernel(x, w_in, attn_sink, ssm_a_re, ssm_a_im, ssm_log_dt, ssm_b_re, ssm_b_im, ssm_c_re, ssm_c_im,
           ssm_d, w_glu, b_glu, norm_attn_g, norm_ssm_g, w_out, ln_g, ln_b):
    assert x.shape == (BATCH, SEQ, D_MODEL) and w_in.shape[0] == 1
    f32 = jnp.float32
    rope = _rope_tables()

    x2 = x.reshape(BATCH * SEQ, D_MODEL)
    b_t = [jnp.swapaxes(p, -1, -2) for p in (ssm_b_re, ssm_b_im)]
    params = [p[0].astype(f32) for p in (ssm_log_dt, ssm_a_re, ssm_a_im, *b_t, ssm_c_re, ssm_c_im)]
    q, kv, sz, ut, m_intra, w_state, w_so, coef = _inproj(x2, w_in[0], rope, *params,
                                                           ssm_d[0].astype(f32).reshape(1, D_SSM))
    ya = _attention(q.reshape(BATCH, SEQ, Q_COLS), kv.reshape(BATCH, SEQ, 2 * KV_COLS),
                    attn_sink[0].astype(f32))
    gt = _ssm(ut, m_intra, w_state, w_so, coef)

    row = lambda v: v[0].astype(f32)[None, :]
    out = _final(x2, ya.reshape(BATCH * SEQ, D_ATTN), sz, gt, w_glu[0].astype(f32), row(b_glu),
                 w_out[0].astype(f32), row(norm_attn_g), row(norm_ssm_g), row(ln_g), row(ln_b))
    return out.reshape(BATCH, SEQ, D_MODEL)
```

```python
import jax
import jax.numpy as jnp
import numpy as np
from jax import lax
from jax.experimental import pallas as pl
from jax.experimental.pallas import tpu as pltpu

D_MODEL = 1024
BATCH = 4
SEQ = 4096
D_ATTN = 512
D_SSM = 512
HEAD_DIM = 64
N_Q_HEADS = 8
N_KV_HEADS = 2
WINDOW = 128
BLOCK = 128
ROPE_THETA = 10000.0
SSM_CH = 16
N_GROUPS = 32
SSM_STATE = 64
NORM_EPS = 1e-5
NEG_INF = -1e30
DEEPNORM_ALPHA = 2.0 ** 0.25
LOG2_E = 1.4426950408889634
Q_SCALE = HEAD_DIM ** -0.5 * LOG2_E

Q_COLS = N_Q_HEADS * HEAD_DIM
KV_COLS = N_KV_HEADS * HEAD_DIM
CHUNK = 16
N_CHUNKS = SEQ // CHUNK
CHUNK_VEC = CHUNK * SSM_CH
STATE_VEC = 2 * SSM_STATE
LANES = 128
ROWS = BATCH * N_CHUNKS
N_ROW_BLOCKS = ROWS // LANES
TOK_BLOCK = 1024
REGROUP_TOK = LANES * CHUNK
STEPS_PER_REGROUP = REGROUP_TOK // TOK_BLOCK
SUB_ROWS = 512
N_LANE_BLOCKS = D_SSM // LANES
SLAB_CHUNK_PITCH = 20
GROUPS_PER_LANE_BLOCK = LANES // SSM_CH
VMEM_LIMIT = 56 * 1024 * 1024
HIGHEST = lax.Precision.HIGHEST


LAG_ROWS = 512
PREP_GB = N_GROUPS // (BATCH * SEQ // TOK_BLOCK)


def _ssm_tables(ldt_ref, are_ref, aim_ref, bre_ref, bim_ref, cre_ref, cim_ref, d_ref,
                m_ref, ws_ref, wo_ref, coef_ref):
    f32 = jnp.float32
    lo16 = lax.broadcasted_iota(jnp.int32, (SSM_CH, LANES), 1) < SSM_STATE
    lo1 = lax.broadcasted_iota(jnp.int32, (1, LANES), 1) < SSM_STATE
    units = [(gi, d) for gi in range(PREP_GB) for d in range(2)]

    def group(gi):
        return pl.program_id(0) * PREP_GB + gi

    def dup(v):
        return jnp.concatenate([v, v], axis=1)

    lam, zoh = {}, {}
    for gi, d in units:
        ar = dup(are_ref[d, pl.ds(group(gi), 1), :])
        ai = dup(aim_ref[d, pl.ds(group(gi), 1), :])
        dt = jnp.exp(jnp.full((1, LANES), ldt_ref[d, group(gi)], f32))
        zr = dt * ar
        zi = dt * ai
        mag = jnp.exp(zr)
        lr = mag * jnp.cos(zi)
        li = mag * jnp.sin(zi)
        den = ar * ar + ai * ai
        lam[gi, d] = (lr, li)
        zoh[gi, d] = (((lr - 1.0) * ar + li * ai) / den, (li * ar - (lr - 1.0) * ai) / den)

    pr = {u: [jnp.ones((1, LANES), f32)] for u in units}
    pi = {u: [jnp.zeros((1, LANES), f32)] for u in units}
    for _ in range(CHUNK):
        for u in units:
            lr, li = lam[u]
            pr[u].append(pr[u][-1] * lr - pi[u][-1] * li)
            pi[u].append(pr[u][-2] * li + pi[u][-1] * lr)

    lag_tabs, kws = {}, {}
    for gi in range(PREP_GB):
        ws_parts, coef_rows = [], []
        for d in range(2):
            u = (gi, d)
            fr, fi = zoh[u]
            btr = dup(bre_ref[d, gi])
            bti = dup(bim_ref[d, gi])
            bbr = fr * btr - fi * bti
            bbi = fr * bti + fi * btr
            cr = dup(cre_ref[d, gi])
            ci = dup(cim_ref[d, gi])
            pa = [jnp.where(lo1, r, i) for r, i in zip(pr[u], pi[u])]
            pb = [jnp.where(lo1, -i, r) for r, i in zip(pr[u], pi[u])]
            pc = [jnp.where(lo1, i, r) for r, i in zip(pr[u], pi[u])]
            pd = [jnp.where(lo1, r, -i) for r, i in zip(pr[u], pi[u])]

            order = range(CHUNK + 1) if d == 0 else range(CHUNK, -1, -1)
            blocks = [cr * pa[t] + ci * pb[t] for t in order]
            blocks.append(jnp.zeros((LAG_ROWS - (CHUNK + 1) * SSM_CH, LANES), f32))
            lag_tabs[u] = jnp.concatenate(blocks, axis=0)
            lhs = jnp.where(lo16, bbr, -bbi)
            kws[u] = lax.dot_general(lhs, lag_tabs[u], (((1,), (1,)), ((), ())),
                                     precision=HIGHEST, preferred_element_type=f32)

            parts = []
            for j in range(CHUNK):
                t = CHUNK - 1 - j if d == 0 else j
                parts.append(jnp.concatenate([bbr * pa[t] + bbi * pb[t],
                                              bbr * pc[t] + bbi * pd[t]], axis=1))
            ws_parts.append(jnp.concatenate(parts, axis=0))

            a2 = jnp.where(lo1, -pi[u][CHUNK], pi[u][CHUNK])
            coef_rows += [pr[u][CHUNK], a2, -a2]

        ws_ref[gi] = jnp.concatenate(ws_parts, axis=1).astype(ws_ref.dtype)
        coef_ref[gi] = jnp.concatenate(coef_rows + [jnp.zeros((2, LANES), f32)], axis=0)

    sign = jnp.where(lax.broadcasted_iota(jnp.int32, (LANES, CHUNK_VEC), 0) < SSM_STATE, 1.0, -1.0)
    sub = lax.broadcasted_iota(jnp.int32, (SSM_CH, CHUNK_VEC), 0)
    lane = lax.broadcasted_iota(jnp.int32, (SSM_CH, CHUNK_VEC), 1)
    for gi in range(PREP_GB):
        wo_f = jnp.transpose(lag_tabs[gi, 0][SSM_CH:SSM_CH + CHUNK_VEC, :]) * sign
        wo_b = jnp.transpose(lag_tabs[gi, 1][:CHUNK_VEC, :]) * sign
        wo_ref[gi] = jnp.concatenate([wo_f, wo_b], axis=0).astype(wo_ref.dtype)

        g = group(gi)
        dblk = d_ref[:, pl.ds(pl.multiple_of((g // GROUPS_PER_LANE_BLOCK) * LANES, LANES), LANES)]
        dblk = pltpu.roll(dblk, (LANES - (g % GROUPS_PER_LANE_BLOCK) * SSM_CH) % LANES, 1)
        dblk = jnp.where(lax.broadcasted_iota(jnp.int32, (1, LANES), 1) < SSM_CH, dblk, 0.0)
        for shift in (SSM_CH, 2 * SSM_CH, 4 * SSM_CH):
            dblk = dblk + pltpu.roll(dblk, shift, 1)
        dtile = jnp.broadcast_to(jnp.concatenate([dblk, dblk], axis=1), (SSM_CH, CHUNK_VEC))
        for jp in range(CHUNK):
            fwd = kws[gi, 0] if jp == 0 else pltpu.roll(kws[gi, 0], SSM_CH * jp, 1)
            bwd = pltpu.roll(kws[gi, 1], (LAG_ROWS - SSM_CH * (CHUNK - jp)) % LAG_ROWS, 1)
            skip = jnp.where((lane // SSM_CH == jp) & (lane % SSM_CH == sub), dtile, 0.0)
            blk = fwd[:, :CHUNK_VEC] + bwd[:, :CHUNK_VEC] + skip
            m_ref[gi, jp * SSM_CH:(jp + 1) * SSM_CH, :] = blk.astype(m_ref.dtype)


def _rope_tables():
    half = HEAD_DIM // 2
    inv_freq = ROPE_THETA ** (-np.arange(half, dtype=np.float64) / half)
    ang = np.arange(SEQ, dtype=np.float64)[:, None] * inv_freq[None, :]
    cos, sin = np.cos(ang), np.sin(ang)
    return jnp.asarray(np.concatenate([cos, cos, cos, cos, -sin, sin, -sin, sin], axis=1), jnp.float32)


def _rope_block(xb, cos_t, sin_t, first_half):
    swapped = jnp.where(first_half, pltpu.roll(xb, 96, 1), pltpu.roll(xb, 32, 1))
    return xb * cos_t + swapped * sin_t


COL_Q = (0, Q_COLS)
COL_KV = (COL_Q[1], COL_Q[1] + 2 * KV_COLS)
COL_ZA = (COL_KV[1], COL_KV[1] + D_ATTN)
COL_U = (COL_ZA[1], COL_ZA[1] + D_SSM)
COL_ZS = (COL_U[1], COL_U[1] + D_SSM)


def _silu(z):
    h = 0.5 * z
    return h + h * jnp.tanh(h)


def _inproj_kernel(x_ref, w32_ref, rope_ref, ldt_ref, are_ref, aim_ref, bre_ref, bim_ref, cre_ref, cim_ref, d_ref,
                   q_ref, kv_ref, sz_ref, ut_ref, m_ref, ws_ref, wo_ref, coef_ref, uslab_ref, w_ref):
    step = pl.program_id(0)
    part = step % STEPS_PER_REGROUP
    lane = lax.broadcasted_iota(jnp.int32, (SUB_ROWS, LANES), 1)
    first_half = (lane % HEAD_DIM) < (HEAD_DIM // 2)

    @pl.when(step == 0)
    def _():
        for c0 in range(0, w_ref.shape[1], 2 * LANES):
            w_ref[:, c0:c0 + 2 * LANES] = w32_ref[:, c0:c0 + 2 * LANES].astype(w_ref.dtype)

    def proj(xb, cols):
        return jnp.dot(xb, w_ref[:, cols[0]:cols[1]], preferred_element_type=jnp.float32)

    for sub in range(TOK_BLOCK // SUB_ROWS):
        rows = slice(sub * SUB_ROWS, (sub + 1) * SUB_ROWS)
        xb = x_ref[rows, :].astype(jnp.bfloat16)
        cos_t = rope_ref[rows, :LANES]
        sin_t = rope_ref[rows, LANES:]
        q = proj(xb, COL_Q)
        kv = proj(xb, COL_KV)
        q_blocks = [_rope_block(q[:, v * LANES:(v + 1) * LANES], cos_t, sin_t, first_half)
                    for v in range(Q_COLS // LANES)]
        q_ref[rows, :] = (jnp.concatenate(q_blocks, axis=1) * Q_SCALE).astype(q_ref.dtype)
        za = proj(xb, COL_ZA)
        k_rot = _rope_block(kv[:, :LANES], cos_t, sin_t, first_half)
        kv_ref[rows, :] = jnp.concatenate([k_rot, kv[:, LANES:]], axis=1).astype(kv_ref.dtype)
        u = proj(xb, COL_U)
        sz_ref[rows, :D_ATTN] = _silu(za).astype(sz_ref.dtype)
        zs = proj(xb, COL_ZS)
        for kk in range(SUB_ROWS // CHUNK):
            base = pl.multiple_of((part * (TOK_BLOCK // CHUNK) + sub * (SUB_ROWS // CHUNK)) * SLAB_CHUNK_PITCH, 8)
            for v in range(N_LANE_BLOCKS):
                uslab_ref[v, pl.ds(base + kk * SLAB_CHUNK_PITCH, CHUNK), :] = (
                    u[kk * CHUNK:(kk + 1) * CHUNK, v * LANES:(v + 1) * LANES])
        sz_ref[rows, D_ATTN:] = _silu(zs).astype(sz_ref.dtype)

    _ssm_tables(ldt_ref, are_ref, aim_ref, bre_ref, bim_ref, cre_ref, cim_ref, d_ref, m_ref, ws_ref, wo_ref, coef_ref)

    @pl.when(part == STEPS_PER_REGROUP - 1)
    def _():
        for j in range(CHUNK):
            for v in range(N_LANE_BLOCKS):
                uj = uslab_ref[v, pl.ds(j, LANES, stride=SLAB_CHUNK_PITCH), :]
                ujt = jnp.transpose(uj.astype(ut_ref.dtype)).reshape(GROUPS_PER_LANE_BLOCK, SSM_CH, LANES)
                g0 = v * GROUPS_PER_LANE_BLOCK
                ut_ref[g0:g0 + GROUPS_PER_LANE_BLOCK, j * SSM_CH:(j + 1) * SSM_CH, :] = ujt


def _inproj(x2, w, rope, log_dt, a_re, a_im, b_re, b_im, c_re, c_im, d):
    n_steps = BATCH * SEQ // TOK_BLOCK
    blocks_per_seq = SEQ // TOK_BLOCK
    bf16 = jnp.bfloat16

    def tok(cols):
        return pl.BlockSpec((TOK_BLOCK, cols), lambda s: (s, 0))

    def whole(a):
        return pl.BlockSpec(a.shape, lambda s: (0,) * a.ndim)

    def group_in(a):
        return pl.BlockSpec((2, PREP_GB) + a.shape[2:], lambda s: (0, s, 0, 0))

    def group_out(rows, cols):
        return pl.BlockSpec((PREP_GB, rows, cols), lambda s: (s, 0, 0))

    return pl.pallas_call(
        _inproj_kernel,
        grid=(n_steps,),
        in_specs=[tok(D_MODEL), pl.BlockSpec(w.shape, lambda s: (0, 0), pipeline_mode=pl.Buffered(1)),
                  pl.BlockSpec((TOK_BLOCK, 2 * LANES), lambda s: (s % blocks_per_seq, 0)),
                  pl.BlockSpec(memory_space=pltpu.SMEM), whole(a_re), whole(a_im),
                  group_in(b_re), group_in(b_im), group_in(c_re), group_in(c_im), whole(d)],
        out_specs=[tok(Q_COLS), tok(2 * KV_COLS), tok(D_ATTN + D_SSM),
                   pl.BlockSpec((None, N_GROUPS, CHUNK_VEC, LANES), lambda s: (s // STEPS_PER_REGROUP, 0, 0, 0)),
                   group_out(CHUNK_VEC, CHUNK_VEC), group_out(CHUNK_VEC, 4 * LANES),
                   group_out(2 * STATE_VEC, CHUNK_VEC), group_out(8, LANES)],
        out_shape=[jax.ShapeDtypeStruct((BATCH * SEQ, Q_COLS), bf16),
                   jax.ShapeDtypeStruct((BATCH * SEQ, 2 * KV_COLS), bf16),
                   jax.ShapeDtypeStruct((BATCH * SEQ, D_ATTN + D_SSM), bf16),
                   jax.ShapeDtypeStruct((N_ROW_BLOCKS, N_GROUPS, CHUNK_VEC, LANES), bf16),
                   jax.ShapeDtypeStruct((N_GROUPS, CHUNK_VEC, CHUNK_VEC), bf16),
                   jax.ShapeDtypeStruct((N_GROUPS, CHUNK_VEC, 4 * LANES), bf16),
                   jax.ShapeDtypeStruct((N_GROUPS, 2 * STATE_VEC, CHUNK_VEC), bf16),
                   jax.ShapeDtypeStruct((N_GROUPS, 8, LANES), jnp.float32)],
        scratch_shapes=[pltpu.VMEM((N_LANE_BLOCKS, LANES * SLAB_CHUNK_PITCH, LANES), jnp.float32),
                        pltpu.VMEM(w.shape, bf16)],
        compiler_params=pltpu.CompilerParams(dimension_semantics=("arbitrary",),
                                             vmem_limit_bytes=VMEM_LIMIT),
        name="inproj",
    )(x2, w, rope, log_dt, a_re, a_im, b_re, b_im, c_re, c_im, d)


ATTN_SUB = 32
ATTN_STEP = ATTN_SUB * BLOCK
N_KEYS = 3 * BLOCK


def _attn_kernel(q_ref, kp_ref, kc_ref, kn_ref, sink_ref, y_ref):
    i = pl.program_id(1)
    kvw = jnp.concatenate([kp_ref[...], kc_ref[...], kn_ref[...]], axis=0).astype(jnp.float32)

    n_win = kvw.shape[0]
    lane_w = lax.broadcasted_iota(jnp.int32, (n_win, LANES), 1)
    k_lo, k_hi, vt_lo, vt_hi = [], [], [], []
    for hk in range(N_KV_HEADS):
        in_head = (lane_w >= hk * HEAD_DIM) & (lane_w < (hk + 1) * HEAD_DIM)
        k_nat = jnp.where(in_head, kvw[:, :LANES], 0.0)
        v_nat = jnp.where(in_head, kvw[:, LANES:], 0.0)
        k_oth = pltpu.roll(k_nat, HEAD_DIM, 1)
        v_oth = pltpu.roll(v_nat, HEAD_DIM, 1)
        pairs = ((k_nat, k_oth), (v_nat, v_oth)) if hk == 0 else ((k_oth, k_nat), (v_oth, v_nat))
        k_lo.append(pairs[0][0].astype(jnp.bfloat16))
        k_hi.append(pairs[0][1].astype(jnp.bfloat16))
        vt_lo.append(jnp.transpose(pairs[1][0]).astype(jnp.bfloat16))
        vt_hi.append(jnp.transpose(pairs[1][1]).astype(jnp.bfloat16))

    c = lax.broadcasted_iota(jnp.int32, (N_KEYS, BLOCK), 0)
    r = lax.broadcasted_iota(jnp.int32, (N_KEYS, BLOCK), 1)
    band = (c >= r) & (c - r <= 2 * WINDOW)
    ones_row = lax.broadcasted_iota(jnp.int32, (16, 2 * N_KEYS), 0)
    ones_col = lax.broadcasted_iota(jnp.int32, (16, 2 * N_KEYS), 1)
    ones = jnp.where(((ones_row == 0) & (ones_col < N_KEYS)) | ((ones_row == 1) & (ones_col >= N_KEYS)),
                     1.0, 0.0).astype(jnp.bfloat16)
    row_o = lax.broadcasted_iota(jnp.int32, (LANES, 2 * BLOCK), 0)
    n_blocks = SEQ // BLOCK

    def scores(t, hk):
        blk = i * ATTN_SUB + t
        lo = jnp.where(blk == 0, BLOCK, 0)
        hi = jnp.where(blk == n_blocks - 1, 2 * BLOCK, N_KEYS)
        valid1 = band & (c >= lo) & (c < hi)
        valid = jnp.concatenate([valid1, valid1], axis=1)
        win = slice(t * BLOCK, t * BLOCK + N_KEYS)
        q = q_ref[t * BLOCK:(t + 1) * BLOCK, (2 * hk) * LANES:(2 * hk + 2) * LANES]
        qq = jnp.concatenate([q[:, :LANES], q[:, LANES:]], axis=0)
        kcat = jnp.concatenate([k_lo[hk][win], k_hi[hk][win]], axis=0)
        st = lax.dot_general(kcat, qq, (((1,), (1,)), ((), ())),
                             preferred_element_type=jnp.float32)
        sps, ms, sinks = [], [], []
        for par in range(2):
            base = par * N_KEYS
            sp = jnp.concatenate(
                [jnp.where(valid[:BLOCK], st[base:base + BLOCK, :], NEG_INF),
                 st[base + BLOCK:base + 2 * BLOCK, :],
                 jnp.where(valid[2 * BLOCK:], st[base + 2 * BLOCK:base + N_KEYS, :], NEG_INF)], axis=0)
            h0 = 4 * hk + par
            h1 = h0 + 2
            sink = jnp.concatenate([jnp.full((1, BLOCK), sink_ref[h] * LOG2_E, jnp.float32)
                                    for h in (h0, h1)], axis=1)
            sps.append(sp)
            ms.append(jnp.maximum(jnp.max(sp, axis=0, keepdims=True), sink))
            sinks.append(sink)
        return sps, ms, sinks

    def weights(sps, ms, sinks):
        return [jnp.exp2((sp - m).astype(jnp.bfloat16)) for sp, m in zip(sps, ms)], ms, sinks

    def finish(t, hk, ps, ms, sinks):
        win = slice(t * BLOCK, t * BLOCK + N_KEYS)
        vt = jnp.concatenate([jnp.concatenate([vt_lo[hk][:, win], vt_hi[hk][:, win]], axis=1), ones], axis=0)
        ot = jnp.dot(vt, jnp.concatenate(ps, axis=0), preferred_element_type=jnp.float32)
        inv0 = 1.0 / (ot[LANES:LANES + 1, :] + jnp.exp2(sinks[0] - ms[0]))
        inv1 = 1.0 / (ot[LANES + 1:LANES + 2, :] + jnp.exp2(sinks[1] - ms[1]))
        o = jnp.transpose(ot[:LANES, :] * jnp.where(row_o < HEAD_DIM, inv0, inv1))
        rows = slice(t * BLOCK, (t + 1) * BLOCK)
        y_ref[rows, (2 * hk) * LANES:(2 * hk + 1) * LANES] = o[:BLOCK].astype(y_ref.dtype)
        y_ref[rows, (2 * hk + 1) * LANES:(2 * hk + 2) * LANES] = o[BLOCK:].astype(y_ref.dtype)

    work = [(t, hk) for t in range(ATTN_SUB) for hk in range(N_KV_HEADS)]
    stage_a = {0: scores(*work[0])}
    stage_b = {}
    for n in range(len(work) + 1):
        if n + 1 < len(work):
            stage_a[n + 1] = scores(*work[n + 1])
        if n < len(work):
            stage_b[n] = weights(*stage_a.pop(n))
        if n >= 1:
            finish(*work[n - 1], *stage_b.pop(n - 1))


def _attention(q, kv, sink):
    nb = SEQ // BLOCK
    return pl.pallas_call(
        _attn_kernel,
        grid=(BATCH, SEQ // ATTN_STEP),
        in_specs=[pl.BlockSpec((None, ATTN_STEP, Q_COLS), lambda b, i: (b, i, 0)),
                  pl.BlockSpec((None, BLOCK, 2 * KV_COLS),
                               lambda b, i: (b, jnp.maximum(i * ATTN_SUB - 1, 0), 0)),
                  pl.BlockSpec((None, ATTN_STEP, 2 * KV_COLS), lambda b, i: (b, i, 0)),
                  pl.BlockSpec((None, BLOCK, 2 * KV_COLS),
                               lambda b, i: (b, jnp.minimum((i + 1) * ATTN_SUB, nb - 1), 0)),
                  pl.BlockSpec(memory_space=pltpu.SMEM)],
        out_specs=pl.BlockSpec((None, ATTN_STEP, D_ATTN), lambda b, i: (b, i, 0)),
        out_shape=jax.ShapeDtypeStruct((BATCH, SEQ, D_ATTN), jnp.bfloat16),
        compiler_params=pltpu.CompilerParams(dimension_semantics=("arbitrary", "arbitrary"),
                                             vmem_limit_bytes=VMEM_LIMIT),
        name="attention",
    )(q, kv, kv, kv, sink)


SSM_GB = 4
SSM_STEPS = N_GROUPS // SSM_GB
N_CHAINS = SSM_GB * BATCH
SLOT_PITCH = 36
RECUR_SLAB_ROWS = (N_CHUNKS + 1) * SLOT_PITCH + 4


def _ssm_kernel(ut_ref, m_ref, ws_ref, wo_ref, coef_ref, gt_ref,
                sf_ref, sb_ref, x_ref, u_even_ref, u_odd_ref):
    s = pl.program_id(0)

    def chain_rows(gi, b, second, first_slot):
        return pl.ds(first_slot * SLOT_PITCH + 2 * (gi * BATCH + b) + second, N_CHUNKS, stride=SLOT_PITCH)

    def head(gi, u_ref):
        ut = jnp.concatenate([ut_ref[lb, gi] for lb in range(N_ROW_BLOCKS)], axis=1)
        u = jnp.transpose(ut)
        u_ref[gi] = u
        st = jnp.dot(u, ws_ref[gi], preferred_element_type=jnp.float32)
        for b in range(BATCH):
            rows = slice(b * N_CHUNKS, (b + 1) * N_CHUNKS)
            sf_ref[chain_rows(gi, b, 0, 0), :] = st[rows, 0 * LANES:1 * LANES]
            sf_ref[chain_rows(gi, b, 1, 0), :] = st[rows, 1 * LANES:2 * LANES]
            sb_ref[chain_rows(gi, b, 0, 0), :] = st[rows, 2 * LANES:3 * LANES]
            sb_ref[chain_rows(gi, b, 1, 0), :] = st[rows, 3 * LANES:4 * LANES]

    def chunk_out(gi, u_ref):
        xin_f = jnp.concatenate([x_ref[chain_rows(gi, b, 0, 0), :] for b in range(BATCH)], axis=0)
        xin_b = jnp.concatenate([x_ref[chain_rows(gi, b, 1, 1), :] for b in range(BATCH)], axis=0)
        lhs = jnp.concatenate([u_ref[gi], xin_f.astype(jnp.bfloat16), xin_b.astype(jnp.bfloat16)], axis=1)
        rhs = jnp.concatenate([m_ref[gi], wo_ref[gi]], axis=0)
        return jnp.dot(lhs, rhs, preferred_element_type=jnp.float32)

    def store_out(gi, y):
        gt = jnp.transpose(jax.nn.gelu(y).astype(gt_ref.dtype))
        for lb in range(N_ROW_BLOCKS):
            gt_ref[lb, gi] = gt[:, lb * LANES:(lb + 1) * LANES]

    def head_and_tail(u_head_ref, u_tail_ref):
        y = chunk_out(0, u_tail_ref) if u_tail_ref is not None else None
        for gi in range(SSM_GB):
            if u_head_ref is not None:
                head(gi, u_head_ref)
            if u_tail_ref is not None:
                y_next = chunk_out(gi + 1, u_tail_ref) if gi + 1 < SSM_GB else None
                store_out(gi, y)
                y = y_next

    u_refs = (u_even_ref, u_odd_ref)
    inner = (s > 0) & (s < SSM_STEPS)
    pl.when(s == 0)(lambda: head_and_tail(u_refs[0], None))
    pl.when(inner & (s % 2 == 0))(lambda: head_and_tail(u_refs[0], u_refs[1]))
    pl.when(inner & (s % 2 == 1))(lambda: head_and_tail(u_refs[1], u_refs[0]))
    pl.when(s == SSM_STEPS)(lambda: head_and_tail(None, u_refs[(SSM_STEPS - 1) % 2]))

    @pl.when(s < SSM_STEPS)
    def _():
        n_halves = N_CHAINS // 8

        def chains_at(slot, half, second):
            return pl.ds(slot * SLOT_PITCH + 16 * half + second, 8, stride=2)

        zero = jnp.zeros((8, LANES), jnp.float32)
        coefs = []
        for h in range(n_halves):
            rows = [jnp.concatenate([jnp.broadcast_to(coef_ref[2 * h + m, r:r + 1, :], (BATCH, LANES))
                                     for m in range(2)], axis=0) for r in range(6)]
            per_dir = []
            for a1, a2, a3 in (rows[:3], rows[3:]):
                b2 = 2.0 * a1 * a2
                per_dir.append((a1, a2, a3, a1 * a1 - a2 * a2, b2, -b2))
            coefs.append(per_dir)
            x_ref[chains_at(0, h, 0), :] = zero
            x_ref[chains_at(N_CHUNKS, h, 1), :] = zero

        def two_chunks(coef, x, xs, s_ref, half, slot0, slot1, second, out0, out1):
            a1, a2, a3, b1, b2, b3 = coef
            s0, ss0 = s_ref[chains_at(slot0, half, 0), :], s_ref[chains_at(slot0, half, 1), :]
            s1, ss1 = s_ref[chains_at(slot1, half, 0), :], s_ref[chains_at(slot1, half, 1), :]
            x_ref[chains_at(out0, half, second), :] = a1 * x + a2 * xs + s0
            c = a1 * s0 + a2 * ss0 + s1
            cs = a1 * ss0 + a3 * s0 + ss1
            nx = b1 * x + b2 * xs + c
            nxs = b1 * xs + b3 * x + cs
            x_ref[chains_at(out1, half, second), :] = nx
            return nx, nxs

        def trip(t, carry):
            k = 2 * t
            kb = N_CHUNKS - 1 - k
            out = []
            for h in range(n_halves):
                xf, xfs, xb, xbs = carry[4 * h:4 * h + 4]
                out += two_chunks(coefs[h][0], xf, xfs, sf_ref, h, k, k + 1, 0, k + 1, k + 2)
                out += two_chunks(coefs[h][1], xb, xbs, sb_ref, h, kb, kb - 1, 1, kb, kb - 1)
            return tuple(out)

        lax.fori_loop(0, N_CHUNKS // 2, trip, (zero,) * (4 * n_halves), unroll=4)


def _ssm(ut, m_intra, w_state, w_out, coef):
    def head_block(s):
        return jnp.minimum(s, SSM_STEPS - 1)

    def tail_block(s):
        return jnp.maximum(s - 1, 0)

    def per_group(shape, block):
        return pl.BlockSpec((SSM_GB,) + shape, lambda s: (block(s), 0, 0))

    def lane_blocked(block):
        return pl.BlockSpec((N_ROW_BLOCKS, SSM_GB, CHUNK_VEC, LANES), lambda s: (0, block(s), 0, 0))

    slab = pltpu.VMEM((RECUR_SLAB_ROWS, LANES), jnp.float32)
    return pl.pallas_call(
        _ssm_kernel,
        grid=(SSM_STEPS + 1,),
        in_specs=[lane_blocked(head_block), per_group((CHUNK_VEC, CHUNK_VEC), tail_block),
                  per_group((CHUNK_VEC, 4 * LANES), head_block), per_group((2 * STATE_VEC, CHUNK_VEC), tail_block),
                  per_group((8, LANES), head_block)],
        out_specs=lane_blocked(tail_block),
        out_shape=jax.ShapeDtypeStruct((N_ROW_BLOCKS, N_GROUPS, CHUNK_VEC, LANES), jnp.bfloat16),
        scratch_shapes=[slab] * 3 + [pltpu.VMEM((SSM_GB, ROWS, CHUNK_VEC), jnp.bfloat16)] * 2,
        compiler_params=pltpu.CompilerParams(dimension_semantics=("arbitrary",),
                                             vmem_limit_bytes=VMEM_LIMIT),
        name="ssm",
    )(ut, m_intra, w_state, w_out, coef)


FINAL_SUB_ROWS = 256


def _unit_rms(v):
    return v * lax.rsqrt(jnp.mean(v * v, axis=-1, keepdims=True) + NORM_EPS)


def _final_kernel(x_ref, ya_ref, sz_ref, gt_ref, wglu32_ref, bglu_ref, wo32_ref, ga_ref, gs_ref,
                  lng_ref, lnb_ref, o_ref, gslab_ref, wglu_ref, wo_ref):
    step = pl.program_id(0)
    part = step % STEPS_PER_REGROUP

    @pl.when(step == 0)
    def _():
        wglu_ref[...] = wglu32_ref[...].astype(wglu_ref.dtype)
        for gain_ref, row0 in ((ga_ref, 0), (gs_ref, D_ATTN)):
            for v in range(gain_ref.shape[1] // LANES):
                gain = gain_ref[:, v * LANES:(v + 1) * LANES] / DEEPNORM_ALPHA
                col = jnp.transpose(jnp.broadcast_to(gain, (LANES, LANES)))
                rows = slice(row0 + v * LANES, row0 + (v + 1) * LANES)
                for c0 in range(0, D_MODEL, LANES):
                    wo_ref[rows, c0:c0 + LANES] = (wo32_ref[rows, c0:c0 + LANES] * col).astype(wo_ref.dtype)

    @pl.when(part == 0)
    def _():
        for j in range(CHUNK):
            for v in range(N_LANE_BLOCKS):
                g0 = v * GROUPS_PER_LANE_BLOCK
                gjt = gt_ref[g0:g0 + GROUPS_PER_LANE_BLOCK, j * SSM_CH:(j + 1) * SSM_CH, :]
                gj = jnp.transpose(gjt.reshape(LANES, LANES))
                gslab_ref[v, pl.ds(j, LANES, stride=SLAB_CHUNK_PITCH), :] = gj.astype(jnp.float32)

    def rows_of(sub):
        return slice(sub * FINAL_SUB_ROWS, (sub + 1) * FINAL_SUB_ROWS)

    def gate_in(sub):
        rows = rows_of(sub)
        ya = ya_ref[rows, :].astype(jnp.float32) * sz_ref[rows, :D_ATTN].astype(jnp.float32)
        ya = _unit_rms(ya).astype(jnp.bfloat16)
        base = pl.multiple_of((part * (TOK_BLOCK // CHUNK) + sub * (FINAL_SUB_ROWS // CHUNK)) * SLAB_CHUNK_PITCH, 8)
        g = jnp.concatenate(
            [jnp.concatenate([gslab_ref[v, pl.ds(base + kk * SLAB_CHUNK_PITCH, CHUNK), :]
                              for kk in range(FINAL_SUB_ROWS // CHUNK)], axis=0)
             for v in range(N_LANE_BLOCKS)], axis=1)
        return ya, g

    def glu(sub, ya, g):
        rows = rows_of(sub)
        gate = jnp.dot(g.astype(jnp.bfloat16), wglu_ref[...], preferred_element_type=jnp.float32) + bglu_ref[...]
        ys = g * jax.nn.sigmoid(gate)
        ys = ys * sz_ref[rows, D_ATTN:].astype(jnp.float32)
        return ya, _unit_rms(ys).astype(jnp.bfloat16)

    def project(sub, ya, ys):
        return (jnp.dot(ya, wo_ref[:D_ATTN, :], preferred_element_type=jnp.float32)
                + jnp.dot(ys, wo_ref[D_ATTN:, :], preferred_element_type=jnp.float32))

    def layer_norm(sub, out):
        rows = rows_of(sub)
        h = x_ref[rows, :] + out
        mu = jnp.mean(h, axis=-1, keepdims=True)
        hc = h - mu
        var = jnp.mean(hc * hc, axis=-1, keepdims=True)
        o_ref[rows, :] = hc * lax.rsqrt(var + NORM_EPS / DEEPNORM_ALPHA ** 2) * lng_ref[...] + lnb_ref[...]

    n_sub = TOK_BLOCK // FINAL_SUB_ROWS
    stage_a, stage_b = {}, {}
    for n in range(n_sub + 2):
        if n < n_sub:
            stage_a[n] = gate_in(n)
        if 1 <= n <= n_sub:
            stage_b[n - 1] = glu(n - 1, *stage_a.pop(n - 1))
        if n >= 2:
            layer_norm(n - 2, project(n - 2, *stage_b.pop(n - 2)))


def _final(x2, ya, sz, gt, wglu, bglu, wo, gain_a, gain_s, lng, lnb):
    n_steps = BATCH * SEQ // TOK_BLOCK

    def tok(cols):
        return pl.BlockSpec((TOK_BLOCK, cols), lambda s: (s, 0))

    def whole(a, **kw):
        return pl.BlockSpec(a.shape, lambda s: (0,) * a.ndim, **kw)

    return pl.pallas_call(
        _final_kernel,
        grid=(n_steps,),
        in_specs=[tok(D_MODEL), tok(D_ATTN), tok(D_ATTN + D_SSM),
                  pl.BlockSpec((None, N_GROUPS, CHUNK_VEC, LANES), lambda s: (s // STEPS_PER_REGROUP, 0, 0, 0)),
                  whole(wglu, pipeline_mode=pl.Buffered(1)), whole(bglu),
                  whole(wo, pipeline_mode=pl.Buffered(1)), whole(gain_a), whole(gain_s), whole(lng), whole(lnb)],
        out_specs=tok(D_MODEL),
        out_shape=jax.ShapeDtypeStruct((BATCH * SEQ, D_MODEL), jnp.float32),
        scratch_shapes=[pltpu.VMEM((N_LANE_BLOCKS, LANES * SLAB_CHUNK_PITCH, LANES), jnp.float32),
                        pltpu.VMEM(wglu.shape, jnp.bfloat16), pltpu.VMEM(wo.shape, jnp.bfloat16)],
        compiler_params=pltpu.CompilerParams(dimension_semantics=("arbitrary",),
                                             vmem_limit_bytes=VMEM_LIMIT),
        name="final",
    )(x2, ya, sz, gt, wglu, bglu, wo, gain_a, gain_s, lng, lnb)


def kernel(x, w_in, attn_sink, ssm_a_re, ssm_a_im, ssm_log_dt, ssm_b_re, ssm_b_im, ssm_c_re, ssm_c_im,
           ssm_d, w_glu, b_glu, norm_attn_g, norm_ssm_g, w_out, ln_g, ln_b):
    assert x.shape == (BATCH, SEQ, D_MODEL) and w_in.shape[0] == 1
    f32 = jnp.float32
    rope = _rope_tables()

    x2 = x.reshape(BATCH * SEQ, D_MODEL)
    b_t = [jnp.swapaxes(p, -1, -2) for p in (ssm_b_re, ssm_b_im)]
    params = [p[0].astype(f32) for p in (ssm_log_dt, ssm_a_re, ssm_a_im, *b_t, ssm_c_re, ssm_c_im)]
    q, kv, sz, ut, m_intra, w_state, w_so, coef = _inproj(x2, w_in[0], rope, *params,
                                                           ssm_d[0].astype(f32).reshape(1, D_SSM))
    ya = _attention(q.reshape(BATCH, SEQ, Q_COLS), kv.reshape(BATCH, SEQ, 2 * KV_COLS),
                    attn_sink[0].astype(f32))
    gt = _ssm(ut, m_intra, w_state, w_so, coef)

    row = lambda v: v[0].astype(f32)[None, :]
    out = _final(x2, ya.reshape(BATCH * SEQ, D_ATTN), sz, gt, w_glu[0].astype(f32), row(b_glu),
                 w_out[0].astype(f32), row(norm_attn_g), row(norm_ssm_g), row(ln_g), row(ln_b))
    return out.reshape(BATCH, SEQ, D_MODEL)
```

```python
import jax
import jax.numpy as jnp
import numpy as np
from jax import lax
from jax.experimental import pallas as pl
from jax.experimental.pallas import tpu as pltpu

D_MODEL = 1024
BATCH = 4
SEQ = 4096
D_ATTN = 512
D_SSM = 512
HEAD_DIM = 64
N_Q_HEADS = 8
N_KV_HEADS = 2
WINDOW = 128
BLOCK = 128
ROPE_THETA = 10000.0
SSM_CH = 16
N_GROUPS = 32
SSM_STATE = 64
NORM_EPS = 1e-5
NEG_INF = -1e30
DEEPNORM_ALPHA = 2.0 ** 0.25
LOG2_E = 1.4426950408889634
Q_SCALE = HEAD_DIM ** -0.5 * LOG2_E

Q_COLS = N_Q_HEADS * HEAD_DIM
KV_COLS = N_KV_HEADS * HEAD_DIM
CHUNK = 16
N_CHUNKS = SEQ // CHUNK
CHUNK_VEC = CHUNK * SSM_CH
STATE_VEC = 2 * SSM_STATE
LANES = 128
ROWS = BATCH * N_CHUNKS
N_ROW_BLOCKS = ROWS // LANES
TOK_BLOCK = 1024
REGROUP_TOK = LANES * CHUNK
STEPS_PER_REGROUP = REGROUP_TOK // TOK_BLOCK
SUB_ROWS = 512
N_LANE_BLOCKS = D_SSM // LANES
SLAB_CHUNK_PITCH = 20
GROUPS_PER_LANE_BLOCK = LANES // SSM_CH
VMEM_LIMIT = 56 * 1024 * 1024
HIGHEST = lax.Precision.HIGHEST


LAG_ROWS = 512
PREP_GB = N_GROUPS // (BATCH * SEQ // TOK_BLOCK)


def _ssm_tables(ldt_ref, are_ref, aim_ref, bre_ref, bim_ref, cre_ref, cim_ref, d_ref,
                m_ref, ws_ref, wo_ref, coef_ref):
    f32 = jnp.float32
    lo16 = lax.broadcasted_iota(jnp.int32, (SSM_CH, LANES), 1) < SSM_STATE
    lo1 = lax.broadcasted_iota(jnp.int32, (1, LANES), 1) < SSM_STATE
    units = [(gi, d) for gi in range(PREP_GB) for d in range(2)]

    def group(gi):
        return pl.program_id(0) * PREP_GB + gi

    def dup(v):
        return jnp.concatenate([v, v], axis=1)

    lam, zoh = {}, {}
    for gi, d in units:
        ar = dup(are_ref[d, pl.ds(group(gi), 1), :])
        ai = dup(aim_ref[d, pl.ds(group(gi), 1), :])
        dt = jnp.exp(jnp.full((1, LANES), ldt_ref[d, group(gi)], f32))
        zr = dt * ar
        zi = dt * ai
        mag = jnp.exp(zr)
        lr = mag * jnp.cos(zi)
        li = mag * jnp.sin(zi)
        den = ar * ar + ai * ai
        lam[gi, d] = (lr, li)
        zoh[gi, d] = (((lr - 1.0) * ar + li * ai) / den, (li * ar - (lr - 1.0) * ai) / den)

    pr = {u: [jnp.ones((1, LANES), f32)] for u in units}
    pi = {u: [jnp.zeros((1, LANES), f32)] for u in units}
    for _ in range(CHUNK):
        for u in units:
            lr, li = lam[u]
            pr[u].append(pr[u][-1] * lr - pi[u][-1] * li)
            pi[u].append(pr[u][-2] * li + pi[u][-1] * lr)

    lag_tabs, kws = {}, {}
    for gi in range(PREP_GB):
        ws_parts, coef_rows = [], []
        for d in range(2):
            u = (gi, d)
            fr, fi = zoh[u]
            btr = dup(bre_ref[d, gi])
            bti = dup(bim_ref[d, gi])
            bbr = fr * btr - fi * bti
            bbi = fr * bti + fi * btr
            cr = dup(cre_ref[d, gi])
            ci = dup(cim_ref[d, gi])
            pa = [jnp.where(lo1, r, i) for r, i in zip(pr[u], pi[u])]
            pb = [jnp.where(lo1, -i, r) for r, i in zip(pr[u], pi[u])]
            pc = [jnp.where(lo1, i, r) for r, i in zip(pr[u], pi[u])]
            pd = [jnp.where(lo1, r, -i) for r, i in zip(pr[u], pi[u])]

            order = range(CHUNK + 1) if d == 0 else range(CHUNK, -1, -1)
            blocks = [cr * pa[t] + ci * pb[t] for t in order]
            blocks.append(jnp.zeros((LAG_ROWS - (CHUNK + 1) * SSM_CH, LANES), f32))
            lag_tabs[u] = jnp.concatenate(blocks, axis=0)
            lhs = jnp.where(lo16, bbr, -bbi)
            kws[u] = lax.dot_general(lhs, lag_tabs[u], (((1,), (1,)), ((), ())),
                                     precision=HIGHEST, preferred_element_type=f32)

            parts = []
            for j in range(CHUNK):
                t = CHUNK - 1 - j if d == 0 else j
                parts.append(jnp.concatenate([bbr * pa[t] + bbi * pb[t],
                                              bbr * pc[t] + bbi * pd[t]], axis=1))
            ws_parts.append(jnp.concatenate(parts, axis=0))

            a2 = jnp.where(lo1, -pi[u][CHUNK], pi[u][CHUNK])
            coef_rows += [pr[u][CHUNK], a2, -a2]

        ws_ref[gi] = jnp.concatenate(ws_parts, axis=1).astype(ws_ref.dtype)
        coef_ref[gi] = jnp.concatenate(coef_rows + [jnp.zeros((2, LANES), f32)], axis=0)

    sign = jnp.where(lax.broadcasted_iota(jnp.int32, (LANES, CHUNK_VEC), 0) < SSM_STATE, 1.0, -1.0)
    sub = lax.broadcasted_iota(jnp.int32, (SSM_CH, CHUNK_VEC), 0)
    lane = lax.broadcasted_iota(jnp.int32, (SSM_CH, CHUNK_VEC), 1)
    for gi in range(PREP_GB):
        wo_f = jnp.transpose(lag_tabs[gi, 0][SSM_CH:SSM_CH + CHUNK_VEC, :]) * sign
        wo_b = jnp.transpose(lag_tabs[gi, 1][:CHUNK_VEC, :]) * sign
        wo_ref[gi] = jnp.concatenate([wo_f, wo_b], axis=0).astype(wo_ref.dtype)

        g = group(gi)
        dblk = d_ref[:, pl.ds(pl.multiple_of((g // GROUPS_PER_LANE_BLOCK) * LANES, LANES), LANES)]
        dblk = pltpu.roll(dblk, (LANES - (g % GROUPS_PER_LANE_BLOCK) * SSM_CH) % LANES, 1)
        dblk = jnp.where(lax.broadcasted_iota(jnp.int32, (1, LANES), 1) < SSM_CH, dblk, 0.0)
        for shift in (SSM_CH, 2 * SSM_CH, 4 * SSM_CH):
            dblk = dblk + pltpu.roll(dblk, shift, 1)
        dtile = jnp.broadcast_to(jnp.concatenate([dblk, dblk], axis=1), (SSM_CH, CHUNK_VEC))
        for jp in range(CHUNK):
            fwd = kws[gi, 0] if jp == 0 else pltpu.roll(kws[gi, 0], SSM_CH * jp, 1)
            bwd = pltpu.roll(kws[gi, 1], (LAG_ROWS - SSM_CH * (CHUNK - jp)) % LAG_ROWS, 1)
            skip = jnp.where((lane // SSM_CH == jp) & (lane % SSM_CH == sub), dtile, 0.0)
            blk = fwd[:, :CHUNK_VEC] + bwd[:, :CHUNK_VEC] + skip
            m_ref[gi, jp * SSM_CH:(jp + 1) * SSM_CH, :] = blk.astype(m_ref.dtype)


def _rope_tables():
    half = HEAD_DIM // 2
    inv_freq = ROPE_THETA ** (-np.arange(half, dtype=np.float64) / half)
    ang = np.arange(SEQ, dtype=np.float64)[:, None] * inv_freq[None, :]
    cos, sin = np.cos(ang), np.sin(ang)
    return jnp.asarray(np.concatenate([cos, cos, cos, cos, -sin, sin, -sin, sin], axis=1), jnp.float32)


def _rope_block(xb, cos_t, sin_t, first_half):
    swapped = jnp.where(first_half, pltpu.roll(xb, 96, 1), pltpu.roll(xb, 32, 1))
    return xb * cos_t + swapped * sin_t


COL_Q = (0, Q_COLS)
COL_KV = (COL_Q[1], COL_Q[1] + 2 * KV_COLS)
COL_ZA = (COL_KV[1], COL_KV[1] + D_ATTN)
COL_U = (COL_ZA[1], COL_ZA[1] + D_SSM)
COL_ZS = (COL_U[1], COL_U[1] + D_SSM)


def _silu(z):
    h = 0.5 * z
    return h + h * jnp.tanh(h)


def _inproj_kernel(x_ref, w32_ref, rope_ref, ldt_ref, are_ref, aim_ref, bre_ref, bim_ref, cre_ref, cim_ref, d_ref,
                   q_ref, kv_ref, sz_ref, ut_ref, m_ref, ws_ref, wo_ref, coef_ref, uslab_ref, w_ref):
    step = pl.program_id(0)
    part = step % STEPS_PER_REGROUP
    lane = lax.broadcasted_iota(jnp.int32, (SUB_ROWS, LANES), 1)
    first_half = (lane % HEAD_DIM) < (HEAD_DIM // 2)

    @pl.when(step == 0)
    def _():
        for c0 in range(0, w_ref.shape[1], 2 * LANES):
            w_ref[:, c0:c0 + 2 * LANES] = w32_ref[:, c0:c0 + 2 * LANES].astype(w_ref.dtype)

    def proj(xb, cols):
        return jnp.dot(xb, w_ref[:, cols[0]:cols[1]], preferred_element_type=jnp.float32)

    for sub in range(TOK_BLOCK // SUB_ROWS):
        rows = slice(sub * SUB_ROWS, (sub + 1) * SUB_ROWS)
        xb = x_ref[rows, :].astype(jnp.bfloat16)
        cos_t = rope_ref[rows, :LANES]
        sin_t = rope_ref[rows, LANES:]
        q = proj(xb, COL_Q)
        kv = proj(xb, COL_KV)
        q_blocks = [_rope_block(q[:, v * LANES:(v + 1) * LANES], cos_t, sin_t, first_half)
                    for v in range(Q_COLS // LANES)]
        q_ref[rows, :] = (jnp.concatenate(q_blocks, axis=1) * Q_SCALE).astype(q_ref.dtype)
        za = proj(xb, COL_ZA)
        k_rot = _rope_block(kv[:, :LANES], cos_t, sin_t, first_half)
        kv_ref[rows, :] = jnp.concatenate([k_rot, kv[:, LANES:]], axis=1).astype(kv_ref.dtype)
        u = proj(xb, COL_U)
        sz_ref[rows, :D_ATTN] = _silu(za).astype(sz_ref.dtype)
        zs = proj(xb, COL_ZS)
        for kk in range(SUB_ROWS // CHUNK):
            base = pl.multiple_of((part * (TOK_BLOCK // CHUNK) + sub * (SUB_ROWS // CHUNK)) * SLAB_CHUNK_PITCH, 8)
            for v in range(N_LANE_BLOCKS):
                uslab_ref[v, pl.ds(base + kk * SLAB_CHUNK_PITCH, CHUNK), :] = (
                    u[kk * CHUNK:(kk + 1) * CHUNK, v * LANES:(v + 1) * LANES])
        sz_ref[rows, D_ATTN:] = _silu(zs).astype(sz_ref.dtype)

    _ssm_tables(ldt_ref, are_ref, aim_ref, bre_ref, bim_ref, cre_ref, cim_ref, d_ref, m_ref, ws_ref, wo_ref, coef_ref)

    @pl.when(part == STEPS_PER_REGROUP - 1)
    def _():
        for j in range(CHUNK):
            for v in range(N_LANE_BLOCKS):
                uj = uslab_ref[v, pl.ds(j, LANES, stride=SLAB_CHUNK_PITCH), :]
                ujt = jnp.transpose(uj.astype(ut_ref.dtype)).reshape(GROUPS_PER_LANE_BLOCK, SSM_CH, LANES)
                g0 = v * GROUPS_PER_LANE_BLOCK
                ut_ref[g0:g0 + GROUPS_PER_LANE_BLOCK, j * SSM_CH:(j + 1) * SSM_CH, :] = ujt


def _inproj(x2, w, rope, log_dt, a_re, a_im, b_re, b_im, c_re, c_im, d):
    n_steps = BATCH * SEQ // TOK_BLOCK
    blocks_per_seq = SEQ // TOK_BLOCK
    bf16 = jnp.bfloat16

    def tok(cols):
        return pl.BlockSpec((TOK_BLOCK, cols), lambda s: (s, 0))

    def whole(a):
        return pl.BlockSpec(a.shape, lambda s: (0,) * a.ndim)

    def group_in(a):
        return pl.BlockSpec((2, PREP_GB) + a.shape[2:], lambda s: (0, s, 0, 0))

    def group_out(rows, cols):
        return pl.BlockSpec((PREP_GB, rows, cols), lambda s: (s, 0, 0))

    return pl.pallas_call(
        _inproj_kernel,
        grid=(n_steps,),
        in_specs=[tok(D_MODEL), pl.BlockSpec(w.shape, lambda s: (0, 0), pipeline_mode=pl.Buffered(1)),
                  pl.BlockSpec((TOK_BLOCK, 2 * LANES), lambda s: (s % blocks_per_seq, 0)),
                  pl.BlockSpec(memory_space=pltpu.SMEM), whole(a_re), whole(a_im),
                  group_in(b_re), group_in(b_im), group_in(c_re), group_in(c_im), whole(d)],
        out_specs=[tok(Q_COLS), tok(2 * KV_COLS), tok(D_ATTN + D_SSM),
                   pl.BlockSpec((None, N_GROUPS, CHUNK_VEC, LANES), lambda s: (s // STEPS_PER_REGROUP, 0, 0, 0)),
                   group_out(CHUNK_VEC, CHUNK_VEC), group_out(CHUNK_VEC, 4 * LANES),
                   group_out(2 * STATE_VEC, CHUNK_VEC), group_out(8, LANES)],
        out_shape=[jax.ShapeDtypeStruct((BATCH * SEQ, Q_COLS), bf16),
                   jax.ShapeDtypeStruct((BATCH * SEQ, 2 * KV_COLS), bf16),
                   jax.ShapeDtypeStruct((BATCH * SEQ, D_ATTN + D_SSM), bf16),
                   jax.ShapeDtypeStruct((N_ROW_BLOCKS, N_GROUPS, CHUNK_VEC, LANES), bf16),
                   jax.ShapeDtypeStruct((N_GROUPS, CHUNK_VEC, CHUNK_VEC), bf16),
                   jax.ShapeDtypeStruct((N_GROUPS, CHUNK_VEC, 4 * LANES), bf16),
                   jax.ShapeDtypeStruct((N_GROUPS, 2 * STATE_VEC, CHUNK_VEC), bf16),
                   jax.ShapeDtypeStruct((N_GROUPS, 8, LANES), jnp.float32)],
        scratch_shapes=[pltpu.VMEM((N_LANE_BLOCKS, LANES * SLAB_CHUNK_PITCH, LANES), jnp.float32),
                        pltpu.VMEM(w.shape, bf16)],
        compiler_params=pltpu.CompilerParams(dimension_semantics=("arbitrary",),
                                             vmem_limit_bytes=VMEM_LIMIT),
        name="inproj",
    )(x2, w, rope, log_dt, a_re, a_im, b_re, b_im, c_re, c_im, d)


ATTN_SUB = 32
ATTN_STEP = ATTN_SUB * BLOCK
N_KEYS = 3 * BLOCK


def _attn_kernel(q_ref, kp_ref, kc_ref, kn_ref, sink_ref, y_ref):
    i = pl.program_id(1)
    kvw = jnp.concatenate([kp_ref[...], kc_ref[...], kn_ref[...]], axis=0).astype(jnp.float32)

    n_win = kvw.shape[0]
    lane_w = lax.broadcasted_iota(jnp.int32, (n_win, LANES), 1)
    k_lo, k_hi, vt_lo, vt_hi = [], [], [], []
    for hk in range(N_KV_HEADS):
        in_head = (lane_w >= hk * HEAD_DIM) & (lane_w < (hk + 1) * HEAD_DIM)
        k_nat = jnp.where(in_head, kvw[:, :LANES], 0.0)
        v_nat = jnp.where(in_head, kvw[:, LANES:], 0.0)
        k_oth = pltpu.roll(k_nat, HEAD_DIM, 1)
        v_oth = pltpu.roll(v_nat, HEAD_DIM, 1)
        pairs = ((k_nat, k_oth), (v_nat, v_oth)) if hk == 0 else ((k_oth, k_nat), (v_oth, v_nat))
        k_lo.append(pairs[0][0].astype(jnp.bfloat16))
        k_hi.append(pairs[0][1].astype(jnp.bfloat16))
        vt_lo.append(jnp.transpose(pairs[1][0]).astype(jnp.bfloat16))
        vt_hi.append(jnp.transpose(pairs[1][1]).astype(jnp.bfloat16))

    c = lax.broadcasted_iota(jnp.int32, (N_KEYS, BLOCK), 0)
    r = lax.broadcasted_iota(jnp.int32, (N_KEYS, BLOCK), 1)
    band = (c >= r) & (c - r <= 2 * WINDOW)
    ones_row = lax.broadcasted_iota(jnp.int32, (16, 2 * N_KEYS), 0)
    ones_col = lax.broadcasted_iota(jnp.int32, (16, 2 * N_KEYS), 1)
    ones = jnp.where(((ones_row == 0) & (ones_col < N_KEYS)) | ((ones_row == 1) & (ones_col >= N_KEYS)),
                     1.0, 0.0).astype(jnp.bfloat16)
    row_o = lax.broadcasted_iota(jnp.int32, (LANES, 2 * BLOCK), 0)
    n_blocks = SEQ // BLOCK

    def scores(t, hk):
        blk = i * ATTN_SUB + t
        lo = jnp.where(blk == 0, BLOCK, 0)
        hi = jnp.where(blk == n_blocks - 1, 2 * BLOCK, N_KEYS)
        valid1 = band & (c >= lo) & (c < hi)
        valid = jnp.concatenate([valid1, valid1], axis=1)
        win = slice(t * BLOCK, t * BLOCK + N_KEYS)
        q = q_ref[t * BLOCK:(t + 1) * BLOCK, (2 * hk) * LANES:(2 * hk + 2) * LANES]
        qq = jnp.concatenate([q[:, :LANES], q[:, LANES:]], axis=0)
        kcat = jnp.concatenate([k_lo[hk][win], k_hi[hk][win]], axis=0)
        st = lax.dot_general(kcat, qq, (((1,), (1,)), ((), ())),
                             preferred_element_type=jnp.float32)
        sps, ms, sinks = [], [], []
        for par in range(2):
            base = par * N_KEYS
            sp = jnp.concatenate(
                [jnp.where(valid[:BLOCK], st[base:base + BLOCK, :], NEG_INF),
                 st[base + BLOCK:base + 2 * BLOCK, :],
                 jnp.where(valid[2 * BLOCK:], st[base + 2 * BLOCK:base + N_KEYS, :], NEG_INF)], axis=0)
            h0 = 4 * hk + par
            h1 = h0 + 2
            sink = jnp.concatenate([jnp.full((1, BLOCK), sink_ref[h] * LOG2_E, jnp.float32)
                                    for h in (h0, h1)], axis=1)
            sps.append(sp)
            ms.append(jnp.maximum(jnp.max(sp, axis=0, keepdims=True), sink))
            sinks.append(sink)
        return sps, ms, sinks

    def weights(sps, ms, sinks):
        return [jnp.exp2((sp - m).astype(jnp.bfloat16)) for sp, m in zip(sps, ms)], ms, sinks

    def finish(t, hk, ps, ms, sinks):
        win = slice(t * BLOCK, t * BLOCK + N_KEYS)
        vt = jnp.concatenate([jnp.concatenate([vt_lo[hk][:, win], vt_hi[hk][:, win]], axis=1), ones], axis=0)
        ot = jnp.dot(vt, jnp.concatenate(ps, axis=0), preferred_element_type=jnp.float32)
        inv0 = 1.0 / (ot[LANES:LANES + 1, :] + jnp.exp2(sinks[0] - ms[0]))
        inv1 = 1.0 / (ot[LANES + 1:LANES + 2, :] + jnp.exp2(sinks[1] - ms[1]))
        o = jnp.transpose(ot[:LANES, :] * jnp.where(row_o < HEAD_DIM, inv0, inv1))
        rows = slice(t * BLOCK, (t + 1) * BLOCK)
        y_ref[rows, (2 * hk) * LANES:(2 * hk + 1) * LANES] = o[:BLOCK].astype(y_ref.dtype)
        y_ref[rows, (2 * hk + 1) * LANES:(2 * hk + 2) * LANES] = o[BLOCK:].astype(y_ref.dtype)

    work = [(t, hk) for t in range(ATTN_SUB) for hk in range(N_KV_HEADS)]
    stage_a = {0: scores(*work[0])}
    stage_b = {}
    for n in range(len(work) + 1):
        if n + 1 < len(work):
            stage_a[n + 1] = scores(*work[n + 1])
        if n < len(work):
            stage_b[n] = weights(*stage_a.pop(n))
        if n >= 1:
            finish(*work[n - 1], *stage_b.pop(n - 1))


def _attention(q, kv, sink):
    nb = SEQ // BLOCK
    return pl.pallas_call(
        _attn_kernel,
        grid=(BATCH, SEQ // ATTN_STEP),
        in_specs=[pl.BlockSpec((None, ATTN_STEP, Q_COLS), lambda b, i: (b, i, 0)),
                  pl.BlockSpec((None, BLOCK, 2 * KV_COLS),
                               lambda b, i: (b, jnp.maximum(i * ATTN_SUB - 1, 0), 0)),
                  pl.BlockSpec((None, ATTN_STEP, 2 * KV_COLS), lambda b, i: (b, i, 0)),
                  pl.BlockSpec((None, BLOCK, 2 * KV_COLS),
                               lambda b, i: (b, jnp.minimum((i + 1) * ATTN_SUB, nb - 1), 0)),
                  pl.BlockSpec(memory_space=pltpu.SMEM)],
        out_specs=pl.BlockSpec((None, ATTN_STEP, D_ATTN), lambda b, i: (b, i, 0)),
        out_shape=jax.ShapeDtypeStruct((BATCH, SEQ, D_ATTN), jnp.bfloat16),
        compiler_params=pltpu.CompilerParams(dimension_semantics=("arbitrary", "arbitrary"),
                                             vmem_limit_bytes=VMEM_LIMIT),
        name="attention",
    )(q, kv, kv, kv, sink)


SSM_GB = 4
SSM_STEPS = N_GROUPS // SSM_GB
N_CHAINS = SSM_GB * BATCH
SLOT_PITCH = 36
RECUR_SLAB_ROWS = (N_CHUNKS + 1) * SLOT_PITCH + 4


def _gelu_tanh(x):
    k = (2.0 / np.pi) ** 0.5
    half = 0.5 * x
    return half + half * jnp.tanh(x * (k + (k * 0.044715) * (x * x)))


def _ssm_kernel(ut_ref, m_ref, ws_ref, wo_ref, coef_ref, gt_ref,
                sf_ref, sb_ref, x_ref, u_even_ref, u_odd_ref):
    s = pl.program_id(0)

    def chain_rows(gi, b, second, first_slot):
        return pl.ds(first_slot * SLOT_PITCH + 2 * (gi * BATCH + b) + second, N_CHUNKS, stride=SLOT_PITCH)

    def head(gi, u_ref):
        ut = jnp.concatenate([ut_ref[lb, gi] for lb in range(N_ROW_BLOCKS)], axis=1)
        u = jnp.transpose(ut)
        u_ref[gi] = u
        st = jnp.dot(u, ws_ref[gi], preferred_element_type=jnp.float32)
        for b in range(BATCH):
            rows = slice(b * N_CHUNKS, (b + 1) * N_CHUNKS)
            sf_ref[chain_rows(gi, b, 0, 0), :] = st[rows, 0 * LANES:1 * LANES]
            sf_ref[chain_rows(gi, b, 1, 0), :] = st[rows, 1 * LANES:2 * LANES]
            sb_ref[chain_rows(gi, b, 0, 0), :] = st[rows, 2 * LANES:3 * LANES]
            sb_ref[chain_rows(gi, b, 1, 0), :] = st[rows, 3 * LANES:4 * LANES]

    def chunk_out(gi, u_ref):
        xin_f = jnp.concatenate([x_ref[chain_rows(gi, b, 0, 0), :] for b in range(BATCH)], axis=0)
        xin_b = jnp.concatenate([x_ref[chain_rows(gi, b, 1, 1), :] for b in range(BATCH)], axis=0)
        lhs = jnp.concatenate([u_ref[gi], xin_f.astype(jnp.bfloat16), xin_b.astype(jnp.bfloat16)], axis=1)
        rhs = jnp.concatenate([m_ref[gi], wo_ref[gi]], axis=0)
        return jnp.dot(lhs, rhs, preferred_element_type=jnp.float32)

    def store_out(gi, y):
        gt = jnp.transpose(_gelu_tanh(y).astype(gt_ref.dtype))
        for lb in range(N_ROW_BLOCKS):
            gt_ref[lb, gi] = gt[:, lb * LANES:(lb + 1) * LANES]

    def head_and_tail(u_head_ref, u_tail_ref):
        y = chunk_out(0, u_tail_ref) if u_tail_ref is not None else None
        for gi in range(SSM_GB):
            if u_head_ref is not None:
                head(gi, u_head_ref)
            if u_tail_ref is not None:
                y_next = chunk_out(gi + 1, u_tail_ref) if gi + 1 < SSM_GB else None
                store_out(gi, y)
                y = y_next

    u_refs = (u_even_ref, u_odd_ref)
    inner = (s > 0) & (s < SSM_STEPS)
    pl.when(s == 0)(lambda: head_and_tail(u_refs[0], None))
    pl.when(inner & (s % 2 == 0))(lambda: head_and_tail(u_refs[0], u_refs[1]))
    pl.when(inner & (s % 2 == 1))(lambda: head_and_tail(u_refs[1], u_refs[0]))
    pl.when(s == SSM_STEPS)(lambda: head_and_tail(None, u_refs[(SSM_STEPS - 1) % 2]))

    @pl.when(s < SSM_STEPS)
    def _():
        n_halves = N_CHAINS // 8

        def chains_at(slot, half, second):
            return pl.ds(slot * SLOT_PITCH + 16 * half + second, 8, stride=2)

        zero = jnp.zeros((8, LANES), jnp.float32)
        coefs = []
        for h in range(n_halves):
            rows = [jnp.concatenate([jnp.broadcast_to(coef_ref[2 * h + m, r:r + 1, :], (BATCH, LANES))
                                     for m in range(2)], axis=0) for r in range(6)]
            per_dir = []
            for a1, a2, a3 in (rows[:3], rows[3:]):
                b2 = 2.0 * a1 * a2
                per_dir.append((a1, a2, a3, a1 * a1 - a2 * a2, b2, -b2))
            coefs.append(per_dir)
            x_ref[chains_at(0, h, 0), :] = zero
            x_ref[chains_at(N_CHUNKS, h, 1), :] = zero

        def two_chunks(coef, x, xs, s_ref, half, slot0, slot1, second, out0, out1):
            a1, a2, a3, b1, b2, b3 = coef
            s0, ss0 = s_ref[chains_at(slot0, half, 0), :], s_ref[chains_at(slot0, half, 1), :]
            s1, ss1 = s_ref[chains_at(slot1, half, 0), :], s_ref[chains_at(slot1, half, 1), :]
            x_ref[chains_at(out0, half, second), :] = a1 * x + a2 * xs + s0
            c = a1 * s0 + a2 * ss0 + s1
            cs = a1 * ss0 + a3 * s0 + ss1
            nx = b1 * x + b2 * xs + c
            nxs = b1 * xs + b3 * x + cs
            x_ref[chains_at(out1, half, second), :] = nx
            return nx, nxs

        def trip(t, carry):
            k = 2 * t
            kb = N_CHUNKS - 1 - k
            out = []
            for h in range(n_halves):
                xf, xfs, xb, xbs = carry[4 * h:4 * h + 4]
                out += two_chunks(coefs[h][0], xf, xfs, sf_ref, h, k, k + 1, 0, k + 1, k + 2)
                out += two_chunks(coefs[h][1], xb, xbs, sb_ref, h, kb, kb - 1, 1, kb, kb - 1)
            return tuple(out)

        lax.fori_loop(0, N_CHUNKS // 2, trip, (zero,) * (4 * n_halves), unroll=4)


def _ssm(ut, m_intra, w_state, w_out, coef):
    def head_block(s):
        return jnp.minimum(s, SSM_STEPS - 1)

    def tail_block(s):
        return jnp.maximum(s - 1, 0)

    def per_group(shape, block):
        return pl.BlockSpec((SSM_GB,) + shape, lambda s: (block(s), 0, 0))

    def lane_blocked(block):
        return pl.BlockSpec((N_ROW_BLOCKS, SSM_GB, CHUNK_VEC, LANES), lambda s: (0, block(s), 0, 0))

    slab = pltpu.VMEM((RECUR_SLAB_ROWS, LANES), jnp.float32)
    return pl.pallas_call(
        _ssm_kernel,
        grid=(SSM_STEPS + 1,),
        in_specs=[lane_blocked(head_block), per_group((CHUNK_VEC, CHUNK_VEC), tail_block),
                  per_group((CHUNK_VEC, 4 * LANES), head_block), per_group((2 * STATE_VEC, CHUNK_VEC), tail_block),
                  per_group((8, LANES), head_block)],
        out_specs=lane_blocked(tail_block),
        out_shape=jax.ShapeDtypeStruct((N_ROW_BLOCKS, N_GROUPS, CHUNK_VEC, LANES), jnp.bfloat16),
        scratch_shapes=[slab] * 3 + [pltpu.VMEM((SSM_GB, ROWS, CHUNK_VEC), jnp.bfloat16)] * 2,
        compiler_params=pltpu.CompilerParams(dimension_semantics=("arbitrary",),
                                             vmem_limit_bytes=VMEM_LIMIT),
        name="ssm",
    )(ut, m_intra, w_state, w_out, coef)


FINAL_SUB_ROWS = 256


def _unit_rms(v):
    return v * lax.rsqrt(jnp.mean(v * v, axis=-1, keepdims=True) + NORM_EPS)


def _final_kernel(x_ref, ya_ref, sz_ref, gt_ref, wglu32_ref, bglu_ref, wo32_ref, ga_ref, gs_ref,
                  lng_ref, lnb_ref, o_ref, gslab_ref, wglu_ref, wo_ref):
    step = pl.program_id(0)
    part = step % STEPS_PER_REGROUP

    @pl.when(step == 0)
    def _():
        wglu_ref[...] = wglu32_ref[...].astype(wglu_ref.dtype)
        for gain_ref, row0 in ((ga_ref, 0), (gs_ref, D_ATTN)):
            for v in range(gain_ref.shape[1] // LANES):
                gain = gain_ref[:, v * LANES:(v + 1) * LANES] / DEEPNORM_ALPHA
                col = jnp.transpose(jnp.broadcast_to(gain, (LANES, LANES)))
                rows = slice(row0 + v * LANES, row0 + (v + 1) * LANES)
                for c0 in range(0, D_MODEL, LANES):
                    wo_ref[rows, c0:c0 + LANES] = (wo32_ref[rows, c0:c0 + LANES] * col).astype(wo_ref.dtype)

    @pl.when(part == 0)
    def _():
        for j in range(CHUNK):
            for v in range(N_LANE_BLOCKS):
                g0 = v * GROUPS_PER_LANE_BLOCK
                gjt = gt_ref[g0:g0 + GROUPS_PER_LANE_BLOCK, j * SSM_CH:(j + 1) * SSM_CH, :]
                gj = jnp.transpose(gjt.reshape(LANES, LANES))
                gslab_ref[v, pl.ds(j, LANES, stride=SLAB_CHUNK_PITCH), :] = gj.astype(jnp.float32)

    def rows_of(sub):
        return slice(sub * FINAL_SUB_ROWS, (sub + 1) * FINAL_SUB_ROWS)

    def gate_in(sub):
        rows = rows_of(sub)
        ya = ya_ref[rows, :].astype(jnp.float32) * sz_ref[rows, :D_ATTN].astype(jnp.float32)
        ya = _unit_rms(ya).astype(jnp.bfloat16)
        base = pl.multiple_of((part * (TOK_BLOCK // CHUNK) + sub * (FINAL_SUB_ROWS // CHUNK)) * SLAB_CHUNK_PITCH, 8)
        g = jnp.concatenate(
            [jnp.concatenate([gslab_ref[v, pl.ds(base + kk * SLAB_CHUNK_PITCH, CHUNK), :]
                              for kk in range(FINAL_SUB_ROWS // CHUNK)], axis=0)
             for v in range(N_LANE_BLOCKS)], axis=1)
        return ya, g

    def glu(sub, ya, g):
        rows = rows_of(sub)
        gate = jnp.dot(g.astype(jnp.bfloat16), wglu_ref[...], preferred_element_type=jnp.float32) + bglu_ref[...]
        ys = g * jax.nn.sigmoid(gate)
        ys = ys * sz_ref[rows, D_ATTN:].astype(jnp.float32)
        return ya, _unit_rms(ys).astype(jnp.bfloat16)

    def project(sub, ya, ys):
        return (jnp.dot(ya, wo_ref[:D_ATTN, :], preferred_element_type=jnp.float32)
                + jnp.dot(ys, wo_ref[D_ATTN:, :], preferred_element_type=jnp.float32))

    def layer_norm(sub, out):
        rows = rows_of(sub)
        h = x_ref[rows, :] + out
        mu = jnp.mean(h, axis=-1, keepdims=True)
        hc = h - mu
        var = jnp.mean(hc * hc, axis=-1, keepdims=True)
        o_ref[rows, :] = hc * lax.rsqrt(var + NORM_EPS / DEEPNORM_ALPHA ** 2) * lng_ref[...] + lnb_ref[...]

    n_sub = TOK_BLOCK // FINAL_SUB_ROWS
    stage_a, stage_b = {}, {}
    for n in range(n_sub + 2):
        if n < n_sub:
            stage_a[n] = gate_in(n)
        if 1 <= n <= n_sub:
            stage_b[n - 1] = glu(n - 1, *stage_a.pop(n - 1))
        if n >= 2:
            layer_norm(n - 2, project(n - 2, *stage_b.pop(n - 2)))


def _final(x2, ya, sz, gt, wglu, bglu, wo, gain_a, gain_s, lng, lnb):
    n_steps = BATCH * SEQ // TOK_BLOCK

    def tok(cols):
        return pl.BlockSpec((TOK_BLOCK, cols), lambda s: (s, 0))

    def whole(a, **kw):
        return pl.BlockSpec(a.shape, lambda s: (0,) * a.ndim, **kw)

    return pl.pallas_call(
        _final_kernel,
        grid=(n_steps,),
        in_specs=[tok(D_MODEL), tok(D_ATTN), tok(D_ATTN + D_SSM),
                  pl.BlockSpec((None, N_GROUPS, CHUNK_VEC, LANES), lambda s: (s // STEPS_PER_REGROUP, 0, 0, 0)),
                  whole(wglu, pipeline_mode=pl.Buffered(1)), whole(bglu),
                  whole(wo, pipeline_mode=pl.Buffered(1)), whole(gain_a), whole(gain_s), whole(lng), whole(lnb)],
        out_specs=tok(D_MODEL),
        out_shape=jax.ShapeDtypeStruct((BATCH * SEQ, D_MODEL), jnp.float32),
        scratch_shapes=[pltpu.VMEM((N_LANE_BLOCKS, LANES * SLAB_CHUNK_PITCH, LANES), jnp.float32),
                        pltpu.VMEM(wglu.shape, jnp.bfloat16), pltpu.VMEM(wo.shape, jnp.bfloat16)],
        compiler_params=pltpu.CompilerParams(dimension_semantics=("arbitrary",),
                                             vmem_limit_bytes=VMEM_LIMIT),
        name="final",
    )(x2, ya, sz, gt, wglu, bglu, wo, gain_a, gain_s, lng, lnb)


def kernel(x, w_in, attn_sink, ssm_a_re, ssm_a_im, ssm_log_dt, ssm_b_re, ssm_b_im, ssm_c_re, ssm_c_im,
           ssm_d, w_glu, b_glu, norm_attn_g, norm_ssm_g, w_out, ln_g, ln_b):
    assert x.shape == (BATCH, SEQ, D_MODEL) and w_in.shape[0] == 1
    f32 = jnp.float32
    rope = _rope_tables()

    x2 = x.reshape(BATCH * SEQ, D_MODEL)
    b_t = [jnp.swapaxes(p, -1, -2) for p in (ssm_b_re, ssm_b_im)]
    params = [p[0].astype(f32) for p in (ssm_log_dt, ssm_a_re, ssm_a_im, *b_t, ssm_c_re, ssm_c_im)]
    q, kv, sz, ut, m_intra, w_state, w_so, coef = _inproj(x2, w_in[0], rope, *params,
                                                           ssm_d[0].astype(f32).reshape(1, D_SSM))
    ya = _attention(q.reshape(BATCH, SEQ, Q_COLS), kv.reshape(BATCH, SEQ, 2 * KV_COLS),
                    attn_sink[0].astype(f32))
    gt = _ssm(ut, m_intra, w_state, w_so, coef)

    row = lambda v: v[0].astype(f32)[None, :]
    out = _final(x2, ya.reshape(BATCH * SEQ, D_ATTN), sz, gt, w_glu[0].astype(f32), row(b_glu),
                 w_out[0].astype(f32), row(norm_attn_g), row(norm_ssm_g), row(ln_g), row(ln_b))
    return out.reshape(BATCH, SEQ, D_MODEL)
```

```python
import jax
import jax.numpy as jnp
import numpy as np
from jax import lax
from jax.experimental import pallas as pl
from jax.experimental.pallas import tpu as pltpu

D_MODEL = 1024
BATCH = 4
SEQ = 4096
D_ATTN = 512
D_SSM = 512
HEAD_DIM = 64
N_Q_HEADS = 8
N_KV_HEADS = 2
WINDOW = 128
BLOCK = 128
ROPE_THETA = 10000.0
SSM_CH = 16
N_GROUPS = 32
SSM_STATE = 64
NORM_EPS = 1e-5
NEG_INF = -1e30
DEEPNORM_ALPHA = 2.0 ** 0.25
LOG2_E = 1.4426950408889634
Q_SCALE = HEAD_DIM ** -0.5 * LOG2_E

Q_COLS = N_Q_HEADS * HEAD_DIM
KV_COLS = N_KV_HEADS * HEAD_DIM
CHUNK = 16
N_CHUNKS = SEQ // CHUNK
CHUNK_VEC = CHUNK * SSM_CH
STATE_VEC = 2 * SSM_STATE
LANES = 128
ROWS = BATCH * N_CHUNKS
N_ROW_BLOCKS = ROWS // LANES
TOK_BLOCK = 1024
REGROUP_TOK = LANES * CHUNK
STEPS_PER_REGROUP = REGROUP_TOK // TOK_BLOCK
SUB_ROWS = 512
N_LANE_BLOCKS = D_SSM // LANES
SLAB_CHUNK_PITCH = 20
GROUPS_PER_LANE_BLOCK = LANES // SSM_CH
VMEM_LIMIT = 56 * 1024 * 1024
HIGHEST = lax.Precision.HIGHEST


LAG_ROWS = 512
PREP_GB = N_GROUPS // (BATCH * SEQ // TOK_BLOCK)


def _ssm_tables(ldt_ref, are_ref, aim_ref, bre_ref, bim_ref, cre_ref, cim_ref, d_ref,
                m_ref, ws_ref, wo_ref, coef_ref):
    f32 = jnp.float32
    lo16 = lax.broadcasted_iota(jnp.int32, (SSM_CH, LANES), 1) < SSM_STATE
    lo1 = lax.broadcasted_iota(jnp.int32, (1, LANES), 1) < SSM_STATE
    units = [(gi, d) for gi in range(PREP_GB) for d in range(2)]

    def group(gi):
        return pl.program_id(0) * PREP_GB + gi

    def dup(v):
        return jnp.concatenate([v, v], axis=1)

    lam, zoh = {}, {}
    for gi, d in units:
        ar = dup(are_ref[d, pl.ds(group(gi), 1), :])
        ai = dup(aim_ref[d, pl.ds(group(gi), 1), :])
        dt = jnp.exp(jnp.full((1, LANES), ldt_ref[d, group(gi)], f32))
        zr = dt * ar
        zi = dt * ai
        mag = jnp.exp(zr)
        lr = mag * jnp.cos(zi)
        li = mag * jnp.sin(zi)
        den = ar * ar + ai * ai
        lam[gi, d] = (lr, li)
        zoh[gi, d] = (((lr - 1.0) * ar + li * ai) / den, (li * ar - (lr - 1.0) * ai) / den)

    pr = {u: [jnp.ones((1, LANES), f32)] for u in units}
    pi = {u: [jnp.zeros((1, LANES), f32)] for u in units}
    for _ in range(CHUNK):
        for u in units:
            lr, li = lam[u]
            pr[u].append(pr[u][-1] * lr - pi[u][-1] * li)
            pi[u].append(pr[u][-2] * li + pi[u][-1] * lr)

    lag_tabs, kws = {}, {}
    for gi in range(PREP_GB):
        ws_parts, coef_rows = [], []
        for d in range(2):
            u = (gi, d)
            fr, fi = zoh[u]
            btr = dup(bre_ref[d, gi])
            bti = dup(bim_ref[d, gi])
            bbr = fr * btr - fi * bti
            bbi = fr * bti + fi * btr
            cr = dup(cre_ref[d, gi])
            ci = dup(cim_ref[d, gi])
            pa = [jnp.where(lo1, r, i) for r, i in zip(pr[u], pi[u])]
            pb = [jnp.where(lo1, -i, r) for r, i in zip(pr[u], pi[u])]
            pc = [jnp.where(lo1, i, r) for r, i in zip(pr[u], pi[u])]
            pd = [jnp.where(lo1, r, -i) for r, i in zip(pr[u], pi[u])]

            order = range(CHUNK + 1) if d == 0 else range(CHUNK, -1, -1)
            blocks = [cr * pa[t] + ci * pb[t] for t in order]
            blocks.append(jnp.zeros((LAG_ROWS - (CHUNK + 1) * SSM_CH, LANES), f32))
            lag_tabs[u] = jnp.concatenate(blocks, axis=0)
            lhs = jnp.where(lo16, bbr, -bbi)
            kws[u] = lax.dot_general(lhs, lag_tabs[u], (((1,), (1,)), ((), ())),
                                     precision=HIGHEST, preferred_element_type=f32)

            parts = []
            for j in range(CHUNK):
                t = CHUNK - 1 - j if d == 0 else j
                parts.append(jnp.concatenate([bbr * pa[t] + bbi * pb[t],
                                              bbr * pc[t] + bbi * pd[t]], axis=1))
            ws_parts.append(jnp.concatenate(parts, axis=0))

            a2 = jnp.where(lo1, -pi[u][CHUNK], pi[u][CHUNK])
            coef_rows += [pr[u][CHUNK], a2, -a2]

        ws_ref[gi] = jnp.concatenate(ws_parts, axis=1).astype(ws_ref.dtype)
        coef_ref[gi] = jnp.concatenate(coef_rows + [jnp.zeros((2, LANES), f32)], axis=0)

    sign = jnp.where(lax.broadcasted_iota(jnp.int32, (LANES, CHUNK_VEC), 0) < SSM_STATE, 1.0, -1.0)
    sub = lax.broadcasted_iota(jnp.int32, (SSM_CH, CHUNK_VEC), 0)
    lane = lax.broadcasted_iota(jnp.int32, (SSM_CH, CHUNK_VEC), 1)
    for gi in range(PREP_GB):
        wo_f = jnp.transpose(lag_tabs[gi, 0][SSM_CH:SSM_CH + CHUNK_VEC, :]) * sign
        wo_b = jnp.transpose(lag_tabs[gi, 1][:CHUNK_VEC, :]) * sign
        wo_ref[gi] = jnp.concatenate([wo_f, wo_b], axis=0).astype(wo_ref.dtype)

        g = group(gi)
        dblk = d_ref[:, pl.ds(pl.multiple_of((g // GROUPS_PER_LANE_BLOCK) * LANES, LANES), LANES)]
        dblk = pltpu.roll(dblk, (LANES - (g % GROUPS_PER_LANE_BLOCK) * SSM_CH) % LANES, 1)
        dblk = jnp.where(lax.broadcasted_iota(jnp.int32, (1, LANES), 1) < SSM_CH, dblk, 0.0)
        for shift in (SSM_CH, 2 * SSM_CH, 4 * SSM_CH):
            dblk = dblk + pltpu.roll(dblk, shift, 1)
        dtile = jnp.broadcast_to(jnp.concatenate([dblk, dblk], axis=1), (SSM_CH, CHUNK_VEC))
        for jp in range(CHUNK):
            fwd = kws[gi, 0] if jp == 0 else pltpu.roll(kws[gi, 0], SSM_CH * jp, 1)
            bwd = pltpu.roll(kws[gi, 1], (LAG_ROWS - SSM_CH * (CHUNK - jp)) % LAG_ROWS, 1)
            skip = jnp.where((lane // SSM_CH == jp) & (lane % SSM_CH == sub), dtile, 0.0)
            blk = fwd[:, :CHUNK_VEC] + bwd[:, :CHUNK_VEC] + skip
            m_ref[gi, jp * SSM_CH:(jp + 1) * SSM_CH, :] = blk.astype(m_ref.dtype)


def _rope_tables():
    half = HEAD_DIM // 2
    inv_freq = ROPE_THETA ** (-np.arange(half, dtype=np.float64) / half)
    ang = np.arange(SEQ, dtype=np.float64)[:, None] * inv_freq[None, :]
    cos, sin = np.cos(ang), np.sin(ang)
    return jnp.asarray(np.concatenate([cos, cos, cos, cos, -sin, sin, -sin, sin], axis=1), jnp.float32)


def _rope_block(xb, cos_t, sin_t, first_half):
    swapped = jnp.where(first_half, pltpu.roll(xb, 96, 1), pltpu.roll(xb, 32, 1))
    return xb * cos_t + swapped * sin_t


COL_Q = (0, Q_COLS)
COL_KV = (COL_Q[1], COL_Q[1] + 2 * KV_COLS)
COL_ZA = (COL_KV[1], COL_KV[1] + D_ATTN)
COL_U = (COL_ZA[1], COL_ZA[1] + D_SSM)
COL_ZS = (COL_U[1], COL_U[1] + D_SSM)


def _silu(z):
    h = 0.5 * z
    return h + h * jnp.tanh(h)


def _inproj_kernel(x_ref, w32_ref, rope_ref, ldt_ref, are_ref, aim_ref, bre_ref, bim_ref, cre_ref, cim_ref, d_ref,
                   q_ref, kv_ref, sz_ref, ut_ref, m_ref, ws_ref, wo_ref, coef_ref, uslab_ref, w_ref):
    step = pl.program_id(0)
    part = step % STEPS_PER_REGROUP
    lane = lax.broadcasted_iota(jnp.int32, (SUB_ROWS, LANES), 1)
    first_half = (lane % HEAD_DIM) < (HEAD_DIM // 2)

    @pl.when(step == 0)
    def _():
        for c0 in range(0, w_ref.shape[1], 2 * LANES):
            w_ref[:, c0:c0 + 2 * LANES] = w32_ref[:, c0:c0 + 2 * LANES].astype(w_ref.dtype)

    def proj(xb, cols):
        return jnp.dot(xb, w_ref[:, cols[0]:cols[1]], preferred_element_type=jnp.float32)

    for sub in range(TOK_BLOCK // SUB_ROWS):
        rows = slice(sub * SUB_ROWS, (sub + 1) * SUB_ROWS)
        xb = x_ref[rows, :].astype(jnp.bfloat16)
        cos_t = rope_ref[rows, :LANES]
        sin_t = rope_ref[rows, LANES:]
        q = proj(xb, COL_Q)
        kv = proj(xb, COL_KV)
        q_blocks = [_rope_block(q[:, v * LANES:(v + 1) * LANES], cos_t, sin_t, first_half)
                    for v in range(Q_COLS // LANES)]
        q_ref[rows, :] = (jnp.concatenate(q_blocks, axis=1) * Q_SCALE).astype(q_ref.dtype)
        za = proj(xb, COL_ZA)
        k_rot = _rope_block(kv[:, :LANES], cos_t, sin_t, first_half)
        kv_ref[rows, :] = jnp.concatenate([k_rot, kv[:, LANES:]], axis=1).astype(kv_ref.dtype)
        u = proj(xb, COL_U)
        sz_ref[rows, :D_ATTN] = _silu(za).astype(sz_ref.dtype)
        zs = proj(xb, COL_ZS)
        for kk in range(SUB_ROWS // CHUNK):
            base = pl.multiple_of((part * (TOK_BLOCK // CHUNK) + sub * (SUB_ROWS // CHUNK)) * SLAB_CHUNK_PITCH, 8)
            for v in range(N_LANE_BLOCKS):
                uslab_ref[v, pl.ds(base + kk * SLAB_CHUNK_PITCH, CHUNK), :] = (
                    u[kk * CHUNK:(kk + 1) * CHUNK, v * LANES:(v + 1) * LANES])
        sz_ref[rows, D_ATTN:] = _silu(zs).astype(sz_ref.dtype)

    _ssm_tables(ldt_ref, are_ref, aim_ref, bre_ref, bim_ref, cre_ref, cim_ref, d_ref, m_ref, ws_ref, wo_ref, coef_ref)

    @pl.when(part == STEPS_PER_REGROUP - 1)
    def _():
        for j in range(CHUNK):
            for v in range(N_LANE_BLOCKS):
                uj = uslab_ref[v, pl.ds(j, LANES, stride=SLAB_CHUNK_PITCH), :]
                ujt = jnp.transpose(uj.astype(ut_ref.dtype)).reshape(GROUPS_PER_LANE_BLOCK, SSM_CH, LANES)
                g0 = v * GROUPS_PER_LANE_BLOCK
                ut_ref[g0:g0 + GROUPS_PER_LANE_BLOCK, j * SSM_CH:(j + 1) * SSM_CH, :] = ujt


def _inproj(x2, w, rope, log_dt, a_re, a_im, b_re, b_im, c_re, c_im, d):
    n_steps = BATCH * SEQ // TOK_BLOCK
    blocks_per_seq = SEQ // TOK_BLOCK
    bf16 = jnp.bfloat16

    def tok(cols):
        return pl.BlockSpec((TOK_BLOCK, cols), lambda s: (s, 0))

    def whole(a):
        return pl.BlockSpec(a.shape, lambda s: (0,) * a.ndim)

    def group_in(a):
        return pl.BlockSpec((2, PREP_GB) + a.shape[2:], lambda s: (0, s, 0, 0))

    def group_out(rows, cols):
        return pl.BlockSpec((PREP_GB, rows, cols), lambda s: (s, 0, 0))

    return pl.pallas_call(
        _inproj_kernel,
        grid=(n_steps,),
        in_specs=[tok(D_MODEL), pl.BlockSpec(w.shape, lambda s: (0, 0), pipeline_mode=pl.Buffered(1)),
                  pl.BlockSpec((TOK_BLOCK, 2 * LANES), lambda s: (s % blocks_per_seq, 0)),
                  pl.BlockSpec(memory_space=pltpu.SMEM), whole(a_re), whole(a_im),
                  group_in(b_re), group_in(b_im), group_in(c_re), group_in(c_im), whole(d)],
        out_specs=[tok(Q_COLS), tok(2 * KV_COLS), tok(D_ATTN + D_SSM),
                   pl.BlockSpec((None, N_GROUPS, CHUNK_VEC, LANES), lambda s: (s // STEPS_PER_REGROUP, 0, 0, 0)),
                   group_out(CHUNK_VEC, CHUNK_VEC), group_out(CHUNK_VEC, 4 * LANES),
                   group_out(2 * STATE_VEC, CHUNK_VEC), group_out(8, LANES)],
        out_shape=[jax.ShapeDtypeStruct((BATCH * SEQ, Q_COLS), bf16),
                   jax.ShapeDtypeStruct((BATCH * SEQ, 2 * KV_COLS), bf16),
                   jax.ShapeDtypeStruct((BATCH * SEQ, D_ATTN + D_SSM), bf16),
                   jax.ShapeDtypeStruct((N_ROW_BLOCKS, N_GROUPS, CHUNK_VEC, LANES), bf16),
                   jax.ShapeDtypeStruct((N_GROUPS, CHUNK_VEC, CHUNK_VEC), bf16),
                   jax.ShapeDtypeStruct((N_GROUPS, CHUNK_VEC, 4 * LANES), bf16),
                   jax.ShapeDtypeStruct((N_GROUPS, 2 * STATE_VEC, CHUNK_VEC), bf16),
                   jax.ShapeDtypeStruct((N_GROUPS, 8, LANES), jnp.float32)],
        scratch_shapes=[pltpu.VMEM((N_LANE_BLOCKS, LANES * SLAB_CHUNK_PITCH, LANES), jnp.float32),
                        pltpu.VMEM(w.shape, bf16)],
        compiler_params=pltpu.CompilerParams(dimension_semantics=("arbitrary",),
                                             vmem_limit_bytes=VMEM_LIMIT),
        name="inproj",
    )(x2, w, rope, log_dt, a_re, a_im, b_re, b_im, c_re, c_im, d)


ATTN_SUB = 32
ATTN_STEP = ATTN_SUB * BLOCK
N_KEYS = 3 * BLOCK


def _attn_kernel(q_ref, kp_ref, kc_ref, kn_ref, sink_ref, y_ref):
    i = pl.program_id(1)
    kvw = jnp.concatenate([kp_ref[...], kc_ref[...], kn_ref[...]], axis=0).astype(jnp.float32)

    n_win = kvw.shape[0]
    lane_w = lax.broadcasted_iota(jnp.int32, (n_win, LANES), 1)
    k_lo, k_hi, vt_lo, vt_hi = [], [], [], []
    for hk in range(N_KV_HEADS):
        in_head = (lane_w >= hk * HEAD_DIM) & (lane_w < (hk + 1) * HEAD_DIM)
        k_nat = jnp.where(in_head, kvw[:, :LANES], 0.0)
        v_nat = jnp.where(in_head, kvw[:, LANES:], 0.0)
        k_oth = pltpu.roll(k_nat, HEAD_DIM, 1)
        v_oth = pltpu.roll(v_nat, HEAD_DIM, 1)
        pairs = ((k_nat, k_oth), (v_nat, v_oth)) if hk == 0 else ((k_oth, k_nat), (v_oth, v_nat))
        k_lo.append(pairs[0][0].astype(jnp.bfloat16))
        k_hi.append(pairs[0][1].astype(jnp.bfloat16))
        vt_lo.append(jnp.transpose(pairs[1][0]).astype(jnp.bfloat16))
        vt_hi.append(jnp.transpose(pairs[1][1]).astype(jnp.bfloat16))

    c = lax.broadcasted_iota(jnp.int32, (N_KEYS, BLOCK), 0)
    r = lax.broadcasted_iota(jnp.int32, (N_KEYS, BLOCK), 1)
    band = (c >= r) & (c - r <= 2 * WINDOW)
    ones_row = lax.broadcasted_iota(jnp.int32, (16, 2 * N_KEYS), 0)
    ones_col = lax.broadcasted_iota(jnp.int32, (16, 2 * N_KEYS), 1)
    ones = jnp.where(((ones_row == 0) & (ones_col < N_KEYS)) | ((ones_row == 1) & (ones_col >= N_KEYS)),
                     1.0, 0.0).astype(jnp.bfloat16)
    row_o = lax.broadcasted_iota(jnp.int32, (LANES, 2 * BLOCK), 0)
    n_blocks = SEQ // BLOCK

    def scores(t, hk):
        blk = i * ATTN_SUB + t
        lo = jnp.where(blk == 0, BLOCK, 0)
        hi = jnp.where(blk == n_blocks - 1, 2 * BLOCK, N_KEYS)
        valid1 = band & (c >= lo) & (c < hi)
        valid = jnp.concatenate([valid1, valid1], axis=1)
        win = slice(t * BLOCK, t * BLOCK + N_KEYS)
        q = q_ref[t * BLOCK:(t + 1) * BLOCK, (2 * hk) * LANES:(2 * hk + 2) * LANES]
        qq = jnp.concatenate([q[:, :LANES], q[:, LANES:]], axis=0)
        kcat = jnp.concatenate([k_lo[hk][win], k_hi[hk][win]], axis=0)
        st = lax.dot_general(kcat, qq, (((1,), (1,)), ((), ())),
                             preferred_element_type=jnp.float32)
        sps, ms, sinks = [], [], []
        for par in range(2):
            base = par * N_KEYS
            sp = jnp.concatenate(
                [jnp.where(valid[:BLOCK], st[base:base + BLOCK, :], NEG_INF),
                 st[base + BLOCK:base + 2 * BLOCK, :],
                 jnp.where(valid[2 * BLOCK:], st[base + 2 * BLOCK:base + N_KEYS, :], NEG_INF)], axis=0)
            h0 = 4 * hk + par
            h1 = h0 + 2
            sink = jnp.concatenate([jnp.full((1, BLOCK), sink_ref[h] * LOG2_E, jnp.float32)
                                    for h in (h0, h1)], axis=1)
            sps.append(sp)
            ms.append(jnp.maximum(jnp.max(sp, axis=0, keepdims=True), sink))
            sinks.append(sink)
        return sps, ms, sinks

    def weights(sps, ms, sinks):
        return [jnp.exp2((sp - m).astype(jnp.bfloat16)) for sp, m in zip(sps, ms)], ms, sinks

    def finish(t, hk, ps, ms, sinks):
        win = slice(t * BLOCK, t * BLOCK + N_KEYS)
        vt = jnp.concatenate([jnp.concatenate([vt_lo[hk][:, win], vt_hi[hk][:, win]], axis=1), ones], axis=0)
        ot = jnp.dot(vt, jnp.concatenate(ps, axis=0), preferred_element_type=jnp.float32)
        inv0 = 1.0 / (ot[LANES:LANES + 1, :] + jnp.exp2(sinks[0] - ms[0]))
        inv1 = 1.0 / (ot[LANES + 1:LANES + 2, :] + jnp.exp2(sinks[1] - ms[1]))
        o = jnp.transpose(ot[:LANES, :] * jnp.where(row_o < HEAD_DIM, inv0, inv1))
        rows = slice(t * BLOCK, (t + 1) * BLOCK)
        y_ref[rows, (2 * hk) * LANES:(2 * hk + 1) * LANES] = o[:BLOCK].astype(y_ref.dtype)
        y_ref[rows, (2 * hk + 1) * LANES:(2 * hk + 2) * LANES] = o[BLOCK:].astype(y_ref.dtype)

    work = [(t, hk) for t in range(ATTN_SUB) for hk in range(N_KV_HEADS)]
    stage_a = {0: scores(*work[0])}
    stage_b = {}
    for n in range(len(work) + 1):
        if n + 1 < len(work):
            stage_a[n + 1] = scores(*work[n + 1])
        if n < len(work):
            stage_b[n] = weights(*stage_a.pop(n))
        if n >= 1:
            finish(*work[n - 1], *stage_b.pop(n - 1))


def _attention(q, kv, sink):
    nb = SEQ // BLOCK
    return pl.pallas_call(
        _attn_kernel,
        grid=(BATCH, SEQ // ATTN_STEP),
        in_specs=[pl.BlockSpec((None, ATTN_STEP, Q_COLS), lambda b, i: (b, i, 0)),
                  pl.BlockSpec((None, BLOCK, 2 * KV_COLS),
                               lambda b, i: (b, jnp.maximum(i * ATTN_SUB - 1, 0), 0)),
                  pl.BlockSpec((None, ATTN_STEP, 2 * KV_COLS), lambda b, i: (b, i, 0)),
                  pl.BlockSpec((None, BLOCK, 2 * KV_COLS),
                               lambda b, i: (b, jnp.minimum((i + 1) * ATTN_SUB, nb - 1), 0)),
                  pl.BlockSpec(memory_space=pltpu.SMEM)],
        out_specs=pl.BlockSpec((None, ATTN_STEP, D_ATTN), lambda b, i: (b, i, 0)),
        out_shape=jax.ShapeDtypeStruct((BATCH, SEQ, D_ATTN), jnp.bfloat16),
        compiler_params=pltpu.CompilerParams(dimension_semantics=("arbitrary", "arbitrary"),
                                             vmem_limit_bytes=VMEM_LIMIT),
        name="attention",
    )(q, kv, kv, kv, sink)


SSM_GB = 4
SSM_STEPS = N_GROUPS // SSM_GB
N_CHAINS = SSM_GB * BATCH
SLOT_PITCH = 36
RECUR_SLAB_ROWS = (N_CHUNKS + 1) * SLOT_PITCH + 4


def _gelu_tanh(x):
    k = (2.0 / np.pi) ** 0.5
    half = 0.5 * x
    return half + half * jnp.tanh(x * (k + (k * 0.044715) * (x * x)))


def _ssm_kernel(ut_ref, m_ref, ws_ref, wo_ref, coef_ref, gt_ref,
                sf_ref, sb_ref, x_ref, u_even_ref, u_odd_ref):
    s = pl.program_id(0)

    def chain_rows(gi, b, second, first_slot):
        return pl.ds(first_slot * SLOT_PITCH + 2 * (gi * BATCH + b) + second, N_CHUNKS, stride=SLOT_PITCH)

    def head(gi, u_ref):
        ut = jnp.concatenate([ut_ref[lb, gi] for lb in range(N_ROW_BLOCKS)], axis=1)
        u = jnp.transpose(ut)
        u_ref[gi] = u
        st = jnp.dot(u, ws_ref[gi], preferred_element_type=jnp.float32)
        for b in range(BATCH):
            rows = slice(b * N_CHUNKS, (b + 1) * N_CHUNKS)
            sf_ref[chain_rows(gi, b, 0, 0), :] = st[rows, 0 * LANES:1 * LANES]
            sf_ref[chain_rows(gi, b, 1, 0), :] = st[rows, 1 * LANES:2 * LANES]
            sb_ref[chain_rows(gi, b, 0, 0), :] = st[rows, 2 * LANES:3 * LANES]
            sb_ref[chain_rows(gi, b, 1, 0), :] = st[rows, 3 * LANES:4 * LANES]

    def chunk_out(gi, u_ref):
        xin_f = jnp.concatenate([x_ref[chain_rows(gi, b, 0, 0), :] for b in range(BATCH)], axis=0)
        xin_b = jnp.concatenate([x_ref[chain_rows(gi, b, 1, 1), :] for b in range(BATCH)], axis=0)
        lhs = jnp.concatenate([u_ref[gi], xin_f.astype(jnp.bfloat16), xin_b.astype(jnp.bfloat16)], axis=1)
        rhs = jnp.concatenate([m_ref[gi], wo_ref[gi]], axis=0)
        return jnp.dot(lhs, rhs, preferred_element_type=jnp.float32)

    def store_out(gi, y):
        gt = jnp.transpose(_gelu_tanh(y).astype(gt_ref.dtype))
        for lb in range(N_ROW_BLOCKS):
            gt_ref[lb, gi] = gt[:, lb * LANES:(lb + 1) * LANES]

    def head_and_tail(u_head_ref, u_tail_ref):
        y = chunk_out(0, u_tail_ref) if u_tail_ref is not None else None
        for gi in range(SSM_GB):
            if u_head_ref is not None:
                head(gi, u_head_ref)
            if u_tail_ref is not None:
                y_next = chunk_out(gi + 1, u_tail_ref) if gi + 1 < SSM_GB else None
                store_out(gi, y)
                y = y_next

    u_refs = (u_even_ref, u_odd_ref)
    inner = (s > 0) & (s < SSM_STEPS)
    pl.when(s == 0)(lambda: head_and_tail(u_refs[0], None))
    pl.when(inner & (s % 2 == 0))(lambda: head_and_tail(u_refs[0], u_refs[1]))
    pl.when(inner & (s % 2 == 1))(lambda: head_and_tail(u_refs[1], u_refs[0]))
    pl.when(s == SSM_STEPS)(lambda: head_and_tail(None, u_refs[(SSM_STEPS - 1) % 2]))

    @pl.when(s < SSM_STEPS)
    def _():
        n_halves = N_CHAINS // 8

        def chains_at(slot, half, second):
            return pl.ds(slot * SLOT_PITCH + 16 * half + second, 8, stride=2)

        zero = jnp.zeros((8, LANES), jnp.float32)
        coefs = []
        for h in range(n_halves):
            rows = [jnp.concatenate([jnp.broadcast_to(coef_ref[2 * h + m, r:r + 1, :], (BATCH, LANES))
                                     for m in range(2)], axis=0) for r in range(6)]
            per_dir = []
            for a1, a2, a3 in (rows[:3], rows[3:]):
                b2 = 2.0 * a1 * a2
                per_dir.append((a1, a2, a3, a1 * a1 - a2 * a2, b2, -b2))
            coefs.append(per_dir)
            x_ref[chains_at(0, h, 0), :] = zero
            x_ref[chains_at(N_CHUNKS, h, 1), :] = zero

        def two_chunks(coef, x, xs, s_ref, half, slot0, slot1, second, out0, out1):
            a1, a2, a3, b1, b2, b3 = coef
            s0, ss0 = s_ref[chains_at(slot0, half, 0), :], s_ref[chains_at(slot0, half, 1), :]
            s1, ss1 = s_ref[chains_at(slot1, half, 0), :], s_ref[chains_at(slot1, half, 1), :]
            x_ref[chains_at(out0, half, second), :] = a1 * x + a2 * xs + s0
            c = a1 * s0 + a2 * ss0 + s1
            cs = a1 * ss0 + a3 * s0 + ss1
            nx = b1 * x + b2 * xs + c
            nxs = b1 * xs + b3 * x + cs
            x_ref[chains_at(out1, half, second), :] = nx
            return nx, nxs

        def trip(t, carry):
            k = 2 * t
            kb = N_CHUNKS - 1 - k
            out = []
            for h in range(n_halves):
                xf, xfs, xb, xbs = carry[4 * h:4 * h + 4]
                out += two_chunks(coefs[h][0], xf, xfs, sf_ref, h, k, k + 1, 0, k + 1, k + 2)
                out += two_chunks(coefs[h][1], xb, xbs, sb_ref, h, kb, kb - 1, 1, kb, kb - 1)
            return tuple(out)

        lax.fori_loop(0, N_CHUNKS // 2, trip, (zero,) * (4 * n_halves), unroll=4)


def _ssm(ut, m_intra, w_state, w_out, coef):
    def head_block(s):
        return jnp.minimum(s, SSM_STEPS - 1)

    def tail_block(s):
        return jnp.maximum(s - 1, 0)

    def per_group(shape, block):
        return pl.BlockSpec((SSM_GB,) + shape, lambda s: (block(s), 0, 0))

    def lane_blocked(block):
        return pl.BlockSpec((N_ROW_BLOCKS, SSM_GB, CHUNK_VEC, LANES), lambda s: (0, block(s), 0, 0))

    slab = pltpu.VMEM((RECUR_SLAB_ROWS, LANES), jnp.float32)
    return pl.pallas_call(
        _ssm_kernel,
        grid=(SSM_STEPS + 1,),
        in_specs=[lane_blocked(head_block), per_group((CHUNK_VEC, CHUNK_VEC), tail_block),
                  per_group((CHUNK_VEC, 4 * LANES), head_block), per_group((2 * STATE_VEC, CHUNK_VEC), tail_block),
                  per_group((8, LANES), head_block)],
        out_specs=lane_blocked(tail_block),
        out_shape=jax.ShapeDtypeStruct((N_ROW_BLOCKS, N_GROUPS, CHUNK_VEC, LANES), jnp.bfloat16),
        scratch_shapes=[slab] * 3 + [pltpu.VMEM((SSM_GB, ROWS, CHUNK_VEC), jnp.bfloat16)] * 2,
        compiler_params=pltpu.CompilerParams(dimension_semantics=("arbitrary",),
                                             vmem_limit_bytes=VMEM_LIMIT),
        name="ssm",
    )(ut, m_intra, w_state, w_out, coef)


FINAL_SUB_ROWS = 256
X_SLOTS = 3


def _unit_rms(v):
    return v * lax.rsqrt(jnp.mean(v * v, axis=-1, keepdims=True) + NORM_EPS)


def _final_kernel(x_hbm_ref, ya_ref, sz_ref, gt_ref, wglu32_ref, bglu_ref, wo32_ref, ga_ref, gs_ref,
                  lng_ref, lnb_ref, o_ref, gslab_ref, wglu_ref, wo_ref, xbuf_ref, xsem_ref):
    step = pl.program_id(0)
    part = step % STEPS_PER_REGROUP

    def x_copy(block):
        slot = block % X_SLOTS
        return pltpu.make_async_copy(x_hbm_ref.at[pl.ds(block * TOK_BLOCK, TOK_BLOCK), :],
                                     xbuf_ref.at[slot], xsem_ref.at[slot])

    @pl.when(step == 0)
    def _():
        x_copy(0).start()
        x_copy(1).start()

    @pl.when(step + 2 < pl.num_programs(0))
    def _():
        x_copy(step + 2).start()

    x_copy(step).wait()
    x_ref = xbuf_ref.at[step % X_SLOTS]

    @pl.when(step == 0)
    def _():
        wglu_ref[...] = wglu32_ref[...].astype(wglu_ref.dtype)
        for gain_ref, row0 in ((ga_ref, 0), (gs_ref, D_ATTN)):
            for v in range(gain_ref.shape[1] // LANES):
                gain = gain_ref[:, v * LANES:(v + 1) * LANES] / DEEPNORM_ALPHA
                col = jnp.transpose(jnp.broadcast_to(gain, (LANES, LANES)))
                rows = slice(row0 + v * LANES, row0 + (v + 1) * LANES)
                for c0 in range(0, D_MODEL, LANES):
                    wo_ref[rows, c0:c0 + LANES] = (wo32_ref[rows, c0:c0 + LANES] * col).astype(wo_ref.dtype)

    @pl.when(part == 0)
    def _():
        for j in range(CHUNK):
            for v in range(N_LANE_BLOCKS):
                g0 = v * GROUPS_PER_LANE_BLOCK
                gjt = gt_ref[g0:g0 + GROUPS_PER_LANE_BLOCK, j * SSM_CH:(j + 1) * SSM_CH, :]
                gj = jnp.transpose(gjt.reshape(LANES, LANES))
                gslab_ref[v, pl.ds(j, LANES, stride=SLAB_CHUNK_PITCH), :] = gj.astype(jnp.float32)

    def rows_of(sub):
        return slice(sub * FINAL_SUB_ROWS, (sub + 1) * FINAL_SUB_ROWS)

    def gate_in(sub):
        rows = rows_of(sub)
        ya = ya_ref[rows, :].astype(jnp.float32) * sz_ref[rows, :D_ATTN].astype(jnp.float32)
        ya = _unit_rms(ya).astype(jnp.bfloat16)
        base = pl.multiple_of((part * (TOK_BLOCK // CHUNK) + sub * (FINAL_SUB_ROWS // CHUNK)) * SLAB_CHUNK_PITCH, 8)
        g = jnp.concatenate(
            [jnp.concatenate([gslab_ref[v, pl.ds(base + kk * SLAB_CHUNK_PITCH, CHUNK), :]
                              for kk in range(FINAL_SUB_ROWS // CHUNK)], axis=0)
             for v in range(N_LANE_BLOCKS)], axis=1)
        return ya, g

    def glu(sub, ya, g):
        rows = rows_of(sub)
        gate = jnp.dot(g.astype(jnp.bfloat16), wglu_ref[...], preferred_element_type=jnp.float32) + bglu_ref[...]
        ys = g * jax.nn.sigmoid(gate)
        ys = ys * sz_ref[rows, D_ATTN:].astype(jnp.float32)
        return ya, _unit_rms(ys).astype(jnp.bfloat16)

    def project(sub, ya, ys):
        return (jnp.dot(ya, wo_ref[:D_ATTN, :], preferred_element_type=jnp.float32)
                + jnp.dot(ys, wo_ref[D_ATTN:, :], preferred_element_type=jnp.float32))

    def layer_norm(sub, out):
        rows = rows_of(sub)
        h = x_ref[rows, :] + out
        mu = jnp.mean(h, axis=-1, keepdims=True)
        hc = h - mu
        var = jnp.mean(hc * hc, axis=-1, keepdims=True)
        o_ref[rows, :] = hc * lax.rsqrt(var + NORM_EPS / DEEPNORM_ALPHA ** 2) * lng_ref[...] + lnb_ref[...]

    n_sub = TOK_BLOCK // FINAL_SUB_ROWS
    stage_a, stage_b = {}, {}
    for n in range(n_sub + 2):
        if n < n_sub:
            stage_a[n] = gate_in(n)
        if 1 <= n <= n_sub:
            stage_b[n - 1] = glu(n - 1, *stage_a.pop(n - 1))
        if n >= 2:
            layer_norm(n - 2, project(n - 2, *stage_b.pop(n - 2)))


def _final(x2, ya, sz, gt, wglu, bglu, wo, gain_a, gain_s, lng, lnb):
    n_steps = BATCH * SEQ // TOK_BLOCK

    def tok(cols):
        return pl.BlockSpec((TOK_BLOCK, cols), lambda s: (s, 0))

    def whole(a, **kw):
        return pl.BlockSpec(a.shape, lambda s: (0,) * a.ndim, **kw)

    return pl.pallas_call(
        _final_kernel,
        grid=(n_steps,),
        in_specs=[pl.BlockSpec(memory_space=pl.ANY), tok(D_ATTN), tok(D_ATTN + D_SSM),
                  pl.BlockSpec((None, N_GROUPS, CHUNK_VEC, LANES), lambda s: (s // STEPS_PER_REGROUP, 0, 0, 0)),
                  whole(wglu, pipeline_mode=pl.Buffered(1)), whole(bglu),
                  whole(wo, pipeline_mode=pl.Buffered(1)), whole(gain_a), whole(gain_s), whole(lng), whole(lnb)],
        out_specs=tok(D_MODEL),
        out_shape=jax.ShapeDtypeStruct((BATCH * SEQ, D_MODEL), jnp.float32),
        scratch_shapes=[pltpu.VMEM((N_LANE_BLOCKS, LANES * SLAB_CHUNK_PITCH, LANES), jnp.float32),
                        pltpu.VMEM(wglu.shape, jnp.bfloat16), pltpu.VMEM(wo.shape, jnp.bfloat16),
                        pltpu.VMEM((X_SLOTS, TOK_BLOCK, D_MODEL), jnp.float32),
                        pltpu.SemaphoreType.DMA((X_SLOTS,))],
        compiler_params=pltpu.CompilerParams(dimension_semantics=("arbitrary",),
                                             vmem_limit_bytes=VMEM_LIMIT),
        name="final",
    )(x2, ya, sz, gt, wglu, bglu, wo, gain_a, gain_s, lng, lnb)


def kernel(x, w_in, attn_sink, ssm_a_re, ssm_a_im, ssm_log_dt, ssm_b_re, ssm_b_im, ssm_c_re, ssm_c_im,
           ssm_d, w_glu, b_glu, norm_attn_g, norm_ssm_g, w_out, ln_g, ln_b):
    assert x.shape == (BATCH, SEQ, D_MODEL) and w_in.shape[0] == 1
    f32 = jnp.float32
    rope = _rope_tables()

    x2 = x.reshape(BATCH * SEQ, D_MODEL)
    b_t = [jnp.swapaxes(p, -1, -2) for p in (ssm_b_re, ssm_b_im)]
    params = [p[0].astype(f32) for p in (ssm_log_dt, ssm_a_re, ssm_a_im, *b_t, ssm_c_re, ssm_c_im)]
    q, kv, sz, ut, m_intra, w_state, w_so, coef = _inproj(x2, w_in[0], rope, *params,
                                                           ssm_d[0].astype(f32).reshape(1, D_SSM))
    ya = _attention(q.reshape(BATCH, SEQ, Q_COLS), kv.reshape(BATCH, SEQ, 2 * KV_COLS),
                    attn_sink[0].astype(f32))
    gt = _ssm(ut, m_intra, w_state, w_so, coef)

    row = lambda v: v[0].astype(f32)[None, :]
    out = _final(x2, ya.reshape(BATCH * SEQ, D_ATTN), sz, gt, w_glu[0].astype(f32), row(b_glu),
                 w_out[0].astype(f32), row(norm_attn_g), row(norm_ssm_g), row(ln_g), row(ln_b))
    return out.reshape(BATCH, SEQ, D_MODEL)
```

```python
import jax
import jax.numpy as jnp
import numpy as np
from jax import lax
from jax.experimental import pallas as pl
from jax.experimental.pallas import tpu as pltpu

D_MODEL = 1024
BATCH = 4
SEQ = 4096
D_ATTN = 512
D_SSM = 512
HEAD_DIM = 64
N_Q_HEADS = 8
N_KV_HEADS = 2
WINDOW = 128
BLOCK = 128
ROPE_THETA = 10000.0
SSM_CH = 16
N_GROUPS = 32
SSM_STATE = 64
NORM_EPS = 1e-5
NEG_INF = -1e30
DEEPNORM_ALPHA = 2.0 ** 0.25
LOG2_E = 1.4426950408889634
Q_SCALE = HEAD_DIM ** -0.5 * LOG2_E

Q_COLS = N_Q_HEADS * HEAD_DIM
KV_COLS = N_KV_HEADS * HEAD_DIM
CHUNK = 16
N_CHUNKS = SEQ // CHUNK
CHUNK_VEC = CHUNK * SSM_CH
STATE_VEC = 2 * SSM_STATE
LANES = 128
ROWS = BATCH * N_CHUNKS
N_ROW_BLOCKS = ROWS // LANES
TOK_BLOCK = 1024
REGROUP_TOK = LANES * CHUNK
STEPS_PER_REGROUP = REGROUP_TOK // TOK_BLOCK
SUB_ROWS = 512
N_LANE_BLOCKS = D_SSM // LANES
SLAB_CHUNK_PITCH = 20
GROUPS_PER_LANE_BLOCK = LANES // SSM_CH
VMEM_LIMIT = 56 * 1024 * 1024
HIGHEST = lax.Precision.HIGHEST


LAG_ROWS = 512
PREP_GB = N_GROUPS // (BATCH * SEQ // TOK_BLOCK)


def _ssm_tables(ldt_ref, are_ref, aim_ref, bre_ref, bim_ref, cre_ref, cim_ref, d_ref,
                m_ref, ws_ref, wo_ref, coef_ref):
    f32 = jnp.float32
    lo16 = lax.broadcasted_iota(jnp.int32, (SSM_CH, LANES), 1) < SSM_STATE
    lo1 = lax.broadcasted_iota(jnp.int32, (1, LANES), 1) < SSM_STATE
    units = [(gi, d) for gi in range(PREP_GB) for d in range(2)]

    def group(gi):
        return pl.program_id(0) * PREP_GB + gi

    def dup(v):
        return jnp.concatenate([v, v], axis=1)

    lam, zoh = {}, {}
    for gi, d in units:
        ar = dup(are_ref[d, pl.ds(group(gi), 1), :])
        ai = dup(aim_ref[d, pl.ds(group(gi), 1), :])
        dt = jnp.exp(jnp.full((1, LANES), ldt_ref[d, group(gi)], f32))
        zr = dt * ar
        zi = dt * ai
        mag = jnp.exp(zr)
        lr = mag * jnp.cos(zi)
        li = mag * jnp.sin(zi)
        den = ar * ar + ai * ai
        lam[gi, d] = (lr, li)
        zoh[gi, d] = (((lr - 1.0) * ar + li * ai) / den, (li * ar - (lr - 1.0) * ai) / den)

    pr = {u: [jnp.ones((1, LANES), f32)] for u in units}
    pi = {u: [jnp.zeros((1, LANES), f32)] for u in units}
    for _ in range(CHUNK):
        for u in units:
            lr, li = lam[u]
            pr[u].append(pr[u][-1] * lr - pi[u][-1] * li)
            pi[u].append(pr[u][-2] * li + pi[u][-1] * lr)

    lag_tabs, kws = {}, {}
    for gi in range(PREP_GB):
        ws_parts, coef_rows = [], []
        for d in range(2):
            u = (gi, d)
            fr, fi = zoh[u]
            btr = dup(bre_ref[d, gi])
            bti = dup(bim_ref[d, gi])
            bbr = fr * btr - fi * bti
            bbi = fr * bti + fi * btr
            cr = dup(cre_ref[d, gi])
            ci = dup(cim_ref[d, gi])
            pa = [jnp.where(lo1, r, i) for r, i in zip(pr[u], pi[u])]
            pb = [jnp.where(lo1, -i, r) for r, i in zip(pr[u], pi[u])]
            pc = [jnp.where(lo1, i, r) for r, i in zip(pr[u], pi[u])]
            pd = [jnp.where(lo1, r, -i) for r, i in zip(pr[u], pi[u])]

            order = range(CHUNK + 1) if d == 0 else range(CHUNK, -1, -1)
            blocks = [cr * pa[t] + ci * pb[t] for t in order]
            blocks.append(jnp.zeros((LAG_ROWS - (CHUNK + 1) * SSM_CH, LANES), f32))
            lag_tabs[u] = jnp.concatenate(blocks, axis=0)
            lhs = jnp.where(lo16, bbr, -bbi)
            kws[u] = lax.dot_general(lhs, lag_tabs[u], (((1,), (1,)), ((), ())),
                                     precision=HIGHEST, preferred_element_type=f32)

            parts = []
            for j in range(CHUNK):
                t = CHUNK - 1 - j if d == 0 else j
                parts.append(jnp.concatenate([bbr * pa[t] + bbi * pb[t],
                                              bbr * pc[t] + bbi * pd[t]], axis=1))
            ws_parts.append(jnp.concatenate(parts, axis=0))

            a2 = jnp.where(lo1, -pi[u][CHUNK], pi[u][CHUNK])
            coef_rows += [pr[u][CHUNK], a2, -a2]

        ws_ref[gi] = jnp.concatenate(ws_parts, axis=1).astype(ws_ref.dtype)
        coef_ref[gi] = jnp.concatenate(coef_rows + [jnp.zeros((2, LANES), f32)], axis=0)

    sign = jnp.where(lax.broadcasted_iota(jnp.int32, (LANES, CHUNK_VEC), 0) < SSM_STATE, 1.0, -1.0)
    sub = lax.broadcasted_iota(jnp.int32, (SSM_CH, CHUNK_VEC), 0)
    lane = lax.broadcasted_iota(jnp.int32, (SSM_CH, CHUNK_VEC), 1)
    for gi in range(PREP_GB):
        wo_f = jnp.transpose(lag_tabs[gi, 0][SSM_CH:SSM_CH + CHUNK_VEC, :]) * sign
        wo_b = jnp.transpose(lag_tabs[gi, 1][:CHUNK_VEC, :]) * sign
        wo_ref[gi] = jnp.concatenate([wo_f, wo_b], axis=0).astype(wo_ref.dtype)

        g = group(gi)
        dblk = d_ref[:, pl.ds(pl.multiple_of((g // GROUPS_PER_LANE_BLOCK) * LANES, LANES), LANES)]
        dblk = pltpu.roll(dblk, (LANES - (g % GROUPS_PER_LANE_BLOCK) * SSM_CH) % LANES, 1)
        dblk = jnp.where(lax.broadcasted_iota(jnp.int32, (1, LANES), 1) < SSM_CH, dblk, 0.0)
        for shift in (SSM_CH, 2 * SSM_CH, 4 * SSM_CH):
            dblk = dblk + pltpu.roll(dblk, shift, 1)
        dtile = jnp.broadcast_to(jnp.concatenate([dblk, dblk], axis=1), (SSM_CH, CHUNK_VEC))
        for jp in range(CHUNK):
            fwd = kws[gi, 0] if jp == 0 else pltpu.roll(kws[gi, 0], SSM_CH * jp, 1)
            bwd = pltpu.roll(kws[gi, 1], (LAG_ROWS - SSM_CH * (CHUNK - jp)) % LAG_ROWS, 1)
            skip = jnp.where((lane // SSM_CH == jp) & (lane % SSM_CH == sub), dtile, 0.0)
            blk = fwd[:, :CHUNK_VEC] + bwd[:, :CHUNK_VEC] + skip
            m_ref[gi, jp * SSM_CH:(jp + 1) * SSM_CH, :] = blk.astype(m_ref.dtype)


def _rope_tables():
    half = HEAD_DIM // 2
    inv_freq = ROPE_THETA ** (-np.arange(half, dtype=np.float64) / half)
    ang = np.arange(SEQ, dtype=np.float64)[:, None] * inv_freq[None, :]
    cos, sin = np.cos(ang), np.sin(ang)
    return jnp.asarray(np.concatenate([cos, cos, cos, cos, -sin, sin, -sin, sin], axis=1), jnp.float32)


def _rope_block(xb, cos_t, sin_t, first_half):
    swapped = jnp.where(first_half, pltpu.roll(xb, 96, 1), pltpu.roll(xb, 32, 1))
    return xb * cos_t + swapped * sin_t


COL_Q = (0, Q_COLS)
COL_KV = (COL_Q[1], COL_Q[1] + 2 * KV_COLS)
COL_ZA = (COL_KV[1], COL_KV[1] + D_ATTN)
COL_U = (COL_ZA[1], COL_ZA[1] + D_SSM)
COL_ZS = (COL_U[1], COL_U[1] + D_SSM)


def _silu(z):
    h = 0.5 * z
    return h + h * jnp.tanh(h)


def _inproj_kernel(x_ref, w32_ref, rope_ref, ldt_ref, are_ref, aim_ref, bre_ref, bim_ref, cre_ref, cim_ref, d_ref,
                   q_ref, kv_ref, sz_ref, ut_ref, m_ref, ws_ref, wo_ref, coef_ref, uslab_ref, w_ref):
    step = pl.program_id(0)
    part = step % STEPS_PER_REGROUP
    lane = lax.broadcasted_iota(jnp.int32, (SUB_ROWS, LANES), 1)
    first_half = (lane % HEAD_DIM) < (HEAD_DIM // 2)

    @pl.when(step == 0)
    def _():
        for c0 in range(0, w_ref.shape[1], 2 * LANES):
            w_ref[:, c0:c0 + 2 * LANES] = w32_ref[:, c0:c0 + 2 * LANES].astype(w_ref.dtype)

    def proj(xb, cols):
        return jnp.dot(xb, w_ref[:, cols[0]:cols[1]], preferred_element_type=jnp.float32)

    for sub in range(TOK_BLOCK // SUB_ROWS):
        rows = slice(sub * SUB_ROWS, (sub + 1) * SUB_ROWS)
        xb = x_ref[rows, :].astype(jnp.bfloat16)
        cos_t = rope_ref[rows, :LANES]
        sin_t = rope_ref[rows, LANES:]
        q = proj(xb, COL_Q)
        kv = proj(xb, COL_KV)
        q_blocks = [_rope_block(q[:, v * LANES:(v + 1) * LANES], cos_t, sin_t, first_half)
                    for v in range(Q_COLS // LANES)]
        q_ref[rows, :] = (jnp.concatenate(q_blocks, axis=1) * Q_SCALE).astype(q_ref.dtype)
        za = proj(xb, COL_ZA)
        k_rot = _rope_block(kv[:, :LANES], cos_t, sin_t, first_half)
        kv_ref[rows, :] = jnp.concatenate([k_rot, kv[:, LANES:]], axis=1).astype(kv_ref.dtype)
        u = proj(xb, COL_U)
        sz_ref[rows, :D_ATTN] = _silu(za).astype(sz_ref.dtype)
        zs = proj(xb, COL_ZS)
        for kk in range(SUB_ROWS // CHUNK):
            base = pl.multiple_of((part * (TOK_BLOCK // CHUNK) + sub * (SUB_ROWS // CHUNK)) * SLAB_CHUNK_PITCH, 8)
            for v in range(N_LANE_BLOCKS):
                uslab_ref[v, pl.ds(base + kk * SLAB_CHUNK_PITCH, CHUNK), :] = (
                    u[kk * CHUNK:(kk + 1) * CHUNK, v * LANES:(v + 1) * LANES])
        sz_ref[rows, D_ATTN:] = _silu(zs).astype(sz_ref.dtype)

    _ssm_tables(ldt_ref, are_ref, aim_ref, bre_ref, bim_ref, cre_ref, cim_ref, d_ref, m_ref, ws_ref, wo_ref, coef_ref)

    @pl.when(part == STEPS_PER_REGROUP - 1)
    def _():
        for j in range(CHUNK):
            for v in range(N_LANE_BLOCKS):
                uj = uslab_ref[v, pl.ds(j, LANES, stride=SLAB_CHUNK_PITCH), :]
                ujt = jnp.transpose(uj.astype(ut_ref.dtype)).reshape(GROUPS_PER_LANE_BLOCK, SSM_CH, LANES)
                g0 = v * GROUPS_PER_LANE_BLOCK
                ut_ref[g0:g0 + GROUPS_PER_LANE_BLOCK, j * SSM_CH:(j + 1) * SSM_CH, :] = ujt


def _inproj(x2, w, rope, log_dt, a_re, a_im, b_re, b_im, c_re, c_im, d):
    n_steps = BATCH * SEQ // TOK_BLOCK
    blocks_per_seq = SEQ // TOK_BLOCK
    bf16 = jnp.bfloat16

    def tok(cols):
        return pl.BlockSpec((TOK_BLOCK, cols), lambda s: (s, 0))

    def whole(a):
        return pl.BlockSpec(a.shape, lambda s: (0,) * a.ndim)

    def group_in(a):
        return pl.BlockSpec((2, PREP_GB) + a.shape[2:], lambda s: (0, s, 0, 0))

    def group_out(rows, cols):
        return pl.BlockSpec((PREP_GB, rows, cols), lambda s: (s, 0, 0))

    return pl.pallas_call(
        _inproj_kernel,
        grid=(n_steps,),
        in_specs=[tok(D_MODEL), pl.BlockSpec(w.shape, lambda s: (0, 0), pipeline_mode=pl.Buffered(1)),
                  pl.BlockSpec((TOK_BLOCK, 2 * LANES), lambda s: (s % blocks_per_seq, 0)),
                  pl.BlockSpec(memory_space=pltpu.SMEM), whole(a_re), whole(a_im),
                  group_in(b_re), group_in(b_im), group_in(c_re), group_in(c_im), whole(d)],
        out_specs=[tok(Q_COLS), tok(2 * KV_COLS), tok(D_ATTN + D_SSM),
                   pl.BlockSpec((None, N_GROUPS, CHUNK_VEC, LANES), lambda s: (s // STEPS_PER_REGROUP, 0, 0, 0)),
                   group_out(CHUNK_VEC, CHUNK_VEC), group_out(CHUNK_VEC, 4 * LANES),
                   group_out(2 * STATE_VEC, CHUNK_VEC), group_out(8, LANES)],
        out_shape=[jax.ShapeDtypeStruct((BATCH * SEQ, Q_COLS), bf16),
                   jax.ShapeDtypeStruct((BATCH * SEQ, 2 * KV_COLS), bf16),
                   jax.ShapeDtypeStruct((BATCH * SEQ, D_ATTN + D_SSM), bf16),
                   jax.ShapeDtypeStruct((N_ROW_BLOCKS, N_GROUPS, CHUNK_VEC, LANES), bf16),
                   jax.ShapeDtypeStruct((N_GROUPS, CHUNK_VEC, CHUNK_VEC), bf16),
                   jax.ShapeDtypeStruct((N_GROUPS, CHUNK_VEC, 4 * LANES), bf16),
                   jax.ShapeDtypeStruct((N_GROUPS, 2 * STATE_VEC, CHUNK_VEC), bf16),
                   jax.ShapeDtypeStruct((N_GROUPS, 8, LANES), jnp.float32)],
        scratch_shapes=[pltpu.VMEM((N_LANE_BLOCKS, LANES * SLAB_CHUNK_PITCH, LANES), jnp.float32),
                        pltpu.VMEM(w.shape, bf16)],
        compiler_params=pltpu.CompilerParams(dimension_semantics=("arbitrary",),
                                             vmem_limit_bytes=VMEM_LIMIT),
        name="inproj",
    )(x2, w, rope, log_dt, a_re, a_im, b_re, b_im, c_re, c_im, d)


ATTN_SUB = 32
ATTN_STEP = ATTN_SUB * BLOCK
N_KEYS = 3 * BLOCK


def _attn_kernel(q_ref, kp_ref, kc_ref, kn_ref, sink_ref, y_ref):
    i = pl.program_id(1)
    kvw = jnp.concatenate([kp_ref[...], kc_ref[...], kn_ref[...]], axis=0).astype(jnp.float32)

    n_win = kvw.shape[0]
    lane_w = lax.broadcasted_iota(jnp.int32, (n_win, LANES), 1)
    k_lo, k_hi, vt_lo, vt_hi = [], [], [], []
    for hk in range(N_KV_HEADS):
        in_head = (lane_w >= hk * HEAD_DIM) & (lane_w < (hk + 1) * HEAD_DIM)
        k_nat = jnp.where(in_head, kvw[:, :LANES], 0.0)
        v_nat = jnp.where(in_head, kvw[:, LANES:], 0.0)
        k_oth = pltpu.roll(k_nat, HEAD_DIM, 1)
        v_oth = pltpu.roll(v_nat, HEAD_DIM, 1)
        pairs = ((k_nat, k_oth), (v_nat, v_oth)) if hk == 0 else ((k_oth, k_nat), (v_oth, v_nat))
        k_lo.append(pairs[0][0].astype(jnp.bfloat16))
        k_hi.append(pairs[0][1].astype(jnp.bfloat16))
        vt_lo.append(jnp.transpose(pairs[1][0]).astype(jnp.bfloat16))
        vt_hi.append(jnp.transpose(pairs[1][1]).astype(jnp.bfloat16))

    c = lax.broadcasted_iota(jnp.int32, (N_KEYS, BLOCK), 0)
    r = lax.broadcasted_iota(jnp.int32, (N_KEYS, BLOCK), 1)
    band = (c >= r) & (c - r <= 2 * WINDOW)
    ones_row = lax.broadcasted_iota(jnp.int32, (16, 2 * N_KEYS), 0)
    ones_col = lax.broadcasted_iota(jnp.int32, (16, 2 * N_KEYS), 1)
    ones = jnp.where(((ones_row == 0) & (ones_col < N_KEYS)) | ((ones_row == 1) & (ones_col >= N_KEYS)),
                     1.0, 0.0).astype(jnp.bfloat16)
    row_o = lax.broadcasted_iota(jnp.int32, (LANES, 2 * BLOCK), 0)
    n_blocks = SEQ // BLOCK

    def scores(t, hk):
        blk = i * ATTN_SUB + t
        lo = jnp.where(blk == 0, BLOCK, 0)
        hi = jnp.where(blk == n_blocks - 1, 2 * BLOCK, N_KEYS)
        valid1 = band & (c >= lo) & (c < hi)
        valid = jnp.concatenate([valid1, valid1], axis=1)
        win = slice(t * BLOCK, t * BLOCK + N_KEYS)
        q = q_ref[t * BLOCK:(t + 1) * BLOCK, (2 * hk) * LANES:(2 * hk + 2) * LANES]
        qq = jnp.concatenate([q[:, :LANES], q[:, LANES:]], axis=0)
        kcat = jnp.concatenate([k_lo[hk][win], k_hi[hk][win]], axis=0)
        st = lax.dot_general(kcat, qq, (((1,), (1,)), ((), ())),
                             preferred_element_type=jnp.float32)
        sps, ms, sinks = [], [], []
        for par in range(2):
            base = par * N_KEYS
            sp = jnp.concatenate(
                [jnp.where(valid[:BLOCK], st[base:base + BLOCK, :], NEG_INF),
                 st[base + BLOCK:base + 2 * BLOCK, :],
                 jnp.where(valid[2 * BLOCK:], st[base + 2 * BLOCK:base + N_KEYS, :], NEG_INF)], axis=0)
            h0 = 4 * hk + par
            h1 = h0 + 2
            sink = jnp.concatenate([jnp.full((1, BLOCK), sink_ref[h] * LOG2_E, jnp.float32)
                                    for h in (h0, h1)], axis=1)
            sps.append(sp)
            ms.append(jnp.maximum(jnp.max(sp, axis=0, keepdims=True), sink))
            sinks.append(sink)
        return sps, ms, sinks

    def weights(sps, ms, sinks):
        return [jnp.exp2((sp - m).astype(jnp.bfloat16)) for sp, m in zip(sps, ms)], ms, sinks

    def finish(t, hk, ps, ms, sinks):
        win = slice(t * BLOCK, t * BLOCK + N_KEYS)
        vt = jnp.concatenate([jnp.concatenate([vt_lo[hk][:, win], vt_hi[hk][:, win]], axis=1), ones], axis=0)
        ot = jnp.dot(vt, jnp.concatenate(ps, axis=0), preferred_element_type=jnp.float32)
        inv0 = 1.0 / (ot[LANES:LANES + 1, :] + jnp.exp2(sinks[0] - ms[0]))
        inv1 = 1.0 / (ot[LANES + 1:LANES + 2, :] + jnp.exp2(sinks[1] - ms[1]))
        o = jnp.transpose(ot[:LANES, :] * jnp.where(row_o < HEAD_DIM, inv0, inv1))
        rows = slice(t * BLOCK, (t + 1) * BLOCK)
        y_ref[rows, (2 * hk) * LANES:(2 * hk + 1) * LANES] = o[:BLOCK].astype(y_ref.dtype)
        y_ref[rows, (2 * hk + 1) * LANES:(2 * hk + 2) * LANES] = o[BLOCK:].astype(y_ref.dtype)

    work = [(t, hk) for t in range(ATTN_SUB) for hk in range(N_KV_HEADS)]
    stage_a = {0: scores(*work[0])}
    stage_b = {}
    for n in range(len(work) + 1):
        if n + 1 < len(work):
            stage_a[n + 1] = scores(*work[n + 1])
        if n < len(work):
            stage_b[n] = weights(*stage_a.pop(n))
        if n >= 1:
            finish(*work[n - 1], *stage_b.pop(n - 1))


def _attention(q, kv, sink):
    nb = SEQ // BLOCK
    return pl.pallas_call(
        _attn_kernel,
        grid=(BATCH, SEQ // ATTN_STEP),
        in_specs=[pl.BlockSpec((None, ATTN_STEP, Q_COLS), lambda b, i: (b, i, 0)),
                  pl.BlockSpec((None, BLOCK, 2 * KV_COLS),
                               lambda b, i: (b, jnp.maximum(i * ATTN_SUB - 1, 0), 0)),
                  pl.BlockSpec((None, ATTN_STEP, 2 * KV_COLS), lambda b, i: (b, i, 0)),
                  pl.BlockSpec((None, BLOCK, 2 * KV_COLS),
                               lambda b, i: (b, jnp.minimum((i + 1) * ATTN_SUB, nb - 1), 0)),
                  pl.BlockSpec(memory_space=pltpu.SMEM)],
        out_specs=pl.BlockSpec((None, ATTN_STEP, D_ATTN), lambda b, i: (b, i, 0)),
        out_shape=jax.ShapeDtypeStruct((BATCH, SEQ, D_ATTN), jnp.bfloat16),
        compiler_params=pltpu.CompilerParams(dimension_semantics=("arbitrary", "arbitrary"),
                                             vmem_limit_bytes=VMEM_LIMIT),
        name="attention",
    )(q, kv, kv, kv, sink)


SSM_GB = 4
SSM_STEPS = N_GROUPS // SSM_GB
N_CHAINS = SSM_GB * BATCH
SLOT_PITCH = 36
RECUR_SLAB_ROWS = (N_CHUNKS + 1) * SLOT_PITCH + 4


def _gelu_tanh(x):
    k = (2.0 / np.pi) ** 0.5
    half = 0.5 * x
    return half + half * jnp.tanh(x * (k + (k * 0.044715) * (x * x)))


def _ssm_kernel(ut_ref, m_ref, ws_ref, wo_ref, coef_ref, gt_ref,
                sf_ref, sb_ref, x_ref, u_even_ref, u_odd_ref):
    s = pl.program_id(0)

    def chain_rows(gi, b, second, first_slot):
        return pl.ds(first_slot * SLOT_PITCH + 2 * (gi * BATCH + b) + second, N_CHUNKS, stride=SLOT_PITCH)

    def head(gi, u_ref):
        ut = jnp.concatenate([ut_ref[lb, gi] for lb in range(N_ROW_BLOCKS)], axis=1)
        u = jnp.transpose(ut)
        u_ref[gi] = u
        st = jnp.dot(u, ws_ref[gi], preferred_element_type=jnp.float32)
        for b in range(BATCH):
            rows = slice(b * N_CHUNKS, (b + 1) * N_CHUNKS)
            sf_ref[chain_rows(gi, b, 0, 0), :] = st[rows, 0 * LANES:1 * LANES]
            sf_ref[chain_rows(gi, b, 1, 0), :] = st[rows, 1 * LANES:2 * LANES]
            sb_ref[chain_rows(gi, b, 0, 0), :] = st[rows, 2 * LANES:3 * LANES]
            sb_ref[chain_rows(gi, b, 1, 0), :] = st[rows, 3 * LANES:4 * LANES]

    def chunk_out(gi, u_ref):
        xin_f = jnp.concatenate([x_ref[chain_rows(gi, b, 0, 0), :] for b in range(BATCH)], axis=0)
        xin_b = jnp.concatenate([x_ref[chain_rows(gi, b, 1, 1), :] for b in range(BATCH)], axis=0)
        lhs = jnp.concatenate([u_ref[gi], xin_f.astype(jnp.bfloat16), xin_b.astype(jnp.bfloat16)], axis=1)
        rhs = jnp.concatenate([m_ref[gi], wo_ref[gi]], axis=0)
        return jnp.dot(lhs, rhs, preferred_element_type=jnp.float32)

    def store_out(gi, y):
        gt = jnp.transpose(_gelu_tanh(y).astype(gt_ref.dtype))
        for lb in range(N_ROW_BLOCKS):
            gt_ref[lb, gi] = gt[:, lb * LANES:(lb + 1) * LANES]

    def head_and_tail(u_head_ref, u_tail_ref):
        y = chunk_out(0, u_tail_ref) if u_tail_ref is not None else None
        for gi in range(SSM_GB):
            if u_head_ref is not None:
                head(gi, u_head_ref)
            if u_tail_ref is not None:
                y_next = chunk_out(gi + 1, u_tail_ref) if gi + 1 < SSM_GB else None
                store_out(gi, y)
                y = y_next

    u_refs = (u_even_ref, u_odd_ref)
    inner = (s > 0) & (s < SSM_STEPS)
    pl.when(s == 0)(lambda: head_and_tail(u_refs[0], None))
    pl.when(inner & (s % 2 == 0))(lambda: head_and_tail(u_refs[0], u_refs[1]))
    pl.when(inner & (s % 2 == 1))(lambda: head_and_tail(u_refs[1], u_refs[0]))
    pl.when(s == SSM_STEPS)(lambda: head_and_tail(None, u_refs[(SSM_STEPS - 1) % 2]))

    @pl.when(s < SSM_STEPS)
    def _():
        n_halves = N_CHAINS // 8

        def chains_at(slot, half, second):
            return pl.ds(slot * SLOT_PITCH + 16 * half + second, 8, stride=2)

        zero = jnp.zeros((8, LANES), jnp.float32)
        coefs = []
        for h in range(n_halves):
            rows = [jnp.concatenate([jnp.broadcast_to(coef_ref[2 * h + m, r:r + 1, :], (BATCH, LANES))
                                     for m in range(2)], axis=0) for r in range(6)]
            per_dir = []
            for a1, a2, a3 in (rows[:3], rows[3:]):
                b2 = 2.0 * a1 * a2
                per_dir.append((a1, a2, a3, a1 * a1 - a2 * a2, b2, -b2))
            coefs.append(per_dir)
            x_ref[chains_at(0, h, 0), :] = zero
            x_ref[chains_at(N_CHUNKS, h, 1), :] = zero

        def two_chunks(coef, x, xs, s_ref, half, slot0, slot1, second, out0, out1):
            a1, a2, a3, b1, b2, b3 = coef
            s0, ss0 = s_ref[chains_at(slot0, half, 0), :], s_ref[chains_at(slot0, half, 1), :]
            s1, ss1 = s_ref[chains_at(slot1, half, 0), :], s_ref[chains_at(slot1, half, 1), :]
            x_ref[chains_at(out0, half, second), :] = a1 * x + a2 * xs + s0
            c = a1 * s0 + a2 * ss0 + s1
            cs = a1 * ss0 + a3 * s0 + ss1
            nx = b1 * x + b2 * xs + c
            nxs = b1 * xs + b3 * x + cs
            x_ref[chains_at(out1, half, second), :] = nx
            return nx, nxs

        def trip(t, carry):
            k = 2 * t
            kb = N_CHUNKS - 1 - k
            out = []
            for h in range(n_halves):
                xf, xfs, xb, xbs = carry[4 * h:4 * h + 4]
                out += two_chunks(coefs[h][0], xf, xfs, sf_ref, h, k, k + 1, 0, k + 1, k + 2)
                out += two_chunks(coefs[h][1], xb, xbs, sb_ref, h, kb, kb - 1, 1, kb, kb - 1)
            return tuple(out)

        lax.fori_loop(0, N_CHUNKS // 2, trip, (zero,) * (4 * n_halves), unroll=4)


def _ssm(ut, m_intra, w_state, w_out, coef):
    def head_block(s):
        return jnp.minimum(s, SSM_STEPS - 1)

    def tail_block(s):
        return jnp.maximum(s - 1, 0)

    def per_group(shape, block):
        return pl.BlockSpec((SSM_GB,) + shape, lambda s: (block(s), 0, 0))

    def lane_blocked(block):
        return pl.BlockSpec((N_ROW_BLOCKS, SSM_GB, CHUNK_VEC, LANES), lambda s: (0, block(s), 0, 0))

    slab = pltpu.VMEM((RECUR_SLAB_ROWS, LANES), jnp.float32)
    return pl.pallas_call(
        _ssm_kernel,
        grid=(SSM_STEPS + 1,),
        in_specs=[lane_blocked(head_block), per_group((CHUNK_VEC, CHUNK_VEC), tail_block),
                  per_group((CHUNK_VEC, 4 * LANES), head_block), per_group((2 * STATE_VEC, CHUNK_VEC), tail_block),
                  per_group((8, LANES), head_block)],
        out_specs=lane_blocked(tail_block),
        out_shape=jax.ShapeDtypeStruct((N_ROW_BLOCKS, N_GROUPS, CHUNK_VEC, LANES), jnp.bfloat16),
        scratch_shapes=[slab] * 3 + [pltpu.VMEM((SSM_GB, ROWS, CHUNK_VEC), jnp.bfloat16)] * 2,
        compiler_params=pltpu.CompilerParams(dimension_semantics=("arbitrary",),
                                             vmem_limit_bytes=VMEM_LIMIT),
        name="ssm",
    )(ut, m_intra, w_state, w_out, coef)


FINAL_SUB_ROWS = 256
X_SLOTS = 3


def _unit_rms(v):
    return v * lax.rsqrt(jnp.mean(v * v, axis=-1, keepdims=True) + NORM_EPS)


def _final_kernel(x_hbm_ref, ya_ref, sz_ref, gt_ref, wglu32_ref, bglu_ref, wo32_ref, ga_ref, gs_ref,
                  lng_ref, lnb_ref, o_hbm_ref, gslab_ref, wglu_ref, wo_ref, xbuf_ref, xsem_ref, obuf_ref, osem_ref):
    step = pl.program_id(0)
    part = step % STEPS_PER_REGROUP

    def x_copy(block):
        slot = block % X_SLOTS
        return pltpu.make_async_copy(x_hbm_ref.at[pl.ds(block * TOK_BLOCK, TOK_BLOCK), :],
                                     xbuf_ref.at[slot], xsem_ref.at[slot])

    @pl.when(step == 0)
    def _():
        x_copy(0).start()
        x_copy(1).start()

    @pl.when(step + 2 < pl.num_programs(0))
    def _():
        x_copy(step + 2).start()

    x_copy(step).wait()
    x_ref = xbuf_ref.at[step % X_SLOTS]

    n_sub = TOK_BLOCK // FINAL_SUB_ROWS
    last_step = pl.num_programs(0) - 1

    def out_copy(block, sub):
        slot = block % 2
        rows = pl.ds(sub * FINAL_SUB_ROWS, FINAL_SUB_ROWS)
        return pltpu.make_async_copy(obuf_ref.at[slot, rows, :],
                                     o_hbm_ref.at[pl.ds(block * TOK_BLOCK + sub * FINAL_SUB_ROWS, FINAL_SUB_ROWS), :],
                                     osem_ref.at[slot, sub])

    @pl.when(step >= 2)
    def _():
        for sub in range(n_sub):
            out_copy(step - 2, sub).wait()

    o_ref = obuf_ref.at[step % 2]

    @pl.when(step == 0)
    def _():
        wglu_ref[...] = wglu32_ref[...].astype(wglu_ref.dtype)
        for gain_ref, row0 in ((ga_ref, 0), (gs_ref, D_ATTN)):
            for v in range(gain_ref.shape[1] // LANES):
                gain = gain_ref[:, v * LANES:(v + 1) * LANES] / DEEPNORM_ALPHA
                col = jnp.transpose(jnp.broadcast_to(gain, (LANES, LANES)))
                rows = slice(row0 + v * LANES, row0 + (v + 1) * LANES)
                for c0 in range(0, D_MODEL, LANES):
                    wo_ref[rows, c0:c0 + LANES] = (wo32_ref[rows, c0:c0 + LANES] * col).astype(wo_ref.dtype)

    @pl.when(part == 0)
    def _():
        for j in range(CHUNK):
            for v in range(N_LANE_BLOCKS):
                g0 = v * GROUPS_PER_LANE_BLOCK
                gjt = gt_ref[g0:g0 + GROUPS_PER_LANE_BLOCK, j * SSM_CH:(j + 1) * SSM_CH, :]
                gj = jnp.transpose(gjt.reshape(LANES, LANES))
                gslab_ref[v, pl.ds(j, LANES, stride=SLAB_CHUNK_PITCH), :] = gj.astype(jnp.float32)

    def rows_of(sub):
        return slice(sub * FINAL_SUB_ROWS, (sub + 1) * FINAL_SUB_ROWS)

    def gate_in(sub):
        rows = rows_of(sub)
        ya = ya_ref[rows, :].astype(jnp.float32) * sz_ref[rows, :D_ATTN].astype(jnp.float32)
        ya = _unit_rms(ya).astype(jnp.bfloat16)
        base = pl.multiple_of((part * (TOK_BLOCK // CHUNK) + sub * (FINAL_SUB_ROWS // CHUNK)) * SLAB_CHUNK_PITCH, 8)
        g = jnp.concatenate(
            [jnp.concatenate([gslab_ref[v, pl.ds(base + kk * SLAB_CHUNK_PITCH, CHUNK), :]
                              for kk in range(FINAL_SUB_ROWS // CHUNK)], axis=0)
             for v in range(N_LANE_BLOCKS)], axis=1)
        return ya, g

    def glu(sub, ya, g):
        rows = rows_of(sub)
        gate = jnp.dot(g.astype(jnp.bfloat16), wglu_ref[...], preferred_element_type=jnp.float32) + bglu_ref[...]
        ys = g * jax.nn.sigmoid(gate)
        ys = ys * sz_ref[rows, D_ATTN:].astype(jnp.float32)
        return ya, _unit_rms(ys).astype(jnp.bfloat16)

    def project(sub, ya, ys):
        return (jnp.dot(ya, wo_ref[:D_ATTN, :], preferred_element_type=jnp.float32)
                + jnp.dot(ys, wo_ref[D_ATTN:, :], preferred_element_type=jnp.float32))

    def layer_norm(sub, out):
        rows = rows_of(sub)
        h = x_ref[rows, :] + out
        mu = jnp.mean(h, axis=-1, keepdims=True)
        hc = h - mu
        var = jnp.mean(hc * hc, axis=-1, keepdims=True)
        o_ref[rows, :] = hc * lax.rsqrt(var + NORM_EPS / DEEPNORM_ALPHA ** 2) * lng_ref[...] + lnb_ref[...]
        out_copy(step, sub).start()

    stage_a, stage_b = {}, {}
    for n in range(n_sub + 2):
        if n < n_sub:
            stage_a[n] = gate_in(n)
        if 1 <= n <= n_sub:
            stage_b[n - 1] = glu(n - 1, *stage_a.pop(n - 1))
        if n >= 2:
            layer_norm(n - 2, project(n - 2, *stage_b.pop(n - 2)))

    @pl.when(step == last_step)
    def _():
        for block in (step - 1, step):
            for sub in range(n_sub):
                out_copy(block, sub).wait()


def _final(x2, ya, sz, gt, wglu, bglu, wo, gain_a, gain_s, lng, lnb):
    n_steps = BATCH * SEQ // TOK_BLOCK

    def tok(cols):
        return pl.BlockSpec((TOK_BLOCK, cols), lambda s: (s, 0))

    def whole(a, **kw):
        return pl.BlockSpec(a.shape, lambda s: (0,) * a.ndim, **kw)

    return pl.pallas_call(
        _final_kernel,
        grid=(n_steps,),
        in_specs=[pl.BlockSpec(memory_space=pl.ANY), tok(D_ATTN), tok(D_ATTN + D_SSM),
                  pl.BlockSpec((None, N_GROUPS, CHUNK_VEC, LANES), lambda s: (s // STEPS_PER_REGROUP, 0, 0, 0)),
                  whole(wglu, pipeline_mode=pl.Buffered(1)), whole(bglu),
                  whole(wo, pipeline_mode=pl.Buffered(1)), whole(gain_a), whole(gain_s), whole(lng), whole(lnb)],
        out_specs=pl.BlockSpec(memory_space=pl.ANY),
        out_shape=jax.ShapeDtypeStruct((BATCH * SEQ, D_MODEL), jnp.float32),
        scratch_shapes=[pltpu.VMEM((N_LANE_BLOCKS, LANES * SLAB_CHUNK_PITCH, LANES), jnp.float32),
                        pltpu.VMEM(wglu.shape, jnp.bfloat16), pltpu.VMEM(wo.shape, jnp.bfloat16),
                        pltpu.VMEM((X_SLOTS, TOK_BLOCK, D_MODEL), jnp.float32),
                        pltpu.SemaphoreType.DMA((X_SLOTS,)),
                        pltpu.VMEM((2, TOK_BLOCK, D_MODEL), jnp.float32),
                        pltpu.SemaphoreType.DMA((2, TOK_BLOCK // FINAL_SUB_ROWS))],
        compiler_params=pltpu.CompilerParams(dimension_semantics=("arbitrary",),
                                             vmem_limit_bytes=VMEM_LIMIT),
        name="final",
    )(x2, ya, sz, gt, wglu, bglu, wo, gain_a, gain_s, lng, lnb)


def kernel(x, w_in, attn_sink, ssm_a_re, ssm_a_im, ssm_log_dt, ssm_b_re, ssm_b_im, ssm_c_re, ssm_c_im,
           ssm_d, w_glu, b_glu, norm_attn_g, norm_ssm_g, w_out, ln_g, ln_b):
    assert x.shape == (BATCH, SEQ, D_MODEL) and w_in.shape[0] == 1
    f32 = jnp.float32
    rope = _rope_tables()

    x2 = x.reshape(BATCH * SEQ, D_MODEL)
    b_t = [jnp.swapaxes(p, -1, -2) for p in (ssm_b_re, ssm_b_im)]
    params = [p[0].astype(f32) for p in (ssm_log_dt, ssm_a_re, ssm_a_im, *b_t, ssm_c_re, ssm_c_im)]
    q, kv, sz, ut, m_intra, w_state, w_so, coef = _inproj(x2, w_in[0], rope, *params,
                                                           ssm_d[0].astype(f32).reshape(1, D_SSM))
    ya = _attention(q.reshape(BATCH, SEQ, Q_COLS), kv.reshape(BATCH, SEQ, 2 * KV_COLS),
                    attn_sink[0].astype(f32))
    gt = _ssm(ut, m_intra, w_state, w_so, coef)

    row = lambda v: v[0].astype(f32)[None, :]
    out = _final(x2, ya.reshape(BATCH * SEQ, D_ATTN), sz, gt, w_glu[0].astype(f32), row(b_glu),
                 w_out[0].astype(f32), row(norm_attn_g), row(norm_ssm_g), row(ln_g), row(ln_b))
    return out.reshape(BATCH, SEQ, D_MODEL)
```

```python
import jax
import jax.numpy as jnp
import numpy as np
from jax import lax
from jax.experimental import pallas as pl
from jax.experimental.pallas import tpu as pltpu

D_MODEL = 1024
BATCH = 4
SEQ = 4096
D_ATTN = 512
D_SSM = 512
HEAD_DIM = 64
N_Q_HEADS = 8
N_KV_HEADS = 2
WINDOW = 128
BLOCK = 128
ROPE_THETA = 10000.0
SSM_CH = 16
N_GROUPS = 32
SSM_STATE = 64
NORM_EPS = 1e-5
NEG_INF = -1e30
DEEPNORM_ALPHA = 2.0 ** 0.25
LOG2_E = 1.4426950408889634
Q_SCALE = HEAD_DIM ** -0.5 * LOG2_E

Q_COLS = N_Q_HEADS * HEAD_DIM
KV_COLS = N_KV_HEADS * HEAD_DIM
CHUNK = 16
N_CHUNKS = SEQ // CHUNK
CHUNK_VEC = CHUNK * SSM_CH
STATE_VEC = 2 * SSM_STATE
LANES = 128
ROWS = BATCH * N_CHUNKS
N_ROW_BLOCKS = ROWS // LANES
TOK_BLOCK = 1024
REGROUP_TOK = LANES * CHUNK
STEPS_PER_REGROUP = REGROUP_TOK // TOK_BLOCK
SUB_ROWS = 512
N_LANE_BLOCKS = D_SSM // LANES
SLAB_CHUNK_PITCH = 20
GROUPS_PER_LANE_BLOCK = LANES // SSM_CH
VMEM_LIMIT = 56 * 1024 * 1024
HIGHEST = lax.Precision.HIGHEST


LAG_ROWS = 512
PREP_GB = N_GROUPS // (BATCH * SEQ // TOK_BLOCK)


def _ssm_tables(ldt_ref, are_ref, aim_ref, bre_ref, bim_ref, cre_ref, cim_ref, d_ref,
                m_ref, ws_ref, wo_ref, coef_ref):
    f32 = jnp.float32
    lo16 = lax.broadcasted_iota(jnp.int32, (SSM_CH, LANES), 1) < SSM_STATE
    lo1 = lax.broadcasted_iota(jnp.int32, (1, LANES), 1) < SSM_STATE
    units = [(gi, d) for gi in range(PREP_GB) for d in range(2)]

    def group(gi):
        return pl.program_id(0) * PREP_GB + gi

    def dup(v):
        return jnp.concatenate([v, v], axis=1)

    lam, zoh = {}, {}
    for gi, d in units:
        ar = dup(are_ref[d, pl.ds(group(gi), 1), :])
        ai = dup(aim_ref[d, pl.ds(group(gi), 1), :])
        dt = jnp.exp(jnp.full((1, LANES), ldt_ref[d, group(gi)], f32))
        zr = dt * ar
        zi = dt * ai
        mag = jnp.exp(zr)
        lr = mag * jnp.cos(zi)
        li = mag * jnp.sin(zi)
        den = ar * ar + ai * ai
        lam[gi, d] = (lr, li)
        zoh[gi, d] = (((lr - 1.0) * ar + li * ai) / den, (li * ar - (lr - 1.0) * ai) / den)

    pr = {u: [jnp.ones((1, LANES), f32)] for u in units}
    pi = {u: [jnp.zeros((1, LANES), f32)] for u in units}
    for _ in range(CHUNK):
        for u in units:
            lr, li = lam[u]
            pr[u].append(pr[u][-1] * lr - pi[u][-1] * li)
            pi[u].append(pr[u][-2] * li + pi[u][-1] * lr)

    lag_tabs, kws = {}, {}
    for gi in range(PREP_GB):
        ws_parts, coef_rows = [], []
        for d in range(2):
            u = (gi, d)
            fr, fi = zoh[u]
            btr = dup(bre_ref[d, gi])
            bti = dup(bim_ref[d, gi])
            bbr = fr * btr - fi * bti
            bbi = fr * bti + fi * btr
            cr = dup(cre_ref[d, gi])
            ci = dup(cim_ref[d, gi])
            pa = [jnp.where(lo1, r, i) for r, i in zip(pr[u], pi[u])]
            pb = [jnp.where(lo1, -i, r) for r, i in zip(pr[u], pi[u])]
            pc = [jnp.where(lo1, i, r) for r, i in zip(pr[u], pi[u])]
            pd = [jnp.where(lo1, r, -i) for r, i in zip(pr[u], pi[u])]

            order = range(CHUNK + 1) if d == 0 else range(CHUNK, -1, -1)
            blocks = [cr * pa[t] + ci * pb[t] for t in order]
            blocks.append(jnp.zeros((LAG_ROWS - (CHUNK + 1) * SSM_CH, LANES), f32))
            lag_tabs[u] = jnp.concatenate(blocks, axis=0)
            lhs = jnp.where(lo16, bbr, -bbi)
            kws[u] = lax.dot_general(lhs, lag_tabs[u], (((1,), (1,)), ((), ())),
                                     precision=HIGHEST, preferred_element_type=f32)

            parts = []
            for j in range(CHUNK):
                t = CHUNK - 1 - j if d == 0 else j
                parts.append(jnp.concatenate([bbr * pa[t] + bbi * pb[t],
                                              bbr * pc[t] + bbi * pd[t]], axis=1))
            ws_parts.append(jnp.concatenate(parts, axis=0))

            a2 = jnp.where(lo1, -pi[u][CHUNK], pi[u][CHUNK])
            coef_rows += [pr[u][CHUNK], a2, -a2]

        ws_ref[gi] = jnp.concatenate(ws_parts, axis=1).astype(ws_ref.dtype)
        coef_ref[gi] = jnp.concatenate(coef_rows + [jnp.zeros((2, LANES), f32)], axis=0)

    sign = jnp.where(lax.broadcasted_iota(jnp.int32, (LANES, CHUNK_VEC), 0) < SSM_STATE, 1.0, -1.0)
    sub = lax.broadcasted_iota(jnp.int32, (SSM_CH, CHUNK_VEC), 0)
    lane = lax.broadcasted_iota(jnp.int32, (SSM_CH, CHUNK_VEC), 1)
    for gi in range(PREP_GB):
        wo_f = jnp.transpose(lag_tabs[gi, 0][SSM_CH:SSM_CH + CHUNK_VEC, :]) * sign
        wo_b = jnp.transpose(lag_tabs[gi, 1][:CHUNK_VEC, :]) * sign
        wo_ref[gi] = jnp.concatenate([wo_f, wo_b], axis=0).astype(wo_ref.dtype)

        g = group(gi)
        dblk = d_ref[:, pl.ds(pl.multiple_of((g // GROUPS_PER_LANE_BLOCK) * LANES, LANES), LANES)]
        dblk = pltpu.roll(dblk, (LANES - (g % GROUPS_PER_LANE_BLOCK) * SSM_CH) % LANES, 1)
        dblk = jnp.where(lax.broadcasted_iota(jnp.int32, (1, LANES), 1) < SSM_CH, dblk, 0.0)
        for shift in (SSM_CH, 2 * SSM_CH, 4 * SSM_CH):
            dblk = dblk + pltpu.roll(dblk, shift, 1)
        dtile = jnp.broadcast_to(jnp.concatenate([dblk, dblk], axis=1), (SSM_CH, CHUNK_VEC))
        for jp in range(CHUNK):
            fwd = kws[gi, 0] if jp == 0 else pltpu.roll(kws[gi, 0], SSM_CH * jp, 1)
            bwd = pltpu.roll(kws[gi, 1], (LAG_ROWS - SSM_CH * (CHUNK - jp)) % LAG_ROWS, 1)
            skip = jnp.where((lane // SSM_CH == jp) & (lane % SSM_CH == sub), dtile, 0.0)
            blk = fwd[:, :CHUNK_VEC] + bwd[:, :CHUNK_VEC] + skip
            m_ref[gi, jp * SSM_CH:(jp + 1) * SSM_CH, :] = blk.astype(m_ref.dtype)


def _rope_tables():
    half = HEAD_DIM // 2
    inv_freq = ROPE_THETA ** (-np.arange(half, dtype=np.float64) / half)
    ang = np.arange(SEQ, dtype=np.float64)[:, None] * inv_freq[None, :]
    cos, sin = np.cos(ang), np.sin(ang)
    return jnp.asarray(np.concatenate([cos, cos, cos, cos, -sin, sin, -sin, sin], axis=1), jnp.float32)


def _rope_block(xb, cos_t, sin_t, first_half):
    swapped = jnp.where(first_half, pltpu.roll(xb, 96, 1), pltpu.roll(xb, 32, 1))
    return xb * cos_t + swapped * sin_t


COL_Q = (0, Q_COLS)
COL_KV = (COL_Q[1], COL_Q[1] + 2 * KV_COLS)
COL_ZA = (COL_KV[1], COL_KV[1] + D_ATTN)
COL_U = (COL_ZA[1], COL_ZA[1] + D_SSM)
COL_ZS = (COL_U[1], COL_U[1] + D_SSM)


def _silu(z):
    h = 0.5 * z
    return h + h * jnp.tanh(h)


def _inproj_kernel(x_ref, w32_ref, rope_ref, ldt_ref, are_ref, aim_ref, bre_ref, bim_ref, cre_ref, cim_ref, d_ref,
                   q_ref, kv_ref, sz_ref, ut_ref, m_ref, ws_ref, wo_ref, coef_ref, uslab_ref, w_ref):
    step = pl.program_id(0)
    part = step % STEPS_PER_REGROUP
    lane = lax.broadcasted_iota(jnp.int32, (SUB_ROWS, LANES), 1)
    first_half = (lane % HEAD_DIM) < (HEAD_DIM // 2)

    @pl.when(step == 0)
    def _():
        for c0 in range(0, w_ref.shape[1], 2 * LANES):
            w_ref[:, c0:c0 + 2 * LANES] = w32_ref[:, c0:c0 + 2 * LANES].astype(w_ref.dtype)

    def proj(xb, cols):
        return jnp.dot(xb, w_ref[:, cols[0]:cols[1]], preferred_element_type=jnp.float32)

    for sub in range(TOK_BLOCK // SUB_ROWS):
        rows = slice(sub * SUB_ROWS, (sub + 1) * SUB_ROWS)
        xb = x_ref[rows, :].astype(jnp.bfloat16)
        cos_t = rope_ref[rows, :LANES]
        sin_t = rope_ref[rows, LANES:]
        q = proj(xb, COL_Q)
        kv = proj(xb, COL_KV)
        q_blocks = [_rope_block(q[:, v * LANES:(v + 1) * LANES], cos_t, sin_t, first_half)
                    for v in range(Q_COLS // LANES)]
        q_ref[rows, :] = (jnp.concatenate(q_blocks, axis=1) * Q_SCALE).astype(q_ref.dtype)
        za = proj(xb, COL_ZA)
        k_rot = _rope_block(kv[:, :LANES], cos_t, sin_t, first_half)
        kv_ref[rows, :] = jnp.concatenate([k_rot, kv[:, LANES:]], axis=1).astype(kv_ref.dtype)
        u = proj(xb, COL_U)
        sz_ref[rows, :D_ATTN] = _silu(za).astype(sz_ref.dtype)
        zs = proj(xb, COL_ZS)
        for kk in range(SUB_ROWS // CHUNK):
            base = pl.multiple_of((part * (TOK_BLOCK // CHUNK) + sub * (SUB_ROWS // CHUNK)) * SLAB_CHUNK_PITCH, 8)
            for v in range(N_LANE_BLOCKS):
                uslab_ref[v, pl.ds(base + kk * SLAB_CHUNK_PITCH, CHUNK), :] = (
                    u[kk * CHUNK:(kk + 1) * CHUNK, v * LANES:(v + 1) * LANES])
        sz_ref[rows, D_ATTN:] = _silu(zs).astype(sz_ref.dtype)

    _ssm_tables(ldt_ref, are_ref, aim_ref, bre_ref, bim_ref, cre_ref, cim_ref, d_ref, m_ref, ws_ref, wo_ref, coef_ref)

    @pl.when(part == STEPS_PER_REGROUP - 1)
    def _():
        for j in range(CHUNK):
            for v in range(N_LANE_BLOCKS):
                uj = uslab_ref[v, pl.ds(j, LANES, stride=SLAB_CHUNK_PITCH), :]
                ujt = jnp.transpose(uj.astype(ut_ref.dtype)).reshape(GROUPS_PER_LANE_BLOCK, SSM_CH, LANES)
                g0 = v * GROUPS_PER_LANE_BLOCK
                ut_ref[g0:g0 + GROUPS_PER_LANE_BLOCK, j * SSM_CH:(j + 1) * SSM_CH, :] = ujt


def _inproj(x2, w, rope, log_dt, a_re, a_im, b_re, b_im, c_re, c_im, d):
    n_steps = BATCH * SEQ // TOK_BLOCK
    blocks_per_seq = SEQ // TOK_BLOCK
    bf16 = jnp.bfloat16

    def tok(cols):
        return pl.BlockSpec((TOK_BLOCK, cols), lambda s: (s, 0))

    def whole(a):
        return pl.BlockSpec(a.shape, lambda s: (0,) * a.ndim)

    def group_in(a):
        return pl.BlockSpec((2, PREP_GB) + a.shape[2:], lambda s: (0, s, 0, 0))

    def group_out(rows, cols):
        return pl.BlockSpec((PREP_GB, rows, cols), lambda s: (s, 0, 0))

    return pl.pallas_call(
        _inproj_kernel,
        grid=(n_steps,),
        in_specs=[tok(D_MODEL), pl.BlockSpec(w.shape, lambda s: (0, 0), pipeline_mode=pl.Buffered(1)),
                  pl.BlockSpec((TOK_BLOCK, 2 * LANES), lambda s: (s % blocks_per_seq, 0)),
                  pl.BlockSpec(memory_space=pltpu.SMEM), whole(a_re), whole(a_im),
                  group_in(b_re), group_in(b_im), group_in(c_re), group_in(c_im), whole(d)],
        out_specs=[tok(Q_COLS), tok(2 * KV_COLS), tok(D_ATTN + D_SSM),
                   pl.BlockSpec((None, N_GROUPS, CHUNK_VEC, LANES), lambda s: (s // STEPS_PER_REGROUP, 0, 0, 0)),
                   group_out(CHUNK_VEC, CHUNK_VEC), group_out(CHUNK_VEC, 4 * LANES),
                   group_out(2 * STATE_VEC, CHUNK_VEC), group_out(8, LANES)],
        out_shape=[jax.ShapeDtypeStruct((BATCH * SEQ, Q_COLS), bf16),
                   jax.ShapeDtypeStruct((BATCH * SEQ, 2 * KV_COLS), bf16),
                   jax.ShapeDtypeStruct((BATCH * SEQ, D_ATTN + D_SSM), bf16),
                   jax.ShapeDtypeStruct((N_ROW_BLOCKS, N_GROUPS, CHUNK_VEC, LANES), bf16),
                   jax.ShapeDtypeStruct((N_GROUPS, CHUNK_VEC, CHUNK_VEC), bf16),
                   jax.ShapeDtypeStruct((N_GROUPS, CHUNK_VEC, 4 * LANES), bf16),
                   jax.ShapeDtypeStruct((N_GROUPS, 2 * STATE_VEC, CHUNK_VEC), bf16),
                   jax.ShapeDtypeStruct((N_GROUPS, 8, LANES), jnp.float32)],
        scratch_shapes=[pltpu.VMEM((N_LANE_BLOCKS, LANES * SLAB_CHUNK_PITCH, LANES), jnp.float32),
                        pltpu.VMEM(w.shape, bf16)],
        compiler_params=pltpu.CompilerParams(dimension_semantics=("arbitrary",),
                                             vmem_limit_bytes=VMEM_LIMIT),
        name="inproj",
    )(x2, w, rope, log_dt, a_re, a_im, b_re, b_im, c_re, c_im, d)


ATTN_SUB = 32
ATTN_STEP = ATTN_SUB * BLOCK
N_KEYS = 3 * BLOCK


def _attn_kernel(q_ref, kp_ref, kc_ref, kn_ref, sink_ref, y_ref):
    i = pl.program_id(1)
    kvw = jnp.concatenate([kp_ref[...], kc_ref[...], kn_ref[...]], axis=0).astype(jnp.float32)

    n_win = kvw.shape[0]
    lane_w = lax.broadcasted_iota(jnp.int32, (n_win, LANES), 1)
    k_lo, k_hi, vt_lo, vt_hi = [], [], [], []
    for hk in range(N_KV_HEADS):
        in_head = (lane_w >= hk * HEAD_DIM) & (lane_w < (hk + 1) * HEAD_DIM)
        k_nat = jnp.where(in_head, kvw[:, :LANES], 0.0)
        v_nat = jnp.where(in_head, kvw[:, LANES:], 0.0)
        k_oth = pltpu.roll(k_nat, HEAD_DIM, 1)
        v_oth = pltpu.roll(v_nat, HEAD_DIM, 1)
        pairs = ((k_nat, k_oth), (v_nat, v_oth)) if hk == 0 else ((k_oth, k_nat), (v_oth, v_nat))
        k_lo.append(pairs[0][0].astype(jnp.bfloat16))
        k_hi.append(pairs[0][1].astype(jnp.bfloat16))
        vt_lo.append(jnp.transpose(pairs[1][0]).astype(jnp.bfloat16))
        vt_hi.append(jnp.transpose(pairs[1][1]).astype(jnp.bfloat16))

    c = lax.broadcasted_iota(jnp.int32, (N_KEYS, BLOCK), 0)
    r = lax.broadcasted_iota(jnp.int32, (N_KEYS, BLOCK), 1)
    band = (c >= r) & (c - r <= 2 * WINDOW)
    ones_row = lax.broadcasted_iota(jnp.int32, (16, 2 * N_KEYS), 0)
    ones_col = lax.broadcasted_iota(jnp.int32, (16, 2 * N_KEYS), 1)
    ones = jnp.where(((ones_row == 0) & (ones_col < N_KEYS)) | ((ones_row == 1) & (ones_col >= N_KEYS)),
                     1.0, 0.0).astype(jnp.bfloat16)
    row_o = lax.broadcasted_iota(jnp.int32, (LANES, 2 * BLOCK), 0)
    n_blocks = SEQ // BLOCK

    def scores(t, hk):
        blk = i * ATTN_SUB + t
        lo = jnp.where(blk == 0, BLOCK, 0)
        hi = jnp.where(blk == n_blocks - 1, 2 * BLOCK, N_KEYS)
        valid1 = band & (c >= lo) & (c < hi)
        valid = jnp.concatenate([valid1, valid1], axis=1)
        win = slice(t * BLOCK, t * BLOCK + N_KEYS)
        q = q_ref[t * BLOCK:(t + 1) * BLOCK, (2 * hk) * LANES:(2 * hk + 2) * LANES]
        qq = jnp.concatenate([q[:, :LANES], q[:, LANES:]], axis=0)
        kcat = jnp.concatenate([k_lo[hk][win], k_hi[hk][win]], axis=0)
        st = lax.dot_general(kcat, qq, (((1,), (1,)), ((), ())),
                             preferred_element_type=jnp.float32)
        sps, ms, sinks = [], [], []
        for par in range(2):
            base = par * N_KEYS
            sp = jnp.concatenate(
                [jnp.where(valid[:BLOCK], st[base:base + BLOCK, :], NEG_INF),
                 st[base + BLOCK:base + 2 * BLOCK, :],
                 jnp.where(valid[2 * BLOCK:], st[base + 2 * BLOCK:base + N_KEYS, :], NEG_INF)], axis=0)
            h0 = 4 * hk + par
            h1 = h0 + 2
            sink = jnp.concatenate([jnp.full((1, BLOCK), sink_ref[h] * LOG2_E, jnp.float32)
                                    for h in (h0, h1)], axis=1)
            sps.append(sp)
            ms.append(jnp.maximum(jnp.max(sp, axis=0, keepdims=True), sink))
            sinks.append(sink)
        return sps, ms, sinks

    def weights(sps, ms, sinks):
        return [jnp.exp2((sp - m).astype(jnp.bfloat16)) for sp, m in zip(sps, ms)], ms, sinks

    def finish(t, hk, ps, ms, sinks):
        win = slice(t * BLOCK, t * BLOCK + N_KEYS)
        vt = jnp.concatenate([jnp.concatenate([vt_lo[hk][:, win], vt_hi[hk][:, win]], axis=1), ones], axis=0)
        ot = jnp.dot(vt, jnp.concatenate(ps, axis=0), preferred_element_type=jnp.float32)
        inv0 = 1.0 / (ot[LANES:LANES + 1, :] + jnp.exp2(sinks[0] - ms[0]))
        inv1 = 1.0 / (ot[LANES + 1:LANES + 2, :] + jnp.exp2(sinks[1] - ms[1]))
        o = jnp.transpose(ot[:LANES, :] * jnp.where(row_o < HEAD_DIM, inv0, inv1))
        rows = slice(t * BLOCK, (t + 1) * BLOCK)
        y_ref[rows, (2 * hk) * LANES:(2 * hk + 1) * LANES] = o[:BLOCK].astype(y_ref.dtype)
        y_ref[rows, (2 * hk + 1) * LANES:(2 * hk + 2) * LANES] = o[BLOCK:].astype(y_ref.dtype)

    work = [(t, hk) for t in range(ATTN_SUB) for hk in range(N_KV_HEADS)]
    stage_a = {0: scores(*work[0])}
    stage_b = {}
    for n in range(len(work) + 1):
        if n + 1 < len(work):
            stage_a[n + 1] = scores(*work[n + 1])
        if n < len(work):
            stage_b[n] = weights(*stage_a.pop(n))
        if n >= 1:
            finish(*work[n - 1], *stage_b.pop(n - 1))


def _attention(q, kv, sink):
    nb = SEQ // BLOCK
    return pl.pallas_call(
        _attn_kernel,
        grid=(BATCH, SEQ // ATTN_STEP),
        in_specs=[pl.BlockSpec((None, ATTN_STEP, Q_COLS), lambda b, i: (b, i, 0)),
                  pl.BlockSpec((None, BLOCK, 2 * KV_COLS),
                               lambda b, i: (b, jnp.maximum(i * ATTN_SUB - 1, 0), 0)),
                  pl.BlockSpec((None, ATTN_STEP, 2 * KV_COLS), lambda b, i: (b, i, 0)),
                  pl.BlockSpec((None, BLOCK, 2 * KV_COLS),
                               lambda b, i: (b, jnp.minimum((i + 1) * ATTN_SUB, nb - 1), 0)),
                  pl.BlockSpec(memory_space=pltpu.SMEM)],
        out_specs=pl.BlockSpec((None, ATTN_STEP, D_ATTN), lambda b, i: (b, i, 0)),
        out_shape=jax.ShapeDtypeStruct((BATCH, SEQ, D_ATTN), jnp.bfloat16),
        compiler_params=pltpu.CompilerParams(dimension_semantics=("arbitrary", "arbitrary"),
                                             vmem_limit_bytes=VMEM_LIMIT),
        name="attention",
    )(q, kv, kv, kv, sink)


SSM_GB = 4
SSM_STEPS = N_GROUPS // SSM_GB
N_CHAINS = SSM_GB * BATCH
SLOT_PITCH = 36
RECUR_SLAB_ROWS = (N_CHUNKS + 1) * SLOT_PITCH + 4


def _gelu_tanh(x):
    k = (2.0 / np.pi) ** 0.5
    half = 0.5 * x
    return half + half * jnp.tanh(x * (k + (k * 0.044715) * (x * x)))


def _ssm_kernel(ut_ref, m_ref, ws_ref, wo_ref, coef_ref, gt_ref,
                sf_ref, sb_ref, x_ref, u_even_ref, u_odd_ref):
    s = pl.program_id(0)

    def chain_rows(gi, b, second, first_slot):
        return pl.ds(first_slot * SLOT_PITCH + 2 * (gi * BATCH + b) + second, N_CHUNKS, stride=SLOT_PITCH)

    def head(gi, u_ref):
        ut = jnp.concatenate([ut_ref[lb, gi] for lb in range(N_ROW_BLOCKS)], axis=1)
        u = jnp.transpose(ut)
        u_ref[gi] = u
        st = jnp.dot(u, ws_ref[gi], preferred_element_type=jnp.float32)
        for b in range(BATCH):
            rows = slice(b * N_CHUNKS, (b + 1) * N_CHUNKS)
            sf_ref[chain_rows(gi, b, 0, 0), :] = st[rows, 0 * LANES:1 * LANES]
            sf_ref[chain_rows(gi, b, 1, 0), :] = st[rows, 1 * LANES:2 * LANES]
            sb_ref[chain_rows(gi, b, 0, 0), :] = st[rows, 2 * LANES:3 * LANES]
            sb_ref[chain_rows(gi, b, 1, 0), :] = st[rows, 3 * LANES:4 * LANES]

    def chunk_out(gi, u_ref):
        xin_f = jnp.concatenate([x_ref[chain_rows(gi, b, 0, 0), :] for b in range(BATCH)], axis=0)
        xin_b = jnp.concatenate([x_ref[chain_rows(gi, b, 1, 1), :] for b in range(BATCH)], axis=0)
        lhs = jnp.concatenate([u_ref[gi], xin_f.astype(jnp.bfloat16), xin_b.astype(jnp.bfloat16)], axis=1)
        rhs = jnp.concatenate([m_ref[gi], wo_ref[gi]], axis=0)
        return jnp.dot(lhs, rhs, preferred_element_type=jnp.float32)

    def store_out(gi, y):
        gt = jnp.transpose(_gelu_tanh(y).astype(gt_ref.dtype))
        for lb in range(N_ROW_BLOCKS):
            gt_ref[lb, gi] = gt[:, lb * LANES:(lb + 1) * LANES]

    def head_and_tail(u_head_ref, u_tail_ref):
        y = chunk_out(0, u_tail_ref) if u_tail_ref is not None else None
        for gi in range(SSM_GB):
            if u_head_ref is not None:
                head(gi, u_head_ref)
            if u_tail_ref is not None:
                y_next = chunk_out(gi + 1, u_tail_ref) if gi + 1 < SSM_GB else None
                store_out(gi, y)
                y = y_next

    u_refs = (u_even_ref, u_odd_ref)
    inner = (s > 0) & (s < SSM_STEPS)
    pl.when(s == 0)(lambda: head_and_tail(u_refs[0], None))
    pl.when(inner & (s % 2 == 0))(lambda: head_and_tail(u_refs[0], u_refs[1]))
    pl.when(inner & (s % 2 == 1))(lambda: head_and_tail(u_refs[1], u_refs[0]))
    pl.when(s == SSM_STEPS)(lambda: head_and_tail(None, u_refs[(SSM_STEPS - 1) % 2]))

    @pl.when(s < SSM_STEPS)
    def _():
        n_halves = N_CHAINS // 8

        def chains_at(slot, half, second):
            return pl.ds(slot * SLOT_PITCH + 16 * half + second, 8, stride=2)

        zero = jnp.zeros((8, LANES), jnp.float32)
        coefs = []
        for h in range(n_halves):
            rows = [jnp.concatenate([jnp.broadcast_to(coef_ref[2 * h + m, r:r + 1, :], (BATCH, LANES))
                                     for m in range(2)], axis=0) for r in range(6)]
            per_dir = []
            for a1, a2, a3 in (rows[:3], rows[3:]):
                b2 = 2.0 * a1 * a2
                per_dir.append((a1, a2, a3, a1 * a1 - a2 * a2, b2, -b2))
            coefs.append(per_dir)
            x_ref[chains_at(0, h, 0), :] = zero
            x_ref[chains_at(N_CHUNKS, h, 1), :] = zero

        def two_chunks(coef, x, xs, s_ref, half, slot0, slot1, second, out0, out1):
            a1, a2, a3, b1, b2, b3 = coef
            s0, ss0 = s_ref[chains_at(slot0, half, 0), :], s_ref[chains_at(slot0, half, 1), :]
            s1, ss1 = s_ref[chains_at(slot1, half, 0), :], s_ref[chains_at(slot1, half, 1), :]
            x_ref[chains_at(out0, half, second), :] = a1 * x + a2 * xs + s0
            c = a1 * s0 + a2 * ss0 + s1
            cs = a1 * ss0 + a3 * s0 + ss1
            nx = b1 * x + b2 * xs + c
            nxs = b1 * xs + b3 * x + cs
            x_ref[chains_at(out1, half, second), :] = nx
            return nx, nxs

        def trip(t, carry):
            k = 2 * t
            kb = N_CHUNKS - 1 - k
            out = []
            for h in range(n_halves):
                xf, xfs, xb, xbs = carry[4 * h:4 * h + 4]
                out += two_chunks(coefs[h][0], xf, xfs, sf_ref, h, k, k + 1, 0, k + 1, k + 2)
                out += two_chunks(coefs[h][1], xb, xbs, sb_ref, h, kb, kb - 1, 1, kb, kb - 1)
            return tuple(out)

        lax.fori_loop(0, N_CHUNKS // 2, trip, (zero,) * (4 * n_halves), unroll=4)


def _ssm(ut, m_intra, w_state, w_out, coef):
    def head_block(s):
        return jnp.minimum(s, SSM_STEPS - 1)

    def tail_block(s):
        return jnp.maximum(s - 1, 0)

    def per_group(shape, block):
        return pl.BlockSpec((SSM_GB,) + shape, lambda s: (block(s), 0, 0))

    def lane_blocked(block):
        return pl.BlockSpec((N_ROW_BLOCKS, SSM_GB, CHUNK_VEC, LANES), lambda s: (0, block(s), 0, 0))

    slab = pltpu.VMEM((RECUR_SLAB_ROWS, LANES), jnp.float32)
    return pl.pallas_call(
        _ssm_kernel,
        grid=(SSM_STEPS + 1,),
        in_specs=[lane_blocked(head_block), per_group((CHUNK_VEC, CHUNK_VEC), tail_block),
                  per_group((CHUNK_VEC, 4 * LANES), head_block), per_group((2 * STATE_VEC, CHUNK_VEC), tail_block),
                  per_group((8, LANES), head_block)],
        out_specs=lane_blocked(tail_block),
        out_shape=jax.ShapeDtypeStruct((N_ROW_BLOCKS, N_GROUPS, CHUNK_VEC, LANES), jnp.bfloat16),
        scratch_shapes=[slab] * 3 + [pltpu.VMEM((SSM_GB, ROWS, CHUNK_VEC), jnp.bfloat16)] * 2,
        compiler_params=pltpu.CompilerParams(dimension_semantics=("arbitrary",),
                                             vmem_limit_bytes=VMEM_LIMIT),
        name="ssm",
    )(ut, m_intra, w_state, w_out, coef)


FINAL_SUB_ROWS = 256
X_SLOTS = 3


def _unit_rms(v):
    return v * lax.rsqrt(jnp.mean(v * v, axis=-1, keepdims=True) + NORM_EPS)


def _final_kernel(x_hbm_ref, ya_hbm_ref, sz_hbm_ref, gt_ref, wglu32_ref, bglu_ref, wo32_ref, ga_ref, gs_ref,
                  lng_ref, lnb_ref, o_hbm_ref, gslab_ref, wglu_ref, wo_ref, xbuf_ref, xsem_ref, obuf_ref, osem_ref,
                  yabuf_ref, yasem_ref, szbuf_ref, szsem_ref):
    step = pl.program_id(0)
    part = step % STEPS_PER_REGROUP

    rings = ((x_hbm_ref, xbuf_ref, xsem_ref), (ya_hbm_ref, yabuf_ref, yasem_ref), (sz_hbm_ref, szbuf_ref, szsem_ref))

    def in_copies(block):
        slot = block % X_SLOTS
        return [pltpu.make_async_copy(hbm.at[pl.ds(block * TOK_BLOCK, TOK_BLOCK), :], buf.at[slot], sem.at[slot])
                for hbm, buf, sem in rings]

    @pl.when(step == 0)
    def _():
        for block in (0, 1):
            for copy in in_copies(block):
                copy.start()

    @pl.when(step + 2 < pl.num_programs(0))
    def _():
        for copy in in_copies(step + 2):
            copy.start()

    for copy in in_copies(step):
        copy.wait()
    x_ref, ya_ref, sz_ref = (buf.at[step % X_SLOTS] for _, buf, _ in rings)

    n_sub = TOK_BLOCK // FINAL_SUB_ROWS
    last_step = pl.num_programs(0) - 1

    def out_copy(block, sub):
        slot = block % 2
        rows = pl.ds(sub * FINAL_SUB_ROWS, FINAL_SUB_ROWS)
        return pltpu.make_async_copy(obuf_ref.at[slot, rows, :],
                                     o_hbm_ref.at[pl.ds(block * TOK_BLOCK + sub * FINAL_SUB_ROWS, FINAL_SUB_ROWS), :],
                                     osem_ref.at[slot, sub])

    @pl.when(step >= 2)
    def _():
        for sub in range(n_sub):
            out_copy(step - 2, sub).wait()

    o_ref = obuf_ref.at[step % 2]

    @pl.when(step == 0)
    def _():
        wglu_ref[...] = wglu32_ref[...].astype(wglu_ref.dtype)
        for gain_ref, row0 in ((ga_ref, 0), (gs_ref, D_ATTN)):
            for v in range(gain_ref.shape[1] // LANES):
                gain = gain_ref[:, v * LANES:(v + 1) * LANES] / DEEPNORM_ALPHA
                col = jnp.transpose(jnp.broadcast_to(gain, (LANES, LANES)))
                rows = slice(row0 + v * LANES, row0 + (v + 1) * LANES)
                for c0 in range(0, D_MODEL, LANES):
                    wo_ref[rows, c0:c0 + LANES] = (wo32_ref[rows, c0:c0 + LANES] * col).astype(wo_ref.dtype)

    @pl.when(part == 0)
    def _():
        for j in range(CHUNK):
            for v in range(N_LANE_BLOCKS):
                g0 = v * GROUPS_PER_LANE_BLOCK
                gjt = gt_ref[g0:g0 + GROUPS_PER_LANE_BLOCK, j * SSM_CH:(j + 1) * SSM_CH, :]
                gj = jnp.transpose(gjt.reshape(LANES, LANES))
                gslab_ref[v, pl.ds(j, LANES, stride=SLAB_CHUNK_PITCH), :] = gj.astype(jnp.float32)

    def rows_of(sub):
        return slice(sub * FINAL_SUB_ROWS, (sub + 1) * FINAL_SUB_ROWS)

    def gate_in(sub):
        rows = rows_of(sub)
        ya = ya_ref[rows, :].astype(jnp.float32) * sz_ref[rows, :D_ATTN].astype(jnp.float32)
        ya = _unit_rms(ya).astype(jnp.bfloat16)
        base = pl.multiple_of((part * (TOK_BLOCK // CHUNK) + sub * (FINAL_SUB_ROWS // CHUNK)) * SLAB_CHUNK_PITCH, 8)
        g = jnp.concatenate(
            [jnp.concatenate([gslab_ref[v, pl.ds(base + kk * SLAB_CHUNK_PITCH, CHUNK), :]
                              for kk in range(FINAL_SUB_ROWS // CHUNK)], axis=0)
             for v in range(N_LANE_BLOCKS)], axis=1)
        return ya, g

    def glu(sub, ya, g):
        rows = rows_of(sub)
        gate = jnp.dot(g.astype(jnp.bfloat16), wglu_ref[...], preferred_element_type=jnp.float32) + bglu_ref[...]
        ys = g * jax.nn.sigmoid(gate)
        ys = ys * sz_ref[rows, D_ATTN:].astype(jnp.float32)
        return ya, _unit_rms(ys).astype(jnp.bfloat16)

    def project(sub, ya, ys):
        return (jnp.dot(ya, wo_ref[:D_ATTN, :], preferred_element_type=jnp.float32)
                + jnp.dot(ys, wo_ref[D_ATTN:, :], preferred_element_type=jnp.float32))

    def layer_norm(sub, out):
        rows = rows_of(sub)
        h = x_ref[rows, :] + out
        mu = jnp.mean(h, axis=-1, keepdims=True)
        hc = h - mu
        var = jnp.mean(hc * hc, axis=-1, keepdims=True)
        o_ref[rows, :] = hc * lax.rsqrt(var + NORM_EPS / DEEPNORM_ALPHA ** 2) * lng_ref[...] + lnb_ref[...]
        out_copy(step, sub).start()

    stage_a, stage_b = {}, {}
    for n in range(n_sub + 2):
        if n < n_sub:
            stage_a[n] = gate_in(n)
        if 1 <= n <= n_sub:
            stage_b[n - 1] = glu(n - 1, *stage_a.pop(n - 1))
        if n >= 2:
            layer_norm(n - 2, project(n - 2, *stage_b.pop(n - 2)))

    @pl.when(step == last_step)
    def _():
        for block in (step - 1, step):
            for sub in range(n_sub):
                out_copy(block, sub).wait()


def _final(x2, ya, sz, gt, wglu, bglu, wo, gain_a, gain_s, lng, lnb):
    n_steps = BATCH * SEQ // TOK_BLOCK

    def tok(cols):
        return pl.BlockSpec((TOK_BLOCK, cols), lambda s: (s, 0))

    def whole(a, **kw):
        return pl.BlockSpec(a.shape, lambda s: (0,) * a.ndim, **kw)

    return pl.pallas_call(
        _final_kernel,
        grid=(n_steps,),
        in_specs=[pl.BlockSpec(memory_space=pl.ANY)] * 3 + [
                  pl.BlockSpec((None, N_GROUPS, CHUNK_VEC, LANES), lambda s: (s // STEPS_PER_REGROUP, 0, 0, 0)),
                  whole(wglu, pipeline_mode=pl.Buffered(1)), whole(bglu),
                  whole(wo, pipeline_mode=pl.Buffered(1)), whole(gain_a), whole(gain_s), whole(lng), whole(lnb)],
        out_specs=pl.BlockSpec(memory_space=pl.ANY),
        out_shape=jax.ShapeDtypeStruct((BATCH * SEQ, D_MODEL), jnp.float32),
        scratch_shapes=[pltpu.VMEM((N_LANE_BLOCKS, LANES * SLAB_CHUNK_PITCH, LANES), jnp.float32),
                        pltpu.VMEM(wglu.shape, jnp.bfloat16), pltpu.VMEM(wo.shape, jnp.bfloat16),
                        pltpu.VMEM((X_SLOTS, TOK_BLOCK, D_MODEL), jnp.float32),
                        pltpu.SemaphoreType.DMA((X_SLOTS,)),
                        pltpu.VMEM((2, TOK_BLOCK, D_MODEL), jnp.float32),
                        pltpu.SemaphoreType.DMA((2, TOK_BLOCK // FINAL_SUB_ROWS)),
                        pltpu.VMEM((X_SLOTS, TOK_BLOCK, D_ATTN), ya.dtype), pltpu.SemaphoreType.DMA((X_SLOTS,)),
                        pltpu.VMEM((X_SLOTS, TOK_BLOCK, D_ATTN + D_SSM), sz.dtype),
                        pltpu.SemaphoreType.DMA((X_SLOTS,))],
        compiler_params=pltpu.CompilerParams(dimension_semantics=("arbitrary",),
                                             vmem_limit_bytes=VMEM_LIMIT),
        name="final",
    )(x2, ya, sz, gt, wglu, bglu, wo, gain_a, gain_s, lng, lnb)


def kernel(x, w_in, attn_sink, ssm_a_re, ssm_a_im, ssm_log_dt, ssm_b_re, ssm_b_im, ssm_c_re, ssm_c_im,
           ssm_d, w_glu, b_glu, norm_attn_g, norm_ssm_g, w_out, ln_g, ln_b):
    assert x.shape == (BATCH, SEQ, D_MODEL) and w_in.shape[0] == 1
    f32 = jnp.float32
    rope = _rope_tables()

    x2 = x.reshape(BATCH * SEQ, D_MODEL)
    b_t = [jnp.swapaxes(p, -1, -2) for p in (ssm_b_re, ssm_b_im)]
    params = [p[0].astype(f32) for p in (ssm_log_dt, ssm_a_re, ssm_a_im, *b_t, ssm_c_re, ssm_c_im)]
    q, kv, sz, ut, m_intra, w_state, w_so, coef = _inproj(x2, w_in[0], rope, *params,
                                                           ssm_d[0].astype(f32).reshape(1, D_SSM))
    ya = _attention(q.reshape(BATCH, SEQ, Q_COLS), kv.reshape(BATCH, SEQ, 2 * KV_COLS),
                    attn_sink[0].astype(f32))
    gt = _ssm(ut, m_intra, w_state, w_so, coef)

    row = lambda v: v[0].astype(f32)[None, :]
    out = _final(x2, ya.reshape(BATCH * SEQ, D_ATTN), sz, gt, w_glu[0].astype(f32), row(b_glu),
                 w_out[0].astype(f32), row(norm_attn_g), row(norm_ssm_g), row(ln_g), row(ln_b))
    return out.reshape(BATCH, SEQ, D_MODEL)
```

```python
import jax
import jax.numpy as jnp
import numpy as np
from jax import lax
from jax.experimental import pallas as pl
from jax.experimental.pallas import tpu as pltpu

D_MODEL = 1024
BATCH = 4
SEQ = 4096
D_ATTN = 512
D_SSM = 512
HEAD_DIM = 64
N_Q_HEADS = 8
N_KV_HEADS = 2
WINDOW = 128
BLOCK = 128
ROPE_THETA = 10000.0
SSM_CH = 16
N_GROUPS = 32
SSM_STATE = 64
NORM_EPS = 1e-5
NEG_INF = -1e30
DEEPNORM_ALPHA = 2.0 ** 0.25
LOG2_E = 1.4426950408889634
Q_SCALE = HEAD_DIM ** -0.5 * LOG2_E

Q_COLS = N_Q_HEADS * HEAD_DIM
KV_COLS = N_KV_HEADS * HEAD_DIM
CHUNK = 16
N_CHUNKS = SEQ // CHUNK
CHUNK_VEC = CHUNK * SSM_CH
STATE_VEC = 2 * SSM_STATE
LANES = 128
ROWS = BATCH * N_CHUNKS
N_ROW_BLOCKS = ROWS // LANES
TOK_BLOCK = 1024
REGROUP_TOK = LANES * CHUNK
STEPS_PER_REGROUP = REGROUP_TOK // TOK_BLOCK
SUB_ROWS = 512
N_LANE_BLOCKS = D_SSM // LANES
SLAB_CHUNK_PITCH = 20
GROUPS_PER_LANE_BLOCK = LANES // SSM_CH
VMEM_LIMIT = 56 * 1024 * 1024
HIGHEST = lax.Precision.HIGHEST


LAG_ROWS = 512
PREP_GB = N_GROUPS // (BATCH * SEQ // TOK_BLOCK)


def _ssm_tables(ldt_ref, are_ref, aim_ref, bre_ref, bim_ref, cre_ref, cim_ref, d_ref,
                m_ref, ws_ref, wo_ref, coef_ref):
    f32 = jnp.float32
    lo16 = lax.broadcasted_iota(jnp.int32, (SSM_CH, LANES), 1) < SSM_STATE
    lo1 = lax.broadcasted_iota(jnp.int32, (1, LANES), 1) < SSM_STATE
    units = [(gi, d) for gi in range(PREP_GB) for d in range(2)]

    def group(gi):
        return pl.program_id(0) * PREP_GB + gi

    def dup(v):
        return jnp.concatenate([v, v], axis=1)

    lam, zoh = {}, {}
    for gi, d in units:
        ar = dup(are_ref[d, pl.ds(group(gi), 1), :])
        ai = dup(aim_ref[d, pl.ds(group(gi), 1), :])
        dt = jnp.exp(jnp.full((1, LANES), ldt_ref[d, group(gi)], f32))
        zr = dt * ar
        zi = dt * ai
        mag = jnp.exp(zr)
        lr = mag * jnp.cos(zi)
        li = mag * jnp.sin(zi)
        den = ar * ar + ai * ai
        lam[gi, d] = (lr, li)
        zoh[gi, d] = (((lr - 1.0) * ar + li * ai) / den, (li * ar - (lr - 1.0) * ai) / den)

    pr = {u: [jnp.ones((1, LANES), f32)] for u in units}
    pi = {u: [jnp.zeros((1, LANES), f32)] for u in units}
    for _ in range(CHUNK):
        for u in units:
            lr, li = lam[u]
            pr[u].append(pr[u][-1] * lr - pi[u][-1] * li)
            pi[u].append(pr[u][-2] * li + pi[u][-1] * lr)

    lag_tabs, kws = {}, {}
    for gi in range(PREP_GB):
        ws_parts, coef_rows = [], []
        for d in range(2):
            u = (gi, d)
            fr, fi = zoh[u]
            btr = dup(bre_ref[d, gi])
            bti = dup(bim_ref[d, gi])
            bbr = fr * btr - fi * bti
            bbi = fr * bti + fi * btr
            cr = dup(cre_ref[d, gi])
            ci = dup(cim_ref[d, gi])
            pa = [jnp.where(lo1, r, i) for r, i in zip(pr[u], pi[u])]
            pb = [jnp.where(lo1, -i, r) for r, i in zip(pr[u], pi[u])]
            pc = [jnp.where(lo1, i, r) for r, i in zip(pr[u], pi[u])]
            pd = [jnp.where(lo1, r, -i) for r, i in zip(pr[u], pi[u])]

            order = range(CHUNK + 1) if d == 0 else range(CHUNK, -1, -1)
            blocks = [cr * pa[t] + ci * pb[t] for t in order]
            blocks.append(jnp.zeros((LAG_ROWS - (CHUNK + 1) * SSM_CH, LANES), f32))
            lag_tabs[u] = jnp.concatenate(blocks, axis=0)
            lhs = jnp.where(lo16, bbr, -bbi)
            kws[u] = lax.dot_general(lhs, lag_tabs[u], (((1,), (1,)), ((), ())),
                                     precision=HIGHEST, preferred_element_type=f32)

            parts = []
            for j in range(CHUNK):
                t = CHUNK - 1 - j if d == 0 else j
                parts.append(jnp.concatenate([bbr * pa[t] + bbi * pb[t],
                                              bbr * pc[t] + bbi * pd[t]], axis=1))
            ws_parts.append(jnp.concatenate(parts, axis=0))

            a2 = jnp.where(lo1, -pi[u][CHUNK], pi[u][CHUNK])
            coef_rows += [pr[u][CHUNK], a2, -a2]

        ws_ref[gi] = jnp.concatenate(ws_parts, axis=1).astype(ws_ref.dtype)
        coef_ref[gi] = jnp.concatenate(coef_rows + [jnp.zeros((2, LANES), f32)], axis=0)

    sign = jnp.where(lax.broadcasted_iota(jnp.int32, (LANES, CHUNK_VEC), 0) < SSM_STATE, 1.0, -1.0)
    sub = lax.broadcasted_iota(jnp.int32, (SSM_CH, CHUNK_VEC), 0)
    lane = lax.broadcasted_iota(jnp.int32, (SSM_CH, CHUNK_VEC), 1)
    for gi in range(PREP_GB):
        wo_f = jnp.transpose(lag_tabs[gi, 0][SSM_CH:SSM_CH + CHUNK_VEC, :]) * sign
        wo_b = jnp.transpose(lag_tabs[gi, 1][:CHUNK_VEC, :]) * sign
        wo_ref[gi] = jnp.concatenate([wo_f, wo_b], axis=0).astype(wo_ref.dtype)

        g = group(gi)
        dblk = d_ref[:, pl.ds(pl.multiple_of((g // GROUPS_PER_LANE_BLOCK) * LANES, LANES), LANES)]
        dblk = pltpu.roll(dblk, (LANES - (g % GROUPS_PER_LANE_BLOCK) * SSM_CH) % LANES, 1)
        dblk = jnp.where(lax.broadcasted_iota(jnp.int32, (1, LANES), 1) < SSM_CH, dblk, 0.0)
        for shift in (SSM_CH, 2 * SSM_CH, 4 * SSM_CH):
            dblk = dblk + pltpu.roll(dblk, shift, 1)
        dtile = jnp.broadcast_to(jnp.concatenate([dblk, dblk], axis=1), (SSM_CH, CHUNK_VEC))
        for jp in range(CHUNK):
            fwd = kws[gi, 0] if jp == 0 else pltpu.roll(kws[gi, 0], SSM_CH * jp, 1)
            bwd = pltpu.roll(kws[gi, 1], (LAG_ROWS - SSM_CH * (CHUNK - jp)) % LAG_ROWS, 1)
            skip = jnp.where((lane // SSM_CH == jp) & (lane % SSM_CH == sub), dtile, 0.0)
            blk = fwd[:, :CHUNK_VEC] + bwd[:, :CHUNK_VEC] + skip
            m_ref[gi, jp * SSM_CH:(jp + 1) * SSM_CH, :] = blk.astype(m_ref.dtype)


def _rope_tables():
    half = HEAD_DIM // 2
    inv_freq = ROPE_THETA ** (-np.arange(half, dtype=np.float64) / half)
    ang = np.arange(SEQ, dtype=np.float64)[:, None] * inv_freq[None, :]
    cos, sin = np.cos(ang), np.sin(ang)
    return jnp.asarray(np.concatenate([cos, cos, cos, cos, -sin, sin, -sin, sin], axis=1), jnp.float32)


def _rope_block(xb, cos_t, sin_t, first_half):
    swapped = jnp.where(first_half, pltpu.roll(xb, 96, 1), pltpu.roll(xb, 32, 1))
    return xb * cos_t + swapped * sin_t


COL_Q = (0, Q_COLS)
COL_KV = (COL_Q[1], COL_Q[1] + 2 * KV_COLS)
COL_ZA = (COL_KV[1], COL_KV[1] + D_ATTN)
COL_U = (COL_ZA[1], COL_ZA[1] + D_SSM)
COL_ZS = (COL_U[1], COL_U[1] + D_SSM)


def _silu(z):
    h = 0.5 * z
    return h + h * jnp.tanh(h)


def _inproj_kernel(x_ref, w32_ref, rope_ref, ldt_ref, are_ref, aim_ref, bre_ref, bim_ref, cre_ref, cim_ref, d_ref,
                   q_ref, kv_ref, sz_ref, ut_ref, m_ref, ws_ref, wo_ref, coef_ref, uslab_ref, w_ref):
    step = pl.program_id(0)
    part = step % STEPS_PER_REGROUP
    lane = lax.broadcasted_iota(jnp.int32, (SUB_ROWS, LANES), 1)
    first_half = (lane % HEAD_DIM) < (HEAD_DIM // 2)

    @pl.when(step == 0)
    def _():
        for c0 in range(0, w_ref.shape[1], 2 * LANES):
            w_ref[:, c0:c0 + 2 * LANES] = w32_ref[:, c0:c0 + 2 * LANES].astype(w_ref.dtype)

    def proj(xb, cols):
        return jnp.dot(xb, w_ref[:, cols[0]:cols[1]], preferred_element_type=jnp.float32)

    for sub in range(TOK_BLOCK // SUB_ROWS):
        rows = slice(sub * SUB_ROWS, (sub + 1) * SUB_ROWS)
        xb = x_ref[rows, :].astype(jnp.bfloat16)
        cos_t = rope_ref[rows, :LANES]
        sin_t = rope_ref[rows, LANES:]
        q = proj(xb, COL_Q)
        kv = proj(xb, COL_KV)
        q_blocks = [_rope_block(q[:, v * LANES:(v + 1) * LANES], cos_t, sin_t, first_half)
                    for v in range(Q_COLS // LANES)]
        q_ref[rows, :] = (jnp.concatenate(q_blocks, axis=1) * Q_SCALE).astype(q_ref.dtype)
        za = proj(xb, COL_ZA)
        k_rot = _rope_block(kv[:, :LANES], cos_t, sin_t, first_half)
        kv_ref[rows, :] = jnp.concatenate([k_rot, kv[:, LANES:]], axis=1).astype(kv_ref.dtype)
        u = proj(xb, COL_U)
        sz_ref[rows, :D_ATTN] = _silu(za).astype(sz_ref.dtype)
        zs = proj(xb, COL_ZS)
        for kk in range(SUB_ROWS // CHUNK):
            base = pl.multiple_of((part * (TOK_BLOCK // CHUNK) + sub * (SUB_ROWS // CHUNK)) * SLAB_CHUNK_PITCH, 8)
            for v in range(N_LANE_BLOCKS):
                uslab_ref[v, pl.ds(base + kk * SLAB_CHUNK_PITCH, CHUNK), :] = (
                    u[kk * CHUNK:(kk + 1) * CHUNK, v * LANES:(v + 1) * LANES])
        sz_ref[rows, D_ATTN:] = _silu(zs).astype(sz_ref.dtype)

    _ssm_tables(ldt_ref, are_ref, aim_ref, bre_ref, bim_ref, cre_ref, cim_ref, d_ref, m_ref, ws_ref, wo_ref, coef_ref)

    @pl.when(part == STEPS_PER_REGROUP - 1)
    def _():
        for j in range(CHUNK):
            for v in range(N_LANE_BLOCKS):
                uj = uslab_ref[v, pl.ds(j, LANES, stride=SLAB_CHUNK_PITCH), :]
                ujt = jnp.transpose(uj.astype(ut_ref.dtype)).reshape(GROUPS_PER_LANE_BLOCK, SSM_CH, LANES)
                g0 = v * GROUPS_PER_LANE_BLOCK
                ut_ref[g0:g0 + GROUPS_PER_LANE_BLOCK, j * SSM_CH:(j + 1) * SSM_CH, :] = ujt


def _inproj(x2, w, rope, log_dt, a_re, a_im, b_re, b_im, c_re, c_im, d):
    n_steps = BATCH * SEQ // TOK_BLOCK
    blocks_per_seq = SEQ // TOK_BLOCK
    bf16 = jnp.bfloat16

    def tok(cols):
        return pl.BlockSpec((TOK_BLOCK, cols), lambda s: (s, 0))

    def whole(a):
        return pl.BlockSpec(a.shape, lambda s: (0,) * a.ndim)

    def group_in(a):
        return pl.BlockSpec((2, PREP_GB) + a.shape[2:], lambda s: (0, s, 0, 0))

    def group_out(rows, cols):
        return pl.BlockSpec((PREP_GB, rows, cols), lambda s: (s, 0, 0))

    return pl.pallas_call(
        _inproj_kernel,
        grid=(n_steps,),
        in_specs=[tok(D_MODEL), pl.BlockSpec(w.shape, lambda s: (0, 0), pipeline_mode=pl.Buffered(1)),
                  pl.BlockSpec((TOK_BLOCK, 2 * LANES), lambda s: (s % blocks_per_seq, 0)),
                  pl.BlockSpec(memory_space=pltpu.SMEM), whole(a_re), whole(a_im),
                  group_in(b_re), group_in(b_im), group_in(c_re), group_in(c_im), whole(d)],
        out_specs=[tok(Q_COLS), tok(2 * KV_COLS), tok(D_ATTN + D_SSM),
                   pl.BlockSpec((None, N_GROUPS, CHUNK_VEC, LANES), lambda s: (s // STEPS_PER_REGROUP, 0, 0, 0)),
                   group_out(CHUNK_VEC, CHUNK_VEC), group_out(CHUNK_VEC, 4 * LANES),
                   group_out(2 * STATE_VEC, CHUNK_VEC), group_out(8, LANES)],
        out_shape=[jax.ShapeDtypeStruct((BATCH * SEQ, Q_COLS), bf16),
                   jax.ShapeDtypeStruct((BATCH * SEQ, 2 * KV_COLS), bf16),
                   jax.ShapeDtypeStruct((BATCH * SEQ, D_ATTN + D_SSM), bf16),
                   jax.ShapeDtypeStruct((N_ROW_BLOCKS, N_GROUPS, CHUNK_VEC, LANES), bf16),
                   jax.ShapeDtypeStruct((N_GROUPS, CHUNK_VEC, CHUNK_VEC), bf16),
                   jax.ShapeDtypeStruct((N_GROUPS, CHUNK_VEC, 4 * LANES), bf16),
                   jax.ShapeDtypeStruct((N_GROUPS, 2 * STATE_VEC, CHUNK_VEC), bf16),
                   jax.ShapeDtypeStruct((N_GROUPS, 8, LANES), jnp.float32)],
        scratch_shapes=[pltpu.VMEM((N_LANE_BLOCKS, LANES * SLAB_CHUNK_PITCH, LANES), jnp.float32),
                        pltpu.VMEM(w.shape, bf16)],
        compiler_params=pltpu.CompilerParams(dimension_semantics=("arbitrary",),
                                             vmem_limit_bytes=VMEM_LIMIT),
        name="inproj",
    )(x2, w, rope, log_dt, a_re, a_im, b_re, b_im, c_re, c_im, d)


ATTN_SUB = 32
ATTN_STEP = ATTN_SUB * BLOCK
N_KEYS = 3 * BLOCK


def _attn_kernel(q_ref, kp_ref, kc_ref, kn_ref, sink_ref, y_ref):
    i = pl.program_id(1)
    kvw = jnp.concatenate([kp_ref[...], kc_ref[...], kn_ref[...]], axis=0).astype(jnp.float32)

    n_win = kvw.shape[0]
    lane_w = lax.broadcasted_iota(jnp.int32, (n_win, LANES), 1)
    k_lo, k_hi, vt_lo, vt_hi = [], [], [], []
    for hk in range(N_KV_HEADS):
        in_head = (lane_w >= hk * HEAD_DIM) & (lane_w < (hk + 1) * HEAD_DIM)
        k_nat = jnp.where(in_head, kvw[:, :LANES], 0.0)
        v_nat = jnp.where(in_head, kvw[:, LANES:], 0.0)
        k_oth = pltpu.roll(k_nat, HEAD_DIM, 1)
        v_oth = pltpu.roll(v_nat, HEAD_DIM, 1)
        pairs = ((k_nat, k_oth), (v_nat, v_oth)) if hk == 0 else ((k_oth, k_nat), (v_oth, v_nat))
        k_lo.append(pairs[0][0].astype(jnp.bfloat16))
        k_hi.append(pairs[0][1].astype(jnp.bfloat16))
        vt_lo.append(jnp.transpose(pairs[1][0]).astype(jnp.bfloat16))
        vt_hi.append(jnp.transpose(pairs[1][1]).astype(jnp.bfloat16))

    c = lax.broadcasted_iota(jnp.int32, (N_KEYS, BLOCK), 0)
    r = lax.broadcasted_iota(jnp.int32, (N_KEYS, BLOCK), 1)
    band = (c >= r) & (c - r <= 2 * WINDOW)
    ones_row = lax.broadcasted_iota(jnp.int32, (16, 2 * N_KEYS), 0)
    ones_col = lax.broadcasted_iota(jnp.int32, (16, 2 * N_KEYS), 1)
    ones = jnp.where(((ones_row == 0) & (ones_col < N_KEYS)) | ((ones_row == 1) & (ones_col >= N_KEYS)),
                     1.0, 0.0).astype(jnp.bfloat16)
    row_o = lax.broadcasted_iota(jnp.int32, (LANES, 2 * BLOCK), 0)
    n_blocks = SEQ // BLOCK

    def scores(t, hk):
        blk = i * ATTN_SUB + t
        lo = jnp.where(blk == 0, BLOCK, 0)
        hi = jnp.where(blk == n_blocks - 1, 2 * BLOCK, N_KEYS)
        valid1 = band & (c >= lo) & (c < hi)
        valid = jnp.concatenate([valid1, valid1], axis=1)
        win = slice(t * BLOCK, t * BLOCK + N_KEYS)
        q = q_ref[t * BLOCK:(t + 1) * BLOCK, (2 * hk) * LANES:(2 * hk + 2) * LANES]
        qq = jnp.concatenate([q[:, :LANES], q[:, LANES:]], axis=0)
        kcat = jnp.concatenate([k_lo[hk][win], k_hi[hk][win]], axis=0)
        st = lax.dot_general(kcat, qq, (((1,), (1,)), ((), ())),
                             preferred_element_type=jnp.float32)
        sps, ms, sinks = [], [], []
        for par in range(2):
            base = par * N_KEYS
            sp = jnp.concatenate(
                [jnp.where(valid[:BLOCK], st[base:base + BLOCK, :], NEG_INF),
                 st[base + BLOCK:base + 2 * BLOCK, :],
                 jnp.where(valid[2 * BLOCK:], st[base + 2 * BLOCK:base + N_KEYS, :], NEG_INF)], axis=0)
            h0 = 4 * hk + par
            h1 = h0 + 2
            sink = jnp.concatenate([jnp.full((1, BLOCK), sink_ref[h] * LOG2_E, jnp.float32)
                                    for h in (h0, h1)], axis=1)
            sps.append(sp)
            ms.append(jnp.maximum(jnp.max(sp, axis=0, keepdims=True), sink))
            sinks.append(sink)
        return sps, ms, sinks

    def weights(sps, ms, sinks):
        return [jnp.exp2((sp - m).astype(jnp.bfloat16)) for sp, m in zip(sps, ms)], ms, sinks

    def finish(t, hk, ps, ms, sinks):
        win = slice(t * BLOCK, t * BLOCK + N_KEYS)
        vt = jnp.concatenate([jnp.concatenate([vt_lo[hk][:, win], vt_hi[hk][:, win]], axis=1), ones], axis=0)
        ot = jnp.dot(vt, jnp.concatenate(ps, axis=0), preferred_element_type=jnp.float32)
        inv0 = 1.0 / (ot[LANES:LANES + 1, :] + jnp.exp2(sinks[0] - ms[0]))
        inv1 = 1.0 / (ot[LANES + 1:LANES + 2, :] + jnp.exp2(sinks[1] - ms[1]))
        o = jnp.transpose(ot[:LANES, :] * jnp.where(row_o < HEAD_DIM, inv0, inv1))
        rows = slice(t * BLOCK, (t + 1) * BLOCK)
        y_ref[rows, (2 * hk) * LANES:(2 * hk + 1) * LANES] = o[:BLOCK].astype(y_ref.dtype)
        y_ref[rows, (2 * hk + 1) * LANES:(2 * hk + 2) * LANES] = o[BLOCK:].astype(y_ref.dtype)

    work = [(t, hk) for t in range(ATTN_SUB) for hk in range(N_KV_HEADS)]
    stage_a = {0: scores(*work[0])}
    stage_b = {}
    for n in range(len(work) + 1):
        if n + 1 < len(work):
            stage_a[n + 1] = scores(*work[n + 1])
        if n < len(work):
            stage_b[n] = weights(*stage_a.pop(n))
        if n >= 1:
            finish(*work[n - 1], *stage_b.pop(n - 1))


def _attention(q, kv, sink):
    nb = SEQ // BLOCK
    return pl.pallas_call(
        _attn_kernel,
        grid=(BATCH, SEQ // ATTN_STEP),
        in_specs=[pl.BlockSpec((None, ATTN_STEP, Q_COLS), lambda b, i: (b, i, 0)),
                  pl.BlockSpec((None, BLOCK, 2 * KV_COLS),
                               lambda b, i: (b, jnp.maximum(i * ATTN_SUB - 1, 0), 0)),
                  pl.BlockSpec((None, ATTN_STEP, 2 * KV_COLS), lambda b, i: (b, i, 0)),
                  pl.BlockSpec((None, BLOCK, 2 * KV_COLS),
                               lambda b, i: (b, jnp.minimum((i + 1) * ATTN_SUB, nb - 1), 0)),
                  pl.BlockSpec(memory_space=pltpu.SMEM)],
        out_specs=pl.BlockSpec((None, ATTN_STEP, D_ATTN), lambda b, i: (b, i, 0)),
        out_shape=jax.ShapeDtypeStruct((BATCH, SEQ, D_ATTN), jnp.bfloat16),
        compiler_params=pltpu.CompilerParams(dimension_semantics=("arbitrary", "arbitrary"),
                                             vmem_limit_bytes=VMEM_LIMIT),
        name="attention",
    )(q, kv, kv, kv, sink)


SSM_GB = 4
SSM_STEPS = N_GROUPS // SSM_GB
N_CHAINS = SSM_GB * BATCH
SLOT_PITCH = 36
RECUR_SLAB_ROWS = (N_CHUNKS + 1) * SLOT_PITCH + 4


def _gelu_tanh(x):
    k = (2.0 / np.pi) ** 0.5
    half = 0.5 * x
    return half + half * jnp.tanh(x * (k + (k * 0.044715) * (x * x)))


def _ssm_kernel(ut_ref, m_ref, ws_ref, wo_ref, coef_ref, gt_ref,
                sf_ref, sb_ref, x_ref, u_even_ref, u_odd_ref):
    s = pl.program_id(0)

    def chain_rows(gi, b, second, first_slot):
        return pl.ds(first_slot * SLOT_PITCH + 2 * (gi * BATCH + b) + second, N_CHUNKS, stride=SLOT_PITCH)

    def head(gi, u_ref):
        ut = jnp.concatenate([ut_ref[lb, gi] for lb in range(N_ROW_BLOCKS)], axis=1)
        u = jnp.transpose(ut)
        u_ref[gi] = u
        st = jnp.dot(u, ws_ref[gi], preferred_element_type=jnp.float32)
        for b in range(BATCH):
            rows = slice(b * N_CHUNKS, (b + 1) * N_CHUNKS)
            sf_ref[chain_rows(gi, b, 0, 0), :] = st[rows, 0 * LANES:1 * LANES]
            sf_ref[chain_rows(gi, b, 1, 0), :] = st[rows, 1 * LANES:2 * LANES]
            sb_ref[chain_rows(gi, b, 0, 0), :] = st[rows, 2 * LANES:3 * LANES]
            sb_ref[chain_rows(gi, b, 1, 0), :] = st[rows, 3 * LANES:4 * LANES]

    def chunk_out(gi, u_ref):
        xin_f = jnp.concatenate([x_ref[chain_rows(gi, b, 0, 0), :] for b in range(BATCH)], axis=0)
        xin_b = jnp.concatenate([x_ref[chain_rows(gi, b, 1, 1), :] for b in range(BATCH)], axis=0)
        lhs = jnp.concatenate([u_ref[gi], xin_f.astype(jnp.bfloat16), xin_b.astype(jnp.bfloat16)], axis=1)
        rhs = jnp.concatenate([m_ref[gi], wo_ref[gi]], axis=0)
        return jnp.dot(lhs, rhs, preferred_element_type=jnp.float32)

    def store_out(gi, y):
        gt = jnp.transpose(_gelu_tanh(y).astype(gt_ref.dtype))
        for lb in range(N_ROW_BLOCKS):
            gt_ref[lb, gi] = gt[:, lb * LANES:(lb + 1) * LANES]

    def head_and_tail(u_head_ref, u_tail_ref):
        y = chunk_out(0, u_tail_ref) if u_tail_ref is not None else None
        for gi in range(SSM_GB):
            if u_head_ref is not None:
                head(gi, u_head_ref)
            if u_tail_ref is not None:
                y_next = chunk_out(gi + 1, u_tail_ref) if gi + 1 < SSM_GB else None
                store_out(gi, y)
                y = y_next

    u_refs = (u_even_ref, u_odd_ref)
    inner = (s > 0) & (s < SSM_STEPS)
    pl.when(s == 0)(lambda: head_and_tail(u_refs[0], None))
    pl.when(inner & (s % 2 == 0))(lambda: head_and_tail(u_refs[0], u_refs[1]))
    pl.when(inner & (s % 2 == 1))(lambda: head_and_tail(u_refs[1], u_refs[0]))
    pl.when(s == SSM_STEPS)(lambda: head_and_tail(None, u_refs[(SSM_STEPS - 1) % 2]))

    @pl.when(s < SSM_STEPS)
    def _():
        n_halves = N_CHAINS // 8

        def chains_at(slot, half, second):
            return pl.ds(slot * SLOT_PITCH + 16 * half + second, 8, stride=2)

        zero = jnp.zeros((8, LANES), jnp.float32)
        coefs = []
        for h in range(n_halves):
            rows = [jnp.concatenate([jnp.broadcast_to(coef_ref[2 * h + m, r:r + 1, :], (BATCH, LANES))
                                     for m in range(2)], axis=0) for r in range(6)]
            per_dir = []
            for a1, a2, a3 in (rows[:3], rows[3:]):
                b2 = 2.0 * a1 * a2
                per_dir.append((a1, a2, a3, a1 * a1 - a2 * a2, b2, -b2))
            coefs.append(per_dir)
            x_ref[chains_at(0, h, 0), :] = zero
            x_ref[chains_at(N_CHUNKS, h, 1), :] = zero

        def two_chunks(coef, x, xs, s_ref, half, slot0, slot1, second, out0, out1):
            a1, a2, a3, b1, b2, b3 = coef
            s0, ss0 = s_ref[chains_at(slot0, half, 0), :], s_ref[chains_at(slot0, half, 1), :]
            s1, ss1 = s_ref[chains_at(slot1, half, 0), :], s_ref[chains_at(slot1, half, 1), :]
            x_ref[chains_at(out0, half, second), :] = a1 * x + a2 * xs + s0
            c = a1 * s0 + a2 * ss0 + s1
            cs = a1 * ss0 + a3 * s0 + ss1
            nx = b1 * x + b2 * xs + c
            nxs = b1 * xs + b3 * x + cs
            x_ref[chains_at(out1, half, second), :] = nx
            return nx, nxs

        def trip(t, carry):
            k = 2 * t
            kb = N_CHUNKS - 1 - k
            out = []
            for h in range(n_halves):
                xf, xfs, xb, xbs = carry[4 * h:4 * h + 4]
                out += two_chunks(coefs[h][0], xf, xfs, sf_ref, h, k, k + 1, 0, k + 1, k + 2)
                out += two_chunks(coefs[h][1], xb, xbs, sb_ref, h, kb, kb - 1, 1, kb, kb - 1)
            return tuple(out)

        lax.fori_loop(0, N_CHUNKS // 2, trip, (zero,) * (4 * n_halves), unroll=4)


def _ssm(ut, m_intra, w_state, w_out, coef):
    def head_block(s):
        return jnp.minimum(s, SSM_STEPS - 1)

    def tail_block(s):
        return jnp.maximum(s - 1, 0)

    def per_group(shape, block):
        return pl.BlockSpec((SSM_GB,) + shape, lambda s: (block(s), 0, 0))

    def lane_blocked(block):
        return pl.BlockSpec((N_ROW_BLOCKS, SSM_GB, CHUNK_VEC, LANES), lambda s: (0, block(s), 0, 0))

    slab = pltpu.VMEM((RECUR_SLAB_ROWS, LANES), jnp.float32)
    return pl.pallas_call(
        _ssm_kernel,
        grid=(SSM_STEPS + 1,),
        in_specs=[lane_blocked(head_block), per_group((CHUNK_VEC, CHUNK_VEC), tail_block),
                  per_group((CHUNK_VEC, 4 * LANES), head_block), per_group((2 * STATE_VEC, CHUNK_VEC), tail_block),
                  per_group((8, LANES), head_block)],
        out_specs=lane_blocked(tail_block),
        out_shape=jax.ShapeDtypeStruct((N_ROW_BLOCKS, N_GROUPS, CHUNK_VEC, LANES), jnp.bfloat16),
        scratch_shapes=[slab] * 3 + [pltpu.VMEM((SSM_GB, ROWS, CHUNK_VEC), jnp.bfloat16)] * 2,
        compiler_params=pltpu.CompilerParams(dimension_semantics=("arbitrary",),
                                             vmem_limit_bytes=VMEM_LIMIT),
        name="ssm",
    )(ut, m_intra, w_state, w_out, coef)


FINAL_SUB_ROWS = 256
X_SLOTS = 3


def _unit_rms(v):
    return v * lax.rsqrt(jnp.mean(v * v, axis=-1, keepdims=True) + NORM_EPS)


def _final_kernel(x_hbm_ref, ya_hbm_ref, sz_hbm_ref, gt_ref, wglu32_ref, bglu_ref, wo32_ref, ga_ref, gs_ref,
                  lng_ref, lnb_ref, o_hbm_ref, gslab_ref, wglu_ref, wo_ref, xbuf_ref, xsem_ref, obuf_ref, osem_ref,
                  yabuf_ref, yasem_ref, szbuf_ref, szsem_ref):
    step = pl.program_id(0)
    part = step % STEPS_PER_REGROUP

    rings = ((x_hbm_ref, xbuf_ref, xsem_ref), (ya_hbm_ref, yabuf_ref, yasem_ref), (sz_hbm_ref, szbuf_ref, szsem_ref))

    def in_copies(block):
        slot = block % X_SLOTS
        return [pltpu.make_async_copy(hbm.at[pl.ds(block * TOK_BLOCK, TOK_BLOCK), :], buf.at[slot], sem.at[slot])
                for hbm, buf, sem in rings]

    @pl.when(step == 0)
    def _():
        for block in (0, 1):
            for copy in in_copies(block):
                copy.start()

    @pl.when(step + 2 < pl.num_programs(0))
    def _():
        for copy in in_copies(step + 2):
            copy.start()

    for copy in in_copies(step):
        copy.wait()
    x_ref, ya_ref, sz_ref = (buf.at[step % X_SLOTS] for _, buf, _ in rings)

    n_sub = TOK_BLOCK // FINAL_SUB_ROWS
    last_step = pl.num_programs(0) - 1

    def out_copy(block, sub):
        slot = block % 2
        rows = pl.ds(sub * FINAL_SUB_ROWS, FINAL_SUB_ROWS)
        return pltpu.make_async_copy(obuf_ref.at[slot, rows, :],
                                     o_hbm_ref.at[pl.ds(block * TOK_BLOCK + sub * FINAL_SUB_ROWS, FINAL_SUB_ROWS), :],
                                     osem_ref.at[slot, sub])

    @pl.when(step >= 2)
    def _():
        for sub in range(n_sub):
            out_copy(step - 2, sub).wait()

    o_ref = obuf_ref.at[step % 2]

    @pl.when(step == 0)
    def _():
        wglu_ref[...] = wglu32_ref[...].astype(wglu_ref.dtype)
        for gain_ref, row0 in ((ga_ref, 0), (gs_ref, D_ATTN)):
            for v in range(gain_ref.shape[1] // LANES):
                gain = gain_ref[:, v * LANES:(v + 1) * LANES] / DEEPNORM_ALPHA
                col = jnp.transpose(jnp.broadcast_to(gain, (LANES, LANES)))
                rows = slice(row0 + v * LANES, row0 + (v + 1) * LANES)
                for c0 in range(0, D_MODEL, LANES):
                    wo_ref[rows, c0:c0 + LANES] = (wo32_ref[rows, c0:c0 + LANES] * col).astype(wo_ref.dtype)

    @pl.when(part == 0)
    def _():
        for j in range(CHUNK):
            for v in range(N_LANE_BLOCKS):
                g0 = v * GROUPS_PER_LANE_BLOCK
                gjt = gt_ref[g0:g0 + GROUPS_PER_LANE_BLOCK, j * SSM_CH:(j + 1) * SSM_CH, :]
                gj = jnp.transpose(gjt.reshape(LANES, LANES))
                gslab_ref[v, pl.ds(j, LANES, stride=SLAB_CHUNK_PITCH), :] = gj.astype(jnp.float32)

    def rows_of(sub):
        return slice(sub * FINAL_SUB_ROWS, (sub + 1) * FINAL_SUB_ROWS)

    def gate_in(sub):
        rows = rows_of(sub)
        ya = ya_ref[rows, :].astype(jnp.float32) * sz_ref[rows, :D_ATTN].astype(jnp.float32)
        ya = _unit_rms(ya).astype(jnp.bfloat16)
        base = pl.multiple_of((part * (TOK_BLOCK // CHUNK) + sub * (FINAL_SUB_ROWS // CHUNK)) * SLAB_CHUNK_PITCH, 8)
        g = jnp.concatenate(
            [jnp.concatenate([gslab_ref[v, pl.ds(base + kk * SLAB_CHUNK_PITCH, CHUNK), :]
                              for kk in range(FINAL_SUB_ROWS // CHUNK)], axis=0)
             for v in range(N_LANE_BLOCKS)], axis=1)
        return ya, g

    def glu(sub, ya, g):
        rows = rows_of(sub)
        gate = jnp.dot(g.astype(jnp.bfloat16), wglu_ref[...], preferred_element_type=jnp.float32) + bglu_ref[...]
        ys = g * jax.nn.sigmoid(gate)
        ys = ys * sz_ref[rows, D_ATTN:].astype(jnp.float32)
        return ya, _unit_rms(ys).astype(jnp.bfloat16)

    def project(sub, ya, ys):
        return (jnp.dot(ya, wo_ref[:D_ATTN, :], preferred_element_type=jnp.float32)
                + jnp.dot(ys, wo_ref[D_ATTN:, :], preferred_element_type=jnp.float32))

    def layer_norm(sub, out):
        rows = rows_of(sub)
        h = x_ref[rows, :] + out
        mu = jnp.mean(h, axis=-1, keepdims=True)
        hc = h - mu
        var = jnp.mean(hc * hc, axis=-1, keepdims=True)
        o_ref[rows, :] = hc * lax.rsqrt(var + NORM_EPS / DEEPNORM_ALPHA ** 2) * lng_ref[...] + lnb_ref[...]
        out_copy(step, sub).start(priority=sub % 2)

    stage_a, stage_b = {}, {}
    for n in range(n_sub + 2):
        if n < n_sub:
            stage_a[n] = gate_in(n)
        if 1 <= n <= n_sub:
            stage_b[n - 1] = glu(n - 1, *stage_a.pop(n - 1))
        if n >= 2:
            layer_norm(n - 2, project(n - 2, *stage_b.pop(n - 2)))

    @pl.when(step == last_step)
    def _():
        for block in (step - 1, step):
            for sub in range(n_sub):
                out_copy(block, sub).wait()


def _final(x2, ya, sz, gt, wglu, bglu, wo, gain_a, gain_s, lng, lnb):
    n_steps = BATCH * SEQ // TOK_BLOCK

    def tok(cols):
        return pl.BlockSpec((TOK_BLOCK, cols), lambda s: (s, 0))

    def whole(a, **kw):
        return pl.BlockSpec(a.shape, lambda s: (0,) * a.ndim, **kw)

    return pl.pallas_call(
        _final_kernel,
        grid=(n_steps,),
        in_specs=[pl.BlockSpec(memory_space=pl.ANY)] * 3 + [
                  pl.BlockSpec((None, N_GROUPS, CHUNK_VEC, LANES), lambda s: (s // STEPS_PER_REGROUP, 0, 0, 0)),
                  whole(wglu, pipeline_mode=pl.Buffered(1)), whole(bglu),
                  whole(wo, pipeline_mode=pl.Buffered(1)), whole(gain_a), whole(gain_s), whole(lng), whole(lnb)],
        out_specs=pl.BlockSpec(memory_space=pl.ANY),
        out_shape=jax.ShapeDtypeStruct((BATCH * SEQ, D_MODEL), jnp.float32),
        scratch_shapes=[pltpu.VMEM((N_LANE_BLOCKS, LANES * SLAB_CHUNK_PITCH, LANES), jnp.float32),
                        pltpu.VMEM(wglu.shape, jnp.bfloat16), pltpu.VMEM(wo.shape, jnp.bfloat16),
                        pltpu.VMEM((X_SLOTS, TOK_BLOCK, D_MODEL), jnp.float32),
                        pltpu.SemaphoreType.DMA((X_SLOTS,)),
                        pltpu.VMEM((2, TOK_BLOCK, D_MODEL), jnp.float32),
                        pltpu.SemaphoreType.DMA((2, TOK_BLOCK // FINAL_SUB_ROWS)),
                        pltpu.VMEM((X_SLOTS, TOK_BLOCK, D_ATTN), ya.dtype), pltpu.SemaphoreType.DMA((X_SLOTS,)),
                        pltpu.VMEM((X_SLOTS, TOK_BLOCK, D_ATTN + D_SSM), sz.dtype),
                        pltpu.SemaphoreType.DMA((X_SLOTS,))],
        compiler_params=pltpu.CompilerParams(dimension_semantics=("arbitrary",),
                                             vmem_limit_bytes=VMEM_LIMIT),
        name="final",
    )(x2, ya, sz, gt, wglu, bglu, wo, gain_a, gain_s, lng, lnb)


def kernel(x, w_in, attn_sink, ssm_a_re, ssm_a_im, ssm_log_dt, ssm_b_re, ssm_b_im, ssm_c_re, ssm_c_im,
           ssm_d, w_glu, b_glu, norm_attn_g, norm_ssm_g, w_out, ln_g, ln_b):
    assert x.shape == (BATCH, SEQ, D_MODEL) and w_in.shape[0] == 1
    f32 = jnp.float32
    rope = _rope_tables()

    x2 = x.reshape(BATCH * SEQ, D_MODEL)
    b_t = [jnp.swapaxes(p, -1, -2) for p in (ssm_b_re, ssm_b_im)]
    params = [p[0].astype(f32) for p in (ssm_log_dt, ssm_a_re, ssm_a_im, *b_t, ssm_c_re, ssm_c_im)]
    q, kv, sz, ut, m_intra, w_state, w_so, coef = _inproj(x2, w_in[0], rope, *params,
                                                           ssm_d[0].astype(f32).reshape(1, D_SSM))
    ya = _attention(q.reshape(BATCH, SEQ, Q_COLS), kv.reshape(BATCH, SEQ, 2 * KV_COLS),
                    attn_sink[0].astype(f32))
    gt = _ssm(ut, m_intra, w_state, w_so, coef)

    row = lambda v: v[0].astype(f32)[None, :]
    out = _final(x2, ya.reshape(BATCH * SEQ, D_ATTN), sz, gt, w_glu[0].astype(f32), row(b_glu),
                 w_out[0].astype(f32), row(norm_attn_g), row(norm_ssm_g), row(ln_g), row(ln_b))
    return out.reshape(BATCH, SEQ, D_MODEL)
```
